```python
import math, functools
import jax, jax.numpy as jnp
from jax import lax
import numpy as np

D_MODEL = 1024
BATCH = 16
SEQ = 256
DEPTH = 1
DEC_BATCH = 2
DEC_SEQ = 2048
PAST_LEN = 256

GRID_W = 64
H_ATT = 8
HEAD_DIM = 64
D_ATT = H_ATT * HEAD_DIM
D_HYENA = 512
D_MIX = D_ATT + D_HYENA
D_IN = 3 * D_ATT + 3 * D_HYENA
NA_ROWS = 8
NA_COLS = 16
Q_BLOCK = 128
SHORT_CONV = 3
FILTER_BANDS = 16
EMB_DIM = 1 + 2 * FILTER_BANDS
FILTER_FF = 64
DECAY_TARGET = 1e-2
FAST_DECAY_PCT = 0.3
SLOW_DECAY_PCT = 1.5
MIN_DECAY = math.log(DECAY_TARGET) / SLOW_DECAY_PCT
MAX_DECAY = math.log(DECAY_TARGET) / FAST_DECAY_PCT
N_GROUPS = 4
EXPERTS_PER_GROUP = 4
N_EXPERTS = N_GROUPS * EXPERTS_PER_GROUP
TOP_K = 2
D_EXPERT = 512
N_MOD = 6
EPS = 1e-6
NEG_INF = -1e30

kernel_name = "hybrid_na_hyena_hmoe_diffusion_step"

f32 = jnp.float32


def rms_norm(x, g):
    xf = x.astype(f32)
    y = xf * lax.rsqrt(jnp.mean(xf * xf, axis=-1, keepdims=True) + EPS)
    return (y * g.astype(f32)).astype(x.dtype)


def ada_mod(cond, w, b):
    m = jax.nn.silu(cond) @ w + b
    m = m.reshape(cond.shape[0], N_MOD, 1, D_MODEL)
    return tuple(m[:, i] for i in range(N_MOD))


def split_projection(proj):
    B, L, _ = proj.shape
    q, k, v, hy = jnp.split(proj, [D_ATT, 2 * D_ATT, 3 * D_ATT], axis=-1)
    shp = (B, L, H_ATT, HEAD_DIM)
    return q.reshape(shp), k.reshape(shp), v.reshape(shp), hy


def context_attention(q, k, v):
    B, S, H, Dh = q.shape
    nb = S // Q_BLOCK
    qb = (q * (Dh ** -0.5)).reshape(B, nb, Q_BLOCK, H, Dh).transpose(1, 0, 2, 3, 4)

    def block(qi):
        s = jnp.einsum('bqhd,bkhd->bhqk', qi, k).astype(f32)
        p = jax.nn.softmax(s, axis=-1).astype(v.dtype)
        return jnp.einsum('bhqk,bkhd->bqhd', p, v)

    o = lax.map(block, qb)
    return o.transpose(1, 0, 2, 3, 4).reshape(B, S, H * Dh)


def neighbourhood_attention(q, k, v, k_ctx, v_ctx, rpb):
    B, L, H, Dh = q.shape
    rows = L // GRID_W
    kh = min(NA_ROWS, rows)
    r = np.arange(rows)
    rs = np.clip(r - kh // 2, 0, rows - kh)
    row_idx = rs[:, None] + np.arange(kh)[None, :]
    col = np.arange(GRID_W)
    cs = np.clip(col - NA_COLS // 2, 0, GRID_W - NA_COLS)
    col_mask = (col[None, :] >= cs[:, None]) & (col[None, :] < cs[:, None] + NA_COLS)
    dr = row_idx - r[:, None] + NA_ROWS - 1
    dc = np.clip(col[None, :] - col[:, None] + NA_COLS - 1, 0, 2 * NA_COLS - 2)

    qg = (q * (Dh ** -0.5)).reshape(B, rows, GRID_W, H, Dh)
    kg = k.reshape(B, rows, GRID_W, H, Dh)[:, row_idx]
    vg = v.reshape(B, rows, GRID_W, H, Dh)[:, row_idx]
    s_lat = jnp.einsum('brqhd,brkwhd->bhrqkw', qg, kg).astype(f32)
    bias = rpb[:, dr[:, None, :, None], dc[None, :, None, :]].astype(f32)
    s_lat = jnp.where(col_mask[:, None, :], s_lat + bias[None], NEG_INF)
    s_ctx = jnp.einsum('brqhd,bchd->bhrqc', qg, k_ctx).astype(f32)
    nk = kh * GRID_W
    s = jnp.concatenate([s_lat.reshape(B, H, rows, GRID_W, nk), s_ctx], axis=-1)
    p = jax.nn.softmax(s, axis=-1).astype(v.dtype)
    p_lat = p[..., :nk].reshape(B, H, rows, GRID_W, kh, GRID_W)
    p_ctx = p[..., nk:]
    o = (jnp.einsum('bhrqkw,brkwhd->brqhd', p_lat, vg)
         + jnp.einsum('bhrqc,bchd->brqhd', p_ctx, v_ctx))
    return o.reshape(B, L, H * Dh)


def short_conv(u, w, b):
    L = u.shape[1]
    pad = SHORT_CONV // 2
    up = jnp.pad(u, ((0, 0), (pad, pad), (0, 0)))
    y = b
    for j in range(SHORT_CONV):
        y = y + up[:, j:j + L] * w[j]
    return y


def hyena_filters(L, w1, b1, w2, b2, w3, freq):
    t = jnp.linspace(0.0, 1.0, L, dtype=f32)[:, None]
    w = 2.0 * math.pi * jnp.arange(L, dtype=f32)[:, None] / L
    fb = jnp.linspace(1e-4, FILTER_BANDS - 1, FILTER_BANDS, dtype=f32)[None, :]
    z = jnp.concatenate([t, jnp.cos(fb * w), -jnp.sin(fb * w)], axis=-1)
    fr = freq.astype(f32)
    h = jnp.sin(fr * (z @ w1.astype(f32) + b1.astype(f32)))
    h = jnp.sin(fr * (h @ w2.astype(f32) + b2.astype(f32)))
    h = h @ w3.astype(f32)
    deltas = jnp.abs(jnp.linspace(MIN_DECAY, MAX_DECAY, D_HYENA, dtype=f32))
    decay = jnp.exp(-t * deltas[None, :])
    return h[:, :D_HYENA] * decay, h[:, D_HYENA:] * decay


def bidirectional_long_conv(u, h_fwd, h_bwd):
    L = u.shape[1]
    n = 2 * L
    two_sided = jnp.concatenate([h_fwd, jnp.zeros((1, D_HYENA), f32), h_bwd[1:][::-1]], axis=0)
    U = jnp.fft.rfft(u, n=n, axis=1)
    K = jnp.fft.rfft(two_sided, n=n, axis=0)
    return jnp.fft.irfft(U * K[None], n=n, axis=1)[:, :L]


def hyena_mixer(hy, lp):
    L = hy.shape[1]
    uc = short_conv(hy, lp['conv_w'], lp['conv_b'])
    x0, x1, v = jnp.split(uc, 3, axis=-1)
    z = (v * x1).astype(f32)
    h_fwd, h_bwd = hyena_filters(L, lp['filt_w1'], lp['filt_b1'], lp['filt_w2'],
                                 lp['filt_b2'], lp['filt_w3'], lp['filt_freq'])
    y = bidirectional_long_conv(z, h_fwd, h_bwd) + z * lp['hyena_skip'].astype(f32)
    return (y * x0.astype(f32)).astype(hy.dtype)


def hier_moe(h, lp):
    B, L, D = h.shape
    xt = h.reshape(B * L, D)
    g_prob = jax.nn.softmax((xt @ lp['router_grp_w']).astype(f32) + lp['router_grp_b'].astype(f32), axis=-1)
    g_idx = jnp.argmax(g_prob, axis=-1)
    g_w = jnp.take_along_axis(g_prob, g_idx[:, None], axis=-1)
    e_logit = ((xt @ lp['router_exp_w']).astype(f32) + lp['router_exp_b'].astype(f32))
    e_logit = e_logit.reshape(-1, N_GROUPS, EXPERTS_PER_GROUP)
    e_logit = jnp.take_along_axis(e_logit, g_idx[:, None, None], axis=1)[:, 0]
    e_prob = jax.nn.softmax(e_logit, axis=-1)
    top_w, top_i = lax.top_k(e_prob, TOP_K)
    top_w = top_w / jnp.sum(top_w, axis=-1, keepdims=True)
    expert_id = g_idx[:, None] * EXPERTS_PER_GROUP + top_i
    combine = jnp.sum(jax.nn.one_hot(expert_id, N_EXPERTS, dtype=f32) * (g_w * top_w)[..., None], axis=1)
    hid = jax.nn.silu(jnp.einsum('td,edf->tef', xt, lp['w_gate'])) * jnp.einsum('td,edf->tef', xt, lp['w_up'])
    hid = hid * combine[..., None].astype(hid.dtype)
    out = jnp.einsum('tef,efd->td', hid, lp['w_down'])
    return out.reshape(B, L, D)


def trunk_layer(x, cond, lp, attend):
    sh1, sc1, gt1, sh2, sc2, gt2 = ada_mod(cond, lp['w_ada'], lp['b_ada'])
    h = rms_norm(x, lp['norm1_g']) * (1.0 + sc1) + sh1
    q, k, v, hy = split_projection(h @ lp['w_in'])
    att = attend(q, k, v)
    hyo = hyena_mixer(hy, lp)
    merged = jnp.concatenate([rms_norm(att, lp['gnorm_att']), rms_norm(hyo, lp['gnorm_hyena'])], axis=-1)
    x = x + gt1 * (merged @ lp['w_out'])
    h2 = rms_norm(x, lp['norm2_g']) * (1.0 + sc2) + sh2
    x = x + gt2 * hier_moe(h2, lp)
    return x, k, v


def setup_inputs(seed: int = 0) -> dict:
    key = jax.random.key(seed)
    ks = jax.random.split(key, 40)
    nrm = lambda k, shape, s: jax.random.normal(k, shape, f32) * s
    return {
        'x_prompt': nrm(ks[0], (BATCH, SEQ, D_MODEL), 1.0),
        'x_sample': nrm(ks[1], (DEC_BATCH, DEC_SEQ, D_MODEL), 1.0),
        'cache_k': nrm(ks[2], (DEC_BATCH, DEPTH, PAST_LEN, H_ATT, HEAD_DIM), 1.0),
        'cache_v': nrm(ks[3], (DEC_BATCH, DEPTH, PAST_LEN, H_ATT, HEAD_DIM), 1.0),
        'c': nrm(ks[4], (DEC_BATCH, D_MODEL), 1.0),
        'c_ctx': nrm(ks[5], (D_MODEL,), 1.0),
        'w_ada': nrm(ks[6], (DEPTH, D_MODEL, N_MOD * D_MODEL), 0.5 * D_MODEL ** -0.5),
        'b_ada': nrm(ks[7], (DEPTH, N_MOD * D_MODEL), 0.05),
        'norm1_g': 1.0 + nrm(ks[8], (DEPTH, D_MODEL), 0.02),
        'w_in': nrm(ks[9], (DEPTH, D_MODEL, D_IN), D_MODEL ** -0.5),
        'rpb': nrm(ks[10], (DEPTH, H_ATT, 2 * NA_ROWS - 1, 2 * NA_COLS - 1), 0.5),
        'conv_w': nrm(ks[11], (DEPTH, SHORT_CONV, 3 * D_HYENA), SHORT_CONV ** -0.5),
        'conv_b': nrm(ks[12], (DEPTH, 3 * D_HYENA), 0.02),
        'filt_w1': nrm(ks[13], (DEPTH, EMB_DIM, FILTER_FF), EMB_DIM ** -0.5),
        'filt_b1': nrm(ks[14], (DEPTH, FILTER_FF), 0.02),
        'filt_w2': nrm(ks[15], (DEPTH, FILTER_FF, FILTER_FF), FILTER_FF ** -0.5),
        'filt_b2': nrm(ks[16], (DEPTH, FILTER_FF), 0.02),
        'filt_w3': nrm(ks[17], (DEPTH, FILTER_FF, 2 * D_HYENA), FILTER_FF ** -0.5),
        'filt_freq': 1.0 + nrm(ks[18], (DEPTH, FILTER_FF), 0.1),
        'hyena_skip': nrm(ks[19], (DEPTH, D_HYENA), 1.0),
        'gnorm_att': 1.0 + nrm(ks[20], (DEPTH, D_ATT), 0.02),
        'gnorm_hyena': 1.0 + nrm(ks[21], (DEPTH, D_HYENA), 0.02),
        'w_out': nrm(ks[22], (DEPTH, D_MIX, D_MODEL), D_MIX ** -0.5),
        'norm2_g': 1.0 + nrm(ks[23], (DEPTH, D_MODEL), 0.02),
        'router_grp_w': nrm(ks[24], (DEPTH, D_MODEL, N_GROUPS), D_MODEL ** -0.5),
        'router_grp_b': nrm(ks[25], (DEPTH, N_GROUPS), 0.01),
        'router_exp_w': nrm(ks[26], (DEPTH, D_MODEL, N_EXPERTS), D_MODEL ** -0.5),
        'router_exp_b': nrm(ks[27], (DEPTH, N_EXPERTS), 0.01),
        'w_gate': nrm(ks[28], (DEPTH, N_EXPERTS, D_MODEL, D_EXPERT), D_MODEL ** -0.5),
        'w_up': nrm(ks[29], (DEPTH, N_EXPERTS, D_MODEL, D_EXPERT), D_MODEL ** -0.5),
        'w_down': nrm(ks[30], (DEPTH, N_EXPERTS, D_EXPERT, D_MODEL), D_EXPERT ** -0.5),
        'final_g': 1.0 + nrm(ks[31], (D_MODEL,), 0.02),
    }


def reference(x_prompt, x_sample, cache_k, cache_v, c, c_ctx,
              w_ada, b_ada, norm1_g, w_in, rpb, conv_w, conv_b,
              filt_w1, filt_b1, filt_w2, filt_b2, filt_w3, filt_freq, hyena_skip,
              gnorm_att, gnorm_hyena, w_out, norm2_g,
              router_grp_w, router_grp_b, router_exp_w, router_exp_b,
              w_gate, w_up, w_down, final_g):
    xp = x_prompt
    xs = x_sample
    k_states = []
    v_states = []
    for l in range(DEPTH):
        lp = {
            'w_ada': w_ada[l], 'b_ada': b_ada[l], 'norm1_g': norm1_g[l], 'w_in': w_in[l],
            'conv_w': conv_w[l], 'conv_b': conv_b[l],
            'filt_w1': filt_w1[l], 'filt_b1': filt_b1[l], 'filt_w2': filt_w2[l],
            'filt_b2': filt_b2[l], 'filt_w3': filt_w3[l], 'filt_freq': filt_freq[l],
            'hyena_skip': hyena_skip[l], 'gnorm_att': gnorm_att[l], 'gnorm_hyena': gnorm_hyena[l],
            'w_out': w_out[l], 'norm2_g': norm2_g[l],
            'router_grp_w': router_grp_w[l], 'router_grp_b': router_grp_b[l],
            'router_exp_w': router_exp_w[l], 'router_exp_b': router_exp_b[l],
            'w_gate': w_gate[l], 'w_up': w_up[l], 'w_down': w_down[l],
        }
        xp, k_ctx_new, v_ctx_new = trunk_layer(xp, c_ctx[None, :], lp, context_attention)
        k_states.append(k_ctx_new)
        v_states.append(v_ctx_new)
        attend_lat = functools.partial(neighbourhood_attention, k_ctx=cache_k[:, l],
                                       v_ctx=cache_v[:, l], rpb=rpb[l])
        xs, _, _ = trunk_layer(xs, c, lp, attend_lat)
    y_prompt = rms_norm(xp, final_g)
    y_sample = rms_norm(xs, final_g)
    new_k = jnp.stack(k_states, axis=1)
    new_v = jnp.stack(v_states, axis=1)
    return (y_prompt, y_sample, new_k, new_v)
```

```python
import functools
import math

import jax
import jax.numpy as jnp
import numpy as np
from jax import lax
from jax.experimental import pallas as pl
from jax.experimental.pallas import tpu as pltpu

f32 = jnp.float32
bf16 = jnp.bfloat16
HIGHEST = lax.Precision.HIGHEST

D_MODEL = 1024
GRID_W = 64
H_ATT = 8
HEAD_DIM = 64
D_ATT = H_ATT * HEAD_DIM
D_HYENA = 512
D_IN = 3 * D_ATT + 3 * D_HYENA
NA_ROWS = 8
NA_COLS = 16
SHORT_CONV = 3
FILTER_BANDS = 16
EMB_DIM = 1 + 2 * FILTER_BANDS
FILTER_FF = 64
DECAY_TARGET = 1e-2
MIN_DECAY = math.log(DECAY_TARGET) / 1.5
MAX_DECAY = math.log(DECAY_TARGET) / 0.3
N_GROUPS = 4
EXPERTS_PER_GROUP = 4
N_EXPERTS = N_GROUPS * EXPERTS_PER_GROUP
D_EXPERT = 512
N_MOD = 6
EPS = 1e-6
NEG_INF = -1e30
ATT_SCALE = HEAD_DIM ** -0.5

LANES = 128
SUBLANES = 8
MOD_ROWS = 8
ROUTE_LANES = 128
ROUTE_EXP_LANE0 = 16
VMEM_LIMIT = 56 * 1024 * 1024

SH1, SC1, GT1, SH2, SC2, GT2 = range(6)


def _cparams(sem, vmem=VMEM_LIMIT):
    return pltpu.CompilerParams(dimension_semantics=sem, vmem_limit_bytes=vmem)


def _dot(a, b):
    return jnp.dot(a, b, preferred_element_type=f32)


def _dot_hi(a, b):
    return lax.dot_general(a, b, (((1,), (0,)), ((), ())), precision=HIGHEST,
                           preferred_element_type=f32)


def _dot_nt(a, b):
    return lax.dot_general(a, b, (((1,), (1,)), ((), ())), preferred_element_type=f32)


def _rms(x, g):
    ms = jnp.mean(x * x, axis=-1, keepdims=True)
    return x * lax.rsqrt(ms + EPS) * g


def _cast_rows(src_ref, dst_ref, chunk):
    n = src_ref.shape[0] // chunk

    def body(i, c):
        r = pl.multiple_of(i * chunk, chunk)
        dst_ref[pl.ds(r, chunk), :] = src_ref[pl.ds(r, chunk), :].astype(dst_ref.dtype)
        return c

    lax.fori_loop(0, n, body, 0)


def _ada_kernel(c_ref, w_ref, b_ref, o_ref):
    c = c_ref[...]
    s = c * (1.0 / (1.0 + jnp.exp(-c)))
    o_ref[...] = _dot_hi(s, w_ref[...]) + b_ref[...]


def _ada_mod(cond8, w_ada, b_ada):
    tn = 1536
    n = N_MOD * D_MODEL
    return pl.pallas_call(
        _ada_kernel,
        out_shape=jax.ShapeDtypeStruct((SUBLANES, n), f32),
        grid=(n // tn,),
        in_specs=[pl.BlockSpec((SUBLANES, D_MODEL), lambda j: (0, 0)),
                  pl.BlockSpec((D_MODEL, tn), lambda j: (0, j)),
                  pl.BlockSpec((1, tn), lambda j: (0, j))],
        out_specs=pl.BlockSpec((SUBLANES, tn), lambda j: (0, j)),
        compiler_params=_cparams(("arbitrary",)),
        name="ada_mod",
    )(cond8, w_ada, b_ada.reshape(1, n))


def _inproj_kernel(x_ref, mod_ref, g_ref, w_ref, q_ref, k_ref, v_ref, hy_ref, wbf_ref):
    @pl.when(pl.program_id(0) == 0)
    def _():
        _cast_rows(w_ref, wbf_ref, 128)

    h = _rms(x_ref[...], g_ref[...])
    h = h * (1.0 + mod_ref[0, SC1:SC1 + 1, :]) + mod_ref[0, SH1:SH1 + 1, :]
    p = _dot(h.astype(bf16), wbf_ref[...])
    q_ref[...] = p[:, 0:D_ATT]
    k_ref[...] = p[:, D_ATT:2 * D_ATT]
    v_ref[...] = p[:, 2 * D_ATT:3 * D_ATT]
    hy_ref[...] = p[:, 3 * D_ATT:]


def _inproj(x, mod, norm_g, w_in, rows_per_mod):
    t = x.shape[0]
    tm = 512
    blocks_per_mod = rows_per_mod // tm
    return pl.pallas_call(
        _inproj_kernel,
        out_shape=(jax.ShapeDtypeStruct((t, D_ATT), f32),
                   jax.ShapeDtypeStruct((t, D_ATT), f32),
                   jax.ShapeDtypeStruct((t, D_ATT), f32),
                   jax.ShapeDtypeStruct((t, 3 * D_HYENA), f32)),
        grid=(t // tm,),
        in_specs=[pl.BlockSpec((tm, D_MODEL), lambda i: (i, 0)),
                  pl.BlockSpec((1, MOD_ROWS, D_MODEL), lambda i: (i // blocks_per_mod, 0, 0)),
                  pl.BlockSpec((1, D_MODEL), lambda i: (0, 0)),
                  pl.BlockSpec((D_MODEL, D_IN), lambda i: (0, 0), pipeline_mode=pl.Buffered(1))],
        out_specs=(pl.BlockSpec((tm, D_ATT), lambda i: (i, 0)),
                   pl.BlockSpec((tm, D_ATT), lambda i: (i, 0)),
                   pl.BlockSpec((tm, D_ATT), lambda i: (i, 0)),
                   pl.BlockSpec((tm, 3 * D_HYENA), lambda i: (i, 0))),
        scratch_shapes=[pltpu.VMEM((D_MODEL, D_IN), bf16)],
        compiler_params=_cparams(("arbitrary",)),
        name="inproj",
    )(x, mod, norm_g.reshape(1, D_MODEL), w_in)


def _split_heads(q2):
    lane = lax.broadcasted_iota(jnp.int32, q2.shape, 1)
    qa = jnp.where(lane < HEAD_DIM, q2, 0.0)
    qb = jnp.where(lane >= HEAD_DIM, q2, 0.0)
    return jnp.concatenate([qa, qb], axis=0)


def _merge_heads(o_ab):
    m = o_ab.shape[0] // 2
    lane = lax.broadcasted_iota(jnp.int32, (m, LANES), 1)
    return jnp.where(lane < HEAD_DIM, o_ab[:m], o_ab[m:])


def _ctx_attn_kernel(q_ref, k_ref, v_ref, g_ref, o_ref):
    outs = []
    for p in range(D_ATT // LANES):
        cs = slice(p * LANES, (p + 1) * LANES)
        qq = _split_heads(q_ref[:, cs] * ATT_SCALE).astype(bf16)
        s = _dot_nt(qq, k_ref[:, cs].astype(bf16))
        m = jnp.max(s, axis=-1, keepdims=True)
        e = jnp.exp(s - m)
        l = jnp.sum(e, axis=-1, keepdims=True)
        o_ab = _dot(e.astype(bf16), v_ref[:, cs].astype(bf16)) / l
        outs.append(_merge_heads(o_ab))
    o_ref[...] = _rms(jnp.concatenate(outs, axis=-1), g_ref[...])


def _ctx_attention(q, k, v, gnorm, seq):
    t = q.shape[0]
    spec = pl.BlockSpec((seq, D_ATT), lambda b: (b, 0))
    return pl.pallas_call(
        _ctx_attn_kernel,
        out_shape=jax.ShapeDtypeStruct((t, D_ATT), f32),
        grid=(t // seq,),
        in_specs=[spec, spec, spec, pl.BlockSpec((1, D_ATT), lambda b: (0, 0))],
        out_specs=spec,
        compiler_params=_cparams(("arbitrary",)),
        name="ctx_attn",
    )(q, k, v, gnorm.reshape(1, D_ATT))


def _na_tables():
    col = np.arange(GRID_W)
    cs = np.clip(col - NA_COLS // 2, 0, GRID_W - NA_COLS)
    col_mask = (col[None, :] >= cs[:, None]) & (col[None, :] < cs[:, None] + NA_COLS)
    dc = np.clip(col[None, :] - col[:, None] + NA_COLS - 1, 0, 2 * NA_COLS - 2)
    n_dc = 2 * NA_COLS - 1
    onehot = np.zeros((32, GRID_W * GRID_W), np.float32)
    onehot[dc.reshape(-1), np.arange(GRID_W * GRID_W)] = 1.0
    assert n_dc <= 32
    mask = np.tile(col_mask.astype(np.float32), (1, NA_ROWS))
    return onehot, mask


def _bias_tile_kernel(r_ref, oh_ref, o_ref):
    o_ref[...] = _dot_hi(r_ref[...], oh_ref[...])


def _na_bias(rpb):
    onehot, _ = _na_tables()
    n_dr = 2 * NA_ROWS - 1
    r2 = jnp.pad(rpb.reshape(H_ATT * n_dr, 2 * NA_COLS - 1), ((0, 0), (0, 1)))
    tiles = pl.pallas_call(
        _bias_tile_kernel,
        out_shape=jax.ShapeDtypeStruct((H_ATT * n_dr, GRID_W * GRID_W), f32),
        name="na_bias_tiles",
    )(r2, jnp.asarray(onehot))
    tiles = tiles.reshape(H_ATT, n_dr, GRID_W, GRID_W)
    win = jnp.stack([tiles[:, i0:i0 + NA_ROWS] for i0 in range(NA_ROWS)], axis=0)
    return win.transpose(0, 1, 3, 2, 4).reshape(NA_ROWS, H_ATT, GRID_W, NA_ROWS * GRID_W)


def _na_row_start(r, rows):
    return jnp.clip(r - NA_ROWS // 2, 0, rows - NA_ROWS)


def _na_attn_kernel(q_ref, k_ref, v_ref, kc_ref, vc_ref, bias_ref, mask_ref, g_ref, o_ref,
                    kbf_ref, vbf_ref, *, rows):
    r = pl.program_id(1)

    @pl.when(r == 0)
    def _():
        _cast_rows(k_ref, kbf_ref, 256)
        _cast_rows(v_ref, vbf_ref, 256)

    nwin = NA_ROWS * GRID_W
    start = pl.multiple_of(_na_row_start(r, rows) * GRID_W, GRID_W)
    valid = mask_ref[...] != 0.0
    valid2 = jnp.concatenate([valid, valid], axis=0)
    outs = []
    for p in range(D_ATT // LANES):
        cs = slice(p * LANES, (p + 1) * LANES)
        qq = _split_heads(q_ref[:, cs] * ATT_SCALE).astype(bf16)
        kw = kbf_ref[pl.ds(start, nwin), cs]
        vw = vbf_ref[pl.ds(start, nwin), cs]
        s_lat = _dot_nt(qq, kw)
        s_ctx = _dot_nt(qq, kc_ref[:, cs].astype(bf16))
        bias2 = jnp.concatenate([bias_ref[0, 2 * p], bias_ref[0, 2 * p + 1]], axis=0)
        s_lat = jnp.where(valid2, s_lat + bias2, NEG_INF)
        m = jnp.maximum(jnp.max(s_lat, axis=-1, keepdims=True), jnp.max(s_ctx, axis=-1, keepdims=True))
        e_lat = jnp.exp(s_lat - m)
        e_ctx = jnp.exp(s_ctx - m)
        l = jnp.sum(e_lat, axis=-1, keepdims=True) + jnp.sum(e_ctx, axis=-1, keepdims=True)
        o_ab = (_dot(e_lat.astype(bf16), vw) + _dot(e_ctx.astype(bf16), vc_ref[:, cs].astype(bf16))) / l
        outs.append(_merge_heads(o_ab))
    o_ref[...] = _rms(jnp.concatenate(outs, axis=-1), g_ref[...])


def _na_attention(q, k, v, kc, vc, bias, gnorm, nb, seq):
    rows = seq // GRID_W
    past = kc.shape[0] // nb
    _, mask = _na_tables()

    def bias_map(b, r):
        return (_na_row_start(r, rows) - r + NA_ROWS - 1, 0, 0, 0)

    return pl.pallas_call(
        functools.partial(_na_attn_kernel, rows=rows),
        out_shape=jax.ShapeDtypeStruct((nb * seq, D_ATT), f32),
        grid=(nb, rows),
        in_specs=[pl.BlockSpec((GRID_W, D_ATT), lambda b, r: (b * rows + r, 0)),
                  pl.BlockSpec((seq, D_ATT), lambda b, r: (b, 0)),
                  pl.BlockSpec((seq, D_ATT), lambda b, r: (b, 0)),
                  pl.BlockSpec((past, D_ATT), lambda b, r: (b, 0)),
                  pl.BlockSpec((past, D_ATT), lambda b, r: (b, 0)),
                  pl.BlockSpec((1, H_ATT, GRID_W, NA_ROWS * GRID_W), bias_map),
                  pl.BlockSpec((GRID_W, NA_ROWS * GRID_W), lambda b, r: (0, 0)),
                  pl.BlockSpec((1, D_ATT), lambda b, r: (0, 0))],
        out_specs=pl.BlockSpec((GRID_W, D_ATT), lambda b, r: (b * rows + r, 0)),
        scratch_shapes=[pltpu.VMEM((seq, D_ATT), bf16), pltpu.VMEM((seq, D_ATT), bf16)],
        compiler_params=_cparams(("arbitrary", "arbitrary")),
        name="na_attn",
    )(q, k, v, kc, vc, bias, jnp.asarray(mask), gnorm.reshape(1, D_ATT))


def _hy_front_kernel(x0_ref, x1_ref, v_ref, w0_ref, w1_ref, wv_ref, b0_ref, b1_ref, bv_ref,
                     zbf_ref, z_ref, x0c_ref):
    seq = x0_ref.shape[0]
    row = lax.broadcasted_iota(jnp.int32, x0_ref.shape, 0)
    first = row == 0
    last = row == seq - 1

    def conv(u_ref, w_ref, b_ref):
        u = u_ref[...]
        up = jnp.where(first, 0.0, pltpu.roll(u, 1, 0))
        un = jnp.where(last, 0.0, pltpu.roll(u, seq - 1, 0))
        y = b_ref[...] + up * w_ref[0:1, :]
        y = y + u * w_ref[1:2, :]
        return y + un * w_ref[2:3, :]

    z = conv(v_ref, wv_ref, bv_ref) * conv(x1_ref, w1_ref, b1_ref)
    z_ref[...] = z
    zbf_ref[...] = z.astype(bf16)
    x0c_ref[...] = conv(x0_ref, w0_ref, b0_ref)


def _hy_front(hy, conv_w, conv_b, nb, seq):
    tc = 256
    nc = D_HYENA // tc
    n = nb * D_HYENA
    cb = conv_b.reshape(1, 3 * D_HYENA)

    def part(k):
        return (pl.BlockSpec((seq, tc), lambda b, j: (b, k * nc + j)),
                pl.BlockSpec((SHORT_CONV, tc), lambda b, j: (0, k * nc + j)),
                pl.BlockSpec((1, tc), lambda b, j: (0, k * nc + j)))

    (x0s, w0s, b0s), (x1s, w1s, b1s), (vs, wvs, bvs) = part(0), part(1), part(2)
    ospec = pl.BlockSpec((seq, tc), lambda b, j: (0, b * nc + j))
    return pl.pallas_call(
        _hy_front_kernel,
        out_shape=(jax.ShapeDtypeStruct((seq, n), bf16),
                   jax.ShapeDtypeStruct((seq, n), f32),
                   jax.ShapeDtypeStruct((seq, n), f32)),
        grid=(nb, nc),
        in_specs=[x0s, x1s, vs, w0s, w1s, wvs, b0s, b1s, bvs],
        out_specs=(ospec, ospec, ospec),
        compiler_params=_cparams(("arbitrary", "arbitrary")),
        name="hyena_front",
    )(hy, hy, hy, conv_w, conv_w, conv_w, cb, cb, cb)


def _filter_features(seq):
    t = np.linspace(0.0, 1.0, seq, dtype=np.float64)[:, None]
    w = 2.0 * math.pi * np.arange(seq, dtype=np.float64)[:, None] / seq
    fb = np.linspace(1e-4, FILTER_BANDS - 1, FILTER_BANDS, dtype=np.float64)[None, :]
    ang = fb * w
    z = np.concatenate([t, np.cos(ang), -np.sin(ang)], axis=-1).astype(np.float32)
    return np.pad(z, ((0, 0), (0, LANES - EMB_DIM)))


def _filt_kernel(zf_ref, w1_ref, b1_ref, fr_ref, w2_ref, b2_ref, w3_ref, dl_ref, h_ref, kl_ref):
    i = pl.program_id(0)
    tr = zf_ref.shape[0]
    zf = zf_ref[...]
    fr = fr_ref[...]
    h = jnp.sin(fr * (_dot_hi(zf, w1_ref[...]) + b1_ref[...]))
    h = jnp.sin(fr * (_dot_hi(h, w2_ref[...]) + b2_ref[...]))
    h = _dot_hi(h, w3_ref[...])
    decay = jnp.exp(-zf[:, 0:1] * dl_ref[...])
    row = lax.broadcasted_iota(jnp.int32, (tr, D_HYENA), 0) + i * tr
    hf = h[:, :D_HYENA] * decay
    hb = jnp.where(row == 0, 0.0, h[:, D_HYENA:] * decay)
    h_ref[:, :D_HYENA] = hf.astype(bf16)
    h_ref[:, D_HYENA:] = hb.astype(bf16)
    alt = (1 - 2 * (row & 1)).astype(f32)
    part = jnp.sum(alt * (hf + hb), axis=0, keepdims=True)

    @pl.when(i == 0)
    def _():
        kl_ref[...] = jnp.zeros_like(kl_ref)

    kl_ref[...] += jnp.broadcast_to(part, kl_ref.shape)


def _hy_filters(seq, w1, b1, w2, b2, w3, freq):
    tr = 256
    zf = jnp.asarray(_filter_features(seq))
    deltas = np.abs(np.linspace(MIN_DECAY, MAX_DECAY, D_HYENA, dtype=np.float64))[None, :].astype(np.float32)
    w1p = jnp.pad(w1, ((0, LANES - EMB_DIM), (0, 0)))
    const = lambda shape: pl.BlockSpec(shape, lambda i: (0, 0))
    return pl.pallas_call(
        _filt_kernel,
        out_shape=(jax.ShapeDtypeStruct((seq, 2 * D_HYENA), bf16),
                   jax.ShapeDtypeStruct((SUBLANES, D_HYENA), f32)),
        grid=(seq // tr,),
        in_specs=[pl.BlockSpec((tr, LANES), lambda i: (i, 0)),
                  const((LANES, FILTER_FF)), const((1, FILTER_FF)), const((1, FILTER_FF)),
                  const((FILTER_FF, FILTER_FF)), const((1, FILTER_FF)),
                  const((FILTER_FF, 2 * D_HYENA)), const((1, D_HYENA))],
        out_specs=(pl.BlockSpec((tr, 2 * D_HYENA), lambda i: (i, 0)),
                   pl.BlockSpec((SUBLANES, D_HYENA), lambda i: (0, 0))),
        compiler_params=_cparams(("arbitrary",)),
        name="hyena_filters",
    )(zf, w1p, b1.reshape(1, -1), freq.reshape(1, -1), w2, b2.reshape(1, -1), w3, jnp.asarray(deltas))


def _dft_mats(seq):
    n = 2 * seq
    ph = (np.arange(seq, dtype=np.int64)[:, None] * np.arange(seq, dtype=np.int64)[None, :]) % n
    ang = ph.astype(np.float64) * (2.0 * math.pi / n)
    return np.cos(ang).astype(np.float32), np.sin(ang).astype(np.float32)


def _alt_col(rows, offset):
    row = lax.broadcasted_iota(jnp.int32, (rows, 1), 0) + offset
    return (1 - 2 * (row & 1)).astype(f32)


def _hy_fwd_kernel(fr_ref, fi_ref, z_ref, h_ref, kl_ref, yr_ref, yi_ref, yl_ref, kr_s, ki_s, *, n):
    i = pl.program_id(0)
    j = pl.program_id(1)
    tf = fr_ref.shape[0]
    tn = z_ref.shape[1]
    frb = fr_ref[...].astype(bf16)
    fib = fi_ref[...].astype(bf16)

    @pl.when(j == 0)
    def _():
        ah = _dot(frb, h_ref[...])
        bh = _dot(fib, h_ref[...])
        f = lax.broadcasted_iota(jnp.int32, (tf, 1), 0) + i * tf
        cf = jnp.where(f == 0, 1.0 / n, 2.0 / n)
        kr_s[...] = (ah[:, :D_HYENA] + ah[:, D_HYENA:]) * cf
        ki_s[...] = (bh[:, D_HYENA:] - bh[:, :D_HYENA]) * cf

    a = _dot(frb, z_ref[...])
    b = _dot(fib, z_ref[...])
    kr = kr_s[...]
    ki = ki_s[...]
    for c in range(tn // D_HYENA):
        cs = slice(c * D_HYENA, (c + 1) * D_HYENA)
        yr_ref[:, cs] = (a[:, cs] * kr + b[:, cs] * ki).astype(bf16)
        yi_ref[:, cs] = (b[:, cs] * kr - a[:, cs] * ki).astype(bf16)

    @pl.when(i == 0)
    def _():
        alt = _alt_col(z_ref.shape[0], 0)
        nz = jnp.sum(z_ref[...].astype(f32) * alt, axis=0, keepdims=True)
        kl = jnp.concatenate([kl_ref[0:1, :]] * (tn // D_HYENA), axis=-1)
        yl_ref[...] = jnp.broadcast_to(nz * kl * (1.0 / n), yl_ref.shape)


def _hy_fwd(fr, fi, zbf, hcat, kl, seq):
    n_cols = zbf.shape[1]
    tf = 256
    tn = min(n_cols, 1024)
    ni, nj = seq // tf, n_cols // tn
    assert ni == 1 or nj == 1
    return pl.pallas_call(
        functools.partial(_hy_fwd_kernel, n=2 * seq),
        out_shape=(jax.ShapeDtypeStruct((seq, n_cols), bf16),
                   jax.ShapeDtypeStruct((seq, n_cols), bf16),
                   jax.ShapeDtypeStruct((SUBLANES, n_cols), f32)),
        grid=(ni, nj),
        in_specs=[pl.BlockSpec((tf, seq), lambda i, j: (i, 0)),
                  pl.BlockSpec((tf, seq), lambda i, j: (i, 0)),
                  pl.BlockSpec((seq, tn), lambda i, j: (0, j)),
                  pl.BlockSpec((seq, 2 * D_HYENA), lambda i, j: (0, 0)),
                  pl.BlockSpec((SUBLANES, D_HYENA), lambda i, j: (0, 0))],
        out_specs=(pl.BlockSpec((tf, tn), lambda i, j: (i, j)),
                   pl.BlockSpec((tf, tn), lambda i, j: (i, j)),
                   pl.BlockSpec((SUBLANES, tn), lambda i, j: (0, j))),
        scratch_shapes=[pltpu.VMEM((tf, D_HYENA), f32), pltpu.VMEM((tf, D_HYENA), f32)],
        compiler_params=_cparams(("arbitrary", "arbitrary")),
        name="hyena_dft_fwd",
    )(fr, fi, zbf, hcat, kl)


def _hy_inv_kernel(fr_ref, fi_ref, yr_ref, yi_ref, yl_ref, z_ref, x0_ref, skip_ref, g_ref, o_ref):
    tt = fr_ref.shape[0]
    tn = yr_ref.shape[1]
    y = _dot(fr_ref[...].astype(bf16), yr_ref[...]) + _dot(fi_ref[...].astype(bf16), yi_ref[...])
    alt = _alt_col(tt, pl.program_id(0) * tt)
    for c in range(tn // D_HYENA):
        cs = slice(c * D_HYENA, (c + 1) * D_HYENA)
        yc = y[:, cs] + alt * yl_ref[0:1, cs] + z_ref[:, cs] * skip_ref[...]
        o_ref[:, cs] = _rms(yc * x0_ref[:, cs], g_ref[...])


def _hy_inv(fr, fi, yr, yi, yl, z, x0c, skip, gnorm, seq):
    n_cols = z.shape[1]
    tt = 256
    tn = min(n_cols, 1024)
    blk = pl.BlockSpec((tt, tn), lambda i, j: (i, j))
    return pl.pallas_call(
        _hy_inv_kernel,
        out_shape=jax.ShapeDtypeStruct((seq, n_cols), f32),
        grid=(seq // tt, n_cols // tn),
        in_specs=[pl.BlockSpec((tt, seq), lambda i, j: (i, 0)),
                  pl.BlockSpec((tt, seq), lambda i, j: (i, 0)),
                  pl.BlockSpec((seq, tn), lambda i, j: (0, j)),
                  pl.BlockSpec((seq, tn), lambda i, j: (0, j)),
                  pl.BlockSpec((SUBLANES, tn), lambda i, j: (0, j)),
                  blk, blk,
                  pl.BlockSpec((1, D_HYENA), lambda i, j: (0, 0)),
                  pl.BlockSpec((1, D_HYENA), lambda i, j: (0, 0))],
        out_specs=blk,
        compiler_params=_cparams(("arbitrary", "arbitrary")),
        name="hyena_dft_inv",
    )(fr, fi, yr, yi, yl, z, x0c, skip.reshape(1, D_HYENA), gnorm.reshape(1, D_HYENA))


def _hyena(hy, lp, nb, seq):
    fr_np, fi_np = _dft_mats(seq)
    fr, fi = jnp.asarray(fr_np), jnp.asarray(fi_np)
    zbf, z, x0c = _hy_front(hy, lp['conv_w'], lp['conv_b'], nb, seq)
    hcat, kl = _hy_filters(seq, lp['filt_w1'], lp['filt_b1'], lp['filt_w2'], lp['filt_b2'],
                           lp['filt_w3'], lp['filt_freq'])
    yr, yi, yl = _hy_fwd(fr, fi, zbf, hcat, kl, seq)
    return _hy_inv(fr, fi, yr, yi, yl, z, x0c, lp['hyena_skip'], lp['gnorm_hyena'], seq)


def _route(logits):
    lane_i = lax.broadcasted_iota(jnp.int32, logits.shape, 1)
    lane = lane_i.astype(f32)
    big = float(ROUTE_LANES)
    is_g = lane_i < N_GROUPS
    mg = jnp.max(jnp.where(is_g, logits, -jnp.inf), axis=-1, keepdims=True)
    sg = jnp.sum(jnp.where(is_g, jnp.exp(logits - mg), 0.0), axis=-1, keepdims=True)
    g_w = 1.0 / sg
    g_idx = jnp.min(jnp.where(is_g & (logits == mg), lane, big), axis=-1, keepdims=True)
    e_id = lane_i - ROUTE_EXP_LANE0
    sel = (e_id >= 0) & (e_id < N_EXPERTS) & ((e_id >> 2).astype(f32) == g_idx)
    me = jnp.max(jnp.where(sel, logits, -jnp.inf), axis=-1, keepdims=True)
    ee = jnp.where(sel, jnp.exp(logits - me), 0.0)
    prob = ee / jnp.sum(ee, axis=-1, keepdims=True)
    p1 = jnp.max(jnp.where(sel, prob, -1.0), axis=-1, keepdims=True)
    i1 = jnp.min(jnp.where(sel & (prob == p1), lane, big), axis=-1, keepdims=True)
    sel2 = sel & (lane != i1)
    p2 = jnp.max(jnp.where(sel2, prob, -1.0), axis=-1, keepdims=True)
    i2 = jnp.min(jnp.where(sel2 & (prob == p2), lane, big), axis=-1, keepdims=True)
    tot = p1 + p2
    comb = jnp.where(lane == i1, g_w * (p1 / tot), 0.0) + jnp.where(lane == i2, g_w * (p2 / tot), 0.0)
    return jnp.where(lane_i == 0, g_idx, comb)


def _outproj_kernel(x_ref, att_ref, hy_ref, mod_ref, wo_ref, g2_ref, wr_ref, br_ref,
                    x1_ref, h2_ref, rt_ref, wobf_ref):
    @pl.when((pl.program_id(0) == 0) & (pl.program_id(1) == 0))
    def _():
        _cast_rows(wo_ref, wobf_ref, 128)

    proj = (_dot(att_ref[...].astype(bf16), wobf_ref[0:D_ATT, :])
            + _dot(hy_ref[...].astype(bf16), wobf_ref[D_ATT:, :]))
    x1 = x_ref[...] + mod_ref[0, GT1:GT1 + 1, :] * proj
    x1_ref[...] = x1
    h2 = _rms(x1, g2_ref[...]) * (1.0 + mod_ref[0, SC2:SC2 + 1, :]) + mod_ref[0, SH2:SH2 + 1, :]
    h2_ref[...] = h2.astype(bf16)
    rt_ref[...] = _route(_dot_hi(h2, wr_ref[...]) + br_ref[...])


def _outproj(x, att, hyo, mod, w_out, norm2_g, wr, br, nb, seq):
    t = x.shape[0]
    tm = 256
    spb = seq // tm
    tok = lambda b, s: (b * spb + s, 0)
    return pl.pallas_call(
        _outproj_kernel,
        out_shape=(jax.ShapeDtypeStruct((t, D_MODEL), f32),
                   jax.ShapeDtypeStruct((t, D_MODEL), bf16),
                   jax.ShapeDtypeStruct((t, ROUTE_LANES), f32)),
        grid=(nb, spb),
        in_specs=[pl.BlockSpec((tm, D_MODEL), tok),
                  pl.BlockSpec((tm, D_ATT), tok),
                  pl.BlockSpec((tm, D_HYENA), lambda b, s: (s, b)),
                  pl.BlockSpec((1, MOD_ROWS, D_MODEL), lambda b, s: (b if mod.shape[0] > 1 else 0, 0, 0)),
                  pl.BlockSpec((D_MODEL, D_MODEL), lambda b, s: (0, 0)),
                  pl.BlockSpec((1, D_MODEL), lambda b, s: (0, 0)),
                  pl.BlockSpec((D_MODEL, ROUTE_LANES), lambda b, s: (0, 0)),
                  pl.BlockSpec((1, ROUTE_LANES), lambda b, s: (0, 0))],
        out_specs=(pl.BlockSpec((tm, D_MODEL), tok),
                   pl.BlockSpec((tm, D_MODEL), tok),
                   pl.BlockSpec((tm, ROUTE_LANES), tok)),
        scratch_shapes=[pltpu.VMEM((D_MODEL, D_MODEL), bf16)],
        compiler_params=_cparams(("arbitrary", "arbitrary")),
        name="outproj_router",
    )(x, att, hyo, mod, w_out, norm2_g.reshape(1, D_MODEL), wr, br)


def _moe_kernel(h2_ref, rt_ref, x1_ref, mod_ref, wg_ref, wu_ref, wd_ref, fg_ref, o_ref,
                acc_ref, wgb_ref, wub_ref, wdb_ref, *, chunk):
    e = pl.program_id(1)
    tm = h2_ref.shape[0]
    wgb_ref[...] = wg_ref[0].astype(bf16)
    wub_ref[...] = wu_ref[0].astype(bf16)
    wdb_ref[...] = wd_ref[0].astype(bf16)

    @pl.when(e == 0)
    def _():
        acc_ref[...] = jnp.zeros_like(acc_ref)

    def body(i, c):
        r = pl.multiple_of(i * chunk, chunk)
        x = h2_ref[pl.ds(r, chunk), :]
        g = _dot(x, wgb_ref[...])
        u = _dot(x, wub_ref[...])
        rt = rt_ref[pl.ds(r, chunk), :]
        lane = lax.broadcasted_iota(jnp.int32, rt.shape, 1)
        comb = jnp.sum(jnp.where(lane == e + ROUTE_EXP_LANE0, rt, 0.0), axis=-1, keepdims=True)
        hid = (g * (1.0 / (1.0 + jnp.exp(-g)))) * u
        hid = hid * comb
        acc_ref[pl.ds(r, chunk), :] += _dot(hid.astype(bf16), wdb_ref[...])
        return c

    lax.fori_loop(0, tm // chunk, body, 0)

    @pl.when(e == N_EXPERTS - 1)
    def _():
        y = x1_ref[...] + mod_ref[0, GT2:GT2 + 1, :] * acc_ref[...]
        o_ref[...] = _rms(y, fg_ref[...])


def _moe(h2, rt, x1, mod, w_gate, w_up, w_down, final_g, rows_per_mod):
    t = h2.shape[0]
    tm = 1024
    blocks_per_mod = rows_per_mod // tm
    tok = lambda i, e: (i, 0)
    return pl.pallas_call(
        functools.partial(_moe_kernel, chunk=256),
        out_shape=jax.ShapeDtypeStruct((t, D_MODEL), f32),
        grid=(t // tm, N_EXPERTS),
        in_specs=[pl.BlockSpec((tm, D_MODEL), tok),
                  pl.BlockSpec((tm, ROUTE_LANES), tok),
                  pl.BlockSpec((tm, D_MODEL), tok),
                  pl.BlockSpec((1, MOD_ROWS, D_MODEL), lambda i, e: (i // blocks_per_mod, 0, 0)),
                  pl.BlockSpec((1, D_MODEL, D_EXPERT), lambda i, e: (e, 0, 0)),
                  pl.BlockSpec((1, D_MODEL, D_EXPERT), lambda i, e: (e, 0, 0)),
                  pl.BlockSpec((1, D_EXPERT, D_MODEL), lambda i, e: (e, 0, 0)),
                  pl.BlockSpec((1, D_MODEL), lambda i, e: (0, 0))],
        out_specs=pl.BlockSpec((tm, D_MODEL), tok),
        scratch_shapes=[pltpu.VMEM((tm, D_MODEL), f32),
                        pltpu.VMEM((D_MODEL, D_EXPERT), bf16),
                        pltpu.VMEM((D_MODEL, D_EXPERT), bf16),
                        pltpu.VMEM((D_EXPERT, D_MODEL), bf16)],
        compiler_params=_cparams(("arbitrary", "arbitrary")),
        name="moe_final",
    )(h2, rt, x1, mod, w_gate, w_up, w_down, final_g.reshape(1, D_MODEL))


def _trunk(x, mod, lp, attend, nb, seq, final_g):
    rows_per_mod = x.shape[0] if mod.shape[0] == 1 else seq
    q, k, v, hy = _inproj(x, mod, lp['norm1_g'], lp['w_in'], rows_per_mod)
    att = attend(q, k, v)
    hyo = _hyena(hy, lp, nb, seq)
    x1, h2, rt = _outproj(x, att, hyo, mod, lp['w_out'], lp['norm2_g'], lp['wr'], lp['br'], nb, seq)
    y = _moe(h2, rt, x1, mod, lp['w_gate'], lp['w_up'], lp['w_down'], final_g, rows_per_mod)
    return y, k, v


def kernel(x_prompt, x_sample, cache_k, cache_v, c, c_ctx, w_ada, b_ada, norm1_g, w_in, rpb, conv_w, conv_b, filt_w1, filt_b1, filt_w2, filt_b2, filt_w3, filt_freq, hyena_skip, gnorm_att, gnorm_hyena, w_out, norm2_g, router_grp_w, router_grp_b, router_exp_w, router_exp_b, w_gate, w_up, w_down, final_g):
    depth = w_ada.shape[0]
    assert depth == 1
    batch, seq, _ = x_prompt.shape
    dec_batch, dec_seq, _ = x_sample.shape
    l = 0

    wr = jnp.zeros((D_MODEL, ROUTE_LANES), f32)
    wr = wr.at[:, :N_GROUPS].set(router_grp_w[l])
    wr = wr.at[:, ROUTE_EXP_LANE0:ROUTE_EXP_LANE0 + N_EXPERTS].set(router_exp_w[l])
    br = jnp.zeros((1, ROUTE_LANES), f32)
    br = br.at[0, :N_GROUPS].set(router_grp_b[l])
    br = br.at[0, ROUTE_EXP_LANE0:ROUTE_EXP_LANE0 + N_EXPERTS].set(router_exp_b[l])

    lp = {
        'norm1_g': norm1_g[l], 'w_in': w_in[l], 'conv_w': conv_w[l], 'conv_b': conv_b[l],
        'filt_w1': filt_w1[l], 'filt_b1': filt_b1[l], 'filt_w2': filt_w2[l], 'filt_b2': filt_b2[l],
        'filt_w3': filt_w3[l], 'filt_freq': filt_freq[l], 'hyena_skip': hyena_skip[l],
        'gnorm_hyena': gnorm_hyena[l], 'w_out': w_out[l], 'norm2_g': norm2_g[l],
        'wr': wr, 'br': br, 'w_gate': w_gate[l], 'w_up': w_up[l], 'w_down': w_down[l],
    }

    cond8 = jnp.zeros((SUBLANES, D_MODEL), f32).at[0].set(c_ctx).at[1:1 + dec_batch].set(c)
    mod = _ada_mod(cond8, w_ada[l], b_ada[l]).reshape(SUBLANES, N_MOD, D_MODEL)
    mod = jnp.pad(mod, ((0, 0), (0, MOD_ROWS - N_MOD), (0, 0)))
    mod_ctx, mod_lat = mod[0:1], mod[1:1 + dec_batch]

    xp = x_prompt.reshape(batch * seq, D_MODEL)
    xs = x_sample.reshape(dec_batch * dec_seq, D_MODEL)

    ctx_attend = lambda q, k, v: _ctx_attention(q, k, v, gnorm_att[l], seq)
    yp, k_ctx, v_ctx = _trunk(xp, mod_ctx, lp, ctx_attend, batch, seq, final_g)

    bias = _na_bias(rpb[l])
    kc = cache_k[:, l].reshape(dec_batch * cache_k.shape[2], D_ATT)
    vc = cache_v[:, l].reshape(dec_batch * cache_v.shape[2], D_ATT)
    lat_attend = lambda q, k, v: _na_attention(q, k, v, kc, vc, bias, gnorm_att[l], dec_batch, dec_seq)
    ys, _, _ = _trunk(xs, mod_lat, lp, lat_attend, dec_batch, dec_seq, final_g)

    y_prompt = yp.reshape(batch, seq, D_MODEL)
    y_sample = ys.reshape(dec_batch, dec_seq, D_MODEL)
    new_k = k_ctx.reshape(batch, 1, seq, H_ATT, HEAD_DIM)
    new_v = v_ctx.reshape(batch, 1, seq, H_ATT, HEAD_DIM)
    return (y_prompt, y_sample, new_k, new_v)
```

```python
import functools
import math

import jax
import jax.numpy as jnp
import numpy as np
from jax import lax
from jax.experimental import pallas as pl
from jax.experimental.pallas import tpu as pltpu

f32 = jnp.float32
bf16 = jnp.bfloat16
HIGHEST = lax.Precision.HIGHEST

D_MODEL = 1024
GRID_W = 64
H_ATT = 8
HEAD_DIM = 64
D_ATT = H_ATT * HEAD_DIM
D_HYENA = 512
D_IN = 3 * D_ATT + 3 * D_HYENA
NA_ROWS = 8
NA_COLS = 16
SHORT_CONV = 3
FILTER_BANDS = 16
EMB_DIM = 1 + 2 * FILTER_BANDS
FILTER_FF = 64
DECAY_TARGET = 1e-2
MIN_DECAY = math.log(DECAY_TARGET) / 1.5
MAX_DECAY = math.log(DECAY_TARGET) / 0.3
N_GROUPS = 4
EXPERTS_PER_GROUP = 4
N_EXPERTS = N_GROUPS * EXPERTS_PER_GROUP
D_EXPERT = 512
N_MOD = 6
EPS = 1e-6
NEG_INF = -1e30
ATT_SCALE = HEAD_DIM ** -0.5

LANES = 128
SUBLANES = 8
MOD_ROWS = 8
ROUTE_LANES = 128
ROUTE_EXP_LANE0 = 16
TOK_SUB = 16
TOK_RT_ROW = 8
MOE_TM = 512
VMEM_LIMIT = 56 * 1024 * 1024

SH1, SC1, GT1, SH2, SC2, GT2 = range(6)


def _cparams(sem, vmem=VMEM_LIMIT):
    return pltpu.CompilerParams(dimension_semantics=sem, vmem_limit_bytes=vmem)


def _dot(a, b):
    return jnp.dot(a, b, preferred_element_type=f32)


def _dot_hi(a, b):
    return lax.dot_general(a, b, (((1,), (0,)), ((), ())), precision=HIGHEST,
                           preferred_element_type=f32)


def _dot_nt(a, b):
    return lax.dot_general(a, b, (((1,), (1,)), ((), ())), preferred_element_type=f32)


def _rms(x, g):
    ms = jnp.mean(x * x, axis=-1, keepdims=True)
    return x * lax.rsqrt(ms + EPS) * g


def _cast_rows(src_ref, dst_ref, chunk):
    n = src_ref.shape[0] // chunk

    def body(i, c):
        r = pl.multiple_of(i * chunk, chunk)
        dst_ref[pl.ds(r, chunk), :] = src_ref[pl.ds(r, chunk), :].astype(dst_ref.dtype)
        return c

    lax.fori_loop(0, n, body, 0)


def _ada_kernel(c_ref, w_ref, b_ref, o_ref):
    c = c_ref[...]
    s = c * (1.0 / (1.0 + jnp.exp(-c)))
    o_ref[...] = _dot_hi(s, w_ref[...]) + b_ref[...]


def _ada_mod(cond8, w_ada, b_ada):
    tn = 1536
    n = N_MOD * D_MODEL
    return pl.pallas_call(
        _ada_kernel,
        out_shape=jax.ShapeDtypeStruct((SUBLANES, n), f32),
        grid=(n // tn,),
        in_specs=[pl.BlockSpec((SUBLANES, D_MODEL), lambda j: (0, 0)),
                  pl.BlockSpec((D_MODEL, tn), lambda j: (0, j)),
                  pl.BlockSpec((1, tn), lambda j: (0, j))],
        out_specs=pl.BlockSpec((SUBLANES, tn), lambda j: (0, j)),
        compiler_params=_cparams(("arbitrary",)),
        name="ada_mod",
    )(cond8, w_ada, b_ada.reshape(1, n))


def _inproj_kernel(x_ref, mod_ref, g_ref, w_ref, q_ref, k_ref, v_ref, hy_ref, wbf_ref):
    @pl.when(pl.program_id(0) == 0)
    def _():
        _cast_rows(w_ref, wbf_ref, 128)

    h = _rms(x_ref[...], g_ref[...])
    h = h * (1.0 + mod_ref[0, SC1:SC1 + 1, :]) + mod_ref[0, SH1:SH1 + 1, :]
    p = _dot(h.astype(bf16), wbf_ref[...])
    q_ref[...] = p[:, 0:D_ATT]
    k_ref[...] = p[:, D_ATT:2 * D_ATT]
    v_ref[...] = p[:, 2 * D_ATT:3 * D_ATT]
    hy_ref[...] = p[:, 3 * D_ATT:]


def _inproj(x, mod, norm_g, w_in, rows_per_mod):
    t = x.shape[0]
    tm = 512
    blocks_per_mod = rows_per_mod // tm
    return pl.pallas_call(
        _inproj_kernel,
        out_shape=(jax.ShapeDtypeStruct((t, D_ATT), f32),
                   jax.ShapeDtypeStruct((t, D_ATT), f32),
                   jax.ShapeDtypeStruct((t, D_ATT), f32),
                   jax.ShapeDtypeStruct((t, 3 * D_HYENA), f32)),
        grid=(t // tm,),
        in_specs=[pl.BlockSpec((tm, D_MODEL), lambda i: (i, 0)),
                  pl.BlockSpec((1, MOD_ROWS, D_MODEL), lambda i: (i // blocks_per_mod, 0, 0)),
                  pl.BlockSpec((1, D_MODEL), lambda i: (0, 0)),
                  pl.BlockSpec((D_MODEL, D_IN), lambda i: (0, 0), pipeline_mode=pl.Buffered(1))],
        out_specs=(pl.BlockSpec((tm, D_ATT), lambda i: (i, 0)),
                   pl.BlockSpec((tm, D_ATT), lambda i: (i, 0)),
                   pl.BlockSpec((tm, D_ATT), lambda i: (i, 0)),
                   pl.BlockSpec((tm, 3 * D_HYENA), lambda i: (i, 0))),
        scratch_shapes=[pltpu.VMEM((D_MODEL, D_IN), bf16)],
        compiler_params=_cparams(("arbitrary",)),
        name="inproj",
    )(x, mod, norm_g.reshape(1, D_MODEL), w_in)


def _split_heads(q2):
    lane = lax.broadcasted_iota(jnp.int32, q2.shape, 1)
    qa = jnp.where(lane < HEAD_DIM, q2, 0.0)
    qb = jnp.where(lane >= HEAD_DIM, q2, 0.0)
    return jnp.concatenate([qa, qb], axis=0)


def _merge_heads(o_ab):
    m = o_ab.shape[0] // 2
    lane = lax.broadcasted_iota(jnp.int32, (m, LANES), 1)
    return jnp.where(lane < HEAD_DIM, o_ab[:m], o_ab[m:])


def _ctx_attn_kernel(q_ref, k_ref, v_ref, g_ref, o_ref):
    outs = []
    for p in range(D_ATT // LANES):
        cs = slice(p * LANES, (p + 1) * LANES)
        qq = _split_heads(q_ref[:, cs] * ATT_SCALE).astype(bf16)
        s = _dot_nt(qq, k_ref[:, cs].astype(bf16))
        m = jnp.max(s, axis=-1, keepdims=True)
        e = jnp.exp(s - m)
        l = jnp.sum(e, axis=-1, keepdims=True)
        o_ab = _dot(e.astype(bf16), v_ref[:, cs].astype(bf16)) / l
        outs.append(_merge_heads(o_ab))
    o_ref[...] = _rms(jnp.concatenate(outs, axis=-1), g_ref[...])


def _ctx_attention(q, k, v, gnorm, seq):
    t = q.shape[0]
    spec = pl.BlockSpec((seq, D_ATT), lambda b: (b, 0))
    return pl.pallas_call(
        _ctx_attn_kernel,
        out_shape=jax.ShapeDtypeStruct((t, D_ATT), f32),
        grid=(t // seq,),
        in_specs=[spec, spec, spec, pl.BlockSpec((1, D_ATT), lambda b: (0, 0))],
        out_specs=spec,
        compiler_params=_cparams(("arbitrary",)),
        name="ctx_attn",
    )(q, k, v, gnorm.reshape(1, D_ATT))


def _na_tables():
    col = np.arange(GRID_W)
    cs = np.clip(col - NA_COLS // 2, 0, GRID_W - NA_COLS)
    col_mask = (col[None, :] >= cs[:, None]) & (col[None, :] < cs[:, None] + NA_COLS)
    dc = np.clip(col[None, :] - col[:, None] + NA_COLS - 1, 0, 2 * NA_COLS - 2)
    n_dc = 2 * NA_COLS - 1
    onehot = np.zeros((32, GRID_W * GRID_W), np.float32)
    onehot[dc.reshape(-1), np.arange(GRID_W * GRID_W)] = 1.0
    assert n_dc <= 32
    mask = np.tile(col_mask.astype(np.float32), (1, NA_ROWS))
    return onehot, mask


def _bias_tile_kernel(r_ref, oh_ref, o_ref):
    o_ref[...] = _dot_hi(r_ref[...], oh_ref[...])


def _na_bias(rpb):
    onehot, _ = _na_tables()
    n_dr = 2 * NA_ROWS - 1
    r2 = jnp.pad(rpb.reshape(H_ATT * n_dr, 2 * NA_COLS - 1), ((0, 0), (0, 1)))
    tiles = pl.pallas_call(
        _bias_tile_kernel,
        out_shape=jax.ShapeDtypeStruct((H_ATT * n_dr, GRID_W * GRID_W), f32),
        name="na_bias_tiles",
    )(r2, jnp.asarray(onehot))
    tiles = tiles.reshape(H_ATT, n_dr, GRID_W, GRID_W)
    win = jnp.stack([tiles[:, i0:i0 + NA_ROWS] for i0 in range(NA_ROWS)], axis=0)
    return win.transpose(0, 1, 3, 2, 4).reshape(NA_ROWS, H_ATT, GRID_W, NA_ROWS * GRID_W)


def _na_row_start(r, rows):
    return jnp.clip(r - NA_ROWS // 2, 0, rows - NA_ROWS)


def _na_attn_kernel(q_ref, k_ref, v_ref, kc_ref, vc_ref, bias_ref, mask_ref, g_ref, o_ref,
                    kbf_ref, vbf_ref, *, rows):
    r = pl.program_id(1)

    @pl.when(r == 0)
    def _():
        _cast_rows(k_ref, kbf_ref, 256)
        _cast_rows(v_ref, vbf_ref, 256)

    nwin = NA_ROWS * GRID_W
    start = pl.multiple_of(_na_row_start(r, rows) * GRID_W, GRID_W)
    valid = mask_ref[...] != 0.0
    valid2 = jnp.concatenate([valid, valid], axis=0)
    outs = []
    for p in range(D_ATT // LANES):
        cs = slice(p * LANES, (p + 1) * LANES)
        qq = _split_heads(q_ref[:, cs] * ATT_SCALE).astype(bf16)
        kw = kbf_ref[pl.ds(start, nwin), cs]
        vw = vbf_ref[pl.ds(start, nwin), cs]
        s_lat = _dot_nt(qq, kw)
        s_ctx = _dot_nt(qq, kc_ref[:, cs].astype(bf16))
        bias2 = jnp.concatenate([bias_ref[0, 2 * p], bias_ref[0, 2 * p + 1]], axis=0)
        s_lat = jnp.where(valid2, s_lat + bias2, NEG_INF)
        m = jnp.maximum(jnp.max(s_lat, axis=-1, keepdims=True), jnp.max(s_ctx, axis=-1, keepdims=True))
        e_lat = jnp.exp(s_lat - m)
        e_ctx = jnp.exp(s_ctx - m)
        l = jnp.sum(e_lat, axis=-1, keepdims=True) + jnp.sum(e_ctx, axis=-1, keepdims=True)
        o_ab = (_dot(e_lat.astype(bf16), vw) + _dot(e_ctx.astype(bf16), vc_ref[:, cs].astype(bf16))) / l
        outs.append(_merge_heads(o_ab))
    o_ref[...] = _rms(jnp.concatenate(outs, axis=-1), g_ref[...])


def _na_attention(q, k, v, kc, vc, bias, gnorm, nb, seq):
    rows = seq // GRID_W
    past = kc.shape[0] // nb
    _, mask = _na_tables()

    def bias_map(b, r):
        return (_na_row_start(r, rows) - r + NA_ROWS - 1, 0, 0, 0)

    return pl.pallas_call(
        functools.partial(_na_attn_kernel, rows=rows),
        out_shape=jax.ShapeDtypeStruct((nb * seq, D_ATT), f32),
        grid=(nb, rows),
        in_specs=[pl.BlockSpec((GRID_W, D_ATT), lambda b, r: (b * rows + r, 0)),
                  pl.BlockSpec((seq, D_ATT), lambda b, r: (b, 0)),
                  pl.BlockSpec((seq, D_ATT), lambda b, r: (b, 0)),
                  pl.BlockSpec((past, D_ATT), lambda b, r: (b, 0)),
                  pl.BlockSpec((past, D_ATT), lambda b, r: (b, 0)),
                  pl.BlockSpec((1, H_ATT, GRID_W, NA_ROWS * GRID_W), bias_map),
                  pl.BlockSpec((GRID_W, NA_ROWS * GRID_W), lambda b, r: (0, 0)),
                  pl.BlockSpec((1, D_ATT), lambda b, r: (0, 0))],
        out_specs=pl.BlockSpec((GRID_W, D_ATT), lambda b, r: (b * rows + r, 0)),
        scratch_shapes=[pltpu.VMEM((seq, D_ATT), bf16), pltpu.VMEM((seq, D_ATT), bf16)],
        compiler_params=_cparams(("arbitrary", "arbitrary")),
        name="na_attn",
    )(q, k, v, kc, vc, bias, jnp.asarray(mask), gnorm.reshape(1, D_ATT))


def _hy_front_kernel(x0_ref, x1_ref, v_ref, w0_ref, w1_ref, wv_ref, b0_ref, b1_ref, bv_ref,
                     zbf_ref, z_ref, x0c_ref):
    seq = x0_ref.shape[0]
    row = lax.broadcasted_iota(jnp.int32, x0_ref.shape, 0)
    first = row == 0
    last = row == seq - 1

    def conv(u_ref, w_ref, b_ref):
        u = u_ref[...]
        up = jnp.where(first, 0.0, pltpu.roll(u, 1, 0))
        un = jnp.where(last, 0.0, pltpu.roll(u, seq - 1, 0))
        y = b_ref[...] + up * w_ref[0:1, :]
        y = y + u * w_ref[1:2, :]
        return y + un * w_ref[2:3, :]

    z = conv(v_ref, wv_ref, bv_ref) * conv(x1_ref, w1_ref, b1_ref)
    z_ref[...] = z
    zbf_ref[...] = z.astype(bf16)
    x0c_ref[...] = conv(x0_ref, w0_ref, b0_ref)


def _hy_front(hy, conv_w, conv_b, nb, seq):
    tc = 256
    nc = D_HYENA // tc
    n = nb * D_HYENA
    cb = conv_b.reshape(1, 3 * D_HYENA)

    def part(k):
        return (pl.BlockSpec((seq, tc), lambda b, j: (b, k * nc + j)),
                pl.BlockSpec((SHORT_CONV, tc), lambda b, j: (0, k * nc + j)),
                pl.BlockSpec((1, tc), lambda b, j: (0, k * nc + j)))

    (x0s, w0s, b0s), (x1s, w1s, b1s), (vs, wvs, bvs) = part(0), part(1), part(2)
    ospec = pl.BlockSpec((seq, tc), lambda b, j: (0, b * nc + j))
    return pl.pallas_call(
        _hy_front_kernel,
        out_shape=(jax.ShapeDtypeStruct((seq, n), bf16),
                   jax.ShapeDtypeStruct((seq, n), f32),
                   jax.ShapeDtypeStruct((seq, n), f32)),
        grid=(nb, nc),
        in_specs=[x0s, x1s, vs, w0s, w1s, wvs, b0s, b1s, bvs],
        out_specs=(ospec, ospec, ospec),
        compiler_params=_cparams(("arbitrary", "arbitrary")),
        name="hyena_front",
    )(hy, hy, hy, conv_w, conv_w, conv_w, cb, cb, cb)


def _filter_features(seq):
    t = np.linspace(0.0, 1.0, seq, dtype=np.float64)[:, None]
    w = 2.0 * math.pi * np.arange(seq, dtype=np.float64)[:, None] / seq
    fb = np.linspace(1e-4, FILTER_BANDS - 1, FILTER_BANDS, dtype=np.float64)[None, :]
    ang = fb * w
    z = np.concatenate([t, np.cos(ang), -np.sin(ang)], axis=-1).astype(np.float32)
    return np.pad(z, ((0, 0), (0, LANES - EMB_DIM)))


def _filt_kernel(zf_ref, w1_ref, b1_ref, fr_ref, w2_ref, b2_ref, w3_ref, dl_ref, h_ref, kl_ref):
    i = pl.program_id(0)
    tr = zf_ref.shape[0]
    zf = zf_ref[...]
    fr = fr_ref[...]
    h = jnp.sin(fr * (_dot_hi(zf, w1_ref[...]) + b1_ref[...]))
    h = jnp.sin(fr * (_dot_hi(h, w2_ref[...]) + b2_ref[...]))
    h = _dot_hi(h, w3_ref[...])
    decay = jnp.exp(-zf[:, 0:1] * dl_ref[...])
    row = lax.broadcasted_iota(jnp.int32, (tr, D_HYENA), 0) + i * tr
    hf = h[:, :D_HYENA] * decay
    hb = jnp.where(row == 0, 0.0, h[:, D_HYENA:] * decay)
    h_ref[:, :D_HYENA] = hf.astype(bf16)
    h_ref[:, D_HYENA:] = hb.astype(bf16)
    alt = (1 - 2 * (row & 1)).astype(f32)
    part = jnp.sum(alt * (hf + hb), axis=0, keepdims=True)

    @pl.when(i == 0)
    def _():
        kl_ref[...] = jnp.zeros_like(kl_ref)

    kl_ref[...] += jnp.broadcast_to(part, kl_ref.shape)


def _hy_filters(seq, w1, b1, w2, b2, w3, freq):
    tr = 256
    zf = jnp.asarray(_filter_features(seq))
    deltas = np.abs(np.linspace(MIN_DECAY, MAX_DECAY, D_HYENA, dtype=np.float64))[None, :].astype(np.float32)
    w1p = jnp.pad(w1, ((0, LANES - EMB_DIM), (0, 0)))
    const = lambda shape: pl.BlockSpec(shape, lambda i: (0, 0))
    return pl.pallas_call(
        _filt_kernel,
        out_shape=(jax.ShapeDtypeStruct((seq, 2 * D_HYENA), bf16),
                   jax.ShapeDtypeStruct((SUBLANES, D_HYENA), f32)),
        grid=(seq // tr,),
        in_specs=[pl.BlockSpec((tr, LANES), lambda i: (i, 0)),
                  const((LANES, FILTER_FF)), const((1, FILTER_FF)), const((1, FILTER_FF)),
                  const((FILTER_FF, FILTER_FF)), const((1, FILTER_FF)),
                  const((FILTER_FF, 2 * D_HYENA)), const((1, D_HYENA))],
        out_specs=(pl.BlockSpec((tr, 2 * D_HYENA), lambda i: (i, 0)),
                   pl.BlockSpec((SUBLANES, D_HYENA), lambda i: (0, 0))),
        compiler_params=_cparams(("arbitrary",)),
        name="hyena_filters",
    )(zf, w1p, b1.reshape(1, -1), freq.reshape(1, -1), w2, b2.reshape(1, -1), w3, jnp.asarray(deltas))


def _dft_mats(seq):
    n = 2 * seq
    ph = (np.arange(seq, dtype=np.int64)[:, None] * np.arange(seq, dtype=np.int64)[None, :]) % n
    ang = ph.astype(np.float64) * (2.0 * math.pi / n)
    return np.cos(ang).astype(np.float32), np.sin(ang).astype(np.float32)


def _alt_col(rows, offset):
    row = lax.broadcasted_iota(jnp.int32, (rows, 1), 0) + offset
    return (1 - 2 * (row & 1)).astype(f32)


def _hy_fwd_kernel(fr_ref, fi_ref, z_ref, h_ref, kl_ref, yr_ref, yi_ref, yl_ref, kr_s, ki_s, *, n):
    i = pl.program_id(0)
    j = pl.program_id(1)
    tf = fr_ref.shape[0]
    tn = z_ref.shape[1]
    frb = fr_ref[...].astype(bf16)
    fib = fi_ref[...].astype(bf16)

    @pl.when(j == 0)
    def _():
        ah = _dot(frb, h_ref[...])
        bh = _dot(fib, h_ref[...])
        f = lax.broadcasted_iota(jnp.int32, (tf, 1), 0) + i * tf
        cf = jnp.where(f == 0, 1.0 / n, 2.0 / n)
        kr_s[...] = (ah[:, :D_HYENA] + ah[:, D_HYENA:]) * cf
        ki_s[...] = (bh[:, D_HYENA:] - bh[:, :D_HYENA]) * cf

    a = _dot(frb, z_ref[...])
    b = _dot(fib, z_ref[...])
    kr = kr_s[...]
    ki = ki_s[...]
    for c in range(tn // D_HYENA):
        cs = slice(c * D_HYENA, (c + 1) * D_HYENA)
        yr_ref[:, cs] = (a[:, cs] * kr + b[:, cs] * ki).astype(bf16)
        yi_ref[:, cs] = (b[:, cs] * kr - a[:, cs] * ki).astype(bf16)

    @pl.when(i == 0)
    def _():
        alt = _alt_col(z_ref.shape[0], 0)
        nz = jnp.sum(z_ref[...].astype(f32) * alt, axis=0, keepdims=True)
        kl = jnp.concatenate([kl_ref[0:1, :]] * (tn // D_HYENA), axis=-1)
        yl_ref[...] = jnp.broadcast_to(nz * kl * (1.0 / n), yl_ref.shape)


def _hy_fwd(fr, fi, zbf, hcat, kl, seq):
    n_cols = zbf.shape[1]
    tf = 256
    tn = min(n_cols, 1024)
    ni, nj = seq // tf, n_cols // tn
    assert ni == 1 or nj == 1
    return pl.pallas_call(
        functools.partial(_hy_fwd_kernel, n=2 * seq),
        out_shape=(jax.ShapeDtypeStruct((seq, n_cols), bf16),
                   jax.ShapeDtypeStruct((seq, n_cols), bf16),
                   jax.ShapeDtypeStruct((SUBLANES, n_cols), f32)),
        grid=(ni, nj),
        in_specs=[pl.BlockSpec((tf, seq), lambda i, j: (i, 0)),
                  pl.BlockSpec((tf, seq), lambda i, j: (i, 0)),
                  pl.BlockSpec((seq, tn), lambda i, j: (0, j)),
                  pl.BlockSpec((seq, 2 * D_HYENA), lambda i, j: (0, 0)),
                  pl.BlockSpec((SUBLANES, D_HYENA), lambda i, j: (0, 0))],
        out_specs=(pl.BlockSpec((tf, tn), lambda i, j: (i, j)),
                   pl.BlockSpec((tf, tn), lambda i, j: (i, j)),
                   pl.BlockSpec((SUBLANES, tn), lambda i, j: (0, j))),
        scratch_shapes=[pltpu.VMEM((tf, D_HYENA), f32), pltpu.VMEM((tf, D_HYENA), f32)],
        compiler_params=_cparams(("arbitrary", "arbitrary")),
        name="hyena_dft_fwd",
    )(fr, fi, zbf, hcat, kl)


def _hy_inv_kernel(fr_ref, fi_ref, yr_ref, yi_ref, yl_ref, z_ref, x0_ref, skip_ref, g_ref, o_ref):
    tt = fr_ref.shape[0]
    tn = yr_ref.shape[1]
    y = _dot(fr_ref[...].astype(bf16), yr_ref[...]) + _dot(fi_ref[...].astype(bf16), yi_ref[...])
    alt = _alt_col(tt, pl.program_id(0) * tt)
    for c in range(tn // D_HYENA):
        cs = slice(c * D_HYENA, (c + 1) * D_HYENA)
        yc = y[:, cs] + alt * yl_ref[0:1, cs] + z_ref[:, cs] * skip_ref[...]
        o_ref[:, cs] = _rms(yc * x0_ref[:, cs], g_ref[...])


def _hy_inv(fr, fi, yr, yi, yl, z, x0c, skip, gnorm, seq):
    n_cols = z.shape[1]
    tt = 256
    tn = min(n_cols, 1024)
    blk = pl.BlockSpec((tt, tn), lambda i, j: (i, j))
    return pl.pallas_call(
        _hy_inv_kernel,
        out_shape=jax.ShapeDtypeStruct((seq, n_cols), f32),
        grid=(seq // tt, n_cols // tn),
        in_specs=[pl.BlockSpec((tt, seq), lambda i, j: (i, 0)),
                  pl.BlockSpec((tt, seq), lambda i, j: (i, 0)),
                  pl.BlockSpec((seq, tn), lambda i, j: (0, j)),
                  pl.BlockSpec((seq, tn), lambda i, j: (0, j)),
                  pl.BlockSpec((SUBLANES, tn), lambda i, j: (0, j)),
                  blk, blk,
                  pl.BlockSpec((1, D_HYENA), lambda i, j: (0, 0)),
                  pl.BlockSpec((1, D_HYENA), lambda i, j: (0, 0))],
        out_specs=blk,
        compiler_params=_cparams(("arbitrary", "arbitrary")),
        name="hyena_dft_inv",
    )(fr, fi, yr, yi, yl, z, x0c, skip.reshape(1, D_HYENA), gnorm.reshape(1, D_HYENA))


def _hyena(hy, lp, nb, seq):
    fr_np, fi_np = _dft_mats(seq)
    fr, fi = jnp.asarray(fr_np), jnp.asarray(fi_np)
    zbf, z, x0c = _hy_front(hy, lp['conv_w'], lp['conv_b'], nb, seq)
    hcat, kl = _hy_filters(seq, lp['filt_w1'], lp['filt_b1'], lp['filt_w2'], lp['filt_b2'],
                           lp['filt_w3'], lp['filt_freq'])
    yr, yi, yl = _hy_fwd(fr, fi, zbf, hcat, kl, seq)
    return _hy_inv(fr, fi, yr, yi, yl, z, x0c, lp['hyena_skip'], lp['gnorm_hyena'], seq)


def _route(logits):
    lane_i = lax.broadcasted_iota(jnp.int32, logits.shape, 1)
    lane = lane_i.astype(f32)
    big = float(ROUTE_LANES)
    is_g = lane_i < N_GROUPS
    mg = jnp.max(jnp.where(is_g, logits, -jnp.inf), axis=-1, keepdims=True)
    sg = jnp.sum(jnp.where(is_g, jnp.exp(logits - mg), 0.0), axis=-1, keepdims=True)
    g_w = 1.0 / sg
    g_idx = jnp.min(jnp.where(is_g & (logits == mg), lane, big), axis=-1, keepdims=True)
    e_id = lane_i - ROUTE_EXP_LANE0
    sel = (e_id >= 0) & (e_id < N_EXPERTS) & ((e_id >> 2).astype(f32) == g_idx)
    me = jnp.max(jnp.where(sel, logits, -jnp.inf), axis=-1, keepdims=True)
    ee = jnp.where(sel, jnp.exp(logits - me), 0.0)
    prob = ee / jnp.sum(ee, axis=-1, keepdims=True)
    p1 = jnp.max(jnp.where(sel, prob, -1.0), axis=-1, keepdims=True)
    i1 = jnp.min(jnp.where(sel & (prob == p1), lane, big), axis=-1, keepdims=True)
    sel2 = sel & (lane != i1)
    p2 = jnp.max(jnp.where(sel2, prob, -1.0), axis=-1, keepdims=True)
    i2 = jnp.min(jnp.where(sel2 & (prob == p2), lane, big), axis=-1, keepdims=True)
    tot = p1 + p2
    comb = jnp.where(lane == i1, g_w * (p1 / tot), 0.0) + jnp.where(lane == i2, g_w * (p2 / tot), 0.0)
    return jnp.where(lane_i == 0, g_idx, comb)


def _outproj_kernel(xp_ref, xs_ref, attp_ref, atts_ref, hyp_ref, hys_ref, mod_ref, wo_ref, g2_ref,
                    wr_ref, br_ref, x1_ref, tok_ref, rt_ref, wobf_ref, *, prompt_tiles):
    i = pl.program_id(0)
    tm = xp_ref.shape[0]

    @pl.when(i == 0)
    def _():
        _cast_rows(wo_ref, wobf_ref, 128)

    is_p = i < prompt_tiles
    x = jnp.where(is_p, xp_ref[...], xs_ref[...])
    att = jnp.where(is_p, attp_ref[...], atts_ref[...])
    hyo = jnp.where(is_p, hyp_ref[...], hys_ref[...])
    proj = (_dot(att.astype(bf16), wobf_ref[0:D_ATT, :]) + _dot(hyo.astype(bf16), wobf_ref[D_ATT:, :]))
    x1 = x + mod_ref[0, GT1:GT1 + 1, :] * proj
    x1_ref[...] = x1
    h2 = _rms(x1, g2_ref[...]) * (1.0 + mod_ref[0, SC2:SC2 + 1, :]) + mod_ref[0, SH2:SH2 + 1, :]
    rt = _route(_dot_hi(h2, wr_ref[...]) + br_ref[...])
    rt_ref[...] = rt
    for c in range(D_MODEL // LANES):
        tok_ref[pl.ds(c, tm, stride=TOK_SUB), :] = h2[:, c * LANES:(c + 1) * LANES]
    tok_ref[pl.ds(TOK_RT_ROW, tm, stride=TOK_SUB), :] = rt
    for c in range(TOK_RT_ROW + 1, TOK_SUB):
        tok_ref[pl.ds(c, tm, stride=TOK_SUB), :] = jnp.zeros((tm, LANES), f32)


def _outproj(xp, xs, attp, atts, hyp, hys, mod, w_out, norm2_g, wr, br, seq_p, seq_s):
    tm = 256
    tp, ts = xp.shape[0], xs.shape[0]
    npt, nst = tp // tm, ts // tm
    assert seq_p == tm and seq_s % tm == 0
    spb = seq_s // tm
    p_idx = lambda i: jnp.minimum(i, npt - 1)
    s_idx = lambda i: jnp.maximum(i - npt, 0)
    const = lambda shape: pl.BlockSpec(shape, lambda i: (0,) * len(shape))
    return pl.pallas_call(
        functools.partial(_outproj_kernel, prompt_tiles=npt),
        out_shape=(jax.ShapeDtypeStruct((tp + ts, D_MODEL), f32),
                   jax.ShapeDtypeStruct(((tp + ts) * TOK_SUB, LANES), f32),
                   jax.ShapeDtypeStruct((tp + ts, ROUTE_LANES), f32)),
        grid=(npt + nst,),
        in_specs=[pl.BlockSpec((tm, D_MODEL), lambda i: (p_idx(i), 0)),
                  pl.BlockSpec((tm, D_MODEL), lambda i: (s_idx(i), 0)),
                  pl.BlockSpec((tm, D_ATT), lambda i: (p_idx(i), 0)),
                  pl.BlockSpec((tm, D_ATT), lambda i: (s_idx(i), 0)),
                  pl.BlockSpec((tm, D_HYENA), lambda i: (0, p_idx(i))),
                  pl.BlockSpec((tm, D_HYENA), lambda i: (s_idx(i) % spb, s_idx(i) // spb)),
                  pl.BlockSpec((1, MOD_ROWS, D_MODEL),
                               lambda i: (jnp.where(i < npt, 0, 1 + s_idx(i) // spb), 0, 0)),
                  const((D_MODEL, D_MODEL)), const((1, D_MODEL)),
                  const((D_MODEL, ROUTE_LANES)), const((1, ROUTE_LANES))],
        out_specs=(pl.BlockSpec((tm, D_MODEL), lambda i: (i, 0)),
                   pl.BlockSpec((tm * TOK_SUB, LANES), lambda i: (i, 0)),
                   pl.BlockSpec((tm, ROUTE_LANES), lambda i: (i, 0))),
        scratch_shapes=[pltpu.VMEM((D_MODEL, D_MODEL), bf16)],
        compiler_params=_cparams(("arbitrary",)),
        name="outproj_router",
    )(xp, xs, attp, atts, hyp, hys, mod, w_out, norm2_g.reshape(1, D_MODEL), wr, br)


def _route_plan(g, tm, nt_max):
    t = g.shape[0]
    i32 = jnp.int32
    oh = (g[:, None] == jnp.arange(N_GROUPS, dtype=i32)[None, :]).astype(i32)
    cs = jnp.cumsum(oh, axis=0)
    counts = cs[-1]
    rank = jnp.sum(oh * (cs - 1), axis=1)
    tpg = (counts + tm - 1) // tm
    tend = jnp.cumsum(tpg)
    tstart = tend - tpg
    nt = tend[-1]
    pos = jnp.sum(oh * tstart[None, :], axis=1) * tm + rank
    src = jnp.zeros((nt_max * tm,), i32).at[pos].set(jnp.arange(t, dtype=i32))
    tiles = jnp.minimum(jnp.arange(nt_max, dtype=i32), nt - 1)
    tile_group = jnp.minimum(jnp.sum((tiles[:, None] >= tend[None, :]).astype(i32), axis=1), N_GROUPS - 1)
    tile_first = (tiles == jnp.sum((tile_group[:, None] == jnp.arange(N_GROUPS)[None, :]) * tstart[None, :],
                                   axis=1)).astype(i32)
    n_items = EXPERTS_PER_GROUP * nt
    slots = jnp.arange(EXPERTS_PER_GROUP * nt_max, dtype=i32)
    items = jnp.minimum(slots, n_items - 1)
    iend = EXPERTS_PER_GROUP * tend
    ig = jnp.minimum(jnp.sum((items[:, None] >= iend[None, :]).astype(i32), axis=1), N_GROUPS - 1)
    sel = (ig[:, None] == jnp.arange(N_GROUPS)[None, :]).astype(i32)
    g_tstart = jnp.sum(sel * tstart[None, :], axis=1)
    g_tpg = jnp.maximum(jnp.sum(sel * tpg[None, :], axis=1), 1)
    local = items - EXPERTS_PER_GROUP * g_tstart
    it_tile = g_tstart + local % g_tpg
    it_exp = EXPERTS_PER_GROUP * ig + local // g_tpg
    it_first = (local % g_tpg == 0).astype(i32)
    idle = jnp.maximum(slots - n_items, 0)
    is_idle = slots >= n_items
    it_otile = jnp.where(is_idle, nt + idle // EXPERTS_PER_GROUP, it_tile)
    it_ocol = jnp.where(is_idle, idle % EXPERTS_PER_GROUP, it_exp % EXPERTS_PER_GROUP)
    return dict(pos=pos, src=src, nt=nt.reshape(1), tiles=tiles, tile_group=tile_group, tile_first=tile_first,
                n_items=n_items.reshape(1), it_tile=it_tile, it_exp=it_exp, it_first=it_first,
                it_otile=it_otile, it_ocol=it_ocol)


def _row_dma_loop(n, make_copy, start):
    def body(r, c):
        cp = make_copy(r)
        cp.start() if start else cp.wait()
        return c

    lax.fori_loop(0, n, body, 0, unroll=8)


def _gather_kernel(src_ref, nt_ref, tok_hbm, xs_ref, rts_ref, buf_ref, sem_ref):
    t = pl.program_id(0)
    nt = nt_ref[0]
    tm = xs_ref.shape[0]

    def copy(tile, slot, r):
        return pltpu.make_async_copy(
            tok_hbm.at[src_ref[tile * tm + r]],
            buf_ref.at[slot, pl.ds(pl.multiple_of(r * TOK_SUB, TOK_SUB), TOK_SUB), :],
            sem_ref.at[slot])

    @pl.when(t == 0)
    def _():
        _row_dma_loop(tm, lambda r: copy(0, 0, r), True)

    @pl.when(t + 1 < nt)
    def _():
        _row_dma_loop(tm, lambda r: copy(t + 1, (t + 1) % 2, r), True)

    @pl.when(t < nt)
    def _():
        slot = t % 2
        _row_dma_loop(tm, lambda r: copy(t, slot, r), False)
        for c in range(D_MODEL // LANES):
            xs_ref[:, c * LANES:(c + 1) * LANES] = buf_ref[slot, pl.ds(c, tm, stride=TOK_SUB), :].astype(bf16)
        rts_ref[...] = buf_ref[slot, pl.ds(TOK_RT_ROW, tm, stride=TOK_SUB), :]

    @pl.when(t >= nt)
    def _():
        xs_ref[...] = jnp.zeros_like(xs_ref)
        rts_ref[...] = jnp.zeros_like(rts_ref)


def _gather(plan, tok, nt_max, tm):
    npad = nt_max * tm
    tok3 = tok.reshape(tok.shape[0] // TOK_SUB, TOK_SUB, LANES)
    return pl.pallas_call(
        _gather_kernel,
        out_shape=(jax.ShapeDtypeStruct((npad, D_MODEL), bf16),
                   jax.ShapeDtypeStruct((npad, ROUTE_LANES), f32)),
        grid_spec=pltpu.PrefetchScalarGridSpec(
            num_scalar_prefetch=2,
            grid=(nt_max,),
            in_specs=[pl.BlockSpec(memory_space=pl.ANY)],
            out_specs=(pl.BlockSpec((tm, D_MODEL), lambda t, *_: (t, 0)),
                       pl.BlockSpec((tm, ROUTE_LANES), lambda t, *_: (t, 0))),
            scratch_shapes=[pltpu.VMEM((2, tm * TOK_SUB, LANES), f32),
                            pltpu.SemaphoreType.DMA((2,))]),
        compiler_params=_cparams(("arbitrary",)),
        name="moe_gather",
    )(plan['src'], plan['nt'], tok3)


def _moe_up_kernel(it_tile, it_exp, it_first, it_otile, it_ocol, n_items, xs_ref, rts_ref, wg_ref, wu_ref,
                   hid_ref, wgb_ref, wub_ref, *, chunk):
    i = pl.program_id(0)
    tm = xs_ref.shape[0]

    @pl.when(i >= n_items[0])
    def _():
        hid_ref[...] = jnp.zeros_like(hid_ref)

    @pl.when(i < n_items[0])
    def _():
        @pl.when(it_first[i] == 1)
        def _():
            wgb_ref[...] = wg_ref[0].astype(bf16)
            wub_ref[...] = wu_ref[0].astype(bf16)

        lane_sel = it_exp[i] + ROUTE_EXP_LANE0

        def body(k, c):
            r = pl.multiple_of(k * chunk, chunk)
            x = xs_ref[pl.ds(r, chunk), :]
            g = _dot(x, wgb_ref[...])
            u = _dot(x, wub_ref[...])
            rt = rts_ref[pl.ds(r, chunk), :]
            lane = lax.broadcasted_iota(jnp.int32, rt.shape, 1)
            comb = jnp.sum(jnp.where(lane == lane_sel, rt, 0.0), axis=-1, keepdims=True)
            hid = (g * (1.0 / (1.0 + jnp.exp(-g)))) * u
            hid_ref[pl.ds(r, chunk), :] = (hid * comb).astype(bf16)
            return c

        lax.fori_loop(0, tm // chunk, body, 0)


def _moe_up(plan, xs, rts, w_gate, w_up, nt_max, tm):
    npad = nt_max * tm
    return pl.pallas_call(
        functools.partial(_moe_up_kernel, chunk=256),
        out_shape=jax.ShapeDtypeStruct((npad, EXPERTS_PER_GROUP * D_EXPERT), bf16),
        grid_spec=pltpu.PrefetchScalarGridSpec(
            num_scalar_prefetch=6,
            grid=(EXPERTS_PER_GROUP * nt_max,),
            in_specs=[pl.BlockSpec((tm, D_MODEL), lambda i, tl, ex, fi, ot, oc, n: (tl[i], 0)),
                      pl.BlockSpec((tm, ROUTE_LANES), lambda i, tl, ex, fi, ot, oc, n: (tl[i], 0)),
                      pl.BlockSpec((1, D_MODEL, D_EXPERT), lambda i, tl, ex, fi, ot, oc, n: (ex[i], 0, 0)),
                      pl.BlockSpec((1, D_MODEL, D_EXPERT), lambda i, tl, ex, fi, ot, oc, n: (ex[i], 0, 0))],
            out_specs=pl.BlockSpec((tm, D_EXPERT), lambda i, tl, ex, fi, ot, oc, n: (ot[i], oc[i])),
            scratch_shapes=[pltpu.VMEM((D_MODEL, D_EXPERT), bf16), pltpu.VMEM((D_MODEL, D_EXPERT), bf16)]),
        compiler_params=_cparams(("arbitrary",)),
        name="moe_gate_up",
    )(plan['it_tile'], plan['it_exp'], plan['it_first'], plan['it_otile'], plan['it_ocol'], plan['n_items'],
      xs, rts, w_gate, w_up)


def _moe_down_kernel(tiles, tile_group, tile_first, nt, hid_ref, wd_ref, ys_ref, wdb_ref):
    t = pl.program_id(0)
    tm = hid_ref.shape[0]

    @pl.when(t >= nt[0])
    def _():
        ys_ref[...] = jnp.zeros_like(ys_ref)

    @pl.when(t < nt[0])
    def _():
        @pl.when(tile_first[t] == 1)
        def _():
            _cast_rows(wd_ref.at[0], wdb_ref, 256)

        y = _dot(hid_ref[...], wdb_ref[...])
        for c in range(D_MODEL // LANES):
            ys_ref[pl.ds(c, tm, stride=SUBLANES), :] = y[:, c * LANES:(c + 1) * LANES]


def _moe_down(plan, hid, w_down, nt_max, tm):
    npad = nt_max * tm
    kdim = EXPERTS_PER_GROUP * D_EXPERT
    wd4 = w_down.reshape(N_GROUPS, kdim, D_MODEL)
    return pl.pallas_call(
        _moe_down_kernel,
        out_shape=jax.ShapeDtypeStruct((npad * SUBLANES, LANES), f32),
        grid_spec=pltpu.PrefetchScalarGridSpec(
            num_scalar_prefetch=4,
            grid=(nt_max,),
            in_specs=[pl.BlockSpec((tm, kdim), lambda t, tl, tg, tf, n: (tl[t], 0)),
                      pl.BlockSpec((1, kdim, D_MODEL), lambda t, tl, tg, tf, n: (tg[t], 0, 0))],
            out_specs=pl.BlockSpec((tm * SUBLANES, LANES), lambda t, tl, tg, tf, n: (t, 0)),
            scratch_shapes=[pltpu.VMEM((kdim, D_MODEL), bf16)]),
        compiler_params=_cparams(("arbitrary",)),
        name="moe_down",
    )(plan['tiles'], plan['tile_group'], plan['tile_first'], plan['nt'], hid, wd4)


def _final_kernel(pos_ref, ys_hbm, x1_ref, mod_ref, fg_ref, yp_ref, yl_ref, buf_ref, sem_ref, *, prompt_tiles):
    i = pl.program_id(0)
    n = pl.num_programs(0)
    tm = x1_ref.shape[0]

    def copy(tile, slot, r):
        return pltpu.make_async_copy(
            ys_hbm.at[pos_ref[tile * tm + r]],
            buf_ref.at[slot, pl.ds(pl.multiple_of(r * SUBLANES, SUBLANES), SUBLANES), :],
            sem_ref.at[slot])

    @pl.when(i == 0)
    def _():
        _row_dma_loop(tm, lambda r: copy(0, 0, r), True)

    @pl.when(i + 1 < n)
    def _():
        _row_dma_loop(tm, lambda r: copy(i + 1, (i + 1) % 2, r), True)

    slot = i % 2
    _row_dma_loop(tm, lambda r: copy(i, slot, r), False)
    moe = jnp.concatenate([buf_ref[slot, pl.ds(c, tm, stride=SUBLANES), :] for c in range(D_MODEL // LANES)],
                          axis=-1)
    y = _rms(x1_ref[...] + mod_ref[0, GT2:GT2 + 1, :] * moe, fg_ref[...])

    @pl.when(i < prompt_tiles)
    def _():
        yp_ref[...] = y

    @pl.when(i >= prompt_tiles)
    def _():
        yl_ref[...] = y


def _final(plan, ys, x1, mod, final_g, t_prompt, t_lat, seq_s):
    tm = 256
    npt, nst = t_prompt // tm, t_lat // tm
    spb = seq_s // tm
    ys3 = ys.reshape(ys.shape[0] // SUBLANES, SUBLANES, LANES)
    return pl.pallas_call(
        functools.partial(_final_kernel, prompt_tiles=npt),
        out_shape=(jax.ShapeDtypeStruct((t_prompt, D_MODEL), f32),
                   jax.ShapeDtypeStruct((t_lat, D_MODEL), f32)),
        grid_spec=pltpu.PrefetchScalarGridSpec(
            num_scalar_prefetch=1,
            grid=(npt + nst,),
            in_specs=[pl.BlockSpec(memory_space=pl.ANY),
                      pl.BlockSpec((tm, D_MODEL), lambda i, p: (i, 0)),
                      pl.BlockSpec((1, MOD_ROWS, D_MODEL),
                                   lambda i, p: (jnp.where(i < npt, 0, 1 + jnp.maximum(i - npt, 0) // spb), 0, 0)),
                      pl.BlockSpec((1, D_MODEL), lambda i, p: (0, 0))],
            out_specs=(pl.BlockSpec((tm, D_MODEL), lambda i, p: (jnp.minimum(i, npt - 1), 0)),
                       pl.BlockSpec((tm, D_MODEL), lambda i, p: (jnp.maximum(i - npt, 0), 0))),
            scratch_shapes=[pltpu.VMEM((2, tm * SUBLANES, LANES), f32),
                            pltpu.SemaphoreType.DMA((2,))]),
        compiler_params=_cparams(("arbitrary",)),
        name="moe_unsort_final",
    )(plan['pos'], ys3, x1, mod, final_g.reshape(1, D_MODEL))


def _moe(tok, rt, x1, mod, w_gate, w_up, w_down, final_g, t_prompt, t_lat, seq_s):
    t = t_prompt + t_lat
    tm = MOE_TM
    nt_max = t // tm + N_GROUPS
    plan = _route_plan(rt[:, 0].astype(jnp.int32), tm, nt_max)
    xs, rts = _gather(plan, tok, nt_max, tm)
    hid = _moe_up(plan, xs, rts, w_gate, w_up, nt_max, tm)
    ys = _moe_down(plan, hid, w_down, nt_max, tm)
    return _final(plan, ys, x1, mod, final_g, t_prompt, t_lat, seq_s)


def kernel(x_prompt, x_sample, cache_k, cache_v, c, c_ctx, w_ada, b_ada, norm1_g, w_in, rpb, conv_w, conv_b, filt_w1, filt_b1, filt_w2, filt_b2, filt_w3, filt_freq, hyena_skip, gnorm_att, gnorm_hyena, w_out, norm2_g, router_grp_w, router_grp_b, router_exp_w, router_exp_b, w_gate, w_up, w_down, final_g):
    depth = w_ada.shape[0]
    assert depth == 1
    batch, seq, _ = x_prompt.shape
    dec_batch, dec_seq, _ = x_sample.shape
    l = 0

    wr = jnp.zeros((D_MODEL, ROUTE_LANES), f32)
    wr = wr.at[:, :N_GROUPS].set(router_grp_w[l])
    wr = wr.at[:, ROUTE_EXP_LANE0:ROUTE_EXP_LANE0 + N_EXPERTS].set(router_exp_w[l])
    br = jnp.zeros((1, ROUTE_LANES), f32)
    br = br.at[0, :N_GROUPS].set(router_grp_b[l])
    br = br.at[0, ROUTE_EXP_LANE0:ROUTE_EXP_LANE0 + N_EXPERTS].set(router_exp_b[l])

    lp = {
        'norm1_g': norm1_g[l], 'w_in': w_in[l], 'conv_w': conv_w[l], 'conv_b': conv_b[l],
        'filt_w1': filt_w1[l], 'filt_b1': filt_b1[l], 'filt_w2': filt_w2[l], 'filt_b2': filt_b2[l],
        'filt_w3': filt_w3[l], 'filt_freq': filt_freq[l], 'hyena_skip': hyena_skip[l],
        'gnorm_hyena': gnorm_hyena[l], 'w_out': w_out[l], 'norm2_g': norm2_g[l],
        'wr': wr, 'br': br, 'w_gate': w_gate[l], 'w_up': w_up[l], 'w_down': w_down[l],
    }

    cond8 = jnp.zeros((SUBLANES, D_MODEL), f32).at[0].set(c_ctx).at[1:1 + dec_batch].set(c)
    mod = _ada_mod(cond8, w_ada[l], b_ada[l]).reshape(SUBLANES, N_MOD, D_MODEL)
    mod = jnp.pad(mod, ((0, 0), (0, MOD_ROWS - N_MOD), (0, 0)))
    mod_ctx, mod_lat = mod[0:1], mod[1:1 + dec_batch]

    xp = x_prompt.reshape(batch * seq, D_MODEL)
    xs = x_sample.reshape(dec_batch * dec_seq, D_MODEL)

    qp, k_ctx, v_ctx, hyp = _inproj(xp, mod_ctx, lp['norm1_g'], lp['w_in'], batch * seq)
    attp = _ctx_attention(qp, k_ctx, v_ctx, gnorm_att[l], seq)
    hyop = _hyena(hyp, lp, batch, seq)

    ql, kl, vl, hyl = _inproj(xs, mod_lat, lp['norm1_g'], lp['w_in'], dec_seq)
    bias = _na_bias(rpb[l])
    kc = cache_k[:, l].reshape(dec_batch * cache_k.shape[2], D_ATT)
    vc = cache_v[:, l].reshape(dec_batch * cache_v.shape[2], D_ATT)
    attl = _na_attention(ql, kl, vl, kc, vc, bias, gnorm_att[l], dec_batch, dec_seq)
    hyol = _hyena(hyl, lp, dec_batch, dec_seq)

    x1, tok, rt = _outproj(xp, xs, attp, attl, hyop, hyol, mod[0:1 + dec_batch], lp['w_out'], lp['norm2_g'],
                           lp['wr'], lp['br'], seq, dec_seq)
    yp, ys = _moe(tok, rt, x1, mod[0:1 + dec_batch], lp['w_gate'], lp['w_up'], lp['w_down'], final_g,
                  batch * seq, dec_batch * dec_seq, dec_seq)

    y_prompt = yp.reshape(batch, seq, D_MODEL)
    y_sample = ys.reshape(dec_batch, dec_seq, D_MODEL)
    new_k = k_ctx.reshape(batch, 1, seq, H_ATT, HEAD_DIM)
    new_v = v_ctx.reshape(batch, 1, seq, H_ATT, HEAD_DIM)
    return (y_prompt, y_sample, new_k, new_v)
```

```python
import functools
import math

import jax
import jax.numpy as jnp
import numpy as np
from jax import lax
from jax.experimental import pallas as pl
from jax.experimental.pallas import tpu as pltpu

f32 = jnp.float32
bf16 = jnp.bfloat16
HIGHEST = lax.Precision.HIGHEST

D_MODEL = 1024
GRID_W = 64
H_ATT = 8
HEAD_DIM = 64
D_ATT = H_ATT * HEAD_DIM
D_HYENA = 512
D_IN = 3 * D_ATT + 3 * D_HYENA
NA_ROWS = 8
NA_COLS = 16
SHORT_CONV = 3
FILTER_BANDS = 16
EMB_DIM = 1 + 2 * FILTER_BANDS
FILTER_FF = 64
DECAY_TARGET = 1e-2
MIN_DECAY = math.log(DECAY_TARGET) / 1.5
MAX_DECAY = math.log(DECAY_TARGET) / 0.3
N_GROUPS = 4
EXPERTS_PER_GROUP = 4
N_EXPERTS = N_GROUPS * EXPERTS_PER_GROUP
D_EXPERT = 512
N_MOD = 6
EPS = 1e-6
NEG_INF = -1e30
ATT_SCALE = HEAD_DIM ** -0.5

LANES = 128
SUBLANES = 8
MOD_ROWS = 8
ROUTE_LANES = 128
ROUTE_EXP_LANE0 = 16
ROUTE_E1, ROUTE_E2, ROUTE_W1, ROUTE_W2 = 0, 1, 2, 3
MOE_TM = 512
DMA_PRIORITIES = 2
VMEM_LIMIT = 56 * 1024 * 1024

SH1, SC1, GT1, SH2, SC2, GT2 = range(6)


def _cparams(sem, vmem=VMEM_LIMIT):
    return pltpu.CompilerParams(dimension_semantics=sem, vmem_limit_bytes=vmem)


def _dot(a, b):
    return jnp.dot(a, b, preferred_element_type=f32)


def _dot_hi(a, b):
    return lax.dot_general(a, b, (((1,), (0,)), ((), ())), precision=HIGHEST,
                           preferred_element_type=f32)


def _dot_nt(a, b):
    return lax.dot_general(a, b, (((1,), (1,)), ((), ())), preferred_element_type=f32)


def _rms(x, g):
    ms = jnp.mean(x * x, axis=-1, keepdims=True)
    return x * lax.rsqrt(ms + EPS) * g


def _cast_rows(src_ref, dst_ref, chunk):
    n = src_ref.shape[0] // chunk

    def body(i, c):
        r = pl.multiple_of(i * chunk, chunk)
        dst_ref[pl.ds(r, chunk), :] = src_ref[pl.ds(r, chunk), :].astype(dst_ref.dtype)
        return c

    lax.fori_loop(0, n, body, 0)


def _ada_kernel(c_ref, w_ref, b_ref, o_ref):
    c = c_ref[...]
    s = c * (1.0 / (1.0 + jnp.exp(-c)))
    o_ref[...] = _dot_hi(s, w_ref[...]) + b_ref[...]


def _ada_mod(cond8, w_ada, b_ada):
    tn = 1536
    n = N_MOD * D_MODEL
    return pl.pallas_call(
        _ada_kernel,
        out_shape=jax.ShapeDtypeStruct((SUBLANES, n), f32),
        grid=(n // tn,),
        in_specs=[pl.BlockSpec((SUBLANES, D_MODEL), lambda j: (0, 0)),
                  pl.BlockSpec((D_MODEL, tn), lambda j: (0, j)),
                  pl.BlockSpec((1, tn), lambda j: (0, j))],
        out_specs=pl.BlockSpec((SUBLANES, tn), lambda j: (0, j)),
        compiler_params=_cparams(("arbitrary",)),
        name="ada_mod",
    )(cond8, w_ada, b_ada.reshape(1, n))


def _inproj_kernel(x_ref, mod_ref, g_ref, w_ref, q_ref, k_ref, v_ref, hy_ref, wbf_ref):
    @pl.when(pl.program_id(0) == 0)
    def _():
        _cast_rows(w_ref, wbf_ref, 128)

    h = _rms(x_ref[...], g_ref[...])
    h = h * (1.0 + mod_ref[0, SC1:SC1 + 1, :]) + mod_ref[0, SH1:SH1 + 1, :]
    p = _dot(h.astype(bf16), wbf_ref[...])
    q_ref[...] = p[:, 0:D_ATT]
    k_ref[...] = p[:, D_ATT:2 * D_ATT]
    v_ref[...] = p[:, 2 * D_ATT:3 * D_ATT]
    hy_ref[...] = p[:, 3 * D_ATT:]


def _inproj(x, mod, norm_g, w_in, rows_per_mod):
    t = x.shape[0]
    tm = 512
    blocks_per_mod = rows_per_mod // tm
    return pl.pallas_call(
        _inproj_kernel,
        out_shape=(jax.ShapeDtypeStruct((t, D_ATT), f32),
                   jax.ShapeDtypeStruct((t, D_ATT), f32),
                   jax.ShapeDtypeStruct((t, D_ATT), f32),
                   jax.ShapeDtypeStruct((t, 3 * D_HYENA), f32)),
        grid=(t // tm,),
        in_specs=[pl.BlockSpec((tm, D_MODEL), lambda i: (i, 0)),
                  pl.BlockSpec((1, MOD_ROWS, D_MODEL), lambda i: (i // blocks_per_mod, 0, 0)),
                  pl.BlockSpec((1, D_MODEL), lambda i: (0, 0)),
                  pl.BlockSpec((D_MODEL, D_IN), lambda i: (0, 0), pipeline_mode=pl.Buffered(1))],
        out_specs=(pl.BlockSpec((tm, D_ATT), lambda i: (i, 0)),
                   pl.BlockSpec((tm, D_ATT), lambda i: (i, 0)),
                   pl.BlockSpec((tm, D_ATT), lambda i: (i, 0)),
                   pl.BlockSpec((tm, 3 * D_HYENA), lambda i: (i, 0))),
        scratch_shapes=[pltpu.VMEM((D_MODEL, D_IN), bf16)],
        compiler_params=_cparams(("arbitrary",)),
        name="inproj",
    )(x, mod, norm_g.reshape(1, D_MODEL), w_in)


def _split_heads(q2):
    lane = lax.broadcasted_iota(jnp.int32, q2.shape, 1)
    qa = jnp.where(lane < HEAD_DIM, q2, 0.0)
    qb = jnp.where(lane >= HEAD_DIM, q2, 0.0)
    return jnp.concatenate([qa, qb], axis=0)


def _merge_heads(o_ab):
    m = o_ab.shape[0] // 2
    lane = lax.broadcasted_iota(jnp.int32, (m, LANES), 1)
    return jnp.where(lane < HEAD_DIM, o_ab[:m], o_ab[m:])


def _ctx_attn_kernel(q_ref, k_ref, v_ref, g_ref, o_ref):
    outs = []
    for p in range(D_ATT // LANES):
        cs = slice(p * LANES, (p + 1) * LANES)
        qq = _split_heads(q_ref[:, cs] * ATT_SCALE).astype(bf16)
        s = _dot_nt(qq, k_ref[:, cs].astype(bf16))
        m = jnp.max(s, axis=-1, keepdims=True)
        e = jnp.exp(s - m)
        l = jnp.sum(e, axis=-1, keepdims=True)
        o_ab = _dot(e.astype(bf16), v_ref[:, cs].astype(bf16)) / l
        outs.append(_merge_heads(o_ab))
    o_ref[...] = _rms(jnp.concatenate(outs, axis=-1), g_ref[...])


def _ctx_attention(q, k, v, gnorm, seq):
    t = q.shape[0]
    spec = pl.BlockSpec((seq, D_ATT), lambda b: (b, 0))
    return pl.pallas_call(
        _ctx_attn_kernel,
        out_shape=jax.ShapeDtypeStruct((t, D_ATT), f32),
        grid=(t // seq,),
        in_specs=[spec, spec, spec, pl.BlockSpec((1, D_ATT), lambda b: (0, 0))],
        out_specs=spec,
        compiler_params=_cparams(("arbitrary",)),
        name="ctx_attn",
    )(q, k, v, gnorm.reshape(1, D_ATT))


def _na_tables():
    col = np.arange(GRID_W)
    cs = np.clip(col - NA_COLS // 2, 0, GRID_W - NA_COLS)
    col_mask = (col[None, :] >= cs[:, None]) & (col[None, :] < cs[:, None] + NA_COLS)
    dc = np.clip(col[None, :] - col[:, None] + NA_COLS - 1, 0, 2 * NA_COLS - 2)
    n_dc = 2 * NA_COLS - 1
    onehot = np.zeros((32, GRID_W * GRID_W), np.float32)
    onehot[dc.reshape(-1), np.arange(GRID_W * GRID_W)] = 1.0
    assert n_dc <= 32
    mask = np.tile(col_mask.astype(np.float32), (1, NA_ROWS))
    return onehot, mask


def _bias_tile_kernel(r_ref, oh_ref, o_ref):
    o_ref[...] = _dot_hi(r_ref[...], oh_ref[...])


def _na_bias(rpb):
    onehot, _ = _na_tables()
    n_dr = 2 * NA_ROWS - 1
    r2 = jnp.pad(rpb.reshape(H_ATT * n_dr, 2 * NA_COLS - 1), ((0, 0), (0, 1)))
    tiles = pl.pallas_call(
        _bias_tile_kernel,
        out_shape=jax.ShapeDtypeStruct((H_ATT * n_dr, GRID_W * GRID_W), f32),
        name="na_bias_tiles",
    )(r2, jnp.asarray(onehot))
    tiles = tiles.reshape(H_ATT, n_dr, GRID_W, GRID_W)
    win = jnp.stack([tiles[:, i0:i0 + NA_ROWS] for i0 in range(NA_ROWS)], axis=0)
    return win.transpose(0, 1, 3, 2, 4).reshape(NA_ROWS, H_ATT, GRID_W, NA_ROWS * GRID_W)


def _na_row_start(r, rows):
    return jnp.clip(r - NA_ROWS // 2, 0, rows - NA_ROWS)


def _na_attn_kernel(q_ref, k_ref, v_ref, kc_ref, vc_ref, bias_ref, mask_ref, g_ref, o_ref,
                    kbf_ref, vbf_ref, *, rows):
    r = pl.program_id(1)

    @pl.when(r == 0)
    def _():
        _cast_rows(k_ref, kbf_ref, 256)
        _cast_rows(v_ref, vbf_ref, 256)

    nwin = NA_ROWS * GRID_W
    start = pl.multiple_of(_na_row_start(r, rows) * GRID_W, GRID_W)
    valid = mask_ref[...] != 0.0
    valid2 = jnp.concatenate([valid, valid], axis=0)
    outs = []
    for p in range(D_ATT // LANES):
        cs = slice(p * LANES, (p + 1) * LANES)
        qq = _split_heads(q_ref[:, cs] * ATT_SCALE).astype(bf16)
        kw = kbf_ref[pl.ds(start, nwin), cs]
        vw = vbf_ref[pl.ds(start, nwin), cs]
        s_lat = _dot_nt(qq, kw)
        s_ctx = _dot_nt(qq, kc_ref[:, cs].astype(bf16))
        bias2 = jnp.concatenate([bias_ref[0, 2 * p], bias_ref[0, 2 * p + 1]], axis=0)
        s_lat = jnp.where(valid2, s_lat + bias2, NEG_INF)
        m = jnp.maximum(jnp.max(s_lat, axis=-1, keepdims=True), jnp.max(s_ctx, axis=-1, keepdims=True))
        e_lat = jnp.exp(s_lat - m)
        e_ctx = jnp.exp(s_ctx - m)
        l = jnp.sum(e_lat, axis=-1, keepdims=True) + jnp.sum(e_ctx, axis=-1, keepdims=True)
        o_ab = (_dot(e_lat.astype(bf16), vw) + _dot(e_ctx.astype(bf16), vc_ref[:, cs].astype(bf16))) / l
        outs.append(_merge_heads(o_ab))
    o_ref[...] = _rms(jnp.concatenate(outs, axis=-1), g_ref[...])


def _na_attention(q, k, v, kc, vc, bias, gnorm, nb, seq):
    rows = seq // GRID_W
    past = kc.shape[0] // nb
    _, mask = _na_tables()

    def bias_map(b, r):
        return (_na_row_start(r, rows) - r + NA_ROWS - 1, 0, 0, 0)

    return pl.pallas_call(
        functools.partial(_na_attn_kernel, rows=rows),
        out_shape=jax.ShapeDtypeStruct((nb * seq, D_ATT), f32),
        grid=(nb, rows),
        in_specs=[pl.BlockSpec((GRID_W, D_ATT), lambda b, r: (b * rows + r, 0)),
                  pl.BlockSpec((seq, D_ATT), lambda b, r: (b, 0)),
                  pl.BlockSpec((seq, D_ATT), lambda b, r: (b, 0)),
                  pl.BlockSpec((past, D_ATT), lambda b, r: (b, 0)),
                  pl.BlockSpec((past, D_ATT), lambda b, r: (b, 0)),
                  pl.BlockSpec((1, H_ATT, GRID_W, NA_ROWS * GRID_W), bias_map),
                  pl.BlockSpec((GRID_W, NA_ROWS * GRID_W), lambda b, r: (0, 0)),
                  pl.BlockSpec((1, D_ATT), lambda b, r: (0, 0))],
        out_specs=pl.BlockSpec((GRID_W, D_ATT), lambda b, r: (b * rows + r, 0)),
        scratch_shapes=[pltpu.VMEM((seq, D_ATT), bf16), pltpu.VMEM((seq, D_ATT), bf16)],
        compiler_params=_cparams(("arbitrary", "arbitrary")),
        name="na_attn",
    )(q, k, v, kc, vc, bias, jnp.asarray(mask), gnorm.reshape(1, D_ATT))


def _hy_front_kernel(x0_ref, x1_ref, v_ref, w0_ref, w1_ref, wv_ref, b0_ref, b1_ref, bv_ref,
                     zbf_ref, z_ref, x0c_ref):
    seq = x0_ref.shape[0]
    row = lax.broadcasted_iota(jnp.int32, x0_ref.shape, 0)
    first = row == 0
    last = row == seq - 1

    def conv(u_ref, w_ref, b_ref):
        u = u_ref[...]
        up = jnp.where(first, 0.0, pltpu.roll(u, 1, 0))
        un = jnp.where(last, 0.0, pltpu.roll(u, seq - 1, 0))
        y = b_ref[...] + up * w_ref[0:1, :]
        y = y + u * w_ref[1:2, :]
        return y + un * w_ref[2:3, :]

    z = conv(v_ref, wv_ref, bv_ref) * conv(x1_ref, w1_ref, b1_ref)
    z_ref[...] = z
    zbf_ref[...] = z.astype(bf16)
    x0c_ref[...] = conv(x0_ref, w0_ref, b0_ref)


def _hy_front(hy, conv_w, conv_b, nb, seq):
    tc = 256
    nc = D_HYENA // tc
    n = nb * D_HYENA
    cb = conv_b.reshape(1, 3 * D_HYENA)

    def part(k):
        return (pl.BlockSpec((seq, tc), lambda b, j: (b, k * nc + j)),
                pl.BlockSpec((SHORT_CONV, tc), lambda b, j: (0, k * nc + j)),
                pl.BlockSpec((1, tc), lambda b, j: (0, k * nc + j)))

    (x0s, w0s, b0s), (x1s, w1s, b1s), (vs, wvs, bvs) = part(0), part(1), part(2)
    ospec = pl.BlockSpec((seq, tc), lambda b, j: (0, b * nc + j))
    return pl.pallas_call(
        _hy_front_kernel,
        out_shape=(jax.ShapeDtypeStruct((seq, n), bf16),
                   jax.ShapeDtypeStruct((seq, n), f32),
                   jax.ShapeDtypeStruct((seq, n), f32)),
        grid=(nb, nc),
        in_specs=[x0s, x1s, vs, w0s, w1s, wvs, b0s, b1s, bvs],
        out_specs=(ospec, ospec, ospec),
        compiler_params=_cparams(("arbitrary", "arbitrary")),
        name="hyena_front",
    )(hy, hy, hy, conv_w, conv_w, conv_w, cb, cb, cb)


def _filter_features(seq):
    t = np.linspace(0.0, 1.0, seq, dtype=np.float64)[:, None]
    w = 2.0 * math.pi * np.arange(seq, dtype=np.float64)[:, None] / seq
    fb = np.linspace(1e-4, FILTER_BANDS - 1, FILTER_BANDS, dtype=np.float64)[None, :]
    ang = fb * w
    z = np.concatenate([t, np.cos(ang), -np.sin(ang)], axis=-1).astype(np.float32)
    return np.pad(z, ((0, 0), (0, LANES - EMB_DIM)))


def _filt_kernel(zf_ref, w1_ref, b1_ref, fr_ref, w2_ref, b2_ref, w3_ref, dl_ref, h_ref, kl_ref):
    i = pl.program_id(0)
    tr = zf_ref.shape[0]
    zf = zf_ref[...]
    fr = fr_ref[...]
    h = jnp.sin(fr * (_dot_hi(zf, w1_ref[...]) + b1_ref[...]))
    h = jnp.sin(fr * (_dot_hi(h, w2_ref[...]) + b2_ref[...]))
    h = _dot_hi(h, w3_ref[...])
    decay = jnp.exp(-zf[:, 0:1] * dl_ref[...])
    row = lax.broadcasted_iota(jnp.int32, (tr, D_HYENA), 0) + i * tr
    hf = h[:, :D_HYENA] * decay
    hb = jnp.where(row == 0, 0.0, h[:, D_HYENA:] * decay)
    h_ref[:, :D_HYENA] = hf.astype(bf16)
    h_ref[:, D_HYENA:] = hb.astype(bf16)
    alt = (1 - 2 * (row & 1)).astype(f32)
    part = jnp.sum(alt * (hf + hb), axis=0, keepdims=True)

    @pl.when(i == 0)
    def _():
        kl_ref[...] = jnp.zeros_like(kl_ref)

    kl_ref[...] += jnp.broadcast_to(part, kl_ref.shape)


def _hy_filters(seq, w1, b1, w2, b2, w3, freq):
    tr = 256
    zf = jnp.asarray(_filter_features(seq))
    deltas = np.abs(np.linspace(MIN_DECAY, MAX_DECAY, D_HYENA, dtype=np.float64))[None, :].astype(np.float32)
    w1p = jnp.pad(w1, ((0, LANES - EMB_DIM), (0, 0)))
    const = lambda shape: pl.BlockSpec(shape, lambda i: (0, 0))
    return pl.pallas_call(
        _filt_kernel,
        out_shape=(jax.ShapeDtypeStruct((seq, 2 * D_HYENA), bf16),
                   jax.ShapeDtypeStruct((SUBLANES, D_HYENA), f32)),
        grid=(seq // tr,),
        in_specs=[pl.BlockSpec((tr, LANES), lambda i: (i, 0)),
                  const((LANES, FILTER_FF)), const((1, FILTER_FF)), const((1, FILTER_FF)),
                  const((FILTER_FF, FILTER_FF)), const((1, FILTER_FF)),
                  const((FILTER_FF, 2 * D_HYENA)), const((1, D_HYENA))],
        out_specs=(pl.BlockSpec((tr, 2 * D_HYENA), lambda i: (i, 0)),
                   pl.BlockSpec((SUBLANES, D_HYENA), lambda i: (0, 0))),
        compiler_params=_cparams(("arbitrary",)),
        name="hyena_filters",
    )(zf, w1p, b1.reshape(1, -1), freq.reshape(1, -1), w2, b2.reshape(1, -1), w3, jnp.asarray(deltas))


def _dft_mats(seq):
    n = 2 * seq
    ph = (np.arange(seq, dtype=np.int64)[:, None] * np.arange(seq, dtype=np.int64)[None, :]) % n
    ang = ph.astype(np.float64) * (2.0 * math.pi / n)
    return np.cos(ang).astype(np.float32), np.sin(ang).astype(np.float32)


def _alt_col(rows, offset):
    row = lax.broadcasted_iota(jnp.int32, (rows, 1), 0) + offset
    return (1 - 2 * (row & 1)).astype(f32)


def _hy_fwd_kernel(fr_ref, fi_ref, z_ref, h_ref, kl_ref, yr_ref, yi_ref, yl_ref, kr_s, ki_s, *, n):
    i = pl.program_id(0)
    j = pl.program_id(1)
    tf = fr_ref.shape[0]
    tn = z_ref.shape[1]
    frb = fr_ref[...].astype(bf16)
    fib = fi_ref[...].astype(bf16)

    @pl.when(j == 0)
    def _():
        ah = _dot(frb, h_ref[...])
        bh = _dot(fib, h_ref[...])
        f = lax.broadcasted_iota(jnp.int32, (tf, 1), 0) + i * tf
        cf = jnp.where(f == 0, 1.0 / n, 2.0 / n)
        kr_s[...] = (ah[:, :D_HYENA] + ah[:, D_HYENA:]) * cf
        ki_s[...] = (bh[:, D_HYENA:] - bh[:, :D_HYENA]) * cf

    a = _dot(frb, z_ref[...])
    b = _dot(fib, z_ref[...])
    kr = kr_s[...]
    ki = ki_s[...]
    for c in range(tn // D_HYENA):
        cs = slice(c * D_HYENA, (c + 1) * D_HYENA)
        yr_ref[:, cs] = (a[:, cs] * kr + b[:, cs] * ki).astype(bf16)
        yi_ref[:, cs] = (b[:, cs] * kr - a[:, cs] * ki).astype(bf16)

    @pl.when(i == 0)
    def _():
        alt = _alt_col(z_ref.shape[0], 0)
        nz = jnp.sum(z_ref[...].astype(f32) * alt, axis=0, keepdims=True)
        kl = jnp.concatenate([kl_ref[0:1, :]] * (tn // D_HYENA), axis=-1)
        yl_ref[...] = jnp.broadcast_to(nz * kl * (1.0 / n), yl_ref.shape)


def _hy_fwd(fr, fi, zbf, hcat, kl, seq):
    n_cols = zbf.shape[1]
    tf = 256
    tn = min(n_cols, 1024)
    ni, nj = seq // tf, n_cols // tn
    assert ni == 1 or nj == 1
    return pl.pallas_call(
        functools.partial(_hy_fwd_kernel, n=2 * seq),
        out_shape=(jax.ShapeDtypeStruct((seq, n_cols), bf16),
                   jax.ShapeDtypeStruct((seq, n_cols), bf16),
                   jax.ShapeDtypeStruct((SUBLANES, n_cols), f32)),
        grid=(ni, nj),
        in_specs=[pl.BlockSpec((tf, seq), lambda i, j: (i, 0)),
                  pl.BlockSpec((tf, seq), lambda i, j: (i, 0)),
                  pl.BlockSpec((seq, tn), lambda i, j: (0, j)),
                  pl.BlockSpec((seq, 2 * D_HYENA), lambda i, j: (0, 0)),
                  pl.BlockSpec((SUBLANES, D_HYENA), lambda i, j: (0, 0))],
        out_specs=(pl.BlockSpec((tf, tn), lambda i, j: (i, j)),
                   pl.BlockSpec((tf, tn), lambda i, j: (i, j)),
                   pl.BlockSpec((SUBLANES, tn), lambda i, j: (0, j))),
        scratch_shapes=[pltpu.VMEM((tf, D_HYENA), f32), pltpu.VMEM((tf, D_HYENA), f32)],
        compiler_params=_cparams(("arbitrary", "arbitrary")),
        name="hyena_dft_fwd",
    )(fr, fi, zbf, hcat, kl)


def _hy_inv_kernel(fr_ref, fi_ref, yr_ref, yi_ref, yl_ref, z_ref, x0_ref, skip_ref, g_ref, o_ref):
    tt = fr_ref.shape[0]
    tn = yr_ref.shape[1]
    y = _dot(fr_ref[...].astype(bf16), yr_ref[...]) + _dot(fi_ref[...].astype(bf16), yi_ref[...])
    alt = _alt_col(tt, pl.program_id(0) * tt)
    for c in range(tn // D_HYENA):
        cs = slice(c * D_HYENA, (c + 1) * D_HYENA)
        yc = y[:, cs] + alt * yl_ref[0:1, cs] + z_ref[:, cs] * skip_ref[...]
        o_ref[:, cs] = _rms(yc * x0_ref[:, cs], g_ref[...])


def _hy_inv(fr, fi, yr, yi, yl, z, x0c, skip, gnorm, seq):
    n_cols = z.shape[1]
    tt = 256
    tn = min(n_cols, 1024)
    blk = pl.BlockSpec((tt, tn), lambda i, j: (i, j))
    return pl.pallas_call(
        _hy_inv_kernel,
        out_shape=jax.ShapeDtypeStruct((seq, n_cols), f32),
        grid=(seq // tt, n_cols // tn),
        in_specs=[pl.BlockSpec((tt, seq), lambda i, j: (i, 0)),
                  pl.BlockSpec((tt, seq), lambda i, j: (i, 0)),
                  pl.BlockSpec((seq, tn), lambda i, j: (0, j)),
                  pl.BlockSpec((seq, tn), lambda i, j: (0, j)),
                  pl.BlockSpec((SUBLANES, tn), lambda i, j: (0, j)),
                  blk, blk,
                  pl.BlockSpec((1, D_HYENA), lambda i, j: (0, 0)),
                  pl.BlockSpec((1, D_HYENA), lambda i, j: (0, 0))],
        out_specs=blk,
        compiler_params=_cparams(("arbitrary", "arbitrary")),
        name="hyena_dft_inv",
    )(fr, fi, yr, yi, yl, z, x0c, skip.reshape(1, D_HYENA), gnorm.reshape(1, D_HYENA))


def _hyena(hy, lp, nb, seq):
    fr_np, fi_np = _dft_mats(seq)
    fr, fi = jnp.asarray(fr_np), jnp.asarray(fi_np)
    zbf, z, x0c = _hy_front(hy, lp['conv_w'], lp['conv_b'], nb, seq)
    hcat, kl = _hy_filters(seq, lp['filt_w1'], lp['filt_b1'], lp['filt_w2'], lp['filt_b2'],
                           lp['filt_w3'], lp['filt_freq'])
    yr, yi, yl = _hy_fwd(fr, fi, zbf, hcat, kl, seq)
    return _hy_inv(fr, fi, yr, yi, yl, z, x0c, lp['hyena_skip'], lp['gnorm_hyena'], seq)


def _store_token_tiles(ref, x):
    m = x.shape[0]
    for c in range(D_MODEL // LANES):
        ref[pl.ds(c, m, stride=SUBLANES), :] = x[:, c * LANES:(c + 1) * LANES]


def _load_token_tiles(ref, m, lead=()):
    return jnp.concatenate([ref[lead + (pl.ds(c, m, stride=SUBLANES), slice(None))]
                            for c in range(D_MODEL // LANES)], axis=-1)


def _route(logits):
    lane_i = lax.broadcasted_iota(jnp.int32, logits.shape, 1)
    lane = lane_i.astype(f32)
    big = float(ROUTE_LANES)
    is_g = lane_i < N_GROUPS
    mg = jnp.max(jnp.where(is_g, logits, -jnp.inf), axis=-1, keepdims=True)
    sg = jnp.sum(jnp.where(is_g, jnp.exp(logits - mg), 0.0), axis=-1, keepdims=True)
    g_w = 1.0 / sg
    g_idx = jnp.min(jnp.where(is_g & (logits == mg), lane, big), axis=-1, keepdims=True)
    e_id = lane_i - ROUTE_EXP_LANE0
    sel = (e_id >= 0) & (e_id < N_EXPERTS) & ((e_id >> 2).astype(f32) == g_idx)
    me = jnp.max(jnp.where(sel, logits, -jnp.inf), axis=-1, keepdims=True)
    ee = jnp.where(sel, jnp.exp(logits - me), 0.0)
    prob = ee / jnp.sum(ee, axis=-1, keepdims=True)
    p1 = jnp.max(jnp.where(sel, prob, -1.0), axis=-1, keepdims=True)
    i1 = jnp.min(jnp.where(sel & (prob == p1), lane, big), axis=-1, keepdims=True)
    sel2 = sel & (lane != i1)
    p2 = jnp.max(jnp.where(sel2, prob, -1.0), axis=-1, keepdims=True)
    i2 = jnp.min(jnp.where(sel2 & (prob == p2), lane, big), axis=-1, keepdims=True)
    tot = p1 + p2
    rec = jnp.where(lane_i == ROUTE_E1, i1 - ROUTE_EXP_LANE0, 0.0)
    rec = jnp.where(lane_i == ROUTE_E2, i2 - ROUTE_EXP_LANE0, rec)
    rec = jnp.where(lane_i == ROUTE_W1, g_w * (p1 / tot), rec)
    return jnp.where(lane_i == ROUTE_W2, g_w * (p2 / tot), rec)


def _outproj_kernel(xp_ref, xs_ref, attp_ref, atts_ref, hyp_ref, hys_ref, mod_ref, wo_ref, g2_ref,
                    wr_ref, br_ref, x1_ref, tok_ref, rt_ref, wobf_ref, *, prompt_tiles):
    i = pl.program_id(0)
    tm = xp_ref.shape[0]

    @pl.when(i == 0)
    def _():
        _cast_rows(wo_ref, wobf_ref, 128)

    is_p = i < prompt_tiles
    x = jnp.where(is_p, xp_ref[...], xs_ref[...])
    att = jnp.where(is_p, attp_ref[...], atts_ref[...])
    hyo = jnp.where(is_p, hyp_ref[...], hys_ref[...])
    proj = (_dot(att.astype(bf16), wobf_ref[0:D_ATT, :]) + _dot(hyo.astype(bf16), wobf_ref[D_ATT:, :]))
    x1 = x + mod_ref[0, GT1:GT1 + 1, :] * proj
    x1_ref[...] = x1
    h2 = _rms(x1, g2_ref[...]) * (1.0 + mod_ref[0, SC2:SC2 + 1, :]) + mod_ref[0, SH2:SH2 + 1, :]
    rt_ref[...] = _route(_dot_hi(h2, wr_ref[...]) + br_ref[...])
    _store_token_tiles(tok_ref, h2)


def _outproj(xp, xs, attp, atts, hyp, hys, mod, w_out, norm2_g, wr, br, seq_p, seq_s):
    tm = 256
    tp, ts = xp.shape[0], xs.shape[0]
    npt, nst = tp // tm, ts // tm
    assert seq_p == tm and seq_s % tm == 0
    spb = seq_s // tm
    p_idx = lambda i: jnp.minimum(i, npt - 1)
    s_idx = lambda i: jnp.maximum(i - npt, 0)
    const = lambda shape: pl.BlockSpec(shape, lambda i: (0,) * len(shape))
    return pl.pallas_call(
        functools.partial(_outproj_kernel, prompt_tiles=npt),
        out_shape=(jax.ShapeDtypeStruct((tp + ts, D_MODEL), f32),
                   jax.ShapeDtypeStruct(((tp + ts) * SUBLANES, LANES), f32),
                   jax.ShapeDtypeStruct((tp + ts, ROUTE_LANES), f32)),
        grid=(npt + nst,),
        in_specs=[pl.BlockSpec((tm, D_MODEL), lambda i: (p_idx(i), 0)),
                  pl.BlockSpec((tm, D_MODEL), lambda i: (s_idx(i), 0)),
                  pl.BlockSpec((tm, D_ATT), lambda i: (p_idx(i), 0)),
                  pl.BlockSpec((tm, D_ATT), lambda i: (s_idx(i), 0)),
                  pl.BlockSpec((tm, D_HYENA), lambda i: (0, p_idx(i))),
                  pl.BlockSpec((tm, D_HYENA), lambda i: (s_idx(i) % spb, s_idx(i) // spb)),
                  pl.BlockSpec((1, MOD_ROWS, D_MODEL),
                               lambda i: (jnp.where(i < npt, 0, 1 + s_idx(i) // spb), 0, 0)),
                  const((D_MODEL, D_MODEL)), const((1, D_MODEL)),
                  const((D_MODEL, ROUTE_LANES)), const((1, ROUTE_LANES))],
        out_specs=(pl.BlockSpec((tm, D_MODEL), lambda i: (i, 0)),
                   pl.BlockSpec((tm * SUBLANES, LANES), lambda i: (i, 0)),
                   pl.BlockSpec((tm, ROUTE_LANES), lambda i: (i, 0))),
        scratch_shapes=[pltpu.VMEM((D_MODEL, D_MODEL), bf16)],
        compiler_params=_cparams(("arbitrary",)),
        name="outproj_router",
    )(xp, xs, attp, atts, hyp, hys, mod, w_out, norm2_g.reshape(1, D_MODEL), wr, br)


def _route_plan(e12, tm, max_items):
    i32 = jnp.int32
    experts = jnp.arange(N_EXPERTS, dtype=i32)
    oh = (e12[:, None] == experts[None, :]).astype(i32)
    cs = jnp.cumsum(oh, axis=0)
    counts = cs[-1]
    rank = jnp.sum(oh * (cs - 1), axis=1)
    end = jnp.cumsum(counts)
    off = end - counts
    pos = jnp.sum(oh * off[None, :], axis=1) + rank
    first_tile = off // tm
    n_tiles = jnp.where(counts > 0, (end - 1) // tm - first_tile + 1, 0)
    iend = jnp.cumsum(n_tiles)
    istart = iend - n_tiles
    n_items = iend[-1]
    items = jnp.minimum(jnp.arange(max_items, dtype=i32), n_items - 1)
    it_exp = jnp.minimum(jnp.sum((items[:, None] >= iend[None, :]).astype(i32), axis=1), N_EXPERTS - 1)
    sel = (it_exp[:, None] == experts[None, :]).astype(i32)
    pick = lambda v: jnp.sum(sel * v[None, :], axis=1)
    k = items - pick(istart)
    it_tile = pick(first_tile) + k
    it_lo = jnp.maximum(pick(off) - it_tile * tm, 0)
    it_hi = jnp.minimum(pick(end) - it_tile * tm, tm)
    flags = (k == 0).astype(i32) + 2 * (it_lo == 0).astype(i32) + 4 * (it_hi == tm).astype(i32)
    return dict(pos=pos, n_items=n_items.reshape(1), it_tile=it_tile, it_exp=it_exp, it_lo=it_lo,
                it_hi=it_hi, it_flags=flags)


FLAG_NEW_EXPERT, FLAG_TILE_START, FLAG_TILE_END = 1, 2, 4


ROW_DMA_UNROLL = 8


def _row_dma_loop(n, make_copy):
    assert n % ROW_DMA_UNROLL == 0

    def body(i, c):
        for u in range(ROW_DMA_UNROLL):
            make_copy(i * ROW_DMA_UNROLL + u).start(priority=u % DMA_PRIORITIES)
        return c

    lax.fori_loop(0, n // ROW_DMA_UNROLL, body, 0)


def _scatter_kernel(pos_ref, tok_hbm, xs_hbm, sem_ref, *, chunk, n_tok):
    i = pl.program_id(0)
    n = pl.num_programs(0)

    def copy(r):
        t = i * chunk + r
        t = jnp.where(t >= n_tok, t - n_tok, t)
        return pltpu.make_async_copy(tok_hbm.at[t], xs_hbm.at[pos_ref[i * chunk + r]], sem_ref.at[i % 2])

    def wait_chunk(slot):
        pltpu.make_async_copy(tok_hbm.at[pl.ds(0, chunk)], xs_hbm.at[pl.ds(0, chunk)], sem_ref.at[slot]).wait()

    _row_dma_loop(chunk, copy)

    @pl.when(i > 0)
    def _():
        wait_chunk((i + 1) % 2)

    @pl.when(i == n - 1)
    def _():
        wait_chunk(i % 2)


def _scatter(plan, tok, chunk=512):
    n_tok = tok.shape[0] // SUBLANES
    n_rows = plan['pos'].shape[0]
    tok3 = tok.reshape(n_tok, SUBLANES, LANES)
    return pl.pallas_call(
        functools.partial(_scatter_kernel, chunk=chunk, n_tok=n_tok),
        out_shape=jax.ShapeDtypeStruct((n_rows, SUBLANES, LANES), f32),
        grid_spec=pltpu.PrefetchScalarGridSpec(
            num_scalar_prefetch=1,
            grid=(n_rows // chunk,),
            in_specs=[pl.BlockSpec(memory_space=pl.ANY)],
            out_specs=pl.BlockSpec(memory_space=pl.ANY),
            scratch_shapes=[pltpu.SemaphoreType.DMA((2,))]),
        compiler_params=_cparams(("arbitrary",)),
        name="moe_scatter",
    )(plan['pos'], tok3)


def _moe_ffn_kernel(it_tile, it_exp, it_lo, it_hi, it_flags, n_items, xs_ref, wg_ref, wu_ref, wd_ref, ys_ref,
                    x_s, acc_s, wgb_s, wub_s, wdb_s, *, chunk):
    i = pl.program_id(0)
    tm = x_s.shape[0]

    @pl.when(i < n_items[0])
    def _():
        flags = it_flags[i]

        @pl.when((flags & FLAG_NEW_EXPERT) != 0)
        def _():
            wgb_s[...] = wg_ref[0].astype(bf16)
            wub_s[...] = wu_ref[0].astype(bf16)
            wdb_s[...] = wd_ref[0].astype(bf16)

        @pl.when((flags & FLAG_TILE_START) != 0)
        def _():
            x_s[...] = _load_token_tiles(xs_ref, tm).astype(bf16)
            acc_s[...] = jnp.zeros_like(acc_s)

        lo = it_lo[i]
        hi = it_hi[i]

        def body(k, c):
            r = pl.multiple_of(k * chunk, chunk)

            @pl.when((r < hi) & (r + chunk > lo))
            def _():
                x = x_s[pl.ds(r, chunk), :]
                g = _dot(x, wgb_s[...])
                u = _dot(x, wub_s[...])
                hid = (g * (1.0 / (1.0 + jnp.exp(-g)))) * u
                y = _dot(hid.astype(bf16), wdb_s[...])
                row = lax.broadcasted_iota(jnp.int32, (chunk, 1), 0) + r
                mine = (row >= lo) & (row < hi)
                acc_s[pl.ds(r, chunk), :] = jnp.where(mine, y, acc_s[pl.ds(r, chunk), :])

            return c

        lax.fori_loop(0, tm // chunk, body, 0)

        @pl.when((flags & FLAG_TILE_END) != 0)
        def _():
            _store_token_tiles(ys_ref, acc_s[...])


def _moe_ffn(plan, xs, w_gate, w_up, w_down, tm, max_items):
    n_rows = xs.shape[0]
    xs2 = xs.reshape(n_rows * SUBLANES, LANES)
    tile = lambda i, tl, ex, lo, hi, fl, n: (tl[i], 0)
    wspec = lambda shape: pl.BlockSpec((1,) + shape, lambda i, tl, ex, lo, hi, fl, n: (ex[i], 0, 0))
    return pl.pallas_call(
        functools.partial(_moe_ffn_kernel, chunk=256),
        out_shape=jax.ShapeDtypeStruct((n_rows * SUBLANES, LANES), f32),
        grid_spec=pltpu.PrefetchScalarGridSpec(
            num_scalar_prefetch=6,
            grid=(max_items,),
            in_specs=[pl.BlockSpec((tm * SUBLANES, LANES), tile),
                      wspec((D_MODEL, D_EXPERT)), wspec((D_MODEL, D_EXPERT)), wspec((D_EXPERT, D_MODEL))],
            out_specs=pl.BlockSpec((tm * SUBLANES, LANES), tile),
            scratch_shapes=[pltpu.VMEM((tm, D_MODEL), bf16), pltpu.VMEM((tm, D_MODEL), f32),
                            pltpu.VMEM((D_MODEL, D_EXPERT), bf16), pltpu.VMEM((D_MODEL, D_EXPERT), bf16),
                            pltpu.VMEM((D_EXPERT, D_MODEL), bf16)]),
        compiler_params=_cparams(("arbitrary",)),
        name="moe_ffn",
    )(plan['it_tile'], plan['it_exp'], plan['it_lo'], plan['it_hi'], plan['it_flags'], plan['n_items'],
      xs2, w_gate, w_up, w_down)


def _final_kernel(pos_ref, ys_hbm, x1_ref, rt_ref, mod_ref, fg_ref, yp_ref, yl_ref, buf_ref, sem_ref,
                  *, prompt_tiles, n_tok):
    i = pl.program_id(0)
    n = pl.num_programs(0)
    tm = x1_ref.shape[0]

    def start_tile(tile, slot):
        def copy(r):
            j = r // tm
            tok = r - j * tm
            return pltpu.make_async_copy(
                ys_hbm.at[pos_ref[j * n_tok + tile * tm + tok]],
                buf_ref.at[slot, j, pl.ds(pl.multiple_of(tok * SUBLANES, SUBLANES), SUBLANES), :],
                sem_ref.at[slot])

        _row_dma_loop(2 * tm, copy)

    @pl.when(i == 0)
    def _():
        start_tile(0, 0)

    @pl.when(i + 1 < n)
    def _():
        start_tile(i + 1, (i + 1) % 2)

    slot = i % 2
    pltpu.make_async_copy(buf_ref.at[slot], buf_ref.at[slot], sem_ref.at[slot]).wait()
    rt = rt_ref[...]
    moe = (rt[:, ROUTE_W1:ROUTE_W1 + 1] * _load_token_tiles(buf_ref, tm, (slot, 0))
           + rt[:, ROUTE_W2:ROUTE_W2 + 1] * _load_token_tiles(buf_ref, tm, (slot, 1)))
    y = _rms(x1_ref[...] + mod_ref[0, GT2:GT2 + 1, :] * moe, fg_ref[...])

    @pl.when(i < prompt_tiles)
    def _():
        yp_ref[...] = y

    @pl.when(i >= prompt_tiles)
    def _():
        yl_ref[...] = y


def _final(plan, ys, x1, rt, mod, final_g, t_prompt, t_lat, seq_s):
    tm = 256
    npt, nst = t_prompt // tm, t_lat // tm
    spb = seq_s // tm
    ys3 = ys.reshape(ys.shape[0] // SUBLANES, SUBLANES, LANES)
    return pl.pallas_call(
        functools.partial(_final_kernel, prompt_tiles=npt, n_tok=t_prompt + t_lat),
        out_shape=(jax.ShapeDtypeStruct((t_prompt, D_MODEL), f32),
                   jax.ShapeDtypeStruct((t_lat, D_MODEL), f32)),
        grid_spec=pltpu.PrefetchScalarGridSpec(
            num_scalar_prefetch=1,
            grid=(npt + nst,),
            in_specs=[pl.BlockSpec(memory_space=pl.ANY),
                      pl.BlockSpec((tm, D_MODEL), lambda i, p: (i, 0)),
                      pl.BlockSpec((tm, ROUTE_LANES), lambda i, p: (i, 0)),
                      pl.BlockSpec((1, MOD_ROWS, D_MODEL),
                                   lambda i, p: (jnp.where(i < npt, 0, 1 + jnp.maximum(i - npt, 0) // spb), 0, 0)),
                      pl.BlockSpec((1, D_MODEL), lambda i, p: (0, 0))],
            out_specs=(pl.BlockSpec((tm, D_MODEL), lambda i, p: (jnp.minimum(i, npt - 1), 0)),
                       pl.BlockSpec((tm, D_MODEL), lambda i, p: (jnp.maximum(i - npt, 0), 0))),
            scratch_shapes=[pltpu.VMEM((2, 2, tm * SUBLANES, LANES), f32),
                            pltpu.SemaphoreType.DMA((2,))]),
        compiler_params=_cparams(("arbitrary",)),
        name="moe_unsort_final",
    )(plan['pos'], ys3, x1, rt, mod, final_g.reshape(1, D_MODEL))


def _moe(tok, rt, x1, mod, w_gate, w_up, w_down, final_g, t_prompt, t_lat, seq_s):
    tm = MOE_TM
    n_rows = 2 * (t_prompt + t_lat)
    assert n_rows % tm == 0
    max_items = n_rows // tm + N_EXPERTS
    e12 = jnp.concatenate([rt[:, ROUTE_E1], rt[:, ROUTE_E2]]).astype(jnp.int32)
    plan = _route_plan(e12, tm, max_items)
    xs = _scatter(plan, tok)
    ys = _moe_ffn(plan, xs, w_gate, w_up, w_down, tm, max_items)
    return _final(plan, ys, x1, rt, mod, final_g, t_prompt, t_lat, seq_s)


def kernel(x_prompt, x_sample, cache_k, cache_v, c, c_ctx, w_ada, b_ada, norm1_g, w_in, rpb, conv_w, conv_b, filt_w1, filt_b1, filt_w2, filt_b2, filt_w3, filt_freq, hyena_skip, gnorm_att, gnorm_hyena, w_out, norm2_g, router_grp_w, router_grp_b, router_exp_w, router_exp_b, w_gate, w_up, w_down, final_g):
    depth = w_ada.shape[0]
    assert depth == 1
    batch, seq, _ = x_prompt.shape
    dec_batch, dec_seq, _ = x_sample.shape
    l = 0

    wr = jnp.zeros((D_MODEL, ROUTE_LANES), f32)
    wr = wr.at[:, :N_GROUPS].set(router_grp_w[l])
    wr = wr.at[:, ROUTE_EXP_LANE0:ROUTE_EXP_LANE0 + N_EXPERTS].set(router_exp_w[l])
    br = jnp.zeros((1, ROUTE_LANES), f32)
    br = br.at[0, :N_GROUPS].set(router_grp_b[l])
    br = br.at[0, ROUTE_EXP_LANE0:ROUTE_EXP_LANE0 + N_EXPERTS].set(router_exp_b[l])

    lp = {
        'norm1_g': norm1_g[l], 'w_in': w_in[l], 'conv_w': conv_w[l], 'conv_b': conv_b[l],
        'filt_w1': filt_w1[l], 'filt_b1': filt_b1[l], 'filt_w2': filt_w2[l], 'filt_b2': filt_b2[l],
        'filt_w3': filt_w3[l], 'filt_freq': filt_freq[l], 'hyena_skip': hyena_skip[l],
        'gnorm_hyena': gnorm_hyena[l], 'w_out': w_out[l], 'norm2_g': norm2_g[l],
        'wr': wr, 'br': br, 'w_gate': w_gate[l], 'w_up': w_up[l], 'w_down': w_down[l],
    }

    cond8 = jnp.zeros((SUBLANES, D_MODEL), f32).at[0].set(c_ctx).at[1:1 + dec_batch].set(c)
    mod = _ada_mod(cond8, w_ada[l], b_ada[l]).reshape(SUBLANES, N_MOD, D_MODEL)
    mod = jnp.pad(mod, ((0, 0), (0, MOD_ROWS - N_MOD), (0, 0)))
    mod_ctx, mod_lat = mod[0:1], mod[1:1 + dec_batch]

    xp = x_prompt.reshape(batch * seq, D_MODEL)
    xs = x_sample.reshape(dec_batch * dec_seq, D_MODEL)

    qp, k_ctx, v_ctx, hyp = _inproj(xp, mod_ctx, lp['norm1_g'], lp['w_in'], batch * seq)
    attp = _ctx_attention(qp, k_ctx, v_ctx, gnorm_att[l], seq)
    hyop = _hyena(hyp, lp, batch, seq)

    ql, kl, vl, hyl = _inproj(xs, mod_lat, lp['norm1_g'], lp['w_in'], dec_seq)
    bias = _na_bias(rpb[l])
    kc = cache_k[:, l].reshape(dec_batch * cache_k.shape[2], D_ATT)
    vc = cache_v[:, l].reshape(dec_batch * cache_v.shape[2], D_ATT)
    attl = _na_attention(ql, kl, vl, kc, vc, bias, gnorm_att[l], dec_batch, dec_seq)
    hyol = _hyena(hyl, lp, dec_batch, dec_seq)

    x1, tok, rt = _outproj(xp, xs, attp, attl, hyop, hyol, mod[0:1 + dec_batch], lp['w_out'], lp['norm2_g'],
                           lp['wr'], lp['br'], seq, dec_seq)
    yp, ys = _moe(tok, rt, x1, mod[0:1 + dec_batch], lp['w_gate'], lp['w_up'], lp['w_down'], final_g,
                  batch * seq, dec_batch * dec_seq, dec_seq)

    y_prompt = yp.reshape(batch, seq, D_MODEL)
    y_sample = ys.reshape(dec_batch, dec_seq, D_MODEL)
    new_k = k_ctx.reshape(batch, 1, seq, H_ATT, HEAD_DIM)
    new_v = v_ctx.reshape(batch, 1, seq, H_ATT, HEAD_DIM)
    return (y_prompt, y_sample, new_k, new_v)
```

```python
import functools
import math

import jax
import jax.numpy as jnp
import numpy as np
from jax import lax
from jax.experimental import pallas as pl
from jax.experimental.pallas import tpu as pltpu

f32 = jnp.float32
bf16 = jnp.bfloat16
HIGHEST = lax.Precision.HIGHEST

D_MODEL = 1024
GRID_W = 64
H_ATT = 8
HEAD_DIM = 64
D_ATT = H_ATT * HEAD_DIM
D_HYENA = 512
D_IN = 3 * D_ATT + 3 * D_HYENA
NA_ROWS = 8
NA_COLS = 16
SHORT_CONV = 3
FILTER_BANDS = 16
EMB_DIM = 1 + 2 * FILTER_BANDS
FILTER_FF = 64
DECAY_TARGET = 1e-2
MIN_DECAY = math.log(DECAY_TARGET) / 1.5
MAX_DECAY = math.log(DECAY_TARGET) / 0.3
N_GROUPS = 4
EXPERTS_PER_GROUP = 4
N_EXPERTS = N_GROUPS * EXPERTS_PER_GROUP
D_EXPERT = 512
N_MOD = 6
EPS = 1e-6
NEG_INF = -1e30
ATT_SCALE = HEAD_DIM ** -0.5

LANES = 128
SUBLANES = 8
MOD_ROWS = 8
ROUTE_LANES = 128
ROUTE_EXP_LANE0 = 16
ROUTE_E1, ROUTE_E2, ROUTE_W1, ROUTE_W2 = 0, 1, 2, 3
MOE_TM = 512
DMA_PRIORITIES = 2
VMEM_LIMIT = 56 * 1024 * 1024

SH1, SC1, GT1, SH2, SC2, GT2 = range(6)


def _cparams(sem, vmem=VMEM_LIMIT):
    return pltpu.CompilerParams(dimension_semantics=sem, vmem_limit_bytes=vmem)


def _dot(a, b):
    return jnp.dot(a, b, preferred_element_type=f32)


def _dot_hi(a, b):
    return lax.dot_general(a, b, (((1,), (0,)), ((), ())), precision=HIGHEST,
                           preferred_element_type=f32)


def _dot_nt(a, b):
    return lax.dot_general(a, b, (((1,), (1,)), ((), ())), preferred_element_type=f32)


def _rms(x, g):
    ms = jnp.mean(x * x, axis=-1, keepdims=True)
    return x * lax.rsqrt(ms + EPS) * g


def _cast_rows(src_ref, dst_ref, chunk):
    n = src_ref.shape[0] // chunk

    def body(i, c):
        r = pl.multiple_of(i * chunk, chunk)
        dst_ref[pl.ds(r, chunk), :] = src_ref[pl.ds(r, chunk), :].astype(dst_ref.dtype)
        return c

    lax.fori_loop(0, n, body, 0)


def _ada_kernel(c_ref, w_ref, b_ref, o_ref):
    c = c_ref[...]
    s = c * (1.0 / (1.0 + jnp.exp(-c)))
    o_ref[...] = _dot_hi(s, w_ref[...]) + b_ref[...]


def _ada_mod(cond8, w_ada, b_ada):
    tn = 1536
    n = N_MOD * D_MODEL
    return pl.pallas_call(
        _ada_kernel,
        out_shape=jax.ShapeDtypeStruct((SUBLANES, n), f32),
        grid=(n // tn,),
        in_specs=[pl.BlockSpec((SUBLANES, D_MODEL), lambda j: (0, 0)),
                  pl.BlockSpec((D_MODEL, tn), lambda j: (0, j)),
                  pl.BlockSpec((1, tn), lambda j: (0, j))],
        out_specs=pl.BlockSpec((SUBLANES, tn), lambda j: (0, j)),
        compiler_params=_cparams(("arbitrary",)),
        name="ada_mod",
    )(cond8, w_ada, b_ada.reshape(1, n))


def _inproj_kernel(x_ref, mod_ref, g_ref, w_ref, q_ref, k_ref, v_ref, hy_ref, wbf_ref):
    @pl.when(pl.program_id(0) == 0)
    def _():
        _cast_rows(w_ref, wbf_ref, 128)

    h = _rms(x_ref[...], g_ref[...])
    h = h * (1.0 + mod_ref[0, SC1:SC1 + 1, :]) + mod_ref[0, SH1:SH1 + 1, :]
    p = _dot(h.astype(bf16), wbf_ref[...])
    q_ref[...] = p[:, 0:D_ATT]
    k_ref[...] = p[:, D_ATT:2 * D_ATT]
    v_ref[...] = p[:, 2 * D_ATT:3 * D_ATT]
    hy_ref[...] = p[:, 3 * D_ATT:]


def _inproj(x, mod, norm_g, w_in, rows_per_mod):
    t = x.shape[0]
    tm = 512
    blocks_per_mod = rows_per_mod // tm
    return pl.pallas_call(
        _inproj_kernel,
        out_shape=(jax.ShapeDtypeStruct((t, D_ATT), f32),
                   jax.ShapeDtypeStruct((t, D_ATT), f32),
                   jax.ShapeDtypeStruct((t, D_ATT), f32),
                   jax.ShapeDtypeStruct((t, 3 * D_HYENA), f32)),
        grid=(t // tm,),
        in_specs=[pl.BlockSpec((tm, D_MODEL), lambda i: (i, 0)),
                  pl.BlockSpec((1, MOD_ROWS, D_MODEL), lambda i: (i // blocks_per_mod, 0, 0)),
                  pl.BlockSpec((1, D_MODEL), lambda i: (0, 0)),
                  pl.BlockSpec((D_MODEL, D_IN), lambda i: (0, 0), pipeline_mode=pl.Buffered(1))],
        out_specs=(pl.BlockSpec((tm, D_ATT), lambda i: (i, 0)),
                   pl.BlockSpec((tm, D_ATT), lambda i: (i, 0)),
                   pl.BlockSpec((tm, D_ATT), lambda i: (i, 0)),
                   pl.BlockSpec((tm, 3 * D_HYENA), lambda i: (i, 0))),
        scratch_shapes=[pltpu.VMEM((D_MODEL, D_IN), bf16)],
        compiler_params=_cparams(("arbitrary",)),
        name="inproj",
    )(x, mod, norm_g.reshape(1, D_MODEL), w_in)


def _split_heads(q2):
    lane = lax.broadcasted_iota(jnp.int32, q2.shape, 1)
    qa = jnp.where(lane < HEAD_DIM, q2, 0.0)
    qb = jnp.where(lane >= HEAD_DIM, q2, 0.0)
    return jnp.concatenate([qa, qb], axis=0)


def _merge_heads(o_ab):
    m = o_ab.shape[0] // 2
    lane = lax.broadcasted_iota(jnp.int32, (m, LANES), 1)
    return jnp.where(lane < HEAD_DIM, o_ab[:m], o_ab[m:])


def _ctx_attn_kernel(q_ref, k_ref, v_ref, g_ref, o_ref):
    outs = []
    for p in range(D_ATT // LANES):
        cs = slice(p * LANES, (p + 1) * LANES)
        qq = _split_heads(q_ref[:, cs] * ATT_SCALE).astype(bf16)
        s = _dot_nt(qq, k_ref[:, cs].astype(bf16))
        m = jnp.max(s, axis=-1, keepdims=True)
        e = jnp.exp(s - m)
        l = jnp.sum(e, axis=-1, keepdims=True)
        o_ab = _dot(e.astype(bf16), v_ref[:, cs].astype(bf16)) / l
        outs.append(_merge_heads(o_ab))
    o_ref[...] = _rms(jnp.concatenate(outs, axis=-1), g_ref[...])


def _ctx_attention(q, k, v, gnorm, seq):
    t = q.shape[0]
    spec = pl.BlockSpec((seq, D_ATT), lambda b: (b, 0))
    return pl.pallas_call(
        _ctx_attn_kernel,
        out_shape=jax.ShapeDtypeStruct((t, D_ATT), f32),
        grid=(t // seq,),
        in_specs=[spec, spec, spec, pl.BlockSpec((1, D_ATT), lambda b: (0, 0))],
        out_specs=spec,
        compiler_params=_cparams(("arbitrary",)),
        name="ctx_attn",
    )(q, k, v, gnorm.reshape(1, D_ATT))


def _na_tables():
    col = np.arange(GRID_W)
    cs = np.clip(col - NA_COLS // 2, 0, GRID_W - NA_COLS)
    col_mask = (col[None, :] >= cs[:, None]) & (col[None, :] < cs[:, None] + NA_COLS)
    dc = np.clip(col[None, :] - col[:, None] + NA_COLS - 1, 0, 2 * NA_COLS - 2)
    n_dc = 2 * NA_COLS - 1
    onehot = np.zeros((32, GRID_W * GRID_W), np.float32)
    onehot[dc.reshape(-1), np.arange(GRID_W * GRID_W)] = 1.0
    assert n_dc <= 32
    mask = np.tile(col_mask.astype(np.float32), (1, NA_ROWS))
    return onehot, mask


def _bias_tile_kernel(r_ref, oh_ref, o_ref):
    o_ref[...] = _dot_hi(r_ref[...], oh_ref[...])


def _na_bias(rpb):
    onehot, _ = _na_tables()
    n_dr = 2 * NA_ROWS - 1
    r2 = jnp.pad(rpb.reshape(H_ATT * n_dr, 2 * NA_COLS - 1), ((0, 0), (0, 1)))
    tiles = pl.pallas_call(
        _bias_tile_kernel,
        out_shape=jax.ShapeDtypeStruct((H_ATT * n_dr, GRID_W * GRID_W), f32),
        name="na_bias_tiles",
    )(r2, jnp.asarray(onehot))
    tiles = tiles.reshape(H_ATT, n_dr, GRID_W, GRID_W)
    win = jnp.stack([tiles[:, i0:i0 + NA_ROWS] for i0 in range(NA_ROWS)], axis=0)
    return win.transpose(0, 1, 3, 2, 4).reshape(NA_ROWS, H_ATT, GRID_W, NA_ROWS * GRID_W)


def _na_row_start(r, rows):
    return jnp.clip(r - NA_ROWS // 2, 0, rows - NA_ROWS)


def _na_attn_kernel(q_ref, k_ref, v_ref, kc_ref, vc_ref, bias_ref, mask_ref, g_ref, o_ref,
                    kbf_ref, vbf_ref, *, rows):
    r = pl.program_id(1)

    @pl.when(r == 0)
    def _():
        _cast_rows(k_ref, kbf_ref, 256)
        _cast_rows(v_ref, vbf_ref, 256)

    nwin = NA_ROWS * GRID_W
    start = pl.multiple_of(_na_row_start(r, rows) * GRID_W, GRID_W)
    valid = mask_ref[...] != 0.0
    valid2 = jnp.concatenate([valid, valid], axis=0)
    outs = []
    for p in range(D_ATT // LANES):
        cs = slice(p * LANES, (p + 1) * LANES)
        qq = _split_heads(q_ref[:, cs] * ATT_SCALE).astype(bf16)
        kw = kbf_ref[pl.ds(start, nwin), cs]
        vw = vbf_ref[pl.ds(start, nwin), cs]
        s_lat = _dot_nt(qq, kw)
        s_ctx = _dot_nt(qq, kc_ref[:, cs].astype(bf16))
        bias2 = jnp.concatenate([bias_ref[0, 2 * p], bias_ref[0, 2 * p + 1]], axis=0)
        s_lat = jnp.where(valid2, s_lat + bias2, NEG_INF)
        m = jnp.maximum(jnp.max(s_lat, axis=-1, keepdims=True), jnp.max(s_ctx, axis=-1, keepdims=True))
        e_lat = jnp.exp(s_lat - m)
        e_ctx = jnp.exp(s_ctx - m)
        l = jnp.sum(e_lat, axis=-1, keepdims=True) + jnp.sum(e_ctx, axis=-1, keepdims=True)
        o_ab = (_dot(e_lat.astype(bf16), vw) + _dot(e_ctx.astype(bf16), vc_ref[:, cs].astype(bf16))) / l
        outs.append(_merge_heads(o_ab))
    o_ref[...] = _rms(jnp.concatenate(outs, axis=-1), g_ref[...])


def _na_attention(q, k, v, kc, vc, bias, gnorm, nb, seq):
    rows = seq // GRID_W
    past = kc.shape[0] // nb
    _, mask = _na_tables()

    def bias_map(b, r):
        return (_na_row_start(r, rows) - r + NA_ROWS - 1, 0, 0, 0)

    return pl.pallas_call(
        functools.partial(_na_attn_kernel, rows=rows),
        out_shape=jax.ShapeDtypeStruct((nb * seq, D_ATT), f32),
        grid=(nb, rows),
        in_specs=[pl.BlockSpec((GRID_W, D_ATT), lambda b, r: (b * rows + r, 0)),
                  pl.BlockSpec((seq, D_ATT), lambda b, r: (b, 0)),
                  pl.BlockSpec((seq, D_ATT), lambda b, r: (b, 0)),
                  pl.BlockSpec((past, D_ATT), lambda b, r: (b, 0)),
                  pl.BlockSpec((past, D_ATT), lambda b, r: (b, 0)),
                  pl.BlockSpec((1, H_ATT, GRID_W, NA_ROWS * GRID_W), bias_map),
                  pl.BlockSpec((GRID_W, NA_ROWS * GRID_W), lambda b, r: (0, 0)),
                  pl.BlockSpec((1, D_ATT), lambda b, r: (0, 0))],
        out_specs=pl.BlockSpec((GRID_W, D_ATT), lambda b, r: (b * rows + r, 0)),
        scratch_shapes=[pltpu.VMEM((seq, D_ATT), bf16), pltpu.VMEM((seq, D_ATT), bf16)],
        compiler_params=_cparams(("arbitrary", "arbitrary")),
        name="na_attn",
    )(q, k, v, kc, vc, bias, jnp.asarray(mask), gnorm.reshape(1, D_ATT))


def _hy_front_kernel(x0_ref, x1_ref, v_ref, w0_ref, w1_ref, wv_ref, b0_ref, b1_ref, bv_ref,
                     zbf_ref, z_ref, x0c_ref):
    seq = x0_ref.shape[0]
    row = lax.broadcasted_iota(jnp.int32, x0_ref.shape, 0)
    first = row == 0
    last = row == seq - 1

    def conv(u_ref, w_ref, b_ref):
        u = u_ref[...]
        up = jnp.where(first, 0.0, pltpu.roll(u, 1, 0))
        un = jnp.where(last, 0.0, pltpu.roll(u, seq - 1, 0))
        y = b_ref[...] + up * w_ref[0:1, :]
        y = y + u * w_ref[1:2, :]
        return y + un * w_ref[2:3, :]

    z = conv(v_ref, wv_ref, bv_ref) * conv(x1_ref, w1_ref, b1_ref)
    z_ref[...] = z
    zbf_ref[...] = z.astype(bf16)
    x0c_ref[...] = conv(x0_ref, w0_ref, b0_ref)


def _hy_front(hy, conv_w, conv_b, nb, seq):
    tc = 256
    nc = D_HYENA // tc
    n = nb * D_HYENA
    cb = conv_b.reshape(1, 3 * D_HYENA)

    def part(k):
        return (pl.BlockSpec((seq, tc), lambda b, j: (b, k * nc + j)),
                pl.BlockSpec((SHORT_CONV, tc), lambda b, j: (0, k * nc + j)),
                pl.BlockSpec((1, tc), lambda b, j: (0, k * nc + j)))

    (x0s, w0s, b0s), (x1s, w1s, b1s), (vs, wvs, bvs) = part(0), part(1), part(2)
    ospec = pl.BlockSpec((seq, tc), lambda b, j: (0, b * nc + j))
    return pl.pallas_call(
        _hy_front_kernel,
        out_shape=(jax.ShapeDtypeStruct((seq, n), bf16),
                   jax.ShapeDtypeStruct((seq, n), f32),
                   jax.ShapeDtypeStruct((seq, n), f32)),
        grid=(nb, nc),
        in_specs=[x0s, x1s, vs, w0s, w1s, wvs, b0s, b1s, bvs],
        out_specs=(ospec, ospec, ospec),
        compiler_params=_cparams(("arbitrary", "arbitrary")),
        name="hyena_front",
    )(hy, hy, hy, conv_w, conv_w, conv_w, cb, cb, cb)


def _filter_features(seq):
    t = np.linspace(0.0, 1.0, seq, dtype=np.float64)[:, None]
    w = 2.0 * math.pi * np.arange(seq, dtype=np.float64)[:, None] / seq
    fb = np.linspace(1e-4, FILTER_BANDS - 1, FILTER_BANDS, dtype=np.float64)[None, :]
    ang = fb * w
    z = np.concatenate([t, np.cos(ang), -np.sin(ang)], axis=-1).astype(np.float32)
    return np.pad(z, ((0, 0), (0, LANES - EMB_DIM)))


def _filt_kernel(zf_ref, w1_ref, b1_ref, fr_ref, w2_ref, b2_ref, w3_ref, dl_ref, h_ref, kl_ref):
    i = pl.program_id(0)
    tr = zf_ref.shape[0]
    zf = zf_ref[...]
    fr = fr_ref[...]
    h = jnp.sin(fr * (_dot_hi(zf, w1_ref[...]) + b1_ref[...]))
    h = jnp.sin(fr * (_dot_hi(h, w2_ref[...]) + b2_ref[...]))
    h = _dot_hi(h, w3_ref[...])
    decay = jnp.exp(-zf[:, 0:1] * dl_ref[...])
    row = lax.broadcasted_iota(jnp.int32, (tr, D_HYENA), 0) + i * tr
    hf = h[:, :D_HYENA] * decay
    hb = jnp.where(row == 0, 0.0, h[:, D_HYENA:] * decay)
    h_ref[:, :D_HYENA] = hf.astype(bf16)
    h_ref[:, D_HYENA:] = hb.astype(bf16)
    alt = (1 - 2 * (row & 1)).astype(f32)
    part = jnp.sum(alt * (hf + hb), axis=0, keepdims=True)

    @pl.when(i == 0)
    def _():
        kl_ref[...] = jnp.zeros_like(kl_ref)

    kl_ref[...] += jnp.broadcast_to(part, kl_ref.shape)


def _hy_filters(seq, w1, b1, w2, b2, w3, freq):
    tr = 256
    zf = jnp.asarray(_filter_features(seq))
    deltas = np.abs(np.linspace(MIN_DECAY, MAX_DECAY, D_HYENA, dtype=np.float64))[None, :].astype(np.float32)
    w1p = jnp.pad(w1, ((0, LANES - EMB_DIM), (0, 0)))
    const = lambda shape: pl.BlockSpec(shape, lambda i: (0, 0))
    return pl.pallas_call(
        _filt_kernel,
        out_shape=(jax.ShapeDtypeStruct((seq, 2 * D_HYENA), bf16),
                   jax.ShapeDtypeStruct((SUBLANES, D_HYENA), f32)),
        grid=(seq // tr,),
        in_specs=[pl.BlockSpec((tr, LANES), lambda i: (i, 0)),
                  const((LANES, FILTER_FF)), const((1, FILTER_FF)), const((1, FILTER_FF)),
                  const((FILTER_FF, FILTER_FF)), const((1, FILTER_FF)),
                  const((FILTER_FF, 2 * D_HYENA)), const((1, D_HYENA))],
        out_specs=(pl.BlockSpec((tr, 2 * D_HYENA), lambda i: (i, 0)),
                   pl.BlockSpec((SUBLANES, D_HYENA), lambda i: (0, 0))),
        compiler_params=_cparams(("arbitrary",)),
        name="hyena_filters",
    )(zf, w1p, b1.reshape(1, -1), freq.reshape(1, -1), w2, b2.reshape(1, -1), w3, jnp.asarray(deltas))


def _dft_mats(seq):
    n = 2 * seq
    ph = (np.arange(seq, dtype=np.int64)[:, None] * np.arange(seq, dtype=np.int64)[None, :]) % n
    ang = ph.astype(np.float64) * (2.0 * math.pi / n)
    return np.cos(ang).astype(np.float32), np.sin(ang).astype(np.float32)


def _alt_col(rows, offset):
    row = lax.broadcasted_iota(jnp.int32, (rows, 1), 0) + offset
    return (1 - 2 * (row & 1)).astype(f32)


def _hy_fwd_kernel(fr_ref, fi_ref, z_ref, h_ref, kl_ref, yr_ref, yi_ref, yl_ref, kr_s, ki_s, *, n):
    i = pl.program_id(0)
    j = pl.program_id(1)
    tf = fr_ref.shape[0]
    tn = z_ref.shape[1]
    frb = fr_ref[...].astype(bf16)
    fib = fi_ref[...].astype(bf16)

    @pl.when(j == 0)
    def _():
        ah = _dot(frb, h_ref[...])
        bh = _dot(fib, h_ref[...])
        f = lax.broadcasted_iota(jnp.int32, (tf, 1), 0) + i * tf
        cf = jnp.where(f == 0, 1.0 / n, 2.0 / n)
        kr_s[...] = (ah[:, :D_HYENA] + ah[:, D_HYENA:]) * cf
        ki_s[...] = (bh[:, D_HYENA:] - bh[:, :D_HYENA]) * cf

    a = _dot(frb, z_ref[...])
    b = _dot(fib, z_ref[...])
    kr = kr_s[...]
    ki = ki_s[...]
    for c in range(tn // D_HYENA):
        cs = slice(c * D_HYENA, (c + 1) * D_HYENA)
        yr_ref[:, cs] = (a[:, cs] * kr + b[:, cs] * ki).astype(bf16)
        yi_ref[:, cs] = (b[:, cs] * kr - a[:, cs] * ki).astype(bf16)

    @pl.when(i == 0)
    def _():
        alt = _alt_col(z_ref.shape[0], 0)
        nz = jnp.sum(z_ref[...].astype(f32) * alt, axis=0, keepdims=True)
        kl = jnp.concatenate([kl_ref[0:1, :]] * (tn // D_HYENA), axis=-1)
        yl_ref[...] = jnp.broadcast_to(nz * kl * (1.0 / n), yl_ref.shape)


def _hy_fwd(fr, fi, zbf, hcat, kl, seq):
    n_cols = zbf.shape[1]
    tf = 256
    tn = min(n_cols, 1024)
    ni, nj = seq // tf, n_cols // tn
    assert ni == 1 or nj == 1
    return pl.pallas_call(
        functools.partial(_hy_fwd_kernel, n=2 * seq),
        out_shape=(jax.ShapeDtypeStruct((seq, n_cols), bf16),
                   jax.ShapeDtypeStruct((seq, n_cols), bf16),
                   jax.ShapeDtypeStruct((SUBLANES, n_cols), f32)),
        grid=(ni, nj),
        in_specs=[pl.BlockSpec((tf, seq), lambda i, j: (i, 0)),
                  pl.BlockSpec((tf, seq), lambda i, j: (i, 0)),
                  pl.BlockSpec((seq, tn), lambda i, j: (0, j)),
                  pl.BlockSpec((seq, 2 * D_HYENA), lambda i, j: (0, 0)),
                  pl.BlockSpec((SUBLANES, D_HYENA), lambda i, j: (0, 0))],
        out_specs=(pl.BlockSpec((tf, tn), lambda i, j: (i, j)),
                   pl.BlockSpec((tf, tn), lambda i, j: (i, j)),
                   pl.BlockSpec((SUBLANES, tn), lambda i, j: (0, j))),
        scratch_shapes=[pltpu.VMEM((tf, D_HYENA), f32), pltpu.VMEM((tf, D_HYENA), f32)],
        compiler_params=_cparams(("arbitrary", "arbitrary")),
        name="hyena_dft_fwd",
    )(fr, fi, zbf, hcat, kl)


def _hy_inv_kernel(fr_ref, fi_ref, yr_ref, yi_ref, yl_ref, z_ref, x0_ref, skip_ref, g_ref, o_ref):
    tt = fr_ref.shape[0]
    tn = yr_ref.shape[1]
    y = _dot(fr_ref[...].astype(bf16), yr_ref[...]) + _dot(fi_ref[...].astype(bf16), yi_ref[...])
    alt = _alt_col(tt, pl.program_id(0) * tt)
    for c in range(tn // D_HYENA):
        cs = slice(c * D_HYENA, (c + 1) * D_HYENA)
        yc = y[:, cs] + alt * yl_ref[0:1, cs] + z_ref[:, cs] * skip_ref[...]
        o_ref[:, cs] = _rms(yc * x0_ref[:, cs], g_ref[...])


def _hy_inv(fr, fi, yr, yi, yl, z, x0c, skip, gnorm, seq):
    n_cols = z.shape[1]
    tt = 256
    tn = min(n_cols, 1024)
    blk = pl.BlockSpec((tt, tn), lambda i, j: (i, j))
    return pl.pallas_call(
        _hy_inv_kernel,
        out_shape=jax.ShapeDtypeStruct((seq, n_cols), f32),
        grid=(seq // tt, n_cols // tn),
        in_specs=[pl.BlockSpec((tt, seq), lambda i, j: (i, 0)),
                  pl.BlockSpec((tt, seq), lambda i, j: (i, 0)),
                  pl.BlockSpec((seq, tn), lambda i, j: (0, j)),
                  pl.BlockSpec((seq, tn), lambda i, j: (0, j)),
                  pl.BlockSpec((SUBLANES, tn), lambda i, j: (0, j)),
                  blk, blk,
                  pl.BlockSpec((1, D_HYENA), lambda i, j: (0, 0)),
                  pl.BlockSpec((1, D_HYENA), lambda i, j: (0, 0))],
        out_specs=blk,
        compiler_params=_cparams(("arbitrary", "arbitrary")),
        name="hyena_dft_inv",
    )(fr, fi, yr, yi, yl, z, x0c, skip.reshape(1, D_HYENA), gnorm.reshape(1, D_HYENA))


def _hyena(hy, lp, nb, seq):
    fr_np, fi_np = _dft_mats(seq)
    fr, fi = jnp.asarray(fr_np), jnp.asarray(fi_np)
    zbf, z, x0c = _hy_front(hy, lp['conv_w'], lp['conv_b'], nb, seq)
    hcat, kl = _hy_filters(seq, lp['filt_w1'], lp['filt_b1'], lp['filt_w2'], lp['filt_b2'],
                           lp['filt_w3'], lp['filt_freq'])
    yr, yi, yl = _hy_fwd(fr, fi, zbf, hcat, kl, seq)
    return _hy_inv(fr, fi, yr, yi, yl, z, x0c, lp['hyena_skip'], lp['gnorm_hyena'], seq)


def _store_token_tiles(ref, x):
    m = x.shape[0]
    for c in range(D_MODEL // LANES):
        ref[pl.ds(c, m, stride=SUBLANES), :] = x[:, c * LANES:(c + 1) * LANES]


def _load_token_tiles(ref, m, lead=()):
    return jnp.concatenate([ref[lead + (pl.ds(c, m, stride=SUBLANES), slice(None))]
                            for c in range(D_MODEL // LANES)], axis=-1)


def _route(logits):
    lane_i = lax.broadcasted_iota(jnp.int32, logits.shape, 1)
    lane = lane_i.astype(f32)
    big = float(ROUTE_LANES)
    is_g = lane_i < N_GROUPS
    mg = jnp.max(jnp.where(is_g, logits, -jnp.inf), axis=-1, keepdims=True)
    sg = jnp.sum(jnp.where(is_g, jnp.exp(logits - mg), 0.0), axis=-1, keepdims=True)
    g_w = 1.0 / sg
    g_idx = jnp.min(jnp.where(is_g & (logits == mg), lane, big), axis=-1, keepdims=True)
    e_id = lane_i - ROUTE_EXP_LANE0
    sel = (e_id >= 0) & (e_id < N_EXPERTS) & ((e_id >> 2).astype(f32) == g_idx)
    me = jnp.max(jnp.where(sel, logits, -jnp.inf), axis=-1, keepdims=True)
    ee = jnp.where(sel, jnp.exp(logits - me), 0.0)
    prob = ee / jnp.sum(ee, axis=-1, keepdims=True)
    p1 = jnp.max(jnp.where(sel, prob, -1.0), axis=-1, keepdims=True)
    i1 = jnp.min(jnp.where(sel & (prob == p1), lane, big), axis=-1, keepdims=True)
    sel2 = sel & (lane != i1)
    p2 = jnp.max(jnp.where(sel2, prob, -1.0), axis=-1, keepdims=True)
    i2 = jnp.min(jnp.where(sel2 & (prob == p2), lane, big), axis=-1, keepdims=True)
    tot = p1 + p2
    rec = jnp.where(lane_i == ROUTE_E1, i1 - ROUTE_EXP_LANE0, 0.0)
    rec = jnp.where(lane_i == ROUTE_E2, i2 - ROUTE_EXP_LANE0, rec)
    rec = jnp.where(lane_i == ROUTE_W1, g_w * (p1 / tot), rec)
    return jnp.where(lane_i == ROUTE_W2, g_w * (p2 / tot), rec)


def _outproj_kernel(xp_ref, xs_ref, attp_ref, atts_ref, hyp_ref, hys_ref, mod_ref, wo_ref, g2_ref,
                    wr_ref, br_ref, x1_ref, tok_ref, rt_ref, wobf_ref, *, prompt_tiles):
    i = pl.program_id(0)
    tm = xp_ref.shape[0]

    @pl.when(i == 0)
    def _():
        _cast_rows(wo_ref, wobf_ref, 128)

    is_p = i < prompt_tiles
    x = jnp.where(is_p, xp_ref[...], xs_ref[...])
    att = jnp.where(is_p, attp_ref[...], atts_ref[...])
    hyo = jnp.where(is_p, hyp_ref[...], hys_ref[...])
    proj = (_dot(att.astype(bf16), wobf_ref[0:D_ATT, :]) + _dot(hyo.astype(bf16), wobf_ref[D_ATT:, :]))
    x1 = x + mod_ref[0, GT1:GT1 + 1, :] * proj
    x1_ref[...] = x1
    h2 = _rms(x1, g2_ref[...]) * (1.0 + mod_ref[0, SC2:SC2 + 1, :]) + mod_ref[0, SH2:SH2 + 1, :]
    rt_ref[...] = _route(_dot_hi(h2, wr_ref[...]) + br_ref[...])
    _store_token_tiles(tok_ref, h2)


def _outproj(xp, xs, attp, atts, hyp, hys, mod, w_out, norm2_g, wr, br, seq_p, seq_s):
    tm = 256
    tp, ts = xp.shape[0], xs.shape[0]
    npt, nst = tp // tm, ts // tm
    assert seq_p == tm and seq_s % tm == 0
    spb = seq_s // tm
    p_idx = lambda i: jnp.minimum(i, npt - 1)
    s_idx = lambda i: jnp.maximum(i - npt, 0)
    const = lambda shape: pl.BlockSpec(shape, lambda i: (0,) * len(shape))
    return pl.pallas_call(
        functools.partial(_outproj_kernel, prompt_tiles=npt),
        out_shape=(jax.ShapeDtypeStruct((tp + ts, D_MODEL), f32),
                   jax.ShapeDtypeStruct(((tp + ts) * SUBLANES, LANES), f32),
                   jax.ShapeDtypeStruct((tp + ts, ROUTE_LANES), f32)),
        grid=(npt + nst,),
        in_specs=[pl.BlockSpec((tm, D_MODEL), lambda i: (p_idx(i), 0)),
                  pl.BlockSpec((tm, D_MODEL), lambda i: (s_idx(i), 0)),
                  pl.BlockSpec((tm, D_ATT), lambda i: (p_idx(i), 0)),
                  pl.BlockSpec((tm, D_ATT), lambda i: (s_idx(i), 0)),
                  pl.BlockSpec((tm, D_HYENA), lambda i: (0, p_idx(i))),
                  pl.BlockSpec((tm, D_HYENA), lambda i: (s_idx(i) % spb, s_idx(i) // spb)),
                  pl.BlockSpec((1, MOD_ROWS, D_MODEL),
                               lambda i: (jnp.where(i < npt, 0, 1 + s_idx(i) // spb), 0, 0)),
                  const((D_MODEL, D_MODEL)), const((1, D_MODEL)),
                  const((D_MODEL, ROUTE_LANES)), const((1, ROUTE_LANES))],
        out_specs=(pl.BlockSpec((tm, D_MODEL), lambda i: (i, 0)),
                   pl.BlockSpec((tm * SUBLANES, LANES), lambda i: (i, 0)),
                   pl.BlockSpec((tm, ROUTE_LANES), lambda i: (i, 0))),
        scratch_shapes=[pltpu.VMEM((D_MODEL, D_MODEL), bf16)],
        compiler_params=_cparams(("arbitrary",)),
        name="outproj_router",
    )(xp, xs, attp, atts, hyp, hys, mod, w_out, norm2_g.reshape(1, D_MODEL), wr, br)


def _route_plan(e12, tm, max_items):
    i32 = jnp.int32
    experts = jnp.arange(N_EXPERTS, dtype=i32)
    oh = (e12[:, None] == experts[None, :]).astype(i32)
    cs = jnp.cumsum(oh, axis=0)
    counts = cs[-1]
    rank = jnp.sum(oh * (cs - 1), axis=1)
    end = jnp.cumsum(counts)
    off = end - counts
    pos = jnp.sum(oh * off[None, :], axis=1) + rank
    first_tile = off // tm
    n_tiles = jnp.where(counts > 0, (end - 1) // tm - first_tile + 1, 0)
    iend = jnp.cumsum(n_tiles)
    istart = iend - n_tiles
    n_items = iend[-1]
    items = jnp.minimum(jnp.arange(max_items, dtype=i32), n_items - 1)
    it_exp = jnp.minimum(jnp.sum((items[:, None] >= iend[None, :]).astype(i32), axis=1), N_EXPERTS - 1)
    sel = (it_exp[:, None] == experts[None, :]).astype(i32)
    pick = lambda v: jnp.sum(sel * v[None, :], axis=1)
    k = items - pick(istart)
    it_tile = pick(first_tile) + k
    it_lo = jnp.maximum(pick(off) - it_tile * tm, 0)
    it_hi = jnp.minimum(pick(end) - it_tile * tm, tm)
    flags = (k == 0).astype(i32) + 2 * (it_lo == 0).astype(i32) + 4 * (it_hi == tm).astype(i32)
    return dict(pos=pos, n_items=n_items.reshape(1), it_tile=it_tile, it_exp=it_exp, it_lo=it_lo,
                it_hi=it_hi, it_flags=flags)


FLAG_NEW_EXPERT, FLAG_TILE_START, FLAG_TILE_END = 1, 2, 4


ROW_DMA_UNROLL = 8


def _row_dma_loop(n, make_copy):
    assert n % ROW_DMA_UNROLL == 0

    def body(i, c):
        for u in range(ROW_DMA_UNROLL):
            make_copy(i * ROW_DMA_UNROLL + u).start(priority=u % DMA_PRIORITIES)
        return c

    lax.fori_loop(0, n // ROW_DMA_UNROLL, body, 0)


def _moe_ffn_kernel(pos_ref, it_tile, it_exp, it_lo, it_hi, it_flags, n_items, tok_hbm, wg_ref, wu_ref, wd_ref,
                    ab_hbm, src_s, xbuf, ybuf, x_s, acc_s, wgb_s, wub_s, wdb_s, sem_in, sem_out,
                    *, chunk, n_tok, n_tiles):
    i = pl.program_id(0)
    tm = x_s.shape[0]

    def gather_tile(k):
        base = k * tm
        slot = k % 2
        _row_dma_loop(tm, lambda r: pltpu.make_async_copy(
            tok_hbm.at[src_s[base + r] & (n_tok - 1)],
            xbuf.at[slot, pl.ds(pl.multiple_of(r * SUBLANES, SUBLANES), SUBLANES), :],
            sem_in.at[slot]))

    def scatter_tile(k):
        base = k * tm
        slot = k % 2
        _row_dma_loop(tm, lambda r: pltpu.make_async_copy(
            ybuf.at[slot, pl.ds(pl.multiple_of(r * SUBLANES, SUBLANES), SUBLANES), :],
            ab_hbm.at[src_s[base + r]],
            sem_out.at[slot]))

    def wait_all(buf, sem, slot):
        pltpu.make_async_copy(buf.at[slot], buf.at[slot], sem.at[slot]).wait()

    @pl.when(i == 0)
    def _():
        def inv(a, c):
            for u in range(ROW_DMA_UNROLL):
                src_s[pos_ref[a * ROW_DMA_UNROLL + u]] = a * ROW_DMA_UNROLL + u
            return c

        lax.fori_loop(0, pos_ref.shape[0] // ROW_DMA_UNROLL, inv, 0)
        gather_tile(0)

    @pl.when(i < n_items[0])
    def _():
        flags = it_flags[i]
        k = it_tile[i]

        @pl.when((flags & FLAG_NEW_EXPERT) != 0)
        def _():
            wgb_s[...] = wg_ref[0].astype(bf16)
            wub_s[...] = wu_ref[0].astype(bf16)
            wdb_s[...] = wd_ref[0].astype(bf16)

        @pl.when((flags & FLAG_TILE_START) != 0)
        def _():
            @pl.when(k + 1 < n_tiles)
            def _():
                gather_tile(k + 1)

            wait_all(xbuf, sem_in, k % 2)
            x_s[...] = _load_token_tiles(xbuf, tm, (k % 2,)).astype(bf16)
            acc_s[...] = jnp.zeros_like(acc_s)

        lo = it_lo[i]
        hi = it_hi[i]

        def body(k, c):
            r = pl.multiple_of(k * chunk, chunk)

            @pl.when((r < hi) & (r + chunk > lo))
            def _():
                x = x_s[pl.ds(r, chunk), :]
                g = _dot(x, wgb_s[...])
                u = _dot(x, wub_s[...])
                hid = (g * (1.0 / (1.0 + jnp.exp(-g)))) * u
                y = _dot(hid.astype(bf16), wdb_s[...])
                row = lax.broadcasted_iota(jnp.int32, (chunk, 1), 0) + r
                mine = (row >= lo) & (row < hi)
                acc_s[pl.ds(r, chunk), :] = jnp.where(mine, y, acc_s[pl.ds(r, chunk), :])

            return c

        lax.fori_loop(0, tm // chunk, body, 0)

        @pl.when((flags & FLAG_TILE_END) != 0)
        def _():
            @pl.when(k >= 2)
            def _():
                wait_all(ybuf, sem_out, k % 2)

            _store_token_tiles(ybuf.at[k % 2], acc_s[...])
            scatter_tile(k)

            @pl.when(k == n_tiles - 1)
            def _():
                if n_tiles > 1:
                    wait_all(ybuf, sem_out, (n_tiles - 2) % 2)
                wait_all(ybuf, sem_out, (n_tiles - 1) % 2)


def _moe_ffn(plan, tok, w_gate, w_up, w_down, tm, max_items):
    n_tok = tok.shape[0] // SUBLANES
    n_rows = plan['pos'].shape[0]
    n_tiles = n_rows // tm
    assert n_tok & (n_tok - 1) == 0
    tok3 = tok.reshape(n_tok, SUBLANES, LANES)
    wspec = lambda shape: pl.BlockSpec((1,) + shape, lambda i, ps, tl, ex, lo, hi, fl, n: (ex[i], 0, 0))
    return pl.pallas_call(
        functools.partial(_moe_ffn_kernel, chunk=256, n_tok=n_tok, n_tiles=n_tiles),
        out_shape=jax.ShapeDtypeStruct((n_rows, SUBLANES, LANES), f32),
        grid_spec=pltpu.PrefetchScalarGridSpec(
            num_scalar_prefetch=7,
            grid=(max_items,),
            in_specs=[pl.BlockSpec(memory_space=pl.ANY),
                      wspec((D_MODEL, D_EXPERT)), wspec((D_MODEL, D_EXPERT)), wspec((D_EXPERT, D_MODEL))],
            out_specs=pl.BlockSpec(memory_space=pl.ANY),
            scratch_shapes=[pltpu.SMEM((n_rows,), jnp.int32),
                            pltpu.VMEM((2, tm * SUBLANES, LANES), f32),
                            pltpu.VMEM((2, tm * SUBLANES, LANES), f32),
                            pltpu.VMEM((tm, D_MODEL), bf16), pltpu.VMEM((tm, D_MODEL), f32),
                            pltpu.VMEM((D_MODEL, D_EXPERT), bf16), pltpu.VMEM((D_MODEL, D_EXPERT), bf16),
                            pltpu.VMEM((D_EXPERT, D_MODEL), bf16),
                            pltpu.SemaphoreType.DMA((2,)), pltpu.SemaphoreType.DMA((2,))]),
        compiler_params=_cparams(("arbitrary",)),
        name="moe_ffn",
    )(plan['pos'], plan['it_tile'], plan['it_exp'], plan['it_lo'], plan['it_hi'], plan['it_flags'],
      plan['n_items'], tok3, w_gate, w_up, w_down)


def _final_kernel(a_ref, b_ref, x1_ref, rt_ref, mod_ref, fg_ref, yp_ref, yl_ref, *, prompt_tiles):
    i = pl.program_id(0)
    tm = x1_ref.shape[0]
    rt = rt_ref[...]
    moe = (rt[:, ROUTE_W1:ROUTE_W1 + 1] * _load_token_tiles(a_ref, tm)
           + rt[:, ROUTE_W2:ROUTE_W2 + 1] * _load_token_tiles(b_ref, tm))
    y = _rms(x1_ref[...] + mod_ref[0, GT2:GT2 + 1, :] * moe, fg_ref[...])

    @pl.when(i < prompt_tiles)
    def _():
        yp_ref[...] = y

    @pl.when(i >= prompt_tiles)
    def _():
        yl_ref[...] = y


def _final(ab, x1, rt, mod, final_g, t_prompt, t_lat, seq_s):
    tm = 256
    npt, nst = t_prompt // tm, t_lat // tm
    spb = seq_s // tm
    ab2 = ab.reshape(ab.shape[0] * SUBLANES, LANES)
    return pl.pallas_call(
        functools.partial(_final_kernel, prompt_tiles=npt),
        out_shape=(jax.ShapeDtypeStruct((t_prompt, D_MODEL), f32),
                   jax.ShapeDtypeStruct((t_lat, D_MODEL), f32)),
        grid=(npt + nst,),
        in_specs=[pl.BlockSpec((tm * SUBLANES, LANES), lambda i: (i, 0)),
                  pl.BlockSpec((tm * SUBLANES, LANES), lambda i: (npt + nst + i, 0)),
                  pl.BlockSpec((tm, D_MODEL), lambda i: (i, 0)),
                  pl.BlockSpec((tm, ROUTE_LANES), lambda i: (i, 0)),
                  pl.BlockSpec((1, MOD_ROWS, D_MODEL),
                               lambda i: (jnp.where(i < npt, 0, 1 + jnp.maximum(i - npt, 0) // spb), 0, 0)),
                  pl.BlockSpec((1, D_MODEL), lambda i: (0, 0))],
        out_specs=(pl.BlockSpec((tm, D_MODEL), lambda i: (jnp.minimum(i, npt - 1), 0)),
                   pl.BlockSpec((tm, D_MODEL), lambda i: (jnp.maximum(i - npt, 0), 0))),
        compiler_params=_cparams(("arbitrary",)),
        name="moe_combine_final",
    )(ab2, ab2, x1, rt, mod, final_g.reshape(1, D_MODEL))


def _moe(tok, rt, x1, mod, w_gate, w_up, w_down, final_g, t_prompt, t_lat, seq_s):
    tm = MOE_TM
    n_rows = 2 * (t_prompt + t_lat)
    assert n_rows % tm == 0
    max_items = n_rows // tm + N_EXPERTS
    e12 = jnp.concatenate([rt[:, ROUTE_E1], rt[:, ROUTE_E2]]).astype(jnp.int32)
    plan = _route_plan(e12, tm, max_items)
    ab = _moe_ffn(plan, tok, w_gate, w_up, w_down, tm, max_items)
    return _final(ab, x1, rt, mod, final_g, t_prompt, t_lat, seq_s)


def kernel(x_prompt, x_sample, cache_k, cache_v, c, c_ctx, w_ada, b_ada, norm1_g, w_in, rpb, conv_w, conv_b, filt_w1, filt_b1, filt_w2, filt_b2, filt_w3, filt_freq, hyena_skip, gnorm_att, gnorm_hyena, w_out, norm2_g, router_grp_w, router_grp_b, router_exp_w, router_exp_b, w_gate, w_up, w_down, final_g):
    depth = w_ada.shape[0]
    assert depth == 1
    batch, seq, _ = x_prompt.shape
    dec_batch, dec_seq, _ = x_sample.shape
    l = 0

    wr = jnp.zeros((D_MODEL, ROUTE_LANES), f32)
    wr = wr.at[:, :N_GROUPS].set(router_grp_w[l])
    wr = wr.at[:, ROUTE_EXP_LANE0:ROUTE_EXP_LANE0 + N_EXPERTS].set(router_exp_w[l])
    br = jnp.zeros((1, ROUTE_LANES), f32)
    br = br.at[0, :N_GROUPS].set(router_grp_b[l])
    br = br.at[0, ROUTE_EXP_LANE0:ROUTE_EXP_LANE0 + N_EXPERTS].set(router_exp_b[l])

    lp = {
        'norm1_g': norm1_g[l], 'w_in': w_in[l], 'conv_w': conv_w[l], 'conv_b': conv_b[l],
        'filt_w1': filt_w1[l], 'filt_b1': filt_b1[l], 'filt_w2': filt_w2[l], 'filt_b2': filt_b2[l],
        'filt_w3': filt_w3[l], 'filt_freq': filt_freq[l], 'hyena_skip': hyena_skip[l],
        'gnorm_hyena': gnorm_hyena[l], 'w_out': w_out[l], 'norm2_g': norm2_g[l],
        'wr': wr, 'br': br, 'w_gate': w_gate[l], 'w_up': w_up[l], 'w_down': w_down[l],
    }

    cond8 = jnp.zeros((SUBLANES, D_MODEL), f32).at[0].set(c_ctx).at[1:1 + dec_batch].set(c)
    mod = _ada_mod(cond8, w_ada[l], b_ada[l]).reshape(SUBLANES, N_MOD, D_MODEL)
    mod = jnp.pad(mod, ((0, 0), (0, MOD_ROWS - N_MOD), (0, 0)))
    mod_ctx, mod_lat = mod[0:1], mod[1:1 + dec_batch]

    xp = x_prompt.reshape(batch * seq, D_MODEL)
    xs = x_sample.reshape(dec_batch * dec_seq, D_MODEL)

    qp, k_ctx, v_ctx, hyp = _inproj(xp, mod_ctx, lp['norm1_g'], lp['w_in'], batch * seq)
    attp = _ctx_attention(qp, k_ctx, v_ctx, gnorm_att[l], seq)
    hyop = _hyena(hyp, lp, batch, seq)

    ql, kl, vl, hyl = _inproj(xs, mod_lat, lp['norm1_g'], lp['w_in'], dec_seq)
    bias = _na_bias(rpb[l])
    kc = cache_k[:, l].reshape(dec_batch * cache_k.shape[2], D_ATT)
    vc = cache_v[:, l].reshape(dec_batch * cache_v.shape[2], D_ATT)
    attl = _na_attention(ql, kl, vl, kc, vc, bias, gnorm_att[l], dec_batch, dec_seq)
    hyol = _hyena(hyl, lp, dec_batch, dec_seq)

    x1, tok, rt = _outproj(xp, xs, attp, attl, hyop, hyol, mod[0:1 + dec_batch], lp['w_out'], lp['norm2_g'],
                           lp['wr'], lp['br'], seq, dec_seq)
    yp, ys = _moe(tok, rt, x1, mod[0:1 + dec_batch], lp['w_gate'], lp['w_up'], lp['w_down'], final_g,
                  batch * seq, dec_batch * dec_seq, dec_seq)

    y_prompt = yp.reshape(batch, seq, D_MODEL)
    y_sample = ys.reshape(dec_batch, dec_seq, D_MODEL)
    new_k = k_ctx.reshape(batch, 1, seq, H_ATT, HEAD_DIM)
    new_v = v_ctx.reshape(batch, 1, seq, H_ATT, HEAD_DIM)
    return (y_prompt, y_sample, new_k, new_v)
```

```python
import functools
import math

import jax
import jax.numpy as jnp
import numpy as np
from jax import lax
from jax.experimental import pallas as pl
from jax.experimental.pallas import tpu as pltpu

f32 = jnp.float32
bf16 = jnp.bfloat16
HIGHEST = lax.Precision.HIGHEST

D_MODEL = 1024
GRID_W = 64
H_ATT = 8
HEAD_DIM = 64
D_ATT = H_ATT * HEAD_DIM
D_HYENA = 512
D_IN = 3 * D_ATT + 3 * D_HYENA
NA_ROWS = 8
NA_COLS = 16
SHORT_CONV = 3
FILTER_BANDS = 16
EMB_DIM = 1 + 2 * FILTER_BANDS
FILTER_FF = 64
DECAY_TARGET = 1e-2
MIN_DECAY = math.log(DECAY_TARGET) / 1.5
MAX_DECAY = math.log(DECAY_TARGET) / 0.3
N_GROUPS = 4
EXPERTS_PER_GROUP = 4
N_EXPERTS = N_GROUPS * EXPERTS_PER_GROUP
D_EXPERT = 512
N_MOD = 6
EPS = 1e-6
NEG_INF = -1e30
ATT_SCALE = HEAD_DIM ** -0.5

LANES = 128
SUBLANES = 8
MOD_ROWS = 8
ROUTE_LANES = 128
ROUTE_EXP_LANE0 = 16
ROUTE_E1, ROUTE_E2, ROUTE_W1, ROUTE_W2 = 0, 1, 2, 3
MOE_TM = 512
GATHER_PRIORITIES = (0,)
SCATTER_PRIORITIES = (1,)
VMEM_LIMIT = 56 * 1024 * 1024

SH1, SC1, GT1, SH2, SC2, GT2 = range(6)


def _cparams(sem, vmem=VMEM_LIMIT):
    return pltpu.CompilerParams(dimension_semantics=sem, vmem_limit_bytes=vmem)


def _dot(a, b):
    return jnp.dot(a, b, preferred_element_type=f32)


def _dot_hi(a, b):
    return lax.dot_general(a, b, (((1,), (0,)), ((), ())), precision=HIGHEST,
                           preferred_element_type=f32)


def _dot_nt(a, b):
    return lax.dot_general(a, b, (((1,), (1,)), ((), ())), preferred_element_type=f32)


def _rms(x, g):
    ms = jnp.mean(x * x, axis=-1, keepdims=True)
    return x * lax.rsqrt(ms + EPS) * g


def _cast_rows(src_ref, dst_ref, chunk):
    n = src_ref.shape[0] // chunk

    def body(i, c):
        r = pl.multiple_of(i * chunk, chunk)
        dst_ref[pl.ds(r, chunk), :] = src_ref[pl.ds(r, chunk), :].astype(dst_ref.dtype)
        return c

    lax.fori_loop(0, n, body, 0)


def _ada_kernel(c_ref, w_ref, b_ref, o_ref):
    c = c_ref[...]
    s = c * (1.0 / (1.0 + jnp.exp(-c)))
    o_ref[...] = _dot_hi(s, w_ref[...]) + b_ref[...]


def _ada_mod(cond8, w_ada, b_ada):
    tn = 1536
    n = N_MOD * D_MODEL
    return pl.pallas_call(
        _ada_kernel,
        out_shape=jax.ShapeDtypeStruct((SUBLANES, n), f32),
        grid=(n // tn,),
        in_specs=[pl.BlockSpec((SUBLANES, D_MODEL), lambda j: (0, 0)),
                  pl.BlockSpec((D_MODEL, tn), lambda j: (0, j)),
                  pl.BlockSpec((1, tn), lambda j: (0, j))],
        out_specs=pl.BlockSpec((SUBLANES, tn), lambda j: (0, j)),
        compiler_params=_cparams(("arbitrary",)),
        name="ada_mod",
    )(cond8, w_ada, b_ada.reshape(1, n))


def _inproj_kernel(x_ref, mod_ref, g_ref, w_ref, q_ref, k_ref, v_ref, hy_ref, wbf_ref):
    @pl.when(pl.program_id(0) == 0)
    def _():
        _cast_rows(w_ref, wbf_ref, 128)

    h = _rms(x_ref[...], g_ref[...])
    h = h * (1.0 + mod_ref[0, SC1:SC1 + 1, :]) + mod_ref[0, SH1:SH1 + 1, :]
    p = _dot(h.astype(bf16), wbf_ref[...])
    q_ref[...] = p[:, 0:D_ATT]
    k_ref[...] = p[:, D_ATT:2 * D_ATT]
    v_ref[...] = p[:, 2 * D_ATT:3 * D_ATT]
    hy_ref[...] = p[:, 3 * D_ATT:]


def _inproj(x, mod, norm_g, w_in, rows_per_mod):
    t = x.shape[0]
    tm = 512
    blocks_per_mod = rows_per_mod // tm
    return pl.pallas_call(
        _inproj_kernel,
        out_shape=(jax.ShapeDtypeStruct((t, D_ATT), f32),
                   jax.ShapeDtypeStruct((t, D_ATT), f32),
                   jax.ShapeDtypeStruct((t, D_ATT), f32),
                   jax.ShapeDtypeStruct((t, 3 * D_HYENA), f32)),
        grid=(t // tm,),
        in_specs=[pl.BlockSpec((tm, D_MODEL), lambda i: (i, 0)),
                  pl.BlockSpec((1, MOD_ROWS, D_MODEL), lambda i: (i // blocks_per_mod, 0, 0)),
                  pl.BlockSpec((1, D_MODEL), lambda i: (0, 0)),
                  pl.BlockSpec((D_MODEL, D_IN), lambda i: (0, 0), pipeline_mode=pl.Buffered(1))],
        out_specs=(pl.BlockSpec((tm, D_ATT), lambda i: (i, 0)),
                   pl.BlockSpec((tm, D_ATT), lambda i: (i, 0)),
                   pl.BlockSpec((tm, D_ATT), lambda i: (i, 0)),
                   pl.BlockSpec((tm, 3 * D_HYENA), lambda i: (i, 0))),
        scratch_shapes=[pltpu.VMEM((D_MODEL, D_IN), bf16)],
        compiler_params=_cparams(("arbitrary",)),
        name="inproj",
    )(x, mod, norm_g.reshape(1, D_MODEL), w_in)


def _split_heads(q2):
    lane = lax.broadcasted_iota(jnp.int32, q2.shape, 1)
    qa = jnp.where(lane < HEAD_DIM, q2, 0.0)
    qb = jnp.where(lane >= HEAD_DIM, q2, 0.0)
    return jnp.concatenate([qa, qb], axis=0)


def _merge_heads(o_ab):
    m = o_ab.shape[0] // 2
    lane = lax.broadcasted_iota(jnp.int32, (m, LANES), 1)
    return jnp.where(lane < HEAD_DIM, o_ab[:m], o_ab[m:])


def _ctx_attn_kernel(q_ref, k_ref, v_ref, g_ref, o_ref):
    outs = []
    for p in range(D_ATT // LANES):
        cs = slice(p * LANES, (p + 1) * LANES)
        qq = _split_heads(q_ref[:, cs] * ATT_SCALE).astype(bf16)
        s = _dot_nt(qq, k_ref[:, cs].astype(bf16))
        m = jnp.max(s, axis=-1, keepdims=True)
        e = jnp.exp(s - m)
        l = jnp.sum(e, axis=-1, keepdims=True)
        o_ab = _dot(e.astype(bf16), v_ref[:, cs].astype(bf16)) / l
        outs.append(_merge_heads(o_ab))
    o_ref[...] = _rms(jnp.concatenate(outs, axis=-1), g_ref[...])


def _ctx_attention(q, k, v, gnorm, seq):
    t = q.shape[0]
    spec = pl.BlockSpec((seq, D_ATT), lambda b: (b, 0))
    return pl.pallas_call(
        _ctx_attn_kernel,
        out_shape=jax.ShapeDtypeStruct((t, D_ATT), f32),
        grid=(t // seq,),
        in_specs=[spec, spec, spec, pl.BlockSpec((1, D_ATT), lambda b: (0, 0))],
        out_specs=spec,
        compiler_params=_cparams(("arbitrary",)),
        name="ctx_attn",
    )(q, k, v, gnorm.reshape(1, D_ATT))


def _na_tables():
    col = np.arange(GRID_W)
    cs = np.clip(col - NA_COLS // 2, 0, GRID_W - NA_COLS)
    col_mask = (col[None, :] >= cs[:, None]) & (col[None, :] < cs[:, None] + NA_COLS)
    mask = np.tile(col_mask.astype(np.float32), (1, NA_ROWS))
    return mask


N_DR = 2 * NA_ROWS - 1
N_DC = 2 * NA_COLS - 1
BIAS_PAIRS = N_DR - 1


def _na_bias_rows(rpb):
    out = jnp.zeros((H_ATT, BIAS_PAIRS, LANES), f32)
    out = out.at[:, :, 0:N_DC].set(rpb[:, 0:BIAS_PAIRS])
    return out.at[:, :, GRID_W:GRID_W + N_DC].set(rpb[:, 1:N_DR])


def _na_row_start(r, rows):
    return jnp.clip(r - NA_ROWS // 2, 0, rows - NA_ROWS)


def _na_attn_kernel(q_ref, k_ref, v_ref, kc_ref, vc_ref, rp_ref, mask_ref, g_ref, o_ref,
                    kbf_ref, vbf_ref, t2_ref, *, rows):
    b = pl.program_id(0)
    r = pl.program_id(1)

    @pl.when((b == 0) & (r == 0))
    def _():
        for h in range(H_ATT):
            for i in range(BIAS_PAIRS):
                v = jnp.broadcast_to(rp_ref[h, i:i + 1, :], (GRID_W, LANES))
                t2_ref[h * BIAS_PAIRS + i] = pltpu.roll(v, LANES - (NA_COLS - 1), 1, stride=1, stride_axis=0)

    @pl.when(r == 0)
    def _():
        _cast_rows(k_ref, kbf_ref, 256)
        _cast_rows(v_ref, vbf_ref, 256)

    nwin = NA_ROWS * GRID_W
    rs = _na_row_start(r, rows)
    start = pl.multiple_of(rs * GRID_W, GRID_W)
    i0 = rs - r + NA_ROWS - 1
    valid = mask_ref[...] != 0.0
    valid2 = jnp.concatenate([valid, valid], axis=0)
    outs = []
    for p in range(D_ATT // LANES):
        cs = slice(p * LANES, (p + 1) * LANES)
        qq = _split_heads(q_ref[:, cs] * ATT_SCALE).astype(bf16)
        kw = kbf_ref[pl.ds(start, nwin), cs]
        vw = vbf_ref[pl.ds(start, nwin), cs]
        s_lat = _dot_nt(qq, kw)
        s_ctx = _dot_nt(qq, kc_ref[:, cs].astype(bf16))
        bias2 = jnp.concatenate(
            [jnp.concatenate([t2_ref[(2 * p + hh) * BIAS_PAIRS + i0 + 2 * jp] for hh in range(2)], axis=0)
             for jp in range(NA_ROWS // 2)], axis=-1)
        s_lat = jnp.where(valid2, s_lat + bias2, NEG_INF)
        m = jnp.maximum(jnp.max(s_lat, axis=-1, keepdims=True), jnp.max(s_ctx, axis=-1, keepdims=True))
        e_lat = jnp.exp(s_lat - m)
        e_ctx = jnp.exp(s_ctx - m)
        l = jnp.sum(e_lat, axis=-1, keepdims=True) + jnp.sum(e_ctx, axis=-1, keepdims=True)
        o_ab = (_dot(e_lat.astype(bf16), vw) + _dot(e_ctx.astype(bf16), vc_ref[:, cs].astype(bf16))) / l
        outs.append(_merge_heads(o_ab))
    o_ref[...] = _rms(jnp.concatenate(outs, axis=-1), g_ref[...])


def _na_attention(q, k, v, kc, vc, rpb, gnorm, nb, seq):
    rows = seq // GRID_W
    past = kc.shape[0] // nb
    mask = _na_tables()
    return pl.pallas_call(
        functools.partial(_na_attn_kernel, rows=rows),
        out_shape=jax.ShapeDtypeStruct((nb * seq, D_ATT), f32),
        grid=(nb, rows),
        in_specs=[pl.BlockSpec((GRID_W, D_ATT), lambda b, r: (b * rows + r, 0)),
                  pl.BlockSpec((seq, D_ATT), lambda b, r: (b, 0)),
                  pl.BlockSpec((seq, D_ATT), lambda b, r: (b, 0)),
                  pl.BlockSpec((past, D_ATT), lambda b, r: (b, 0)),
                  pl.BlockSpec((past, D_ATT), lambda b, r: (b, 0)),
                  pl.BlockSpec((H_ATT, BIAS_PAIRS, LANES), lambda b, r: (0, 0, 0)),
                  pl.BlockSpec((GRID_W, NA_ROWS * GRID_W), lambda b, r: (0, 0)),
                  pl.BlockSpec((1, D_ATT), lambda b, r: (0, 0))],
        out_specs=pl.BlockSpec((GRID_W, D_ATT), lambda b, r: (b * rows + r, 0)),
        scratch_shapes=[pltpu.VMEM((seq, D_ATT), bf16), pltpu.VMEM((seq, D_ATT), bf16),
                        pltpu.VMEM((H_ATT * BIAS_PAIRS, GRID_W, LANES), f32)],
        compiler_params=_cparams(("arbitrary", "arbitrary")),
        name="na_attn",
    )(q, k, v, kc, vc, _na_bias_rows(rpb), jnp.asarray(mask), gnorm.reshape(1, D_ATT))


def _hy_front_kernel(x0_ref, x1_ref, v_ref, w0_ref, w1_ref, wv_ref, b0_ref, b1_ref, bv_ref,
                     zbf_ref, z_ref, x0c_ref):
    seq = x0_ref.shape[0]
    row = lax.broadcasted_iota(jnp.int32, x0_ref.shape, 0)
    first = row == 0
    last = row == seq - 1

    def conv(u_ref, w_ref, b_ref):
        u = u_ref[...]
        up = jnp.where(first, 0.0, pltpu.roll(u, 1, 0))
        un = jnp.where(last, 0.0, pltpu.roll(u, seq - 1, 0))
        y = b_ref[...] + up * w_ref[0:1, :]
        y = y + u * w_ref[1:2, :]
        return y + un * w_ref[2:3, :]

    z = conv(v_ref, wv_ref, bv_ref) * conv(x1_ref, w1_ref, b1_ref)
    z_ref[...] = z
    zbf_ref[...] = z.astype(bf16)
    x0c_ref[...] = conv(x0_ref, w0_ref, b0_ref)


def _hy_front(hy, conv_w, conv_b, nb, seq):
    tc = 256
    nc = D_HYENA // tc
    n = nb * D_HYENA
    cb = conv_b.reshape(1, 3 * D_HYENA)

    def part(k):
        return (pl.BlockSpec((seq, tc), lambda b, j: (b, k * nc + j)),
                pl.BlockSpec((SHORT_CONV, tc), lambda b, j: (0, k * nc + j)),
                pl.BlockSpec((1, tc), lambda b, j: (0, k * nc + j)))

    (x0s, w0s, b0s), (x1s, w1s, b1s), (vs, wvs, bvs) = part(0), part(1), part(2)
    ospec = pl.BlockSpec((seq, tc), lambda b, j: (0, b * nc + j))
    return pl.pallas_call(
        _hy_front_kernel,
        out_shape=(jax.ShapeDtypeStruct((seq, n), bf16),
                   jax.ShapeDtypeStruct((seq, n), f32),
                   jax.ShapeDtypeStruct((seq, n), f32)),
        grid=(nb, nc),
        in_specs=[x0s, x1s, vs, w0s, w1s, wvs, b0s, b1s, bvs],
        out_specs=(ospec, ospec, ospec),
        compiler_params=_cparams(("arbitrary", "arbitrary")),
        name="hyena_front",
    )(hy, hy, hy, conv_w, conv_w, conv_w, cb, cb, cb)


def _filter_features(seq):
    t = np.linspace(0.0, 1.0, seq, dtype=np.float64)[:, None]
    w = 2.0 * math.pi * np.arange(seq, dtype=np.float64)[:, None] / seq
    fb = np.linspace(1e-4, FILTER_BANDS - 1, FILTER_BANDS, dtype=np.float64)[None, :]
    ang = fb * w
    z = np.concatenate([t, np.cos(ang), -np.sin(ang)], axis=-1).astype(np.float32)
    return np.pad(z, ((0, 0), (0, LANES - EMB_DIM)))


def _filt_kernel(zf_ref, w1_ref, b1_ref, fr_ref, w2_ref, b2_ref, w3_ref, dl_ref, h_ref, kl_ref):
    i = pl.program_id(0)
    tr = zf_ref.shape[0]
    zf = zf_ref[...]
    fr = fr_ref[...]
    h = jnp.sin(fr * (_dot_hi(zf, w1_ref[...]) + b1_ref[...]))
    h = jnp.sin(fr * (_dot_hi(h, w2_ref[...]) + b2_ref[...]))
    h = _dot_hi(h, w3_ref[...])
    decay = jnp.exp(-zf[:, 0:1] * dl_ref[...])
    row = lax.broadcasted_iota(jnp.int32, (tr, D_HYENA), 0) + i * tr
    hf = h[:, :D_HYENA] * decay
    hb = jnp.where(row == 0, 0.0, h[:, D_HYENA:] * decay)
    h_ref[:, :D_HYENA] = hf.astype(bf16)
    h_ref[:, D_HYENA:] = hb.astype(bf16)
    alt = (1 - 2 * (row & 1)).astype(f32)
    part = jnp.sum(alt * (hf + hb), axis=0, keepdims=True)

    @pl.when(i == 0)
    def _():
        kl_ref[...] = jnp.zeros_like(kl_ref)

    kl_ref[...] += jnp.broadcast_to(part, kl_ref.shape)


def _hy_filters(seq, w1, b1, w2, b2, w3, freq):
    tr = 256
    zf = jnp.asarray(_filter_features(seq))
    deltas = np.abs(np.linspace(MIN_DECAY, MAX_DECAY, D_HYENA, dtype=np.float64))[None, :].astype(np.float32)
    w1p = jnp.pad(w1, ((0, LANES - EMB_DIM), (0, 0)))
    const = lambda shape: pl.BlockSpec(shape, lambda i: (0, 0))
    return pl.pallas_call(
        _filt_kernel,
        out_shape=(jax.ShapeDtypeStruct((seq, 2 * D_HYENA), bf16),
                   jax.ShapeDtypeStruct((SUBLANES, D_HYENA), f32)),
        grid=(seq // tr,),
        in_specs=[pl.BlockSpec((tr, LANES), lambda i: (i, 0)),
                  const((LANES, FILTER_FF)), const((1, FILTER_FF)), const((1, FILTER_FF)),
                  const((FILTER_FF, FILTER_FF)), const((1, FILTER_FF)),
                  const((FILTER_FF, 2 * D_HYENA)), const((1, D_HYENA))],
        out_specs=(pl.BlockSpec((tr, 2 * D_HYENA), lambda i: (i, 0)),
                   pl.BlockSpec((SUBLANES, D_HYENA), lambda i: (0, 0))),
        compiler_params=_cparams(("arbitrary",)),
        name="hyena_filters",
    )(zf, w1p, b1.reshape(1, -1), freq.reshape(1, -1), w2, b2.reshape(1, -1), w3, jnp.asarray(deltas))


def _dft_mats(seq):
    n = 2 * seq
    ph = (np.arange(seq, dtype=np.int64)[:, None] * np.arange(seq, dtype=np.int64)[None, :]) % n
    ang = ph.astype(np.float64) * (2.0 * math.pi / n)
    return np.cos(ang).astype(np.float32), np.sin(ang).astype(np.float32)


def _alt_col(rows, offset):
    row = lax.broadcasted_iota(jnp.int32, (rows, 1), 0) + offset
    return (1 - 2 * (row & 1)).astype(f32)


def _hy_fwd_kernel(fr_ref, fi_ref, z_ref, h_ref, kl_ref, yr_ref, yi_ref, yl_ref, kr_s, ki_s, *, n):
    i = pl.program_id(0)
    j = pl.program_id(1)
    tf = fr_ref.shape[0]
    tn = z_ref.shape[1]
    frb = fr_ref[...].astype(bf16)
    fib = fi_ref[...].astype(bf16)

    @pl.when(j == 0)
    def _():
        ah = _dot(frb, h_ref[...])
        bh = _dot(fib, h_ref[...])
        f = lax.broadcasted_iota(jnp.int32, (tf, 1), 0) + i * tf
        cf = jnp.where(f == 0, 1.0 / n, 2.0 / n)
        kr_s[...] = (ah[:, :D_HYENA] + ah[:, D_HYENA:]) * cf
        ki_s[...] = (bh[:, D_HYENA:] - bh[:, :D_HYENA]) * cf

    a = _dot(frb, z_ref[...])
    b = _dot(fib, z_ref[...])
    kr = kr_s[...]
    ki = ki_s[...]
    for c in range(tn // D_HYENA):
        cs = slice(c * D_HYENA, (c + 1) * D_HYENA)
        yr_ref[:, cs] = (a[:, cs] * kr + b[:, cs] * ki).astype(bf16)
        yi_ref[:, cs] = (b[:, cs] * kr - a[:, cs] * ki).astype(bf16)

    @pl.when(i == 0)
    def _():
        alt = _alt_col(z_ref.shape[0], 0)
        nz = jnp.sum(z_ref[...].astype(f32) * alt, axis=0, keepdims=True)
        kl = jnp.concatenate([kl_ref[0:1, :]] * (tn // D_HYENA), axis=-1)
        yl_ref[...] = jnp.broadcast_to(nz * kl * (1.0 / n), yl_ref.shape)


def _hy_fwd(fr, fi, zbf, hcat, kl, seq):
    n_cols = zbf.shape[1]
    tf = 256
    tn = min(n_cols, 1024)
    ni, nj = seq // tf, n_cols // tn
    assert ni == 1 or nj == 1
    return pl.pallas_call(
        functools.partial(_hy_fwd_kernel, n=2 * seq),
        out_shape=(jax.ShapeDtypeStruct((seq, n_cols), bf16),
                   jax.ShapeDtypeStruct((seq, n_cols), bf16),
                   jax.ShapeDtypeStruct((SUBLANES, n_cols), f32)),
        grid=(ni, nj),
        in_specs=[pl.BlockSpec((tf, seq), lambda i, j: (i, 0)),
                  pl.BlockSpec((tf, seq), lambda i, j: (i, 0)),
                  pl.BlockSpec((seq, tn), lambda i, j: (0, j)),
                  pl.BlockSpec((seq, 2 * D_HYENA), lambda i, j: (0, 0)),
                  pl.BlockSpec((SUBLANES, D_HYENA), lambda i, j: (0, 0))],
        out_specs=(pl.BlockSpec((tf, tn), lambda i, j: (i, j)),
                   pl.BlockSpec((tf, tn), lambda i, j: (i, j)),
                   pl.BlockSpec((SUBLANES, tn), lambda i, j: (0, j))),
        scratch_shapes=[pltpu.VMEM((tf, D_HYENA), f32), pltpu.VMEM((tf, D_HYENA), f32)],
        compiler_params=_cparams(("arbitrary", "arbitrary")),
        name="hyena_dft_fwd",
    )(fr, fi, zbf, hcat, kl)


def _hy_inv_kernel(fr_ref, fi_ref, yr_ref, yi_ref, yl_ref, z_ref, x0_ref, skip_ref, g_ref, o_ref):
    tt = fr_ref.shape[0]
    tn = yr_ref.shape[1]
    y = _dot(fr_ref[...].astype(bf16), yr_ref[...]) + _dot(fi_ref[...].astype(bf16), yi_ref[...])
    alt = _alt_col(tt, pl.program_id(0) * tt)
    for c in range(tn // D_HYENA):
        cs = slice(c * D_HYENA, (c + 1) * D_HYENA)
        yc = y[:, cs] + alt * yl_ref[0:1, cs] + z_ref[:, cs] * skip_ref[...]
        o_ref[:, cs] = _rms(yc * x0_ref[:, cs], g_ref[...])


def _hy_inv(fr, fi, yr, yi, yl, z, x0c, skip, gnorm, seq):
    n_cols = z.shape[1]
    tt = 256
    tn = min(n_cols, 1024)
    blk = pl.BlockSpec((tt, tn), lambda i, j: (i, j))
    return pl.pallas_call(
        _hy_inv_kernel,
        out_shape=jax.ShapeDtypeStruct((seq, n_cols), f32),
        grid=(seq // tt, n_cols // tn),
        in_specs=[pl.BlockSpec((tt, seq), lambda i, j: (i, 0)),
                  pl.BlockSpec((tt, seq), lambda i, j: (i, 0)),
                  pl.BlockSpec((seq, tn), lambda i, j: (0, j)),
                  pl.BlockSpec((seq, tn), lambda i, j: (0, j)),
                  pl.BlockSpec((SUBLANES, tn), lambda i, j: (0, j)),
                  blk, blk,
                  pl.BlockSpec((1, D_HYENA), lambda i, j: (0, 0)),
                  pl.BlockSpec((1, D_HYENA), lambda i, j: (0, 0))],
        out_specs=blk,
        compiler_params=_cparams(("arbitrary", "arbitrary")),
        name="hyena_dft_inv",
    )(fr, fi, yr, yi, yl, z, x0c, skip.reshape(1, D_HYENA), gnorm.reshape(1, D_HYENA))


def _hyena(hy, lp, nb, seq):
    fr_np, fi_np = _dft_mats(seq)
    fr, fi = jnp.asarray(fr_np), jnp.asarray(fi_np)
    zbf, z, x0c = _hy_front(hy, lp['conv_w'], lp['conv_b'], nb, seq)
    hcat, kl = _hy_filters(seq, lp['filt_w1'], lp['filt_b1'], lp['filt_w2'], lp['filt_b2'],
                           lp['filt_w3'], lp['filt_freq'])
    yr, yi, yl = _hy_fwd(fr, fi, zbf, hcat, kl, seq)
    return _hy_inv(fr, fi, yr, yi, yl, z, x0c, lp['hyena_skip'], lp['gnorm_hyena'], seq)


def _store_token_tiles(ref, x):
    m = x.shape[0]
    for c in range(D_MODEL // LANES):
        ref[pl.ds(c, m, stride=SUBLANES), :] = x[:, c * LANES:(c + 1) * LANES]


def _load_token_tiles(ref, m, lead=()):
    return jnp.concatenate([ref[lead + (pl.ds(c, m, stride=SUBLANES), slice(None))]
                            for c in range(D_MODEL // LANES)], axis=-1)


def _route(logits):
    lane_i = lax.broadcasted_iota(jnp.int32, logits.shape, 1)
    lane = lane_i.astype(f32)
    big = float(ROUTE_LANES)
    is_g = lane_i < N_GROUPS
    mg = jnp.max(jnp.where(is_g, logits, -jnp.inf), axis=-1, keepdims=True)
    sg = jnp.sum(jnp.where(is_g, jnp.exp(logits - mg), 0.0), axis=-1, keepdims=True)
    g_w = 1.0 / sg
    g_idx = jnp.min(jnp.where(is_g & (logits == mg), lane, big), axis=-1, keepdims=True)
    e_id = lane_i - ROUTE_EXP_LANE0
    sel = (e_id >= 0) & (e_id < N_EXPERTS) & ((e_id >> 2).astype(f32) == g_idx)
    me = jnp.max(jnp.where(sel, logits, -jnp.inf), axis=-1, keepdims=True)
    ee = jnp.where(sel, jnp.exp(logits - me), 0.0)
    prob = ee / jnp.sum(ee, axis=-1, keepdims=True)
    p1 = jnp.max(jnp.where(sel, prob, -1.0), axis=-1, keepdims=True)
    i1 = jnp.min(jnp.where(sel & (prob == p1), lane, big), axis=-1, keepdims=True)
    sel2 = sel & (lane != i1)
    p2 = jnp.max(jnp.where(sel2, prob, -1.0), axis=-1, keepdims=True)
    i2 = jnp.min(jnp.where(sel2 & (prob == p2), lane, big), axis=-1, keepdims=True)
    tot = p1 + p2
    rec = jnp.where(lane_i == ROUTE_E1, i1 - ROUTE_EXP_LANE0, 0.0)
    rec = jnp.where(lane_i == ROUTE_E2, i2 - ROUTE_EXP_LANE0, rec)
    rec = jnp.where(lane_i == ROUTE_W1, g_w * (p1 / tot), rec)
    return jnp.where(lane_i == ROUTE_W2, g_w * (p2 / tot), rec)


def _outproj_kernel(xp_ref, xs_ref, attp_ref, atts_ref, hyp_ref, hys_ref, mod_ref, wo_ref, g2_ref,
                    wr_ref, br_ref, x1_ref, tok_ref, rt_ref, wobf_ref, wrh_ref, wrl_ref, *, prompt_tiles):
    i = pl.program_id(0)

    @pl.when(i == 0)
    def _():
        _cast_rows(wo_ref, wobf_ref, 128)
        wr = wr_ref[...]
        hi = wr.astype(bf16)
        wrh_ref[...] = hi
        wrl_ref[...] = (wr - hi.astype(f32)).astype(bf16)

    is_p = i < prompt_tiles
    hyp = jnp.concatenate([hyp_ref[:, b * D_HYENA:(b + 1) * D_HYENA]
                           for b in range(hyp_ref.shape[1] // D_HYENA)], axis=0)
    x = jnp.where(is_p, xp_ref[...], xs_ref[...])
    att = jnp.where(is_p, attp_ref[...], atts_ref[...])
    hyo = jnp.where(is_p, hyp, hys_ref[...])
    proj = (_dot(att.astype(bf16), wobf_ref[0:D_ATT, :]) + _dot(hyo.astype(bf16), wobf_ref[D_ATT:, :]))
    x1 = x + mod_ref[0, GT1:GT1 + 1, :] * proj
    x1_ref[...] = x1
    h2 = _rms(x1, g2_ref[...]) * (1.0 + mod_ref[0, SC2:SC2 + 1, :]) + mod_ref[0, SH2:SH2 + 1, :]
    h2h = h2.astype(bf16)
    h2l = (h2 - h2h.astype(f32)).astype(bf16)
    logits = _dot(h2h, wrh_ref[...]) + _dot(h2l, wrh_ref[...]) + _dot(h2h, wrl_ref[...]) + br_ref[...]
    rt_ref[...] = _route(logits)
    _store_token_tiles(tok_ref, h2)


def _outproj(xp, xs, attp, atts, hyp, hys, mod, w_out, norm2_g, wr, br, seq_p, seq_s):
    tm = 512
    tp, ts = xp.shape[0], xs.shape[0]
    npt, nst = tp // tm, ts // tm
    assert tm % seq_p == 0 and seq_s % tm == 0
    spb = seq_s // tm
    bpt = tm // seq_p
    p_idx = lambda i: jnp.minimum(i, npt - 1)
    s_idx = lambda i: jnp.maximum(i - npt, 0)
    const = lambda shape: pl.BlockSpec(shape, lambda i: (0,) * len(shape))
    return pl.pallas_call(
        functools.partial(_outproj_kernel, prompt_tiles=npt),
        out_shape=(jax.ShapeDtypeStruct((tp + ts, D_MODEL), f32),
                   jax.ShapeDtypeStruct(((tp + ts) * SUBLANES, LANES), f32),
                   jax.ShapeDtypeStruct((tp + ts, ROUTE_LANES), f32)),
        grid=(npt + nst,),
        in_specs=[pl.BlockSpec((tm, D_MODEL), lambda i: (p_idx(i), 0)),
                  pl.BlockSpec((tm, D_MODEL), lambda i: (s_idx(i), 0)),
                  pl.BlockSpec((tm, D_ATT), lambda i: (p_idx(i), 0)),
                  pl.BlockSpec((tm, D_ATT), lambda i: (s_idx(i), 0)),
                  pl.BlockSpec((seq_p, bpt * D_HYENA), lambda i: (0, p_idx(i))),
                  pl.BlockSpec((tm, D_HYENA), lambda i: (s_idx(i) % spb, s_idx(i) // spb)),
                  pl.BlockSpec((1, MOD_ROWS, D_MODEL),
                               lambda i: (jnp.where(i < npt, 0, 1 + s_idx(i) // spb), 0, 0)),
                  const((D_MODEL, D_MODEL)), const((1, D_MODEL)),
                  const((D_MODEL, ROUTE_LANES)), const((1, ROUTE_LANES))],
        out_specs=(pl.BlockSpec((tm, D_MODEL), lambda i: (i, 0)),
                   pl.BlockSpec((tm * SUBLANES, LANES), lambda i: (i, 0)),
                   pl.BlockSpec((tm, ROUTE_LANES), lambda i: (i, 0))),
        scratch_shapes=[pltpu.VMEM((D_MODEL, D_MODEL), bf16),
                        pltpu.VMEM((D_MODEL, ROUTE_LANES), bf16), pltpu.VMEM((D_MODEL, ROUTE_LANES), bf16)],
        compiler_params=_cparams(("arbitrary",)),
        name="outproj_router",
    )(xp, xs, attp, atts, hyp, hys, mod, w_out, norm2_g.reshape(1, D_MODEL), wr, br)


def _route_plan(e12, tm, max_items):
    i32 = jnp.int32
    experts = jnp.arange(N_EXPERTS, dtype=i32)
    oh = (e12[:, None] == experts[None, :]).astype(i32)
    cs = jnp.cumsum(oh, axis=0)
    counts = cs[-1]
    rank = jnp.sum(oh * (cs - 1), axis=1)
    end = jnp.cumsum(counts)
    off = end - counts
    pos = jnp.sum(oh * off[None, :], axis=1) + rank
    first_tile = off // tm
    n_tiles = jnp.where(counts > 0, (end - 1) // tm - first_tile + 1, 0)
    iend = jnp.cumsum(n_tiles)
    istart = iend - n_tiles
    n_items = iend[-1]
    items = jnp.minimum(jnp.arange(max_items, dtype=i32), n_items - 1)
    it_exp = jnp.minimum(jnp.sum((items[:, None] >= iend[None, :]).astype(i32), axis=1), N_EXPERTS - 1)
    sel = (it_exp[:, None] == experts[None, :]).astype(i32)
    pick = lambda v: jnp.sum(sel * v[None, :], axis=1)
    k = items - pick(istart)
    it_tile = pick(first_tile) + k
    it_lo = jnp.maximum(pick(off) - it_tile * tm, 0)
    it_hi = jnp.minimum(pick(end) - it_tile * tm, tm)
    flags = (k == 0).astype(i32) + 2 * (it_lo == 0).astype(i32) + 4 * (it_hi == tm).astype(i32)
    return dict(pos=pos, n_items=n_items.reshape(1), it_tile=it_tile, it_exp=it_exp, it_lo=it_lo,
                it_hi=it_hi, it_flags=flags)


FLAG_NEW_EXPERT, FLAG_TILE_START, FLAG_TILE_END = 1, 2, 4


ROW_DMA_UNROLL = 8


def _row_dma_loop(n, make_copy, priorities):
    assert n % ROW_DMA_UNROLL == 0

    def body(i, c):
        for u in range(ROW_DMA_UNROLL):
            make_copy(i * ROW_DMA_UNROLL + u).start(priority=priorities[u % len(priorities)])
        return c

    lax.fori_loop(0, n // ROW_DMA_UNROLL, body, 0)


def _moe_ffn_kernel(pos_ref, it_tile, it_exp, it_lo, it_hi, it_flags, n_items, tok_hbm, wg_ref, wu_ref, wd_ref,
                    ab_hbm, src_s, xbuf, ybuf, x_s, acc_s, wgb_s, wub_s, wdb_s, sem_in, sem_out,
                    *, chunk, n_tok, n_tiles):
    i = pl.program_id(0)
    tm = x_s.shape[0]

    def gather_tile(k):
        base = k * tm
        slot = k % 2
        _row_dma_loop(tm, lambda r: pltpu.make_async_copy(
            tok_hbm.at[src_s[base + r] & (n_tok - 1)],
            xbuf.at[slot, pl.ds(pl.multiple_of(r * SUBLANES, SUBLANES), SUBLANES), :],
            sem_in.at[slot]), GATHER_PRIORITIES)

    def scatter_tile(k):
        base = k * tm
        slot = k % 2
        _row_dma_loop(tm, lambda r: pltpu.make_async_copy(
            ybuf.at[slot, pl.ds(pl.multiple_of(r * SUBLANES, SUBLANES), SUBLANES), :],
            ab_hbm.at[src_s[base + r]],
            sem_out.at[slot]), SCATTER_PRIORITIES)

    def wait_all(buf, sem, slot):
        pltpu.make_async_copy(buf.at[slot], buf.at[slot], sem.at[slot]).wait()

    @pl.when(i == 0)
    def _():
        def inv(a, c):
            for u in range(ROW_DMA_UNROLL):
                src_s[pos_ref[a * ROW_DMA_UNROLL + u]] = a * ROW_DMA_UNROLL + u
            return c

        lax.fori_loop(0, pos_ref.shape[0] // ROW_DMA_UNROLL, inv, 0)
        gather_tile(0)

    @pl.when(i < n_items[0])
    def _():
        flags = it_flags[i]
        k = it_tile[i]

        @pl.when((flags & FLAG_NEW_EXPERT) != 0)
        def _():
            wgb_s[...] = wg_ref[0].astype(bf16)
            wub_s[...] = wu_ref[0].astype(bf16)
            wdb_s[...] = wd_ref[0].astype(bf16)

        @pl.when((flags & FLAG_TILE_START) != 0)
        def _():
            @pl.when(k + 1 < n_tiles)
            def _():
                gather_tile(k + 1)

            wait_all(xbuf, sem_in, k % 2)
            x_s[...] = _load_token_tiles(xbuf, tm, (k % 2,)).astype(bf16)
            acc_s[...] = jnp.zeros_like(acc_s)

        lo = it_lo[i]
        hi = it_hi[i]

        def body(k, c):
            r = pl.multiple_of(k * chunk, chunk)

            @pl.when((r < hi) & (r + chunk > lo))
            def _():
                x = x_s[pl.ds(r, chunk), :]
                g = _dot(x, wgb_s[...])
                u = _dot(x, wub_s[...])
                hid = (g * (1.0 / (1.0 + jnp.exp(-g)))) * u
                y = _dot(hid.astype(bf16), wdb_s[...])
                row = lax.broadcasted_iota(jnp.int32, (chunk, 1), 0) + r
                mine = (row >= lo) & (row < hi)
                acc_s[pl.ds(r, chunk), :] = jnp.where(mine, y, acc_s[pl.ds(r, chunk), :])

            return c

        lax.fori_loop(0, tm // chunk, body, 0)

        @pl.when((flags & FLAG_TILE_END) != 0)
        def _():
            @pl.when(k >= 2)
            def _():
                wait_all(ybuf, sem_out, k % 2)

            _store_token_tiles(ybuf.at[k % 2], acc_s[...])
            scatter_tile(k)

            @pl.when(k == n_tiles - 1)
            def _():
                if n_tiles > 1:
                    wait_all(ybuf, sem_out, (n_tiles - 2) % 2)
                wait_all(ybuf, sem_out, (n_tiles - 1) % 2)


def _moe_ffn(plan, tok, w_gate, w_up, w_down, tm, max_items):
    n_tok = tok.shape[0] // SUBLANES
    n_rows = plan['pos'].shape[0]
    n_tiles = n_rows // tm
    assert n_tok & (n_tok - 1) == 0
    tok3 = tok.reshape(n_tok, SUBLANES, LANES)
    wspec = lambda shape: pl.BlockSpec((1,) + shape, lambda i, ps, tl, ex, lo, hi, fl, n: (ex[i], 0, 0))
    return pl.pallas_call(
        functools.partial(_moe_ffn_kernel, chunk=256, n_tok=n_tok, n_tiles=n_tiles),
        out_shape=jax.ShapeDtypeStruct((n_rows, SUBLANES, LANES), f32),
        grid_spec=pltpu.PrefetchScalarGridSpec(
            num_scalar_prefetch=7,
            grid=(max_items,),
            in_specs=[pl.BlockSpec(memory_space=pl.ANY),
                      wspec((D_MODEL, D_EXPERT)), wspec((D_MODEL, D_EXPERT)), wspec((D_EXPERT, D_MODEL))],
            out_specs=pl.BlockSpec(memory_space=pl.ANY),
            scratch_shapes=[pltpu.SMEM((n_rows,), jnp.int32),
                            pltpu.VMEM((2, tm * SUBLANES, LANES), f32),
                            pltpu.VMEM((2, tm * SUBLANES, LANES), f32),
                            pltpu.VMEM((tm, D_MODEL), bf16), pltpu.VMEM((tm, D_MODEL), f32),
                            pltpu.VMEM((D_MODEL, D_EXPERT), bf16), pltpu.VMEM((D_MODEL, D_EXPERT), bf16),
                            pltpu.VMEM((D_EXPERT, D_MODEL), bf16),
                            pltpu.SemaphoreType.DMA((2,)), pltpu.SemaphoreType.DMA((2,))]),
        compiler_params=_cparams(("arbitrary",)),
        name="moe_ffn",
    )(plan['pos'], plan['it_tile'], plan['it_exp'], plan['it_lo'], plan['it_hi'], plan['it_flags'],
      plan['n_items'], tok3, w_gate, w_up, w_down)


def _final_kernel(a_ref, b_ref, x1_ref, rt_ref, mod_ref, fg_ref, yp_ref, yl_ref, *, prompt_tiles):
    i = pl.program_id(0)
    tm = x1_ref.shape[0]
    rt = rt_ref[...]
    moe = (rt[:, ROUTE_W1:ROUTE_W1 + 1] * _load_token_tiles(a_ref, tm)
           + rt[:, ROUTE_W2:ROUTE_W2 + 1] * _load_token_tiles(b_ref, tm))
    y = _rms(x1_ref[...] + mod_ref[0, GT2:GT2 + 1, :] * moe, fg_ref[...])

    @pl.when(i < prompt_tiles)
    def _():
        yp_ref[...] = y

    @pl.when(i >= prompt_tiles)
    def _():
        yl_ref[...] = y


def _final(ab, x1, rt, mod, final_g, t_prompt, t_lat, seq_s):
    tm = 256
    npt, nst = t_prompt // tm, t_lat // tm
    spb = seq_s // tm
    ab2 = ab.reshape(ab.shape[0] * SUBLANES, LANES)
    return pl.pallas_call(
        functools.partial(_final_kernel, prompt_tiles=npt),
        out_shape=(jax.ShapeDtypeStruct((t_prompt, D_MODEL), f32),
                   jax.ShapeDtypeStruct((t_lat, D_MODEL), f32)),
        grid=(npt + nst,),
        in_specs=[pl.BlockSpec((tm * SUBLANES, LANES), lambda i: (i, 0)),
                  pl.BlockSpec((tm * SUBLANES, LANES), lambda i: (npt + nst + i, 0)),
                  pl.BlockSpec((tm, D_MODEL), lambda i: (i, 0)),
                  pl.BlockSpec((tm, ROUTE_LANES), lambda i: (i, 0)),
                  pl.BlockSpec((1, MOD_ROWS, D_MODEL),
                               lambda i: (jnp.where(i < npt, 0, 1 + jnp.maximum(i - npt, 0) // spb), 0, 0)),
                  pl.BlockSpec((1, D_MODEL), lambda i: (0, 0))],
        out_specs=(pl.BlockSpec((tm, D_MODEL), lambda i: (jnp.minimum(i, npt - 1), 0)),
                   pl.BlockSpec((tm, D_MODEL), lambda i: (jnp.maximum(i - npt, 0), 0))),
        compiler_params=_cparams(("arbitrary",)),
        name="moe_combine_final",
    )(ab2, ab2, x1, rt, mod, final_g.reshape(1, D_MODEL))


def _moe(tok, rt, x1, mod, w_gate, w_up, w_down, final_g, t_prompt, t_lat, seq_s):
    tm = MOE_TM
    n_rows = 2 * (t_prompt + t_lat)
    assert n_rows % tm == 0
    max_items = n_rows // tm + N_EXPERTS
    e12 = jnp.concatenate([rt[:, ROUTE_E1], rt[:, ROUTE_E2]]).astype(jnp.int32)
    plan = _route_plan(e12, tm, max_items)
    ab = _moe_ffn(plan, tok, w_gate, w_up, w_down, tm, max_items)
    return _final(ab, x1, rt, mod, final_g, t_prompt, t_lat, seq_s)


def kernel(x_prompt, x_sample, cache_k, cache_v, c, c_ctx, w_ada, b_ada, norm1_g, w_in, rpb, conv_w, conv_b, filt_w1, filt_b1, filt_w2, filt_b2, filt_w3, filt_freq, hyena_skip, gnorm_att, gnorm_hyena, w_out, norm2_g, router_grp_w, router_grp_b, router_exp_w, router_exp_b, w_gate, w_up, w_down, final_g):
    depth = w_ada.shape[0]
    assert depth == 1
    batch, seq, _ = x_prompt.shape
    dec_batch, dec_seq, _ = x_sample.shape
    l = 0

    wr = jnp.zeros((D_MODEL, ROUTE_LANES), f32)
    wr = wr.at[:, :N_GROUPS].set(router_grp_w[l])
    wr = wr.at[:, ROUTE_EXP_LANE0:ROUTE_EXP_LANE0 + N_EXPERTS].set(router_exp_w[l])
    br = jnp.zeros((1, ROUTE_LANES), f32)
    br = br.at[0, :N_GROUPS].set(router_grp_b[l])
    br = br.at[0, ROUTE_EXP_LANE0:ROUTE_EXP_LANE0 + N_EXPERTS].set(router_exp_b[l])

    lp = {
        'norm1_g': norm1_g[l], 'w_in': w_in[l], 'conv_w': conv_w[l], 'conv_b': conv_b[l],
        'filt_w1': filt_w1[l], 'filt_b1': filt_b1[l], 'filt_w2': filt_w2[l], 'filt_b2': filt_b2[l],
        'filt_w3': filt_w3[l], 'filt_freq': filt_freq[l], 'hyena_skip': hyena_skip[l],
        'gnorm_hyena': gnorm_hyena[l], 'w_out': w_out[l], 'norm2_g': norm2_g[l],
        'wr': wr, 'br': br, 'w_gate': w_gate[l], 'w_up': w_up[l], 'w_down': w_down[l],
    }

    cond8 = jnp.zeros((SUBLANES, D_MODEL), f32).at[0].set(c_ctx).at[1:1 + dec_batch].set(c)
    mod = _ada_mod(cond8, w_ada[l], b_ada[l]).reshape(SUBLANES, N_MOD, D_MODEL)
    mod = jnp.pad(mod, ((0, 0), (0, MOD_ROWS - N_MOD), (0, 0)))
    mod_ctx, mod_lat = mod[0:1], mod[1:1 + dec_batch]

    xp = x_prompt.reshape(batch * seq, D_MODEL)
    xs = x_sample.reshape(dec_batch * dec_seq, D_MODEL)

    qp, k_ctx, v_ctx, hyp = _inproj(xp, mod_ctx, lp['norm1_g'], lp['w_in'], batch * seq)
    attp = _ctx_attention(qp, k_ctx, v_ctx, gnorm_att[l], seq)
    hyop = _hyena(hyp, lp, batch, seq)

    ql, kl, vl, hyl = _inproj(xs, mod_lat, lp['norm1_g'], lp['w_in'], dec_seq)
    kc = cache_k[:, l].reshape(dec_batch * cache_k.shape[2], D_ATT)
    vc = cache_v[:, l].reshape(dec_batch * cache_v.shape[2], D_ATT)
    attl = _na_attention(ql, kl, vl, kc, vc, rpb[l], gnorm_att[l], dec_batch, dec_seq)
    hyol = _hyena(hyl, lp, dec_batch, dec_seq)

    x1, tok, rt = _outproj(xp, xs, attp, attl, hyop, hyol, mod[0:1 + dec_batch], lp['w_out'], lp['norm2_g'],
                           lp['wr'], lp['br'], seq, dec_seq)
    yp, ys = _moe(tok, rt, x1, mod[0:1 + dec_batch], lp['w_gate'], lp['w_up'], lp['w_down'], final_g,
                  batch * seq, dec_batch * dec_seq, dec_seq)

    y_prompt = yp.reshape(batch, seq, D_MODEL)
    y_sample = ys.reshape(dec_batch, dec_seq, D_MODEL)
    new_k = k_ctx.reshape(batch, 1, seq, H_ATT, HEAD_DIM)
    new_v = v_ctx.reshape(batch, 1, seq, H_ATT, HEAD_DIM)
    return (y_prompt, y_sample, new_k, new_v)
```

```python
import functools
import math

import jax
import jax.numpy as jnp
import numpy as np
from jax import lax
from jax.experimental import pallas as pl
from jax.experimental.pallas import tpu as pltpu

f32 = jnp.float32
bf16 = jnp.bfloat16
HIGHEST = lax.Precision.HIGHEST

D_MODEL = 1024
GRID_W = 64
H_ATT = 8
HEAD_DIM = 64
D_ATT = H_ATT * HEAD_DIM
D_HYENA = 512
D_IN = 3 * D_ATT + 3 * D_HYENA
NA_ROWS = 8
NA_COLS = 16
SHORT_CONV = 3
FILTER_BANDS = 16
EMB_DIM = 1 + 2 * FILTER_BANDS
FILTER_FF = 64
DECAY_TARGET = 1e-2
MIN_DECAY = math.log(DECAY_TARGET) / 1.5
MAX_DECAY = math.log(DECAY_TARGET) / 0.3
N_GROUPS = 4
EXPERTS_PER_GROUP = 4
N_EXPERTS = N_GROUPS * EXPERTS_PER_GROUP
D_EXPERT = 512
N_MOD = 6
EPS = 1e-6
NEG_INF = -1e30
ATT_SCALE = HEAD_DIM ** -0.5

LANES = 128
SUBLANES = 8
MOD_ROWS = 8
ROUTE_LANES = 128
ROUTE_EXP_LANE0 = 16
ROUTE_E1, ROUTE_E2, ROUTE_W1, ROUTE_W2 = 0, 1, 2, 3
TOK_PITCH = 16
TOK_RT_ROW = 8
MOE_TM = 512
GATHER_PRIORITIES = (0,)
SCATTER_PRIORITIES = (1,)
VMEM_LIMIT = 56 * 1024 * 1024

SH1, SC1, GT1, SH2, SC2, GT2 = range(6)


def _cparams(sem, vmem=VMEM_LIMIT):
    return pltpu.CompilerParams(dimension_semantics=sem, vmem_limit_bytes=vmem)


def _dot(a, b):
    return jnp.dot(a, b, preferred_element_type=f32)


def _dot_hi(a, b):
    return lax.dot_general(a, b, (((1,), (0,)), ((), ())), precision=HIGHEST,
                           preferred_element_type=f32)


def _dot_nt(a, b):
    return lax.dot_general(a, b, (((1,), (1,)), ((), ())), preferred_element_type=f32)


def _rms(x, g):
    ms = jnp.mean(x * x, axis=-1, keepdims=True)
    return x * lax.rsqrt(ms + EPS) * g


def _cast_rows(src_ref, dst_ref, chunk):
    n = src_ref.shape[0] // chunk

    def body(i, c):
        r = pl.multiple_of(i * chunk, chunk)
        dst_ref[pl.ds(r, chunk), :] = src_ref[pl.ds(r, chunk), :].astype(dst_ref.dtype)
        return c

    lax.fori_loop(0, n, body, 0)


def _ada_kernel(c_ref, w_ref, b_ref, o_ref):
    c = c_ref[...]
    s = c * (1.0 / (1.0 + jnp.exp(-c)))
    o_ref[...] = _dot_hi(s, w_ref[...]) + b_ref[...]


def _ada_mod(cond8, w_ada, b_ada):
    tn = 1536
    n = N_MOD * D_MODEL
    return pl.pallas_call(
        _ada_kernel,
        out_shape=jax.ShapeDtypeStruct((SUBLANES, n), f32),
        grid=(n // tn,),
        in_specs=[pl.BlockSpec((SUBLANES, D_MODEL), lambda j: (0, 0)),
                  pl.BlockSpec((D_MODEL, tn), lambda j: (0, j)),
                  pl.BlockSpec((1, tn), lambda j: (0, j))],
        out_specs=pl.BlockSpec((SUBLANES, tn), lambda j: (0, j)),
        compiler_params=_cparams(("arbitrary",)),
        name="ada_mod",
    )(cond8, w_ada, b_ada.reshape(1, n))


def _inproj_kernel(x_ref, mod_ref, g_ref, w_ref, q_ref, k_ref, v_ref, hy_ref, wbf_ref):
    @pl.when(pl.program_id(0) == 0)
    def _():
        _cast_rows(w_ref, wbf_ref, 128)

    h = _rms(x_ref[...], g_ref[...])
    h = h * (1.0 + mod_ref[0, SC1:SC1 + 1, :]) + mod_ref[0, SH1:SH1 + 1, :]
    p = _dot(h.astype(bf16), wbf_ref[...])
    q_ref[...] = p[:, 0:D_ATT]
    k_ref[...] = p[:, D_ATT:2 * D_ATT]
    v_ref[...] = p[:, 2 * D_ATT:3 * D_ATT]
    hy_ref[...] = p[:, 3 * D_ATT:]


def _inproj(x, mod, norm_g, w_in, rows_per_mod):
    t = x.shape[0]
    tm = 512
    blocks_per_mod = rows_per_mod // tm
    return pl.pallas_call(
        _inproj_kernel,
        out_shape=(jax.ShapeDtypeStruct((t, D_ATT), f32),
                   jax.ShapeDtypeStruct((t, D_ATT), f32),
                   jax.ShapeDtypeStruct((t, D_ATT), f32),
                   jax.ShapeDtypeStruct((t, 3 * D_HYENA), f32)),
        grid=(t // tm,),
        in_specs=[pl.BlockSpec((tm, D_MODEL), lambda i: (i, 0)),
                  pl.BlockSpec((1, MOD_ROWS, D_MODEL), lambda i: (i // blocks_per_mod, 0, 0)),
                  pl.BlockSpec((1, D_MODEL), lambda i: (0, 0)),
                  pl.BlockSpec((D_MODEL, D_IN), lambda i: (0, 0), pipeline_mode=pl.Buffered(1))],
        out_specs=(pl.BlockSpec((tm, D_ATT), lambda i: (i, 0)),
                   pl.BlockSpec((tm, D_ATT), lambda i: (i, 0)),
                   pl.BlockSpec((tm, D_ATT), lambda i: (i, 0)),
                   pl.BlockSpec((tm, 3 * D_HYENA), lambda i: (i, 0))),
        scratch_shapes=[pltpu.VMEM((D_MODEL, D_IN), bf16)],
        compiler_params=_cparams(("arbitrary",)),
        name="inproj",
    )(x, mod, norm_g.reshape(1, D_MODEL), w_in)


def _split_heads(q2):
    lane = lax.broadcasted_iota(jnp.int32, q2.shape, 1)
    qa = jnp.where(lane < HEAD_DIM, q2, 0.0)
    qb = jnp.where(lane >= HEAD_DIM, q2, 0.0)
    return jnp.concatenate([qa, qb], axis=0)


def _merge_heads(o_ab):
    m = o_ab.shape[0] // 2
    lane = lax.broadcasted_iota(jnp.int32, (m, LANES), 1)
    return jnp.where(lane < HEAD_DIM, o_ab[:m], o_ab[m:])


def _ctx_attn_kernel(q_ref, k_ref, v_ref, g_ref, o_ref):
    outs = []
    for p in range(D_ATT // LANES):
        cs = slice(p * LANES, (p + 1) * LANES)
        qq = _split_heads(q_ref[:, cs] * ATT_SCALE).astype(bf16)
        s = _dot_nt(qq, k_ref[:, cs].astype(bf16))
        m = jnp.max(s, axis=-1, keepdims=True)
        e = jnp.exp(s - m)
        l = jnp.sum(e, axis=-1, keepdims=True)
        o_ab = _dot(e.astype(bf16), v_ref[:, cs].astype(bf16)) / l
        outs.append(_merge_heads(o_ab))
    o_ref[...] = _rms(jnp.concatenate(outs, axis=-1), g_ref[...])


def _ctx_attention(q, k, v, gnorm, seq):
    t = q.shape[0]
    spec = pl.BlockSpec((seq, D_ATT), lambda b: (b, 0))
    return pl.pallas_call(
        _ctx_attn_kernel,
        out_shape=jax.ShapeDtypeStruct((t, D_ATT), f32),
        grid=(t // seq,),
        in_specs=[spec, spec, spec, pl.BlockSpec((1, D_ATT), lambda b: (0, 0))],
        out_specs=spec,
        compiler_params=_cparams(("arbitrary",)),
        name="ctx_attn",
    )(q, k, v, gnorm.reshape(1, D_ATT))


def _na_tables():
    col = np.arange(GRID_W)
    cs = np.clip(col - NA_COLS // 2, 0, GRID_W - NA_COLS)
    col_mask = (col[None, :] >= cs[:, None]) & (col[None, :] < cs[:, None] + NA_COLS)
    mask = np.tile(col_mask.astype(np.float32), (1, NA_ROWS))
    return mask


N_DR = 2 * NA_ROWS - 1
N_DC = 2 * NA_COLS - 1
BIAS_PAIRS = N_DR - 1


def _na_bias_rows(rpb):
    out = jnp.zeros((H_ATT, BIAS_PAIRS, LANES), f32)
    out = out.at[:, :, 0:N_DC].set(rpb[:, 0:BIAS_PAIRS])
    return out.at[:, :, GRID_W:GRID_W + N_DC].set(rpb[:, 1:N_DR])


def _na_row_start(r, rows):
    return jnp.clip(r - NA_ROWS // 2, 0, rows - NA_ROWS)


def _na_attn_kernel(q_ref, k_ref, v_ref, kc_ref, vc_ref, rp_ref, mask_ref, g_ref, o_ref,
                    kbf_ref, vbf_ref, t2_ref, *, rows):
    b = pl.program_id(0)
    r = pl.program_id(1)

    @pl.when((b == 0) & (r == 0))
    def _():
        for h in range(H_ATT):
            for i in range(BIAS_PAIRS):
                v = jnp.broadcast_to(rp_ref[h, i:i + 1, :], (GRID_W, LANES))
                t2_ref[h * BIAS_PAIRS + i] = pltpu.roll(v, LANES - (NA_COLS - 1), 1, stride=1, stride_axis=0)

    @pl.when(r == 0)
    def _():
        _cast_rows(k_ref, kbf_ref, 256)
        _cast_rows(v_ref, vbf_ref, 256)

    nwin = NA_ROWS * GRID_W
    rs = _na_row_start(r, rows)
    start = pl.multiple_of(rs * GRID_W, GRID_W)
    i0 = rs - r + NA_ROWS - 1
    valid = mask_ref[...] != 0.0
    valid2 = jnp.concatenate([valid, valid], axis=0)
    outs = []
    for p in range(D_ATT // LANES):
        cs = slice(p * LANES, (p + 1) * LANES)
        qq = _split_heads(q_ref[:, cs] * ATT_SCALE).astype(bf16)
        kw = kbf_ref[pl.ds(start, nwin), cs]
        vw = vbf_ref[pl.ds(start, nwin), cs]
        s_lat = _dot_nt(qq, kw)
        s_ctx = _dot_nt(qq, kc_ref[:, cs].astype(bf16))
        bias2 = jnp.concatenate(
            [jnp.concatenate([t2_ref[(2 * p + hh) * BIAS_PAIRS + i0 + 2 * jp] for hh in range(2)], axis=0)
             for jp in range(NA_ROWS // 2)], axis=-1)
        s_lat = jnp.where(valid2, s_lat + bias2, NEG_INF)
        m = jnp.maximum(jnp.max(s_lat, axis=-1, keepdims=True), jnp.max(s_ctx, axis=-1, keepdims=True))
        e_lat = jnp.exp(s_lat - m)
        e_ctx = jnp.exp(s_ctx - m)
        l = jnp.sum(e_lat, axis=-1, keepdims=True) + jnp.sum(e_ctx, axis=-1, keepdims=True)
        o_ab = (_dot(e_lat.astype(bf16), vw) + _dot(e_ctx.astype(bf16), vc_ref[:, cs].astype(bf16))) / l
        outs.append(_merge_heads(o_ab))
    o_ref[...] = _rms(jnp.concatenate(outs, axis=-1), g_ref[...])


def _na_attention(q, k, v, kc, vc, rpb, gnorm, nb, seq):
    rows = seq // GRID_W
    past = kc.shape[0] // nb
    mask = _na_tables()
    return pl.pallas_call(
        functools.partial(_na_attn_kernel, rows=rows),
        out_shape=jax.ShapeDtypeStruct((nb * seq, D_ATT), f32),
        grid=(nb, rows),
        in_specs=[pl.BlockSpec((GRID_W, D_ATT), lambda b, r: (b * rows + r, 0)),
                  pl.BlockSpec((seq, D_ATT), lambda b, r: (b, 0)),
                  pl.BlockSpec((seq, D_ATT), lambda b, r: (b, 0)),
                  pl.BlockSpec((past, D_ATT), lambda b, r: (b, 0)),
                  pl.BlockSpec((past, D_ATT), lambda b, r: (b, 0)),
                  pl.BlockSpec((H_ATT, BIAS_PAIRS, LANES), lambda b, r: (0, 0, 0)),
                  pl.BlockSpec((GRID_W, NA_ROWS * GRID_W), lambda b, r: (0, 0)),
                  pl.BlockSpec((1, D_ATT), lambda b, r: (0, 0))],
        out_specs=pl.BlockSpec((GRID_W, D_ATT), lambda b, r: (b * rows + r, 0)),
        scratch_shapes=[pltpu.VMEM((seq, D_ATT), bf16), pltpu.VMEM((seq, D_ATT), bf16),
                        pltpu.VMEM((H_ATT * BIAS_PAIRS, GRID_W, LANES), f32)],
        compiler_params=_cparams(("arbitrary", "arbitrary")),
        name="na_attn",
    )(q, k, v, kc, vc, _na_bias_rows(rpb), jnp.asarray(mask), gnorm.reshape(1, D_ATT))


def _hy_front_kernel(x0_ref, x1_ref, v_ref, w0_ref, w1_ref, wv_ref, b0_ref, b1_ref, bv_ref,
                     zbf_ref, z_ref, x0c_ref):
    seq = x0_ref.shape[0]
    row = lax.broadcasted_iota(jnp.int32, x0_ref.shape, 0)
    first = row == 0
    last = row == seq - 1

    def conv(u_ref, w_ref, b_ref):
        u = u_ref[...]
        up = jnp.where(first, 0.0, pltpu.roll(u, 1, 0))
        un = jnp.where(last, 0.0, pltpu.roll(u, seq - 1, 0))
        y = b_ref[...] + up * w_ref[0:1, :]
        y = y + u * w_ref[1:2, :]
        return y + un * w_ref[2:3, :]

    z = conv(v_ref, wv_ref, bv_ref) * conv(x1_ref, w1_ref, b1_ref)
    z_ref[...] = z
    zbf_ref[...] = z.astype(bf16)
    x0c_ref[...] = conv(x0_ref, w0_ref, b0_ref)


def _hy_front(hy, conv_w, conv_b, nb, seq):
    tc = 256
    nc = D_HYENA // tc
    n = nb * D_HYENA
    cb = conv_b.reshape(1, 3 * D_HYENA)

    def part(k):
        return (pl.BlockSpec((seq, tc), lambda b, j: (b, k * nc + j)),
                pl.BlockSpec((SHORT_CONV, tc), lambda b, j: (0, k * nc + j)),
                pl.BlockSpec((1, tc), lambda b, j: (0, k * nc + j)))

    (x0s, w0s, b0s), (x1s, w1s, b1s), (vs, wvs, bvs) = part(0), part(1), part(2)
    ospec = pl.BlockSpec((seq, tc), lambda b, j: (0, b * nc + j))
    return pl.pallas_call(
        _hy_front_kernel,
        out_shape=(jax.ShapeDtypeStruct((seq, n), bf16),
                   jax.ShapeDtypeStruct((seq, n), f32),
                   jax.ShapeDtypeStruct((seq, n), f32)),
        grid=(nb, nc),
        in_specs=[x0s, x1s, vs, w0s, w1s, wvs, b0s, b1s, bvs],
        out_specs=(ospec, ospec, ospec),
        compiler_params=_cparams(("arbitrary", "arbitrary")),
        name="hyena_front",
    )(hy, hy, hy, conv_w, conv_w, conv_w, cb, cb, cb)


def _filter_features(seq):
    t = np.linspace(0.0, 1.0, seq, dtype=np.float64)[:, None]
    w = 2.0 * math.pi * np.arange(seq, dtype=np.float64)[:, None] / seq
    fb = np.linspace(1e-4, FILTER_BANDS - 1, FILTER_BANDS, dtype=np.float64)[None, :]
    ang = fb * w
    z = np.concatenate([t, np.cos(ang), -np.sin(ang)], axis=-1).astype(np.float32)
    return np.pad(z, ((0, 0), (0, LANES - EMB_DIM)))


def _filt_kernel(zf_ref, w1_ref, b1_ref, fr_ref, w2_ref, b2_ref, w3_ref, dl_ref, h_ref, kl_ref):
    i = pl.program_id(0)
    tr = zf_ref.shape[0]
    zf = zf_ref[...]
    fr = fr_ref[...]
    h = jnp.sin(fr * (_dot_hi(zf, w1_ref[...]) + b1_ref[...]))
    h = jnp.sin(fr * (_dot_hi(h, w2_ref[...]) + b2_ref[...]))
    h = _dot_hi(h, w3_ref[...])
    decay = jnp.exp(-zf[:, 0:1] * dl_ref[...])
    row = lax.broadcasted_iota(jnp.int32, (tr, D_HYENA), 0) + i * tr
    hf = h[:, :D_HYENA] * decay
    hb = jnp.where(row == 0, 0.0, h[:, D_HYENA:] * decay)
    h_ref[:, :D_HYENA] = hf.astype(bf16)
    h_ref[:, D_HYENA:] = hb.astype(bf16)
    alt = (1 - 2 * (row & 1)).astype(f32)
    part = jnp.sum(alt * (hf + hb), axis=0, keepdims=True)

    @pl.when(i == 0)
    def _():
        kl_ref[...] = jnp.zeros_like(kl_ref)

    kl_ref[...] += jnp.broadcast_to(part, kl_ref.shape)


def _hy_filters(seq, w1, b1, w2, b2, w3, freq):
    tr = 256
    zf = jnp.asarray(_filter_features(seq))
    deltas = np.abs(np.linspace(MIN_DECAY, MAX_DECAY, D_HYENA, dtype=np.float64))[None, :].astype(np.float32)
    w1p = jnp.pad(w1, ((0, LANES - EMB_DIM), (0, 0)))
    const = lambda shape: pl.BlockSpec(shape, lambda i: (0, 0))
    return pl.pallas_call(
        _filt_kernel,
        out_shape=(jax.ShapeDtypeStruct((seq, 2 * D_HYENA), bf16),
                   jax.ShapeDtypeStruct((SUBLANES, D_HYENA), f32)),
        grid=(seq // tr,),
        in_specs=[pl.BlockSpec((tr, LANES), lambda i: (i, 0)),
                  const((LANES, FILTER_FF)), const((1, FILTER_FF)), const((1, FILTER_FF)),
                  const((FILTER_FF, FILTER_FF)), const((1, FILTER_FF)),
                  const((FILTER_FF, 2 * D_HYENA)), const((1, D_HYENA))],
        out_specs=(pl.BlockSpec((tr, 2 * D_HYENA), lambda i: (i, 0)),
                   pl.BlockSpec((SUBLANES, D_HYENA), lambda i: (0, 0))),
        compiler_params=_cparams(("arbitrary",)),
        name="hyena_filters",
    )(zf, w1p, b1.reshape(1, -1), freq.reshape(1, -1), w2, b2.reshape(1, -1), w3, jnp.asarray(deltas))


def _dft_mats(seq):
    n = 2 * seq
    ph = (np.arange(seq, dtype=np.int64)[:, None] * np.arange(seq, dtype=np.int64)[None, :]) % n
    ang = ph.astype(np.float64) * (2.0 * math.pi / n)
    return np.cos(ang).astype(np.float32), np.sin(ang).astype(np.float32)


def _alt_col(rows, offset):
    row = lax.broadcasted_iota(jnp.int32, (rows, 1), 0) + offset
    return (1 - 2 * (row & 1)).astype(f32)


def _hy_fwd_kernel(fr_ref, fi_ref, z_ref, h_ref, kl_ref, yr_ref, yi_ref, yl_ref, kr_s, ki_s, *, n):
    i = pl.program_id(0)
    j = pl.program_id(1)
    tf = fr_ref.shape[0]
    tn = z_ref.shape[1]
    frb = fr_ref[...].astype(bf16)
    fib = fi_ref[...].astype(bf16)

    @pl.when(j == 0)
    def _():
        ah = _dot(frb, h_ref[...])
        bh = _dot(fib, h_ref[...])
        f = lax.broadcasted_iota(jnp.int32, (tf, 1), 0) + i * tf
        cf = jnp.where(f == 0, 1.0 / n, 2.0 / n)
        kr_s[...] = (ah[:, :D_HYENA] + ah[:, D_HYENA:]) * cf
        ki_s[...] = (bh[:, D_HYENA:] - bh[:, :D_HYENA]) * cf

    a = _dot(frb, z_ref[...])
    b = _dot(fib, z_ref[...])
    kr = kr_s[...]
    ki = ki_s[...]
    for c in range(tn // D_HYENA):
        cs = slice(c * D_HYENA, (c + 1) * D_HYENA)
        yr_ref[:, cs] = (a[:, cs] * kr + b[:, cs] * ki).astype(bf16)
        yi_ref[:, cs] = (b[:, cs] * kr - a[:, cs] * ki).astype(bf16)

    @pl.when(i == 0)
    def _():
        alt = _alt_col(z_ref.shape[0], 0)
        nz = jnp.sum(z_ref[...].astype(f32) * alt, axis=0, keepdims=True)
        kl = jnp.concatenate([kl_ref[0:1, :]] * (tn // D_HYENA), axis=-1)
        yl_ref[...] = jnp.broadcast_to(nz * kl * (1.0 / n), yl_ref.shape)


def _hy_fwd(fr, fi, zbf, hcat, kl, seq):
    n_cols = zbf.shape[1]
    tf = 256
    tn = min(n_cols, 1024)
    ni, nj = seq // tf, n_cols // tn
    assert ni == 1 or nj == 1
    return pl.pallas_call(
        functools.partial(_hy_fwd_kernel, n=2 * seq),
        out_shape=(jax.ShapeDtypeStruct((seq, n_cols), bf16),
                   jax.ShapeDtypeStruct((seq, n_cols), bf16),
                   jax.ShapeDtypeStruct((SUBLANES, n_cols), f32)),
        grid=(ni, nj),
        in_specs=[pl.BlockSpec((tf, seq), lambda i, j: (i, 0)),
                  pl.BlockSpec((tf, seq), lambda i, j: (i, 0)),
                  pl.BlockSpec((seq, tn), lambda i, j: (0, j)),
                  pl.BlockSpec((seq, 2 * D_HYENA), lambda i, j: (0, 0)),
                  pl.BlockSpec((SUBLANES, D_HYENA), lambda i, j: (0, 0))],
        out_specs=(pl.BlockSpec((tf, tn), lambda i, j: (i, j)),
                   pl.BlockSpec((tf, tn), lambda i, j: (i, j)),
                   pl.BlockSpec((SUBLANES, tn), lambda i, j: (0, j))),
        scratch_shapes=[pltpu.VMEM((tf, D_HYENA), f32), pltpu.VMEM((tf, D_HYENA), f32)],
        compiler_params=_cparams(("arbitrary", "arbitrary")),
        name="hyena_dft_fwd",
    )(fr, fi, zbf, hcat, kl)


def _hy_inv_kernel(fr_ref, fi_ref, yr_ref, yi_ref, yl_ref, z_ref, x0_ref, skip_ref, g_ref, o_ref):
    tt = fr_ref.shape[0]
    tn = yr_ref.shape[1]
    y = _dot(fr_ref[...].astype(bf16), yr_ref[...]) + _dot(fi_ref[...].astype(bf16), yi_ref[...])
    alt = _alt_col(tt, pl.program_id(0) * tt)
    for c in range(tn // D_HYENA):
        cs = slice(c * D_HYENA, (c + 1) * D_HYENA)
        yc = y[:, cs] + alt * yl_ref[0:1, cs] + z_ref[:, cs] * skip_ref[...]
        o_ref[:, cs] = _rms(yc * x0_ref[:, cs], g_ref[...])


def _hy_inv(fr, fi, yr, yi, yl, z, x0c, skip, gnorm, seq):
    n_cols = z.shape[1]
    tt = 256
    tn = min(n_cols, 1024)
    blk = pl.BlockSpec((tt, tn), lambda i, j: (i, j))
    return pl.pallas_call(
        _hy_inv_kernel,
        out_shape=jax.ShapeDtypeStruct((seq, n_cols), f32),
        grid=(seq // tt, n_cols // tn),
        in_specs=[pl.BlockSpec((tt, seq), lambda i, j: (i, 0)),
                  pl.BlockSpec((tt, seq), lambda i, j: (i, 0)),
                  pl.BlockSpec((seq, tn), lambda i, j: (0, j)),
                  pl.BlockSpec((seq, tn), lambda i, j: (0, j)),
                  pl.BlockSpec((SUBLANES, tn), lambda i, j: (0, j)),
                  blk, blk,
                  pl.BlockSpec((1, D_HYENA), lambda i, j: (0, 0)),
                  pl.BlockSpec((1, D_HYENA), lambda i, j: (0, 0))],
        out_specs=blk,
        compiler_params=_cparams(("arbitrary", "arbitrary")),
        name="hyena_dft_inv",
    )(fr, fi, yr, yi, yl, z, x0c, skip.reshape(1, D_HYENA), gnorm.reshape(1, D_HYENA))


def _hyena(hy, lp, nb, seq):
    fr_np, fi_np = _dft_mats(seq)
    fr, fi = jnp.asarray(fr_np), jnp.asarray(fi_np)
    zbf, z, x0c = _hy_front(hy, lp['conv_w'], lp['conv_b'], nb, seq)
    hcat, kl = _hy_filters(seq, lp['filt_w1'], lp['filt_b1'], lp['filt_w2'], lp['filt_b2'],
                           lp['filt_w3'], lp['filt_freq'])
    yr, yi, yl = _hy_fwd(fr, fi, zbf, hcat, kl, seq)
    return _hy_inv(fr, fi, yr, yi, yl, z, x0c, lp['hyena_skip'], lp['gnorm_hyena'], seq)


def _store_token_tiles(ref, x, pitch=SUBLANES):
    m = x.shape[0]
    for c in range(D_MODEL // LANES):
        ref[pl.ds(c, m, stride=pitch), :] = x[:, c * LANES:(c + 1) * LANES]


def _load_token_tiles(ref, m, lead=(), pitch=SUBLANES):
    return jnp.concatenate([ref[lead + (pl.ds(c, m, stride=pitch), slice(None))]
                            for c in range(D_MODEL // LANES)], axis=-1)


def _route(logits):
    lane_i = lax.broadcasted_iota(jnp.int32, logits.shape, 1)
    lane = lane_i.astype(f32)
    big = float(ROUTE_LANES)
    is_g = lane_i < N_GROUPS
    mg = jnp.max(jnp.where(is_g, logits, -jnp.inf), axis=-1, keepdims=True)
    sg = jnp.sum(jnp.where(is_g, jnp.exp(logits - mg), 0.0), axis=-1, keepdims=True)
    g_w = 1.0 / sg
    g_idx = jnp.min(jnp.where(is_g & (logits == mg), lane, big), axis=-1, keepdims=True)
    e_id = lane_i - ROUTE_EXP_LANE0
    sel = (e_id >= 0) & (e_id < N_EXPERTS) & ((e_id >> 2).astype(f32) == g_idx)
    me = jnp.max(jnp.where(sel, logits, -jnp.inf), axis=-1, keepdims=True)
    ee = jnp.where(sel, jnp.exp(logits - me), 0.0)
    prob = ee / jnp.sum(ee, axis=-1, keepdims=True)
    p1 = jnp.max(jnp.where(sel, prob, -1.0), axis=-1, keepdims=True)
    i1 = jnp.min(jnp.where(sel & (prob == p1), lane, big), axis=-1, keepdims=True)
    sel2 = sel & (lane != i1)
    p2 = jnp.max(jnp.where(sel2, prob, -1.0), axis=-1, keepdims=True)
    i2 = jnp.min(jnp.where(sel2 & (prob == p2), lane, big), axis=-1, keepdims=True)
    tot = p1 + p2
    rec = jnp.where(lane_i == ROUTE_E1, i1 - ROUTE_EXP_LANE0, 0.0)
    rec = jnp.where(lane_i == ROUTE_E2, i2 - ROUTE_EXP_LANE0, rec)
    rec = jnp.where(lane_i == ROUTE_W1, g_w * (p1 / tot), rec)
    return jnp.where(lane_i == ROUTE_W2, g_w * (p2 / tot), rec)


def _outproj_kernel(xp_ref, xs_ref, attp_ref, atts_ref, hyp_ref, hys_ref, mod_ref, wo_ref, g2_ref,
                    wr_ref, br_ref, x1_ref, tok_ref, rt_ref, wobf_ref, wrh_ref, wrl_ref, *, prompt_tiles):
    i = pl.program_id(0)

    @pl.when(i == 0)
    def _():
        _cast_rows(wo_ref, wobf_ref, 128)
        wr = wr_ref[...]
        hi = wr.astype(bf16)
        wrh_ref[...] = hi
        wrl_ref[...] = (wr - hi.astype(f32)).astype(bf16)

    is_p = i < prompt_tiles
    hyp = jnp.concatenate([hyp_ref[:, b * D_HYENA:(b + 1) * D_HYENA]
                           for b in range(hyp_ref.shape[1] // D_HYENA)], axis=0)
    x = jnp.where(is_p, xp_ref[...], xs_ref[...])
    att = jnp.where(is_p, attp_ref[...], atts_ref[...])
    hyo = jnp.where(is_p, hyp, hys_ref[...])
    proj = (_dot(att.astype(bf16), wobf_ref[0:D_ATT, :]) + _dot(hyo.astype(bf16), wobf_ref[D_ATT:, :]))
    x1 = x + mod_ref[0, GT1:GT1 + 1, :] * proj
    x1_ref[...] = x1
    h2 = _rms(x1, g2_ref[...]) * (1.0 + mod_ref[0, SC2:SC2 + 1, :]) + mod_ref[0, SH2:SH2 + 1, :]
    h2h = h2.astype(bf16)
    h2l = (h2 - h2h.astype(f32)).astype(bf16)
    logits = _dot(h2h, wrh_ref[...]) + _dot(h2l, wrh_ref[...]) + _dot(h2h, wrl_ref[...]) + br_ref[...]
    rt = _route(logits)
    rt_ref[...] = rt
    _store_token_tiles(tok_ref, h2, TOK_PITCH)
    tok_ref[pl.ds(TOK_RT_ROW, h2.shape[0], stride=TOK_PITCH), :] = rt
    for c in range(TOK_RT_ROW + 1, TOK_PITCH):
        tok_ref[pl.ds(c, h2.shape[0], stride=TOK_PITCH), :] = jnp.zeros_like(rt)


def _outproj(xp, xs, attp, atts, hyp, hys, mod, w_out, norm2_g, wr, br, seq_p, seq_s):
    tm = 512
    tp, ts = xp.shape[0], xs.shape[0]
    npt, nst = tp // tm, ts // tm
    assert tm % seq_p == 0 and seq_s % tm == 0
    spb = seq_s // tm
    bpt = tm // seq_p
    p_idx = lambda i: jnp.minimum(i, npt - 1)
    s_idx = lambda i: jnp.maximum(i - npt, 0)
    const = lambda shape: pl.BlockSpec(shape, lambda i: (0,) * len(shape))
    return pl.pallas_call(
        functools.partial(_outproj_kernel, prompt_tiles=npt),
        out_shape=(jax.ShapeDtypeStruct((tp + ts, D_MODEL), f32),
                   jax.ShapeDtypeStruct(((tp + ts) * TOK_PITCH, LANES), f32),
                   jax.ShapeDtypeStruct((tp + ts, ROUTE_LANES), f32)),
        grid=(npt + nst,),
        in_specs=[pl.BlockSpec((tm, D_MODEL), lambda i: (p_idx(i), 0)),
                  pl.BlockSpec((tm, D_MODEL), lambda i: (s_idx(i), 0)),
                  pl.BlockSpec((tm, D_ATT), lambda i: (p_idx(i), 0)),
                  pl.BlockSpec((tm, D_ATT), lambda i: (s_idx(i), 0)),
                  pl.BlockSpec((seq_p, bpt * D_HYENA), lambda i: (0, p_idx(i))),
                  pl.BlockSpec((tm, D_HYENA), lambda i: (s_idx(i) % spb, s_idx(i) // spb)),
                  pl.BlockSpec((1, MOD_ROWS, D_MODEL),
                               lambda i: (jnp.where(i < npt, 0, 1 + s_idx(i) // spb), 0, 0)),
                  const((D_MODEL, D_MODEL)), const((1, D_MODEL)),
                  const((D_MODEL, ROUTE_LANES)), const((1, ROUTE_LANES))],
        out_specs=(pl.BlockSpec((tm, D_MODEL), lambda i: (i, 0)),
                   pl.BlockSpec((tm * TOK_PITCH, LANES), lambda i: (i, 0)),
                   pl.BlockSpec((tm, ROUTE_LANES), lambda i: (i, 0))),
        scratch_shapes=[pltpu.VMEM((D_MODEL, D_MODEL), bf16),
                        pltpu.VMEM((D_MODEL, ROUTE_LANES), bf16), pltpu.VMEM((D_MODEL, ROUTE_LANES), bf16)],
        compiler_params=_cparams(("arbitrary",)),
        name="outproj_router",
    )(xp, xs, attp, atts, hyp, hys, mod, w_out, norm2_g.reshape(1, D_MODEL), wr, br)


PAIRS_PER_GROUP = 6
N_CLASSES = N_GROUPS * PAIRS_PER_GROUP
PAIR_SLOT_A = (0, 0, 0, 1, 1, 3)
PAIR_SLOT_B = (1, 2, 3, 3, 2, 2)
FLAG_NEW_A, FLAG_NEW_B, FLAG_TILE_START, FLAG_TILE_END = 1, 2, 4, 8


def _pair_tables():
    pair_of = np.zeros((EXPERTS_PER_GROUP, EXPERTS_PER_GROUP), np.int32)
    for p, (a, b) in enumerate(zip(PAIR_SLOT_A, PAIR_SLOT_B)):
        pair_of[a, b] = pair_of[b, a] = p
    cls = np.arange(N_CLASSES)
    grp = cls // PAIRS_PER_GROUP
    ea = EXPERTS_PER_GROUP * grp + np.asarray(PAIR_SLOT_A)[cls % PAIRS_PER_GROUP]
    eb = EXPERTS_PER_GROUP * grp + np.asarray(PAIR_SLOT_B)[cls % PAIRS_PER_GROUP]
    return pair_of.reshape(-1), ea.astype(np.int32), eb.astype(np.int32)


def _route_plan(e1, e2, tm, max_items):
    i32 = jnp.int32
    pair_of, cls_ea, cls_eb = (jnp.asarray(t) for t in _pair_tables())
    grp = lax.shift_right_logical(e1, 2)
    cls_tok = grp * PAIRS_PER_GROUP + pair_of[(e1 & 3) * EXPERTS_PER_GROUP + (e2 & 3)]
    classes = jnp.arange(N_CLASSES, dtype=i32)
    oh = (cls_tok[:, None] == classes[None, :]).astype(i32)
    cs = jnp.cumsum(oh, axis=0)
    counts = cs[-1]
    rank = jnp.sum(oh * (cs - 1), axis=1)
    end = jnp.cumsum(counts)
    off = end - counts
    pos = jnp.sum(oh * off[None, :], axis=1) + rank
    first_tile = off // tm
    n_tiles = jnp.where(counts > 0, (end - 1) // tm - first_tile + 1, 0)
    iend = jnp.cumsum(n_tiles)
    istart = iend - n_tiles
    n_items = iend[-1]
    slots = jnp.arange(max_items, dtype=i32)
    items = jnp.minimum(slots, n_items - 1)
    it_cls = jnp.minimum(jnp.sum((items[:, None] >= iend[None, :]).astype(i32), axis=1), N_CLASSES - 1)
    sel = (it_cls[:, None] == classes[None, :]).astype(i32)
    pick = lambda v: jnp.sum(sel * v[None, :], axis=1)
    k = items - pick(istart)
    it_tile = pick(first_tile) + k
    it_lo = jnp.maximum(pick(off) - it_tile * tm, 0)
    it_hi = jnp.minimum(pick(end) - it_tile * tm, tm)
    it_ea, it_eb = pick(cls_ea), pick(cls_eb)
    changed = lambda v: ((slots == 0) | (v != jnp.roll(v, 1))).astype(i32)
    flags = (FLAG_NEW_A * changed(it_ea) + FLAG_NEW_B * changed(it_eb)
             + FLAG_TILE_START * (it_lo == 0).astype(i32) + FLAG_TILE_END * (it_hi == tm).astype(i32))
    return dict(pos=pos, n_items=n_items.reshape(1), it_tile=it_tile, it_ea=it_ea, it_eb=it_eb,
                it_lo=it_lo, it_hi=it_hi, it_flags=flags)


ROW_DMA_UNROLL = 8


def _row_dma_loop(n, make_copy, priorities):
    assert n % ROW_DMA_UNROLL == 0

    def body(i, c):
        for u in range(ROW_DMA_UNROLL):
            make_copy(i * ROW_DMA_UNROLL + u).start(priority=priorities[u % len(priorities)])
        return c

    lax.fori_loop(0, n // ROW_DMA_UNROLL, body, 0)


def _moe_ffn_kernel(pos_ref, it_tile, it_ea, it_eb, it_lo, it_hi, it_flags, n_items, tok_hbm,
                    wga_ref, wua_ref, wda_ref, wgb_ref, wub_ref, wdb_ref, out_hbm,
                    src_s, xbuf, ybuf, x_s, rt_s, acc_s, wa_s, wb_s, sem_in, sem_out, *, chunk, n_tiles):
    i = pl.program_id(0)
    tm = x_s.shape[0]

    def gather_tile(k):
        base = k * tm
        slot = k % 2
        _row_dma_loop(tm, lambda r: pltpu.make_async_copy(
            tok_hbm.at[src_s[base + r]],
            xbuf.at[slot, pl.ds(pl.multiple_of(r * TOK_PITCH, TOK_PITCH), TOK_PITCH), :],
            sem_in.at[slot]), GATHER_PRIORITIES)

    def scatter_tile(k):
        base = k * tm
        slot = k % 2
        _row_dma_loop(tm, lambda r: pltpu.make_async_copy(
            ybuf.at[slot, pl.ds(pl.multiple_of(r * SUBLANES, SUBLANES), SUBLANES), :],
            out_hbm.at[src_s[base + r]],
            sem_out.at[slot]), SCATTER_PRIORITIES)

    def expert_ffn(x, w, wg, wu, wd):
        g = _dot(x, wg[...])
        u = _dot(x, wu[...])
        hid = (g * (1.0 / (1.0 + jnp.exp(-g)))) * u
        return _dot((hid * w).astype(bf16), wd[...])

    def wait_all(buf, sem, slot):
        pltpu.make_async_copy(buf.at[slot], buf.at[slot], sem.at[slot]).wait()

    @pl.when(i == 0)
    def _():
        def inv(a, c):
            for u in range(ROW_DMA_UNROLL):
                src_s[pos_ref[a * ROW_DMA_UNROLL + u]] = a * ROW_DMA_UNROLL + u
            return c

        lax.fori_loop(0, pos_ref.shape[0] // ROW_DMA_UNROLL, inv, 0)
        gather_tile(0)

    @pl.when(i < n_items[0])
    def _():
        flags = it_flags[i]
        k = it_tile[i]

        @pl.when((flags & FLAG_NEW_A) != 0)
        def _():
            for dst, src in zip(wa_s, (wga_ref, wua_ref, wda_ref)):
                dst[...] = src[0].astype(bf16)

        @pl.when((flags & FLAG_NEW_B) != 0)
        def _():
            for dst, src in zip(wb_s, (wgb_ref, wub_ref, wdb_ref)):
                dst[...] = src[0].astype(bf16)

        @pl.when((flags & FLAG_TILE_START) != 0)
        def _():
            @pl.when(k + 1 < n_tiles)
            def _():
                gather_tile(k + 1)

            wait_all(xbuf, sem_in, k % 2)
            x_s[...] = _load_token_tiles(xbuf, tm, (k % 2,), TOK_PITCH).astype(bf16)
            rt_s[...] = xbuf[k % 2, pl.ds(TOK_RT_ROW, tm, stride=TOK_PITCH), :]
            acc_s[...] = jnp.zeros_like(acc_s)

        lo = it_lo[i]
        hi = it_hi[i]
        ea = it_ea[i].astype(f32)

        def body(k, c):
            r = pl.multiple_of(k * chunk, chunk)

            @pl.when((r < hi) & (r + chunk > lo))
            def _():
                x = x_s[pl.ds(r, chunk), :]
                rt = rt_s[pl.ds(r, chunk), :]
                first_is_a = rt[:, ROUTE_E1:ROUTE_E1 + 1] == ea
                w1 = rt[:, ROUTE_W1:ROUTE_W1 + 1]
                w2 = rt[:, ROUTE_W2:ROUTE_W2 + 1]
                y = (expert_ffn(x, jnp.where(first_is_a, w1, w2), *wa_s)
                     + expert_ffn(x, jnp.where(first_is_a, w2, w1), *wb_s))
                row = lax.broadcasted_iota(jnp.int32, (chunk, 1), 0) + r
                mine = (row >= lo) & (row < hi)
                acc_s[pl.ds(r, chunk), :] = jnp.where(mine, y, acc_s[pl.ds(r, chunk), :])

            return c

        lax.fori_loop(0, tm // chunk, body, 0)

        @pl.when((flags & FLAG_TILE_END) != 0)
        def _():
            @pl.when(k >= 2)
            def _():
                wait_all(ybuf, sem_out, k % 2)

            _store_token_tiles(ybuf.at[k % 2], acc_s[...])
            scatter_tile(k)

            @pl.when(k == n_tiles - 1)
            def _():
                if n_tiles > 1:
                    wait_all(ybuf, sem_out, (n_tiles - 2) % 2)
                wait_all(ybuf, sem_out, (n_tiles - 1) % 2)


def _moe_ffn(plan, tok, w_gate, w_up, w_down, tm, max_items):
    n_tok = tok.shape[0] // TOK_PITCH
    n_tiles = n_tok // tm
    tok3 = tok.reshape(n_tok, TOK_PITCH, LANES)
    spec_a = lambda shape: pl.BlockSpec((1,) + shape, lambda i, ps, tl, ea, eb, lo, hi, fl, n: (ea[i], 0, 0))
    spec_b = lambda shape: pl.BlockSpec((1,) + shape, lambda i, ps, tl, ea, eb, lo, hi, fl, n: (eb[i], 0, 0))
    shapes = ((D_MODEL, D_EXPERT), (D_MODEL, D_EXPERT), (D_EXPERT, D_MODEL))
    wscratch = lambda: tuple(pltpu.VMEM(s, bf16) for s in shapes)
    return pl.pallas_call(
        functools.partial(_moe_ffn_kernel, chunk=256, n_tiles=n_tiles),
        out_shape=jax.ShapeDtypeStruct((n_tok, SUBLANES, LANES), f32),
        grid_spec=pltpu.PrefetchScalarGridSpec(
            num_scalar_prefetch=8,
            grid=(max_items,),
            in_specs=[pl.BlockSpec(memory_space=pl.ANY)] + [spec_a(s) for s in shapes] + [spec_b(s) for s in shapes],
            out_specs=pl.BlockSpec(memory_space=pl.ANY),
            scratch_shapes=[pltpu.SMEM((n_tok,), jnp.int32),
                            pltpu.VMEM((2, tm * TOK_PITCH, LANES), f32),
                            pltpu.VMEM((2, tm * SUBLANES, LANES), f32),
                            pltpu.VMEM((tm, D_MODEL), bf16), pltpu.VMEM((tm, ROUTE_LANES), f32),
                            pltpu.VMEM((tm, D_MODEL), f32), wscratch(), wscratch(),
                            pltpu.SemaphoreType.DMA((2,)), pltpu.SemaphoreType.DMA((2,))]),
        compiler_params=_cparams(("arbitrary",), 60 * 1024 * 1024),
        name="moe_ffn",
    )(plan['pos'], plan['it_tile'], plan['it_ea'], plan['it_eb'], plan['it_lo'], plan['it_hi'],
      plan['it_flags'], plan['n_items'], tok3, w_gate, w_up, w_down, w_gate, w_up, w_down)


def _final_kernel(moe_ref, x1_ref, mod_ref, fg_ref, yp_ref, yl_ref, *, prompt_tiles):
    i = pl.program_id(0)
    tm = x1_ref.shape[0]
    y = _rms(x1_ref[...] + mod_ref[0, GT2:GT2 + 1, :] * _load_token_tiles(moe_ref, tm), fg_ref[...])

    @pl.when(i < prompt_tiles)
    def _():
        yp_ref[...] = y

    @pl.when(i >= prompt_tiles)
    def _():
        yl_ref[...] = y


def _final(moe, x1, mod, final_g, t_prompt, t_lat, seq_s):
    tm = 256
    npt, nst = t_prompt // tm, t_lat // tm
    spb = seq_s // tm
    moe2 = moe.reshape(moe.shape[0] * SUBLANES, LANES)
    return pl.pallas_call(
        functools.partial(_final_kernel, prompt_tiles=npt),
        out_shape=(jax.ShapeDtypeStruct((t_prompt, D_MODEL), f32),
                   jax.ShapeDtypeStruct((t_lat, D_MODEL), f32)),
        grid=(npt + nst,),
        in_specs=[pl.BlockSpec((tm * SUBLANES, LANES), lambda i: (i, 0)),
                  pl.BlockSpec((tm, D_MODEL), lambda i: (i, 0)),
                  pl.BlockSpec((1, MOD_ROWS, D_MODEL),
                               lambda i: (jnp.where(i < npt, 0, 1 + jnp.maximum(i - npt, 0) // spb), 0, 0)),
                  pl.BlockSpec((1, D_MODEL), lambda i: (0, 0))],
        out_specs=(pl.BlockSpec((tm, D_MODEL), lambda i: (jnp.minimum(i, npt - 1), 0)),
                   pl.BlockSpec((tm, D_MODEL), lambda i: (jnp.maximum(i - npt, 0), 0))),
        compiler_params=_cparams(("arbitrary",)),
        name="moe_combine_final",
    )(moe2, x1, mod, final_g.reshape(1, D_MODEL))


def _moe(tok, rt, x1, mod, w_gate, w_up, w_down, final_g, t_prompt, t_lat, seq_s):
    tm = MOE_TM
    n_rows = t_prompt + t_lat
    assert n_rows % tm == 0
    max_items = n_rows // tm + N_CLASSES
    plan = _route_plan(rt[:, ROUTE_E1].astype(jnp.int32), rt[:, ROUTE_E2].astype(jnp.int32), tm, max_items)
    moe = _moe_ffn(plan, tok, w_gate, w_up, w_down, tm, max_items)
    return _final(moe, x1, mod, final_g, t_prompt, t_lat, seq_s)


def kernel(x_prompt, x_sample, cache_k, cache_v, c, c_ctx, w_ada, b_ada, norm1_g, w_in, rpb, conv_w, conv_b, filt_w1, filt_b1, filt_w2, filt_b2, filt_w3, filt_freq, hyena_skip, gnorm_att, gnorm_hyena, w_out, norm2_g, router_grp_w, router_grp_b, router_exp_w, router_exp_b, w_gate, w_up, w_down, final_g):
    depth = w_ada.shape[0]
    assert depth == 1
    batch, seq, _ = x_prompt.shape
    dec_batch, dec_seq, _ = x_sample.shape
    l = 0

    wr = jnp.zeros((D_MODEL, ROUTE_LANES), f32)
    wr = wr.at[:, :N_GROUPS].set(router_grp_w[l])
    wr = wr.at[:, ROUTE_EXP_LANE0:ROUTE_EXP_LANE0 + N_EXPERTS].set(router_exp_w[l])
    br = jnp.zeros((1, ROUTE_LANES), f32)
    br = br.at[0, :N_GROUPS].set(router_grp_b[l])
    br = br.at[0, ROUTE_EXP_LANE0:ROUTE_EXP_LANE0 + N_EXPERTS].set(router_exp_b[l])

    lp = {
        'norm1_g': norm1_g[l], 'w_in': w_in[l], 'conv_w': conv_w[l], 'conv_b': conv_b[l],
        'filt_w1': filt_w1[l], 'filt_b1': filt_b1[l], 'filt_w2': filt_w2[l], 'filt_b2': filt_b2[l],
        'filt_w3': filt_w3[l], 'filt_freq': filt_freq[l], 'hyena_skip': hyena_skip[l],
        'gnorm_hyena': gnorm_hyena[l], 'w_out': w_out[l], 'norm2_g': norm2_g[l],
        'wr': wr, 'br': br, 'w_gate': w_gate[l], 'w_up': w_up[l], 'w_down': w_down[l],
    }

    cond8 = jnp.zeros((SUBLANES, D_MODEL), f32).at[0].set(c_ctx).at[1:1 + dec_batch].set(c)
    mod = _ada_mod(cond8, w_ada[l], b_ada[l]).reshape(SUBLANES, N_MOD, D_MODEL)
    mod = jnp.pad(mod, ((0, 0), (0, MOD_ROWS - N_MOD), (0, 0)))
    mod_ctx, mod_lat = mod[0:1], mod[1:1 + dec_batch]

    xp = x_prompt.reshape(batch * seq, D_MODEL)
    xs = x_sample.reshape(dec_batch * dec_seq, D_MODEL)

    qp, k_ctx, v_ctx, hyp = _inproj(xp, mod_ctx, lp['norm1_g'], lp['w_in'], batch * seq)
    attp = _ctx_attention(qp, k_ctx, v_ctx, gnorm_att[l], seq)
    hyop = _hyena(hyp, lp, batch, seq)

    ql, kl, vl, hyl = _inproj(xs, mod_lat, lp['norm1_g'], lp['w_in'], dec_seq)
    kc = cache_k[:, l].reshape(dec_batch * cache_k.shape[2], D_ATT)
    vc = cache_v[:, l].reshape(dec_batch * cache_v.shape[2], D_ATT)
    attl = _na_attention(ql, kl, vl, kc, vc, rpb[l], gnorm_att[l], dec_batch, dec_seq)
    hyol = _hyena(hyl, lp, dec_batch, dec_seq)

    x1, tok, rt = _outproj(xp, xs, attp, attl, hyop, hyol, mod[0:1 + dec_batch], lp['w_out'], lp['norm2_g'],
                           lp['wr'], lp['br'], seq, dec_seq)
    yp, ys = _moe(tok, rt, x1, mod[0:1 + dec_batch], lp['w_gate'], lp['w_up'], lp['w_down'], final_g,
                  batch * seq, dec_batch * dec_seq, dec_seq)

    y_prompt = yp.reshape(batch, seq, D_MODEL)
    y_sample = ys.reshape(dec_batch, dec_seq, D_MODEL)
    new_k = k_ctx.reshape(batch, 1, seq, H_ATT, HEAD_DIM)
    new_v = v_ctx.reshape(batch, 1, seq, H_ATT, HEAD_DIM)
    return (y_prompt, y_sample, new_k, new_v)
```

```python
import functools
import math

import jax
import jax.numpy as jnp
import numpy as np
from jax import lax
from jax.experimental import pallas as pl
from jax.experimental.pallas import tpu as pltpu

f32 = jnp.float32
bf16 = jnp.bfloat16
HIGHEST = lax.Precision.HIGHEST

D_MODEL = 1024
GRID_W = 64
H_ATT = 8
HEAD_DIM = 64
D_ATT = H_ATT * HEAD_DIM
D_HYENA = 512
D_IN = 3 * D_ATT + 3 * D_HYENA
NA_ROWS = 8
NA_COLS = 16
SHORT_CONV = 3
FILTER_BANDS = 16
EMB_DIM = 1 + 2 * FILTER_BANDS
FILTER_FF = 64
DECAY_TARGET = 1e-2
MIN_DECAY = math.log(DECAY_TARGET) / 1.5
MAX_DECAY = math.log(DECAY_TARGET) / 0.3
N_GROUPS = 4
EXPERTS_PER_GROUP = 4
N_EXPERTS = N_GROUPS * EXPERTS_PER_GROUP
D_EXPERT = 512
N_MOD = 6
EPS = 1e-6
NEG_INF = -1e30
ATT_SCALE = HEAD_DIM ** -0.5

LANES = 128
SUBLANES = 8
MOD_ROWS = 8
ROUTE_LANES = 128
ROUTE_EXP_LANE0 = 16
ROUTE_E1, ROUTE_E2, ROUTE_W1, ROUTE_W2 = 0, 1, 2, 3
TOK_PITCH = 16
TOK_RT_ROW = 8
MOE_TM = 512
GATHER_PRIORITIES = (0,)
SCATTER_PRIORITIES = (1,)
VMEM_LIMIT = 56 * 1024 * 1024

SH1, SC1, GT1, SH2, SC2, GT2 = range(6)


def _cparams(sem, vmem=VMEM_LIMIT):
    return pltpu.CompilerParams(dimension_semantics=sem, vmem_limit_bytes=vmem)


def _dot(a, b):
    return jnp.dot(a, b, preferred_element_type=f32)


def _dot_hi(a, b):
    return lax.dot_general(a, b, (((1,), (0,)), ((), ())), precision=HIGHEST,
                           preferred_element_type=f32)


def _dot_nt(a, b):
    return lax.dot_general(a, b, (((1,), (1,)), ((), ())), preferred_element_type=f32)


def _rms(x, g):
    ms = jnp.mean(x * x, axis=-1, keepdims=True)
    return x * lax.rsqrt(ms + EPS) * g


def _cast_rows(src_ref, dst_ref, chunk):
    n = src_ref.shape[0] // chunk

    def body(i, c):
        r = pl.multiple_of(i * chunk, chunk)
        dst_ref[pl.ds(r, chunk), :] = src_ref[pl.ds(r, chunk), :].astype(dst_ref.dtype)
        return c

    lax.fori_loop(0, n, body, 0)


def _ada_kernel(ct_ref, w_ref, b_ref, o_ref, *, n_cond):
    ct = ct_ref[...]
    st = ct * (1.0 / (1.0 + jnp.exp(-ct)))
    w = w_ref[...]
    rid = lax.broadcasted_iota(jnp.int32, o_ref.shape, 0)
    out = jnp.broadcast_to(b_ref[...], o_ref.shape)
    for m in range(n_cond):
        row = jnp.sum(w * st[:, m:m + 1], axis=0, keepdims=True)
        out = out + jnp.where(rid == m, row, 0.0)
    o_ref[...] = out


def _ada_mod(cond8, w_ada, b_ada, n_cond):
    tn = 1536
    n = N_MOD * D_MODEL
    return pl.pallas_call(
        functools.partial(_ada_kernel, n_cond=n_cond),
        out_shape=jax.ShapeDtypeStruct((SUBLANES, n), f32),
        grid=(n // tn,),
        in_specs=[pl.BlockSpec((D_MODEL, SUBLANES), lambda j: (0, 0)),
                  pl.BlockSpec((D_MODEL, tn), lambda j: (0, j)),
                  pl.BlockSpec((1, tn), lambda j: (0, j))],
        out_specs=pl.BlockSpec((SUBLANES, tn), lambda j: (0, j)),
        compiler_params=_cparams(("arbitrary",)),
        name="ada_mod",
    )(cond8.T, w_ada, b_ada.reshape(1, n))


def _inproj_kernel(x_ref, mod_ref, g_ref, w_ref, q_ref, k_ref, v_ref, hy_ref, wbf_ref):
    @pl.when(pl.program_id(0) == 0)
    def _():
        _cast_rows(w_ref, wbf_ref, 128)

    h = _rms(x_ref[...], g_ref[...])
    h = h * (1.0 + mod_ref[0, SC1:SC1 + 1, :]) + mod_ref[0, SH1:SH1 + 1, :]
    p = _dot(h.astype(bf16), wbf_ref[...])
    q_ref[...] = p[:, 0:D_ATT]
    k_ref[...] = p[:, D_ATT:2 * D_ATT]
    v_ref[...] = p[:, 2 * D_ATT:3 * D_ATT]
    hy_ref[...] = p[:, 3 * D_ATT:]


def _inproj(x, mod, norm_g, w_in, rows_per_mod):
    t = x.shape[0]
    tm = 512
    blocks_per_mod = rows_per_mod // tm
    return pl.pallas_call(
        _inproj_kernel,
        out_shape=(jax.ShapeDtypeStruct((t, D_ATT), f32),
                   jax.ShapeDtypeStruct((t, D_ATT), f32),
                   jax.ShapeDtypeStruct((t, D_ATT), f32),
                   jax.ShapeDtypeStruct((t, 3 * D_HYENA), f32)),
        grid=(t // tm,),
        in_specs=[pl.BlockSpec((tm, D_MODEL), lambda i: (i, 0)),
                  pl.BlockSpec((1, MOD_ROWS, D_MODEL), lambda i: (i // blocks_per_mod, 0, 0)),
                  pl.BlockSpec((1, D_MODEL), lambda i: (0, 0)),
                  pl.BlockSpec((D_MODEL, D_IN), lambda i: (0, 0), pipeline_mode=pl.Buffered(1))],
        out_specs=(pl.BlockSpec((tm, D_ATT), lambda i: (i, 0)),
                   pl.BlockSpec((tm, D_ATT), lambda i: (i, 0)),
                   pl.BlockSpec((tm, D_ATT), lambda i: (i, 0)),
                   pl.BlockSpec((tm, 3 * D_HYENA), lambda i: (i, 0))),
        scratch_shapes=[pltpu.VMEM((D_MODEL, D_IN), bf16)],
        compiler_params=_cparams(("arbitrary",)),
        name="inproj",
    )(x, mod, norm_g.reshape(1, D_MODEL), w_in)


def _split_heads(q2):
    lane = lax.broadcasted_iota(jnp.int32, q2.shape, 1)
    qa = jnp.where(lane < HEAD_DIM, q2, 0.0)
    qb = jnp.where(lane >= HEAD_DIM, q2, 0.0)
    return jnp.concatenate([qa, qb], axis=0)


def _merge_heads(o_ab):
    m = o_ab.shape[0] // 2
    lane = lax.broadcasted_iota(jnp.int32, (m, LANES), 1)
    return jnp.where(lane < HEAD_DIM, o_ab[:m], o_ab[m:])


def _ctx_attn_kernel(q_ref, k_ref, v_ref, g_ref, o_ref):
    outs = []
    for p in range(D_ATT // LANES):
        cs = slice(p * LANES, (p + 1) * LANES)
        qq = _split_heads(q_ref[:, cs] * ATT_SCALE).astype(bf16)
        s = _dot_nt(qq, k_ref[:, cs].astype(bf16))
        m = jnp.max(s, axis=-1, keepdims=True)
        e = jnp.exp(s - m)
        l = jnp.sum(e, axis=-1, keepdims=True)
        o_ab = _dot(e.astype(bf16), v_ref[:, cs].astype(bf16)) / l
        outs.append(_merge_heads(o_ab))
    o_ref[...] = _rms(jnp.concatenate(outs, axis=-1), g_ref[...])


def _ctx_attention(q, k, v, gnorm, seq):
    t = q.shape[0]
    spec = pl.BlockSpec((seq, D_ATT), lambda b: (b, 0))
    return pl.pallas_call(
        _ctx_attn_kernel,
        out_shape=jax.ShapeDtypeStruct((t, D_ATT), f32),
        grid=(t // seq,),
        in_specs=[spec, spec, spec, pl.BlockSpec((1, D_ATT), lambda b: (0, 0))],
        out_specs=spec,
        compiler_params=_cparams(("arbitrary",)),
        name="ctx_attn",
    )(q, k, v, gnorm.reshape(1, D_ATT))


def _na_tables():
    col = np.arange(GRID_W)
    cs = np.clip(col - NA_COLS // 2, 0, GRID_W - NA_COLS)
    col_mask = (col[None, :] >= cs[:, None]) & (col[None, :] < cs[:, None] + NA_COLS)
    mask = np.tile(col_mask.astype(np.float32), (1, NA_ROWS))
    return mask


N_DR = 2 * NA_ROWS - 1
N_DC = 2 * NA_COLS - 1
BIAS_PAIRS = N_DR - 1


def _na_bias_rows(rpb):
    out = jnp.zeros((H_ATT, BIAS_PAIRS, LANES), f32)
    out = out.at[:, :, 0:N_DC].set(rpb[:, 0:BIAS_PAIRS])
    return out.at[:, :, GRID_W:GRID_W + N_DC].set(rpb[:, 1:N_DR])


def _na_row_start(r, rows):
    return jnp.clip(r - NA_ROWS // 2, 0, rows - NA_ROWS)


def _na_attn_kernel(q_ref, k_ref, v_ref, kc_ref, vc_ref, rp_ref, mask_ref, g_ref, o_ref,
                    kbf_ref, vbf_ref, t2_ref, *, rows):
    b = pl.program_id(0)
    r = pl.program_id(1)

    @pl.when((b == 0) & (r == 0))
    def _():
        for h in range(H_ATT):
            for i in range(BIAS_PAIRS):
                v = jnp.broadcast_to(rp_ref[h, i:i + 1, :], (GRID_W, LANES))
                t2_ref[h * BIAS_PAIRS + i] = pltpu.roll(v, LANES - (NA_COLS - 1), 1, stride=1, stride_axis=0)

    @pl.when(r == 0)
    def _():
        _cast_rows(k_ref, kbf_ref, 256)
        _cast_rows(v_ref, vbf_ref, 256)

    nwin = NA_ROWS * GRID_W
    rs = _na_row_start(r, rows)
    start = pl.multiple_of(rs * GRID_W, GRID_W)
    i0 = rs - r + NA_ROWS - 1
    valid = mask_ref[...] != 0.0
    valid2 = jnp.concatenate([valid, valid], axis=0)
    outs = []
    for p in range(D_ATT // LANES):
        cs = slice(p * LANES, (p + 1) * LANES)
        qq = _split_heads(q_ref[:, cs] * ATT_SCALE).astype(bf16)
        kw = kbf_ref[pl.ds(start, nwin), cs]
        vw = vbf_ref[pl.ds(start, nwin), cs]
        s_lat = _dot_nt(qq, kw)
        s_ctx = _dot_nt(qq, kc_ref[:, cs].astype(bf16))
        bias2 = jnp.concatenate(
            [jnp.concatenate([t2_ref[(2 * p + hh) * BIAS_PAIRS + i0 + 2 * jp] for hh in range(2)], axis=0)
             for jp in range(NA_ROWS // 2)], axis=-1)
        s_lat = jnp.where(valid2, s_lat + bias2, NEG_INF)
        m = jnp.maximum(jnp.max(s_lat, axis=-1, keepdims=True), jnp.max(s_ctx, axis=-1, keepdims=True))
        e_lat = jnp.exp(s_lat - m)
        e_ctx = jnp.exp(s_ctx - m)
        l = jnp.sum(e_lat, axis=-1, keepdims=True) + jnp.sum(e_ctx, axis=-1, keepdims=True)
        o_ab = (_dot(e_lat.astype(bf16), vw) + _dot(e_ctx.astype(bf16), vc_ref[:, cs].astype(bf16))) / l
        outs.append(_merge_heads(o_ab))
    o_ref[...] = _rms(jnp.concatenate(outs, axis=-1), g_ref[...])


def _na_attention(q, k, v, kc, vc, rpb, gnorm, nb, seq):
    rows = seq // GRID_W
    past = kc.shape[0] // nb
    mask = _na_tables()
    return pl.pallas_call(
        functools.partial(_na_attn_kernel, rows=rows),
        out_shape=jax.ShapeDtypeStruct((nb * seq, D_ATT), f32),
        grid=(nb, rows),
        in_specs=[pl.BlockSpec((GRID_W, D_ATT), lambda b, r: (b * rows + r, 0)),
                  pl.BlockSpec((seq, D_ATT), lambda b, r: (b, 0)),
                  pl.BlockSpec((seq, D_ATT), lambda b, r: (b, 0)),
                  pl.BlockSpec((past, D_ATT), lambda b, r: (b, 0)),
                  pl.BlockSpec((past, D_ATT), lambda b, r: (b, 0)),
                  pl.BlockSpec((H_ATT, BIAS_PAIRS, LANES), lambda b, r: (0, 0, 0)),
                  pl.BlockSpec((GRID_W, NA_ROWS * GRID_W), lambda b, r: (0, 0)),
                  pl.BlockSpec((1, D_ATT), lambda b, r: (0, 0))],
        out_specs=pl.BlockSpec((GRID_W, D_ATT), lambda b, r: (b * rows + r, 0)),
        scratch_shapes=[pltpu.VMEM((seq, D_ATT), bf16), pltpu.VMEM((seq, D_ATT), bf16),
                        pltpu.VMEM((H_ATT * BIAS_PAIRS, GRID_W, LANES), f32)],
        compiler_params=_cparams(("arbitrary", "arbitrary")),
        name="na_attn",
    )(q, k, v, kc, vc, _na_bias_rows(rpb), jnp.asarray(mask), gnorm.reshape(1, D_ATT))


def _hy_front_kernel(x0_ref, x1_ref, v_ref, w0_ref, w1_ref, wv_ref, b0_ref, b1_ref, bv_ref,
                     zbf_ref, z_ref, x0c_ref):
    seq = x0_ref.shape[0]
    row = lax.broadcasted_iota(jnp.int32, x0_ref.shape, 0)
    first = row == 0
    last = row == seq - 1

    def conv(u_ref, w_ref, b_ref):
        u = u_ref[...]
        up = jnp.where(first, 0.0, pltpu.roll(u, 1, 0))
        un = jnp.where(last, 0.0, pltpu.roll(u, seq - 1, 0))
        y = b_ref[...] + up * w_ref[0:1, :]
        y = y + u * w_ref[1:2, :]
        return y + un * w_ref[2:3, :]

    z = conv(v_ref, wv_ref, bv_ref) * conv(x1_ref, w1_ref, b1_ref)
    z_ref[...] = z
    zbf_ref[...] = z.astype(bf16)
    x0c_ref[...] = conv(x0_ref, w0_ref, b0_ref)


def _hy_front(hy, conv_w, conv_b, nb, seq):
    tc = D_HYENA if seq <= 512 else D_HYENA // 2
    nc = D_HYENA // tc
    n = nb * D_HYENA
    cb = conv_b.reshape(1, 3 * D_HYENA)

    def part(k):
        return (pl.BlockSpec((seq, tc), lambda b, j: (b, k * nc + j)),
                pl.BlockSpec((SHORT_CONV, tc), lambda b, j: (0, k * nc + j)),
                pl.BlockSpec((1, tc), lambda b, j: (0, k * nc + j)))

    (x0s, w0s, b0s), (x1s, w1s, b1s), (vs, wvs, bvs) = part(0), part(1), part(2)
    ospec = pl.BlockSpec((seq, tc), lambda b, j: (0, b * nc + j))
    return pl.pallas_call(
        _hy_front_kernel,
        out_shape=(jax.ShapeDtypeStruct((seq, n), bf16),
                   jax.ShapeDtypeStruct((seq, n), f32),
                   jax.ShapeDtypeStruct((seq, n), f32)),
        grid=(nb, nc),
        in_specs=[x0s, x1s, vs, w0s, w1s, wvs, b0s, b1s, bvs],
        out_specs=(ospec, ospec, ospec),
        compiler_params=_cparams(("arbitrary", "arbitrary")),
        name="hyena_front",
    )(hy, hy, hy, conv_w, conv_w, conv_w, cb, cb, cb)


def _filter_features(seq):
    t = np.linspace(0.0, 1.0, seq, dtype=np.float64)[:, None]
    w = 2.0 * math.pi * np.arange(seq, dtype=np.float64)[:, None] / seq
    fb = np.linspace(1e-4, FILTER_BANDS - 1, FILTER_BANDS, dtype=np.float64)[None, :]
    ang = fb * w
    z = np.concatenate([t, np.cos(ang), -np.sin(ang)], axis=-1).astype(np.float32)
    return np.pad(z, ((0, 0), (0, LANES - EMB_DIM)))


def _filt_kernel(zf_ref, w1_ref, b1_ref, fr_ref, w2_ref, b2_ref, w3_ref, dl_ref, h_ref, kl_ref):
    i = pl.program_id(0)
    tr = zf_ref.shape[0]
    zf = zf_ref[...]
    fr = fr_ref[...]
    h = jnp.sin(fr * (_dot_hi(zf, w1_ref[...]) + b1_ref[...]))
    h = jnp.sin(fr * (_dot_hi(h, w2_ref[...]) + b2_ref[...]))
    h = _dot_hi(h, w3_ref[...])
    decay = jnp.exp(-zf[:, 0:1] * dl_ref[...])
    row = lax.broadcasted_iota(jnp.int32, (tr, D_HYENA), 0) + i * tr
    hf = h[:, :D_HYENA] * decay
    hb = jnp.where(row == 0, 0.0, h[:, D_HYENA:] * decay)
    h_ref[:, :D_HYENA] = hf.astype(bf16)
    h_ref[:, D_HYENA:] = hb.astype(bf16)
    alt = (1 - 2 * (row & 1)).astype(f32)
    part = jnp.sum(alt * (hf + hb), axis=0, keepdims=True)

    @pl.when(i == 0)
    def _():
        kl_ref[...] = jnp.zeros_like(kl_ref)

    kl_ref[...] += jnp.broadcast_to(part, kl_ref.shape)


def _hy_filters(seq, w1, b1, w2, b2, w3, freq):
    tr = 256
    zf = jnp.asarray(_filter_features(seq))
    deltas = np.abs(np.linspace(MIN_DECAY, MAX_DECAY, D_HYENA, dtype=np.float64))[None, :].astype(np.float32)
    w1p = jnp.pad(w1, ((0, LANES - EMB_DIM), (0, 0)))
    const = lambda shape: pl.BlockSpec(shape, lambda i: (0, 0))
    return pl.pallas_call(
        _filt_kernel,
        out_shape=(jax.ShapeDtypeStruct((seq, 2 * D_HYENA), bf16),
                   jax.ShapeDtypeStruct((SUBLANES, D_HYENA), f32)),
        grid=(seq // tr,),
        in_specs=[pl.BlockSpec((tr, LANES), lambda i: (i, 0)),
                  const((LANES, FILTER_FF)), const((1, FILTER_FF)), const((1, FILTER_FF)),
                  const((FILTER_FF, FILTER_FF)), const((1, FILTER_FF)),
                  const((FILTER_FF, 2 * D_HYENA)), const((1, D_HYENA))],
        out_specs=(pl.BlockSpec((tr, 2 * D_HYENA), lambda i: (i, 0)),
                   pl.BlockSpec((SUBLANES, D_HYENA), lambda i: (0, 0))),
        compiler_params=_cparams(("arbitrary",)),
        name="hyena_filters",
    )(zf, w1p, b1.reshape(1, -1), freq.reshape(1, -1), w2, b2.reshape(1, -1), w3, jnp.asarray(deltas))


def _dft_mats(seq):
    n = 2 * seq
    ph = (np.arange(seq, dtype=np.int64)[:, None] * np.arange(seq, dtype=np.int64)[None, :]) % n
    ang = ph.astype(np.float64) * (2.0 * math.pi / n)
    return np.cos(ang).astype(np.float32), np.sin(ang).astype(np.float32)


def _alt_col(rows, offset):
    row = lax.broadcasted_iota(jnp.int32, (rows, 1), 0) + offset
    return (1 - 2 * (row & 1)).astype(f32)


def _hy_fwd_kernel(fr_ref, fi_ref, z_ref, h_ref, kl_ref, yr_ref, yi_ref, yl_ref, kr_s, ki_s, *, n):
    i = pl.program_id(0)
    j = pl.program_id(1)
    tf = fr_ref.shape[0]
    tn = z_ref.shape[1]
    frb = fr_ref[...].astype(bf16)
    fib = fi_ref[...].astype(bf16)

    @pl.when(j == 0)
    def _():
        ah = _dot(frb, h_ref[...])
        bh = _dot(fib, h_ref[...])
        f = lax.broadcasted_iota(jnp.int32, (tf, 1), 0) + i * tf
        cf = jnp.where(f == 0, 1.0 / n, 2.0 / n)
        kr_s[...] = (ah[:, :D_HYENA] + ah[:, D_HYENA:]) * cf
        ki_s[...] = (bh[:, D_HYENA:] - bh[:, :D_HYENA]) * cf

    a = _dot(frb, z_ref[...])
    b = _dot(fib, z_ref[...])
    kr = kr_s[...]
    ki = ki_s[...]
    for c in range(tn // D_HYENA):
        cs = slice(c * D_HYENA, (c + 1) * D_HYENA)
        yr_ref[:, cs] = (a[:, cs] * kr + b[:, cs] * ki).astype(bf16)
        yi_ref[:, cs] = (b[:, cs] * kr - a[:, cs] * ki).astype(bf16)

    @pl.when(i == 0)
    def _():
        alt = _alt_col(z_ref.shape[0], 0)
        nz = jnp.sum(z_ref[...].astype(f32) * alt, axis=0, keepdims=True)
        kl = jnp.concatenate([kl_ref[0:1, :]] * (tn // D_HYENA), axis=-1)
        yl_ref[...] = jnp.broadcast_to(nz * kl * (1.0 / n), yl_ref.shape)


def _hy_fwd(fr, fi, zbf, hcat, kl, seq):
    n_cols = zbf.shape[1]
    tf = 256
    tn = min(n_cols, 2048)
    ni, nj = seq // tf, n_cols // tn
    assert ni == 1 or nj == 1
    return pl.pallas_call(
        functools.partial(_hy_fwd_kernel, n=2 * seq),
        out_shape=(jax.ShapeDtypeStruct((seq, n_cols), bf16),
                   jax.ShapeDtypeStruct((seq, n_cols), bf16),
                   jax.ShapeDtypeStruct((SUBLANES, n_cols), f32)),
        grid=(ni, nj),
        in_specs=[pl.BlockSpec((tf, seq), lambda i, j: (i, 0)),
                  pl.BlockSpec((tf, seq), lambda i, j: (i, 0)),
                  pl.BlockSpec((seq, tn), lambda i, j: (0, j)),
                  pl.BlockSpec((seq, 2 * D_HYENA), lambda i, j: (0, 0)),
                  pl.BlockSpec((SUBLANES, D_HYENA), lambda i, j: (0, 0))],
        out_specs=(pl.BlockSpec((tf, tn), lambda i, j: (i, j)),
                   pl.BlockSpec((tf, tn), lambda i, j: (i, j)),
                   pl.BlockSpec((SUBLANES, tn), lambda i, j: (0, j))),
        scratch_shapes=[pltpu.VMEM((tf, D_HYENA), f32), pltpu.VMEM((tf, D_HYENA), f32)],
        compiler_params=_cparams(("arbitrary", "arbitrary")),
        name="hyena_dft_fwd",
    )(fr, fi, zbf, hcat, kl)


def _hy_inv_kernel(fr_ref, fi_ref, yr_ref, yi_ref, yl_ref, z_ref, x0_ref, skip_ref, g_ref, o_ref):
    tt = fr_ref.shape[0]
    tn = yr_ref.shape[1]
    y = _dot(fr_ref[...].astype(bf16), yr_ref[...]) + _dot(fi_ref[...].astype(bf16), yi_ref[...])
    alt = _alt_col(tt, pl.program_id(0) * tt)
    for c in range(tn // D_HYENA):
        cs = slice(c * D_HYENA, (c + 1) * D_HYENA)
        yc = y[:, cs] + alt * yl_ref[0:1, cs] + z_ref[:, cs] * skip_ref[...]
        o_ref[:, cs] = _rms(yc * x0_ref[:, cs], g_ref[...])


def _hy_inv(fr, fi, yr, yi, yl, z, x0c, skip, gnorm, seq):
    n_cols = z.shape[1]
    tt = 256
    tn = min(n_cols, 2048)
    blk = pl.BlockSpec((tt, tn), lambda i, j: (i, j))
    return pl.pallas_call(
        _hy_inv_kernel,
        out_shape=jax.ShapeDtypeStruct((seq, n_cols), f32),
        grid=(seq // tt, n_cols // tn),
        in_specs=[pl.BlockSpec((tt, seq), lambda i, j: (i, 0)),
                  pl.BlockSpec((tt, seq), lambda i, j: (i, 0)),
                  pl.BlockSpec((seq, tn), lambda i, j: (0, j)),
                  pl.BlockSpec((seq, tn), lambda i, j: (0, j)),
                  pl.BlockSpec((SUBLANES, tn), lambda i, j: (0, j)),
                  blk, blk,
                  pl.BlockSpec((1, D_HYENA), lambda i, j: (0, 0)),
                  pl.BlockSpec((1, D_HYENA), lambda i, j: (0, 0))],
        out_specs=blk,
        compiler_params=_cparams(("arbitrary", "arbitrary")),
        name="hyena_dft_inv",
    )(fr, fi, yr, yi, yl, z, x0c, skip.reshape(1, D_HYENA), gnorm.reshape(1, D_HYENA))


def _hyena(hy, lp, nb, seq):
    fr_np, fi_np = _dft_mats(seq)
    fr, fi = jnp.asarray(fr_np), jnp.asarray(fi_np)
    zbf, z, x0c = _hy_front(hy, lp['conv_w'], lp['conv_b'], nb, seq)
    hcat, kl = _hy_filters(seq, lp['filt_w1'], lp['filt_b1'], lp['filt_w2'], lp['filt_b2'],
                           lp['filt_w3'], lp['filt_freq'])
    yr, yi, yl = _hy_fwd(fr, fi, zbf, hcat, kl, seq)
    return _hy_inv(fr, fi, yr, yi, yl, z, x0c, lp['hyena_skip'], lp['gnorm_hyena'], seq)


def _store_token_tiles(ref, x, pitch=SUBLANES):
    m = x.shape[0]
    for c in range(D_MODEL // LANES):
        ref[pl.ds(c, m, stride=pitch), :] = x[:, c * LANES:(c + 1) * LANES]


def _load_token_tiles(ref, m, lead=(), pitch=SUBLANES):
    return jnp.concatenate([ref[lead + (pl.ds(c, m, stride=pitch), slice(None))]
                            for c in range(D_MODEL // LANES)], axis=-1)


def _route(logits):
    lane_i = lax.broadcasted_iota(jnp.int32, logits.shape, 1)
    lane = lane_i.astype(f32)
    big = float(ROUTE_LANES)
    is_g = lane_i < N_GROUPS
    mg = jnp.max(jnp.where(is_g, logits, -jnp.inf), axis=-1, keepdims=True)
    sg = jnp.sum(jnp.where(is_g, jnp.exp(logits - mg), 0.0), axis=-1, keepdims=True)
    g_w = 1.0 / sg
    g_idx = jnp.min(jnp.where(is_g & (logits == mg), lane, big), axis=-1, keepdims=True)
    e_id = lane_i - ROUTE_EXP_LANE0
    sel = (e_id >= 0) & (e_id < N_EXPERTS) & ((e_id >> 2).astype(f32) == g_idx)
    me = jnp.max(jnp.where(sel, logits, -jnp.inf), axis=-1, keepdims=True)
    ee = jnp.where(sel, jnp.exp(logits - me), 0.0)
    prob = ee / jnp.sum(ee, axis=-1, keepdims=True)
    p1 = jnp.max(jnp.where(sel, prob, -1.0), axis=-1, keepdims=True)
    i1 = jnp.min(jnp.where(sel & (prob == p1), lane, big), axis=-1, keepdims=True)
    sel2 = sel & (lane != i1)
    p2 = jnp.max(jnp.where(sel2, prob, -1.0), axis=-1, keepdims=True)
    i2 = jnp.min(jnp.where(sel2 & (prob == p2), lane, big), axis=-1, keepdims=True)
    tot = p1 + p2
    rec = jnp.where(lane_i == ROUTE_E1, i1 - ROUTE_EXP_LANE0, 0.0)
    rec = jnp.where(lane_i == ROUTE_E2, i2 - ROUTE_EXP_LANE0, rec)
    rec = jnp.where(lane_i == ROUTE_W1, g_w * (p1 / tot), rec)
    return jnp.where(lane_i == ROUTE_W2, g_w * (p2 / tot), rec)


def _outproj_kernel(xp_ref, xs_ref, attp_ref, atts_ref, hyp_ref, hys_ref, mod_ref, wo_ref, g2_ref,
                    wr_ref, br_ref, x1_ref, tok_ref, rt_ref, wobf_ref, wrh_ref, wrl_ref, *, prompt_tiles):
    i = pl.program_id(0)

    @pl.when(i == 0)
    def _():
        _cast_rows(wo_ref, wobf_ref, 128)
        wr = wr_ref[...]
        hi = wr.astype(bf16)
        wrh_ref[...] = hi
        wrl_ref[...] = (wr - hi.astype(f32)).astype(bf16)

    is_p = i < prompt_tiles
    hyp = jnp.concatenate([hyp_ref[:, b * D_HYENA:(b + 1) * D_HYENA]
                           for b in range(hyp_ref.shape[1] // D_HYENA)], axis=0)
    x = jnp.where(is_p, xp_ref[...], xs_ref[...])
    att = jnp.where(is_p, attp_ref[...], atts_ref[...])
    hyo = jnp.where(is_p, hyp, hys_ref[...])
    proj = (_dot(att.astype(bf16), wobf_ref[0:D_ATT, :]) + _dot(hyo.astype(bf16), wobf_ref[D_ATT:, :]))
    x1 = x + mod_ref[0, GT1:GT1 + 1, :] * proj
    x1_ref[...] = x1
    h2 = _rms(x1, g2_ref[...]) * (1.0 + mod_ref[0, SC2:SC2 + 1, :]) + mod_ref[0, SH2:SH2 + 1, :]
    h2h = h2.astype(bf16)
    h2l = (h2 - h2h.astype(f32)).astype(bf16)
    logits = _dot(h2h, wrh_ref[...]) + _dot(h2l, wrh_ref[...]) + _dot(h2h, wrl_ref[...]) + br_ref[...]
    rt = _route(logits)
    rt_ref[...] = rt
    tok_ref[...] = jnp.zeros_like(tok_ref)
    _store_token_tiles(tok_ref, h2, TOK_PITCH)
    tok_ref[pl.ds(TOK_RT_ROW, h2.shape[0], stride=TOK_PITCH), :] = rt


def _outproj(xp, xs, attp, atts, hyp, hys, mod, w_out, norm2_g, wr, br, seq_p, seq_s):
    tm = 512
    tp, ts = xp.shape[0], xs.shape[0]
    npt, nst = tp // tm, ts // tm
    assert tm % seq_p == 0 and seq_s % tm == 0
    spb = seq_s // tm
    bpt = tm // seq_p
    p_idx = lambda i: jnp.minimum(i, npt - 1)
    s_idx = lambda i: jnp.maximum(i - npt, 0)
    const = lambda shape: pl.BlockSpec(shape, lambda i: (0,) * len(shape))
    return pl.pallas_call(
        functools.partial(_outproj_kernel, prompt_tiles=npt),
        out_shape=(jax.ShapeDtypeStruct((tp + ts, D_MODEL), f32),
                   jax.ShapeDtypeStruct(((tp + ts) * TOK_PITCH, LANES), f32),
                   jax.ShapeDtypeStruct((tp + ts, ROUTE_LANES), f32)),
        grid=(npt + nst,),
        in_specs=[pl.BlockSpec((tm, D_MODEL), lambda i: (p_idx(i), 0)),
                  pl.BlockSpec((tm, D_MODEL), lambda i: (s_idx(i), 0)),
                  pl.BlockSpec((tm, D_ATT), lambda i: (p_idx(i), 0)),
                  pl.BlockSpec((tm, D_ATT), lambda i: (s_idx(i), 0)),
                  pl.BlockSpec((seq_p, bpt * D_HYENA), lambda i: (0, p_idx(i))),
                  pl.BlockSpec((tm, D_HYENA), lambda i: (s_idx(i) % spb, s_idx(i) // spb)),
                  pl.BlockSpec((1, MOD_ROWS, D_MODEL),
                               lambda i: (jnp.where(i < npt, 0, 1 + s_idx(i) // spb), 0, 0)),
                  const((D_MODEL, D_MODEL)), const((1, D_MODEL)),
                  const((D_MODEL, ROUTE_LANES)), const((1, ROUTE_LANES))],
        out_specs=(pl.BlockSpec((tm, D_MODEL), lambda i: (i, 0)),
                   pl.BlockSpec((tm * TOK_PITCH, LANES), lambda i: (i, 0)),
                   pl.BlockSpec((tm, ROUTE_LANES), lambda i: (i, 0))),
        scratch_shapes=[pltpu.VMEM((D_MODEL, D_MODEL), bf16),
                        pltpu.VMEM((D_MODEL, ROUTE_LANES), bf16), pltpu.VMEM((D_MODEL, ROUTE_LANES), bf16)],
        compiler_params=_cparams(("arbitrary",)),
        name="outproj_router",
    )(xp, xs, attp, atts, hyp, hys, mod, w_out, norm2_g.reshape(1, D_MODEL), wr, br)


PAIRS_PER_GROUP = 6
N_CLASSES = N_GROUPS * PAIRS_PER_GROUP
PAIR_SLOT_A = (0, 0, 0, 1, 1, 3)
PAIR_SLOT_B = (1, 2, 3, 3, 2, 2)
FLAG_NEW_A, FLAG_NEW_B, FLAG_TILE_START, FLAG_TILE_END = 1, 2, 4, 8


def _pair_tables():
    pair_of = np.zeros((EXPERTS_PER_GROUP, EXPERTS_PER_GROUP), np.int32)
    for p, (a, b) in enumerate(zip(PAIR_SLOT_A, PAIR_SLOT_B)):
        pair_of[a, b] = pair_of[b, a] = p
    cls = np.arange(N_CLASSES)
    grp = cls // PAIRS_PER_GROUP
    ea = EXPERTS_PER_GROUP * grp + np.asarray(PAIR_SLOT_A)[cls % PAIRS_PER_GROUP]
    eb = EXPERTS_PER_GROUP * grp + np.asarray(PAIR_SLOT_B)[cls % PAIRS_PER_GROUP]
    return pair_of.reshape(-1), ea.astype(np.int32), eb.astype(np.int32)


def _route_plan(e1, e2, tm, max_items):
    i32 = jnp.int32
    pair_of, cls_ea, cls_eb = (jnp.asarray(t) for t in _pair_tables())
    grp = lax.shift_right_logical(e1, 2)
    cls_tok = grp * PAIRS_PER_GROUP + pair_of[(e1 & 3) * EXPERTS_PER_GROUP + (e2 & 3)]
    classes = jnp.arange(N_CLASSES, dtype=i32)
    oh = (cls_tok[:, None] == classes[None, :]).astype(i32)
    cs = jnp.cumsum(oh, axis=0)
    counts = cs[-1]
    rank = jnp.sum(oh * (cs - 1), axis=1)
    end = jnp.cumsum(counts)
    off = end - counts
    pos = jnp.sum(oh * off[None, :], axis=1) + rank
    first_tile = off // tm
    n_tiles = jnp.where(counts > 0, (end - 1) // tm - first_tile + 1, 0)
    iend = jnp.cumsum(n_tiles)
    istart = iend - n_tiles
    n_items = iend[-1]
    slots = jnp.arange(max_items, dtype=i32)
    items = jnp.minimum(slots, n_items - 1)
    it_cls = jnp.minimum(jnp.sum((items[:, None] >= iend[None, :]).astype(i32), axis=1), N_CLASSES - 1)
    sel = (it_cls[:, None] == classes[None, :]).astype(i32)
    pick = lambda v: jnp.sum(sel * v[None, :], axis=1)
    k = items - pick(istart)
    it_tile = pick(first_tile) + k
    it_lo = jnp.maximum(pick(off) - it_tile * tm, 0)
    it_hi = jnp.minimum(pick(end) - it_tile * tm, tm)
    it_ea, it_eb = pick(cls_ea), pick(cls_eb)
    changed = lambda v: ((slots == 0) | (v != jnp.roll(v, 1))).astype(i32)
    flags = (FLAG_NEW_A * changed(it_ea) + FLAG_NEW_B * changed(it_eb)
             + FLAG_TILE_START * (it_lo == 0).astype(i32) + FLAG_TILE_END * (it_hi == tm).astype(i32))
    return dict(pos=pos, n_items=n_items.reshape(1), it_tile=it_tile, it_ea=it_ea, it_eb=it_eb,
                it_lo=it_lo, it_hi=it_hi, it_flags=flags)


ROW_DMA_UNROLL = 8


def _row_dma_loop(g0, g1, make_copy, priorities):
    def body(i, c):
        for u in range(ROW_DMA_UNROLL):
            make_copy(i * ROW_DMA_UNROLL + u).start(priority=priorities[u % len(priorities)])
        return c

    lax.fori_loop(g0, g1, body, 0)


def _moe_ffn_kernel(pos_ref, it_tile, it_ea, it_eb, it_lo, it_hi, it_flags, n_items, tok_hbm,
                    wga_ref, wua_ref, wda_ref, wgb_ref, wub_ref, wdb_ref, out_hbm,
                    src_s, xbuf, ybuf, x_s, rt_s, acc_s, wa_s, wb_s, sem_in, sem_out, *, chunk, n_tiles):
    i = pl.program_id(0)
    tm = x_s.shape[0]
    groups = tm // ROW_DMA_UNROLL

    def gather_tile(k):
        base = k * tm
        slot = k % 2
        _row_dma_loop(0, groups, lambda r: pltpu.make_async_copy(
            tok_hbm.at[src_s[base + r]],
            xbuf.at[slot, pl.ds(pl.multiple_of(r * TOK_PITCH, TOK_PITCH), TOK_PITCH), :],
            sem_in.at[slot]), GATHER_PRIORITIES)

    def scatter_tile(k):
        base = k * tm
        slot = k % 2
        _row_dma_loop(0, groups, lambda r: pltpu.make_async_copy(
            ybuf.at[slot, pl.ds(pl.multiple_of(r * SUBLANES, SUBLANES), SUBLANES), :],
            out_hbm.at[src_s[base + r]],
            sem_out.at[slot]), SCATTER_PRIORITIES)

    def expert_ffn(x, w, wg, wu, wd):
        g = _dot(x, wg[...])
        u = _dot(x, wu[...])
        hid = (g * (1.0 / (1.0 + jnp.exp(-g)))) * u
        return _dot((hid * w).astype(bf16), wd[...])

    def wait_all(buf, sem, slot):
        pltpu.make_async_copy(buf.at[slot], buf.at[slot], sem.at[slot]).wait()

    @pl.when(i == 0)
    def _():
        def inv(a, c):
            for u in range(ROW_DMA_UNROLL):
                src_s[pos_ref[a * ROW_DMA_UNROLL + u]] = a * ROW_DMA_UNROLL + u
            return c

        lax.fori_loop(0, pos_ref.shape[0] // ROW_DMA_UNROLL, inv, 0)
        gather_tile(0)

    @pl.when(i < n_items[0])
    def _():
        flags = it_flags[i]
        k = it_tile[i]

        @pl.when((flags & FLAG_NEW_A) != 0)
        def _():
            for dst, src in zip(wa_s, (wga_ref, wua_ref, wda_ref)):
                dst[...] = src[0].astype(bf16)

        @pl.when((flags & FLAG_NEW_B) != 0)
        def _():
            for dst, src in zip(wb_s, (wgb_ref, wub_ref, wdb_ref)):
                dst[...] = src[0].astype(bf16)

        @pl.when((flags & FLAG_TILE_START) != 0)
        def _():
            @pl.when(k + 1 < n_tiles)
            def _():
                gather_tile(k + 1)

            wait_all(xbuf, sem_in, k % 2)
            x_s[...] = _load_token_tiles(xbuf, tm, (k % 2,), TOK_PITCH).astype(bf16)
            rt_s[...] = xbuf[k % 2, pl.ds(TOK_RT_ROW, tm, stride=TOK_PITCH), :]
            acc_s[...] = jnp.zeros_like(acc_s)

        lo = it_lo[i]
        hi = it_hi[i]
        ea = it_ea[i].astype(f32)

        def body(j, c):
            r = pl.multiple_of(j * chunk, chunk)

            @pl.when((r < hi) & (r + chunk > lo))
            def _():
                x = x_s[pl.ds(r, chunk), :]
                rt = rt_s[pl.ds(r, chunk), :]
                first_is_a = rt[:, ROUTE_E1:ROUTE_E1 + 1] == ea
                w1 = rt[:, ROUTE_W1:ROUTE_W1 + 1]
                w2 = rt[:, ROUTE_W2:ROUTE_W2 + 1]
                y = (expert_ffn(x, jnp.where(first_is_a, w1, w2), *wa_s)
                     + expert_ffn(x, jnp.where(first_is_a, w2, w1), *wb_s))
                row = lax.broadcasted_iota(jnp.int32, (chunk, 1), 0) + r
                mine = (row >= lo) & (row < hi)
                acc_s[pl.ds(r, chunk), :] = jnp.where(mine, y, acc_s[pl.ds(r, chunk), :])

            return c

        lax.fori_loop(0, tm // chunk, body, 0)

        @pl.when((flags & FLAG_TILE_END) != 0)
        def _():
            @pl.when(k >= 2)
            def _():
                wait_all(ybuf, sem_out, k % 2)

            _store_token_tiles(ybuf.at[k % 2], acc_s[...])
            scatter_tile(k)

            @pl.when(k == n_tiles - 1)
            def _():
                if n_tiles > 1:
                    wait_all(ybuf, sem_out, (n_tiles - 2) % 2)
                wait_all(ybuf, sem_out, (n_tiles - 1) % 2)


def _moe_ffn(plan, tok, w_gate, w_up, w_down, tm, max_items):
    n_tok = tok.shape[0] // TOK_PITCH
    n_tiles = n_tok // tm
    tok3 = tok.reshape(n_tok, TOK_PITCH, LANES)
    spec_a = lambda shape: pl.BlockSpec((1,) + shape, lambda i, ps, tl, ea, eb, lo, hi, fl, n: (ea[i], 0, 0))
    spec_b = lambda shape: pl.BlockSpec((1,) + shape, lambda i, ps, tl, ea, eb, lo, hi, fl, n: (eb[i], 0, 0))
    shapes = ((D_MODEL, D_EXPERT), (D_MODEL, D_EXPERT), (D_EXPERT, D_MODEL))
    wscratch = lambda: tuple(pltpu.VMEM(s, bf16) for s in shapes)
    return pl.pallas_call(
        functools.partial(_moe_ffn_kernel, chunk=256, n_tiles=n_tiles),
        out_shape=jax.ShapeDtypeStruct((n_tok, SUBLANES, LANES), f32),
        grid_spec=pltpu.PrefetchScalarGridSpec(
            num_scalar_prefetch=8,
            grid=(max_items,),
            in_specs=[pl.BlockSpec(memory_space=pl.ANY)] + [spec_a(s) for s in shapes] + [spec_b(s) for s in shapes],
            out_specs=pl.BlockSpec(memory_space=pl.ANY),
            scratch_shapes=[pltpu.SMEM((n_tok,), jnp.int32),
                            pltpu.VMEM((2, tm * TOK_PITCH, LANES), f32),
                            pltpu.VMEM((2, tm * SUBLANES, LANES), f32),
                            pltpu.VMEM((tm, D_MODEL), bf16), pltpu.VMEM((tm, ROUTE_LANES), f32),
                            pltpu.VMEM((tm, D_MODEL), f32), wscratch(), wscratch(),
                            pltpu.SemaphoreType.DMA((2,)), pltpu.SemaphoreType.DMA((2,))]),
        compiler_params=_cparams(("arbitrary",), 60 * 1024 * 1024),
        name="moe_ffn",
    )(plan['pos'], plan['it_tile'], plan['it_ea'], plan['it_eb'], plan['it_lo'], plan['it_hi'],
      plan['it_flags'], plan['n_items'], tok3, w_gate, w_up, w_down, w_gate, w_up, w_down)


def _final_kernel(moe_ref, x1_ref, mod_ref, fg_ref, yp_ref, yl_ref, *, prompt_tiles):
    i = pl.program_id(0)
    tm = x1_ref.shape[0]
    y = _rms(x1_ref[...] + mod_ref[0, GT2:GT2 + 1, :] * _load_token_tiles(moe_ref, tm), fg_ref[...])

    @pl.when(i < prompt_tiles)
    def _():
        yp_ref[...] = y

    @pl.when(i >= prompt_tiles)
    def _():
        yl_ref[...] = y


def _final(moe, x1, mod, final_g, t_prompt, t_lat, seq_s):
    tm = 256
    npt, nst = t_prompt // tm, t_lat // tm
    spb = seq_s // tm
    moe2 = moe.reshape(moe.shape[0] * SUBLANES, LANES)
    return pl.pallas_call(
        functools.partial(_final_kernel, prompt_tiles=npt),
        out_shape=(jax.ShapeDtypeStruct((t_prompt, D_MODEL), f32),
                   jax.ShapeDtypeStruct((t_lat, D_MODEL), f32)),
        grid=(npt + nst,),
        in_specs=[pl.BlockSpec((tm * SUBLANES, LANES), lambda i: (i, 0)),
                  pl.BlockSpec((tm, D_MODEL), lambda i: (i, 0)),
                  pl.BlockSpec((1, MOD_ROWS, D_MODEL),
                               lambda i: (jnp.where(i < npt, 0, 1 + jnp.maximum(i - npt, 0) // spb), 0, 0)),
                  pl.BlockSpec((1, D_MODEL), lambda i: (0, 0))],
        out_specs=(pl.BlockSpec((tm, D_MODEL), lambda i: (jnp.minimum(i, npt - 1), 0)),
                   pl.BlockSpec((tm, D_MODEL), lambda i: (jnp.maximum(i - npt, 0), 0))),
        compiler_params=_cparams(("arbitrary",)),
        name="moe_combine_final",
    )(moe2, x1, mod, final_g.reshape(1, D_MODEL))


def _moe(tok, rt, x1, mod, w_gate, w_up, w_down, final_g, t_prompt, t_lat, seq_s):
    tm = MOE_TM
    n_rows = t_prompt + t_lat
    assert n_rows % tm == 0
    max_items = n_rows // tm + N_CLASSES
    plan = _route_plan(rt[:, ROUTE_E1].astype(jnp.int32), rt[:, ROUTE_E2].astype(jnp.int32), tm, max_items)
    moe = _moe_ffn(plan, tok, w_gate, w_up, w_down, tm, max_items)
    return _final(moe, x1, mod, final_g, t_prompt, t_lat, seq_s)


def kernel(x_prompt, x_sample, cache_k, cache_v, c, c_ctx, w_ada, b_ada, norm1_g, w_in, rpb, conv_w, conv_b, filt_w1, filt_b1, filt_w2, filt_b2, filt_w3, filt_freq, hyena_skip, gnorm_att, gnorm_hyena, w_out, norm2_g, router_grp_w, router_grp_b, router_exp_w, router_exp_b, w_gate, w_up, w_down, final_g):
    depth = w_ada.shape[0]
    assert depth == 1
    batch, seq, _ = x_prompt.shape
    dec_batch, dec_seq, _ = x_sample.shape
    l = 0

    wr = jnp.zeros((D_MODEL, ROUTE_LANES), f32)
    wr = wr.at[:, :N_GROUPS].set(router_grp_w[l])
    wr = wr.at[:, ROUTE_EXP_LANE0:ROUTE_EXP_LANE0 + N_EXPERTS].set(router_exp_w[l])
    br = jnp.zeros((1, ROUTE_LANES), f32)
    br = br.at[0, :N_GROUPS].set(router_grp_b[l])
    br = br.at[0, ROUTE_EXP_LANE0:ROUTE_EXP_LANE0 + N_EXPERTS].set(router_exp_b[l])

    lp = {
        'norm1_g': norm1_g[l], 'w_in': w_in[l], 'conv_w': conv_w[l], 'conv_b': conv_b[l],
        'filt_w1': filt_w1[l], 'filt_b1': filt_b1[l], 'filt_w2': filt_w2[l], 'filt_b2': filt_b2[l],
        'filt_w3': filt_w3[l], 'filt_freq': filt_freq[l], 'hyena_skip': hyena_skip[l],
        'gnorm_hyena': gnorm_hyena[l], 'w_out': w_out[l], 'norm2_g': norm2_g[l],
        'wr': wr, 'br': br, 'w_gate': w_gate[l], 'w_up': w_up[l], 'w_down': w_down[l],
    }

    cond8 = jnp.zeros((SUBLANES, D_MODEL), f32).at[0].set(c_ctx).at[1:1 + dec_batch].set(c)
    mod = _ada_mod(cond8, w_ada[l], b_ada[l], 1 + dec_batch).reshape(SUBLANES, N_MOD, D_MODEL)
    mod = jnp.pad(mod, ((0, 0), (0, MOD_ROWS - N_MOD), (0, 0)))
    mod_ctx, mod_lat = mod[0:1], mod[1:1 + dec_batch]

    xp = x_prompt.reshape(batch * seq, D_MODEL)
    xs = x_sample.reshape(dec_batch * dec_seq, D_MODEL)

    qp, k_ctx, v_ctx, hyp = _inproj(xp, mod_ctx, lp['norm1_g'], lp['w_in'], batch * seq)
    attp = _ctx_attention(qp, k_ctx, v_ctx, gnorm_att[l], seq)
    hyop = _hyena(hyp, lp, batch, seq)

    ql, kl, vl, hyl = _inproj(xs, mod_lat, lp['norm1_g'], lp['w_in'], dec_seq)
    kc = cache_k[:, l].reshape(dec_batch * cache_k.shape[2], D_ATT)
    vc = cache_v[:, l].reshape(dec_batch * cache_v.shape[2], D_ATT)
    attl = _na_attention(ql, kl, vl, kc, vc, rpb[l], gnorm_att[l], dec_batch, dec_seq)
    hyol = _hyena(hyl, lp, dec_batch, dec_seq)

    x1, tok, rt = _outproj(xp, xs, attp, attl, hyop, hyol, mod[0:1 + dec_batch], lp['w_out'], lp['norm2_g'],
                           lp['wr'], lp['br'], seq, dec_seq)
    yp, ys = _moe(tok, rt, x1, mod[0:1 + dec_batch], lp['w_gate'], lp['w_up'], lp['w_down'], final_g,
                  batch * seq, dec_batch * dec_seq, dec_seq)

    y_prompt = yp.reshape(batch, seq, D_MODEL)
    y_sample = ys.reshape(dec_batch, dec_seq, D_MODEL)
    new_k = k_ctx.reshape(batch, 1, seq, H_ATT, HEAD_DIM)
    new_v = v_ctx.reshape(batch, 1, seq, H_ATT, HEAD_DIM)
    return (y_prompt, y_sample, new_k, new_v)
```

```python
import functools
import math

import jax
import jax.numpy as jnp
import numpy as np
from jax import lax
from jax.experimental import pallas as pl
from jax.experimental.pallas import tpu as pltpu

f32 = jnp.float32
bf16 = jnp.bfloat16
HIGHEST = lax.Precision.HIGHEST

D_MODEL = 1024
GRID_W = 64
H_ATT = 8
HEAD_DIM = 64
D_ATT = H_ATT * HEAD_DIM
D_HYENA = 512
D_IN = 3 * D_ATT + 3 * D_HYENA
NA_ROWS = 8
NA_COLS = 16
SHORT_CONV = 3
FILTER_BANDS = 16
EMB_DIM = 1 + 2 * FILTER_BANDS
FILTER_FF = 64
DECAY_TARGET = 1e-2
MIN_DECAY = math.log(DECAY_TARGET) / 1.5
MAX_DECAY = math.log(DECAY_TARGET) / 0.3
N_GROUPS = 4
EXPERTS_PER_GROUP = 4
N_EXPERTS = N_GROUPS * EXPERTS_PER_GROUP
D_EXPERT = 512
N_MOD = 6
EPS = 1e-6
NEG_INF = -1e30
ATT_SCALE = HEAD_DIM ** -0.5

LANES = 128
SUBLANES = 8
MOD_ROWS = 8
ROUTE_LANES = 128
ROUTE_EXP_LANE0 = 16
ROUTE_E1, ROUTE_E2, ROUTE_W1, ROUTE_W2 = 0, 1, 2, 3
TOK_PITCH = 16
TOK_RT_ROW = 8
MOE_TM = 512
GATHER_PRIORITIES = (0,)
SCATTER_PRIORITIES = (1,)
VMEM_LIMIT = 56 * 1024 * 1024

SH1, SC1, GT1, SH2, SC2, GT2 = range(6)


def _cparams(sem, vmem=VMEM_LIMIT):
    return pltpu.CompilerParams(dimension_semantics=sem, vmem_limit_bytes=vmem)


def _dot(a, b):
    return jnp.dot(a, b, preferred_element_type=f32)


def _dot_hi(a, b):
    return lax.dot_general(a, b, (((1,), (0,)), ((), ())), precision=HIGHEST,
                           preferred_element_type=f32)


def _dot_nt(a, b):
    return lax.dot_general(a, b, (((1,), (1,)), ((), ())), preferred_element_type=f32)


def _rms(x, g):
    ms = jnp.mean(x * x, axis=-1, keepdims=True)
    return x * lax.rsqrt(ms + EPS) * g


def _cast_rows(src_ref, dst_ref, chunk):
    n = src_ref.shape[0] // chunk

    def body(i, c):
        r = pl.multiple_of(i * chunk, chunk)
        dst_ref[pl.ds(r, chunk), :] = src_ref[pl.ds(r, chunk), :].astype(dst_ref.dtype)
        return c

    lax.fori_loop(0, n, body, 0)


def _ada_kernel(ct_ref, w_ref, b_ref, o_ref, *, n_cond):
    ct = ct_ref[...]
    st = ct * (1.0 / (1.0 + jnp.exp(-ct)))
    w = w_ref[...]
    rid = lax.broadcasted_iota(jnp.int32, o_ref.shape, 0)
    out = jnp.broadcast_to(b_ref[...], o_ref.shape)
    for m in range(n_cond):
        row = jnp.sum(w * st[:, m:m + 1], axis=0, keepdims=True)
        out = out + jnp.where(rid == m, row, 0.0)
    o_ref[...] = out


def _ada_mod(cond8, w_ada, b_ada, n_cond):
    tn = 1536
    n = N_MOD * D_MODEL
    return pl.pallas_call(
        functools.partial(_ada_kernel, n_cond=n_cond),
        out_shape=jax.ShapeDtypeStruct((SUBLANES, n), f32),
        grid=(n // tn,),
        in_specs=[pl.BlockSpec((D_MODEL, SUBLANES), lambda j: (0, 0)),
                  pl.BlockSpec((D_MODEL, tn), lambda j: (0, j)),
                  pl.BlockSpec((1, tn), lambda j: (0, j))],
        out_specs=pl.BlockSpec((SUBLANES, tn), lambda j: (0, j)),
        compiler_params=_cparams(("arbitrary",)),
        name="ada_mod",
    )(cond8.T, w_ada, b_ada.reshape(1, n))


def _inproj_kernel(x_ref, mod_ref, g_ref, w_ref, q_ref, k_ref, v_ref, hy_ref, wbf_ref):
    @pl.when(pl.program_id(0) == 0)
    def _():
        _cast_rows(w_ref, wbf_ref, 128)

    h = _rms(x_ref[...], g_ref[...])
    h = h * (1.0 + mod_ref[0, SC1:SC1 + 1, :]) + mod_ref[0, SH1:SH1 + 1, :]
    p = _dot(h.astype(bf16), wbf_ref[...])
    q_ref[...] = p[:, 0:D_ATT]
    k_ref[...] = p[:, D_ATT:2 * D_ATT]
    v_ref[...] = p[:, 2 * D_ATT:3 * D_ATT]
    hy_ref[...] = p[:, 3 * D_ATT:]


def _inproj(x, mod, norm_g, w_in, rows_per_mod):
    t = x.shape[0]
    tm = 512
    blocks_per_mod = rows_per_mod // tm
    return pl.pallas_call(
        _inproj_kernel,
        out_shape=(jax.ShapeDtypeStruct((t, D_ATT), f32),
                   jax.ShapeDtypeStruct((t, D_ATT), f32),
                   jax.ShapeDtypeStruct((t, D_ATT), f32),
                   jax.ShapeDtypeStruct((t, 3 * D_HYENA), f32)),
        grid=(t // tm,),
        in_specs=[pl.BlockSpec((tm, D_MODEL), lambda i: (i, 0)),
                  pl.BlockSpec((1, MOD_ROWS, D_MODEL), lambda i: (i // blocks_per_mod, 0, 0)),
                  pl.BlockSpec((1, D_MODEL), lambda i: (0, 0)),
                  pl.BlockSpec((D_MODEL, D_IN), lambda i: (0, 0), pipeline_mode=pl.Buffered(1))],
        out_specs=(pl.BlockSpec((tm, D_ATT), lambda i: (i, 0)),
                   pl.BlockSpec((tm, D_ATT), lambda i: (i, 0)),
                   pl.BlockSpec((tm, D_ATT), lambda i: (i, 0)),
                   pl.BlockSpec((tm, 3 * D_HYENA), lambda i: (i, 0))),
        scratch_shapes=[pltpu.VMEM((D_MODEL, D_IN), bf16)],
        compiler_params=_cparams(("arbitrary",)),
        name="inproj",
    )(x, mod, norm_g.reshape(1, D_MODEL), w_in)


def _split_heads(q2):
    lane = lax.broadcasted_iota(jnp.int32, q2.shape, 1)
    qa = jnp.where(lane < HEAD_DIM, q2, 0.0)
    qb = jnp.where(lane >= HEAD_DIM, q2, 0.0)
    return jnp.concatenate([qa, qb], axis=0)


def _merge_heads(o_ab):
    m = o_ab.shape[0] // 2
    lane = lax.broadcasted_iota(jnp.int32, (m, LANES), 1)
    return jnp.where(lane < HEAD_DIM, o_ab[:m], o_ab[m:])


def _ctx_attn_kernel(q_ref, k_ref, v_ref, g_ref, o_ref):
    outs = []
    for p in range(D_ATT // LANES):
        cs = slice(p * LANES, (p + 1) * LANES)
        qq = _split_heads(q_ref[:, cs] * ATT_SCALE).astype(bf16)
        s = _dot_nt(qq, k_ref[:, cs].astype(bf16))
        m = jnp.max(s, axis=-1, keepdims=True)
        e = jnp.exp(s - m)
        l = jnp.sum(e, axis=-1, keepdims=True)
        o_ab = _dot(e.astype(bf16), v_ref[:, cs].astype(bf16)) / l
        outs.append(_merge_heads(o_ab))
    o_ref[...] = _rms(jnp.concatenate(outs, axis=-1), g_ref[...])


def _ctx_attention(q, k, v, gnorm, seq):
    t = q.shape[0]
    spec = pl.BlockSpec((seq, D_ATT), lambda b: (b, 0))
    return pl.pallas_call(
        _ctx_attn_kernel,
        out_shape=jax.ShapeDtypeStruct((t, D_ATT), f32),
        grid=(t // seq,),
        in_specs=[spec, spec, spec, pl.BlockSpec((1, D_ATT), lambda b: (0, 0))],
        out_specs=spec,
        compiler_params=_cparams(("arbitrary",)),
        name="ctx_attn",
    )(q, k, v, gnorm.reshape(1, D_ATT))


def _na_tables():
    col = np.arange(GRID_W)
    cs = np.clip(col - NA_COLS // 2, 0, GRID_W - NA_COLS)
    col_mask = (col[None, :] >= cs[:, None]) & (col[None, :] < cs[:, None] + NA_COLS)
    mask = np.tile(col_mask.astype(np.float32), (1, NA_ROWS))
    return mask


N_DR = 2 * NA_ROWS - 1
N_DC = 2 * NA_COLS - 1
BIAS_PAIRS = N_DR - 1


def _na_bias_rows(rpb):
    out = jnp.zeros((H_ATT, BIAS_PAIRS, LANES), f32)
    out = out.at[:, :, 0:N_DC].set(rpb[:, 0:BIAS_PAIRS])
    return out.at[:, :, GRID_W:GRID_W + N_DC].set(rpb[:, 1:N_DR])


def _na_row_start(r, rows):
    return jnp.clip(r - NA_ROWS // 2, 0, rows - NA_ROWS)


def _na_attn_kernel(q_ref, k_ref, v_ref, kc_ref, vc_ref, rp_ref, mask_ref, g_ref, o_ref,
                    kbf_ref, vbf_ref, t2_ref, *, rows):
    b = pl.program_id(0)
    r = pl.program_id(1)

    @pl.when((b == 0) & (r == 0))
    def _():
        for h in range(H_ATT):
            for i in range(BIAS_PAIRS):
                v = jnp.broadcast_to(rp_ref[h, i:i + 1, :], (GRID_W, LANES))
                t2_ref[h * BIAS_PAIRS + i] = pltpu.roll(v, LANES - (NA_COLS - 1), 1, stride=1, stride_axis=0)

    @pl.when(r == 0)
    def _():
        _cast_rows(k_ref, kbf_ref, 256)
        _cast_rows(v_ref, vbf_ref, 256)

    nwin = NA_ROWS * GRID_W
    rs = _na_row_start(r, rows)
    start = pl.multiple_of(rs * GRID_W, GRID_W)
    i0 = rs - r + NA_ROWS - 1
    valid = mask_ref[...] != 0.0
    valid2 = jnp.concatenate([valid, valid], axis=0)
    outs = []
    for p in range(D_ATT // LANES):
        cs = slice(p * LANES, (p + 1) * LANES)
        qq = _split_heads(q_ref[:, cs] * ATT_SCALE).astype(bf16)
        kw = kbf_ref[pl.ds(start, nwin), cs]
        vw = vbf_ref[pl.ds(start, nwin), cs]
        s_lat = _dot_nt(qq, kw)
        s_ctx = _dot_nt(qq, kc_ref[:, cs].astype(bf16))
        bias2 = jnp.concatenate(
            [jnp.concatenate([t2_ref[(2 * p + hh) * BIAS_PAIRS + i0 + 2 * jp] for hh in range(2)], axis=0)
             for jp in range(NA_ROWS // 2)], axis=-1)
        s_lat = jnp.where(valid2, s_lat + bias2, NEG_INF)
        m = jnp.maximum(jnp.max(s_lat, axis=-1, keepdims=True), jnp.max(s_ctx, axis=-1, keepdims=True))
        e_lat = jnp.exp(s_lat - m)
        e_ctx = jnp.exp(s_ctx - m)
        l = jnp.sum(e_lat, axis=-1, keepdims=True) + jnp.sum(e_ctx, axis=-1, keepdims=True)
        o_ab = (_dot(e_lat.astype(bf16), vw) + _dot(e_ctx.astype(bf16), vc_ref[:, cs].astype(bf16))) / l
        outs.append(_merge_heads(o_ab))
    o_ref[...] = _rms(jnp.concatenate(outs, axis=-1), g_ref[...])


def _na_attention(q, k, v, kc, vc, rpb, gnorm, nb, seq):
    rows = seq // GRID_W
    past = kc.shape[0] // nb
    mask = _na_tables()
    return pl.pallas_call(
        functools.partial(_na_attn_kernel, rows=rows),
        out_shape=jax.ShapeDtypeStruct((nb * seq, D_ATT), f32),
        grid=(nb, rows),
        in_specs=[pl.BlockSpec((GRID_W, D_ATT), lambda b, r: (b * rows + r, 0)),
                  pl.BlockSpec((seq, D_ATT), lambda b, r: (b, 0)),
                  pl.BlockSpec((seq, D_ATT), lambda b, r: (b, 0)),
                  pl.BlockSpec((past, D_ATT), lambda b, r: (b, 0)),
                  pl.BlockSpec((past, D_ATT), lambda b, r: (b, 0)),
                  pl.BlockSpec((H_ATT, BIAS_PAIRS, LANES), lambda b, r: (0, 0, 0)),
                  pl.BlockSpec((GRID_W, NA_ROWS * GRID_W), lambda b, r: (0, 0)),
                  pl.BlockSpec((1, D_ATT), lambda b, r: (0, 0))],
        out_specs=pl.BlockSpec((GRID_W, D_ATT), lambda b, r: (b * rows + r, 0)),
        scratch_shapes=[pltpu.VMEM((seq, D_ATT), bf16), pltpu.VMEM((seq, D_ATT), bf16),
                        pltpu.VMEM((H_ATT * BIAS_PAIRS, GRID_W, LANES), f32)],
        compiler_params=_cparams(("arbitrary", "arbitrary")),
        name="na_attn",
    )(q, k, v, kc, vc, _na_bias_rows(rpb), jnp.asarray(mask), gnorm.reshape(1, D_ATT))


def _hy_front_kernel(x0_ref, x1_ref, v_ref, w0_ref, w1_ref, wv_ref, b0_ref, b1_ref, bv_ref,
                     zbf_ref, z_ref, x0c_ref):
    seq = x0_ref.shape[0]
    row = lax.broadcasted_iota(jnp.int32, x0_ref.shape, 0)
    first = row == 0
    last = row == seq - 1

    def conv(u_ref, w_ref, b_ref):
        u = u_ref[...]
        up = jnp.where(first, 0.0, pltpu.roll(u, 1, 0))
        un = jnp.where(last, 0.0, pltpu.roll(u, seq - 1, 0))
        y = b_ref[...] + up * w_ref[0:1, :]
        y = y + u * w_ref[1:2, :]
        return y + un * w_ref[2:3, :]

    z = conv(v_ref, wv_ref, bv_ref) * conv(x1_ref, w1_ref, b1_ref)
    z_ref[...] = z
    zbf_ref[...] = z.astype(bf16)
    x0c_ref[...] = conv(x0_ref, w0_ref, b0_ref)


def _hy_front(hy, conv_w, conv_b, nb, seq):
    tc = D_HYENA if seq <= 512 else D_HYENA // 2
    nc = D_HYENA // tc
    n = nb * D_HYENA
    cb = conv_b.reshape(1, 3 * D_HYENA)

    def part(k):
        return (pl.BlockSpec((seq, tc), lambda b, j: (b, k * nc + j)),
                pl.BlockSpec((SHORT_CONV, tc), lambda b, j: (0, k * nc + j)),
                pl.BlockSpec((1, tc), lambda b, j: (0, k * nc + j)))

    (x0s, w0s, b0s), (x1s, w1s, b1s), (vs, wvs, bvs) = part(0), part(1), part(2)
    ospec = pl.BlockSpec((seq, tc), lambda b, j: (0, b * nc + j))
    return pl.pallas_call(
        _hy_front_kernel,
        out_shape=(jax.ShapeDtypeStruct((seq, n), bf16),
                   jax.ShapeDtypeStruct((seq, n), f32),
                   jax.ShapeDtypeStruct((seq, n), f32)),
        grid=(nb, nc),
        in_specs=[x0s, x1s, vs, w0s, w1s, wvs, b0s, b1s, bvs],
        out_specs=(ospec, ospec, ospec),
        compiler_params=_cparams(("arbitrary", "arbitrary")),
        name="hyena_front",
    )(hy, hy, hy, conv_w, conv_w, conv_w, cb, cb, cb)


def _filter_features(seq):
    t = np.linspace(0.0, 1.0, seq, dtype=np.float64)[:, None]
    w = 2.0 * math.pi * np.arange(seq, dtype=np.float64)[:, None] / seq
    fb = np.linspace(1e-4, FILTER_BANDS - 1, FILTER_BANDS, dtype=np.float64)[None, :]
    ang = fb * w
    z = np.concatenate([t, np.cos(ang), -np.sin(ang)], axis=-1).astype(np.float32)
    return np.pad(z, ((0, 0), (0, LANES - EMB_DIM)))


def _filt_kernel(zf_ref, w1_ref, b1_ref, fr_ref, w2_ref, b2_ref, w3_ref, dl_ref, h_ref, kl_ref):
    i = pl.program_id(0)
    tr = zf_ref.shape[0]
    zf = zf_ref[...]
    fr = fr_ref[...]
    h = jnp.sin(fr * (_dot_hi(zf, w1_ref[...]) + b1_ref[...]))
    h = jnp.sin(fr * (_dot_hi(h, w2_ref[...]) + b2_ref[...]))
    h = _dot_hi(h, w3_ref[...])
    decay = jnp.exp(-zf[:, 0:1] * dl_ref[...])
    row = lax.broadcasted_iota(jnp.int32, (tr, D_HYENA), 0) + i * tr
    hf = h[:, :D_HYENA] * decay
    hb = jnp.where(row == 0, 0.0, h[:, D_HYENA:] * decay)
    h_ref[:, :D_HYENA] = (hf + hb).astype(bf16)
    h_ref[:, D_HYENA:] = (hb - hf).astype(bf16)
    alt = (1 - 2 * (row & 1)).astype(f32)
    part = jnp.sum(alt * (hf + hb), axis=0, keepdims=True)

    @pl.when(i == 0)
    def _():
        kl_ref[...] = jnp.zeros_like(kl_ref)

    kl_ref[...] += jnp.broadcast_to(part, kl_ref.shape)


def _hy_filters(seq, w1, b1, w2, b2, w3, freq):
    tr = 256
    zf = jnp.asarray(_filter_features(seq))
    deltas = np.abs(np.linspace(MIN_DECAY, MAX_DECAY, D_HYENA, dtype=np.float64))[None, :].astype(np.float32)
    w1p = jnp.pad(w1, ((0, LANES - EMB_DIM), (0, 0)))
    const = lambda shape: pl.BlockSpec(shape, lambda i: (0, 0))
    return pl.pallas_call(
        _filt_kernel,
        out_shape=(jax.ShapeDtypeStruct((seq, 2 * D_HYENA), bf16),
                   jax.ShapeDtypeStruct((SUBLANES, D_HYENA), f32)),
        grid=(seq // tr,),
        in_specs=[pl.BlockSpec((tr, LANES), lambda i: (i, 0)),
                  const((LANES, FILTER_FF)), const((1, FILTER_FF)), const((1, FILTER_FF)),
                  const((FILTER_FF, FILTER_FF)), const((1, FILTER_FF)),
                  const((FILTER_FF, 2 * D_HYENA)), const((1, D_HYENA))],
        out_specs=(pl.BlockSpec((tr, 2 * D_HYENA), lambda i: (i, 0)),
                   pl.BlockSpec((SUBLANES, D_HYENA), lambda i: (0, 0))),
        compiler_params=_cparams(("arbitrary",)),
        name="hyena_filters",
    )(zf, w1p, b1.reshape(1, -1), freq.reshape(1, -1), w2, b2.reshape(1, -1), w3, jnp.asarray(deltas))


def _dft_mats(seq):
    n = 2 * seq
    ph = (np.arange(seq, dtype=np.int64)[:, None] * np.arange(seq, dtype=np.int64)[None, :]) % n
    ang = ph.astype(np.float64) * (2.0 * math.pi / n)
    return np.cos(ang).astype(np.float32), np.sin(ang).astype(np.float32)


def _alt_col(rows, offset):
    row = lax.broadcasted_iota(jnp.int32, (rows, 1), 0) + offset
    return (1 - 2 * (row & 1)).astype(f32)


def _hy_fwd_kernel(fr_ref, fi_ref, z_ref, h_ref, kl_ref, yr_ref, yi_ref, yl_ref, kr_s, ki_s, *, n):
    i = pl.program_id(0)
    j = pl.program_id(1)
    tf = fr_ref.shape[0]
    tn = z_ref.shape[1]
    frb = fr_ref[...].astype(bf16)
    fib = fi_ref[...].astype(bf16)

    @pl.when(j == 0)
    def _():
        f = lax.broadcasted_iota(jnp.int32, (tf, 1), 0) + i * tf
        cf = jnp.where(f == 0, 1.0 / n, 2.0 / n)
        kr_s[...] = _dot(frb, h_ref[:, :D_HYENA]) * cf
        ki_s[...] = _dot(fib, h_ref[:, D_HYENA:]) * cf

    a = _dot(frb, z_ref[...])
    b = _dot(fib, z_ref[...])
    kr = kr_s[...]
    ki = ki_s[...]
    for c in range(tn // D_HYENA):
        cs = slice(c * D_HYENA, (c + 1) * D_HYENA)
        yr_ref[:, cs] = (a[:, cs] * kr + b[:, cs] * ki).astype(bf16)
        yi_ref[:, cs] = (b[:, cs] * kr - a[:, cs] * ki).astype(bf16)

    @pl.when(i == 0)
    def _():
        alt = _alt_col(z_ref.shape[0], 0)
        nz = jnp.sum(z_ref[...].astype(f32) * alt, axis=0, keepdims=True)
        kl = jnp.concatenate([kl_ref[0:1, :]] * (tn // D_HYENA), axis=-1)
        yl_ref[...] = jnp.broadcast_to(nz * kl * (1.0 / n), yl_ref.shape)


def _hy_fwd(fr, fi, zbf, hcat, kl, seq):
    n_cols = zbf.shape[1]
    tf = 256
    tn = min(n_cols, 2048)
    ni, nj = seq // tf, n_cols // tn
    assert ni == 1 or nj == 1
    return pl.pallas_call(
        functools.partial(_hy_fwd_kernel, n=2 * seq),
        out_shape=(jax.ShapeDtypeStruct((seq, n_cols), bf16),
                   jax.ShapeDtypeStruct((seq, n_cols), bf16),
                   jax.ShapeDtypeStruct((SUBLANES, n_cols), f32)),
        grid=(ni, nj),
        in_specs=[pl.BlockSpec((tf, seq), lambda i, j: (i, 0)),
                  pl.BlockSpec((tf, seq), lambda i, j: (i, 0)),
                  pl.BlockSpec((seq, tn), lambda i, j: (0, j)),
                  pl.BlockSpec((seq, 2 * D_HYENA), lambda i, j: (0, 0)),
                  pl.BlockSpec((SUBLANES, D_HYENA), lambda i, j: (0, 0))],
        out_specs=(pl.BlockSpec((tf, tn), lambda i, j: (i, j)),
                   pl.BlockSpec((tf, tn), lambda i, j: (i, j)),
                   pl.BlockSpec((SUBLANES, tn), lambda i, j: (0, j))),
        scratch_shapes=[pltpu.VMEM((tf, D_HYENA), f32), pltpu.VMEM((tf, D_HYENA), f32)],
        compiler_params=_cparams(("arbitrary", "arbitrary")),
        name="hyena_dft_fwd",
    )(fr, fi, zbf, hcat, kl)


def _hy_inv_kernel(fr_ref, fi_ref, yr_ref, yi_ref, yl_ref, z_ref, x0_ref, skip_ref, g_ref, o_ref):
    tt = fr_ref.shape[0]
    tn = yr_ref.shape[1]
    y = _dot(fr_ref[...].astype(bf16), yr_ref[...]) + _dot(fi_ref[...].astype(bf16), yi_ref[...])
    alt = _alt_col(tt, pl.program_id(0) * tt)
    for c in range(tn // D_HYENA):
        cs = slice(c * D_HYENA, (c + 1) * D_HYENA)
        yc = y[:, cs] + alt * yl_ref[0:1, cs] + z_ref[:, cs] * skip_ref[...]
        o_ref[:, cs] = _rms(yc * x0_ref[:, cs], g_ref[...])


def _hy_inv(fr, fi, yr, yi, yl, z, x0c, skip, gnorm, seq):
    n_cols = z.shape[1]
    tt = 256
    tn = min(n_cols, 2048)
    blk = pl.BlockSpec((tt, tn), lambda i, j: (i, j))
    return pl.pallas_call(
        _hy_inv_kernel,
        out_shape=jax.ShapeDtypeStruct((seq, n_cols), f32),
        grid=(seq // tt, n_cols // tn),
        in_specs=[pl.BlockSpec((tt, seq), lambda i, j: (i, 0)),
                  pl.BlockSpec((tt, seq), lambda i, j: (i, 0)),
                  pl.BlockSpec((seq, tn), lambda i, j: (0, j)),
                  pl.BlockSpec((seq, tn), lambda i, j: (0, j)),
                  pl.BlockSpec((SUBLANES, tn), lambda i, j: (0, j)),
                  blk, blk,
                  pl.BlockSpec((1, D_HYENA), lambda i, j: (0, 0)),
                  pl.BlockSpec((1, D_HYENA), lambda i, j: (0, 0))],
        out_specs=blk,
        compiler_params=_cparams(("arbitrary", "arbitrary")),
        name="hyena_dft_inv",
    )(fr, fi, yr, yi, yl, z, x0c, skip.reshape(1, D_HYENA), gnorm.reshape(1, D_HYENA))


def _hyena(hy, lp, nb, seq):
    fr_np, fi_np = _dft_mats(seq)
    fr, fi = jnp.asarray(fr_np), jnp.asarray(fi_np)
    zbf, z, x0c = _hy_front(hy, lp['conv_w'], lp['conv_b'], nb, seq)
    hcat, kl = _hy_filters(seq, lp['filt_w1'], lp['filt_b1'], lp['filt_w2'], lp['filt_b2'],
                           lp['filt_w3'], lp['filt_freq'])
    yr, yi, yl = _hy_fwd(fr, fi, zbf, hcat, kl, seq)
    return _hy_inv(fr, fi, yr, yi, yl, z, x0c, lp['hyena_skip'], lp['gnorm_hyena'], seq)


def _store_token_tiles(ref, x, pitch=SUBLANES):
    m = x.shape[0]
    for c in range(D_MODEL // LANES):
        ref[pl.ds(c, m, stride=pitch), :] = x[:, c * LANES:(c + 1) * LANES]


def _load_token_tiles(ref, m, lead=(), pitch=SUBLANES):
    return jnp.concatenate([ref[lead + (pl.ds(c, m, stride=pitch), slice(None))]
                            for c in range(D_MODEL // LANES)], axis=-1)


def _route(logits):
    lane_i = lax.broadcasted_iota(jnp.int32, logits.shape, 1)
    lane = lane_i.astype(f32)
    big = float(ROUTE_LANES)
    is_g = lane_i < N_GROUPS
    mg = jnp.max(jnp.where(is_g, logits, -jnp.inf), axis=-1, keepdims=True)
    sg = jnp.sum(jnp.where(is_g, jnp.exp(logits - mg), 0.0), axis=-1, keepdims=True)
    g_w = 1.0 / sg
    g_idx = jnp.min(jnp.where(is_g & (logits == mg), lane, big), axis=-1, keepdims=True)
    e_id = lane_i - ROUTE_EXP_LANE0
    sel = (e_id >= 0) & (e_id < N_EXPERTS) & ((e_id >> 2).astype(f32) == g_idx)
    me = jnp.max(jnp.where(sel, logits, -jnp.inf), axis=-1, keepdims=True)
    ee = jnp.where(sel, jnp.exp(logits - me), 0.0)
    prob = ee / jnp.sum(ee, axis=-1, keepdims=True)
    p1 = jnp.max(jnp.where(sel, prob, -1.0), axis=-1, keepdims=True)
    i1 = jnp.min(jnp.where(sel & (prob == p1), lane, big), axis=-1, keepdims=True)
    sel2 = sel & (lane != i1)
    p2 = jnp.max(jnp.where(sel2, prob, -1.0), axis=-1, keepdims=True)
    i2 = jnp.min(jnp.where(sel2 & (prob == p2), lane, big), axis=-1, keepdims=True)
    tot = p1 + p2
    rec = jnp.where(lane_i == ROUTE_E1, i1 - ROUTE_EXP_LANE0, 0.0)
    rec = jnp.where(lane_i == ROUTE_E2, i2 - ROUTE_EXP_LANE0, rec)
    rec = jnp.where(lane_i == ROUTE_W1, g_w * (p1 / tot), rec)
    return jnp.where(lane_i == ROUTE_W2, g_w * (p2 / tot), rec)


def _outproj_kernel(xp_ref, xs_ref, attp_ref, atts_ref, hyp_ref, hys_ref, mod_ref, wo_ref, g2_ref,
                    wr_ref, br_ref, x1_ref, tok_ref, rt_ref, wobf_ref, wrh_ref, wrl_ref, *, prompt_tiles):
    i = pl.program_id(0)

    @pl.when(i == 0)
    def _():
        _cast_rows(wo_ref, wobf_ref, 128)
        wr = wr_ref[...]
        hi = wr.astype(bf16)
        wrh_ref[...] = hi
        wrl_ref[...] = (wr - hi.astype(f32)).astype(bf16)

    is_p = i < prompt_tiles
    hyp = jnp.concatenate([hyp_ref[:, b * D_HYENA:(b + 1) * D_HYENA]
                           for b in range(hyp_ref.shape[1] // D_HYENA)], axis=0)
    x = jnp.where(is_p, xp_ref[...], xs_ref[...])
    att = jnp.where(is_p, attp_ref[...], atts_ref[...])
    hyo = jnp.where(is_p, hyp, hys_ref[...])
    proj = (_dot(att.astype(bf16), wobf_ref[0:D_ATT, :]) + _dot(hyo.astype(bf16), wobf_ref[D_ATT:, :]))
    x1 = x + mod_ref[0, GT1:GT1 + 1, :] * proj
    x1_ref[...] = x1
    h2 = _rms(x1, g2_ref[...]) * (1.0 + mod_ref[0, SC2:SC2 + 1, :]) + mod_ref[0, SH2:SH2 + 1, :]
    h2h = h2.astype(bf16)
    h2l = (h2 - h2h.astype(f32)).astype(bf16)
    logits = _dot(h2h, wrh_ref[...]) + _dot(h2l, wrh_ref[...]) + _dot(h2h, wrl_ref[...]) + br_ref[...]
    rt = _route(logits)
    rt_ref[...] = rt
    tok_ref[...] = jnp.zeros_like(tok_ref)
    _store_token_tiles(tok_ref, h2, TOK_PITCH)
    tok_ref[pl.ds(TOK_RT_ROW, h2.shape[0], stride=TOK_PITCH), :] = rt


def _outproj(xp, xs, attp, atts, hyp, hys, mod, w_out, norm2_g, wr, br, seq_p, seq_s):
    tm = 512
    tp, ts = xp.shape[0], xs.shape[0]
    npt, nst = tp // tm, ts // tm
    assert tm % seq_p == 0 and seq_s % tm == 0
    spb = seq_s // tm
    bpt = tm // seq_p
    p_idx = lambda i: jnp.minimum(i, npt - 1)
    s_idx = lambda i: jnp.maximum(i - npt, 0)
    const = lambda shape: pl.BlockSpec(shape, lambda i: (0,) * len(shape))
    return pl.pallas_call(
        functools.partial(_outproj_kernel, prompt_tiles=npt),
        out_shape=(jax.ShapeDtypeStruct((tp + ts, D_MODEL), f32),
                   jax.ShapeDtypeStruct(((tp + ts) * TOK_PITCH, LANES), f32),
                   jax.ShapeDtypeStruct((tp + ts, ROUTE_LANES), f32)),
        grid=(npt + nst,),
        in_specs=[pl.BlockSpec((tm, D_MODEL), lambda i: (p_idx(i), 0)),
                  pl.BlockSpec((tm, D_MODEL), lambda i: (s_idx(i), 0)),
                  pl.BlockSpec((tm, D_ATT), lambda i: (p_idx(i), 0)),
                  pl.BlockSpec((tm, D_ATT), lambda i: (s_idx(i), 0)),
                  pl.BlockSpec((seq_p, bpt * D_HYENA), lambda i: (0, p_idx(i))),
                  pl.BlockSpec((tm, D_HYENA), lambda i: (s_idx(i) % spb, s_idx(i) // spb)),
                  pl.BlockSpec((1, MOD_ROWS, D_MODEL),
                               lambda i: (jnp.where(i < npt, 0, 1 + s_idx(i) // spb), 0, 0)),
                  const((D_MODEL, D_MODEL)), const((1, D_MODEL)),
                  const((D_MODEL, ROUTE_LANES)), const((1, ROUTE_LANES))],
        out_specs=(pl.BlockSpec((tm, D_MODEL), lambda i: (i, 0)),
                   pl.BlockSpec((tm * TOK_PITCH, LANES), lambda i: (i, 0)),
                   pl.BlockSpec((tm, ROUTE_LANES), lambda i: (i, 0))),
        scratch_shapes=[pltpu.VMEM((D_MODEL, D_MODEL), bf16),
                        pltpu.VMEM((D_MODEL, ROUTE_LANES), bf16), pltpu.VMEM((D_MODEL, ROUTE_LANES), bf16)],
        compiler_params=_cparams(("arbitrary",)),
        name="outproj_router",
    )(xp, xs, attp, atts, hyp, hys, mod, w_out, norm2_g.reshape(1, D_MODEL), wr, br)


PAIRS_PER_GROUP = 6
N_CLASSES = N_GROUPS * PAIRS_PER_GROUP
PAIR_SLOT_A = (0, 0, 0, 1, 1, 3)
PAIR_SLOT_B = (1, 2, 3, 3, 2, 2)
FLAG_NEW_A, FLAG_NEW_B, FLAG_TILE_START, FLAG_TILE_END = 1, 2, 4, 8


CLASS_ROWS = 32
TAB_TILE, TAB_EA, TAB_EB, TAB_LO, TAB_HI, TAB_FLAGS, TAB_N = range(7)


def _select_by(idx, values):
    out = jnp.full(idx.shape, float(values[-1]), f32)
    for i in range(len(values) - 2, -1, -1):
        out = jnp.where(idx == i, float(values[i]), out)
    return out


def _plan_kernel(e1_ref, e2_ref, pos_ref, tab_ref, *, tm):
    e1 = e1_ref[...]
    e2 = e2_ref[...]
    rows = e1.shape[0]
    grp = jnp.floor(e1 * (1.0 / EXPERTS_PER_GROUP))
    l1 = e1 - EXPERTS_PER_GROUP * grp
    l2 = e2 - EXPERTS_PER_GROUP * jnp.floor(e2 * (1.0 / EXPERTS_PER_GROUP))
    lo, hi = jnp.minimum(l1, l2), jnp.maximum(l1, l2)
    pair = jnp.where(lo == 0, hi - 1, jnp.where(lo == 1, jnp.where(hi == 3, 3.0, 4.0), 5.0))
    cls = grp * PAIRS_PER_GROUP + pair

    ri = lax.broadcasted_iota(jnp.int32, (LANES, LANES), 0)
    ci = lax.broadcasted_iota(jnp.int32, (LANES, LANES), 1)
    upper = (ri <= ci).astype(bf16)
    rr = lax.broadcasted_iota(jnp.int32, (rows, rows), 0)
    rc = lax.broadcasted_iota(jnp.int32, (rows, rows), 1)
    strict_lower = (rc < rr).astype(bf16)
    cid = lax.broadcasted_iota(jnp.int32, (CLASS_ROWS, 1), 0)

    pos = jnp.zeros(e1.shape, f32)
    base = jnp.zeros((1, 1), f32)
    cnt_col = jnp.zeros((CLASS_ROWS, 1), f32)
    off_col = jnp.zeros((CLASS_ROWS, 1), f32)
    for c in range(N_CLASSES):
        m = cls == c
        within = _dot(m.astype(bf16), upper)
        tot = jnp.broadcast_to(within[:, LANES - 1:LANES], within.shape)
        before = _dot(strict_lower, tot.astype(bf16))[:, 0:1]
        count = jnp.sum(within[:, LANES - 1:LANES], axis=0, keepdims=True)
        pos = jnp.where(m, base + before + within - 1.0, pos)
        cnt_col = jnp.where(cid == c, count, cnt_col)
        off_col = jnp.where(cid == c, base, off_col)
        base = base + count
    pos_ref[...] = pos.astype(jnp.int32)

    end_col = off_col + cnt_col
    inv_tm = 1.0 / tm
    first_col = jnp.floor(off_col * inv_tm)
    ntl_col = jnp.where(cnt_col > 0, jnp.floor((end_col - 1.0) * inv_tm) - first_col + 1.0, 0.0)
    cr = lax.broadcasted_iota(jnp.int32, (CLASS_ROWS, CLASS_ROWS), 0)
    cc = lax.broadcasted_iota(jnp.int32, (CLASS_ROWS, CLASS_ROWS), 1)
    lower = (cc <= cr).astype(bf16)
    iend_col = _dot(lower, jnp.broadcast_to(ntl_col, (CLASS_ROWS, LANES)).astype(bf16))[:, 0:1]
    istart_col = iend_col - ntl_col
    n_items = jnp.sum(ntl_col, axis=0, keepdims=True)
    slots = lax.broadcasted_iota(jnp.int32, (1, LANES), 1).astype(f32)
    items = jnp.minimum(slots, n_items - 1.0)
    past = jnp.where((items >= iend_col) & (cid < N_CLASSES), 1.0, 0.0)
    it_cls = jnp.minimum(jnp.sum(past, axis=0, keepdims=True), N_CLASSES - 1.0)
    sel = it_cls == cid.astype(f32)
    pick = lambda col: jnp.sum(jnp.where(sel, col, 0.0), axis=0, keepdims=True)
    grp_col = jnp.floor(cid.astype(f32) * (1.0 / PAIRS_PER_GROUP))
    pair_col = cid.astype(f32) - PAIRS_PER_GROUP * grp_col
    ea_col = EXPERTS_PER_GROUP * grp_col + _select_by(pair_col, PAIR_SLOT_A)
    eb_col = EXPERTS_PER_GROUP * grp_col + _select_by(pair_col, PAIR_SLOT_B)
    it_tile = pick(first_col) + items - pick(istart_col)
    it_lo = jnp.maximum(pick(off_col) - it_tile * tm, 0.0)
    it_hi = jnp.minimum(pick(end_col) - it_tile * tm, float(tm))
    it_ea, it_eb = pick(ea_col), pick(eb_col)
    prev = lambda v: pltpu.roll(jnp.broadcast_to(v, (SUBLANES, LANES)), 1, 1)[0:1]
    changed = lambda v: jnp.where((slots == 0) | (v != prev(v)), 1.0, 0.0)
    flags = (FLAG_NEW_A * changed(it_ea) + FLAG_NEW_B * changed(it_eb)
             + FLAG_TILE_START * jnp.where(it_lo == 0, 1.0, 0.0) + FLAG_TILE_END * jnp.where(it_hi == tm, 1.0, 0.0))
    table = {TAB_TILE: it_tile, TAB_EA: it_ea, TAB_EB: it_eb, TAB_LO: it_lo, TAB_HI: it_hi, TAB_FLAGS: flags,
             TAB_N: jnp.broadcast_to(n_items, (1, LANES))}
    trow = lax.broadcasted_iota(jnp.int32, tab_ref.shape, 0)
    tab = jnp.zeros(tab_ref.shape, f32)
    for r, v in table.items():
        tab = jnp.where(trow == r, v, tab)
    tab_ref[...] = tab.astype(jnp.int32)


def _route_plan(e1, e2, tm, max_items):
    t = e1.shape[0]
    assert t % LANES == 0 and max_items <= LANES and tm & (tm - 1) == 0
    pos, tab = pl.pallas_call(
        functools.partial(_plan_kernel, tm=tm),
        out_shape=(jax.ShapeDtypeStruct((t // LANES, LANES), jnp.int32),
                   jax.ShapeDtypeStruct((SUBLANES, LANES), jnp.int32)),
        name="moe_plan",
    )(e1.reshape(t // LANES, LANES), e2.reshape(t // LANES, LANES))
    row = lambda r: tab[r, :max_items]
    return dict(pos=pos.reshape(t), n_items=tab[TAB_N, :1], it_tile=row(TAB_TILE), it_ea=row(TAB_EA),
                it_eb=row(TAB_EB), it_lo=row(TAB_LO), it_hi=row(TAB_HI), it_flags=row(TAB_FLAGS))


ROW_DMA_UNROLL = 8


def _row_dma_loop(g0, g1, make_copy, priorities):
    def body(i, c):
        for u in range(ROW_DMA_UNROLL):
            make_copy(i * ROW_DMA_UNROLL + u).start(priority=priorities[u % len(priorities)])
        return c

    lax.fori_loop(g0, g1, body, 0)


def _moe_ffn_kernel(pos_ref, it_tile, it_ea, it_eb, it_lo, it_hi, it_flags, n_items, tok_hbm,
                    wga_ref, wua_ref, wda_ref, wgb_ref, wub_ref, wdb_ref, out_hbm,
                    src_s, xbuf, ybuf, x_s, rt_s, acc_s, wa_s, wb_s, sem_in, sem_out, *, chunk, n_tiles):
    i = pl.program_id(0)
    tm = x_s.shape[0]
    groups = tm // ROW_DMA_UNROLL

    def gather_tile(k):
        base = k * tm
        slot = k % 2
        _row_dma_loop(0, groups, lambda r: pltpu.make_async_copy(
            tok_hbm.at[src_s[base + r]],
            xbuf.at[slot, pl.ds(pl.multiple_of(r * TOK_PITCH, TOK_PITCH), TOK_PITCH), :],
            sem_in.at[slot]), GATHER_PRIORITIES)

    def scatter_tile(k):
        base = k * tm
        slot = k % 2
        _row_dma_loop(0, groups, lambda r: pltpu.make_async_copy(
            ybuf.at[slot, pl.ds(pl.multiple_of(r * SUBLANES, SUBLANES), SUBLANES), :],
            out_hbm.at[src_s[base + r]],
            sem_out.at[slot]), SCATTER_PRIORITIES)

    def expert_ffn(x, w, wg, wu, wd):
        g = _dot(x, wg[...])
        u = _dot(x, wu[...])
        hid = (g * (1.0 / (1.0 + jnp.exp(-g)))) * u
        return _dot((hid * w).astype(bf16), wd[...])

    def wait_all(buf, sem, slot):
        pltpu.make_async_copy(buf.at[slot], buf.at[slot], sem.at[slot]).wait()

    @pl.when(i == 0)
    def _():
        def inv(a, c):
            for u in range(ROW_DMA_UNROLL):
                src_s[pos_ref[a * ROW_DMA_UNROLL + u]] = a * ROW_DMA_UNROLL + u
            return c

        lax.fori_loop(0, pos_ref.shape[0] // ROW_DMA_UNROLL, inv, 0)
        gather_tile(0)

    @pl.when(i < n_items[0])
    def _():
        flags = it_flags[i]
        k = it_tile[i]

        @pl.when((flags & FLAG_NEW_A) != 0)
        def _():
            for dst, src in zip(wa_s, (wga_ref, wua_ref, wda_ref)):
                dst[...] = src[0].astype(bf16)

        @pl.when((flags & FLAG_NEW_B) != 0)
        def _():
            for dst, src in zip(wb_s, (wgb_ref, wub_ref, wdb_ref)):
                dst[...] = src[0].astype(bf16)

        @pl.when((flags & FLAG_TILE_START) != 0)
        def _():
            @pl.when(k + 1 < n_tiles)
            def _():
                gather_tile(k + 1)

            wait_all(xbuf, sem_in, k % 2)
            x_s[...] = _load_token_tiles(xbuf, tm, (k % 2,), TOK_PITCH).astype(bf16)
            rt_s[...] = xbuf[k % 2, pl.ds(TOK_RT_ROW, tm, stride=TOK_PITCH), :]
            acc_s[...] = jnp.zeros_like(acc_s)

        lo = it_lo[i]
        hi = it_hi[i]
        ea = it_ea[i].astype(f32)

        def body(j, c):
            r = pl.multiple_of(j * chunk, chunk)

            @pl.when((r < hi) & (r + chunk > lo))
            def _():
                x = x_s[pl.ds(r, chunk), :]
                rt = rt_s[pl.ds(r, chunk), :]
                first_is_a = rt[:, ROUTE_E1:ROUTE_E1 + 1] == ea
                w1 = rt[:, ROUTE_W1:ROUTE_W1 + 1]
                w2 = rt[:, ROUTE_W2:ROUTE_W2 + 1]
                y = (expert_ffn(x, jnp.where(first_is_a, w1, w2), *wa_s)
                     + expert_ffn(x, jnp.where(first_is_a, w2, w1), *wb_s))
                row = lax.broadcasted_iota(jnp.int32, (chunk, 1), 0) + r
                mine = (row >= lo) & (row < hi)
                acc_s[pl.ds(r, chunk), :] = jnp.where(mine, y, acc_s[pl.ds(r, chunk), :])

            return c

        lax.fori_loop(0, tm // chunk, body, 0)

        @pl.when((flags & FLAG_TILE_END) != 0)
        def _():
            @pl.when(k >= 2)
            def _():
                wait_all(ybuf, sem_out, k % 2)

            _store_token_tiles(ybuf.at[k % 2], acc_s[...])
            scatter_tile(k)

            @pl.when(k == n_tiles - 1)
            def _():
                if n_tiles > 1:
                    wait_all(ybuf, sem_out, (n_tiles - 2) % 2)
                wait_all(ybuf, sem_out, (n_tiles - 1) % 2)


def _moe_ffn(plan, tok, w_gate, w_up, w_down, tm, max_items):
    n_tok = tok.shape[0] // TOK_PITCH
    n_tiles = n_tok // tm
    tok3 = tok.reshape(n_tok, TOK_PITCH, LANES)
    spec_a = lambda shape: pl.BlockSpec((1,) + shape, lambda i, ps, tl, ea, eb, lo, hi, fl, n: (ea[i], 0, 0))
    spec_b = lambda shape: pl.BlockSpec((1,) + shape, lambda i, ps, tl, ea, eb, lo, hi, fl, n: (eb[i], 0, 0))
    shapes = ((D_MODEL, D_EXPERT), (D_MODEL, D_EXPERT), (D_EXPERT, D_MODEL))
    wscratch = lambda: tuple(pltpu.VMEM(s, bf16) for s in shapes)
    return pl.pallas_call(
        functools.partial(_moe_ffn_kernel, chunk=256, n_tiles=n_tiles),
        out_shape=jax.ShapeDtypeStruct((n_tok, SUBLANES, LANES), f32),
        grid_spec=pltpu.PrefetchScalarGridSpec(
            num_scalar_prefetch=8,
            grid=(max_items,),
            in_specs=[pl.BlockSpec(memory_space=pl.ANY)] + [spec_a(s) for s in shapes] + [spec_b(s) for s in shapes],
            out_specs=pl.BlockSpec(memory_space=pl.ANY),
            scratch_shapes=[pltpu.SMEM((n_tok,), jnp.int32),
                            pltpu.VMEM((2, tm * TOK_PITCH, LANES), f32),
                            pltpu.VMEM((2, tm * SUBLANES, LANES), f32),
                            pltpu.VMEM((tm, D_MODEL), bf16), pltpu.VMEM((tm, ROUTE_LANES), f32),
                            pltpu.VMEM((tm, D_MODEL), f32), wscratch(), wscratch(),
                            pltpu.SemaphoreType.DMA((2,)), pltpu.SemaphoreType.DMA((2,))]),
        compiler_params=_cparams(("arbitrary",), 60 * 1024 * 1024),
        name="moe_ffn",
    )(plan['pos'], plan['it_tile'], plan['it_ea'], plan['it_eb'], plan['it_lo'], plan['it_hi'],
      plan['it_flags'], plan['n_items'], tok3, w_gate, w_up, w_down, w_gate, w_up, w_down)


def _final_kernel(moe_ref, x1_ref, mod_ref, fg_ref, yp_ref, yl_ref, *, prompt_tiles):
    i = pl.program_id(0)
    tm = x1_ref.shape[0]
    y = _rms(x1_ref[...] + mod_ref[0, GT2:GT2 + 1, :] * _load_token_tiles(moe_ref, tm), fg_ref[...])

    @pl.when(i < prompt_tiles)
    def _():
        yp_ref[...] = y

    @pl.when(i >= prompt_tiles)
    def _():
        yl_ref[...] = y


def _final(moe, x1, mod, final_g, t_prompt, t_lat, seq_s):
    tm = 256
    npt, nst = t_prompt // tm, t_lat // tm
    spb = seq_s // tm
    moe2 = moe.reshape(moe.shape[0] * SUBLANES, LANES)
    return pl.pallas_call(
        functools.partial(_final_kernel, prompt_tiles=npt),
        out_shape=(jax.ShapeDtypeStruct((t_prompt, D_MODEL), f32),
                   jax.ShapeDtypeStruct((t_lat, D_MODEL), f32)),
        grid=(npt + nst,),
        in_specs=[pl.BlockSpec((tm * SUBLANES, LANES), lambda i: (i, 0)),
                  pl.BlockSpec((tm, D_MODEL), lambda i: (i, 0)),
                  pl.BlockSpec((1, MOD_ROWS, D_MODEL),
                               lambda i: (jnp.where(i < npt, 0, 1 + jnp.maximum(i - npt, 0) // spb), 0, 0)),
                  pl.BlockSpec((1, D_MODEL), lambda i: (0, 0))],
        out_specs=(pl.BlockSpec((tm, D_MODEL), lambda i: (jnp.minimum(i, npt - 1), 0)),
                   pl.BlockSpec((tm, D_MODEL), lambda i: (jnp.maximum(i - npt, 0), 0))),
        compiler_params=_cparams(("arbitrary",)),
        name="moe_combine_final",
    )(moe2, x1, mod, final_g.reshape(1, D_MODEL))


def _moe(tok, rt, x1, mod, w_gate, w_up, w_down, final_g, t_prompt, t_lat, seq_s):
    tm = MOE_TM
    n_rows = t_prompt + t_lat
    assert n_rows % tm == 0
    max_items = n_rows // tm + N_CLASSES
    plan = _route_plan(rt[:, ROUTE_E1], rt[:, ROUTE_E2], tm, max_items)
    moe = _moe_ffn(plan, tok, w_gate, w_up, w_down, tm, max_items)
    return _final(moe, x1, mod, final_g, t_prompt, t_lat, seq_s)


def kernel(x_prompt, x_sample, cache_k, cache_v, c, c_ctx, w_ada, b_ada, norm1_g, w_in, rpb, conv_w, conv_b, filt_w1, filt_b1, filt_w2, filt_b2, filt_w3, filt_freq, hyena_skip, gnorm_att, gnorm_hyena, w_out, norm2_g, router_grp_w, router_grp_b, router_exp_w, router_exp_b, w_gate, w_up, w_down, final_g):
    depth = w_ada.shape[0]
    assert depth == 1
    batch, seq, _ = x_prompt.shape
    dec_batch, dec_seq, _ = x_sample.shape
    l = 0

    def pack_router(grp, exp):
        rows = grp.shape[0]
        return jnp.concatenate([grp, jnp.zeros((rows, ROUTE_EXP_LANE0 - N_GROUPS), f32), exp,
                                jnp.zeros((rows, ROUTE_LANES - ROUTE_EXP_LANE0 - N_EXPERTS), f32)], axis=1)

    wr = pack_router(router_grp_w[l], router_exp_w[l])
    br = pack_router(router_grp_b[l][None, :], router_exp_b[l][None, :])

    lp = {
        'norm1_g': norm1_g[l], 'w_in': w_in[l], 'conv_w': conv_w[l], 'conv_b': conv_b[l],
        'filt_w1': filt_w1[l], 'filt_b1': filt_b1[l], 'filt_w2': filt_w2[l], 'filt_b2': filt_b2[l],
        'filt_w3': filt_w3[l], 'filt_freq': filt_freq[l], 'hyena_skip': hyena_skip[l],
        'gnorm_hyena': gnorm_hyena[l], 'w_out': w_out[l], 'norm2_g': norm2_g[l],
        'wr': wr, 'br': br, 'w_gate': w_gate[l], 'w_up': w_up[l], 'w_down': w_down[l],
    }

    cond8 = jnp.zeros((SUBLANES, D_MODEL), f32).at[0].set(c_ctx).at[1:1 + dec_batch].set(c)
    mod = _ada_mod(cond8, w_ada[l], b_ada[l], 1 + dec_batch).reshape(SUBLANES, N_MOD, D_MODEL)
    mod = jnp.pad(mod, ((0, 0), (0, MOD_ROWS - N_MOD), (0, 0)))
    mod_ctx, mod_lat = mod[0:1], mod[1:1 + dec_batch]

    xp = x_prompt.reshape(batch * seq, D_MODEL)
    xs = x_sample.reshape(dec_batch * dec_seq, D_MODEL)

    qp, k_ctx, v_ctx, hyp = _inproj(xp, mod_ctx, lp['norm1_g'], lp['w_in'], batch * seq)
    attp = _ctx_attention(qp, k_ctx, v_ctx, gnorm_att[l], seq)
    hyop = _hyena(hyp, lp, batch, seq)

    ql, kl, vl, hyl = _inproj(xs, mod_lat, lp['norm1_g'], lp['w_in'], dec_seq)
    kc = cache_k[:, l].reshape(dec_batch * cache_k.shape[2], D_ATT)
    vc = cache_v[:, l].reshape(dec_batch * cache_v.shape[2], D_ATT)
    attl = _na_attention(ql, kl, vl, kc, vc, rpb[l], gnorm_att[l], dec_batch, dec_seq)
    hyol = _hyena(hyl, lp, dec_batch, dec_seq)

    x1, tok, rt = _outproj(xp, xs, attp, attl, hyop, hyol, mod[0:1 + dec_batch], lp['w_out'], lp['norm2_g'],
                           lp['wr'], lp['br'], seq, dec_seq)
    yp, ys = _moe(tok, rt, x1, mod[0:1 + dec_batch], lp['w_gate'], lp['w_up'], lp['w_down'], final_g,
                  batch * seq, dec_batch * dec_seq, dec_seq)

    y_prompt = yp.reshape(batch, seq, D_MODEL)
    y_sample = ys.reshape(dec_batch, dec_seq, D_MODEL)
    new_k = k_ctx.reshape(batch, 1, seq, H_ATT, HEAD_DIM)
    new_v = v_ctx.reshape(batch, 1, seq, H_ATT, HEAD_DIM)
    return (y_prompt, y_sample, new_k, new_v)
```

```python
import functools
import math

import jax
import jax.numpy as jnp
import numpy as np
from jax import lax
from jax.experimental import pallas as pl
from jax.experimental.pallas import tpu as pltpu

f32 = jnp.float32
bf16 = jnp.bfloat16
HIGHEST = lax.Precision.HIGHEST

D_MODEL = 1024
GRID_W = 64
H_ATT = 8
HEAD_DIM = 64
D_ATT = H_ATT * HEAD_DIM
D_HYENA = 512
D_IN = 3 * D_ATT + 3 * D_HYENA
NA_ROWS = 8
NA_COLS = 16
SHORT_CONV = 3
FILTER_BANDS = 16
EMB_DIM = 1 + 2 * FILTER_BANDS
FILTER_FF = 64
DECAY_TARGET = 1e-2
MIN_DECAY = math.log(DECAY_TARGET) / 1.5
MAX_DECAY = math.log(DECAY_TARGET) / 0.3
N_GROUPS = 4
EXPERTS_PER_GROUP = 4
N_EXPERTS = N_GROUPS * EXPERTS_PER_GROUP
D_EXPERT = 512
N_MOD = 6
EPS = 1e-6
NEG_INF = -1e30
ATT_SCALE = HEAD_DIM ** -0.5

LANES = 128
SUBLANES = 8
MOD_ROWS = 8
ROUTE_LANES = 128
ROUTE_EXP_LANE0 = 16
ROUTE_E1, ROUTE_E2, ROUTE_W1, ROUTE_W2 = 0, 1, 2, 3
TOK_PITCH = 16
TOK_RT_ROW = 8
MOE_TM = 512
GATHER_PRIORITIES = (0,)
SCATTER_PRIORITIES = (1,)
VMEM_LIMIT = 56 * 1024 * 1024

SH1, SC1, GT1, SH2, SC2, GT2 = range(6)


def _cparams(sem, vmem=VMEM_LIMIT):
    return pltpu.CompilerParams(dimension_semantics=sem, vmem_limit_bytes=vmem)


def _dot(a, b):
    return jnp.dot(a, b, preferred_element_type=f32)


def _dot_hi(a, b):
    return lax.dot_general(a, b, (((1,), (0,)), ((), ())), precision=HIGHEST,
                           preferred_element_type=f32)


def _dot_nt(a, b):
    return lax.dot_general(a, b, (((1,), (1,)), ((), ())), preferred_element_type=f32)


def _rms(x, g):
    ms = jnp.mean(x * x, axis=-1, keepdims=True)
    return x * lax.rsqrt(ms + EPS) * g


def _cast_rows(src_ref, dst_ref, chunk):
    n = src_ref.shape[0] // chunk

    def body(i, c):
        r = pl.multiple_of(i * chunk, chunk)
        dst_ref[pl.ds(r, chunk), :] = src_ref[pl.ds(r, chunk), :].astype(dst_ref.dtype)
        return c

    lax.fori_loop(0, n, body, 0)


def _ada_kernel(ct_ref, w_ref, b_ref, o_ref, *, n_cond):
    ct = ct_ref[...]
    st = ct * (1.0 / (1.0 + jnp.exp(-ct)))
    w = w_ref[...]
    rid = lax.broadcasted_iota(jnp.int32, o_ref.shape, 0)
    out = jnp.broadcast_to(b_ref[...], o_ref.shape)
    for m in range(n_cond):
        row = jnp.sum(w * st[:, m:m + 1], axis=0, keepdims=True)
        out = out + jnp.where(rid == m, row, 0.0)
    o_ref[...] = out


def _ada_mod(cond8, w_ada, b_ada, n_cond):
    tn = 1536
    n = N_MOD * D_MODEL
    return pl.pallas_call(
        functools.partial(_ada_kernel, n_cond=n_cond),
        out_shape=jax.ShapeDtypeStruct((SUBLANES, n), f32),
        grid=(n // tn,),
        in_specs=[pl.BlockSpec((D_MODEL, SUBLANES), lambda j: (0, 0)),
                  pl.BlockSpec((D_MODEL, tn), lambda j: (0, j)),
                  pl.BlockSpec((1, tn), lambda j: (0, j))],
        out_specs=pl.BlockSpec((SUBLANES, tn), lambda j: (0, j)),
        compiler_params=_cparams(("arbitrary",)),
        name="ada_mod",
    )(cond8.T, w_ada, b_ada.reshape(1, n))


def _inproj_kernel(x_ref, mod_ref, g_ref, w_ref, q_ref, k_ref, v_ref, hy_ref, wbf_ref):
    @pl.when(pl.program_id(0) == 0)
    def _():
        _cast_rows(w_ref, wbf_ref, 128)

    h = _rms(x_ref[...], g_ref[...])
    h = h * (1.0 + mod_ref[0, SC1:SC1 + 1, :]) + mod_ref[0, SH1:SH1 + 1, :]
    p = _dot(h.astype(bf16), wbf_ref[...])
    q_ref[...] = p[:, 0:D_ATT].astype(q_ref.dtype)
    k_ref[...] = p[:, D_ATT:2 * D_ATT].astype(k_ref.dtype)
    v_ref[...] = p[:, 2 * D_ATT:3 * D_ATT].astype(v_ref.dtype)
    hy_ref[...] = p[:, 3 * D_ATT:]


def _inproj(x, mod, norm_g, w_in, rows_per_mod, kv_dtype):
    t = x.shape[0]
    tm = 512
    blocks_per_mod = rows_per_mod // tm
    return pl.pallas_call(
        _inproj_kernel,
        out_shape=(jax.ShapeDtypeStruct((t, D_ATT), bf16),
                   jax.ShapeDtypeStruct((t, D_ATT), kv_dtype),
                   jax.ShapeDtypeStruct((t, D_ATT), kv_dtype),
                   jax.ShapeDtypeStruct((t, 3 * D_HYENA), f32)),
        grid=(t // tm,),
        in_specs=[pl.BlockSpec((tm, D_MODEL), lambda i: (i, 0)),
                  pl.BlockSpec((1, MOD_ROWS, D_MODEL), lambda i: (i // blocks_per_mod, 0, 0)),
                  pl.BlockSpec((1, D_MODEL), lambda i: (0, 0)),
                  pl.BlockSpec((D_MODEL, D_IN), lambda i: (0, 0), pipeline_mode=pl.Buffered(1))],
        out_specs=(pl.BlockSpec((tm, D_ATT), lambda i: (i, 0)),
                   pl.BlockSpec((tm, D_ATT), lambda i: (i, 0)),
                   pl.BlockSpec((tm, D_ATT), lambda i: (i, 0)),
                   pl.BlockSpec((tm, 3 * D_HYENA), lambda i: (i, 0))),
        scratch_shapes=[pltpu.VMEM((D_MODEL, D_IN), bf16)],
        compiler_params=_cparams(("arbitrary",)),
        name="inproj",
    )(x, mod, norm_g.reshape(1, D_MODEL), w_in)


def _split_heads(q2):
    lane = lax.broadcasted_iota(jnp.int32, q2.shape, 1)
    qa = jnp.where(lane < HEAD_DIM, q2, 0.0)
    qb = jnp.where(lane >= HEAD_DIM, q2, 0.0)
    return jnp.concatenate([qa, qb], axis=0)


def _merge_heads(o_ab):
    m = o_ab.shape[0] // 2
    lane = lax.broadcasted_iota(jnp.int32, (m, LANES), 1)
    return jnp.where(lane < HEAD_DIM, o_ab[:m], o_ab[m:])


def _ctx_attn_kernel(q_ref, k_ref, v_ref, g_ref, o_ref):
    outs = []
    for p in range(D_ATT // LANES):
        cs = slice(p * LANES, (p + 1) * LANES)
        qq = _split_heads(q_ref[:, cs] * ATT_SCALE).astype(bf16)
        s = _dot_nt(qq, k_ref[:, cs].astype(bf16))
        m = jnp.max(s, axis=-1, keepdims=True)
        e = jnp.exp(s - m)
        l = jnp.sum(e, axis=-1, keepdims=True)
        o_ab = _dot(e.astype(bf16), v_ref[:, cs].astype(bf16)) / l
        outs.append(_merge_heads(o_ab))
    o_ref[...] = _rms(jnp.concatenate(outs, axis=-1), g_ref[...]).astype(o_ref.dtype)


def _ctx_attention(q, k, v, gnorm, seq):
    t = q.shape[0]
    spec = pl.BlockSpec((seq, D_ATT), lambda b: (b, 0))
    return pl.pallas_call(
        _ctx_attn_kernel,
        out_shape=jax.ShapeDtypeStruct((t, D_ATT), bf16),
        grid=(t // seq,),
        in_specs=[spec, spec, spec, pl.BlockSpec((1, D_ATT), lambda b: (0, 0))],
        out_specs=spec,
        compiler_params=_cparams(("arbitrary",)),
        name="ctx_attn",
    )(q, k, v, gnorm.reshape(1, D_ATT))


def _na_tables():
    col = np.arange(GRID_W)
    cs = np.clip(col - NA_COLS // 2, 0, GRID_W - NA_COLS)
    col_mask = (col[None, :] >= cs[:, None]) & (col[None, :] < cs[:, None] + NA_COLS)
    mask = np.tile(col_mask.astype(np.float32), (1, NA_ROWS))
    return mask


N_DR = 2 * NA_ROWS - 1
N_DC = 2 * NA_COLS - 1
BIAS_PAIRS = N_DR - 1


def _na_bias_rows(rpb):
    out = jnp.zeros((H_ATT, BIAS_PAIRS, LANES), f32)
    out = out.at[:, :, 0:N_DC].set(rpb[:, 0:BIAS_PAIRS])
    return out.at[:, :, GRID_W:GRID_W + N_DC].set(rpb[:, 1:N_DR])


def _na_row_start(r, rows):
    return jnp.clip(r - NA_ROWS // 2, 0, rows - NA_ROWS)


def _na_attn_kernel(q_ref, k_ref, v_ref, kc_ref, vc_ref, rp_ref, mask_ref, g_ref, o_ref,
                    t2_ref, *, rows):
    b = pl.program_id(0)
    r = pl.program_id(1)

    @pl.when((b == 0) & (r == 0))
    def _():
        for h in range(H_ATT):
            for i in range(BIAS_PAIRS):
                v = jnp.broadcast_to(rp_ref[h, i:i + 1, :], (GRID_W, LANES))
                t2_ref[h * BIAS_PAIRS + i] = pltpu.roll(v, LANES - (NA_COLS - 1), 1, stride=1, stride_axis=0)

    nwin = NA_ROWS * GRID_W
    rs = _na_row_start(r, rows)
    start = pl.multiple_of(rs * GRID_W, GRID_W)
    i0 = rs - r + NA_ROWS - 1
    valid = mask_ref[...] != 0.0
    valid2 = jnp.concatenate([valid, valid], axis=0)
    outs = []
    for p in range(D_ATT // LANES):
        cs = slice(p * LANES, (p + 1) * LANES)
        qq = _split_heads(q_ref[:, cs] * ATT_SCALE).astype(bf16)
        kw = k_ref[pl.ds(start, nwin), cs]
        vw = v_ref[pl.ds(start, nwin), cs]
        s_lat = _dot_nt(qq, kw)
        s_ctx = _dot_nt(qq, kc_ref[:, cs])
        bias2 = jnp.concatenate(
            [jnp.concatenate([t2_ref[(2 * p + hh) * BIAS_PAIRS + i0 + 2 * jp] for hh in range(2)], axis=0)
             for jp in range(NA_ROWS // 2)], axis=-1)
        s_lat = jnp.where(valid2, s_lat + bias2, NEG_INF)
        m = jnp.maximum(jnp.max(s_lat, axis=-1, keepdims=True), jnp.max(s_ctx, axis=-1, keepdims=True))
        e_lat = jnp.exp(s_lat - m)
        e_ctx = jnp.exp(s_ctx - m)
        l = jnp.sum(e_lat, axis=-1, keepdims=True) + jnp.sum(e_ctx, axis=-1, keepdims=True)
        o_ab = (_dot(e_lat.astype(bf16), vw) + _dot(e_ctx.astype(bf16), vc_ref[:, cs])) / l
        outs.append(_merge_heads(o_ab))
    o_ref[...] = _rms(jnp.concatenate(outs, axis=-1), g_ref[...]).astype(o_ref.dtype)


def _na_attention(q, k, v, kc, vc, rpb, gnorm, nb, seq):
    rows = seq // GRID_W
    past = kc.shape[0] // nb
    mask = _na_tables()
    return pl.pallas_call(
        functools.partial(_na_attn_kernel, rows=rows),
        out_shape=jax.ShapeDtypeStruct((nb * seq, D_ATT), bf16),
        grid=(nb, rows),
        in_specs=[pl.BlockSpec((GRID_W, D_ATT), lambda b, r: (b * rows + r, 0)),
                  pl.BlockSpec((seq, D_ATT), lambda b, r: (b, 0)),
                  pl.BlockSpec((seq, D_ATT), lambda b, r: (b, 0)),
                  pl.BlockSpec((past, D_ATT), lambda b, r: (b, 0)),
                  pl.BlockSpec((past, D_ATT), lambda b, r: (b, 0)),
                  pl.BlockSpec((H_ATT, BIAS_PAIRS, LANES), lambda b, r: (0, 0, 0)),
                  pl.BlockSpec((GRID_W, NA_ROWS * GRID_W), lambda b, r: (0, 0)),
                  pl.BlockSpec((1, D_ATT), lambda b, r: (0, 0))],
        out_specs=pl.BlockSpec((GRID_W, D_ATT), lambda b, r: (b * rows + r, 0)),
        scratch_shapes=[pltpu.VMEM((H_ATT * BIAS_PAIRS, GRID_W, LANES), f32)],
        compiler_params=_cparams(("arbitrary", "arbitrary")),
        name="na_attn",
    )(q, k, v, kc, vc, _na_bias_rows(rpb), jnp.asarray(mask), gnorm.reshape(1, D_ATT))


def _hy_front_kernel(x0_ref, x1_ref, v_ref, w0_ref, w1_ref, wv_ref, b0_ref, b1_ref, bv_ref,
                     zbf_ref, z_ref, x0c_ref):
    seq = x0_ref.shape[0]
    row = lax.broadcasted_iota(jnp.int32, x0_ref.shape, 0)
    first = row == 0
    last = row == seq - 1

    def conv(u_ref, w_ref, b_ref):
        u = u_ref[...]
        up = jnp.where(first, 0.0, pltpu.roll(u, 1, 0))
        un = jnp.where(last, 0.0, pltpu.roll(u, seq - 1, 0))
        y = b_ref[...] + up * w_ref[0:1, :]
        y = y + u * w_ref[1:2, :]
        return y + un * w_ref[2:3, :]

    z = conv(v_ref, wv_ref, bv_ref) * conv(x1_ref, w1_ref, b1_ref)
    z_ref[...] = z
    zbf_ref[...] = z.astype(bf16)
    x0c_ref[...] = conv(x0_ref, w0_ref, b0_ref)


def _hy_front(hy, conv_w, conv_b, nb, seq):
    tc = D_HYENA if seq <= 512 else D_HYENA // 2
    nc = D_HYENA // tc
    n = nb * D_HYENA
    cb = conv_b.reshape(1, 3 * D_HYENA)

    def part(k):
        return (pl.BlockSpec((seq, tc), lambda b, j: (b, k * nc + j)),
                pl.BlockSpec((SHORT_CONV, tc), lambda b, j: (0, k * nc + j)),
                pl.BlockSpec((1, tc), lambda b, j: (0, k * nc + j)))

    (x0s, w0s, b0s), (x1s, w1s, b1s), (vs, wvs, bvs) = part(0), part(1), part(2)
    ospec = pl.BlockSpec((seq, tc), lambda b, j: (0, b * nc + j))
    return pl.pallas_call(
        _hy_front_kernel,
        out_shape=(jax.ShapeDtypeStruct((seq, n), bf16),
                   jax.ShapeDtypeStruct((seq, n), f32),
                   jax.ShapeDtypeStruct((seq, n), f32)),
        grid=(nb, nc),
        in_specs=[x0s, x1s, vs, w0s, w1s, wvs, b0s, b1s, bvs],
        out_specs=(ospec, ospec, ospec),
        compiler_params=_cparams(("arbitrary", "arbitrary")),
        name="hyena_front",
    )(hy, hy, hy, conv_w, conv_w, conv_w, cb, cb, cb)


def _filter_features(seq):
    t = np.linspace(0.0, 1.0, seq, dtype=np.float64)[:, None]
    w = 2.0 * math.pi * np.arange(seq, dtype=np.float64)[:, None] / seq
    fb = np.linspace(1e-4, FILTER_BANDS - 1, FILTER_BANDS, dtype=np.float64)[None, :]
    ang = fb * w
    z = np.concatenate([t, np.cos(ang), -np.sin(ang)], axis=-1).astype(np.float32)
    return np.pad(z, ((0, 0), (0, LANES - EMB_DIM)))


def _filt_kernel(zf_ref, w1_ref, b1_ref, fr_ref, w2_ref, b2_ref, w3_ref, dl_ref, h_ref, kl_ref):
    i = pl.program_id(0)
    tr = zf_ref.shape[0]
    zf = zf_ref[...]
    fr = fr_ref[...]
    h = jnp.sin(fr * (_dot_hi(zf, w1_ref[...]) + b1_ref[...]))
    h = jnp.sin(fr * (_dot_hi(h, w2_ref[...]) + b2_ref[...]))
    h = _dot_hi(h, w3_ref[...])
    decay = jnp.exp(-zf[:, 0:1] * dl_ref[...])
    row = lax.broadcasted_iota(jnp.int32, (tr, D_HYENA), 0) + i * tr
    hf = h[:, :D_HYENA] * decay
    hb = jnp.where(row == 0, 0.0, h[:, D_HYENA:] * decay)
    h_ref[:, :D_HYENA] = (hf + hb).astype(bf16)
    h_ref[:, D_HYENA:] = (hb - hf).astype(bf16)
    alt = (1 - 2 * (row & 1)).astype(f32)
    part = jnp.sum(alt * (hf + hb), axis=0, keepdims=True)

    @pl.when(i == 0)
    def _():
        kl_ref[...] = jnp.zeros_like(kl_ref)

    kl_ref[...] += jnp.broadcast_to(part, kl_ref.shape)


def _hy_filters(seq, w1, b1, w2, b2, w3, freq):
    tr = 256
    zf = jnp.asarray(_filter_features(seq))
    deltas = np.abs(np.linspace(MIN_DECAY, MAX_DECAY, D_HYENA, dtype=np.float64))[None, :].astype(np.float32)
    w1p = jnp.pad(w1, ((0, LANES - EMB_DIM), (0, 0)))
    const = lambda shape: pl.BlockSpec(shape, lambda i: (0, 0))
    return pl.pallas_call(
        _filt_kernel,
        out_shape=(jax.ShapeDtypeStruct((seq, 2 * D_HYENA), bf16),
                   jax.ShapeDtypeStruct((SUBLANES, D_HYENA), f32)),
        grid=(seq // tr,),
        in_specs=[pl.BlockSpec((tr, LANES), lambda i: (i, 0)),
                  const((LANES, FILTER_FF)), const((1, FILTER_FF)), const((1, FILTER_FF)),
                  const((FILTER_FF, FILTER_FF)), const((1, FILTER_FF)),
                  const((FILTER_FF, 2 * D_HYENA)), const((1, D_HYENA))],
        out_specs=(pl.BlockSpec((tr, 2 * D_HYENA), lambda i: (i, 0)),
                   pl.BlockSpec((SUBLANES, D_HYENA), lambda i: (0, 0))),
        compiler_params=_cparams(("arbitrary",)),
        name="hyena_filters",
    )(zf, w1p, b1.reshape(1, -1), freq.reshape(1, -1), w2, b2.reshape(1, -1), w3, jnp.asarray(deltas))


def _dft_mats(seq):
    n = 2 * seq
    ph = (np.arange(seq, dtype=np.int64)[:, None] * np.arange(seq, dtype=np.int64)[None, :]) % n
    ang = ph.astype(np.float64) * (2.0 * math.pi / n)
    return np.cos(ang).astype(np.float32), np.sin(ang).astype(np.float32)


def _alt_col(rows, offset):
    row = lax.broadcasted_iota(jnp.int32, (rows, 1), 0) + offset
    return (1 - 2 * (row & 1)).astype(f32)


def _hy_fwd_kernel(fr_ref, fi_ref, z_ref, h_ref, kl_ref, yr_ref, yi_ref, yl_ref, kr_s, ki_s, *, n):
    i = pl.program_id(0)
    j = pl.program_id(1)
    tf = fr_ref.shape[0]
    tn = z_ref.shape[1]
    frb = fr_ref[...].astype(bf16)
    fib = fi_ref[...].astype(bf16)

    @pl.when(j == 0)
    def _():
        f = lax.broadcasted_iota(jnp.int32, (tf, 1), 0) + i * tf
        cf = jnp.where(f == 0, 1.0 / n, 2.0 / n)
        kr_s[...] = _dot(frb, h_ref[:, :D_HYENA]) * cf
        ki_s[...] = _dot(fib, h_ref[:, D_HYENA:]) * cf

    a = _dot(frb, z_ref[...])
    b = _dot(fib, z_ref[...])
    kr = kr_s[...]
    ki = ki_s[...]
    for c in range(tn // D_HYENA):
        cs = slice(c * D_HYENA, (c + 1) * D_HYENA)
        yr_ref[:, cs] = (a[:, cs] * kr + b[:, cs] * ki).astype(bf16)
        yi_ref[:, cs] = (b[:, cs] * kr - a[:, cs] * ki).astype(bf16)

    @pl.when(i == 0)
    def _():
        alt = _alt_col(z_ref.shape[0], 0)
        nz = jnp.sum(z_ref[...].astype(f32) * alt, axis=0, keepdims=True)
        kl = jnp.concatenate([kl_ref[0:1, :]] * (tn // D_HYENA), axis=-1)
        yl_ref[...] = jnp.broadcast_to(nz * kl * (1.0 / n), yl_ref.shape)


def _hy_fwd(fr, fi, zbf, hcat, kl, seq):
    n_cols = zbf.shape[1]
    tf = 256
    tn = min(n_cols, 2048)
    ni, nj = seq // tf, n_cols // tn
    assert ni == 1 or nj == 1
    return pl.pallas_call(
        functools.partial(_hy_fwd_kernel, n=2 * seq),
        out_shape=(jax.ShapeDtypeStruct((seq, n_cols), bf16),
                   jax.ShapeDtypeStruct((seq, n_cols), bf16),
                   jax.ShapeDtypeStruct((SUBLANES, n_cols), f32)),
        grid=(ni, nj),
        in_specs=[pl.BlockSpec((tf, seq), lambda i, j: (i, 0)),
                  pl.BlockSpec((tf, seq), lambda i, j: (i, 0)),
                  pl.BlockSpec((seq, tn), lambda i, j: (0, j)),
                  pl.BlockSpec((seq, 2 * D_HYENA), lambda i, j: (0, 0)),
                  pl.BlockSpec((SUBLANES, D_HYENA), lambda i, j: (0, 0))],
        out_specs=(pl.BlockSpec((tf, tn), lambda i, j: (i, j)),
                   pl.BlockSpec((tf, tn), lambda i, j: (i, j)),
                   pl.BlockSpec((SUBLANES, tn), lambda i, j: (0, j))),
        scratch_shapes=[pltpu.VMEM((tf, D_HYENA), f32), pltpu.VMEM((tf, D_HYENA), f32)],
        compiler_params=_cparams(("arbitrary", "arbitrary")),
        name="hyena_dft_fwd",
    )(fr, fi, zbf, hcat, kl)


def _hy_inv_kernel(fr_ref, fi_ref, yr_ref, yi_ref, yl_ref, z_ref, x0_ref, skip_ref, g_ref, o_ref):
    tt = fr_ref.shape[0]
    tn = yr_ref.shape[1]
    y = _dot(fr_ref[...].astype(bf16), yr_ref[...]) + _dot(fi_ref[...].astype(bf16), yi_ref[...])
    alt = _alt_col(tt, pl.program_id(0) * tt)
    for c in range(tn // D_HYENA):
        cs = slice(c * D_HYENA, (c + 1) * D_HYENA)
        yc = y[:, cs] + alt * yl_ref[0:1, cs] + z_ref[:, cs] * skip_ref[...]
        o_ref[:, cs] = _rms(yc * x0_ref[:, cs], g_ref[...]).astype(o_ref.dtype)


def _hy_inv(fr, fi, yr, yi, yl, z, x0c, skip, gnorm, seq):
    n_cols = z.shape[1]
    tt = 256
    tn = min(n_cols, 2048)
    blk = pl.BlockSpec((tt, tn), lambda i, j: (i, j))
    return pl.pallas_call(
        _hy_inv_kernel,
        out_shape=jax.ShapeDtypeStruct((seq, n_cols), bf16),
        grid=(seq // tt, n_cols // tn),
        in_specs=[pl.BlockSpec((tt, seq), lambda i, j: (i, 0)),
                  pl.BlockSpec((tt, seq), lambda i, j: (i, 0)),
                  pl.BlockSpec((seq, tn), lambda i, j: (0, j)),
                  pl.BlockSpec((seq, tn), lambda i, j: (0, j)),
                  pl.BlockSpec((SUBLANES, tn), lambda i, j: (0, j)),
                  blk, blk,
                  pl.BlockSpec((1, D_HYENA), lambda i, j: (0, 0)),
                  pl.BlockSpec((1, D_HYENA), lambda i, j: (0, 0))],
        out_specs=blk,
        compiler_params=_cparams(("arbitrary", "arbitrary")),
        name="hyena_dft_inv",
    )(fr, fi, yr, yi, yl, z, x0c, skip.reshape(1, D_HYENA), gnorm.reshape(1, D_HYENA))


def _hyena(hy, lp, nb, seq):
    fr_np, fi_np = _dft_mats(seq)
    fr, fi = jnp.asarray(fr_np), jnp.asarray(fi_np)
    zbf, z, x0c = _hy_front(hy, lp['conv_w'], lp['conv_b'], nb, seq)
    hcat, kl = _hy_filters(seq, lp['filt_w1'], lp['filt_b1'], lp['filt_w2'], lp['filt_b2'],
                           lp['filt_w3'], lp['filt_freq'])
    yr, yi, yl = _hy_fwd(fr, fi, zbf, hcat, kl, seq)
    return _hy_inv(fr, fi, yr, yi, yl, z, x0c, lp['hyena_skip'], lp['gnorm_hyena'], seq)


def _store_token_tiles(ref, x, pitch=SUBLANES):
    m = x.shape[0]
    for c in range(D_MODEL // LANES):
        ref[pl.ds(c, m, stride=pitch), :] = x[:, c * LANES:(c + 1) * LANES]


def _load_token_tiles(ref, m, lead=(), pitch=SUBLANES):
    return jnp.concatenate([ref[lead + (pl.ds(c, m, stride=pitch), slice(None))]
                            for c in range(D_MODEL // LANES)], axis=-1)


def _route(logits):
    lane_i = lax.broadcasted_iota(jnp.int32, logits.shape, 1)
    lane = lane_i.astype(f32)
    big = float(ROUTE_LANES)
    is_g = lane_i < N_GROUPS
    mg = jnp.max(jnp.where(is_g, logits, -jnp.inf), axis=-1, keepdims=True)
    sg = jnp.sum(jnp.where(is_g, jnp.exp(logits - mg), 0.0), axis=-1, keepdims=True)
    g_w = 1.0 / sg
    g_idx = jnp.min(jnp.where(is_g & (logits == mg), lane, big), axis=-1, keepdims=True)
    e_id = lane_i - ROUTE_EXP_LANE0
    sel = (e_id >= 0) & (e_id < N_EXPERTS) & ((e_id >> 2).astype(f32) == g_idx)
    me = jnp.max(jnp.where(sel, logits, -jnp.inf), axis=-1, keepdims=True)
    ee = jnp.where(sel, jnp.exp(logits - me), 0.0)
    prob = ee / jnp.sum(ee, axis=-1, keepdims=True)
    p1 = jnp.max(jnp.where(sel, prob, -1.0), axis=-1, keepdims=True)
    i1 = jnp.min(jnp.where(sel & (prob == p1), lane, big), axis=-1, keepdims=True)
    sel2 = sel & (lane != i1)
    p2 = jnp.max(jnp.where(sel2, prob, -1.0), axis=-1, keepdims=True)
    i2 = jnp.min(jnp.where(sel2 & (prob == p2), lane, big), axis=-1, keepdims=True)
    tot = p1 + p2
    rec = jnp.where(lane_i == ROUTE_E1, i1 - ROUTE_EXP_LANE0, 0.0)
    rec = jnp.where(lane_i == ROUTE_E2, i2 - ROUTE_EXP_LANE0, rec)
    rec = jnp.where(lane_i == ROUTE_W1, g_w * (p1 / tot), rec)
    return jnp.where(lane_i == ROUTE_W2, g_w * (p2 / tot), rec)


def _outproj_kernel(xp_ref, xs_ref, attp_ref, atts_ref, hyp_ref, hys_ref, mod_ref, wo_ref, g2_ref,
                    wr_ref, br_ref, x1_ref, tok_ref, rt_ref, wobf_ref, wrh_ref, wrl_ref, *, prompt_tiles):
    i = pl.program_id(0)

    @pl.when(i == 0)
    def _():
        _cast_rows(wo_ref, wobf_ref, 128)
        wr = wr_ref[...]
        hi = wr.astype(bf16)
        wrh_ref[...] = hi
        wrl_ref[...] = (wr - hi.astype(f32)).astype(bf16)

    is_p = i < prompt_tiles
    hyp = jnp.concatenate([hyp_ref[:, b * D_HYENA:(b + 1) * D_HYENA]
                           for b in range(hyp_ref.shape[1] // D_HYENA)], axis=0)
    x = jnp.where(is_p, xp_ref[...], xs_ref[...])
    att = jnp.where(is_p, attp_ref[...], atts_ref[...])
    hyo = jnp.where(is_p, hyp, hys_ref[...])
    proj = _dot(att, wobf_ref[0:D_ATT, :]) + _dot(hyo, wobf_ref[D_ATT:, :])
    x1 = x + mod_ref[0, GT1:GT1 + 1, :] * proj
    x1_ref[...] = x1
    h2 = _rms(x1, g2_ref[...]) * (1.0 + mod_ref[0, SC2:SC2 + 1, :]) + mod_ref[0, SH2:SH2 + 1, :]
    h2h = h2.astype(bf16)
    h2l = (h2 - h2h.astype(f32)).astype(bf16)
    logits = _dot(h2h, wrh_ref[...]) + _dot(h2l, wrh_ref[...]) + _dot(h2h, wrl_ref[...]) + br_ref[...]
    rt = _route(logits)
    rt_ref[...] = rt
    tok_ref[...] = jnp.zeros_like(tok_ref)
    _store_token_tiles(tok_ref, h2, TOK_PITCH)
    tok_ref[pl.ds(TOK_RT_ROW, h2.shape[0], stride=TOK_PITCH), :] = rt


def _outproj(xp, xs, attp, atts, hyp, hys, mod, w_out, norm2_g, wr, br, seq_p, seq_s):
    tm = 512
    tp, ts = xp.shape[0], xs.shape[0]
    npt, nst = tp // tm, ts // tm
    assert tm % seq_p == 0 and seq_s % tm == 0
    spb = seq_s // tm
    bpt = tm // seq_p
    p_idx = lambda i: jnp.minimum(i, npt - 1)
    s_idx = lambda i: jnp.maximum(i - npt, 0)
    const = lambda shape: pl.BlockSpec(shape, lambda i: (0,) * len(shape))
    return pl.pallas_call(
        functools.partial(_outproj_kernel, prompt_tiles=npt),
        out_shape=(jax.ShapeDtypeStruct((tp + ts, D_MODEL), f32),
                   jax.ShapeDtypeStruct(((tp + ts) * TOK_PITCH, LANES), f32),
                   jax.ShapeDtypeStruct((tp + ts, ROUTE_LANES), f32)),
        grid=(npt + nst,),
        in_specs=[pl.BlockSpec((tm, D_MODEL), lambda i: (p_idx(i), 0)),
                  pl.BlockSpec((tm, D_MODEL), lambda i: (s_idx(i), 0)),
                  pl.BlockSpec((tm, D_ATT), lambda i: (p_idx(i), 0)),
                  pl.BlockSpec((tm, D_ATT), lambda i: (s_idx(i), 0)),
                  pl.BlockSpec((seq_p, bpt * D_HYENA), lambda i: (0, p_idx(i))),
                  pl.BlockSpec((tm, D_HYENA), lambda i: (s_idx(i) % spb, s_idx(i) // spb)),
                  pl.BlockSpec((1, MOD_ROWS, D_MODEL),
                               lambda i: (jnp.where(i < npt, 0, 1 + s_idx(i) // spb), 0, 0)),
                  const((D_MODEL, D_MODEL)), const((1, D_MODEL)),
                  const((D_MODEL, ROUTE_LANES)), const((1, ROUTE_LANES))],
        out_specs=(pl.BlockSpec((tm, D_MODEL), lambda i: (i, 0)),
                   pl.BlockSpec((tm * TOK_PITCH, LANES), lambda i: (i, 0)),
                   pl.BlockSpec((tm, ROUTE_LANES), lambda i: (i, 0))),
        scratch_shapes=[pltpu.VMEM((D_MODEL, D_MODEL), bf16),
                        pltpu.VMEM((D_MODEL, ROUTE_LANES), bf16), pltpu.VMEM((D_MODEL, ROUTE_LANES), bf16)],
        compiler_params=_cparams(("arbitrary",)),
        name="outproj_router",
    )(xp, xs, attp, atts, hyp, hys, mod, w_out, norm2_g.reshape(1, D_MODEL), wr, br)


PAIRS_PER_GROUP = 6
N_CLASSES = N_GROUPS * PAIRS_PER_GROUP
PAIR_SLOT_A = (0, 0, 0, 1, 1, 3)
PAIR_SLOT_B = (1, 2, 3, 3, 2, 2)
FLAG_NEW_A, FLAG_NEW_B, FLAG_TILE_START, FLAG_TILE_END = 1, 2, 4, 8


CLASS_ROWS = 32
TAB_TILE, TAB_EA, TAB_EB, TAB_LO, TAB_HI, TAB_FLAGS, TAB_N = range(7)


def _select_by(idx, values):
    out = jnp.full(idx.shape, float(values[-1]), f32)
    for i in range(len(values) - 2, -1, -1):
        out = jnp.where(idx == i, float(values[i]), out)
    return out


def _plan_kernel(e1_ref, e2_ref, pos_ref, tab_ref, *, tm):
    e1 = e1_ref[...]
    e2 = e2_ref[...]
    rows = e1.shape[0]
    grp = jnp.floor(e1 * (1.0 / EXPERTS_PER_GROUP))
    l1 = e1 - EXPERTS_PER_GROUP * grp
    l2 = e2 - EXPERTS_PER_GROUP * jnp.floor(e2 * (1.0 / EXPERTS_PER_GROUP))
    lo, hi = jnp.minimum(l1, l2), jnp.maximum(l1, l2)
    pair = jnp.where(lo == 0, hi - 1, jnp.where(lo == 1, jnp.where(hi == 3, 3.0, 4.0), 5.0))
    cls = grp * PAIRS_PER_GROUP + pair

    ri = lax.broadcasted_iota(jnp.int32, (LANES, LANES), 0)
    ci = lax.broadcasted_iota(jnp.int32, (LANES, LANES), 1)
    upper = (ri <= ci).astype(bf16)
    rr = lax.broadcasted_iota(jnp.int32, (rows, rows), 0)
    rc = lax.broadcasted_iota(jnp.int32, (rows, rows), 1)
    strict_lower = (rc < rr).astype(bf16)
    cid = lax.broadcasted_iota(jnp.int32, (CLASS_ROWS, 1), 0)

    pos = jnp.zeros(e1.shape, f32)
    base = jnp.zeros((1, 1), f32)
    cnt_col = jnp.zeros((CLASS_ROWS, 1), f32)
    off_col = jnp.zeros((CLASS_ROWS, 1), f32)
    for c in range(N_CLASSES):
        m = cls == c
        within = _dot(m.astype(bf16), upper)
        tot = jnp.broadcast_to(within[:, LANES - 1:LANES], within.shape)
        before = _dot(strict_lower, tot.astype(bf16))[:, 0:1]
        count = jnp.sum(within[:, LANES - 1:LANES], axis=0, keepdims=True)
        pos = jnp.where(m, base + before + within - 1.0, pos)
        cnt_col = jnp.where(cid == c, count, cnt_col)
        off_col = jnp.where(cid == c, base, off_col)
        base = base + count
    pos_ref[...] = pos.astype(jnp.int32)

    end_col = off_col + cnt_col
    inv_tm = 1.0 / tm
    first_col = jnp.floor(off_col * inv_tm)
    ntl_col = jnp.where(cnt_col > 0, jnp.floor((end_col - 1.0) * inv_tm) - first_col + 1.0, 0.0)
    cr = lax.broadcasted_iota(jnp.int32, (CLASS_ROWS, CLASS_ROWS), 0)
    cc = lax.broadcasted_iota(jnp.int32, (CLASS_ROWS, CLASS_ROWS), 1)
    lower = (cc <= cr).astype(bf16)
    iend_col = _dot(lower, jnp.broadcast_to(ntl_col, (CLASS_ROWS, LANES)).astype(bf16))[:, 0:1]
    istart_col = iend_col - ntl_col
    n_items = jnp.sum(ntl_col, axis=0, keepdims=True)
    slots = lax.broadcasted_iota(jnp.int32, (1, LANES), 1).astype(f32)
    items = jnp.minimum(slots, n_items - 1.0)
    past = jnp.where((items >= iend_col) & (cid < N_CLASSES), 1.0, 0.0)
    it_cls = jnp.minimum(jnp.sum(past, axis=0, keepdims=True), N_CLASSES - 1.0)
    sel = it_cls == cid.astype(f32)
    pick = lambda col: jnp.sum(jnp.where(sel, col, 0.0), axis=0, keepdims=True)
    grp_col = jnp.floor(cid.astype(f32) * (1.0 / PAIRS_PER_GROUP))
    pair_col = cid.astype(f32) - PAIRS_PER_GROUP * grp_col
    ea_col = EXPERTS_PER_GROUP * grp_col + _select_by(pair_col, PAIR_SLOT_A)
    eb_col = EXPERTS_PER_GROUP * grp_col + _select_by(pair_col, PAIR_SLOT_B)
    it_tile = pick(first_col) + items - pick(istart_col)
    it_lo = jnp.maximum(pick(off_col) - it_tile * tm, 0.0)
    it_hi = jnp.minimum(pick(end_col) - it_tile * tm, float(tm))
    it_ea, it_eb = pick(ea_col), pick(eb_col)
    prev = lambda v: pltpu.roll(jnp.broadcast_to(v, (SUBLANES, LANES)), 1, 1)[0:1]
    changed = lambda v: jnp.where((slots == 0) | (v != prev(v)), 1.0, 0.0)
    flags = (FLAG_NEW_A * changed(it_ea) + FLAG_NEW_B * changed(it_eb)
             + FLAG_TILE_START * jnp.where(it_lo == 0, 1.0, 0.0) + FLAG_TILE_END * jnp.where(it_hi == tm, 1.0, 0.0))
    table = {TAB_TILE: it_tile, TAB_EA: it_ea, TAB_EB: it_eb, TAB_LO: it_lo, TAB_HI: it_hi, TAB_FLAGS: flags,
             TAB_N: jnp.broadcast_to(n_items, (1, LANES))}
    trow = lax.broadcasted_iota(jnp.int32, tab_ref.shape, 0)
    tab = jnp.zeros(tab_ref.shape, f32)
    for r, v in table.items():
        tab = jnp.where(trow == r, v, tab)
    tab_ref[...] = tab.astype(jnp.int32)


def _route_plan(e1, e2, tm, max_items):
    t = e1.shape[0]
    assert t % LANES == 0 and max_items <= LANES and tm & (tm - 1) == 0
    pos, tab = pl.pallas_call(
        functools.partial(_plan_kernel, tm=tm),
        out_shape=(jax.ShapeDtypeStruct((t // LANES, LANES), jnp.int32),
                   jax.ShapeDtypeStruct((SUBLANES, LANES), jnp.int32)),
        name="moe_plan",
    )(e1.reshape(t // LANES, LANES), e2.reshape(t // LANES, LANES))
    row = lambda r: tab[r, :max_items]
    return dict(pos=pos.reshape(t), n_items=tab[TAB_N, :1], it_tile=row(TAB_TILE), it_ea=row(TAB_EA),
                it_eb=row(TAB_EB), it_lo=row(TAB_LO), it_hi=row(TAB_HI), it_flags=row(TAB_FLAGS))


ROW_DMA_UNROLL = 8


def _row_dma_loop(g0, g1, make_copy, priorities):
    def body(i, c):
        for u in range(ROW_DMA_UNROLL):
            make_copy(i * ROW_DMA_UNROLL + u).start(priority=priorities[u % len(priorities)])
        return c

    lax.fori_loop(g0, g1, body, 0)


def _moe_ffn_kernel(pos_ref, it_tile, it_ea, it_eb, it_lo, it_hi, it_flags, n_items, tok_hbm,
                    wga_ref, wua_ref, wda_ref, wgb_ref, wub_ref, wdb_ref, out_hbm,
                    src_s, xbuf, ybuf, x_s, rt_s, acc_s, wa_s, wb_s, sem_in, sem_out, *, chunk, n_tiles):
    i = pl.program_id(0)
    tm = x_s.shape[0]
    groups = tm // ROW_DMA_UNROLL

    def gather_tile(k):
        base = k * tm
        slot = k % 2
        _row_dma_loop(0, groups, lambda r: pltpu.make_async_copy(
            tok_hbm.at[src_s[base + r]],
            xbuf.at[slot, pl.ds(pl.multiple_of(r * TOK_PITCH, TOK_PITCH), TOK_PITCH), :],
            sem_in.at[slot]), GATHER_PRIORITIES)

    def scatter_tile(k):
        base = k * tm
        slot = k % 2
        _row_dma_loop(0, groups, lambda r: pltpu.make_async_copy(
            ybuf.at[slot, pl.ds(pl.multiple_of(r * SUBLANES, SUBLANES), SUBLANES), :],
            out_hbm.at[src_s[base + r]],
            sem_out.at[slot]), SCATTER_PRIORITIES)

    def expert_ffn(x, w, wg, wu, wd):
        g = _dot(x, wg[...])
        u = _dot(x, wu[...])
        hid = (g * (1.0 / (1.0 + jnp.exp(-g)))) * u
        return _dot((hid * w).astype(bf16), wd[...])

    def wait_all(buf, sem, slot):
        pltpu.make_async_copy(buf.at[slot], buf.at[slot], sem.at[slot]).wait()

    @pl.when(i == 0)
    def _():
        def inv(a, c):
            for u in range(ROW_DMA_UNROLL):
                src_s[pos_ref[a * ROW_DMA_UNROLL + u]] = a * ROW_DMA_UNROLL + u
            return c

        lax.fori_loop(0, pos_ref.shape[0] // ROW_DMA_UNROLL, inv, 0)
        gather_tile(0)

    @pl.when(i < n_items[0])
    def _():
        flags = it_flags[i]
        k = it_tile[i]

        @pl.when((flags & FLAG_NEW_A) != 0)
        def _():
            for dst, src in zip(wa_s, (wga_ref, wua_ref, wda_ref)):
                dst[...] = src[0].astype(bf16)

        @pl.when((flags & FLAG_NEW_B) != 0)
        def _():
            for dst, src in zip(wb_s, (wgb_ref, wub_ref, wdb_ref)):
                dst[...] = src[0].astype(bf16)

        @pl.when((flags & FLAG_TILE_START) != 0)
        def _():
            @pl.when(k + 1 < n_tiles)
            def _():
                gather_tile(k + 1)

            wait_all(xbuf, sem_in, k % 2)
            x_s[...] = _load_token_tiles(xbuf, tm, (k % 2,), TOK_PITCH).astype(bf16)
            rt_s[...] = xbuf[k % 2, pl.ds(TOK_RT_ROW, tm, stride=TOK_PITCH), :]
            acc_s[...] = jnp.zeros_like(acc_s)

        lo = it_lo[i]
        hi = it_hi[i]
        ea = it_ea[i].astype(f32)

        def body(j, c):
            r = pl.multiple_of(j * chunk, chunk)

            @pl.when((r < hi) & (r + chunk > lo))
            def _():
                x = x_s[pl.ds(r, chunk), :]
                rt = rt_s[pl.ds(r, chunk), :]
                first_is_a = rt[:, ROUTE_E1:ROUTE_E1 + 1] == ea
                w1 = rt[:, ROUTE_W1:ROUTE_W1 + 1]
                w2 = rt[:, ROUTE_W2:ROUTE_W2 + 1]
                y = (expert_ffn(x, jnp.where(first_is_a, w1, w2), *wa_s)
                     + expert_ffn(x, jnp.where(first_is_a, w2, w1), *wb_s))
                row = lax.broadcasted_iota(jnp.int32, (chunk, 1), 0) + r
                mine = (row >= lo) & (row < hi)
                acc_s[pl.ds(r, chunk), :] = jnp.where(mine, y, acc_s[pl.ds(r, chunk), :])

            return c

        lax.fori_loop(0, tm // chunk, body, 0)

        @pl.when((flags & FLAG_TILE_END) != 0)
        def _():
            @pl.when(k >= 2)
            def _():
                wait_all(ybuf, sem_out, k % 2)

            _store_token_tiles(ybuf.at[k % 2], acc_s[...])
            scatter_tile(k)

            @pl.when(k == n_tiles - 1)
            def _():
                if n_tiles > 1:
                    wait_all(ybuf, sem_out, (n_tiles - 2) % 2)
                wait_all(ybuf, sem_out, (n_tiles - 1) % 2)


def _moe_ffn(plan, tok, w_gate, w_up, w_down, tm, max_items):
    n_tok = tok.shape[0] // TOK_PITCH
    n_tiles = n_tok // tm
    tok3 = tok.reshape(n_tok, TOK_PITCH, LANES)
    spec_a = lambda shape: pl.BlockSpec((1,) + shape, lambda i, ps, tl, ea, eb, lo, hi, fl, n: (ea[i], 0, 0))
    spec_b = lambda shape: pl.BlockSpec((1,) + shape, lambda i, ps, tl, ea, eb, lo, hi, fl, n: (eb[i], 0, 0))
    shapes = ((D_MODEL, D_EXPERT), (D_MODEL, D_EXPERT), (D_EXPERT, D_MODEL))
    wscratch = lambda: tuple(pltpu.VMEM(s, bf16) for s in shapes)
    return pl.pallas_call(
        functools.partial(_moe_ffn_kernel, chunk=128, n_tiles=n_tiles),
        out_shape=jax.ShapeDtypeStruct((n_tok, SUBLANES, LANES), f32),
        grid_spec=pltpu.PrefetchScalarGridSpec(
            num_scalar_prefetch=8,
            grid=(max_items,),
            in_specs=[pl.BlockSpec(memory_space=pl.ANY)] + [spec_a(s) for s in shapes] + [spec_b(s) for s in shapes],
            out_specs=pl.BlockSpec(memory_space=pl.ANY),
            scratch_shapes=[pltpu.SMEM((n_tok,), jnp.int32),
                            pltpu.VMEM((2, tm * TOK_PITCH, LANES), f32),
                            pltpu.VMEM((2, tm * SUBLANES, LANES), f32),
                            pltpu.VMEM((tm, D_MODEL), bf16), pltpu.VMEM((tm, ROUTE_LANES), f32),
                            pltpu.VMEM((tm, D_MODEL), f32), wscratch(), wscratch(),
                            pltpu.SemaphoreType.DMA((2,)), pltpu.SemaphoreType.DMA((2,))]),
        compiler_params=_cparams(("arbitrary",), 60 * 1024 * 1024),
        name="moe_ffn",
    )(plan['pos'], plan['it_tile'], plan['it_ea'], plan['it_eb'], plan['it_lo'], plan['it_hi'],
      plan['it_flags'], plan['n_items'], tok3, w_gate, w_up, w_down, w_gate, w_up, w_down)


def _final_kernel(moe_ref, x1_ref, mod_ref, fg_ref, yp_ref, yl_ref, *, prompt_tiles):
    i = pl.program_id(0)
    tm = x1_ref.shape[0]
    y = _rms(x1_ref[...] + mod_ref[0, GT2:GT2 + 1, :] * _load_token_tiles(moe_ref, tm), fg_ref[...])

    @pl.when(i < prompt_tiles)
    def _():
        yp_ref[...] = y

    @pl.when(i >= prompt_tiles)
    def _():
        yl_ref[...] = y


def _final(moe, x1, mod, final_g, t_prompt, t_lat, seq_s):
    tm = 256
    npt, nst = t_prompt // tm, t_lat // tm
    spb = seq_s // tm
    moe2 = moe.reshape(moe.shape[0] * SUBLANES, LANES)
    return pl.pallas_call(
        functools.partial(_final_kernel, prompt_tiles=npt),
        out_shape=(jax.ShapeDtypeStruct((t_prompt, D_MODEL), f32),
                   jax.ShapeDtypeStruct((t_lat, D_MODEL), f32)),
        grid=(npt + nst,),
        in_specs=[pl.BlockSpec((tm * SUBLANES, LANES), lambda i: (i, 0)),
                  pl.BlockSpec((tm, D_MODEL), lambda i: (i, 0)),
                  pl.BlockSpec((1, MOD_ROWS, D_MODEL),
                               lambda i: (jnp.where(i < npt, 0, 1 + jnp.maximum(i - npt, 0) // spb), 0, 0)),
                  pl.BlockSpec((1, D_MODEL), lambda i: (0, 0))],
        out_specs=(pl.BlockSpec((tm, D_MODEL), lambda i: (jnp.minimum(i, npt - 1), 0)),
                   pl.BlockSpec((tm, D_MODEL), lambda i: (jnp.maximum(i - npt, 0), 0))),
        compiler_params=_cparams(("arbitrary",)),
        name="moe_combine_final",
    )(moe2, x1, mod, final_g.reshape(1, D_MODEL))


def _moe(tok, rt, x1, mod, w_gate, w_up, w_down, final_g, t_prompt, t_lat, seq_s):
    tm = MOE_TM
    n_rows = t_prompt + t_lat
    assert n_rows % tm == 0
    max_items = n_rows // tm + N_CLASSES
    plan = _route_plan(rt[:, ROUTE_E1], rt[:, ROUTE_E2], tm, max_items)
    moe = _moe_ffn(plan, tok, w_gate, w_up, w_down, tm, max_items)
    return _final(moe, x1, mod, final_g, t_prompt, t_lat, seq_s)


def kernel(x_prompt, x_sample, cache_k, cache_v, c, c_ctx, w_ada, b_ada, norm1_g, w_in, rpb, conv_w, conv_b, filt_w1, filt_b1, filt_w2, filt_b2, filt_w3, filt_freq, hyena_skip, gnorm_att, gnorm_hyena, w_out, norm2_g, router_grp_w, router_grp_b, router_exp_w, router_exp_b, w_gate, w_up, w_down, final_g):
    depth = w_ada.shape[0]
    assert depth == 1
    batch, seq, _ = x_prompt.shape
    dec_batch, dec_seq, _ = x_sample.shape
    l = 0

    def pack_router(grp, exp):
        rows = grp.shape[0]
        return jnp.concatenate([grp, jnp.zeros((rows, ROUTE_EXP_LANE0 - N_GROUPS), f32), exp,
                                jnp.zeros((rows, ROUTE_LANES - ROUTE_EXP_LANE0 - N_EXPERTS), f32)], axis=1)

    wr = pack_router(router_grp_w[l], router_exp_w[l])
    br = pack_router(router_grp_b[l][None, :], router_exp_b[l][None, :])

    lp = {
        'norm1_g': norm1_g[l], 'w_in': w_in[l], 'conv_w': conv_w[l], 'conv_b': conv_b[l],
        'filt_w1': filt_w1[l], 'filt_b1': filt_b1[l], 'filt_w2': filt_w2[l], 'filt_b2': filt_b2[l],
        'filt_w3': filt_w3[l], 'filt_freq': filt_freq[l], 'hyena_skip': hyena_skip[l],
        'gnorm_hyena': gnorm_hyena[l], 'w_out': w_out[l], 'norm2_g': norm2_g[l],
        'wr': wr, 'br': br, 'w_gate': w_gate[l], 'w_up': w_up[l], 'w_down': w_down[l],
    }

    cond8 = jnp.concatenate([c_ctx[None, :], c, jnp.zeros((SUBLANES - 1 - dec_batch, D_MODEL), f32)], axis=0)
    mod = _ada_mod(cond8, w_ada[l], b_ada[l], 1 + dec_batch).reshape(SUBLANES, N_MOD, D_MODEL)
    mod = jnp.pad(mod, ((0, 0), (0, MOD_ROWS - N_MOD), (0, 0)))
    mod_ctx, mod_lat = mod[0:1], mod[1:1 + dec_batch]

    xp = x_prompt.reshape(batch * seq, D_MODEL)
    xs = x_sample.reshape(dec_batch * dec_seq, D_MODEL)

    qp, k_ctx, v_ctx, hyp = _inproj(xp, mod_ctx, lp['norm1_g'], lp['w_in'], batch * seq, f32)
    attp = _ctx_attention(qp, k_ctx, v_ctx, gnorm_att[l], seq)
    hyop = _hyena(hyp, lp, batch, seq)

    ql, kl, vl, hyl = _inproj(xs, mod_lat, lp['norm1_g'], lp['w_in'], dec_seq, bf16)
    kc = cache_k[:, l].reshape(dec_batch * cache_k.shape[2], D_ATT).astype(bf16)
    vc = cache_v[:, l].reshape(dec_batch * cache_v.shape[2], D_ATT).astype(bf16)
    attl = _na_attention(ql, kl, vl, kc, vc, rpb[l], gnorm_att[l], dec_batch, dec_seq)
    hyol = _hyena(hyl, lp, dec_batch, dec_seq)

    x1, tok, rt = _outproj(xp, xs, attp, attl, hyop, hyol, mod[0:1 + dec_batch], lp['w_out'], lp['norm2_g'],
                           lp['wr'], lp['br'], seq, dec_seq)
    yp, ys = _moe(tok, rt, x1, mod[0:1 + dec_batch], lp['w_gate'], lp['w_up'], lp['w_down'], final_g,
                  batch * seq, dec_batch * dec_seq, dec_seq)

    y_prompt = yp.reshape(batch, seq, D_MODEL)
    y_sample = ys.reshape(dec_batch, dec_seq, D_MODEL)
    new_k = k_ctx.reshape(batch, 1, seq, H_ATT, HEAD_DIM)
    new_v = v_ctx.reshape(batch, 1, seq, H_ATT, HEAD_DIM)
    return (y_prompt, y_sample, new_k, new_v)
```

```python
import functools
import math

import jax
import jax.numpy as jnp
import numpy as np
from jax import lax
from jax.experimental import pallas as pl
from jax.experimental.pallas import tpu as pltpu

f32 = jnp.float32
bf16 = jnp.bfloat16
HIGHEST = lax.Precision.HIGHEST

D_MODEL = 1024
GRID_W = 64
H_ATT = 8
HEAD_DIM = 64
D_ATT = H_ATT * HEAD_DIM
D_HYENA = 512
D_IN = 3 * D_ATT + 3 * D_HYENA
NA_ROWS = 8
NA_COLS = 16
SHORT_CONV = 3
FILTER_BANDS = 16
EMB_DIM = 1 + 2 * FILTER_BANDS
FILTER_FF = 64
DECAY_TARGET = 1e-2
MIN_DECAY = math.log(DECAY_TARGET) / 1.5
MAX_DECAY = math.log(DECAY_TARGET) / 0.3
N_GROUPS = 4
EXPERTS_PER_GROUP = 4
N_EXPERTS = N_GROUPS * EXPERTS_PER_GROUP
D_EXPERT = 512
N_MOD = 6
EPS = 1e-6
NEG_INF = -1e30
ATT_SCALE = HEAD_DIM ** -0.5

LANES = 128
SUBLANES = 8
MOD_ROWS = 8
ROUTE_LANES = 128
ROUTE_EXP_LANE0 = 16
ROUTE_E1, ROUTE_E2, ROUTE_W1, ROUTE_W2 = 0, 1, 2, 3
TOK_PITCH = 16
TOK_RT_ROW = 8
MOE_TM = 512
GATHER_PRIORITIES = (0,)
SCATTER_PRIORITIES = (1,)
VMEM_LIMIT = 56 * 1024 * 1024

SH1, SC1, GT1, SH2, SC2, GT2 = range(6)


def _cparams(sem, vmem=VMEM_LIMIT):
    return pltpu.CompilerParams(dimension_semantics=sem, vmem_limit_bytes=vmem)


def _dot(a, b):
    return jnp.dot(a, b, preferred_element_type=f32)


def _dot_hi(a, b):
    return lax.dot_general(a, b, (((1,), (0,)), ((), ())), precision=HIGHEST,
                           preferred_element_type=f32)


def _dot_nt(a, b):
    return lax.dot_general(a, b, (((1,), (1,)), ((), ())), preferred_element_type=f32)


def _rms(x, g):
    ms = jnp.mean(x * x, axis=-1, keepdims=True)
    return x * lax.rsqrt(ms + EPS) * g


def _cast_rows(src_ref, dst_ref, chunk):
    n = src_ref.shape[0] // chunk

    def body(i, c):
        r = pl.multiple_of(i * chunk, chunk)
        dst_ref[pl.ds(r, chunk), :] = src_ref[pl.ds(r, chunk), :].astype(dst_ref.dtype)
        return c

    lax.fori_loop(0, n, body, 0)


def _ada_kernel(ct_ref, w_ref, b_ref, o_ref, *, n_cond):
    ct = ct_ref[...]
    st = ct * (1.0 / (1.0 + jnp.exp(-ct)))
    w = w_ref[...]
    rid = lax.broadcasted_iota(jnp.int32, o_ref.shape, 0)
    out = jnp.broadcast_to(b_ref[...], o_ref.shape)
    for m in range(n_cond):
        row = jnp.sum(w * st[:, m:m + 1], axis=0, keepdims=True)
        out = out + jnp.where(rid == m, row, 0.0)
    o_ref[...] = out


def _ada_mod(cond8, w_ada, b_ada, n_cond):
    tn = 1536
    n = N_MOD * D_MODEL
    return pl.pallas_call(
        functools.partial(_ada_kernel, n_cond=n_cond),
        out_shape=jax.ShapeDtypeStruct((SUBLANES, n), f32),
        grid=(n // tn,),
        in_specs=[pl.BlockSpec((D_MODEL, SUBLANES), lambda j: (0, 0)),
                  pl.BlockSpec((D_MODEL, tn), lambda j: (0, j)),
                  pl.BlockSpec((1, tn), lambda j: (0, j))],
        out_specs=pl.BlockSpec((SUBLANES, tn), lambda j: (0, j)),
        compiler_params=_cparams(("arbitrary",)),
        name="ada_mod",
    )(cond8.T, w_ada, b_ada.reshape(1, n))


def _inproj_kernel(x_ref, mod_ref, g_ref, w_ref, q_ref, k_ref, v_ref, hy_ref, wbf_ref):
    @pl.when(pl.program_id(0) == 0)
    def _():
        _cast_rows(w_ref, wbf_ref, 128)

    h = _rms(x_ref[...], g_ref[...])
    h = h * (1.0 + mod_ref[0, SC1:SC1 + 1, :]) + mod_ref[0, SH1:SH1 + 1, :]
    p = _dot(h.astype(bf16), wbf_ref[...])
    q_ref[...] = p[:, 0:D_ATT].astype(q_ref.dtype)
    k_ref[...] = p[:, D_ATT:2 * D_ATT].astype(k_ref.dtype)
    v_ref[...] = p[:, 2 * D_ATT:3 * D_ATT].astype(v_ref.dtype)
    hy_ref[...] = p[:, 3 * D_ATT:]


def _inproj(x, mod, norm_g, w_in, rows_per_mod, kv_dtype):
    t = x.shape[0]
    tm = 512
    blocks_per_mod = rows_per_mod // tm
    return pl.pallas_call(
        _inproj_kernel,
        out_shape=(jax.ShapeDtypeStruct((t, D_ATT), bf16),
                   jax.ShapeDtypeStruct((t, D_ATT), kv_dtype),
                   jax.ShapeDtypeStruct((t, D_ATT), kv_dtype),
                   jax.ShapeDtypeStruct((t, 3 * D_HYENA), f32)),
        grid=(t // tm,),
        in_specs=[pl.BlockSpec((tm, D_MODEL), lambda i: (i, 0)),
                  pl.BlockSpec((1, MOD_ROWS, D_MODEL), lambda i: (i // blocks_per_mod, 0, 0)),
                  pl.BlockSpec((1, D_MODEL), lambda i: (0, 0)),
                  pl.BlockSpec((D_MODEL, D_IN), lambda i: (0, 0), pipeline_mode=pl.Buffered(1))],
        out_specs=(pl.BlockSpec((tm, D_ATT), lambda i: (i, 0)),
                   pl.BlockSpec((tm, D_ATT), lambda i: (i, 0)),
                   pl.BlockSpec((tm, D_ATT), lambda i: (i, 0)),
                   pl.BlockSpec((tm, 3 * D_HYENA), lambda i: (i, 0))),
        scratch_shapes=[pltpu.VMEM((D_MODEL, D_IN), bf16)],
        compiler_params=_cparams(("arbitrary",)),
        name="inproj",
    )(x, mod, norm_g.reshape(1, D_MODEL), w_in)


def _split_heads(q2):
    lane = lax.broadcasted_iota(jnp.int32, q2.shape, 1)
    qa = jnp.where(lane < HEAD_DIM, q2, 0.0)
    qb = jnp.where(lane >= HEAD_DIM, q2, 0.0)
    return jnp.concatenate([qa, qb], axis=0)


def _merge_heads(o_ab):
    m = o_ab.shape[0] // 2
    lane = lax.broadcasted_iota(jnp.int32, (m, LANES), 1)
    return jnp.where(lane < HEAD_DIM, o_ab[:m], o_ab[m:])


def _ctx_attn_kernel(q_ref, k_ref, v_ref, g_ref, o_ref):
    outs = []
    for p in range(D_ATT // LANES):
        cs = slice(p * LANES, (p + 1) * LANES)
        qq = _split_heads(q_ref[:, cs] * ATT_SCALE).astype(bf16)
        s = _dot_nt(qq, k_ref[:, cs].astype(bf16))
        m = jnp.max(s, axis=-1, keepdims=True)
        e = jnp.exp(s - m)
        l = jnp.sum(e, axis=-1, keepdims=True)
        o_ab = _dot(e.astype(bf16), v_ref[:, cs].astype(bf16)) / l
        outs.append(_merge_heads(o_ab))
    o_ref[...] = _rms(jnp.concatenate(outs, axis=-1), g_ref[...]).astype(o_ref.dtype)


def _ctx_attention(q, k, v, gnorm, seq):
    t = q.shape[0]
    spec = pl.BlockSpec((seq, D_ATT), lambda b: (b, 0))
    return pl.pallas_call(
        _ctx_attn_kernel,
        out_shape=jax.ShapeDtypeStruct((t, D_ATT), bf16),
        grid=(t // seq,),
        in_specs=[spec, spec, spec, pl.BlockSpec((1, D_ATT), lambda b: (0, 0))],
        out_specs=spec,
        compiler_params=_cparams(("arbitrary",)),
        name="ctx_attn",
    )(q, k, v, gnorm.reshape(1, D_ATT))


def _na_tables():
    col = np.arange(GRID_W)
    cs = np.clip(col - NA_COLS // 2, 0, GRID_W - NA_COLS)
    col_mask = (col[None, :] >= cs[:, None]) & (col[None, :] < cs[:, None] + NA_COLS)
    mask = np.tile(col_mask.astype(np.float32), (1, NA_ROWS))
    return mask


N_DR = 2 * NA_ROWS - 1
N_DC = 2 * NA_COLS - 1
BIAS_PAIRS = N_DR - 1


def _na_bias_rows(rpb):
    out = jnp.zeros((H_ATT, BIAS_PAIRS, LANES), f32)
    out = out.at[:, :, 0:N_DC].set(rpb[:, 0:BIAS_PAIRS])
    return out.at[:, :, GRID_W:GRID_W + N_DC].set(rpb[:, 1:N_DR])


NA_ROWS_PER_STEP = 4


def _na_row_start(r, rows):
    return jnp.clip(r - NA_ROWS // 2, 0, rows - NA_ROWS)


def _na_attn_kernel(q_ref, k_ref, v_ref, kc_ref, vc_ref, rp_ref, mask_ref, g_ref, o_ref,
                    t2_ref, *, rows):
    b = pl.program_id(0)
    r = pl.program_id(1)

    @pl.when((b == 0) & (r == 0))
    def _():
        for h in range(H_ATT):
            for i in range(BIAS_PAIRS):
                v = jnp.broadcast_to(rp_ref[h, i:i + 1, :], (GRID_W, LANES))
                t2_ref[h * BIAS_PAIRS + i] = pltpu.roll(v, LANES - (NA_COLS - 1), 1, stride=1, stride_axis=0)

    nwin = NA_ROWS * GRID_W
    valid = mask_ref[...] != 0.0
    valid2 = jnp.concatenate([valid, valid], axis=0)
    for rr in range(NA_ROWS_PER_STEP):
        row = r * NA_ROWS_PER_STEP + rr
        qs = slice(rr * GRID_W, (rr + 1) * GRID_W)
        rs = _na_row_start(row, rows)
        start = pl.multiple_of(rs * GRID_W, GRID_W)
        i0 = rs - row + NA_ROWS - 1
        outs = []
        for p in range(D_ATT // LANES):
            cs = slice(p * LANES, (p + 1) * LANES)
            qq = _split_heads(q_ref[qs, cs] * ATT_SCALE).astype(bf16)
            kw = k_ref[pl.ds(start, nwin), cs]
            vw = v_ref[pl.ds(start, nwin), cs]
            s_lat = _dot_nt(qq, kw)
            s_ctx = _dot_nt(qq, kc_ref[:, cs])
            bias2 = jnp.concatenate(
                [jnp.concatenate([t2_ref[(2 * p + hh) * BIAS_PAIRS + i0 + 2 * jp] for hh in range(2)], axis=0)
                 for jp in range(NA_ROWS // 2)], axis=-1)
            s_lat = jnp.where(valid2, s_lat + bias2, NEG_INF)
            m = jnp.maximum(jnp.max(s_lat, axis=-1, keepdims=True), jnp.max(s_ctx, axis=-1, keepdims=True))
            e_lat = jnp.exp(s_lat - m)
            e_ctx = jnp.exp(s_ctx - m)
            l = jnp.sum(e_lat, axis=-1, keepdims=True) + jnp.sum(e_ctx, axis=-1, keepdims=True)
            o_ab = (_dot(e_lat.astype(bf16), vw) + _dot(e_ctx.astype(bf16), vc_ref[:, cs])) / l
            outs.append(_merge_heads(o_ab))
        o_ref[qs, :] = _rms(jnp.concatenate(outs, axis=-1), g_ref[...]).astype(o_ref.dtype)


def _na_attention(q, k, v, kc, vc, rpb, gnorm, nb, seq):
    rows = seq // GRID_W
    assert rows % NA_ROWS_PER_STEP == 0
    steps = rows // NA_ROWS_PER_STEP
    qrows = NA_ROWS_PER_STEP * GRID_W
    past = kc.shape[0] // nb
    mask = _na_tables()
    return pl.pallas_call(
        functools.partial(_na_attn_kernel, rows=rows),
        out_shape=jax.ShapeDtypeStruct((nb * seq, D_ATT), bf16),
        grid=(nb, steps),
        in_specs=[pl.BlockSpec((qrows, D_ATT), lambda b, r: (b * steps + r, 0)),
                  pl.BlockSpec((seq, D_ATT), lambda b, r: (b, 0)),
                  pl.BlockSpec((seq, D_ATT), lambda b, r: (b, 0)),
                  pl.BlockSpec((past, D_ATT), lambda b, r: (b, 0)),
                  pl.BlockSpec((past, D_ATT), lambda b, r: (b, 0)),
                  pl.BlockSpec((H_ATT, BIAS_PAIRS, LANES), lambda b, r: (0, 0, 0)),
                  pl.BlockSpec((GRID_W, NA_ROWS * GRID_W), lambda b, r: (0, 0)),
                  pl.BlockSpec((1, D_ATT), lambda b, r: (0, 0))],
        out_specs=pl.BlockSpec((qrows, D_ATT), lambda b, r: (b * steps + r, 0)),
        scratch_shapes=[pltpu.VMEM((H_ATT * BIAS_PAIRS, GRID_W, LANES), f32)],
        compiler_params=_cparams(("arbitrary", "arbitrary")),
        name="na_attn",
    )(q, k, v, kc, vc, _na_bias_rows(rpb), jnp.asarray(mask), gnorm.reshape(1, D_ATT))


def _hy_front_kernel(x0_ref, x1_ref, v_ref, w0_ref, w1_ref, wv_ref, b0_ref, b1_ref, bv_ref,
                     zbf_ref, z_ref, x0c_ref):
    seq = x0_ref.shape[0]
    row = lax.broadcasted_iota(jnp.int32, x0_ref.shape, 0)
    first = row == 0
    last = row == seq - 1

    def conv(u_ref, w_ref, b_ref):
        u = u_ref[...]
        up = jnp.where(first, 0.0, pltpu.roll(u, 1, 0))
        un = jnp.where(last, 0.0, pltpu.roll(u, seq - 1, 0))
        y = b_ref[...] + up * w_ref[0:1, :]
        y = y + u * w_ref[1:2, :]
        return y + un * w_ref[2:3, :]

    z = conv(v_ref, wv_ref, bv_ref) * conv(x1_ref, w1_ref, b1_ref)
    z_ref[...] = z
    zbf_ref[...] = z.astype(bf16)
    x0c_ref[...] = conv(x0_ref, w0_ref, b0_ref)


def _hy_front(hy, conv_w, conv_b, nb, seq):
    tc = D_HYENA if seq <= 512 else D_HYENA // 2
    nc = D_HYENA // tc
    n = nb * D_HYENA
    cb = conv_b.reshape(1, 3 * D_HYENA)

    def part(k):
        return (pl.BlockSpec((seq, tc), lambda b, j: (b, k * nc + j)),
                pl.BlockSpec((SHORT_CONV, tc), lambda b, j: (0, k * nc + j)),
                pl.BlockSpec((1, tc), lambda b, j: (0, k * nc + j)))

    (x0s, w0s, b0s), (x1s, w1s, b1s), (vs, wvs, bvs) = part(0), part(1), part(2)
    ospec = pl.BlockSpec((seq, tc), lambda b, j: (0, b * nc + j))
    return pl.pallas_call(
        _hy_front_kernel,
        out_shape=(jax.ShapeDtypeStruct((seq, n), bf16),
                   jax.ShapeDtypeStruct((seq, n), f32),
                   jax.ShapeDtypeStruct((seq, n), f32)),
        grid=(nb, nc),
        in_specs=[x0s, x1s, vs, w0s, w1s, wvs, b0s, b1s, bvs],
        out_specs=(ospec, ospec, ospec),
        compiler_params=_cparams(("arbitrary", "arbitrary")),
        name="hyena_front",
    )(hy, hy, hy, conv_w, conv_w, conv_w, cb, cb, cb)


def _filter_features(seq):
    t = np.linspace(0.0, 1.0, seq, dtype=np.float64)[:, None]
    w = 2.0 * math.pi * np.arange(seq, dtype=np.float64)[:, None] / seq
    fb = np.linspace(1e-4, FILTER_BANDS - 1, FILTER_BANDS, dtype=np.float64)[None, :]
    ang = fb * w
    z = np.concatenate([t, np.cos(ang), -np.sin(ang)], axis=-1).astype(np.float32)
    return np.pad(z, ((0, 0), (0, LANES - EMB_DIM)))


def _filt_kernel(zf_ref, w1_ref, b1_ref, fr_ref, w2_ref, b2_ref, w3_ref, dl_ref, h_ref, kl_ref):
    i = pl.program_id(0)
    tr = zf_ref.shape[0]
    zf = zf_ref[...]
    fr = fr_ref[...]
    h = jnp.sin(fr * (_dot_hi(zf, w1_ref[...]) + b1_ref[...]))
    h = jnp.sin(fr * (_dot_hi(h, w2_ref[...]) + b2_ref[...]))
    h = _dot_hi(h, w3_ref[...])
    decay = jnp.exp(-zf[:, 0:1] * dl_ref[...])
    row = lax.broadcasted_iota(jnp.int32, (tr, D_HYENA), 0) + i * tr
    hf = h[:, :D_HYENA] * decay
    hb = jnp.where(row == 0, 0.0, h[:, D_HYENA:] * decay)
    h_ref[:, :D_HYENA] = (hf + hb).astype(bf16)
    h_ref[:, D_HYENA:] = (hb - hf).astype(bf16)
    alt = (1 - 2 * (row & 1)).astype(f32)
    part = jnp.sum(alt * (hf + hb), axis=0, keepdims=True)

    @pl.when(i == 0)
    def _():
        kl_ref[...] = jnp.zeros_like(kl_ref)

    kl_ref[...] += jnp.broadcast_to(part, kl_ref.shape)


def _hy_filters(seq, w1, b1, w2, b2, w3, freq):
    tr = 256
    zf = jnp.asarray(_filter_features(seq))
    deltas = np.abs(np.linspace(MIN_DECAY, MAX_DECAY, D_HYENA, dtype=np.float64))[None, :].astype(np.float32)
    w1p = jnp.pad(w1, ((0, LANES - EMB_DIM), (0, 0)))
    const = lambda shape: pl.BlockSpec(shape, lambda i: (0, 0))
    return pl.pallas_call(
        _filt_kernel,
        out_shape=(jax.ShapeDtypeStruct((seq, 2 * D_HYENA), bf16),
                   jax.ShapeDtypeStruct((SUBLANES, D_HYENA), f32)),
        grid=(seq // tr,),
        in_specs=[pl.BlockSpec((tr, LANES), lambda i: (i, 0)),
                  const((LANES, FILTER_FF)), const((1, FILTER_FF)), const((1, FILTER_FF)),
                  const((FILTER_FF, FILTER_FF)), const((1, FILTER_FF)),
                  const((FILTER_FF, 2 * D_HYENA)), const((1, D_HYENA))],
        out_specs=(pl.BlockSpec((tr, 2 * D_HYENA), lambda i: (i, 0)),
                   pl.BlockSpec((SUBLANES, D_HYENA), lambda i: (0, 0))),
        compiler_params=_cparams(("arbitrary",)),
        name="hyena_filters",
    )(zf, w1p, b1.reshape(1, -1), freq.reshape(1, -1), w2, b2.reshape(1, -1), w3, jnp.asarray(deltas))


def _dft_mats(seq):
    n = 2 * seq
    ph = (np.arange(seq, dtype=np.int64)[:, None] * np.arange(seq, dtype=np.int64)[None, :]) % n
    ang = ph.astype(np.float64) * (2.0 * math.pi / n)
    return np.cos(ang).astype(np.float32), np.sin(ang).astype(np.float32)


def _alt_col(rows, offset):
    row = lax.broadcasted_iota(jnp.int32, (rows, 1), 0) + offset
    return (1 - 2 * (row & 1)).astype(f32)


def _hy_fwd_kernel(fr_ref, fi_ref, z_ref, h_ref, kl_ref, yr_ref, yi_ref, yl_ref, kr_s, ki_s, *, n):
    i = pl.program_id(0)
    j = pl.program_id(1)
    tf = fr_ref.shape[0]
    tn = z_ref.shape[1]
    frb = fr_ref[...].astype(bf16)
    fib = fi_ref[...].astype(bf16)

    @pl.when(j == 0)
    def _():
        f = lax.broadcasted_iota(jnp.int32, (tf, 1), 0) + i * tf
        cf = jnp.where(f == 0, 1.0 / n, 2.0 / n)
        kr_s[...] = _dot(frb, h_ref[:, :D_HYENA]) * cf
        ki_s[...] = _dot(fib, h_ref[:, D_HYENA:]) * cf

    a = _dot(frb, z_ref[...])
    b = _dot(fib, z_ref[...])
    kr = kr_s[...]
    ki = ki_s[...]
    for c in range(tn // D_HYENA):
        cs = slice(c * D_HYENA, (c + 1) * D_HYENA)
        yr_ref[:, cs] = (a[:, cs] * kr + b[:, cs] * ki).astype(bf16)
        yi_ref[:, cs] = (b[:, cs] * kr - a[:, cs] * ki).astype(bf16)

    @pl.when(i == 0)
    def _():
        alt = _alt_col(z_ref.shape[0], 0)
        nz = jnp.sum(z_ref[...].astype(f32) * alt, axis=0, keepdims=True)
        kl = jnp.concatenate([kl_ref[0:1, :]] * (tn // D_HYENA), axis=-1)
        yl_ref[...] = jnp.broadcast_to(nz * kl * (1.0 / n), yl_ref.shape)


def _hy_fwd(fr, fi, zbf, hcat, kl, seq):
    n_cols = zbf.shape[1]
    tf = 256
    tn = min(n_cols, 2048)
    ni, nj = seq // tf, n_cols // tn
    assert ni == 1 or nj == 1
    return pl.pallas_call(
        functools.partial(_hy_fwd_kernel, n=2 * seq),
        out_shape=(jax.ShapeDtypeStruct((seq, n_cols), bf16),
                   jax.ShapeDtypeStruct((seq, n_cols), bf16),
                   jax.ShapeDtypeStruct((SUBLANES, n_cols), f32)),
        grid=(ni, nj),
        in_specs=[pl.BlockSpec((tf, seq), lambda i, j: (i, 0)),
                  pl.BlockSpec((tf, seq), lambda i, j: (i, 0)),
                  pl.BlockSpec((seq, tn), lambda i, j: (0, j)),
                  pl.BlockSpec((seq, 2 * D_HYENA), lambda i, j: (0, 0)),
                  pl.BlockSpec((SUBLANES, D_HYENA), lambda i, j: (0, 0))],
        out_specs=(pl.BlockSpec((tf, tn), lambda i, j: (i, j)),
                   pl.BlockSpec((tf, tn), lambda i, j: (i, j)),
                   pl.BlockSpec((SUBLANES, tn), lambda i, j: (0, j))),
        scratch_shapes=[pltpu.VMEM((tf, D_HYENA), f32), pltpu.VMEM((tf, D_HYENA), f32)],
        compiler_params=_cparams(("arbitrary", "arbitrary")),
        name="hyena_dft_fwd",
    )(fr, fi, zbf, hcat, kl)


def _hy_inv_kernel(fr_ref, fi_ref, yr_ref, yi_ref, yl_ref, z_ref, x0_ref, skip_ref, g_ref, o_ref):
    tt = fr_ref.shape[0]
    tn = yr_ref.shape[1]
    y = _dot(fr_ref[...].astype(bf16), yr_ref[...]) + _dot(fi_ref[...].astype(bf16), yi_ref[...])
    alt = _alt_col(tt, pl.program_id(0) * tt)
    for c in range(tn // D_HYENA):
        cs = slice(c * D_HYENA, (c + 1) * D_HYENA)
        yc = y[:, cs] + alt * yl_ref[0:1, cs] + z_ref[:, cs] * skip_ref[...]
        o_ref[:, cs] = _rms(yc * x0_ref[:, cs], g_ref[...]).astype(o_ref.dtype)


def _hy_inv(fr, fi, yr, yi, yl, z, x0c, skip, gnorm, seq):
    n_cols = z.shape[1]
    tt = 256
    tn = min(n_cols, 2048)
    blk = pl.BlockSpec((tt, tn), lambda i, j: (i, j))
    return pl.pallas_call(
        _hy_inv_kernel,
        out_shape=jax.ShapeDtypeStruct((seq, n_cols), bf16),
        grid=(seq // tt, n_cols // tn),
        in_specs=[pl.BlockSpec((tt, seq), lambda i, j: (i, 0)),
                  pl.BlockSpec((tt, seq), lambda i, j: (i, 0)),
                  pl.BlockSpec((seq, tn), lambda i, j: (0, j)),
                  pl.BlockSpec((seq, tn), lambda i, j: (0, j)),
                  pl.BlockSpec((SUBLANES, tn), lambda i, j: (0, j)),
                  blk, blk,
                  pl.BlockSpec((1, D_HYENA), lambda i, j: (0, 0)),
                  pl.BlockSpec((1, D_HYENA), lambda i, j: (0, 0))],
        out_specs=blk,
        compiler_params=_cparams(("arbitrary", "arbitrary")),
        name="hyena_dft_inv",
    )(fr, fi, yr, yi, yl, z, x0c, skip.reshape(1, D_HYENA), gnorm.reshape(1, D_HYENA))


def _hyena(hy, lp, nb, seq):
    fr_np, fi_np = _dft_mats(seq)
    fr, fi = jnp.asarray(fr_np), jnp.asarray(fi_np)
    zbf, z, x0c = _hy_front(hy, lp['conv_w'], lp['conv_b'], nb, seq)
    hcat, kl = _hy_filters(seq, lp['filt_w1'], lp['filt_b1'], lp['filt_w2'], lp['filt_b2'],
                           lp['filt_w3'], lp['filt_freq'])
    yr, yi, yl = _hy_fwd(fr, fi, zbf, hcat, kl, seq)
    return _hy_inv(fr, fi, yr, yi, yl, z, x0c, lp['hyena_skip'], lp['gnorm_hyena'], seq)


def _store_token_tiles(ref, x, pitch=SUBLANES):
    m = x.shape[0]
    for c in range(D_MODEL // LANES):
        ref[pl.ds(c, m, stride=pitch), :] = x[:, c * LANES:(c + 1) * LANES]


def _load_token_tiles(ref, m, lead=(), pitch=SUBLANES):
    return jnp.concatenate([ref[lead + (pl.ds(c, m, stride=pitch), slice(None))]
                            for c in range(D_MODEL // LANES)], axis=-1)


def _route(logits):
    lane_i = lax.broadcasted_iota(jnp.int32, logits.shape, 1)
    lane = lane_i.astype(f32)
    big = float(ROUTE_LANES)
    is_g = lane_i < N_GROUPS
    mg = jnp.max(jnp.where(is_g, logits, -jnp.inf), axis=-1, keepdims=True)
    sg = jnp.sum(jnp.where(is_g, jnp.exp(logits - mg), 0.0), axis=-1, keepdims=True)
    g_w = 1.0 / sg
    g_idx = jnp.min(jnp.where(is_g & (logits == mg), lane, big), axis=-1, keepdims=True)
    e_id = lane_i - ROUTE_EXP_LANE0
    sel = (e_id >= 0) & (e_id < N_EXPERTS) & ((e_id >> 2).astype(f32) == g_idx)
    me = jnp.max(jnp.where(sel, logits, -jnp.inf), axis=-1, keepdims=True)
    ee = jnp.where(sel, jnp.exp(logits - me), 0.0)
    prob = ee / jnp.sum(ee, axis=-1, keepdims=True)
    p1 = jnp.max(jnp.where(sel, prob, -1.0), axis=-1, keepdims=True)
    i1 = jnp.min(jnp.where(sel & (prob == p1), lane, big), axis=-1, keepdims=True)
    sel2 = sel & (lane != i1)
    p2 = jnp.max(jnp.where(sel2, prob, -1.0), axis=-1, keepdims=True)
    i2 = jnp.min(jnp.where(sel2 & (prob == p2), lane, big), axis=-1, keepdims=True)
    tot = p1 + p2
    rec = jnp.where(lane_i == ROUTE_E1, i1 - ROUTE_EXP_LANE0, 0.0)
    rec = jnp.where(lane_i == ROUTE_E2, i2 - ROUTE_EXP_LANE0, rec)
    rec = jnp.where(lane_i == ROUTE_W1, g_w * (p1 / tot), rec)
    return jnp.where(lane_i == ROUTE_W2, g_w * (p2 / tot), rec)


def _outproj_kernel(xp_ref, xs_ref, attp_ref, atts_ref, hyp_ref, hys_ref, mod_ref, wo_ref, g2_ref,
                    wr_ref, br_ref, x1_ref, tok_ref, rt_ref, wobf_ref, wrh_ref, wrl_ref, *, prompt_tiles):
    i = pl.program_id(0)

    @pl.when(i == 0)
    def _():
        _cast_rows(wo_ref, wobf_ref, 128)
        wr = wr_ref[...]
        hi = wr.astype(bf16)
        wrh_ref[...] = hi
        wrl_ref[...] = (wr - hi.astype(f32)).astype(bf16)

    is_p = i < prompt_tiles
    hyp = jnp.concatenate([hyp_ref[:, b * D_HYENA:(b + 1) * D_HYENA]
                           for b in range(hyp_ref.shape[1] // D_HYENA)], axis=0)
    x = jnp.where(is_p, xp_ref[...], xs_ref[...])
    att = jnp.where(is_p, attp_ref[...], atts_ref[...])
    hyo = jnp.where(is_p, hyp, hys_ref[...])
    proj = _dot(att, wobf_ref[0:D_ATT, :]) + _dot(hyo, wobf_ref[D_ATT:, :])
    x1 = x + mod_ref[0, GT1:GT1 + 1, :] * proj
    x1_ref[...] = x1
    h2 = _rms(x1, g2_ref[...]) * (1.0 + mod_ref[0, SC2:SC2 + 1, :]) + mod_ref[0, SH2:SH2 + 1, :]
    h2h = h2.astype(bf16)
    h2l = (h2 - h2h.astype(f32)).astype(bf16)
    logits = _dot(h2h, wrh_ref[...]) + _dot(h2l, wrh_ref[...]) + _dot(h2h, wrl_ref[...]) + br_ref[...]
    rt = _route(logits)
    rt_ref[...] = rt
    tok_ref[...] = jnp.zeros_like(tok_ref)
    _store_token_tiles(tok_ref, h2, TOK_PITCH)
    tok_ref[pl.ds(TOK_RT_ROW, h2.shape[0], stride=TOK_PITCH), :] = rt


def _outproj(xp, xs, attp, atts, hyp, hys, mod, w_out, norm2_g, wr, br, seq_p, seq_s):
    tm = 512
    tp, ts = xp.shape[0], xs.shape[0]
    npt, nst = tp // tm, ts // tm
    assert tm % seq_p == 0 and seq_s % tm == 0
    spb = seq_s // tm
    bpt = tm // seq_p
    p_idx = lambda i: jnp.minimum(i, npt - 1)
    s_idx = lambda i: jnp.maximum(i - npt, 0)
    const = lambda shape: pl.BlockSpec(shape, lambda i: (0,) * len(shape))
    return pl.pallas_call(
        functools.partial(_outproj_kernel, prompt_tiles=npt),
        out_shape=(jax.ShapeDtypeStruct((tp + ts, D_MODEL), f32),
                   jax.ShapeDtypeStruct(((tp + ts) * TOK_PITCH, LANES), f32),
                   jax.ShapeDtypeStruct((tp + ts, ROUTE_LANES), f32)),
        grid=(npt + nst,),
        in_specs=[pl.BlockSpec((tm, D_MODEL), lambda i: (p_idx(i), 0)),
                  pl.BlockSpec((tm, D_MODEL), lambda i: (s_idx(i), 0)),
                  pl.BlockSpec((tm, D_ATT), lambda i: (p_idx(i), 0)),
                  pl.BlockSpec((tm, D_ATT), lambda i: (s_idx(i), 0)),
                  pl.BlockSpec((seq_p, bpt * D_HYENA), lambda i: (0, p_idx(i))),
                  pl.BlockSpec((tm, D_HYENA), lambda i: (s_idx(i) % spb, s_idx(i) // spb)),
                  pl.BlockSpec((1, MOD_ROWS, D_MODEL),
                               lambda i: (jnp.where(i < npt, 0, 1 + s_idx(i) // spb), 0, 0)),
                  const((D_MODEL, D_MODEL)), const((1, D_MODEL)),
                  const((D_MODEL, ROUTE_LANES)), const((1, ROUTE_LANES))],
        out_specs=(pl.BlockSpec((tm, D_MODEL), lambda i: (i, 0)),
                   pl.BlockSpec((tm * TOK_PITCH, LANES), lambda i: (i, 0)),
                   pl.BlockSpec((tm, ROUTE_LANES), lambda i: (i, 0))),
        scratch_shapes=[pltpu.VMEM((D_MODEL, D_MODEL), bf16),
                        pltpu.VMEM((D_MODEL, ROUTE_LANES), bf16), pltpu.VMEM((D_MODEL, ROUTE_LANES), bf16)],
        compiler_params=_cparams(("arbitrary",)),
        name="outproj_router",
    )(xp, xs, attp, atts, hyp, hys, mod, w_out, norm2_g.reshape(1, D_MODEL), wr, br)


PAIRS_PER_GROUP = 6
N_CLASSES = N_GROUPS * PAIRS_PER_GROUP
PAIR_SLOT_A = (0, 0, 0, 1, 1, 3)
PAIR_SLOT_B = (1, 2, 3, 3, 2, 2)
FLAG_NEW_A, FLAG_NEW_B, FLAG_TILE_START, FLAG_TILE_END = 1, 2, 4, 8


CLASS_ROWS = 32
TAB_TILE, TAB_EA, TAB_EB, TAB_LO, TAB_HI, TAB_FLAGS, TAB_N = range(7)


def _select_by(idx, values):
    out = jnp.full(idx.shape, float(values[-1]), f32)
    for i in range(len(values) - 2, -1, -1):
        out = jnp.where(idx == i, float(values[i]), out)
    return out


def _plan_kernel(e1_ref, e2_ref, pos_ref, tab_ref, *, tm):
    e1 = e1_ref[...]
    e2 = e2_ref[...]
    rows = e1.shape[0]
    grp = jnp.floor(e1 * (1.0 / EXPERTS_PER_GROUP))
    l1 = e1 - EXPERTS_PER_GROUP * grp
    l2 = e2 - EXPERTS_PER_GROUP * jnp.floor(e2 * (1.0 / EXPERTS_PER_GROUP))
    lo, hi = jnp.minimum(l1, l2), jnp.maximum(l1, l2)
    pair = jnp.where(lo == 0, hi - 1, jnp.where(lo == 1, jnp.where(hi == 3, 3.0, 4.0), 5.0))
    cls = grp * PAIRS_PER_GROUP + pair

    ri = lax.broadcasted_iota(jnp.int32, (LANES, LANES), 0)
    ci = lax.broadcasted_iota(jnp.int32, (LANES, LANES), 1)
    upper = (ri <= ci).astype(bf16)
    rr = lax.broadcasted_iota(jnp.int32, (rows, rows), 0)
    rc = lax.broadcasted_iota(jnp.int32, (rows, rows), 1)
    strict_lower = (rc < rr).astype(bf16)
    cid = lax.broadcasted_iota(jnp.int32, (CLASS_ROWS, 1), 0)

    pos = jnp.zeros(e1.shape, f32)
    base = jnp.zeros((1, 1), f32)
    cnt_col = jnp.zeros((CLASS_ROWS, 1), f32)
    off_col = jnp.zeros((CLASS_ROWS, 1), f32)
    for c in range(N_CLASSES):
        m = cls == c
        within = _dot(m.astype(bf16), upper)
        tot = jnp.broadcast_to(within[:, LANES - 1:LANES], within.shape)
        before = _dot(strict_lower, tot.astype(bf16))[:, 0:1]
        count = jnp.sum(within[:, LANES - 1:LANES], axis=0, keepdims=True)
        pos = jnp.where(m, base + before + within - 1.0, pos)
        cnt_col = jnp.where(cid == c, count, cnt_col)
        off_col = jnp.where(cid == c, base, off_col)
        base = base + count
    pos_ref[...] = pos.astype(jnp.int32)

    end_col = off_col + cnt_col
    inv_tm = 1.0 / tm
    first_col = jnp.floor(off_col * inv_tm)
    ntl_col = jnp.where(cnt_col > 0, jnp.floor((end_col - 1.0) * inv_tm) - first_col + 1.0, 0.0)
    cr = lax.broadcasted_iota(jnp.int32, (CLASS_ROWS, CLASS_ROWS), 0)
    cc = lax.broadcasted_iota(jnp.int32, (CLASS_ROWS, CLASS_ROWS), 1)
    lower = (cc <= cr).astype(bf16)
    iend_col = _dot(lower, jnp.broadcast_to(ntl_col, (CLASS_ROWS, LANES)).astype(bf16))[:, 0:1]
    istart_col = iend_col - ntl_col
    n_items = jnp.sum(ntl_col, axis=0, keepdims=True)
    slots = lax.broadcasted_iota(jnp.int32, (1, LANES), 1).astype(f32)
    items = jnp.minimum(slots, n_items - 1.0)
    past = jnp.where((items >= iend_col) & (cid < N_CLASSES), 1.0, 0.0)
    it_cls = jnp.minimum(jnp.sum(past, axis=0, keepdims=True), N_CLASSES - 1.0)
    sel = it_cls == cid.astype(f32)
    pick = lambda col: jnp.sum(jnp.where(sel, col, 0.0), axis=0, keepdims=True)
    grp_col = jnp.floor(cid.astype(f32) * (1.0 / PAIRS_PER_GROUP))
    pair_col = cid.astype(f32) - PAIRS_PER_GROUP * grp_col
    ea_col = EXPERTS_PER_GROUP * grp_col + _select_by(pair_col, PAIR_SLOT_A)
    eb_col = EXPERTS_PER_GROUP * grp_col + _select_by(pair_col, PAIR_SLOT_B)
    it_tile = pick(first_col) + items - pick(istart_col)
    it_lo = jnp.maximum(pick(off_col) - it_tile * tm, 0.0)
    it_hi = jnp.minimum(pick(end_col) - it_tile * tm, float(tm))
    it_ea, it_eb = pick(ea_col), pick(eb_col)
    prev = lambda v: pltpu.roll(jnp.broadcast_to(v, (SUBLANES, LANES)), 1, 1)[0:1]
    changed = lambda v: jnp.where((slots == 0) | (v != prev(v)), 1.0, 0.0)
    flags = (FLAG_NEW_A * changed(it_ea) + FLAG_NEW_B * changed(it_eb)
             + FLAG_TILE_START * jnp.where(it_lo == 0, 1.0, 0.0) + FLAG_TILE_END * jnp.where(it_hi == tm, 1.0, 0.0))
    table = {TAB_TILE: it_tile, TAB_EA: it_ea, TAB_EB: it_eb, TAB_LO: it_lo, TAB_HI: it_hi, TAB_FLAGS: flags,
             TAB_N: jnp.broadcast_to(n_items, (1, LANES))}
    trow = lax.broadcasted_iota(jnp.int32, tab_ref.shape, 0)
    tab = jnp.zeros(tab_ref.shape, f32)
    for r, v in table.items():
        tab = jnp.where(trow == r, v, tab)
    tab_ref[...] = tab.astype(jnp.int32)


def _route_plan(e1, e2, tm, max_items):
    t = e1.shape[0]
    assert t % LANES == 0 and max_items <= LANES and tm & (tm - 1) == 0
    pos, tab = pl.pallas_call(
        functools.partial(_plan_kernel, tm=tm),
        out_shape=(jax.ShapeDtypeStruct((t // LANES, LANES), jnp.int32),
                   jax.ShapeDtypeStruct((SUBLANES, LANES), jnp.int32)),
        name="moe_plan",
    )(e1.reshape(t // LANES, LANES), e2.reshape(t // LANES, LANES))
    row = lambda r: tab[r, :max_items]
    return dict(pos=pos.reshape(t), n_items=tab[TAB_N, :1], it_tile=row(TAB_TILE), it_ea=row(TAB_EA),
                it_eb=row(TAB_EB), it_lo=row(TAB_LO), it_hi=row(TAB_HI), it_flags=row(TAB_FLAGS))


ROW_DMA_UNROLL = 8


def _row_dma_loop(g0, g1, make_copy, priorities):
    def body(i, c):
        for u in range(ROW_DMA_UNROLL):
            make_copy(i * ROW_DMA_UNROLL + u).start(priority=priorities[u % len(priorities)])
        return c

    lax.fori_loop(g0, g1, body, 0)


def _moe_ffn_kernel(pos_ref, it_tile, it_ea, it_eb, it_lo, it_hi, it_flags, n_items, tok_hbm,
                    wga_ref, wua_ref, wda_ref, wgb_ref, wub_ref, wdb_ref, out_hbm,
                    src_s, xbuf, ybuf, x_s, rt_s, acc_s, wa_s, wb_s, sem_in, sem_out, *, chunk, n_tiles):
    i = pl.program_id(0)
    tm = x_s.shape[0]
    groups = tm // ROW_DMA_UNROLL

    def gather_tile(k):
        base = k * tm
        slot = k % 2
        _row_dma_loop(0, groups, lambda r: pltpu.make_async_copy(
            tok_hbm.at[src_s[base + r]],
            xbuf.at[slot, pl.ds(pl.multiple_of(r * TOK_PITCH, TOK_PITCH), TOK_PITCH), :],
            sem_in.at[slot]), GATHER_PRIORITIES)

    def scatter_tile(k):
        base = k * tm
        slot = k % 2
        _row_dma_loop(0, groups, lambda r: pltpu.make_async_copy(
            ybuf.at[slot, pl.ds(pl.multiple_of(r * SUBLANES, SUBLANES), SUBLANES), :],
            out_hbm.at[src_s[base + r]],
            sem_out.at[slot]), SCATTER_PRIORITIES)

    def expert_ffn(x, w, wg, wu, wd):
        g = _dot(x, wg[...])
        u = _dot(x, wu[...])
        hid = (g * (1.0 / (1.0 + jnp.exp(-g)))) * u
        return _dot((hid * w).astype(bf16), wd[...])

    def wait_all(buf, sem, slot):
        pltpu.make_async_copy(buf.at[slot], buf.at[slot], sem.at[slot]).wait()

    @pl.when(i == 0)
    def _():
        def inv(a, c):
            for u in range(ROW_DMA_UNROLL):
                src_s[pos_ref[a * ROW_DMA_UNROLL + u]] = a * ROW_DMA_UNROLL + u
            return c

        lax.fori_loop(0, pos_ref.shape[0] // ROW_DMA_UNROLL, inv, 0)
        gather_tile(0)

    @pl.when(i < n_items[0])
    def _():
        flags = it_flags[i]
        k = it_tile[i]

        @pl.when((flags & FLAG_NEW_A) != 0)
        def _():
            for dst, src in zip(wa_s, (wga_ref, wua_ref, wda_ref)):
                dst[...] = src[0].astype(bf16)

        @pl.when((flags & FLAG_NEW_B) != 0)
        def _():
            for dst, src in zip(wb_s, (wgb_ref, wub_ref, wdb_ref)):
                dst[...] = src[0].astype(bf16)

        @pl.when((flags & FLAG_TILE_START) != 0)
        def _():
            @pl.when(k + 1 < n_tiles)
            def _():
                gather_tile(k + 1)

            wait_all(xbuf, sem_in, k % 2)
            x_s[...] = _load_token_tiles(xbuf, tm, (k % 2,), TOK_PITCH).astype(bf16)
            rt_s[...] = xbuf[k % 2, pl.ds(TOK_RT_ROW, tm, stride=TOK_PITCH), :]
            acc_s[...] = jnp.zeros_like(acc_s)

        lo = it_lo[i]
        hi = it_hi[i]
        ea = it_ea[i].astype(f32)

        def body(j, c):
            r = pl.multiple_of(j * chunk, chunk)

            @pl.when((r < hi) & (r + chunk > lo))
            def _():
                x = x_s[pl.ds(r, chunk), :]
                rt = rt_s[pl.ds(r, chunk), :]
                first_is_a = rt[:, ROUTE_E1:ROUTE_E1 + 1] == ea
                w1 = rt[:, ROUTE_W1:ROUTE_W1 + 1]
                w2 = rt[:, ROUTE_W2:ROUTE_W2 + 1]
                y = (expert_ffn(x, jnp.where(first_is_a, w1, w2), *wa_s)
                     + expert_ffn(x, jnp.where(first_is_a, w2, w1), *wb_s))
                row = lax.broadcasted_iota(jnp.int32, (chunk, 1), 0) + r
                mine = (row >= lo) & (row < hi)
                acc_s[pl.ds(r, chunk), :] = jnp.where(mine, y, acc_s[pl.ds(r, chunk), :])

            return c

        lax.fori_loop(0, tm // chunk, body, 0)

        @pl.when((flags & FLAG_TILE_END) != 0)
        def _():
            @pl.when(k >= 2)
            def _():
                wait_all(ybuf, sem_out, k % 2)

            _store_token_tiles(ybuf.at[k % 2], acc_s[...])
            scatter_tile(k)

            @pl.when(k == n_tiles - 1)
            def _():
                if n_tiles > 1:
                    wait_all(ybuf, sem_out, (n_tiles - 2) % 2)
                wait_all(ybuf, sem_out, (n_tiles - 1) % 2)


def _moe_ffn(plan, tok, w_gate, w_up, w_down, tm, max_items):
    n_tok = tok.shape[0] // TOK_PITCH
    n_tiles = n_tok // tm
    tok3 = tok.reshape(n_tok, TOK_PITCH, LANES)
    spec_a = lambda shape: pl.BlockSpec((1,) + shape, lambda i, ps, tl, ea, eb, lo, hi, fl, n: (ea[i], 0, 0))
    spec_b = lambda shape: pl.BlockSpec((1,) + shape, lambda i, ps, tl, ea, eb, lo, hi, fl, n: (eb[i], 0, 0))
    shapes = ((D_MODEL, D_EXPERT), (D_MODEL, D_EXPERT), (D_EXPERT, D_MODEL))
    wscratch = lambda: tuple(pltpu.VMEM(s, bf16) for s in shapes)
    return pl.pallas_call(
        functools.partial(_moe_ffn_kernel, chunk=256, n_tiles=n_tiles),
        out_shape=jax.ShapeDtypeStruct((n_tok, SUBLANES, LANES), f32),
        grid_spec=pltpu.PrefetchScalarGridSpec(
            num_scalar_prefetch=8,
            grid=(max_items,),
            in_specs=[pl.BlockSpec(memory_space=pl.ANY)] + [spec_a(s) for s in shapes] + [spec_b(s) for s in shapes],
            out_specs=pl.BlockSpec(memory_space=pl.ANY),
            scratch_shapes=[pltpu.SMEM((n_tok,), jnp.int32),
                            pltpu.VMEM((2, tm * TOK_PITCH, LANES), f32),
                            pltpu.VMEM((2, tm * SUBLANES, LANES), f32),
                            pltpu.VMEM((tm, D_MODEL), bf16), pltpu.VMEM((tm, ROUTE_LANES), f32),
                            pltpu.VMEM((tm, D_MODEL), f32), wscratch(), wscratch(),
                            pltpu.SemaphoreType.DMA((2,)), pltpu.SemaphoreType.DMA((2,))]),
        compiler_params=_cparams(("arbitrary",), 60 * 1024 * 1024),
        name="moe_ffn",
    )(plan['pos'], plan['it_tile'], plan['it_ea'], plan['it_eb'], plan['it_lo'], plan['it_hi'],
      plan['it_flags'], plan['n_items'], tok3, w_gate, w_up, w_down, w_gate, w_up, w_down)


def _final_kernel(moe_ref, x1_ref, mod_ref, fg_ref, yp_ref, yl_ref, *, prompt_tiles):
    i = pl.program_id(0)
    tm = x1_ref.shape[0]
    y = _rms(x1_ref[...] + mod_ref[0, GT2:GT2 + 1, :] * _load_token_tiles(moe_ref, tm), fg_ref[...])

    @pl.when(i < prompt_tiles)
    def _():
        yp_ref[...] = y

    @pl.when(i >= prompt_tiles)
    def _():
        yl_ref[...] = y


def _final(moe, x1, mod, final_g, t_prompt, t_lat, seq_s):
    tm = 256
    npt, nst = t_prompt // tm, t_lat // tm
    spb = seq_s // tm
    moe2 = moe.reshape(moe.shape[0] * SUBLANES, LANES)
    return pl.pallas_call(
        functools.partial(_final_kernel, prompt_tiles=npt),
        out_shape=(jax.ShapeDtypeStruct((t_prompt, D_MODEL), f32),
                   jax.ShapeDtypeStruct((t_lat, D_MODEL), f32)),
        grid=(npt + nst,),
        in_specs=[pl.BlockSpec((tm * SUBLANES, LANES), lambda i: (i, 0)),
                  pl.BlockSpec((tm, D_MODEL), lambda i: (i, 0)),
                  pl.BlockSpec((1, MOD_ROWS, D_MODEL),
                               lambda i: (jnp.where(i < npt, 0, 1 + jnp.maximum(i - npt, 0) // spb), 0, 0)),
                  pl.BlockSpec((1, D_MODEL), lambda i: (0, 0))],
        out_specs=(pl.BlockSpec((tm, D_MODEL), lambda i: (jnp.minimum(i, npt - 1), 0)),
                   pl.BlockSpec((tm, D_MODEL), lambda i: (jnp.maximum(i - npt, 0), 0))),
        compiler_params=_cparams(("arbitrary",)),
        name="moe_combine_final",
    )(moe2, x1, mod, final_g.reshape(1, D_MODEL))


def _moe(tok, rt, x1, mod, w_gate, w_up, w_down, final_g, t_prompt, t_lat, seq_s):
    tm = MOE_TM
    n_rows = t_prompt + t_lat
    assert n_rows % tm == 0
    max_items = n_rows // tm + N_CLASSES
    plan = _route_plan(rt[:, ROUTE_E1], rt[:, ROUTE_E2], tm, max_items)
    moe = _moe_ffn(plan, tok, w_gate, w_up, w_down, tm, max_items)
    return _final(moe, x1, mod, final_g, t_prompt, t_lat, seq_s)


def kernel(x_prompt, x_sample, cache_k, cache_v, c, c_ctx, w_ada, b_ada, norm1_g, w_in, rpb, conv_w, conv_b, filt_w1, filt_b1, filt_w2, filt_b2, filt_w3, filt_freq, hyena_skip, gnorm_att, gnorm_hyena, w_out, norm2_g, router_grp_w, router_grp_b, router_exp_w, router_exp_b, w_gate, w_up, w_down, final_g):
    depth = w_ada.shape[0]
    assert depth == 1
    batch, seq, _ = x_prompt.shape
    dec_batch, dec_seq, _ = x_sample.shape
    l = 0

    def pack_router(grp, exp):
        rows = grp.shape[0]
        return jnp.concatenate([grp, jnp.zeros((rows, ROUTE_EXP_LANE0 - N_GROUPS), f32), exp,
                                jnp.zeros((rows, ROUTE_LANES - ROUTE_EXP_LANE0 - N_EXPERTS), f32)], axis=1)

    wr = pack_router(router_grp_w[l], router_exp_w[l])
    br = pack_router(router_grp_b[l][None, :], router_exp_b[l][None, :])

    lp = {
        'norm1_g': norm1_g[l], 'w_in': w_in[l], 'conv_w': conv_w[l], 'conv_b': conv_b[l],
        'filt_w1': filt_w1[l], 'filt_b1': filt_b1[l], 'filt_w2': filt_w2[l], 'filt_b2': filt_b2[l],
        'filt_w3': filt_w3[l], 'filt_freq': filt_freq[l], 'hyena_skip': hyena_skip[l],
        'gnorm_hyena': gnorm_hyena[l], 'w_out': w_out[l], 'norm2_g': norm2_g[l],
        'wr': wr, 'br': br, 'w_gate': w_gate[l], 'w_up': w_up[l], 'w_down': w_down[l],
    }

    cond8 = jnp.concatenate([c_ctx[None, :], c, jnp.zeros((SUBLANES - 1 - dec_batch, D_MODEL), f32)], axis=0)
    mod = _ada_mod(cond8, w_ada[l], b_ada[l], 1 + dec_batch).reshape(SUBLANES, N_MOD, D_MODEL)
    mod = jnp.pad(mod, ((0, 0), (0, MOD_ROWS - N_MOD), (0, 0)))
    mod_ctx, mod_lat = mod[0:1], mod[1:1 + dec_batch]

    xp = x_prompt.reshape(batch * seq, D_MODEL)
    xs = x_sample.reshape(dec_batch * dec_seq, D_MODEL)

    qp, k_ctx, v_ctx, hyp = _inproj(xp, mod_ctx, lp['norm1_g'], lp['w_in'], batch * seq, f32)
    attp = _ctx_attention(qp, k_ctx, v_ctx, gnorm_att[l], seq)
    hyop = _hyena(hyp, lp, batch, seq)

    ql, kl, vl, hyl = _inproj(xs, mod_lat, lp['norm1_g'], lp['w_in'], dec_seq, bf16)
    kc = cache_k[:, l].reshape(dec_batch * cache_k.shape[2], D_ATT).astype(bf16)
    vc = cache_v[:, l].reshape(dec_batch * cache_v.shape[2], D_ATT).astype(bf16)
    attl = _na_attention(ql, kl, vl, kc, vc, rpb[l], gnorm_att[l], dec_batch, dec_seq)
    hyol = _hyena(hyl, lp, dec_batch, dec_seq)

    x1, tok, rt = _outproj(xp, xs, attp, attl, hyop, hyol, mod[0:1 + dec_batch], lp['w_out'], lp['norm2_g'],
                           lp['wr'], lp['br'], seq, dec_seq)
    yp, ys = _moe(tok, rt, x1, mod[0:1 + dec_batch], lp['w_gate'], lp['w_up'], lp['w_down'], final_g,
                  batch * seq, dec_batch * dec_seq, dec_seq)

    y_prompt = yp.reshape(batch, seq, D_MODEL)
    y_sample = ys.reshape(dec_batch, dec_seq, D_MODEL)
    new_k = k_ctx.reshape(batch, 1, seq, H_ATT, HEAD_DIM)
    new_v = v_ctx.reshape(batch, 1, seq, H_ATT, HEAD_DIM)
    return (y_prompt, y_sample, new_k, new_v)
```

```python
import functools
import math

import jax
import jax.numpy as jnp
import numpy as np
from jax import lax
from jax.experimental import pallas as pl
from jax.experimental.pallas import tpu as pltpu

f32 = jnp.float32
bf16 = jnp.bfloat16
HIGHEST = lax.Precision.HIGHEST

D_MODEL = 1024
GRID_W = 64
H_ATT = 8
HEAD_DIM = 64
D_ATT = H_ATT * HEAD_DIM
D_HYENA = 512
D_IN = 3 * D_ATT + 3 * D_HYENA
NA_ROWS = 8
NA_COLS = 16
SHORT_CONV = 3
FILTER_BANDS = 16
EMB_DIM = 1 + 2 * FILTER_BANDS
FILTER_FF = 64
DECAY_TARGET = 1e-2
MIN_DECAY = math.log(DECAY_TARGET) / 1.5
MAX_DECAY = math.log(DECAY_TARGET) / 0.3
N_GROUPS = 4
EXPERTS_PER_GROUP = 4
N_EXPERTS = N_GROUPS * EXPERTS_PER_GROUP
D_EXPERT = 512
N_MOD = 6
EPS = 1e-6
NEG_INF = -1e30
ATT_SCALE = HEAD_DIM ** -0.5

LANES = 128
SUBLANES = 8
MOD_ROWS = 8
ROUTE_LANES = 128
ROUTE_EXP_LANE0 = 16
ROUTE_E1, ROUTE_E2, ROUTE_W1, ROUTE_W2 = 0, 1, 2, 3
TOK_PITCH = 16
TOK_RT_ROW = 8
MOE_TM = 512
GATHER_PRIORITIES = (0,)
SCATTER_PRIORITIES = (1,)
VMEM_LIMIT = 56 * 1024 * 1024

SH1, SC1, GT1, SH2, SC2, GT2 = range(6)


def _cparams(sem, vmem=VMEM_LIMIT):
    return pltpu.CompilerParams(dimension_semantics=sem, vmem_limit_bytes=vmem)


def _dot(a, b):
    return jnp.dot(a, b, preferred_element_type=f32)


def _dot_hi(a, b):
    return lax.dot_general(a, b, (((1,), (0,)), ((), ())), precision=HIGHEST,
                           preferred_element_type=f32)


def _dot_nt(a, b):
    return lax.dot_general(a, b, (((1,), (1,)), ((), ())), preferred_element_type=f32)


def _rms(x, g):
    ms = jnp.mean(x * x, axis=-1, keepdims=True)
    return x * lax.rsqrt(ms + EPS) * g


def _cast_rows(src_ref, dst_ref, chunk):
    n = src_ref.shape[0] // chunk

    def body(i, c):
        r = pl.multiple_of(i * chunk, chunk)
        dst_ref[pl.ds(r, chunk), :] = src_ref[pl.ds(r, chunk), :].astype(dst_ref.dtype)
        return c

    lax.fori_loop(0, n, body, 0)


def _ada_kernel(ct_ref, w_ref, b_ref, o_ref, *, n_cond):
    ct = ct_ref[...]
    st = ct * (1.0 / (1.0 + jnp.exp(-ct)))
    w = w_ref[...]
    rid = lax.broadcasted_iota(jnp.int32, o_ref.shape, 0)
    out = jnp.broadcast_to(b_ref[...], o_ref.shape)
    for m in range(n_cond):
        row = jnp.sum(w * st[:, m:m + 1], axis=0, keepdims=True)
        out = out + jnp.where(rid == m, row, 0.0)
    o_ref[...] = out


def _ada_mod(cond8, w_ada, b_ada, n_cond):
    tn = 1536
    n = N_MOD * D_MODEL
    return pl.pallas_call(
        functools.partial(_ada_kernel, n_cond=n_cond),
        out_shape=jax.ShapeDtypeStruct((SUBLANES, n), f32),
        grid=(n // tn,),
        in_specs=[pl.BlockSpec((D_MODEL, SUBLANES), lambda j: (0, 0)),
                  pl.BlockSpec((D_MODEL, tn), lambda j: (0, j)),
                  pl.BlockSpec((1, tn), lambda j: (0, j))],
        out_specs=pl.BlockSpec((SUBLANES, tn), lambda j: (0, j)),
        compiler_params=_cparams(("arbitrary",)),
        name="ada_mod",
    )(cond8.T, w_ada, b_ada.reshape(1, n))


def _inproj_kernel(x_ref, mod_ref, g_ref, w_ref, q_ref, k_ref, v_ref, hy_ref, wbf_ref):
    @pl.when(pl.program_id(0) == 0)
    def _():
        _cast_rows(w_ref, wbf_ref, 128)

    h = _rms(x_ref[...], g_ref[...])
    h = h * (1.0 + mod_ref[0, SC1:SC1 + 1, :]) + mod_ref[0, SH1:SH1 + 1, :]
    p = _dot(h.astype(bf16), wbf_ref[...])
    q_ref[...] = p[:, 0:D_ATT].astype(q_ref.dtype)
    k_ref[...] = p[:, D_ATT:2 * D_ATT].astype(k_ref.dtype)
    v_ref[...] = p[:, 2 * D_ATT:3 * D_ATT].astype(v_ref.dtype)
    hy_ref[...] = p[:, 3 * D_ATT:]


def _inproj(x, mod, norm_g, w_in, rows_per_mod, kv_dtype):
    t = x.shape[0]
    tm = 512
    blocks_per_mod = rows_per_mod // tm
    return pl.pallas_call(
        _inproj_kernel,
        out_shape=(jax.ShapeDtypeStruct((t, D_ATT), bf16),
                   jax.ShapeDtypeStruct((t, D_ATT), kv_dtype),
                   jax.ShapeDtypeStruct((t, D_ATT), kv_dtype),
                   jax.ShapeDtypeStruct((t, 3 * D_HYENA), f32)),
        grid=(t // tm,),
        in_specs=[pl.BlockSpec((tm, D_MODEL), lambda i: (i, 0)),
                  pl.BlockSpec((1, MOD_ROWS, D_MODEL), lambda i: (i // blocks_per_mod, 0, 0)),
                  pl.BlockSpec((1, D_MODEL), lambda i: (0, 0)),
                  pl.BlockSpec((D_MODEL, D_IN), lambda i: (0, 0), pipeline_mode=pl.Buffered(1))],
        out_specs=(pl.BlockSpec((tm, D_ATT), lambda i: (i, 0)),
                   pl.BlockSpec((tm, D_ATT), lambda i: (i, 0)),
                   pl.BlockSpec((tm, D_ATT), lambda i: (i, 0)),
                   pl.BlockSpec((tm, 3 * D_HYENA), lambda i: (i, 0))),
        scratch_shapes=[pltpu.VMEM((D_MODEL, D_IN), bf16)],
        compiler_params=_cparams(("arbitrary",)),
        name="inproj",
    )(x, mod, norm_g.reshape(1, D_MODEL), w_in)


def _split_heads(q2):
    lane = lax.broadcasted_iota(jnp.int32, q2.shape, 1)
    qa = jnp.where(lane < HEAD_DIM, q2, 0.0)
    qb = jnp.where(lane >= HEAD_DIM, q2, 0.0)
    return jnp.concatenate([qa, qb], axis=0)


def _merge_heads(o_ab):
    m = o_ab.shape[0] // 2
    lane = lax.broadcasted_iota(jnp.int32, (m, LANES), 1)
    return jnp.where(lane < HEAD_DIM, o_ab[:m], o_ab[m:])


CTX_SEQS_PER_STEP = 2


def _ctx_attn_kernel(q_ref, k_ref, v_ref, g_ref, o_ref, *, seq):
    for b in range(q_ref.shape[0] // seq):
        rows = slice(b * seq, (b + 1) * seq)
        outs = []
        for p in range(D_ATT // LANES):
            cs = slice(p * LANES, (p + 1) * LANES)
            qq = _split_heads(q_ref[rows, cs] * ATT_SCALE).astype(bf16)
            s = _dot_nt(qq, k_ref[rows, cs].astype(bf16))
            m = jnp.max(s, axis=-1, keepdims=True)
            e = jnp.exp(s - m)
            l = jnp.sum(e, axis=-1, keepdims=True)
            o_ab = _dot(e.astype(bf16), v_ref[rows, cs].astype(bf16)) / l
            outs.append(_merge_heads(o_ab))
        o_ref[rows, :] = _rms(jnp.concatenate(outs, axis=-1), g_ref[...]).astype(o_ref.dtype)


def _ctx_attention(q, k, v, gnorm, seq):
    t = q.shape[0]
    tm = CTX_SEQS_PER_STEP * seq
    assert t % tm == 0
    spec = pl.BlockSpec((tm, D_ATT), lambda b: (b, 0))
    return pl.pallas_call(
        functools.partial(_ctx_attn_kernel, seq=seq),
        out_shape=jax.ShapeDtypeStruct((t, D_ATT), bf16),
        grid=(t // tm,),
        in_specs=[spec, spec, spec, pl.BlockSpec((1, D_ATT), lambda b: (0, 0))],
        out_specs=spec,
        compiler_params=_cparams(("arbitrary",)),
        name="ctx_attn",
    )(q, k, v, gnorm.reshape(1, D_ATT))


def _na_tables():
    col = np.arange(GRID_W)
    cs = np.clip(col - NA_COLS // 2, 0, GRID_W - NA_COLS)
    col_mask = (col[None, :] >= cs[:, None]) & (col[None, :] < cs[:, None] + NA_COLS)
    mask = np.tile(col_mask.astype(np.float32), (1, NA_ROWS))
    return mask


N_DR = 2 * NA_ROWS - 1
N_DC = 2 * NA_COLS - 1
BIAS_PAIRS = N_DR - 1


def _na_bias_rows(rpb):
    out = jnp.zeros((H_ATT, BIAS_PAIRS, LANES), f32)
    out = out.at[:, :, 0:N_DC].set(rpb[:, 0:BIAS_PAIRS])
    return out.at[:, :, GRID_W:GRID_W + N_DC].set(rpb[:, 1:N_DR])


NA_ROWS_PER_STEP = 4


def _na_row_start(r, rows):
    return jnp.clip(r - NA_ROWS // 2, 0, rows - NA_ROWS)


def _na_attn_kernel(q_ref, k_ref, v_ref, kc_ref, vc_ref, rp_ref, mask_ref, g_ref, o_ref,
                    t2_ref, *, rows):
    b = pl.program_id(0)
    r = pl.program_id(1)

    @pl.when((b == 0) & (r == 0))
    def _():
        for h in range(H_ATT):
            for i in range(BIAS_PAIRS):
                v = jnp.broadcast_to(rp_ref[h, i:i + 1, :], (GRID_W, LANES))
                t2_ref[h * BIAS_PAIRS + i] = pltpu.roll(v, LANES - (NA_COLS - 1), 1, stride=1, stride_axis=0)

    nwin = NA_ROWS * GRID_W
    valid = mask_ref[...] != 0.0
    valid2 = jnp.concatenate([valid, valid], axis=0)
    for rr in range(NA_ROWS_PER_STEP):
        row = r * NA_ROWS_PER_STEP + rr
        qs = slice(rr * GRID_W, (rr + 1) * GRID_W)
        rs = _na_row_start(row, rows)
        start = pl.multiple_of(rs * GRID_W, GRID_W)
        i0 = rs - row + NA_ROWS - 1
        outs = []
        for p in range(D_ATT // LANES):
            cs = slice(p * LANES, (p + 1) * LANES)
            qq = _split_heads(q_ref[qs, cs] * ATT_SCALE).astype(bf16)
            kw = k_ref[pl.ds(start, nwin), cs]
            vw = v_ref[pl.ds(start, nwin), cs]
            s_lat = _dot_nt(qq, kw)
            s_ctx = _dot_nt(qq, kc_ref[:, cs])
            bias2 = jnp.concatenate(
                [jnp.concatenate([t2_ref[(2 * p + hh) * BIAS_PAIRS + i0 + 2 * jp] for hh in range(2)], axis=0)
                 for jp in range(NA_ROWS // 2)], axis=-1)
            s_lat = jnp.where(valid2, s_lat + bias2, NEG_INF)
            m = jnp.maximum(jnp.max(s_lat, axis=-1, keepdims=True), jnp.max(s_ctx, axis=-1, keepdims=True))
            e_lat = jnp.exp(s_lat - m)
            e_ctx = jnp.exp(s_ctx - m)
            l = jnp.sum(e_lat, axis=-1, keepdims=True) + jnp.sum(e_ctx, axis=-1, keepdims=True)
            o_ab = (_dot(e_lat.astype(bf16), vw) + _dot(e_ctx.astype(bf16), vc_ref[:, cs])) / l
            outs.append(_merge_heads(o_ab))
        o_ref[qs, :] = _rms(jnp.concatenate(outs, axis=-1), g_ref[...]).astype(o_ref.dtype)


def _na_attention(q, k, v, kc, vc, rpb, gnorm, nb, seq):
    rows = seq // GRID_W
    assert rows % NA_ROWS_PER_STEP == 0
    steps = rows // NA_ROWS_PER_STEP
    qrows = NA_ROWS_PER_STEP * GRID_W
    past = kc.shape[0] // nb
    mask = _na_tables()
    return pl.pallas_call(
        functools.partial(_na_attn_kernel, rows=rows),
        out_shape=jax.ShapeDtypeStruct((nb * seq, D_ATT), bf16),
        grid=(nb, steps),
        in_specs=[pl.BlockSpec((qrows, D_ATT), lambda b, r: (b * steps + r, 0)),
                  pl.BlockSpec((seq, D_ATT), lambda b, r: (b, 0)),
                  pl.BlockSpec((seq, D_ATT), lambda b, r: (b, 0)),
                  pl.BlockSpec((past, D_ATT), lambda b, r: (b, 0)),
                  pl.BlockSpec((past, D_ATT), lambda b, r: (b, 0)),
                  pl.BlockSpec((H_ATT, BIAS_PAIRS, LANES), lambda b, r: (0, 0, 0)),
                  pl.BlockSpec((GRID_W, NA_ROWS * GRID_W), lambda b, r: (0, 0)),
                  pl.BlockSpec((1, D_ATT), lambda b, r: (0, 0))],
        out_specs=pl.BlockSpec((qrows, D_ATT), lambda b, r: (b * steps + r, 0)),
        scratch_shapes=[pltpu.VMEM((H_ATT * BIAS_PAIRS, GRID_W, LANES), f32)],
        compiler_params=_cparams(("arbitrary", "arbitrary")),
        name="na_attn",
    )(q, k, v, kc, vc, _na_bias_rows(rpb), jnp.asarray(mask), gnorm.reshape(1, D_ATT))


def _hy_front_kernel(x0_ref, x1_ref, v_ref, w0_ref, w1_ref, wv_ref, b0_ref, b1_ref, bv_ref,
                     zbf_ref, z_ref, x0c_ref):
    seq = x0_ref.shape[0]
    row = lax.broadcasted_iota(jnp.int32, x0_ref.shape, 0)
    first = row == 0
    last = row == seq - 1

    def conv(u_ref, w_ref, b_ref):
        u = u_ref[...]
        up = jnp.where(first, 0.0, pltpu.roll(u, 1, 0))
        un = jnp.where(last, 0.0, pltpu.roll(u, seq - 1, 0))
        y = b_ref[...] + up * w_ref[0:1, :]
        y = y + u * w_ref[1:2, :]
        return y + un * w_ref[2:3, :]

    z = conv(v_ref, wv_ref, bv_ref) * conv(x1_ref, w1_ref, b1_ref)
    z_ref[...] = z
    zbf_ref[...] = z.astype(bf16)
    x0c_ref[...] = conv(x0_ref, w0_ref, b0_ref)


def _hy_front(hy, conv_w, conv_b, nb, seq):
    tc = D_HYENA if seq <= 512 else D_HYENA // 2
    nc = D_HYENA // tc
    n = nb * D_HYENA
    cb = conv_b.reshape(1, 3 * D_HYENA)

    def part(k):
        return (pl.BlockSpec((seq, tc), lambda b, j: (b, k * nc + j)),
                pl.BlockSpec((SHORT_CONV, tc), lambda b, j: (0, k * nc + j)),
                pl.BlockSpec((1, tc), lambda b, j: (0, k * nc + j)))

    (x0s, w0s, b0s), (x1s, w1s, b1s), (vs, wvs, bvs) = part(0), part(1), part(2)
    ospec = pl.BlockSpec((seq, tc), lambda b, j: (0, b * nc + j))
    return pl.pallas_call(
        _hy_front_kernel,
        out_shape=(jax.ShapeDtypeStruct((seq, n), bf16),
                   jax.ShapeDtypeStruct((seq, n), f32),
                   jax.ShapeDtypeStruct((seq, n), f32)),
        grid=(nb, nc),
        in_specs=[x0s, x1s, vs, w0s, w1s, wvs, b0s, b1s, bvs],
        out_specs=(ospec, ospec, ospec),
        compiler_params=_cparams(("arbitrary", "arbitrary")),
        name="hyena_front",
    )(hy, hy, hy, conv_w, conv_w, conv_w, cb, cb, cb)


def _filter_features(seq):
    t = np.linspace(0.0, 1.0, seq, dtype=np.float64)[:, None]
    w = 2.0 * math.pi * np.arange(seq, dtype=np.float64)[:, None] / seq
    fb = np.linspace(1e-4, FILTER_BANDS - 1, FILTER_BANDS, dtype=np.float64)[None, :]
    ang = fb * w
    z = np.concatenate([t, np.cos(ang), -np.sin(ang)], axis=-1).astype(np.float32)
    return np.pad(z, ((0, 0), (0, LANES - EMB_DIM)))


def _filt_kernel(zt_ref, w1t_ref, b1_ref, fr_ref, w2t_ref, b2_ref, w3_ref, dl_ref, h_ref, kl_ref, *, seq):
    i = pl.program_id(0)
    tr = zt_ref.shape[1]
    fr = fr_ref[...]
    h = jnp.sin(fr * (_dot_hi(w1t_ref[...], zt_ref[...]) + b1_ref[...]))
    h = jnp.sin(fr * (_dot_hi(w2t_ref[...], h) + b2_ref[...]))
    h = _dot_hi(h.T, w3_ref[...])
    row = lax.broadcasted_iota(jnp.int32, (tr, D_HYENA), 0) + i * tr
    t = row[:, 0:1].astype(f32) * (1.0 / (seq - 1))
    decay = jnp.exp(-t * dl_ref[...])
    hf = h[:, :D_HYENA] * decay
    hb = jnp.where(row == 0, 0.0, h[:, D_HYENA:] * decay)
    h_ref[:, :D_HYENA] = (hf + hb).astype(bf16)
    h_ref[:, D_HYENA:] = (hb - hf).astype(bf16)
    alt = (1 - 2 * (row & 1)).astype(f32)
    part = jnp.sum(alt * (hf + hb), axis=0, keepdims=True)

    @pl.when(i == 0)
    def _():
        kl_ref[...] = jnp.zeros_like(kl_ref)

    kl_ref[...] += jnp.broadcast_to(part, kl_ref.shape)


def _hy_filters(seq, w1, b1, w2, b2, w3, freq):
    tr = 256
    zt = jnp.asarray(np.ascontiguousarray(_filter_features(seq).T))
    deltas = np.abs(np.linspace(MIN_DECAY, MAX_DECAY, D_HYENA, dtype=np.float64))[None, :].astype(np.float32)
    w1t = jnp.pad(w1, ((0, LANES - EMB_DIM), (0, 0))).T
    const = lambda shape: pl.BlockSpec(shape, lambda i: (0, 0))
    col = lambda v: v.reshape(FILTER_FF, 1)
    return pl.pallas_call(
        functools.partial(_filt_kernel, seq=seq),
        out_shape=(jax.ShapeDtypeStruct((seq, 2 * D_HYENA), bf16),
                   jax.ShapeDtypeStruct((SUBLANES, D_HYENA), f32)),
        grid=(seq // tr,),
        in_specs=[pl.BlockSpec((LANES, tr), lambda i: (0, i)),
                  const((FILTER_FF, LANES)), const((FILTER_FF, 1)), const((FILTER_FF, 1)),
                  const((FILTER_FF, FILTER_FF)), const((FILTER_FF, 1)),
                  const((FILTER_FF, 2 * D_HYENA)), const((1, D_HYENA))],
        out_specs=(pl.BlockSpec((tr, 2 * D_HYENA), lambda i: (i, 0)),
                   pl.BlockSpec((SUBLANES, D_HYENA), lambda i: (0, 0))),
        compiler_params=_cparams(("arbitrary",)),
        name="hyena_filters",
    )(zt, w1t, col(b1), col(freq), w2.T, col(b2), w3, jnp.asarray(deltas))


def _dft_mats(seq):
    n = 2 * seq
    ph = (np.arange(seq, dtype=np.int64)[:, None] * np.arange(seq, dtype=np.int64)[None, :]) % n
    ang = ph.astype(np.float64) * (2.0 * math.pi / n)
    return np.cos(ang).astype(np.float32), np.sin(ang).astype(np.float32)


def _alt_col(rows, offset):
    row = lax.broadcasted_iota(jnp.int32, (rows, 1), 0) + offset
    return (1 - 2 * (row & 1)).astype(f32)


def _hy_fwd_kernel(fr_ref, fi_ref, z_ref, h_ref, kl_ref, yr_ref, yi_ref, yl_ref, kr_s, ki_s, *, n):
    i = pl.program_id(0)
    j = pl.program_id(1)
    tf = fr_ref.shape[0]
    tn = z_ref.shape[1]
    frb = fr_ref[...].astype(bf16)
    fib = fi_ref[...].astype(bf16)

    @pl.when(j == 0)
    def _():
        f = lax.broadcasted_iota(jnp.int32, (tf, 1), 0) + i * tf
        cf = jnp.where(f == 0, 1.0 / n, 2.0 / n)
        kr_s[...] = _dot(frb, h_ref[:, :D_HYENA]) * cf
        ki_s[...] = _dot(fib, h_ref[:, D_HYENA:]) * cf

    a = _dot(frb, z_ref[...])
    b = _dot(fib, z_ref[...])
    kr = kr_s[...]
    ki = ki_s[...]
    for c in range(tn // D_HYENA):
        cs = slice(c * D_HYENA, (c + 1) * D_HYENA)
        yr_ref[:, cs] = (a[:, cs] * kr + b[:, cs] * ki).astype(bf16)
        yi_ref[:, cs] = (b[:, cs] * kr - a[:, cs] * ki).astype(bf16)

    @pl.when(i == 0)
    def _():
        alt = _alt_col(z_ref.shape[0], 0)
        nz = jnp.sum(z_ref[...].astype(f32) * alt, axis=0, keepdims=True)
        kl = jnp.concatenate([kl_ref[0:1, :]] * (tn // D_HYENA), axis=-1)
        yl_ref[...] = jnp.broadcast_to(nz * kl * (1.0 / n), yl_ref.shape)


def _hy_fwd(fr, fi, zbf, hcat, kl, seq):
    n_cols = zbf.shape[1]
    tf = 256
    tn = min(n_cols, 2048)
    ni, nj = seq // tf, n_cols // tn
    assert ni == 1 or nj == 1
    return pl.pallas_call(
        functools.partial(_hy_fwd_kernel, n=2 * seq),
        out_shape=(jax.ShapeDtypeStruct((seq, n_cols), bf16),
                   jax.ShapeDtypeStruct((seq, n_cols), bf16),
                   jax.ShapeDtypeStruct((SUBLANES, n_cols), f32)),
        grid=(ni, nj),
        in_specs=[pl.BlockSpec((tf, seq), lambda i, j: (i, 0)),
                  pl.BlockSpec((tf, seq), lambda i, j: (i, 0)),
                  pl.BlockSpec((seq, tn), lambda i, j: (0, j)),
                  pl.BlockSpec((seq, 2 * D_HYENA), lambda i, j: (0, 0)),
                  pl.BlockSpec((SUBLANES, D_HYENA), lambda i, j: (0, 0))],
        out_specs=(pl.BlockSpec((tf, tn), lambda i, j: (i, j)),
                   pl.BlockSpec((tf, tn), lambda i, j: (i, j)),
                   pl.BlockSpec((SUBLANES, tn), lambda i, j: (0, j))),
        scratch_shapes=[pltpu.VMEM((tf, D_HYENA), f32), pltpu.VMEM((tf, D_HYENA), f32)],
        compiler_params=_cparams(("arbitrary", "arbitrary")),
        name="hyena_dft_fwd",
    )(fr, fi, zbf, hcat, kl)


def _hy_inv_kernel(fr_ref, fi_ref, yr_ref, yi_ref, yl_ref, z_ref, x0_ref, skip_ref, g_ref, o_ref):
    tt = fr_ref.shape[0]
    tn = yr_ref.shape[1]
    y = _dot(fr_ref[...].astype(bf16), yr_ref[...]) + _dot(fi_ref[...].astype(bf16), yi_ref[...])
    alt = _alt_col(tt, pl.program_id(0) * tt)
    for c in range(tn // D_HYENA):
        cs = slice(c * D_HYENA, (c + 1) * D_HYENA)
        yc = y[:, cs] + alt * yl_ref[0:1, cs] + z_ref[:, cs] * skip_ref[...]
        o_ref[:, cs] = _rms(yc * x0_ref[:, cs], g_ref[...]).astype(o_ref.dtype)


def _hy_inv(fr, fi, yr, yi, yl, z, x0c, skip, gnorm, seq):
    n_cols = z.shape[1]
    tt = 256
    tn = min(n_cols, 2048)
    blk = pl.BlockSpec((tt, tn), lambda i, j: (i, j))
    return pl.pallas_call(
        _hy_inv_kernel,
        out_shape=jax.ShapeDtypeStruct((seq, n_cols), bf16),
        grid=(seq // tt, n_cols // tn),
        in_specs=[pl.BlockSpec((tt, seq), lambda i, j: (i, 0)),
                  pl.BlockSpec((tt, seq), lambda i, j: (i, 0)),
                  pl.BlockSpec((seq, tn), lambda i, j: (0, j)),
                  pl.BlockSpec((seq, tn), lambda i, j: (0, j)),
                  pl.BlockSpec((SUBLANES, tn), lambda i, j: (0, j)),
                  blk, blk,
                  pl.BlockSpec((1, D_HYENA), lambda i, j: (0, 0)),
                  pl.BlockSpec((1, D_HYENA), lambda i, j: (0, 0))],
        out_specs=blk,
        compiler_params=_cparams(("arbitrary", "arbitrary")),
        name="hyena_dft_inv",
    )(fr, fi, yr, yi, yl, z, x0c, skip.reshape(1, D_HYENA), gnorm.reshape(1, D_HYENA))


def _hyena(hy, lp, nb, seq):
    fr_np, fi_np = _dft_mats(seq)
    fr, fi = jnp.asarray(fr_np), jnp.asarray(fi_np)
    zbf, z, x0c = _hy_front(hy, lp['conv_w'], lp['conv_b'], nb, seq)
    hcat, kl = _hy_filters(seq, lp['filt_w1'], lp['filt_b1'], lp['filt_w2'], lp['filt_b2'],
                           lp['filt_w3'], lp['filt_freq'])
    yr, yi, yl = _hy_fwd(fr, fi, zbf, hcat, kl, seq)
    return _hy_inv(fr, fi, yr, yi, yl, z, x0c, lp['hyena_skip'], lp['gnorm_hyena'], seq)


def _store_token_tiles(ref, x, pitch=SUBLANES):
    m = x.shape[0]
    for c in range(D_MODEL // LANES):
        ref[pl.ds(c, m, stride=pitch), :] = x[:, c * LANES:(c + 1) * LANES]


def _load_token_tiles(ref, m, lead=(), pitch=SUBLANES):
    return jnp.concatenate([ref[lead + (pl.ds(c, m, stride=pitch), slice(None))]
                            for c in range(D_MODEL // LANES)], axis=-1)


def _route(logits):
    lane_i = lax.broadcasted_iota(jnp.int32, logits.shape, 1)
    lane = lane_i.astype(f32)
    big = float(ROUTE_LANES)
    is_g = lane_i < N_GROUPS
    mg = jnp.max(jnp.where(is_g, logits, -jnp.inf), axis=-1, keepdims=True)
    sg = jnp.sum(jnp.where(is_g, jnp.exp(logits - mg), 0.0), axis=-1, keepdims=True)
    g_w = 1.0 / sg
    g_idx = jnp.min(jnp.where(is_g & (logits == mg), lane, big), axis=-1, keepdims=True)
    e_id = lane_i - ROUTE_EXP_LANE0
    sel = (e_id >= 0) & (e_id < N_EXPERTS) & ((e_id >> 2).astype(f32) == g_idx)
    me = jnp.max(jnp.where(sel, logits, -jnp.inf), axis=-1, keepdims=True)
    ee = jnp.where(sel, jnp.exp(logits - me), 0.0)
    prob = ee / jnp.sum(ee, axis=-1, keepdims=True)
    p1 = jnp.max(jnp.where(sel, prob, -1.0), axis=-1, keepdims=True)
    i1 = jnp.min(jnp.where(sel & (prob == p1), lane, big), axis=-1, keepdims=True)
    sel2 = sel & (lane != i1)
    p2 = jnp.max(jnp.where(sel2, prob, -1.0), axis=-1, keepdims=True)
    i2 = jnp.min(jnp.where(sel2 & (prob == p2), lane, big), axis=-1, keepdims=True)
    tot = p1 + p2
    rec = jnp.where(lane_i == ROUTE_E1, i1 - ROUTE_EXP_LANE0, 0.0)
    rec = jnp.where(lane_i == ROUTE_E2, i2 - ROUTE_EXP_LANE0, rec)
    rec = jnp.where(lane_i == ROUTE_W1, g_w * (p1 / tot), rec)
    return jnp.where(lane_i == ROUTE_W2, g_w * (p2 / tot), rec)


def _outproj_kernel(xp_ref, xs_ref, attp_ref, atts_ref, hyp_ref, hys_ref, mod_ref, wo_ref, g2_ref,
                    wr_ref, br_ref, x1_ref, tok_ref, rt_ref, wobf_ref, wrh_ref, wrl_ref, *, prompt_tiles):
    i = pl.program_id(0)

    @pl.when(i == 0)
    def _():
        _cast_rows(wo_ref, wobf_ref, 128)
        wr = wr_ref[...]
        hi = wr.astype(bf16)
        wrh_ref[...] = hi
        wrl_ref[...] = (wr - hi.astype(f32)).astype(bf16)

    is_p = i < prompt_tiles
    hyp = jnp.concatenate([hyp_ref[:, b * D_HYENA:(b + 1) * D_HYENA]
                           for b in range(hyp_ref.shape[1] // D_HYENA)], axis=0)
    x = jnp.where(is_p, xp_ref[...], xs_ref[...])
    att = jnp.where(is_p, attp_ref[...], atts_ref[...])
    hyo = jnp.where(is_p, hyp, hys_ref[...])
    proj = _dot(att, wobf_ref[0:D_ATT, :]) + _dot(hyo, wobf_ref[D_ATT:, :])
    x1 = x + mod_ref[0, GT1:GT1 + 1, :] * proj
    x1_ref[...] = x1
    h2 = _rms(x1, g2_ref[...]) * (1.0 + mod_ref[0, SC2:SC2 + 1, :]) + mod_ref[0, SH2:SH2 + 1, :]
    h2h = h2.astype(bf16)
    h2l = (h2 - h2h.astype(f32)).astype(bf16)
    logits = _dot(h2h, wrh_ref[...]) + _dot(h2l, wrh_ref[...]) + _dot(h2h, wrl_ref[...]) + br_ref[...]
    rt = _route(logits)
    rt_ref[...] = rt
    tok_ref[...] = jnp.zeros_like(tok_ref)
    _store_token_tiles(tok_ref, h2, TOK_PITCH)
    tok_ref[pl.ds(TOK_RT_ROW, h2.shape[0], stride=TOK_PITCH), :] = rt


def _outproj(xp, xs, attp, atts, hyp, hys, mod, w_out, norm2_g, wr, br, seq_p, seq_s):
    tm = 512
    tp, ts = xp.shape[0], xs.shape[0]
    npt, nst = tp // tm, ts // tm
    assert tm % seq_p == 0 and seq_s % tm == 0
    spb = seq_s // tm
    bpt = tm // seq_p
    p_idx = lambda i: jnp.minimum(i, npt - 1)
    s_idx = lambda i: jnp.maximum(i - npt, 0)
    const = lambda shape: pl.BlockSpec(shape, lambda i: (0,) * len(shape))
    return pl.pallas_call(
        functools.partial(_outproj_kernel, prompt_tiles=npt),
        out_shape=(jax.ShapeDtypeStruct((tp + ts, D_MODEL), f32),
                   jax.ShapeDtypeStruct(((tp + ts) * TOK_PITCH, LANES), f32),
                   jax.ShapeDtypeStruct((tp + ts, ROUTE_LANES), f32)),
        grid=(npt + nst,),
        in_specs=[pl.BlockSpec((tm, D_MODEL), lambda i: (p_idx(i), 0)),
                  pl.BlockSpec((tm, D_MODEL), lambda i: (s_idx(i), 0)),
                  pl.BlockSpec((tm, D_ATT), lambda i: (p_idx(i), 0)),
                  pl.BlockSpec((tm, D_ATT), lambda i: (s_idx(i), 0)),
                  pl.BlockSpec((seq_p, bpt * D_HYENA), lambda i: (0, p_idx(i))),
                  pl.BlockSpec((tm, D_HYENA), lambda i: (s_idx(i) % spb, s_idx(i) // spb)),
                  pl.BlockSpec((1, MOD_ROWS, D_MODEL),
                               lambda i: (jnp.where(i < npt, 0, 1 + s_idx(i) // spb), 0, 0)),
                  const((D_MODEL, D_MODEL)), const((1, D_MODEL)),
                  const((D_MODEL, ROUTE_LANES)), const((1, ROUTE_LANES))],
        out_specs=(pl.BlockSpec((tm, D_MODEL), lambda i: (i, 0)),
                   pl.BlockSpec((tm * TOK_PITCH, LANES), lambda i: (i, 0)),
                   pl.BlockSpec((tm, ROUTE_LANES), lambda i: (i, 0))),
        scratch_shapes=[pltpu.VMEM((D_MODEL, D_MODEL), bf16),
                        pltpu.VMEM((D_MODEL, ROUTE_LANES), bf16), pltpu.VMEM((D_MODEL, ROUTE_LANES), bf16)],
        compiler_params=_cparams(("arbitrary",)),
        name="outproj_router",
    )(xp, xs, attp, atts, hyp, hys, mod, w_out, norm2_g.reshape(1, D_MODEL), wr, br)


PAIRS_PER_GROUP = 6
N_CLASSES = N_GROUPS * PAIRS_PER_GROUP
PAIR_SLOT_A = (0, 0, 0, 1, 1, 3)
PAIR_SLOT_B = (1, 2, 3, 3, 2, 2)
FLAG_NEW_A, FLAG_NEW_B, FLAG_TILE_START, FLAG_TILE_END = 1, 2, 4, 8


CLASS_ROWS = 32
TAB_TILE, TAB_EA, TAB_EB, TAB_LO, TAB_HI, TAB_FLAGS, TAB_N = range(7)


def _select_by(idx, values):
    out = jnp.full(idx.shape, float(values[-1]), f32)
    for i in range(len(values) - 2, -1, -1):
        out = jnp.where(idx == i, float(values[i]), out)
    return out


def _plan_kernel(e1_ref, e2_ref, pos_ref, tab_ref, *, tm):
    e1 = e1_ref[...]
    e2 = e2_ref[...]
    rows = e1.shape[0]
    grp = jnp.floor(e1 * (1.0 / EXPERTS_PER_GROUP))
    l1 = e1 - EXPERTS_PER_GROUP * grp
    l2 = e2 - EXPERTS_PER_GROUP * jnp.floor(e2 * (1.0 / EXPERTS_PER_GROUP))
    lo, hi = jnp.minimum(l1, l2), jnp.maximum(l1, l2)
    pair = jnp.where(lo == 0, hi - 1, jnp.where(lo == 1, jnp.where(hi == 3, 3.0, 4.0), 5.0))
    cls = grp * PAIRS_PER_GROUP + pair

    ri = lax.broadcasted_iota(jnp.int32, (LANES, LANES), 0)
    ci = lax.broadcasted_iota(jnp.int32, (LANES, LANES), 1)
    upper = (ri <= ci).astype(bf16)
    rr = lax.broadcasted_iota(jnp.int32, (rows, rows), 0)
    rc = lax.broadcasted_iota(jnp.int32, (rows, rows), 1)
    strict_lower = (rc < rr).astype(bf16)
    cid = lax.broadcasted_iota(jnp.int32, (CLASS_ROWS, 1), 0)

    pos = jnp.zeros(e1.shape, f32)
    base = jnp.zeros((1, 1), f32)
    cnt_col = jnp.zeros((CLASS_ROWS, 1), f32)
    off_col = jnp.zeros((CLASS_ROWS, 1), f32)
    for c in range(N_CLASSES):
        m = cls == c
        within = _dot(m.astype(bf16), upper)
        tot = jnp.broadcast_to(within[:, LANES - 1:LANES], within.shape)
        before = _dot(strict_lower, tot.astype(bf16))[:, 0:1]
        count = jnp.sum(within[:, LANES - 1:LANES], axis=0, keepdims=True)
        pos = jnp.where(m, base + before + within - 1.0, pos)
        cnt_col = jnp.where(cid == c, count, cnt_col)
        off_col = jnp.where(cid == c, base, off_col)
        base = base + count
    pos_ref[...] = pos.astype(jnp.int32)

    end_col = off_col + cnt_col
    inv_tm = 1.0 / tm
    first_col = jnp.floor(off_col * inv_tm)
    ntl_col = jnp.where(cnt_col > 0, jnp.floor((end_col - 1.0) * inv_tm) - first_col + 1.0, 0.0)
    cr = lax.broadcasted_iota(jnp.int32, (CLASS_ROWS, CLASS_ROWS), 0)
    cc = lax.broadcasted_iota(jnp.int32, (CLASS_ROWS, CLASS_ROWS), 1)
    lower = (cc <= cr).astype(bf16)
    iend_col = _dot(lower, jnp.broadcast_to(ntl_col, (CLASS_ROWS, LANES)).astype(bf16))[:, 0:1]
    istart_col = iend_col - ntl_col
    n_items = jnp.sum(ntl_col, axis=0, keepdims=True)
    slots = lax.broadcasted_iota(jnp.int32, (1, LANES), 1).astype(f32)
    items = jnp.minimum(slots, n_items - 1.0)
    past = jnp.where((items >= iend_col) & (cid < N_CLASSES), 1.0, 0.0)
    it_cls = jnp.minimum(jnp.sum(past, axis=0, keepdims=True), N_CLASSES - 1.0)
    sel = it_cls == cid.astype(f32)
    pick = lambda col: jnp.sum(jnp.where(sel, col, 0.0), axis=0, keepdims=True)
    grp_col = jnp.floor(cid.astype(f32) * (1.0 / PAIRS_PER_GROUP))
    pair_col = cid.astype(f32) - PAIRS_PER_GROUP * grp_col
    ea_col = EXPERTS_PER_GROUP * grp_col + _select_by(pair_col, PAIR_SLOT_A)
    eb_col = EXPERTS_PER_GROUP * grp_col + _select_by(pair_col, PAIR_SLOT_B)
    it_tile = pick(first_col) + items - pick(istart_col)
    it_lo = jnp.maximum(pick(off_col) - it_tile * tm, 0.0)
    it_hi = jnp.minimum(pick(end_col) - it_tile * tm, float(tm))
    it_ea, it_eb = pick(ea_col), pick(eb_col)
    prev = lambda v: pltpu.roll(jnp.broadcast_to(v, (SUBLANES, LANES)), 1, 1)[0:1]
    changed = lambda v: jnp.where((slots == 0) | (v != prev(v)), 1.0, 0.0)
    flags = (FLAG_NEW_A * changed(it_ea) + FLAG_NEW_B * changed(it_eb)
             + FLAG_TILE_START * jnp.where(it_lo == 0, 1.0, 0.0) + FLAG_TILE_END * jnp.where(it_hi == tm, 1.0, 0.0))
    table = {TAB_TILE: it_tile, TAB_EA: it_ea, TAB_EB: it_eb, TAB_LO: it_lo, TAB_HI: it_hi, TAB_FLAGS: flags,
             TAB_N: jnp.broadcast_to(n_items, (1, LANES))}
    trow = lax.broadcasted_iota(jnp.int32, tab_ref.shape, 0)
    tab = jnp.zeros(tab_ref.shape, f32)
    for r, v in table.items():
        tab = jnp.where(trow == r, v, tab)
    tab_ref[...] = tab.astype(jnp.int32)


def _route_plan(e1, e2, tm, max_items):
    t = e1.shape[0]
    assert t % LANES == 0 and max_items <= LANES and tm & (tm - 1) == 0
    pos, tab = pl.pallas_call(
        functools.partial(_plan_kernel, tm=tm),
        out_shape=(jax.ShapeDtypeStruct((t // LANES, LANES), jnp.int32),
                   jax.ShapeDtypeStruct((SUBLANES, LANES), jnp.int32)),
        name="moe_plan",
    )(e1.reshape(t // LANES, LANES), e2.reshape(t // LANES, LANES))
    row = lambda r: tab[r, :max_items]
    return dict(pos=pos.reshape(t), n_items=tab[TAB_N, :1], it_tile=row(TAB_TILE), it_ea=row(TAB_EA),
                it_eb=row(TAB_EB), it_lo=row(TAB_LO), it_hi=row(TAB_HI), it_flags=row(TAB_FLAGS))


ROW_DMA_UNROLL = 8


def _row_dma_loop(g0, g1, make_copy, priorities):
    def body(i, c):
        for u in range(ROW_DMA_UNROLL):
            make_copy(i * ROW_DMA_UNROLL + u).start(priority=priorities[u % len(priorities)])
        return c

    lax.fori_loop(g0, g1, body, 0)


def _moe_ffn_kernel(pos_ref, it_tile, it_ea, it_eb, it_lo, it_hi, it_flags, n_items, tok_hbm,
                    wga_ref, wua_ref, wda_ref, wgb_ref, wub_ref, wdb_ref, out_hbm,
                    src_s, xbuf, ybuf, x_s, rt_s, acc_s, wa_s, wb_s, sem_in, sem_out, *, chunk, n_tiles):
    i = pl.program_id(0)
    tm = x_s.shape[0]
    groups = tm // ROW_DMA_UNROLL

    def gather_tile(k):
        base = k * tm
        slot = k % 2
        _row_dma_loop(0, groups, lambda r: pltpu.make_async_copy(
            tok_hbm.at[src_s[base + r]],
            xbuf.at[slot, pl.ds(pl.multiple_of(r * TOK_PITCH, TOK_PITCH), TOK_PITCH), :],
            sem_in.at[slot]), GATHER_PRIORITIES)

    def scatter_tile(k):
        base = k * tm
        slot = k % 2
        _row_dma_loop(0, groups, lambda r: pltpu.make_async_copy(
            ybuf.at[slot, pl.ds(pl.multiple_of(r * SUBLANES, SUBLANES), SUBLANES), :],
            out_hbm.at[src_s[base + r]],
            sem_out.at[slot]), SCATTER_PRIORITIES)

    def expert_ffn(x, w, wg, wu, wd):
        g = _dot(x, wg[...])
        u = _dot(x, wu[...])
        hid = (g * (1.0 / (1.0 + jnp.exp(-g)))) * u
        return _dot((hid * w).astype(bf16), wd[...])

    def wait_all(buf, sem, slot):
        pltpu.make_async_copy(buf.at[slot], buf.at[slot], sem.at[slot]).wait()

    @pl.when(i == 0)
    def _():
        def inv(a, c):
            for u in range(ROW_DMA_UNROLL):
                src_s[pos_ref[a * ROW_DMA_UNROLL + u]] = a * ROW_DMA_UNROLL + u
            return c

        lax.fori_loop(0, pos_ref.shape[0] // ROW_DMA_UNROLL, inv, 0)
        gather_tile(0)

    @pl.when(i < n_items[0])
    def _():
        flags = it_flags[i]
        k = it_tile[i]

        @pl.when((flags & FLAG_NEW_A) != 0)
        def _():
            for dst, src in zip(wa_s, (wga_ref, wua_ref, wda_ref)):
                dst[...] = src[0].astype(bf16)

        @pl.when((flags & FLAG_NEW_B) != 0)
        def _():
            for dst, src in zip(wb_s, (wgb_ref, wub_ref, wdb_ref)):
                dst[...] = src[0].astype(bf16)

        @pl.when((flags & FLAG_TILE_START) != 0)
        def _():
            @pl.when(k + 1 < n_tiles)
            def _():
                gather_tile(k + 1)

            wait_all(xbuf, sem_in, k % 2)
            x_s[...] = _load_token_tiles(xbuf, tm, (k % 2,), TOK_PITCH).astype(bf16)
            rt_s[...] = xbuf[k % 2, pl.ds(TOK_RT_ROW, tm, stride=TOK_PITCH), :]
            acc_s[...] = jnp.zeros_like(acc_s)

        lo = it_lo[i]
        hi = it_hi[i]
        ea = it_ea[i].astype(f32)

        def body(j, c):
            r = pl.multiple_of(j * chunk, chunk)

            @pl.when((r < hi) & (r + chunk > lo))
            def _():
                x = x_s[pl.ds(r, chunk), :]
                rt = rt_s[pl.ds(r, chunk), :]
                first_is_a = rt[:, ROUTE_E1:ROUTE_E1 + 1] == ea
                w1 = rt[:, ROUTE_W1:ROUTE_W1 + 1]
                w2 = rt[:, ROUTE_W2:ROUTE_W2 + 1]
                y = (expert_ffn(x, jnp.where(first_is_a, w1, w2), *wa_s)
                     + expert_ffn(x, jnp.where(first_is_a, w2, w1), *wb_s))
                row = lax.broadcasted_iota(jnp.int32, (chunk, 1), 0) + r
                mine = (row >= lo) & (row < hi)
                acc_s[pl.ds(r, chunk), :] = jnp.where(mine, y, acc_s[pl.ds(r, chunk), :])

            return c

        lax.fori_loop(0, tm // chunk, body, 0)

        @pl.when((flags & FLAG_TILE_END) != 0)
        def _():
            @pl.when(k >= 2)
            def _():
                wait_all(ybuf, sem_out, k % 2)

            _store_token_tiles(ybuf.at[k % 2], acc_s[...])
            scatter_tile(k)

            @pl.when(k == n_tiles - 1)
            def _():
                if n_tiles > 1:
                    wait_all(ybuf, sem_out, (n_tiles - 2) % 2)
                wait_all(ybuf, sem_out, (n_tiles - 1) % 2)


def _moe_ffn(plan, tok, w_gate, w_up, w_down, tm, max_items):
    n_tok = tok.shape[0] // TOK_PITCH
    n_tiles = n_tok // tm
    tok3 = tok.reshape(n_tok, TOK_PITCH, LANES)
    spec_a = lambda shape: pl.BlockSpec((1,) + shape, lambda i, ps, tl, ea, eb, lo, hi, fl, n: (ea[i], 0, 0))
    spec_b = lambda shape: pl.BlockSpec((1,) + shape, lambda i, ps, tl, ea, eb, lo, hi, fl, n: (eb[i], 0, 0))
    shapes = ((D_MODEL, D_EXPERT), (D_MODEL, D_EXPERT), (D_EXPERT, D_MODEL))
    wscratch = lambda: tuple(pltpu.VMEM(s, bf16) for s in shapes)
    return pl.pallas_call(
        functools.partial(_moe_ffn_kernel, chunk=256, n_tiles=n_tiles),
        out_shape=jax.ShapeDtypeStruct((n_tok, SUBLANES, LANES), f32),
        grid_spec=pltpu.PrefetchScalarGridSpec(
            num_scalar_prefetch=8,
            grid=(max_items,),
            in_specs=[pl.BlockSpec(memory_space=pl.ANY)] + [spec_a(s) for s in shapes] + [spec_b(s) for s in shapes],
            out_specs=pl.BlockSpec(memory_space=pl.ANY),
            scratch_shapes=[pltpu.SMEM((n_tok,), jnp.int32),
                            pltpu.VMEM((2, tm * TOK_PITCH, LANES), f32),
                            pltpu.VMEM((2, tm * SUBLANES, LANES), f32),
                            pltpu.VMEM((tm, D_MODEL), bf16), pltpu.VMEM((tm, ROUTE_LANES), f32),
                            pltpu.VMEM((tm, D_MODEL), f32), wscratch(), wscratch(),
                            pltpu.SemaphoreType.DMA((2,)), pltpu.SemaphoreType.DMA((2,))]),
        compiler_params=_cparams(("arbitrary",), 60 * 1024 * 1024),
        name="moe_ffn",
    )(plan['pos'], plan['it_tile'], plan['it_ea'], plan['it_eb'], plan['it_lo'], plan['it_hi'],
      plan['it_flags'], plan['n_items'], tok3, w_gate, w_up, w_down, w_gate, w_up, w_down)


def _final_kernel(moe_ref, x1_ref, mod_ref, fg_ref, yp_ref, yl_ref, *, prompt_tiles):
    i = pl.program_id(0)
    tm = x1_ref.shape[0]
    y = _rms(x1_ref[...] + mod_ref[0, GT2:GT2 + 1, :] * _load_token_tiles(moe_ref, tm), fg_ref[...])

    @pl.when(i < prompt_tiles)
    def _():
        yp_ref[...] = y

    @pl.when(i >= prompt_tiles)
    def _():
        yl_ref[...] = y


def _final(moe, x1, mod, final_g, t_prompt, t_lat, seq_s):
    tm = 512
    npt, nst = t_prompt // tm, t_lat // tm
    spb = seq_s // tm
    moe2 = moe.reshape(moe.shape[0] * SUBLANES, LANES)
    return pl.pallas_call(
        functools.partial(_final_kernel, prompt_tiles=npt),
        out_shape=(jax.ShapeDtypeStruct((t_prompt, D_MODEL), f32),
                   jax.ShapeDtypeStruct((t_lat, D_MODEL), f32)),
        grid=(npt + nst,),
        in_specs=[pl.BlockSpec((tm * SUBLANES, LANES), lambda i: (i, 0)),
                  pl.BlockSpec((tm, D_MODEL), lambda i: (i, 0)),
                  pl.BlockSpec((1, MOD_ROWS, D_MODEL),
                               lambda i: (jnp.where(i < npt, 0, 1 + jnp.maximum(i - npt, 0) // spb), 0, 0)),
                  pl.BlockSpec((1, D_MODEL), lambda i: (0, 0))],
        out_specs=(pl.BlockSpec((tm, D_MODEL), lambda i: (jnp.minimum(i, npt - 1), 0)),
                   pl.BlockSpec((tm, D_MODEL), lambda i: (jnp.maximum(i - npt, 0), 0))),
        compiler_params=_cparams(("arbitrary",)),
        name="moe_combine_final",
    )(moe2, x1, mod, final_g.reshape(1, D_MODEL))


def _moe(tok, rt, x1, mod, w_gate, w_up, w_down, final_g, t_prompt, t_lat, seq_s):
    tm = MOE_TM
    n_rows = t_prompt + t_lat
    assert n_rows % tm == 0
    max_items = n_rows // tm + N_CLASSES
    plan = _route_plan(rt[:, ROUTE_E1], rt[:, ROUTE_E2], tm, max_items)
    moe = _moe_ffn(plan, tok, w_gate, w_up, w_down, tm, max_items)
    return _final(moe, x1, mod, final_g, t_prompt, t_lat, seq_s)


def kernel(x_prompt, x_sample, cache_k, cache_v, c, c_ctx, w_ada, b_ada, norm1_g, w_in, rpb, conv_w, conv_b, filt_w1, filt_b1, filt_w2, filt_b2, filt_w3, filt_freq, hyena_skip, gnorm_att, gnorm_hyena, w_out, norm2_g, router_grp_w, router_grp_b, router_exp_w, router_exp_b, w_gate, w_up, w_down, final_g):
    depth = w_ada.shape[0]
    assert depth == 1
    batch, seq, _ = x_prompt.shape
    dec_batch, dec_seq, _ = x_sample.shape
    l = 0

    def pack_router(grp, exp):
        rows = grp.shape[0]
        return jnp.concatenate([grp, jnp.zeros((rows, ROUTE_EXP_LANE0 - N_GROUPS), f32), exp,
                                jnp.zeros((rows, ROUTE_LANES - ROUTE_EXP_LANE0 - N_EXPERTS), f32)], axis=1)

    wr = pack_router(router_grp_w[l], router_exp_w[l])
    br = pack_router(router_grp_b[l][None, :], router_exp_b[l][None, :])

    lp = {
        'norm1_g': norm1_g[l], 'w_in': w_in[l], 'conv_w': conv_w[l], 'conv_b': conv_b[l],
        'filt_w1': filt_w1[l], 'filt_b1': filt_b1[l], 'filt_w2': filt_w2[l], 'filt_b2': filt_b2[l],
        'filt_w3': filt_w3[l], 'filt_freq': filt_freq[l], 'hyena_skip': hyena_skip[l],
        'gnorm_hyena': gnorm_hyena[l], 'w_out': w_out[l], 'norm2_g': norm2_g[l],
        'wr': wr, 'br': br, 'w_gate': w_gate[l], 'w_up': w_up[l], 'w_down': w_down[l],
    }

    cond8 = jnp.concatenate([c_ctx[None, :], c, jnp.zeros((SUBLANES - 1 - dec_batch, D_MODEL), f32)], axis=0)
    mod = _ada_mod(cond8, w_ada[l], b_ada[l], 1 + dec_batch).reshape(SUBLANES, N_MOD, D_MODEL)
    mod = jnp.pad(mod, ((0, 0), (0, MOD_ROWS - N_MOD), (0, 0)))
    mod_ctx, mod_lat = mod[0:1], mod[1:1 + dec_batch]

    xp = x_prompt.reshape(batch * seq, D_MODEL)
    xs = x_sample.reshape(dec_batch * dec_seq, D_MODEL)

    qp, k_ctx, v_ctx, hyp = _inproj(xp, mod_ctx, lp['norm1_g'], lp['w_in'], batch * seq, f32)
    attp = _ctx_attention(qp, k_ctx, v_ctx, gnorm_att[l], seq)
    hyop = _hyena(hyp, lp, batch, seq)

    ql, kl, vl, hyl = _inproj(xs, mod_lat, lp['norm1_g'], lp['w_in'], dec_seq, bf16)
    kc = cache_k[:, l].reshape(dec_batch * cache_k.shape[2], D_ATT).astype(bf16)
    vc = cache_v[:, l].reshape(dec_batch * cache_v.shape[2], D_ATT).astype(bf16)
    attl = _na_attention(ql, kl, vl, kc, vc, rpb[l], gnorm_att[l], dec_batch, dec_seq)
    hyol = _hyena(hyl, lp, dec_batch, dec_seq)

    x1, tok, rt = _outproj(xp, xs, attp, attl, hyop, hyol, mod[0:1 + dec_batch], lp['w_out'], lp['norm2_g'],
                           lp['wr'], lp['br'], seq, dec_seq)
    yp, ys = _moe(tok, rt, x1, mod[0:1 + dec_batch], lp['w_gate'], lp['w_up'], lp['w_down'], final_g,
                  batch * seq, dec_batch * dec_seq, dec_seq)

    y_prompt = yp.reshape(batch, seq, D_MODEL)
    y_sample = ys.reshape(dec_batch, dec_seq, D_MODEL)
    new_k = k_ctx.reshape(batch, 1, seq, H_ATT, HEAD_DIM)
    new_v = v_ctx.reshape(batch, 1, seq, H_ATT, HEAD_DIM)
    return (y_prompt, y_sample, new_k, new_v)
```

```python
import functools
import math

import jax
import jax.numpy as jnp
import numpy as np
from jax import lax
from jax.experimental import pallas as pl
from jax.experimental.pallas import tpu as pltpu

f32 = jnp.float32
bf16 = jnp.bfloat16
HIGHEST = lax.Precision.HIGHEST

D_MODEL = 1024
GRID_W = 64
H_ATT = 8
HEAD_DIM = 64
D_ATT = H_ATT * HEAD_DIM
D_HYENA = 512
D_IN = 3 * D_ATT + 3 * D_HYENA
NA_ROWS = 8
NA_COLS = 16
SHORT_CONV = 3
FILTER_BANDS = 16
EMB_DIM = 1 + 2 * FILTER_BANDS
FILTER_FF = 64
DECAY_TARGET = 1e-2
MIN_DECAY = math.log(DECAY_TARGET) / 1.5
MAX_DECAY = math.log(DECAY_TARGET) / 0.3
N_GROUPS = 4
EXPERTS_PER_GROUP = 4
N_EXPERTS = N_GROUPS * EXPERTS_PER_GROUP
D_EXPERT = 512
N_MOD = 6
EPS = 1e-6
NEG_INF = -1e30
ATT_SCALE = HEAD_DIM ** -0.5

LANES = 128
SUBLANES = 8
MOD_ROWS = 8
ROUTE_LANES = 128
ROUTE_EXP_LANE0 = 16
ROUTE_E1, ROUTE_E2, ROUTE_W1, ROUTE_W2 = 0, 1, 2, 3
TOK_PITCH = 16
TOK_RT_ROW = 8
MOE_TM = 512
GATHER_PRIORITIES = (0,)
SCATTER_PRIORITIES = (1,)
VMEM_LIMIT = 56 * 1024 * 1024

SH1, SC1, GT1, SH2, SC2, GT2 = range(6)


def _cparams(sem, vmem=VMEM_LIMIT):
    return pltpu.CompilerParams(dimension_semantics=sem, vmem_limit_bytes=vmem)


def _dot(a, b):
    return jnp.dot(a, b, preferred_element_type=f32)


def _dot_hi(a, b):
    return lax.dot_general(a, b, (((1,), (0,)), ((), ())), precision=HIGHEST,
                           preferred_element_type=f32)


def _dot_nt(a, b):
    return lax.dot_general(a, b, (((1,), (1,)), ((), ())), preferred_element_type=f32)


def _rms(x, g):
    ms = jnp.mean(x * x, axis=-1, keepdims=True)
    return x * lax.rsqrt(ms + EPS) * g


def _cast_rows(src_ref, dst_ref, chunk):
    n = src_ref.shape[0] // chunk

    def body(i, c):
        r = pl.multiple_of(i * chunk, chunk)
        dst_ref[pl.ds(r, chunk), :] = src_ref[pl.ds(r, chunk), :].astype(dst_ref.dtype)
        return c

    lax.fori_loop(0, n, body, 0)


def _ada_kernel(ct_ref, w_ref, b_ref, o_ref, *, n_cond):
    ct = ct_ref[...]
    st = ct * (1.0 / (1.0 + jnp.exp(-ct)))
    w = w_ref[...]
    rid = lax.broadcasted_iota(jnp.int32, o_ref.shape, 0)
    out = jnp.broadcast_to(b_ref[...], o_ref.shape)
    for m in range(n_cond):
        row = jnp.sum(w * st[:, m:m + 1], axis=0, keepdims=True)
        out = out + jnp.where(rid == m, row, 0.0)
    o_ref[...] = out


def _ada_mod(cond8, w_ada, b_ada, n_cond):
    tn = 1536
    n = N_MOD * D_MODEL
    return pl.pallas_call(
        functools.partial(_ada_kernel, n_cond=n_cond),
        out_shape=jax.ShapeDtypeStruct((SUBLANES, n), f32),
        grid=(n // tn,),
        in_specs=[pl.BlockSpec((D_MODEL, SUBLANES), lambda j: (0, 0)),
                  pl.BlockSpec((D_MODEL, tn), lambda j: (0, j)),
                  pl.BlockSpec((1, tn), lambda j: (0, j))],
        out_specs=pl.BlockSpec((SUBLANES, tn), lambda j: (0, j)),
        compiler_params=_cparams(("arbitrary",)),
        name="ada_mod",
    )(cond8.T, w_ada, b_ada.reshape(1, n))


def _inproj_kernel(x_ref, mod_ref, g_ref, w_ref, q_ref, k_ref, v_ref, hy_ref, wbf_ref):
    @pl.when(pl.program_id(0) == 0)
    def _():
        _cast_rows(w_ref, wbf_ref, 128)

    h = _rms(x_ref[...], g_ref[...])
    h = h * (1.0 + mod_ref[0, SC1:SC1 + 1, :]) + mod_ref[0, SH1:SH1 + 1, :]
    p = _dot(h.astype(bf16), wbf_ref[...])
    q_ref[...] = p[:, 0:D_ATT].astype(q_ref.dtype)
    k_ref[...] = p[:, D_ATT:2 * D_ATT].astype(k_ref.dtype)
    v_ref[...] = p[:, 2 * D_ATT:3 * D_ATT].astype(v_ref.dtype)
    hy_ref[...] = p[:, 3 * D_ATT:]


def _inproj(x, mod, norm_g, w_in, rows_per_mod, kv_dtype):
    t = x.shape[0]
    tm = 512
    blocks_per_mod = rows_per_mod // tm
    return pl.pallas_call(
        _inproj_kernel,
        out_shape=(jax.ShapeDtypeStruct((t, D_ATT), bf16),
                   jax.ShapeDtypeStruct((t, D_ATT), kv_dtype),
                   jax.ShapeDtypeStruct((t, D_ATT), kv_dtype),
                   jax.ShapeDtypeStruct((t, 3 * D_HYENA), f32)),
        grid=(t // tm,),
        in_specs=[pl.BlockSpec((tm, D_MODEL), lambda i: (i, 0)),
                  pl.BlockSpec((1, MOD_ROWS, D_MODEL), lambda i: (i // blocks_per_mod, 0, 0)),
                  pl.BlockSpec((1, D_MODEL), lambda i: (0, 0)),
                  pl.BlockSpec((D_MODEL, D_IN), lambda i: (0, 0), pipeline_mode=pl.Buffered(1))],
        out_specs=(pl.BlockSpec((tm, D_ATT), lambda i: (i, 0)),
                   pl.BlockSpec((tm, D_ATT), lambda i: (i, 0)),
                   pl.BlockSpec((tm, D_ATT), lambda i: (i, 0)),
                   pl.BlockSpec((tm, 3 * D_HYENA), lambda i: (i, 0))),
        scratch_shapes=[pltpu.VMEM((D_MODEL, D_IN), bf16)],
        compiler_params=_cparams(("arbitrary",)),
        name="inproj",
    )(x, mod, norm_g.reshape(1, D_MODEL), w_in)


def _split_heads(q2):
    lane = lax.broadcasted_iota(jnp.int32, q2.shape, 1)
    qa = jnp.where(lane < HEAD_DIM, q2, 0.0)
    qb = jnp.where(lane >= HEAD_DIM, q2, 0.0)
    return jnp.concatenate([qa, qb], axis=0)


def _merge_heads(o_ab):
    m = o_ab.shape[0] // 2
    lane = lax.broadcasted_iota(jnp.int32, (m, LANES), 1)
    return jnp.where(lane < HEAD_DIM, o_ab[:m], o_ab[m:])


CTX_SEQS_PER_STEP = 2


def _ctx_attn_kernel(q_ref, k_ref, v_ref, g_ref, o_ref, *, seq):
    for b in range(q_ref.shape[0] // seq):
        rows = slice(b * seq, (b + 1) * seq)
        outs = []
        for p in range(D_ATT // LANES):
            cs = slice(p * LANES, (p + 1) * LANES)
            qq = _split_heads(q_ref[rows, cs] * ATT_SCALE).astype(bf16)
            s = _dot_nt(qq, k_ref[rows, cs].astype(bf16))
            m = jnp.max(s, axis=-1, keepdims=True)
            e = jnp.exp(s - m)
            l = jnp.sum(e, axis=-1, keepdims=True)
            o_ab = _dot(e.astype(bf16), v_ref[rows, cs].astype(bf16)) / l
            outs.append(_merge_heads(o_ab))
        o_ref[rows, :] = _rms(jnp.concatenate(outs, axis=-1), g_ref[...]).astype(o_ref.dtype)


def _ctx_attention(q, k, v, gnorm, seq):
    t = q.shape[0]
    tm = CTX_SEQS_PER_STEP * seq
    assert t % tm == 0
    spec = pl.BlockSpec((tm, D_ATT), lambda b: (b, 0))
    return pl.pallas_call(
        functools.partial(_ctx_attn_kernel, seq=seq),
        out_shape=jax.ShapeDtypeStruct((t, D_ATT), bf16),
        grid=(t // tm,),
        in_specs=[spec, spec, spec, pl.BlockSpec((1, D_ATT), lambda b: (0, 0))],
        out_specs=spec,
        compiler_params=_cparams(("arbitrary",)),
        name="ctx_attn",
    )(q, k, v, gnorm.reshape(1, D_ATT))


def _na_tables():
    col = np.arange(GRID_W)
    cs = np.clip(col - NA_COLS // 2, 0, GRID_W - NA_COLS)
    col_mask = (col[None, :] >= cs[:, None]) & (col[None, :] < cs[:, None] + NA_COLS)
    mask = np.tile(col_mask.astype(np.float32), (1, NA_ROWS))
    return mask


N_DR = 2 * NA_ROWS - 1
N_DC = 2 * NA_COLS - 1
BIAS_PAIRS = N_DR - 1


def _na_bias_rows(rpb):
    out = jnp.zeros((H_ATT, BIAS_PAIRS, LANES), f32)
    out = out.at[:, :, 0:N_DC].set(rpb[:, 0:BIAS_PAIRS])
    return out.at[:, :, GRID_W:GRID_W + N_DC].set(rpb[:, 1:N_DR])


NA_ROWS_PER_STEP = 8


def _na_row_start(r, rows):
    return jnp.clip(r - NA_ROWS // 2, 0, rows - NA_ROWS)


def _na_attn_kernel(q_ref, k_ref, v_ref, kc_ref, vc_ref, rp_ref, mask_ref, g_ref, o_ref,
                    t2_ref, *, rows):
    b = pl.program_id(0)
    r = pl.program_id(1)

    @pl.when((b == 0) & (r == 0))
    def _():
        for h in range(H_ATT):
            for i in range(BIAS_PAIRS):
                v = jnp.broadcast_to(rp_ref[h, i:i + 1, :], (GRID_W, LANES))
                t2_ref[h * BIAS_PAIRS + i] = pltpu.roll(v, LANES - (NA_COLS - 1), 1, stride=1, stride_axis=0)

    nwin = NA_ROWS * GRID_W
    valid = mask_ref[...] != 0.0
    valid2 = jnp.concatenate([valid, valid], axis=0)
    for rr in range(NA_ROWS_PER_STEP):
        row = r * NA_ROWS_PER_STEP + rr
        qs = slice(rr * GRID_W, (rr + 1) * GRID_W)
        rs = _na_row_start(row, rows)
        start = pl.multiple_of(rs * GRID_W, GRID_W)
        i0 = rs - row + NA_ROWS - 1
        outs = []
        for p in range(D_ATT // LANES):
            cs = slice(p * LANES, (p + 1) * LANES)
            qq = _split_heads(q_ref[qs, cs] * ATT_SCALE).astype(bf16)
            kw = k_ref[pl.ds(start, nwin), cs]
            vw = v_ref[pl.ds(start, nwin), cs]
            s_lat = _dot_nt(qq, kw)
            s_ctx = _dot_nt(qq, kc_ref[:, cs])
            bias2 = jnp.concatenate(
                [jnp.concatenate([t2_ref[(2 * p + hh) * BIAS_PAIRS + i0 + 2 * jp] for hh in range(2)], axis=0)
                 for jp in range(NA_ROWS // 2)], axis=-1)
            s_lat = jnp.where(valid2, s_lat + bias2, NEG_INF)
            m = jnp.maximum(jnp.max(s_lat, axis=-1, keepdims=True), jnp.max(s_ctx, axis=-1, keepdims=True))
            e_lat = jnp.exp(s_lat - m)
            e_ctx = jnp.exp(s_ctx - m)
            l = jnp.sum(e_lat, axis=-1, keepdims=True) + jnp.sum(e_ctx, axis=-1, keepdims=True)
            o_ab = (_dot(e_lat.astype(bf16), vw) + _dot(e_ctx.astype(bf16), vc_ref[:, cs])) / l
            outs.append(_merge_heads(o_ab))
        o_ref[qs, :] = _rms(jnp.concatenate(outs, axis=-1), g_ref[...]).astype(o_ref.dtype)


def _na_attention(q, k, v, kc, vc, rpb, gnorm, nb, seq):
    rows = seq // GRID_W
    assert rows % NA_ROWS_PER_STEP == 0
    steps = rows // NA_ROWS_PER_STEP
    qrows = NA_ROWS_PER_STEP * GRID_W
    past = kc.shape[0] // nb
    mask = _na_tables()
    return pl.pallas_call(
        functools.partial(_na_attn_kernel, rows=rows),
        out_shape=jax.ShapeDtypeStruct((nb * seq, D_ATT), bf16),
        grid=(nb, steps),
        in_specs=[pl.BlockSpec((qrows, D_ATT), lambda b, r: (b * steps + r, 0)),
                  pl.BlockSpec((seq, D_ATT), lambda b, r: (b, 0)),
                  pl.BlockSpec((seq, D_ATT), lambda b, r: (b, 0)),
                  pl.BlockSpec((past, D_ATT), lambda b, r: (b, 0)),
                  pl.BlockSpec((past, D_ATT), lambda b, r: (b, 0)),
                  pl.BlockSpec((H_ATT, BIAS_PAIRS, LANES), lambda b, r: (0, 0, 0)),
                  pl.BlockSpec((GRID_W, NA_ROWS * GRID_W), lambda b, r: (0, 0)),
                  pl.BlockSpec((1, D_ATT), lambda b, r: (0, 0))],
        out_specs=pl.BlockSpec((qrows, D_ATT), lambda b, r: (b * steps + r, 0)),
        scratch_shapes=[pltpu.VMEM((H_ATT * BIAS_PAIRS, GRID_W, LANES), f32)],
        compiler_params=_cparams(("arbitrary", "arbitrary")),
        name="na_attn",
    )(q, k, v, kc, vc, _na_bias_rows(rpb), jnp.asarray(mask), gnorm.reshape(1, D_ATT))


def _hy_front_kernel(x0_ref, x1_ref, v_ref, w0_ref, w1_ref, wv_ref, b0_ref, b1_ref, bv_ref,
                     zbf_ref, z_ref, x0c_ref):
    seq = x0_ref.shape[0]
    row = lax.broadcasted_iota(jnp.int32, x0_ref.shape, 0)
    first = row == 0
    last = row == seq - 1

    def conv(u_ref, w_ref, b_ref):
        u = u_ref[...]
        up = jnp.where(first, 0.0, pltpu.roll(u, 1, 0))
        un = jnp.where(last, 0.0, pltpu.roll(u, seq - 1, 0))
        y = b_ref[...] + up * w_ref[0:1, :]
        y = y + u * w_ref[1:2, :]
        return y + un * w_ref[2:3, :]

    z = conv(v_ref, wv_ref, bv_ref) * conv(x1_ref, w1_ref, b1_ref)
    z_ref[...] = z
    zbf_ref[...] = z.astype(bf16)
    x0c_ref[...] = conv(x0_ref, w0_ref, b0_ref)


def _hy_front(hy, conv_w, conv_b, nb, seq):
    tc = D_HYENA if seq <= 512 else D_HYENA // 2
    nc = D_HYENA // tc
    n = nb * D_HYENA
    cb = conv_b.reshape(1, 3 * D_HYENA)

    def part(k):
        return (pl.BlockSpec((seq, tc), lambda b, j: (b, k * nc + j)),
                pl.BlockSpec((SHORT_CONV, tc), lambda b, j: (0, k * nc + j)),
                pl.BlockSpec((1, tc), lambda b, j: (0, k * nc + j)))

    (x0s, w0s, b0s), (x1s, w1s, b1s), (vs, wvs, bvs) = part(0), part(1), part(2)
    ospec = pl.BlockSpec((seq, tc), lambda b, j: (0, b * nc + j))
    return pl.pallas_call(
        _hy_front_kernel,
        out_shape=(jax.ShapeDtypeStruct((seq, n), bf16),
                   jax.ShapeDtypeStruct((seq, n), f32),
                   jax.ShapeDtypeStruct((seq, n), f32)),
        grid=(nb, nc),
        in_specs=[x0s, x1s, vs, w0s, w1s, wvs, b0s, b1s, bvs],
        out_specs=(ospec, ospec, ospec),
        compiler_params=_cparams(("arbitrary", "arbitrary")),
        name="hyena_front",
    )(hy, hy, hy, conv_w, conv_w, conv_w, cb, cb, cb)


def _filter_features(seq):
    t = np.linspace(0.0, 1.0, seq, dtype=np.float64)[:, None]
    w = 2.0 * math.pi * np.arange(seq, dtype=np.float64)[:, None] / seq
    fb = np.linspace(1e-4, FILTER_BANDS - 1, FILTER_BANDS, dtype=np.float64)[None, :]
    ang = fb * w
    z = np.concatenate([t, np.cos(ang), -np.sin(ang)], axis=-1).astype(np.float32)
    return np.pad(z, ((0, 0), (0, LANES - EMB_DIM)))


def _filt_kernel(zt_ref, w1t_ref, b1_ref, fr_ref, w2t_ref, b2_ref, w3_ref, dl_ref, h_ref, kl_ref, *, seq):
    i = pl.program_id(0)
    tr = zt_ref.shape[1]
    fr = fr_ref[...]
    h = jnp.sin(fr * (_dot_hi(w1t_ref[...], zt_ref[...]) + b1_ref[...]))
    h = jnp.sin(fr * (_dot_hi(w2t_ref[...], h) + b2_ref[...]))
    h = _dot_hi(h.T, w3_ref[...])
    row = lax.broadcasted_iota(jnp.int32, (tr, D_HYENA), 0) + i * tr
    t = row[:, 0:1].astype(f32) * (1.0 / (seq - 1))
    decay = jnp.exp(-t * dl_ref[...])
    hf = h[:, :D_HYENA] * decay
    hb = jnp.where(row == 0, 0.0, h[:, D_HYENA:] * decay)
    h_ref[:, :D_HYENA] = (hf + hb).astype(bf16)
    h_ref[:, D_HYENA:] = (hb - hf).astype(bf16)
    alt = (1 - 2 * (row & 1)).astype(f32)
    part = jnp.sum(alt * (hf + hb), axis=0, keepdims=True)

    @pl.when(i == 0)
    def _():
        kl_ref[...] = jnp.zeros_like(kl_ref)

    kl_ref[...] += jnp.broadcast_to(part, kl_ref.shape)


def _hy_filters(seq, w1, b1, w2, b2, w3, freq):
    tr = 256
    zt = jnp.asarray(np.ascontiguousarray(_filter_features(seq).T))
    deltas = np.abs(np.linspace(MIN_DECAY, MAX_DECAY, D_HYENA, dtype=np.float64))[None, :].astype(np.float32)
    w1t = jnp.pad(w1, ((0, LANES - EMB_DIM), (0, 0))).T
    const = lambda shape: pl.BlockSpec(shape, lambda i: (0, 0))
    col = lambda v: v.reshape(FILTER_FF, 1)
    return pl.pallas_call(
        functools.partial(_filt_kernel, seq=seq),
        out_shape=(jax.ShapeDtypeStruct((seq, 2 * D_HYENA), bf16),
                   jax.ShapeDtypeStruct((SUBLANES, D_HYENA), f32)),
        grid=(seq // tr,),
        in_specs=[pl.BlockSpec((LANES, tr), lambda i: (0, i)),
                  const((FILTER_FF, LANES)), const((FILTER_FF, 1)), const((FILTER_FF, 1)),
                  const((FILTER_FF, FILTER_FF)), const((FILTER_FF, 1)),
                  const((FILTER_FF, 2 * D_HYENA)), const((1, D_HYENA))],
        out_specs=(pl.BlockSpec((tr, 2 * D_HYENA), lambda i: (i, 0)),
                   pl.BlockSpec((SUBLANES, D_HYENA), lambda i: (0, 0))),
        compiler_params=_cparams(("arbitrary",)),
        name="hyena_filters",
    )(zt, w1t, col(b1), col(freq), w2.T, col(b2), w3, jnp.asarray(deltas))


def _dft_mats(seq):
    n = 2 * seq
    ph = (np.arange(seq, dtype=np.int64)[:, None] * np.arange(seq, dtype=np.int64)[None, :]) % n
    ang = ph.astype(np.float64) * (2.0 * math.pi / n)
    return np.cos(ang).astype(np.float32), np.sin(ang).astype(np.float32)


def _alt_col(rows, offset):
    row = lax.broadcasted_iota(jnp.int32, (rows, 1), 0) + offset
    return (1 - 2 * (row & 1)).astype(f32)


def _hy_fwd_kernel(fr_ref, fi_ref, z_ref, h_ref, kl_ref, yr_ref, yi_ref, yl_ref, kr_s, ki_s, *, n):
    i = pl.program_id(0)
    j = pl.program_id(1)
    tf = fr_ref.shape[0]
    tn = z_ref.shape[1]
    frb = fr_ref[...].astype(bf16)
    fib = fi_ref[...].astype(bf16)

    @pl.when(j == 0)
    def _():
        f = lax.broadcasted_iota(jnp.int32, (tf, 1), 0) + i * tf
        cf = jnp.where(f == 0, 1.0 / n, 2.0 / n)
        kr_s[...] = _dot(frb, h_ref[:, :D_HYENA]) * cf
        ki_s[...] = _dot(fib, h_ref[:, D_HYENA:]) * cf

    a = _dot(frb, z_ref[...])
    b = _dot(fib, z_ref[...])
    kr = kr_s[...]
    ki = ki_s[...]
    for c in range(tn // D_HYENA):
        cs = slice(c * D_HYENA, (c + 1) * D_HYENA)
        yr_ref[:, cs] = (a[:, cs] * kr + b[:, cs] * ki).astype(bf16)
        yi_ref[:, cs] = (b[:, cs] * kr - a[:, cs] * ki).astype(bf16)

    @pl.when(i == 0)
    def _():
        alt = _alt_col(z_ref.shape[0], 0)
        nz = jnp.sum(z_ref[...].astype(f32) * alt, axis=0, keepdims=True)
        kl = jnp.concatenate([kl_ref[0:1, :]] * (tn // D_HYENA), axis=-1)
        yl_ref[...] = jnp.broadcast_to(nz * kl * (1.0 / n), yl_ref.shape)


def _hy_fwd(fr, fi, zbf, hcat, kl, seq):
    n_cols = zbf.shape[1]
    tf = 256
    tn = min(n_cols, 2048)
    ni, nj = seq // tf, n_cols // tn
    assert ni == 1 or nj == 1
    return pl.pallas_call(
        functools.partial(_hy_fwd_kernel, n=2 * seq),
        out_shape=(jax.ShapeDtypeStruct((seq, n_cols), bf16),
                   jax.ShapeDtypeStruct((seq, n_cols), bf16),
                   jax.ShapeDtypeStruct((SUBLANES, n_cols), f32)),
        grid=(ni, nj),
        in_specs=[pl.BlockSpec((tf, seq), lambda i, j: (i, 0)),
                  pl.BlockSpec((tf, seq), lambda i, j: (i, 0)),
                  pl.BlockSpec((seq, tn), lambda i, j: (0, j)),
                  pl.BlockSpec((seq, 2 * D_HYENA), lambda i, j: (0, 0)),
                  pl.BlockSpec((SUBLANES, D_HYENA), lambda i, j: (0, 0))],
        out_specs=(pl.BlockSpec((tf, tn), lambda i, j: (i, j)),
                   pl.BlockSpec((tf, tn), lambda i, j: (i, j)),
                   pl.BlockSpec((SUBLANES, tn), lambda i, j: (0, j))),
        scratch_shapes=[pltpu.VMEM((tf, D_HYENA), f32), pltpu.VMEM((tf, D_HYENA), f32)],
        compiler_params=_cparams(("arbitrary", "arbitrary")),
        name="hyena_dft_fwd",
    )(fr, fi, zbf, hcat, kl)


def _hy_inv_kernel(fr_ref, fi_ref, yr_ref, yi_ref, yl_ref, z_ref, x0_ref, skip_ref, g_ref, o_ref):
    tt = fr_ref.shape[0]
    tn = yr_ref.shape[1]
    y = _dot(fr_ref[...].astype(bf16), yr_ref[...]) + _dot(fi_ref[...].astype(bf16), yi_ref[...])
    alt = _alt_col(tt, pl.program_id(0) * tt)
    for c in range(tn // D_HYENA):
        cs = slice(c * D_HYENA, (c + 1) * D_HYENA)
        yc = y[:, cs] + alt * yl_ref[0:1, cs] + z_ref[:, cs] * skip_ref[...]
        o_ref[:, cs] = _rms(yc * x0_ref[:, cs], g_ref[...]).astype(o_ref.dtype)


def _hy_inv(fr, fi, yr, yi, yl, z, x0c, skip, gnorm, seq):
    n_cols = z.shape[1]
    tt = 256
    tn = min(n_cols, 2048)
    blk = pl.BlockSpec((tt, tn), lambda i, j: (i, j))
    return pl.pallas_call(
        _hy_inv_kernel,
        out_shape=jax.ShapeDtypeStruct((seq, n_cols), bf16),
        grid=(seq // tt, n_cols // tn),
        in_specs=[pl.BlockSpec((tt, seq), lambda i, j: (i, 0)),
                  pl.BlockSpec((tt, seq), lambda i, j: (i, 0)),
                  pl.BlockSpec((seq, tn), lambda i, j: (0, j)),
                  pl.BlockSpec((seq, tn), lambda i, j: (0, j)),
                  pl.BlockSpec((SUBLANES, tn), lambda i, j: (0, j)),
                  blk, blk,
                  pl.BlockSpec((1, D_HYENA), lambda i, j: (0, 0)),
                  pl.BlockSpec((1, D_HYENA), lambda i, j: (0, 0))],
        out_specs=blk,
        compiler_params=_cparams(("arbitrary", "arbitrary")),
        name="hyena_dft_inv",
    )(fr, fi, yr, yi, yl, z, x0c, skip.reshape(1, D_HYENA), gnorm.reshape(1, D_HYENA))


def _hyena(hy, lp, nb, seq):
    fr_np, fi_np = _dft_mats(seq)
    fr, fi = jnp.asarray(fr_np), jnp.asarray(fi_np)
    zbf, z, x0c = _hy_front(hy, lp['conv_w'], lp['conv_b'], nb, seq)
    hcat, kl = _hy_filters(seq, lp['filt_w1'], lp['filt_b1'], lp['filt_w2'], lp['filt_b2'],
                           lp['filt_w3'], lp['filt_freq'])
    yr, yi, yl = _hy_fwd(fr, fi, zbf, hcat, kl, seq)
    return _hy_inv(fr, fi, yr, yi, yl, z, x0c, lp['hyena_skip'], lp['gnorm_hyena'], seq)


def _store_token_tiles(ref, x, pitch=SUBLANES):
    m = x.shape[0]
    for c in range(D_MODEL // LANES):
        ref[pl.ds(c, m, stride=pitch), :] = x[:, c * LANES:(c + 1) * LANES]


def _load_token_tiles(ref, m, lead=(), pitch=SUBLANES):
    return jnp.concatenate([ref[lead + (pl.ds(c, m, stride=pitch), slice(None))]
                            for c in range(D_MODEL // LANES)], axis=-1)


def _route(logits):
    lane_i = lax.broadcasted_iota(jnp.int32, logits.shape, 1)
    lane = lane_i.astype(f32)
    big = float(ROUTE_LANES)
    is_g = lane_i < N_GROUPS
    mg = jnp.max(jnp.where(is_g, logits, -jnp.inf), axis=-1, keepdims=True)
    sg = jnp.sum(jnp.where(is_g, jnp.exp(logits - mg), 0.0), axis=-1, keepdims=True)
    g_w = 1.0 / sg
    g_idx = jnp.min(jnp.where(is_g & (logits == mg), lane, big), axis=-1, keepdims=True)
    e_id = lane_i - ROUTE_EXP_LANE0
    sel = (e_id >= 0) & (e_id < N_EXPERTS) & ((e_id >> 2).astype(f32) == g_idx)
    me = jnp.max(jnp.where(sel, logits, -jnp.inf), axis=-1, keepdims=True)
    ee = jnp.where(sel, jnp.exp(logits - me), 0.0)
    prob = ee / jnp.sum(ee, axis=-1, keepdims=True)
    p1 = jnp.max(jnp.where(sel, prob, -1.0), axis=-1, keepdims=True)
    i1 = jnp.min(jnp.where(sel & (prob == p1), lane, big), axis=-1, keepdims=True)
    sel2 = sel & (lane != i1)
    p2 = jnp.max(jnp.where(sel2, prob, -1.0), axis=-1, keepdims=True)
    i2 = jnp.min(jnp.where(sel2 & (prob == p2), lane, big), axis=-1, keepdims=True)
    tot = p1 + p2
    rec = jnp.where(lane_i == ROUTE_E1, i1 - ROUTE_EXP_LANE0, 0.0)
    rec = jnp.where(lane_i == ROUTE_E2, i2 - ROUTE_EXP_LANE0, rec)
    rec = jnp.where(lane_i == ROUTE_W1, g_w * (p1 / tot), rec)
    return jnp.where(lane_i == ROUTE_W2, g_w * (p2 / tot), rec)


def _outproj_kernel(xp_ref, xs_ref, attp_ref, atts_ref, hyp_ref, hys_ref, mod_ref, wo_ref, g2_ref,
                    wr_ref, br_ref, x1_ref, tok_ref, ee_ref, wobf_ref, wrh_ref, wrl_ref, *, prompt_tiles):
    i = pl.program_id(0)

    @pl.when(i == 0)
    def _():
        _cast_rows(wo_ref, wobf_ref, 128)
        wr = wr_ref[...]
        hi = wr.astype(bf16)
        wrh_ref[...] = hi
        wrl_ref[...] = (wr - hi.astype(f32)).astype(bf16)

    is_p = i < prompt_tiles
    hyp = jnp.concatenate([hyp_ref[:, b * D_HYENA:(b + 1) * D_HYENA]
                           for b in range(hyp_ref.shape[1] // D_HYENA)], axis=0)
    x = jnp.where(is_p, xp_ref[...], xs_ref[...])
    att = jnp.where(is_p, attp_ref[...], atts_ref[...])
    hyo = jnp.where(is_p, hyp, hys_ref[...])
    proj = _dot(att, wobf_ref[0:D_ATT, :]) + _dot(hyo, wobf_ref[D_ATT:, :])
    x1 = x + mod_ref[0, GT1:GT1 + 1, :] * proj
    x1_ref[...] = x1
    h2 = _rms(x1, g2_ref[...]) * (1.0 + mod_ref[0, SC2:SC2 + 1, :]) + mod_ref[0, SH2:SH2 + 1, :]
    h2h = h2.astype(bf16)
    h2l = (h2 - h2h.astype(f32)).astype(bf16)
    logits = _dot(h2h, wrh_ref[...]) + _dot(h2l, wrh_ref[...]) + _dot(h2h, wrl_ref[...]) + br_ref[...]
    rt = _route(logits)
    per_row = h2.shape[0] // LANES
    for j, lane0 in enumerate((ROUTE_E1, ROUTE_E2)):
        for q in range(per_row):
            col = rt[q * LANES:(q + 1) * LANES, lane0:lane0 + 1]
            ee_ref[0, j * per_row + q:j * per_row + q + 1, :] = jnp.broadcast_to(col, (LANES, LANES)).T[0:1, :]
    tok_ref[...] = jnp.zeros_like(tok_ref)
    _store_token_tiles(tok_ref, h2, TOK_PITCH)
    tok_ref[pl.ds(TOK_RT_ROW, h2.shape[0], stride=TOK_PITCH), :] = rt


def _outproj(xp, xs, attp, atts, hyp, hys, mod, w_out, norm2_g, wr, br, seq_p, seq_s):
    tm = 512
    tp, ts = xp.shape[0], xs.shape[0]
    npt, nst = tp // tm, ts // tm
    assert tm % seq_p == 0 and seq_s % tm == 0
    spb = seq_s // tm
    bpt = tm // seq_p
    p_idx = lambda i: jnp.minimum(i, npt - 1)
    s_idx = lambda i: jnp.maximum(i - npt, 0)
    const = lambda shape: pl.BlockSpec(shape, lambda i: (0,) * len(shape))
    return pl.pallas_call(
        functools.partial(_outproj_kernel, prompt_tiles=npt),
        out_shape=(jax.ShapeDtypeStruct((tp + ts, D_MODEL), f32),
                   jax.ShapeDtypeStruct(((tp + ts) * TOK_PITCH, LANES), f32),
                   jax.ShapeDtypeStruct((npt + nst, 2 * tm // LANES, LANES), f32)),
        grid=(npt + nst,),
        in_specs=[pl.BlockSpec((tm, D_MODEL), lambda i: (p_idx(i), 0)),
                  pl.BlockSpec((tm, D_MODEL), lambda i: (s_idx(i), 0)),
                  pl.BlockSpec((tm, D_ATT), lambda i: (p_idx(i), 0)),
                  pl.BlockSpec((tm, D_ATT), lambda i: (s_idx(i), 0)),
                  pl.BlockSpec((seq_p, bpt * D_HYENA), lambda i: (0, p_idx(i))),
                  pl.BlockSpec((tm, D_HYENA), lambda i: (s_idx(i) % spb, s_idx(i) // spb)),
                  pl.BlockSpec((1, MOD_ROWS, D_MODEL),
                               lambda i: (jnp.where(i < npt, 0, 1 + s_idx(i) // spb), 0, 0)),
                  const((D_MODEL, D_MODEL)), const((1, D_MODEL)),
                  const((D_MODEL, ROUTE_LANES)), const((1, ROUTE_LANES))],
        out_specs=(pl.BlockSpec((tm, D_MODEL), lambda i: (i, 0)),
                   pl.BlockSpec((tm * TOK_PITCH, LANES), lambda i: (i, 0)),
                   pl.BlockSpec((1, 2 * tm // LANES, LANES), lambda i: (i, 0, 0))),
        scratch_shapes=[pltpu.VMEM((D_MODEL, D_MODEL), bf16),
                        pltpu.VMEM((D_MODEL, ROUTE_LANES), bf16), pltpu.VMEM((D_MODEL, ROUTE_LANES), bf16)],
        compiler_params=_cparams(("arbitrary",)),
        name="outproj_router",
    )(xp, xs, attp, atts, hyp, hys, mod, w_out, norm2_g.reshape(1, D_MODEL), wr, br)


PAIRS_PER_GROUP = 6
N_CLASSES = N_GROUPS * PAIRS_PER_GROUP
PAIR_SLOT_A = (0, 0, 0, 1, 1, 3)
PAIR_SLOT_B = (1, 2, 3, 3, 2, 2)
FLAG_NEW_A, FLAG_NEW_B, FLAG_TILE_START, FLAG_TILE_END = 1, 2, 4, 8


CLASS_ROWS = 32
TAB_TILE, TAB_EA, TAB_EB, TAB_LO, TAB_HI, TAB_FLAGS, TAB_N = range(7)


def _select_by(idx, values):
    out = jnp.full(idx.shape, float(values[-1]), f32)
    for i in range(len(values) - 2, -1, -1):
        out = jnp.where(idx == i, float(values[i]), out)
    return out


def _plan_kernel(e1_ref, e2_ref, pos_ref, tab_ref, *, tm):
    e1 = e1_ref[...]
    e2 = e2_ref[...]
    rows = e1.shape[0]
    grp = jnp.floor(e1 * (1.0 / EXPERTS_PER_GROUP))
    l1 = e1 - EXPERTS_PER_GROUP * grp
    l2 = e2 - EXPERTS_PER_GROUP * jnp.floor(e2 * (1.0 / EXPERTS_PER_GROUP))
    lo, hi = jnp.minimum(l1, l2), jnp.maximum(l1, l2)
    pair = jnp.where(lo == 0, hi - 1, jnp.where(lo == 1, jnp.where(hi == 3, 3.0, 4.0), 5.0))
    cls = grp * PAIRS_PER_GROUP + pair

    ri = lax.broadcasted_iota(jnp.int32, (LANES, LANES), 0)
    ci = lax.broadcasted_iota(jnp.int32, (LANES, LANES), 1)
    upper = (ri <= ci).astype(bf16)
    rr = lax.broadcasted_iota(jnp.int32, (rows, rows), 0)
    rc = lax.broadcasted_iota(jnp.int32, (rows, rows), 1)
    strict_lower = (rc < rr).astype(bf16)
    cid = lax.broadcasted_iota(jnp.int32, (CLASS_ROWS, 1), 0)

    pos = jnp.zeros(e1.shape, f32)
    base = jnp.zeros((1, 1), f32)
    cnt_col = jnp.zeros((CLASS_ROWS, 1), f32)
    off_col = jnp.zeros((CLASS_ROWS, 1), f32)
    for c in range(N_CLASSES):
        m = cls == c
        within = _dot(m.astype(bf16), upper)
        tot = jnp.broadcast_to(within[:, LANES - 1:LANES], within.shape)
        before = _dot(strict_lower, tot.astype(bf16))[:, 0:1]
        count = jnp.sum(within[:, LANES - 1:LANES], axis=0, keepdims=True)
        pos = jnp.where(m, base + before + within - 1.0, pos)
        cnt_col = jnp.where(cid == c, count, cnt_col)
        off_col = jnp.where(cid == c, base, off_col)
        base = base + count
    pos_ref[...] = pos.astype(jnp.int32)

    end_col = off_col + cnt_col
    inv_tm = 1.0 / tm
    first_col = jnp.floor(off_col * inv_tm)
    ntl_col = jnp.where(cnt_col > 0, jnp.floor((end_col - 1.0) * inv_tm) - first_col + 1.0, 0.0)
    cr = lax.broadcasted_iota(jnp.int32, (CLASS_ROWS, CLASS_ROWS), 0)
    cc = lax.broadcasted_iota(jnp.int32, (CLASS_ROWS, CLASS_ROWS), 1)
    lower = (cc <= cr).astype(bf16)
    iend_col = _dot(lower, jnp.broadcast_to(ntl_col, (CLASS_ROWS, LANES)).astype(bf16))[:, 0:1]
    istart_col = iend_col - ntl_col
    n_items = jnp.sum(ntl_col, axis=0, keepdims=True)
    slots = lax.broadcasted_iota(jnp.int32, (1, LANES), 1).astype(f32)
    items = jnp.minimum(slots, n_items - 1.0)
    past = jnp.where((items >= iend_col) & (cid < N_CLASSES), 1.0, 0.0)
    it_cls = jnp.minimum(jnp.sum(past, axis=0, keepdims=True), N_CLASSES - 1.0)
    sel = it_cls == cid.astype(f32)
    pick = lambda col: jnp.sum(jnp.where(sel, col, 0.0), axis=0, keepdims=True)
    grp_col = jnp.floor(cid.astype(f32) * (1.0 / PAIRS_PER_GROUP))
    pair_col = cid.astype(f32) - PAIRS_PER_GROUP * grp_col
    ea_col = EXPERTS_PER_GROUP * grp_col + _select_by(pair_col, PAIR_SLOT_A)
    eb_col = EXPERTS_PER_GROUP * grp_col + _select_by(pair_col, PAIR_SLOT_B)
    it_tile = pick(first_col) + items - pick(istart_col)
    it_lo = jnp.maximum(pick(off_col) - it_tile * tm, 0.0)
    it_hi = jnp.minimum(pick(end_col) - it_tile * tm, float(tm))
    it_ea, it_eb = pick(ea_col), pick(eb_col)
    prev = lambda v: pltpu.roll(jnp.broadcast_to(v, (SUBLANES, LANES)), 1, 1)[0:1]
    changed = lambda v: jnp.where((slots == 0) | (v != prev(v)), 1.0, 0.0)
    flags = (FLAG_NEW_A * changed(it_ea) + FLAG_NEW_B * changed(it_eb)
             + FLAG_TILE_START * jnp.where(it_lo == 0, 1.0, 0.0) + FLAG_TILE_END * jnp.where(it_hi == tm, 1.0, 0.0))
    table = {TAB_TILE: it_tile, TAB_EA: it_ea, TAB_EB: it_eb, TAB_LO: it_lo, TAB_HI: it_hi, TAB_FLAGS: flags,
             TAB_N: jnp.broadcast_to(n_items, (1, LANES))}
    trow = lax.broadcasted_iota(jnp.int32, tab_ref.shape, 0)
    tab = jnp.zeros(tab_ref.shape, f32)
    for r, v in table.items():
        tab = jnp.where(trow == r, v, tab)
    tab_ref[...] = tab.astype(jnp.int32)


def _route_plan(ee, tm, max_items):
    per = ee.shape[1] // 2
    t = ee.shape[0] * per * LANES
    assert max_items <= LANES and tm & (tm - 1) == 0
    pos, tab = pl.pallas_call(
        functools.partial(_plan_kernel, tm=tm),
        out_shape=(jax.ShapeDtypeStruct((t // LANES, LANES), jnp.int32),
                   jax.ShapeDtypeStruct((SUBLANES, LANES), jnp.int32)),
        name="moe_plan",
    )(ee[:, :per].reshape(t // LANES, LANES), ee[:, per:].reshape(t // LANES, LANES))
    row = lambda r: tab[r, :max_items]
    return dict(pos=pos.reshape(t), n_items=tab[TAB_N, :1], it_tile=row(TAB_TILE), it_ea=row(TAB_EA),
                it_eb=row(TAB_EB), it_lo=row(TAB_LO), it_hi=row(TAB_HI), it_flags=row(TAB_FLAGS))


ROW_DMA_UNROLL = 8


def _row_dma_loop(g0, g1, make_copy, priorities):
    def body(i, c):
        for u in range(ROW_DMA_UNROLL):
            make_copy(i * ROW_DMA_UNROLL + u).start(priority=priorities[u % len(priorities)])
        return c

    lax.fori_loop(g0, g1, body, 0)


def _moe_ffn_kernel(pos_ref, it_tile, it_ea, it_eb, it_lo, it_hi, it_flags, n_items, tok_hbm,
                    wga_ref, wua_ref, wda_ref, wgb_ref, wub_ref, wdb_ref, out_hbm,
                    src_s, xbuf, ybuf, x_s, rt_s, acc_s, wa_s, wb_s, sem_in, sem_out, *, chunk, n_tiles):
    i = pl.program_id(0)
    tm = x_s.shape[0]
    groups = tm // ROW_DMA_UNROLL

    def gather_tile(k):
        base = k * tm
        slot = k % 2
        _row_dma_loop(0, groups, lambda r: pltpu.make_async_copy(
            tok_hbm.at[src_s[base + r]],
            xbuf.at[slot, pl.ds(pl.multiple_of(r * TOK_PITCH, TOK_PITCH), TOK_PITCH), :],
            sem_in.at[slot]), GATHER_PRIORITIES)

    def scatter_tile(k):
        base = k * tm
        slot = k % 2
        _row_dma_loop(0, groups, lambda r: pltpu.make_async_copy(
            ybuf.at[slot, pl.ds(pl.multiple_of(r * SUBLANES, SUBLANES), SUBLANES), :],
            out_hbm.at[src_s[base + r]],
            sem_out.at[slot]), SCATTER_PRIORITIES)

    def expert_ffn(x, w, wg, wu, wd):
        g = _dot(x, wg[...])
        u = _dot(x, wu[...])
        hid = (g * (1.0 / (1.0 + jnp.exp(-g)))) * u
        return _dot((hid * w).astype(bf16), wd[...])

    def wait_all(buf, sem, slot):
        pltpu.make_async_copy(buf.at[slot], buf.at[slot], sem.at[slot]).wait()

    @pl.when(i == 0)
    def _():
        def inv(a, c):
            for u in range(ROW_DMA_UNROLL):
                src_s[pos_ref[a * ROW_DMA_UNROLL + u]] = a * ROW_DMA_UNROLL + u
            return c

        lax.fori_loop(0, pos_ref.shape[0] // ROW_DMA_UNROLL, inv, 0)
        gather_tile(0)

    @pl.when(i < n_items[0])
    def _():
        flags = it_flags[i]
        k = it_tile[i]

        @pl.when((flags & FLAG_NEW_A) != 0)
        def _():
            for dst, src in zip(wa_s, (wga_ref, wua_ref, wda_ref)):
                dst[...] = src[0].astype(bf16)

        @pl.when((flags & FLAG_NEW_B) != 0)
        def _():
            for dst, src in zip(wb_s, (wgb_ref, wub_ref, wdb_ref)):
                dst[...] = src[0].astype(bf16)

        @pl.when((flags & FLAG_TILE_START) != 0)
        def _():
            @pl.when(k + 1 < n_tiles)
            def _():
                gather_tile(k + 1)

            wait_all(xbuf, sem_in, k % 2)
            x_s[...] = _load_token_tiles(xbuf, tm, (k % 2,), TOK_PITCH).astype(bf16)
            rt_s[...] = xbuf[k % 2, pl.ds(TOK_RT_ROW, tm, stride=TOK_PITCH), :]
            acc_s[...] = jnp.zeros_like(acc_s)

        lo = it_lo[i]
        hi = it_hi[i]
        ea = it_ea[i].astype(f32)

        def body(j, c):
            r = pl.multiple_of(j * chunk, chunk)

            @pl.when((r < hi) & (r + chunk > lo))
            def _():
                x = x_s[pl.ds(r, chunk), :]
                rt = rt_s[pl.ds(r, chunk), :]
                first_is_a = rt[:, ROUTE_E1:ROUTE_E1 + 1] == ea
                w1 = rt[:, ROUTE_W1:ROUTE_W1 + 1]
                w2 = rt[:, ROUTE_W2:ROUTE_W2 + 1]
                y = (expert_ffn(x, jnp.where(first_is_a, w1, w2), *wa_s)
                     + expert_ffn(x, jnp.where(first_is_a, w2, w1), *wb_s))
                row = lax.broadcasted_iota(jnp.int32, (chunk, 1), 0) + r
                mine = (row >= lo) & (row < hi)
                acc_s[pl.ds(r, chunk), :] = jnp.where(mine, y, acc_s[pl.ds(r, chunk), :])

            return c

        lax.fori_loop(0, tm // chunk, body, 0)

        @pl.when((flags & FLAG_TILE_END) != 0)
        def _():
            @pl.when(k >= 2)
            def _():
                wait_all(ybuf, sem_out, k % 2)

            _store_token_tiles(ybuf.at[k % 2], acc_s[...])
            scatter_tile(k)

            @pl.when(k == n_tiles - 1)
            def _():
                if n_tiles > 1:
                    wait_all(ybuf, sem_out, (n_tiles - 2) % 2)
                wait_all(ybuf, sem_out, (n_tiles - 1) % 2)


def _moe_ffn(plan, tok, w_gate, w_up, w_down, tm, max_items):
    n_tok = tok.shape[0] // TOK_PITCH
    n_tiles = n_tok // tm
    tok3 = tok.reshape(n_tok, TOK_PITCH, LANES)
    spec_a = lambda shape: pl.BlockSpec((1,) + shape, lambda i, ps, tl, ea, eb, lo, hi, fl, n: (ea[i], 0, 0))
    spec_b = lambda shape: pl.BlockSpec((1,) + shape, lambda i, ps, tl, ea, eb, lo, hi, fl, n: (eb[i], 0, 0))
    shapes = ((D_MODEL, D_EXPERT), (D_MODEL, D_EXPERT), (D_EXPERT, D_MODEL))
    wscratch = lambda: tuple(pltpu.VMEM(s, bf16) for s in shapes)
    return pl.pallas_call(
        functools.partial(_moe_ffn_kernel, chunk=256, n_tiles=n_tiles),
        out_shape=jax.ShapeDtypeStruct((n_tok, SUBLANES, LANES), f32),
        grid_spec=pltpu.PrefetchScalarGridSpec(
            num_scalar_prefetch=8,
            grid=(max_items,),
            in_specs=[pl.BlockSpec(memory_space=pl.ANY)] + [spec_a(s) for s in shapes] + [spec_b(s) for s in shapes],
            out_specs=pl.BlockSpec(memory_space=pl.ANY),
            scratch_shapes=[pltpu.SMEM((n_tok,), jnp.int32),
                            pltpu.VMEM((2, tm * TOK_PITCH, LANES), f32),
                            pltpu.VMEM((2, tm * SUBLANES, LANES), f32),
                            pltpu.VMEM((tm, D_MODEL), bf16), pltpu.VMEM((tm, ROUTE_LANES), f32),
                            pltpu.VMEM((tm, D_MODEL), f32), wscratch(), wscratch(),
                            pltpu.SemaphoreType.DMA((2,)), pltpu.SemaphoreType.DMA((2,))]),
        compiler_params=_cparams(("arbitrary",), 60 * 1024 * 1024),
        name="moe_ffn",
    )(plan['pos'], plan['it_tile'], plan['it_ea'], plan['it_eb'], plan['it_lo'], plan['it_hi'],
      plan['it_flags'], plan['n_items'], tok3, w_gate, w_up, w_down, w_gate, w_up, w_down)


def _final_kernel(moe_ref, x1_ref, mod_ref, fg_ref, yp_ref, yl_ref, *, prompt_tiles):
    i = pl.program_id(0)
    tm = x1_ref.shape[0]
    y = _rms(x1_ref[...] + mod_ref[0, GT2:GT2 + 1, :] * _load_token_tiles(moe_ref, tm), fg_ref[...])

    @pl.when(i < prompt_tiles)
    def _():
        yp_ref[...] = y

    @pl.when(i >= prompt_tiles)
    def _():
        yl_ref[...] = y


def _final(moe, x1, mod, final_g, t_prompt, t_lat, seq_s):
    tm = 512
    npt, nst = t_prompt // tm, t_lat // tm
    spb = seq_s // tm
    moe2 = moe.reshape(moe.shape[0] * SUBLANES, LANES)
    return pl.pallas_call(
        functools.partial(_final_kernel, prompt_tiles=npt),
        out_shape=(jax.ShapeDtypeStruct((t_prompt, D_MODEL), f32),
                   jax.ShapeDtypeStruct((t_lat, D_MODEL), f32)),
        grid=(npt + nst,),
        in_specs=[pl.BlockSpec((tm * SUBLANES, LANES), lambda i: (i, 0)),
                  pl.BlockSpec((tm, D_MODEL), lambda i: (i, 0)),
                  pl.BlockSpec((1, MOD_ROWS, D_MODEL),
                               lambda i: (jnp.where(i < npt, 0, 1 + jnp.maximum(i - npt, 0) // spb), 0, 0)),
                  pl.BlockSpec((1, D_MODEL), lambda i: (0, 0))],
        out_specs=(pl.BlockSpec((tm, D_MODEL), lambda i: (jnp.minimum(i, npt - 1), 0)),
                   pl.BlockSpec((tm, D_MODEL), lambda i: (jnp.maximum(i - npt, 0), 0))),
        compiler_params=_cparams(("arbitrary",)),
        name="moe_combine_final",
    )(moe2, x1, mod, final_g.reshape(1, D_MODEL))


def _moe(tok, ee, x1, mod, w_gate, w_up, w_down, final_g, t_prompt, t_lat, seq_s):
    tm = MOE_TM
    n_rows = t_prompt + t_lat
    assert n_rows % tm == 0
    max_items = n_rows // tm + N_CLASSES
    plan = _route_plan(ee, tm, max_items)
    moe = _moe_ffn(plan, tok, w_gate, w_up, w_down, tm, max_items)
    return _final(moe, x1, mod, final_g, t_prompt, t_lat, seq_s)


def kernel(x_prompt, x_sample, cache_k, cache_v, c, c_ctx, w_ada, b_ada, norm1_g, w_in, rpb, conv_w, conv_b, filt_w1, filt_b1, filt_w2, filt_b2, filt_w3, filt_freq, hyena_skip, gnorm_att, gnorm_hyena, w_out, norm2_g, router_grp_w, router_grp_b, router_exp_w, router_exp_b, w_gate, w_up, w_down, final_g):
    depth = w_ada.shape[0]
    assert depth == 1
    batch, seq, _ = x_prompt.shape
    dec_batch, dec_seq, _ = x_sample.shape
    l = 0

    def pack_router(grp, exp):
        rows = grp.shape[0]
        return jnp.concatenate([grp, jnp.zeros((rows, ROUTE_EXP_LANE0 - N_GROUPS), f32), exp,
                                jnp.zeros((rows, ROUTE_LANES - ROUTE_EXP_LANE0 - N_EXPERTS), f32)], axis=1)

    wr = pack_router(router_grp_w[l], router_exp_w[l])
    br = pack_router(router_grp_b[l][None, :], router_exp_b[l][None, :])

    lp = {
        'norm1_g': norm1_g[l], 'w_in': w_in[l], 'conv_w': conv_w[l], 'conv_b': conv_b[l],
        'filt_w1': filt_w1[l], 'filt_b1': filt_b1[l], 'filt_w2': filt_w2[l], 'filt_b2': filt_b2[l],
        'filt_w3': filt_w3[l], 'filt_freq': filt_freq[l], 'hyena_skip': hyena_skip[l],
        'gnorm_hyena': gnorm_hyena[l], 'w_out': w_out[l], 'norm2_g': norm2_g[l],
        'wr': wr, 'br': br, 'w_gate': w_gate[l], 'w_up': w_up[l], 'w_down': w_down[l],
    }

    cond8 = jnp.concatenate([c_ctx[None, :], c, jnp.zeros((SUBLANES - 1 - dec_batch, D_MODEL), f32)], axis=0)
    mod = _ada_mod(cond8, w_ada[l], b_ada[l], 1 + dec_batch).reshape(SUBLANES, N_MOD, D_MODEL)
    mod = jnp.pad(mod, ((0, 0), (0, MOD_ROWS - N_MOD), (0, 0)))
    mod_ctx, mod_lat = mod[0:1], mod[1:1 + dec_batch]

    xp = x_prompt.reshape(batch * seq, D_MODEL)
    xs = x_sample.reshape(dec_batch * dec_seq, D_MODEL)

    qp, k_ctx, v_ctx, hyp = _inproj(xp, mod_ctx, lp['norm1_g'], lp['w_in'], batch * seq, f32)
    attp = _ctx_attention(qp, k_ctx, v_ctx, gnorm_att[l], seq)
    hyop = _hyena(hyp, lp, batch, seq)

    ql, kl, vl, hyl = _inproj(xs, mod_lat, lp['norm1_g'], lp['w_in'], dec_seq, bf16)
    kc = cache_k[:, l].reshape(dec_batch * cache_k.shape[2], D_ATT).astype(bf16)
    vc = cache_v[:, l].reshape(dec_batch * cache_v.shape[2], D_ATT).astype(bf16)
    attl = _na_attention(ql, kl, vl, kc, vc, rpb[l], gnorm_att[l], dec_batch, dec_seq)
    hyol = _hyena(hyl, lp, dec_batch, dec_seq)

    x1, tok, ee = _outproj(xp, xs, attp, attl, hyop, hyol, mod[0:1 + dec_batch], lp['w_out'], lp['norm2_g'],
                           lp['wr'], lp['br'], seq, dec_seq)
    yp, ys = _moe(tok, ee, x1, mod[0:1 + dec_batch], lp['w_gate'], lp['w_up'], lp['w_down'], final_g,
                  batch * seq, dec_batch * dec_seq, dec_seq)

    y_prompt = yp.reshape(batch, seq, D_MODEL)
    y_sample = ys.reshape(dec_batch, dec_seq, D_MODEL)
    new_k = k_ctx.reshape(batch, 1, seq, H_ATT, HEAD_DIM)
    new_v = v_ctx.reshape(batch, 1, seq, H_ATT, HEAD_DIM)
    return (y_prompt, y_sample, new_k, new_v)
```

```python
import functools
import math

import jax
import jax.numpy as jnp
import numpy as np
from jax import lax
from jax.experimental import pallas as pl
from jax.experimental.pallas import tpu as pltpu

f32 = jnp.float32
bf16 = jnp.bfloat16
HIGHEST = lax.Precision.HIGHEST

D_MODEL = 1024
GRID_W = 64
H_ATT = 8
HEAD_DIM = 64
D_ATT = H_ATT * HEAD_DIM
D_HYENA = 512
D_IN = 3 * D_ATT + 3 * D_HYENA
NA_ROWS = 8
NA_COLS = 16
SHORT_CONV = 3
FILTER_BANDS = 16
EMB_DIM = 1 + 2 * FILTER_BANDS
FILTER_FF = 64
DECAY_TARGET = 1e-2
MIN_DECAY = math.log(DECAY_TARGET) / 1.5
MAX_DECAY = math.log(DECAY_TARGET) / 0.3
N_GROUPS = 4
EXPERTS_PER_GROUP = 4
N_EXPERTS = N_GROUPS * EXPERTS_PER_GROUP
D_EXPERT = 512
N_MOD = 6
EPS = 1e-6
NEG_INF = -1e30
ATT_SCALE = HEAD_DIM ** -0.5

LANES = 128
SUBLANES = 8
MOD_ROWS = 8
ROUTE_LANES = 128
ROUTE_EXP_LANE0 = 16
ROUTE_E1, ROUTE_E2, ROUTE_W1, ROUTE_W2 = 0, 1, 2, 3
TOK_PITCH = 16
TOK_RT_ROW = 8
MOE_TM = 512
GATHER_PRIORITIES = (0,)
SCATTER_PRIORITIES = (1,)
VMEM_LIMIT = 56 * 1024 * 1024

SH1, SC1, GT1, SH2, SC2, GT2 = range(6)


def _cparams(sem, vmem=VMEM_LIMIT):
    return pltpu.CompilerParams(dimension_semantics=sem, vmem_limit_bytes=vmem)


def _dot(a, b):
    return jnp.dot(a, b, preferred_element_type=f32)


def _dot_hi(a, b):
    return lax.dot_general(a, b, (((1,), (0,)), ((), ())), precision=HIGHEST,
                           preferred_element_type=f32)


def _dot_nt(a, b):
    return lax.dot_general(a, b, (((1,), (1,)), ((), ())), preferred_element_type=f32)


def _rms(x, g):
    ms = jnp.mean(x * x, axis=-1, keepdims=True)
    return x * lax.rsqrt(ms + EPS) * g


def _cast_rows(src_ref, dst_ref, chunk):
    n = src_ref.shape[0] // chunk

    def body(i, c):
        r = pl.multiple_of(i * chunk, chunk)
        dst_ref[pl.ds(r, chunk), :] = src_ref[pl.ds(r, chunk), :].astype(dst_ref.dtype)
        return c

    lax.fori_loop(0, n, body, 0)


def _ada_kernel(ct_ref, w_ref, b_ref, o_ref, *, n_cond):
    ct = ct_ref[...]
    st = ct * (1.0 / (1.0 + jnp.exp(-ct)))
    w = w_ref[...]
    rid = lax.broadcasted_iota(jnp.int32, o_ref.shape, 0)
    out = jnp.broadcast_to(b_ref[...], o_ref.shape)
    for m in range(n_cond):
        row = jnp.sum(w * st[:, m:m + 1], axis=0, keepdims=True)
        out = out + jnp.where(rid == m, row, 0.0)
    o_ref[...] = out


def _ada_mod(cond8, w_ada, b_ada, n_cond):
    tn = 1536
    n = N_MOD * D_MODEL
    return pl.pallas_call(
        functools.partial(_ada_kernel, n_cond=n_cond),
        out_shape=jax.ShapeDtypeStruct((SUBLANES, n), f32),
        grid=(n // tn,),
        in_specs=[pl.BlockSpec((D_MODEL, SUBLANES), lambda j: (0, 0)),
                  pl.BlockSpec((D_MODEL, tn), lambda j: (0, j)),
                  pl.BlockSpec((1, tn), lambda j: (0, j))],
        out_specs=pl.BlockSpec((SUBLANES, tn), lambda j: (0, j)),
        compiler_params=_cparams(("arbitrary",)),
        name="ada_mod",
    )(cond8.T, w_ada, b_ada.reshape(1, n))


def _inproj_kernel(x_ref, mod_ref, g_ref, w_ref, q_ref, k_ref, v_ref, hy_ref, wbf_ref):
    @pl.when(pl.program_id(0) == 0)
    def _():
        _cast_rows(w_ref, wbf_ref, 128)

    h = _rms(x_ref[...], g_ref[...])
    h = h * (1.0 + mod_ref[0, SC1:SC1 + 1, :]) + mod_ref[0, SH1:SH1 + 1, :]
    p = _dot(h.astype(bf16), wbf_ref[...])
    q_ref[...] = p[:, 0:D_ATT].astype(q_ref.dtype)
    k_ref[...] = p[:, D_ATT:2 * D_ATT].astype(k_ref.dtype)
    v_ref[...] = p[:, 2 * D_ATT:3 * D_ATT].astype(v_ref.dtype)
    hy_ref[...] = p[:, 3 * D_ATT:]


def _inproj(x, mod, norm_g, w_in, rows_per_mod, kv_dtype):
    t = x.shape[0]
    tm = 512
    blocks_per_mod = rows_per_mod // tm
    return pl.pallas_call(
        _inproj_kernel,
        out_shape=(jax.ShapeDtypeStruct((t, D_ATT), bf16),
                   jax.ShapeDtypeStruct((t, D_ATT), kv_dtype),
                   jax.ShapeDtypeStruct((t, D_ATT), kv_dtype),
                   jax.ShapeDtypeStruct((t, 3 * D_HYENA), f32)),
        grid=(t // tm,),
        in_specs=[pl.BlockSpec((tm, D_MODEL), lambda i: (i, 0)),
                  pl.BlockSpec((1, MOD_ROWS, D_MODEL), lambda i: (i // blocks_per_mod, 0, 0)),
                  pl.BlockSpec((1, D_MODEL), lambda i: (0, 0)),
                  pl.BlockSpec((D_MODEL, D_IN), lambda i: (0, 0), pipeline_mode=pl.Buffered(1))],
        out_specs=(pl.BlockSpec((tm, D_ATT), lambda i: (i, 0)),
                   pl.BlockSpec((tm, D_ATT), lambda i: (i, 0)),
                   pl.BlockSpec((tm, D_ATT), lambda i: (i, 0)),
                   pl.BlockSpec((tm, 3 * D_HYENA), lambda i: (i, 0))),
        scratch_shapes=[pltpu.VMEM((D_MODEL, D_IN), bf16)],
        compiler_params=_cparams(("arbitrary",)),
        name="inproj",
    )(x, mod, norm_g.reshape(1, D_MODEL), w_in)


def _split_heads(q2):
    lane = lax.broadcasted_iota(jnp.int32, q2.shape, 1)
    qa = jnp.where(lane < HEAD_DIM, q2, 0.0)
    qb = jnp.where(lane >= HEAD_DIM, q2, 0.0)
    return jnp.concatenate([qa, qb], axis=0)


def _merge_heads(o_ab):
    m = o_ab.shape[0] // 2
    lane = lax.broadcasted_iota(jnp.int32, (m, LANES), 1)
    return jnp.where(lane < HEAD_DIM, o_ab[:m], o_ab[m:])


CTX_SEQS_PER_STEP = 2


def _ctx_attn_kernel(q_ref, k_ref, v_ref, g_ref, o_ref, *, seq):
    for b in range(q_ref.shape[0] // seq):
        rows = slice(b * seq, (b + 1) * seq)
        outs = []
        for p in range(D_ATT // LANES):
            cs = slice(p * LANES, (p + 1) * LANES)
            qq = _split_heads(q_ref[rows, cs] * ATT_SCALE).astype(bf16)
            s = _dot_nt(qq, k_ref[rows, cs].astype(bf16))
            m = jnp.max(s, axis=-1, keepdims=True)
            e = jnp.exp(s - m)
            l = jnp.sum(e, axis=-1, keepdims=True)
            o_ab = _dot(e.astype(bf16), v_ref[rows, cs].astype(bf16)) / l
            outs.append(_merge_heads(o_ab))
        o_ref[rows, :] = _rms(jnp.concatenate(outs, axis=-1), g_ref[...]).astype(o_ref.dtype)


def _ctx_attention(q, k, v, gnorm, seq):
    t = q.shape[0]
    tm = CTX_SEQS_PER_STEP * seq
    assert t % tm == 0
    spec = pl.BlockSpec((tm, D_ATT), lambda b: (b, 0))
    return pl.pallas_call(
        functools.partial(_ctx_attn_kernel, seq=seq),
        out_shape=jax.ShapeDtypeStruct((t, D_ATT), bf16),
        grid=(t // tm,),
        in_specs=[spec, spec, spec, pl.BlockSpec((1, D_ATT), lambda b: (0, 0))],
        out_specs=spec,
        compiler_params=_cparams(("arbitrary",)),
        name="ctx_attn",
    )(q, k, v, gnorm.reshape(1, D_ATT))


def _na_tables():
    col = np.arange(GRID_W)
    cs = np.clip(col - NA_COLS // 2, 0, GRID_W - NA_COLS)
    col_mask = (col[None, :] >= cs[:, None]) & (col[None, :] < cs[:, None] + NA_COLS)
    mask = np.tile(col_mask.astype(np.float32), (1, NA_ROWS))
    return mask


N_DR = 2 * NA_ROWS - 1
N_DC = 2 * NA_COLS - 1
BIAS_PAIRS = N_DR - 1


def _na_bias_rows(rpb):
    out = jnp.zeros((H_ATT, BIAS_PAIRS, LANES), f32)
    out = out.at[:, :, 0:N_DC].set(rpb[:, 0:BIAS_PAIRS])
    return out.at[:, :, GRID_W:GRID_W + N_DC].set(rpb[:, 1:N_DR])


NA_ROWS_PER_STEP = 8


def _na_row_start(r, rows):
    return jnp.clip(r - NA_ROWS // 2, 0, rows - NA_ROWS)


def _na_attn_kernel(q_ref, k_ref, v_ref, kc_ref, vc_ref, rp_ref, mask_ref, g_ref, o_ref,
                    t2_ref, *, rows):
    b = pl.program_id(0)
    r = pl.program_id(1)

    @pl.when((b == 0) & (r == 0))
    def _():
        valid = mask_ref[:, 0:LANES] != 0.0
        for h in range(H_ATT):
            for i in range(BIAS_PAIRS):
                v = jnp.broadcast_to(rp_ref[h, i:i + 1, :], (GRID_W, LANES))
                v = pltpu.roll(v, LANES - (NA_COLS - 1), 1, stride=1, stride_axis=0)
                t2_ref[h * BIAS_PAIRS + i] = jnp.where(valid, v, NEG_INF)

    nwin = NA_ROWS * GRID_W
    for rr in range(NA_ROWS_PER_STEP):
        row = r * NA_ROWS_PER_STEP + rr
        qs = slice(rr * GRID_W, (rr + 1) * GRID_W)
        rs = _na_row_start(row, rows)
        start = pl.multiple_of(rs * GRID_W, GRID_W)
        i0 = rs - row + NA_ROWS - 1
        outs = []
        for p in range(D_ATT // LANES):
            cs = slice(p * LANES, (p + 1) * LANES)
            qq = _split_heads(q_ref[qs, cs] * ATT_SCALE).astype(bf16)
            kw = k_ref[pl.ds(start, nwin), cs]
            vw = v_ref[pl.ds(start, nwin), cs]
            s_lat = _dot_nt(qq, kw)
            s_ctx = _dot_nt(qq, kc_ref[:, cs])
            bias2 = jnp.concatenate(
                [jnp.concatenate([t2_ref[(2 * p + hh) * BIAS_PAIRS + i0 + 2 * jp] for hh in range(2)], axis=0)
                 for jp in range(NA_ROWS // 2)], axis=-1)
            s_lat = s_lat + bias2
            m = jnp.maximum(jnp.max(s_lat, axis=-1, keepdims=True), jnp.max(s_ctx, axis=-1, keepdims=True))
            e_lat = jnp.exp(s_lat - m)
            e_ctx = jnp.exp(s_ctx - m)
            l = jnp.sum(e_lat, axis=-1, keepdims=True) + jnp.sum(e_ctx, axis=-1, keepdims=True)
            o_ab = (_dot(e_lat.astype(bf16), vw) + _dot(e_ctx.astype(bf16), vc_ref[:, cs])) / l
            outs.append(_merge_heads(o_ab))
        o_ref[qs, :] = _rms(jnp.concatenate(outs, axis=-1), g_ref[...]).astype(o_ref.dtype)


def _na_attention(q, k, v, kc, vc, rpb, gnorm, nb, seq):
    rows = seq // GRID_W
    assert rows % NA_ROWS_PER_STEP == 0
    steps = rows // NA_ROWS_PER_STEP
    qrows = NA_ROWS_PER_STEP * GRID_W
    past = kc.shape[0] // nb
    mask = _na_tables()
    return pl.pallas_call(
        functools.partial(_na_attn_kernel, rows=rows),
        out_shape=jax.ShapeDtypeStruct((nb * seq, D_ATT), bf16),
        grid=(nb, steps),
        in_specs=[pl.BlockSpec((qrows, D_ATT), lambda b, r: (b * steps + r, 0)),
                  pl.BlockSpec((seq, D_ATT), lambda b, r: (b, 0)),
                  pl.BlockSpec((seq, D_ATT), lambda b, r: (b, 0)),
                  pl.BlockSpec((past, D_ATT), lambda b, r: (b, 0)),
                  pl.BlockSpec((past, D_ATT), lambda b, r: (b, 0)),
                  pl.BlockSpec((H_ATT, BIAS_PAIRS, LANES), lambda b, r: (0, 0, 0)),
                  pl.BlockSpec((GRID_W, NA_ROWS * GRID_W), lambda b, r: (0, 0)),
                  pl.BlockSpec((1, D_ATT), lambda b, r: (0, 0))],
        out_specs=pl.BlockSpec((qrows, D_ATT), lambda b, r: (b * steps + r, 0)),
        scratch_shapes=[pltpu.VMEM((H_ATT * BIAS_PAIRS, GRID_W, LANES), f32)],
        compiler_params=_cparams(("arbitrary", "arbitrary")),
        name="na_attn",
    )(q, k, v, kc, vc, _na_bias_rows(rpb), jnp.asarray(mask), gnorm.reshape(1, D_ATT))


def _hy_front_kernel(x0_ref, x1_ref, v_ref, w0_ref, w1_ref, wv_ref, b0_ref, b1_ref, bv_ref,
                     zbf_ref, z_ref, x0c_ref):
    seq = x0_ref.shape[0]
    row = lax.broadcasted_iota(jnp.int32, x0_ref.shape, 0)
    first = row == 0
    last = row == seq - 1

    def conv(u_ref, w_ref, b_ref):
        u = u_ref[...]
        up = jnp.where(first, 0.0, pltpu.roll(u, 1, 0))
        un = jnp.where(last, 0.0, pltpu.roll(u, seq - 1, 0))
        y = b_ref[...] + up * w_ref[0:1, :]
        y = y + u * w_ref[1:2, :]
        return y + un * w_ref[2:3, :]

    z = conv(v_ref, wv_ref, bv_ref) * conv(x1_ref, w1_ref, b1_ref)
    z_ref[...] = z
    zbf_ref[...] = z.astype(bf16)
    x0c_ref[...] = conv(x0_ref, w0_ref, b0_ref)


def _hy_front(hy, conv_w, conv_b, nb, seq):
    tc = D_HYENA if seq <= 512 else D_HYENA // 2
    nc = D_HYENA // tc
    n = nb * D_HYENA
    cb = conv_b.reshape(1, 3 * D_HYENA)

    def part(k):
        return (pl.BlockSpec((seq, tc), lambda b, j: (b, k * nc + j)),
                pl.BlockSpec((SHORT_CONV, tc), lambda b, j: (0, k * nc + j)),
                pl.BlockSpec((1, tc), lambda b, j: (0, k * nc + j)))

    (x0s, w0s, b0s), (x1s, w1s, b1s), (vs, wvs, bvs) = part(0), part(1), part(2)
    ospec = pl.BlockSpec((seq, tc), lambda b, j: (0, b * nc + j))
    return pl.pallas_call(
        _hy_front_kernel,
        out_shape=(jax.ShapeDtypeStruct((seq, n), bf16),
                   jax.ShapeDtypeStruct((seq, n), f32),
                   jax.ShapeDtypeStruct((seq, n), f32)),
        grid=(nb, nc),
        in_specs=[x0s, x1s, vs, w0s, w1s, wvs, b0s, b1s, bvs],
        out_specs=(ospec, ospec, ospec),
        compiler_params=_cparams(("arbitrary", "arbitrary")),
        name="hyena_front",
    )(hy, hy, hy, conv_w, conv_w, conv_w, cb, cb, cb)


def _filter_features(seq):
    t = np.linspace(0.0, 1.0, seq, dtype=np.float64)[:, None]
    w = 2.0 * math.pi * np.arange(seq, dtype=np.float64)[:, None] / seq
    fb = np.linspace(1e-4, FILTER_BANDS - 1, FILTER_BANDS, dtype=np.float64)[None, :]
    ang = fb * w
    z = np.concatenate([t, np.cos(ang), -np.sin(ang)], axis=-1).astype(np.float32)
    return np.pad(z, ((0, 0), (0, LANES - EMB_DIM)))


def _filt_kernel(zt_ref, w1t_ref, b1_ref, fr_ref, w2t_ref, b2_ref, w3_ref, dl_ref, h_ref, kl_ref, *, seq):
    i = pl.program_id(0)
    tr = zt_ref.shape[1]
    fr = fr_ref[...]
    h = jnp.sin(fr * (_dot_hi(w1t_ref[...], zt_ref[...]) + b1_ref[...]))
    h = jnp.sin(fr * (_dot_hi(w2t_ref[...], h) + b2_ref[...]))
    h = _dot_hi(h.T, w3_ref[...])
    row = lax.broadcasted_iota(jnp.int32, (tr, D_HYENA), 0) + i * tr
    t = row[:, 0:1].astype(f32) * (1.0 / (seq - 1))
    decay = jnp.exp(-t * dl_ref[...])
    hf = h[:, :D_HYENA] * decay
    hb = jnp.where(row == 0, 0.0, h[:, D_HYENA:] * decay)
    h_ref[:, :D_HYENA] = (hf + hb).astype(bf16)
    h_ref[:, D_HYENA:] = (hb - hf).astype(bf16)
    alt = (1 - 2 * (row & 1)).astype(f32)
    part = jnp.sum(alt * (hf + hb), axis=0, keepdims=True)

    @pl.when(i == 0)
    def _():
        kl_ref[...] = jnp.zeros_like(kl_ref)

    kl_ref[...] += jnp.broadcast_to(part, kl_ref.shape)


def _hy_filters(seq, w1, b1, w2, b2, w3, freq):
    tr = 256
    zt = jnp.asarray(np.ascontiguousarray(_filter_features(seq).T))
    deltas = np.abs(np.linspace(MIN_DECAY, MAX_DECAY, D_HYENA, dtype=np.float64))[None, :].astype(np.float32)
    w1t = jnp.pad(w1, ((0, LANES - EMB_DIM), (0, 0))).T
    const = lambda shape: pl.BlockSpec(shape, lambda i: (0, 0))
    col = lambda v: v.reshape(FILTER_FF, 1)
    return pl.pallas_call(
        functools.partial(_filt_kernel, seq=seq),
        out_shape=(jax.ShapeDtypeStruct((seq, 2 * D_HYENA), bf16),
                   jax.ShapeDtypeStruct((SUBLANES, D_HYENA), f32)),
        grid=(seq // tr,),
        in_specs=[pl.BlockSpec((LANES, tr), lambda i: (0, i)),
                  const((FILTER_FF, LANES)), const((FILTER_FF, 1)), const((FILTER_FF, 1)),
                  const((FILTER_FF, FILTER_FF)), const((FILTER_FF, 1)),
                  const((FILTER_FF, 2 * D_HYENA)), const((1, D_HYENA))],
        out_specs=(pl.BlockSpec((tr, 2 * D_HYENA), lambda i: (i, 0)),
                   pl.BlockSpec((SUBLANES, D_HYENA), lambda i: (0, 0))),
        compiler_params=_cparams(("arbitrary",)),
        name="hyena_filters",
    )(zt, w1t, col(b1), col(freq), w2.T, col(b2), w3, jnp.asarray(deltas))


def _dft_mats(seq):
    n = 2 * seq
    ph = (np.arange(seq, dtype=np.int64)[:, None] * np.arange(seq, dtype=np.int64)[None, :]) % n
    ang = ph.astype(np.float64) * (2.0 * math.pi / n)
    return np.cos(ang).astype(np.float32), np.sin(ang).astype(np.float32)


def _alt_col(rows, offset):
    row = lax.broadcasted_iota(jnp.int32, (rows, 1), 0) + offset
    return (1 - 2 * (row & 1)).astype(f32)


def _hy_fwd_kernel(fr_ref, fi_ref, z_ref, h_ref, kl_ref, yr_ref, yi_ref, yl_ref, kr_s, ki_s, *, n):
    i = pl.program_id(0)
    j = pl.program_id(1)
    tf = fr_ref.shape[0]
    tn = z_ref.shape[1]
    frb = fr_ref[...].astype(bf16)
    fib = fi_ref[...].astype(bf16)

    @pl.when(j == 0)
    def _():
        f = lax.broadcasted_iota(jnp.int32, (tf, 1), 0) + i * tf
        cf = jnp.where(f == 0, 1.0 / n, 2.0 / n)
        kr_s[...] = _dot(frb, h_ref[:, :D_HYENA]) * cf
        ki_s[...] = _dot(fib, h_ref[:, D_HYENA:]) * cf

    a = _dot(frb, z_ref[...])
    b = _dot(fib, z_ref[...])
    kr = kr_s[...]
    ki = ki_s[...]
    for c in range(tn // D_HYENA):
        cs = slice(c * D_HYENA, (c + 1) * D_HYENA)
        yr_ref[:, cs] = (a[:, cs] * kr + b[:, cs] * ki).astype(bf16)
        yi_ref[:, cs] = (b[:, cs] * kr - a[:, cs] * ki).astype(bf16)

    @pl.when(i == 0)
    def _():
        alt = _alt_col(z_ref.shape[0], 0)
        nz = jnp.sum(z_ref[...].astype(f32) * alt, axis=0, keepdims=True)
        kl = jnp.concatenate([kl_ref[0:1, :]] * (tn // D_HYENA), axis=-1)
        yl_ref[...] = jnp.broadcast_to(nz * kl * (1.0 / n), yl_ref.shape)


def _hy_fwd(fr, fi, zbf, hcat, kl, seq):
    n_cols = zbf.shape[1]
    tf = min(seq, 512)
    tn = min(n_cols, 2048)
    ni, nj = seq // tf, n_cols // tn
    assert ni == 1 or nj == 1
    return pl.pallas_call(
        functools.partial(_hy_fwd_kernel, n=2 * seq),
        out_shape=(jax.ShapeDtypeStruct((seq, n_cols), bf16),
                   jax.ShapeDtypeStruct((seq, n_cols), bf16),
                   jax.ShapeDtypeStruct((SUBLANES, n_cols), f32)),
        grid=(ni, nj),
        in_specs=[pl.BlockSpec((tf, seq), lambda i, j: (i, 0)),
                  pl.BlockSpec((tf, seq), lambda i, j: (i, 0)),
                  pl.BlockSpec((seq, tn), lambda i, j: (0, j)),
                  pl.BlockSpec((seq, 2 * D_HYENA), lambda i, j: (0, 0)),
                  pl.BlockSpec((SUBLANES, D_HYENA), lambda i, j: (0, 0))],
        out_specs=(pl.BlockSpec((tf, tn), lambda i, j: (i, j)),
                   pl.BlockSpec((tf, tn), lambda i, j: (i, j)),
                   pl.BlockSpec((SUBLANES, tn), lambda i, j: (0, j))),
        scratch_shapes=[pltpu.VMEM((tf, D_HYENA), f32), pltpu.VMEM((tf, D_HYENA), f32)],
        compiler_params=_cparams(("arbitrary", "arbitrary")),
        name="hyena_dft_fwd",
    )(fr, fi, zbf, hcat, kl)


def _hy_inv_kernel(fr_ref, fi_ref, yr_ref, yi_ref, yl_ref, z_ref, x0_ref, skip_ref, g_ref, o_ref):
    tt = fr_ref.shape[0]
    tn = yr_ref.shape[1]
    y = _dot(fr_ref[...].astype(bf16), yr_ref[...]) + _dot(fi_ref[...].astype(bf16), yi_ref[...])
    alt = _alt_col(tt, pl.program_id(0) * tt)
    for c in range(tn // D_HYENA):
        cs = slice(c * D_HYENA, (c + 1) * D_HYENA)
        yc = y[:, cs] + alt * yl_ref[0:1, cs] + z_ref[:, cs] * skip_ref[...]
        o_ref[:, cs] = _rms(yc * x0_ref[:, cs], g_ref[...]).astype(o_ref.dtype)


def _hy_inv(fr, fi, yr, yi, yl, z, x0c, skip, gnorm, seq):
    n_cols = z.shape[1]
    tt = min(seq, 512)
    tn = min(n_cols, 2048)
    blk = pl.BlockSpec((tt, tn), lambda i, j: (i, j))
    return pl.pallas_call(
        _hy_inv_kernel,
        out_shape=jax.ShapeDtypeStruct((seq, n_cols), bf16),
        grid=(seq // tt, n_cols // tn),
        in_specs=[pl.BlockSpec((tt, seq), lambda i, j: (i, 0)),
                  pl.BlockSpec((tt, seq), lambda i, j: (i, 0)),
                  pl.BlockSpec((seq, tn), lambda i, j: (0, j)),
                  pl.BlockSpec((seq, tn), lambda i, j: (0, j)),
                  pl.BlockSpec((SUBLANES, tn), lambda i, j: (0, j)),
                  blk, blk,
                  pl.BlockSpec((1, D_HYENA), lambda i, j: (0, 0)),
                  pl.BlockSpec((1, D_HYENA), lambda i, j: (0, 0))],
        out_specs=blk,
        compiler_params=_cparams(("arbitrary", "arbitrary")),
        name="hyena_dft_inv",
    )(fr, fi, yr, yi, yl, z, x0c, skip.reshape(1, D_HYENA), gnorm.reshape(1, D_HYENA))


def _hyena(hy, lp, nb, seq):
    fr_np, fi_np = _dft_mats(seq)
    fr, fi = jnp.asarray(fr_np), jnp.asarray(fi_np)
    zbf, z, x0c = _hy_front(hy, lp['conv_w'], lp['conv_b'], nb, seq)
    hcat, kl = _hy_filters(seq, lp['filt_w1'], lp['filt_b1'], lp['filt_w2'], lp['filt_b2'],
                           lp['filt_w3'], lp['filt_freq'])
    yr, yi, yl = _hy_fwd(fr, fi, zbf, hcat, kl, seq)
    return _hy_inv(fr, fi, yr, yi, yl, z, x0c, lp['hyena_skip'], lp['gnorm_hyena'], seq)


def _store_token_tiles(ref, x, pitch=SUBLANES):
    m = x.shape[0]
    for c in range(D_MODEL // LANES):
        ref[pl.ds(c, m, stride=pitch), :] = x[:, c * LANES:(c + 1) * LANES]


def _load_token_tiles(ref, m, lead=(), pitch=SUBLANES):
    return jnp.concatenate([ref[lead + (pl.ds(c, m, stride=pitch), slice(None))]
                            for c in range(D_MODEL // LANES)], axis=-1)


def _route(logits):
    lane_i = lax.broadcasted_iota(jnp.int32, logits.shape, 1)
    lane = lane_i.astype(f32)
    big = float(ROUTE_LANES)
    is_g = lane_i < N_GROUPS
    mg = jnp.max(jnp.where(is_g, logits, -jnp.inf), axis=-1, keepdims=True)
    sg = jnp.sum(jnp.where(is_g, jnp.exp(logits - mg), 0.0), axis=-1, keepdims=True)
    g_w = 1.0 / sg
    g_idx = jnp.min(jnp.where(is_g & (logits == mg), lane, big), axis=-1, keepdims=True)
    e_id = lane_i - ROUTE_EXP_LANE0
    sel = (e_id >= 0) & (e_id < N_EXPERTS) & ((e_id >> 2).astype(f32) == g_idx)
    me = jnp.max(jnp.where(sel, logits, -jnp.inf), axis=-1, keepdims=True)
    ee = jnp.where(sel, jnp.exp(logits - me), 0.0)
    prob = ee / jnp.sum(ee, axis=-1, keepdims=True)
    p1 = jnp.max(jnp.where(sel, prob, -1.0), axis=-1, keepdims=True)
    i1 = jnp.min(jnp.where(sel & (prob == p1), lane, big), axis=-1, keepdims=True)
    sel2 = sel & (lane != i1)
    p2 = jnp.max(jnp.where(sel2, prob, -1.0), axis=-1, keepdims=True)
    i2 = jnp.min(jnp.where(sel2 & (prob == p2), lane, big), axis=-1, keepdims=True)
    tot = p1 + p2
    rec = jnp.where(lane_i == ROUTE_E1, i1 - ROUTE_EXP_LANE0, 0.0)
    rec = jnp.where(lane_i == ROUTE_E2, i2 - ROUTE_EXP_LANE0, rec)
    rec = jnp.where(lane_i == ROUTE_W1, g_w * (p1 / tot), rec)
    return jnp.where(lane_i == ROUTE_W2, g_w * (p2 / tot), rec)


def _outproj_kernel(xp_ref, xs_ref, attp_ref, atts_ref, hyp_ref, hys_ref, mod_ref, wo_ref, g2_ref,
                    wr_ref, br_ref, x1_ref, tok_ref, ee_ref, wobf_ref, wrh_ref, wrl_ref, *, prompt_tiles):
    i = pl.program_id(0)

    @pl.when(i == 0)
    def _():
        _cast_rows(wo_ref, wobf_ref, 128)
        wr = wr_ref[...]
        hi = wr.astype(bf16)
        wrh_ref[...] = hi
        wrl_ref[...] = (wr - hi.astype(f32)).astype(bf16)

    is_p = i < prompt_tiles
    hyp = jnp.concatenate([hyp_ref[:, b * D_HYENA:(b + 1) * D_HYENA]
                           for b in range(hyp_ref.shape[1] // D_HYENA)], axis=0)
    x = jnp.where(is_p, xp_ref[...], xs_ref[...])
    att = jnp.where(is_p, attp_ref[...], atts_ref[...])
    hyo = jnp.where(is_p, hyp, hys_ref[...])
    proj = _dot(att, wobf_ref[0:D_ATT, :]) + _dot(hyo, wobf_ref[D_ATT:, :])
    x1 = x + mod_ref[0, GT1:GT1 + 1, :] * proj
    x1_ref[...] = x1
    h2 = _rms(x1, g2_ref[...]) * (1.0 + mod_ref[0, SC2:SC2 + 1, :]) + mod_ref[0, SH2:SH2 + 1, :]
    h2h = h2.astype(bf16)
    h2l = (h2 - h2h.astype(f32)).astype(bf16)
    logits = _dot(h2h, wrh_ref[...]) + _dot(h2l, wrh_ref[...]) + _dot(h2h, wrl_ref[...]) + br_ref[...]
    rt = _route(logits)
    tm = h2.shape[0]
    per_row = tm // LANES
    trow = lax.broadcasted_iota(jnp.int32, (tm, LANES), 0)
    diag = (trow & (LANES - 1)) == lax.broadcasted_iota(jnp.int32, (tm, LANES), 1)
    qrow = lax.broadcasted_iota(jnp.int32, (2 * per_row, tm), 0)
    tblk = lax.shift_right_logical(lax.broadcasted_iota(jnp.int32, (2 * per_row, tm), 1), LANES.bit_length() - 1)
    ee = jnp.zeros((2 * per_row, LANES), f32)
    for j, lane0 in enumerate((ROUTE_E1, ROUTE_E2)):
        spread = jnp.where(diag, rt[:, lane0:lane0 + 1], 0.0).astype(bf16)
        pick_rows = (qrow == tblk + j * per_row).astype(bf16)
        ee = ee + _dot(pick_rows, spread)
    ee_ref[0] = ee
    tok_ref[...] = jnp.zeros_like(tok_ref)
    _store_token_tiles(tok_ref, h2, TOK_PITCH)
    tok_ref[pl.ds(TOK_RT_ROW, h2.shape[0], stride=TOK_PITCH), :] = rt


def _outproj(xp, xs, attp, atts, hyp, hys, mod, w_out, norm2_g, wr, br, seq_p, seq_s):
    tm = 512
    tp, ts = xp.shape[0], xs.shape[0]
    npt, nst = tp // tm, ts // tm
    assert tm % seq_p == 0 and seq_s % tm == 0
    spb = seq_s // tm
    bpt = tm // seq_p
    p_idx = lambda i: jnp.minimum(i, npt - 1)
    s_idx = lambda i: jnp.maximum(i - npt, 0)
    const = lambda shape: pl.BlockSpec(shape, lambda i: (0,) * len(shape))
    return pl.pallas_call(
        functools.partial(_outproj_kernel, prompt_tiles=npt),
        out_shape=(jax.ShapeDtypeStruct((tp + ts, D_MODEL), f32),
                   jax.ShapeDtypeStruct(((tp + ts) * TOK_PITCH, LANES), f32),
                   jax.ShapeDtypeStruct((npt + nst, 2 * tm // LANES, LANES), f32)),
        grid=(npt + nst,),
        in_specs=[pl.BlockSpec((tm, D_MODEL), lambda i: (p_idx(i), 0)),
                  pl.BlockSpec((tm, D_MODEL), lambda i: (s_idx(i), 0)),
                  pl.BlockSpec((tm, D_ATT), lambda i: (p_idx(i), 0)),
                  pl.BlockSpec((tm, D_ATT), lambda i: (s_idx(i), 0)),
                  pl.BlockSpec((seq_p, bpt * D_HYENA), lambda i: (0, p_idx(i))),
                  pl.BlockSpec((tm, D_HYENA), lambda i: (s_idx(i) % spb, s_idx(i) // spb)),
                  pl.BlockSpec((1, MOD_ROWS, D_MODEL),
                               lambda i: (jnp.where(i < npt, 0, 1 + s_idx(i) // spb), 0, 0)),
                  const((D_MODEL, D_MODEL)), const((1, D_MODEL)),
                  const((D_MODEL, ROUTE_LANES)), const((1, ROUTE_LANES))],
        out_specs=(pl.BlockSpec((tm, D_MODEL), lambda i: (i, 0)),
                   pl.BlockSpec((tm * TOK_PITCH, LANES), lambda i: (i, 0)),
                   pl.BlockSpec((1, 2 * tm // LANES, LANES), lambda i: (i, 0, 0))),
        scratch_shapes=[pltpu.VMEM((D_MODEL, D_MODEL), bf16),
                        pltpu.VMEM((D_MODEL, ROUTE_LANES), bf16), pltpu.VMEM((D_MODEL, ROUTE_LANES), bf16)],
        compiler_params=_cparams(("arbitrary",)),
        name="outproj_router",
    )(xp, xs, attp, atts, hyp, hys, mod, w_out, norm2_g.reshape(1, D_MODEL), wr, br)


PAIRS_PER_GROUP = 6
N_CLASSES = N_GROUPS * PAIRS_PER_GROUP
PAIR_SLOT_A = (0, 0, 0, 1, 1, 3)
PAIR_SLOT_B = (1, 2, 3, 3, 2, 2)
FLAG_NEW_A, FLAG_NEW_B, FLAG_TILE_START, FLAG_TILE_END = 1, 2, 4, 8


CLASS_ROWS = 32
TAB_TILE, TAB_EA, TAB_EB, TAB_LO, TAB_HI, TAB_FLAGS, TAB_N = range(7)


def _select_by(idx, values):
    out = jnp.full(idx.shape, float(values[-1]), f32)
    for i in range(len(values) - 2, -1, -1):
        out = jnp.where(idx == i, float(values[i]), out)
    return out


def _plan_kernel(e1_ref, e2_ref, pos_ref, tab_ref, *, tm):
    e1 = e1_ref[...]
    e2 = e2_ref[...]
    rows = e1.shape[0]
    grp = jnp.floor(e1 * (1.0 / EXPERTS_PER_GROUP))
    l1 = e1 - EXPERTS_PER_GROUP * grp
    l2 = e2 - EXPERTS_PER_GROUP * jnp.floor(e2 * (1.0 / EXPERTS_PER_GROUP))
    lo, hi = jnp.minimum(l1, l2), jnp.maximum(l1, l2)
    pair = jnp.where(lo == 0, hi - 1, jnp.where(lo == 1, jnp.where(hi == 3, 3.0, 4.0), 5.0))
    cls = grp * PAIRS_PER_GROUP + pair

    ri = lax.broadcasted_iota(jnp.int32, (LANES, LANES), 0)
    ci = lax.broadcasted_iota(jnp.int32, (LANES, LANES), 1)
    upper = (ri <= ci).astype(bf16)
    rr = lax.broadcasted_iota(jnp.int32, (rows, rows), 0)
    rc = lax.broadcasted_iota(jnp.int32, (rows, rows), 1)
    strict_lower = (rc < rr).astype(bf16)
    cid = lax.broadcasted_iota(jnp.int32, (CLASS_ROWS, 1), 0)

    pos = jnp.zeros(e1.shape, f32)
    base = jnp.zeros((1, 1), f32)
    cnt_col = jnp.zeros((CLASS_ROWS, 1), f32)
    off_col = jnp.zeros((CLASS_ROWS, 1), f32)
    for c in range(N_CLASSES):
        m = cls == c
        within = _dot(m.astype(bf16), upper)
        tot = jnp.broadcast_to(within[:, LANES - 1:LANES], within.shape)
        before = _dot(strict_lower, tot.astype(bf16))[:, 0:1]
        count = jnp.sum(within[:, LANES - 1:LANES], axis=0, keepdims=True)
        pos = jnp.where(m, base + before + within - 1.0, pos)
        cnt_col = jnp.where(cid == c, count, cnt_col)
        off_col = jnp.where(cid == c, base, off_col)
        base = base + count
    pos_ref[...] = pos.astype(jnp.int32)

    end_col = off_col + cnt_col
    inv_tm = 1.0 / tm
    first_col = jnp.floor(off_col * inv_tm)
    ntl_col = jnp.where(cnt_col > 0, jnp.floor((end_col - 1.0) * inv_tm) - first_col + 1.0, 0.0)
    cr = lax.broadcasted_iota(jnp.int32, (CLASS_ROWS, CLASS_ROWS), 0)
    cc = lax.broadcasted_iota(jnp.int32, (CLASS_ROWS, CLASS_ROWS), 1)
    lower = (cc <= cr).astype(bf16)
    iend_col = _dot(lower, jnp.broadcast_to(ntl_col, (CLASS_ROWS, LANES)).astype(bf16))[:, 0:1]
    istart_col = iend_col - ntl_col
    n_items = jnp.sum(ntl_col, axis=0, keepdims=True)
    slots = lax.broadcasted_iota(jnp.int32, (1, LANES), 1).astype(f32)
    items = jnp.minimum(slots, n_items - 1.0)
    past = jnp.where((items >= iend_col) & (cid < N_CLASSES), 1.0, 0.0)
    it_cls = jnp.minimum(jnp.sum(past, axis=0, keepdims=True), N_CLASSES - 1.0)
    sel = it_cls == cid.astype(f32)
    pick = lambda col: jnp.sum(jnp.where(sel, col, 0.0), axis=0, keepdims=True)
    grp_col = jnp.floor(cid.astype(f32) * (1.0 / PAIRS_PER_GROUP))
    pair_col = cid.astype(f32) - PAIRS_PER_GROUP * grp_col
    ea_col = EXPERTS_PER_GROUP * grp_col + _select_by(pair_col, PAIR_SLOT_A)
    eb_col = EXPERTS_PER_GROUP * grp_col + _select_by(pair_col, PAIR_SLOT_B)
    it_tile = pick(first_col) + items - pick(istart_col)
    it_lo = jnp.maximum(pick(off_col) - it_tile * tm, 0.0)
    it_hi = jnp.minimum(pick(end_col) - it_tile * tm, float(tm))
    it_ea, it_eb = pick(ea_col), pick(eb_col)
    prev = lambda v: pltpu.roll(jnp.broadcast_to(v, (SUBLANES, LANES)), 1, 1)[0:1]
    changed = lambda v: jnp.where((slots == 0) | (v != prev(v)), 1.0, 0.0)
    flags = (FLAG_NEW_A * changed(it_ea) + FLAG_NEW_B * changed(it_eb)
             + FLAG_TILE_START * jnp.where(it_lo == 0, 1.0, 0.0) + FLAG_TILE_END * jnp.where(it_hi == tm, 1.0, 0.0))
    table = {TAB_TILE: it_tile, TAB_EA: it_ea, TAB_EB: it_eb, TAB_LO: it_lo, TAB_HI: it_hi, TAB_FLAGS: flags,
             TAB_N: jnp.broadcast_to(n_items, (1, LANES))}
    trow = lax.broadcasted_iota(jnp.int32, tab_ref.shape, 0)
    tab = jnp.zeros(tab_ref.shape, f32)
    for r, v in table.items():
        tab = jnp.where(trow == r, v, tab)
    tab_ref[...] = tab.astype(jnp.int32)


def _route_plan(ee, tm, max_items):
    per = ee.shape[1] // 2
    t = ee.shape[0] * per * LANES
    assert max_items <= LANES and tm & (tm - 1) == 0
    pos, tab = pl.pallas_call(
        functools.partial(_plan_kernel, tm=tm),
        out_shape=(jax.ShapeDtypeStruct((t // LANES, LANES), jnp.int32),
                   jax.ShapeDtypeStruct((SUBLANES, LANES), jnp.int32)),
        name="moe_plan",
    )(ee[:, :per].reshape(t // LANES, LANES), ee[:, per:].reshape(t // LANES, LANES))
    row = lambda r: tab[r, :max_items]
    return dict(pos=pos.reshape(t), n_items=tab[TAB_N, :1], it_tile=row(TAB_TILE), it_ea=row(TAB_EA),
                it_eb=row(TAB_EB), it_lo=row(TAB_LO), it_hi=row(TAB_HI), it_flags=row(TAB_FLAGS))


ROW_DMA_UNROLL = 8


def _row_dma_loop(g0, g1, make_copy, priorities):
    def body(i, c):
        for u in range(ROW_DMA_UNROLL):
            make_copy(i * ROW_DMA_UNROLL + u).start(priority=priorities[u % len(priorities)])
        return c

    lax.fori_loop(g0, g1, body, 0)


def _moe_ffn_kernel(pos_ref, it_tile, it_ea, it_eb, it_lo, it_hi, it_flags, n_items, tok_hbm,
                    wga_ref, wua_ref, wda_ref, wgb_ref, wub_ref, wdb_ref, out_hbm,
                    src_s, xbuf, ybuf, x_s, rt_s, acc_s, wa_s, wb_s, sem_in, sem_out, *, chunk, n_tiles):
    i = pl.program_id(0)
    tm = x_s.shape[0]
    groups = tm // ROW_DMA_UNROLL

    def gather_tile(k):
        base = k * tm
        slot = k % 2
        _row_dma_loop(0, groups, lambda r: pltpu.make_async_copy(
            tok_hbm.at[src_s[base + r]],
            xbuf.at[slot, pl.ds(pl.multiple_of(r * TOK_PITCH, TOK_PITCH), TOK_PITCH), :],
            sem_in.at[slot]), GATHER_PRIORITIES)

    def scatter_tile(k):
        base = k * tm
        slot = k % 2
        _row_dma_loop(0, groups, lambda r: pltpu.make_async_copy(
            ybuf.at[slot, pl.ds(pl.multiple_of(r * SUBLANES, SUBLANES), SUBLANES), :],
            out_hbm.at[src_s[base + r]],
            sem_out.at[slot]), SCATTER_PRIORITIES)

    def expert_ffn(x, w, wg, wu, wd):
        g = _dot(x, wg[...])
        u = _dot(x, wu[...])
        hid = (g * (1.0 / (1.0 + jnp.exp(-g)))) * u
        return _dot((hid * w).astype(bf16), wd[...])

    def wait_all(buf, sem, slot):
        pltpu.make_async_copy(buf.at[slot], buf.at[slot], sem.at[slot]).wait()

    @pl.when(i == 0)
    def _():
        def inv(a, c):
            for u in range(ROW_DMA_UNROLL):
                src_s[pos_ref[a * ROW_DMA_UNROLL + u]] = a * ROW_DMA_UNROLL + u
            return c

        lax.fori_loop(0, pos_ref.shape[0] // ROW_DMA_UNROLL, inv, 0)
        gather_tile(0)

    @pl.when(i < n_items[0])
    def _():
        flags = it_flags[i]
        k = it_tile[i]

        @pl.when((flags & FLAG_NEW_A) != 0)
        def _():
            for dst, src in zip(wa_s, (wga_ref, wua_ref, wda_ref)):
                dst[...] = src[0].astype(bf16)

        @pl.when((flags & FLAG_NEW_B) != 0)
        def _():
            for dst, src in zip(wb_s, (wgb_ref, wub_ref, wdb_ref)):
                dst[...] = src[0].astype(bf16)

        @pl.when((flags & FLAG_TILE_START) != 0)
        def _():
            @pl.when(k + 1 < n_tiles)
            def _():
                gather_tile(k + 1)

            wait_all(xbuf, sem_in, k % 2)
            x_s[...] = _load_token_tiles(xbuf, tm, (k % 2,), TOK_PITCH).astype(bf16)
            rt_s[...] = xbuf[k % 2, pl.ds(TOK_RT_ROW, tm, stride=TOK_PITCH), :]
            acc_s[...] = jnp.zeros_like(acc_s)

        lo = it_lo[i]
        hi = it_hi[i]
        ea = it_ea[i].astype(f32)

        def body(j, c):
            r = pl.multiple_of(j * chunk, chunk)

            @pl.when((r < hi) & (r + chunk > lo))
            def _():
                x = x_s[pl.ds(r, chunk), :]
                rt = rt_s[pl.ds(r, chunk), :]
                first_is_a = rt[:, ROUTE_E1:ROUTE_E1 + 1] == ea
                w1 = rt[:, ROUTE_W1:ROUTE_W1 + 1]
                w2 = rt[:, ROUTE_W2:ROUTE_W2 + 1]
                y = (expert_ffn(x, jnp.where(first_is_a, w1, w2), *wa_s)
                     + expert_ffn(x, jnp.where(first_is_a, w2, w1), *wb_s))
                row = lax.broadcasted_iota(jnp.int32, (chunk, 1), 0) + r
                mine = (row >= lo) & (row < hi)
                acc_s[pl.ds(r, chunk), :] = jnp.where(mine, y, acc_s[pl.ds(r, chunk), :])

            return c

        lax.fori_loop(0, tm // chunk, body, 0)

        @pl.when((flags & FLAG_TILE_END) != 0)
        def _():
            @pl.when(k >= 2)
            def _():
                wait_all(ybuf, sem_out, k % 2)

            _store_token_tiles(ybuf.at[k % 2], acc_s[...])
            scatter_tile(k)

            @pl.when(k == n_tiles - 1)
            def _():
                if n_tiles > 1:
                    wait_all(ybuf, sem_out, (n_tiles - 2) % 2)
                wait_all(ybuf, sem_out, (n_tiles - 1) % 2)


def _moe_ffn(plan, tok, w_gate, w_up, w_down, tm, max_items):
    n_tok = tok.shape[0] // TOK_PITCH
    n_tiles = n_tok // tm
    tok3 = tok.reshape(n_tok, TOK_PITCH, LANES)
    spec_a = lambda shape: pl.BlockSpec((1,) + shape, lambda i, ps, tl, ea, eb, lo, hi, fl, n: (ea[i], 0, 0))
    spec_b = lambda shape: pl.BlockSpec((1,) + shape, lambda i, ps, tl, ea, eb, lo, hi, fl, n: (eb[i], 0, 0))
    shapes = ((D_MODEL, D_EXPERT), (D_MODEL, D_EXPERT), (D_EXPERT, D_MODEL))
    wscratch = lambda: tuple(pltpu.VMEM(s, bf16) for s in shapes)
    return pl.pallas_call(
        functools.partial(_moe_ffn_kernel, chunk=256, n_tiles=n_tiles),
        out_shape=jax.ShapeDtypeStruct((n_tok, SUBLANES, LANES), f32),
        grid_spec=pltpu.PrefetchScalarGridSpec(
            num_scalar_prefetch=8,
            grid=(max_items,),
            in_specs=[pl.BlockSpec(memory_space=pl.ANY)] + [spec_a(s) for s in shapes] + [spec_b(s) for s in shapes],
            out_specs=pl.BlockSpec(memory_space=pl.ANY),
            scratch_shapes=[pltpu.SMEM((n_tok,), jnp.int32),
                            pltpu.VMEM((2, tm * TOK_PITCH, LANES), f32),
                            pltpu.VMEM((2, tm * SUBLANES, LANES), f32),
                            pltpu.VMEM((tm, D_MODEL), bf16), pltpu.VMEM((tm, ROUTE_LANES), f32),
                            pltpu.VMEM((tm, D_MODEL), f32), wscratch(), wscratch(),
                            pltpu.SemaphoreType.DMA((2,)), pltpu.SemaphoreType.DMA((2,))]),
        compiler_params=_cparams(("arbitrary",), 60 * 1024 * 1024),
        name="moe_ffn",
    )(plan['pos'], plan['it_tile'], plan['it_ea'], plan['it_eb'], plan['it_lo'], plan['it_hi'],
      plan['it_flags'], plan['n_items'], tok3, w_gate, w_up, w_down, w_gate, w_up, w_down)


def _final_kernel(moe_ref, x1_ref, mod_ref, fg_ref, yp_ref, yl_ref, *, prompt_tiles):
    i = pl.program_id(0)
    tm = x1_ref.shape[0]
    y = _rms(x1_ref[...] + mod_ref[0, GT2:GT2 + 1, :] * _load_token_tiles(moe_ref, tm), fg_ref[...])

    @pl.when(i < prompt_tiles)
    def _():
        yp_ref[...] = y

    @pl.when(i >= prompt_tiles)
    def _():
        yl_ref[...] = y


def _final(moe, x1, mod, final_g, t_prompt, t_lat, seq_s):
    tm = 512
    npt, nst = t_prompt // tm, t_lat // tm
    spb = seq_s // tm
    moe2 = moe.reshape(moe.shape[0] * SUBLANES, LANES)
    return pl.pallas_call(
        functools.partial(_final_kernel, prompt_tiles=npt),
        out_shape=(jax.ShapeDtypeStruct((t_prompt, D_MODEL), f32),
                   jax.ShapeDtypeStruct((t_lat, D_MODEL), f32)),
        grid=(npt + nst,),
        in_specs=[pl.BlockSpec((tm * SUBLANES, LANES), lambda i: (i, 0)),
                  pl.BlockSpec((tm, D_MODEL), lambda i: (i, 0)),
                  pl.BlockSpec((1, MOD_ROWS, D_MODEL),
                               lambda i: (jnp.where(i < npt, 0, 1 + jnp.maximum(i - npt, 0) // spb), 0, 0)),
                  pl.BlockSpec((1, D_MODEL), lambda i: (0, 0))],
        out_specs=(pl.BlockSpec((tm, D_MODEL), lambda i: (jnp.minimum(i, npt - 1), 0)),
                   pl.BlockSpec((tm, D_MODEL), lambda i: (jnp.maximum(i - npt, 0), 0))),
        compiler_params=_cparams(("arbitrary",)),
        name="moe_combine_final",
    )(moe2, x1, mod, final_g.reshape(1, D_MODEL))


def _moe(tok, ee, x1, mod, w_gate, w_up, w_down, final_g, t_prompt, t_lat, seq_s):
    tm = MOE_TM
    n_rows = t_prompt + t_lat
    assert n_rows % tm == 0
    max_items = n_rows // tm + N_CLASSES
    plan = _route_plan(ee, tm, max_items)
    moe = _moe_ffn(plan, tok, w_gate, w_up, w_down, tm, max_items)
    return _final(moe, x1, mod, final_g, t_prompt, t_lat, seq_s)


def kernel(x_prompt, x_sample, cache_k, cache_v, c, c_ctx, w_ada, b_ada, norm1_g, w_in, rpb, conv_w, conv_b, filt_w1, filt_b1, filt_w2, filt_b2, filt_w3, filt_freq, hyena_skip, gnorm_att, gnorm_hyena, w_out, norm2_g, router_grp_w, router_grp_b, router_exp_w, router_exp_b, w_gate, w_up, w_down, final_g):
    depth = w_ada.shape[0]
    assert depth == 1
    batch, seq, _ = x_prompt.shape
    dec_batch, dec_seq, _ = x_sample.shape
    l = 0

    def pack_router(grp, exp):
        rows = grp.shape[0]
        return jnp.concatenate([grp, jnp.zeros((rows, ROUTE_EXP_LANE0 - N_GROUPS), f32), exp,
                                jnp.zeros((rows, ROUTE_LANES - ROUTE_EXP_LANE0 - N_EXPERTS), f32)], axis=1)

    wr = pack_router(router_grp_w[l], router_exp_w[l])
    br = pack_router(router_grp_b[l][None, :], router_exp_b[l][None, :])

    lp = {
        'norm1_g': norm1_g[l], 'w_in': w_in[l], 'conv_w': conv_w[l], 'conv_b': conv_b[l],
        'filt_w1': filt_w1[l], 'filt_b1': filt_b1[l], 'filt_w2': filt_w2[l], 'filt_b2': filt_b2[l],
        'filt_w3': filt_w3[l], 'filt_freq': filt_freq[l], 'hyena_skip': hyena_skip[l],
        'gnorm_hyena': gnorm_hyena[l], 'w_out': w_out[l], 'norm2_g': norm2_g[l],
        'wr': wr, 'br': br, 'w_gate': w_gate[l], 'w_up': w_up[l], 'w_down': w_down[l],
    }

    cond8 = jnp.concatenate([c_ctx[None, :], c, jnp.zeros((SUBLANES - 1 - dec_batch, D_MODEL), f32)], axis=0)
    mod = _ada_mod(cond8, w_ada[l], b_ada[l], 1 + dec_batch).reshape(SUBLANES, N_MOD, D_MODEL)
    mod = jnp.pad(mod, ((0, 0), (0, MOD_ROWS - N_MOD), (0, 0)))
    mod_ctx, mod_lat = mod[0:1], mod[1:1 + dec_batch]

    xp = x_prompt.reshape(batch * seq, D_MODEL)
    xs = x_sample.reshape(dec_batch * dec_seq, D_MODEL)

    qp, k_ctx, v_ctx, hyp = _inproj(xp, mod_ctx, lp['norm1_g'], lp['w_in'], batch * seq, f32)
    attp = _ctx_attention(qp, k_ctx, v_ctx, gnorm_att[l], seq)
    hyop = _hyena(hyp, lp, batch, seq)

    ql, kl, vl, hyl = _inproj(xs, mod_lat, lp['norm1_g'], lp['w_in'], dec_seq, bf16)
    kc = cache_k[:, l].reshape(dec_batch * cache_k.shape[2], D_ATT).astype(bf16)
    vc = cache_v[:, l].reshape(dec_batch * cache_v.shape[2], D_ATT).astype(bf16)
    attl = _na_attention(ql, kl, vl, kc, vc, rpb[l], gnorm_att[l], dec_batch, dec_seq)
    hyol = _hyena(hyl, lp, dec_batch, dec_seq)

    x1, tok, ee = _outproj(xp, xs, attp, attl, hyop, hyol, mod[0:1 + dec_batch], lp['w_out'], lp['norm2_g'],
                           lp['wr'], lp['br'], seq, dec_seq)
    yp, ys = _moe(tok, ee, x1, mod[0:1 + dec_batch], lp['w_gate'], lp['w_up'], lp['w_down'], final_g,
                  batch * seq, dec_batch * dec_seq, dec_seq)

    y_prompt = yp.reshape(batch, seq, D_MODEL)
    y_sample = ys.reshape(dec_batch, dec_seq, D_MODEL)
    new_k = k_ctx.reshape(batch, 1, seq, H_ATT, HEAD_DIM)
    new_v = v_ctx.reshape(batch, 1, seq, H_ATT, HEAD_DIM)
    return (y_prompt, y_sample, new_k, new_v)
```

```python
import functools
import math

import jax
import jax.numpy as jnp
import numpy as np
from jax import lax
from jax.experimental import pallas as pl
from jax.experimental.pallas import tpu as pltpu

f32 = jnp.float32
bf16 = jnp.bfloat16
HIGHEST = lax.Precision.HIGHEST

D_MODEL = 1024
GRID_W = 64
H_ATT = 8
HEAD_DIM = 64
D_ATT = H_ATT * HEAD_DIM
D_HYENA = 512
D_IN = 3 * D_ATT + 3 * D_HYENA
NA_ROWS = 8
NA_COLS = 16
SHORT_CONV = 3
FILTER_BANDS = 16
EMB_DIM = 1 + 2 * FILTER_BANDS
FILTER_FF = 64
DECAY_TARGET = 1e-2
MIN_DECAY = math.log(DECAY_TARGET) / 1.5
MAX_DECAY = math.log(DECAY_TARGET) / 0.3
N_GROUPS = 4
EXPERTS_PER_GROUP = 4
N_EXPERTS = N_GROUPS * EXPERTS_PER_GROUP
D_EXPERT = 512
N_MOD = 6
EPS = 1e-6
NEG_INF = -1e30
ATT_SCALE = HEAD_DIM ** -0.5

LANES = 128
SUBLANES = 8
MOD_ROWS = 8
ROUTE_LANES = 128
ROUTE_EXP_LANE0 = 16
ROUTE_E1, ROUTE_E2, ROUTE_W1, ROUTE_W2 = 0, 1, 2, 3
TOK_PITCH = 16
TOK_RT_ROW = 8
MOE_TM = 512
GATHER_PRIORITIES = (0,)
SCATTER_PRIORITIES = (1,)
VMEM_LIMIT = 56 * 1024 * 1024

SH1, SC1, GT1, SH2, SC2, GT2 = range(6)


def _cparams(sem, vmem=VMEM_LIMIT):
    return pltpu.CompilerParams(dimension_semantics=sem, vmem_limit_bytes=vmem)


def _dot(a, b):
    return jnp.dot(a, b, preferred_element_type=f32)


def _dot_hi(a, b):
    return lax.dot_general(a, b, (((1,), (0,)), ((), ())), precision=HIGHEST,
                           preferred_element_type=f32)


def _dot_nt(a, b):
    return lax.dot_general(a, b, (((1,), (1,)), ((), ())), preferred_element_type=f32)


def _rms(x, g):
    ms = jnp.mean(x * x, axis=-1, keepdims=True)
    return x * lax.rsqrt(ms + EPS) * g


def _cast_rows(src_ref, dst_ref, chunk):
    n = src_ref.shape[0] // chunk

    def body(i, c):
        r = pl.multiple_of(i * chunk, chunk)
        dst_ref[pl.ds(r, chunk), :] = src_ref[pl.ds(r, chunk), :].astype(dst_ref.dtype)
        return c

    lax.fori_loop(0, n, body, 0)


def _ada_kernel(ct_ref, w_ref, b_ref, o_ref, *, n_cond):
    ct = ct_ref[...]
    st = ct * (1.0 / (1.0 + jnp.exp(-ct)))
    w = w_ref[...]
    rid = lax.broadcasted_iota(jnp.int32, o_ref.shape, 0)
    out = jnp.broadcast_to(b_ref[...], o_ref.shape)
    for m in range(n_cond):
        row = jnp.sum(w * st[:, m:m + 1], axis=0, keepdims=True)
        out = out + jnp.where(rid == m, row, 0.0)
    o_ref[...] = out


def _ada_mod(cond8, w_ada, b_ada, n_cond):
    tn = 1536
    n = N_MOD * D_MODEL
    return pl.pallas_call(
        functools.partial(_ada_kernel, n_cond=n_cond),
        out_shape=jax.ShapeDtypeStruct((SUBLANES, n), f32),
        grid=(n // tn,),
        in_specs=[pl.BlockSpec((D_MODEL, SUBLANES), lambda j: (0, 0)),
                  pl.BlockSpec((D_MODEL, tn), lambda j: (0, j)),
                  pl.BlockSpec((1, tn), lambda j: (0, j))],
        out_specs=pl.BlockSpec((SUBLANES, tn), lambda j: (0, j)),
        compiler_params=_cparams(("arbitrary",)),
        name="ada_mod",
    )(cond8.T, w_ada, b_ada.reshape(1, n))


def _inproj_kernel(x_ref, mod_ref, g_ref, w_ref, q_ref, k_ref, v_ref, hy_ref, wbf_ref):
    @pl.when(pl.program_id(0) == 0)
    def _():
        _cast_rows(w_ref, wbf_ref, 128)

    h = _rms(x_ref[...], g_ref[...])
    h = h * (1.0 + mod_ref[0, SC1:SC1 + 1, :]) + mod_ref[0, SH1:SH1 + 1, :]
    p = _dot(h.astype(bf16), wbf_ref[...])
    q_ref[...] = p[:, 0:D_ATT].astype(q_ref.dtype)
    k_ref[...] = p[:, D_ATT:2 * D_ATT].astype(k_ref.dtype)
    v_ref[...] = p[:, 2 * D_ATT:3 * D_ATT].astype(v_ref.dtype)
    hy_ref[...] = p[:, 3 * D_ATT:]


def _short_conv_gate(u, w_ref, b_ref):
    seq = u.shape[0]
    row = lax.broadcasted_iota(jnp.int32, u.shape, 0)
    up = jnp.where(row == 0, 0.0, pltpu.roll(u, 1, 0))
    un = jnp.where(row == seq - 1, 0.0, pltpu.roll(u, seq - 1, 0))
    uc = b_ref[...] + up * w_ref[0:1, :]
    uc = uc + u * w_ref[1:2, :]
    uc = uc + un * w_ref[2:3, :]
    return uc[:, 2 * D_HYENA:] * uc[:, D_HYENA:2 * D_HYENA], uc[:, :D_HYENA]


def _inproj_conv_kernel(x_ref, mod_ref, g_ref, w_ref, cw_ref, cb_ref, q_ref, k_ref, v_ref, zbf_ref, z_ref, x0c_ref,
                        wbf_ref, *, seq):
    @pl.when(pl.program_id(0) == 0)
    def _():
        _cast_rows(w_ref, wbf_ref, 128)

    h = _rms(x_ref[...], g_ref[...])
    h = h * (1.0 + mod_ref[0, SC1:SC1 + 1, :]) + mod_ref[0, SH1:SH1 + 1, :]
    p = _dot(h.astype(bf16), wbf_ref[...])
    q_ref[...] = p[:, 0:D_ATT].astype(q_ref.dtype)
    k_ref[...] = p[:, D_ATT:2 * D_ATT].astype(k_ref.dtype)
    v_ref[...] = p[:, 2 * D_ATT:3 * D_ATT].astype(v_ref.dtype)
    for b in range(x_ref.shape[0] // seq):
        z, x0c = _short_conv_gate(p[b * seq:(b + 1) * seq, 3 * D_ATT:], cw_ref, cb_ref)
        cs = slice(b * D_HYENA, (b + 1) * D_HYENA)
        z_ref[:, cs] = z
        zbf_ref[:, cs] = z.astype(bf16)
        x0c_ref[:, cs] = x0c


def _inproj_conv(x, mod, norm_g, w_in, conv_w, conv_b, seq, kv_dtype):
    t = x.shape[0]
    tm = 512
    assert tm % seq == 0 and mod.shape[0] == 1
    bpt = tm // seq
    n = (t // seq) * D_HYENA
    tok = lambda w: pl.BlockSpec((tm, w), lambda i: (i, 0))
    seqm = pl.BlockSpec((seq, bpt * D_HYENA), lambda i: (0, i))
    const = lambda shape, **kw: pl.BlockSpec(shape, lambda i: (0,) * len(shape), **kw)
    return pl.pallas_call(
        functools.partial(_inproj_conv_kernel, seq=seq),
        out_shape=(jax.ShapeDtypeStruct((t, D_ATT), bf16),
                   jax.ShapeDtypeStruct((t, D_ATT), kv_dtype),
                   jax.ShapeDtypeStruct((t, D_ATT), kv_dtype),
                   jax.ShapeDtypeStruct((seq, n), bf16),
                   jax.ShapeDtypeStruct((seq, n), f32),
                   jax.ShapeDtypeStruct((seq, n), f32)),
        grid=(t // tm,),
        in_specs=[tok(D_MODEL), const((1, MOD_ROWS, D_MODEL)), const((1, D_MODEL)),
                  const((D_MODEL, D_IN), pipeline_mode=pl.Buffered(1)),
                  const((SHORT_CONV, 3 * D_HYENA)), const((1, 3 * D_HYENA))],
        out_specs=(tok(D_ATT), tok(D_ATT), tok(D_ATT), seqm, seqm, seqm),
        scratch_shapes=[pltpu.VMEM((D_MODEL, D_IN), bf16)],
        compiler_params=_cparams(("arbitrary",)),
        name="inproj_conv",
    )(x, mod, norm_g.reshape(1, D_MODEL), w_in, conv_w, conv_b.reshape(1, 3 * D_HYENA))


def _inproj(x, mod, norm_g, w_in, rows_per_mod, kv_dtype):
    t = x.shape[0]
    tm = 512
    blocks_per_mod = rows_per_mod // tm
    return pl.pallas_call(
        _inproj_kernel,
        out_shape=(jax.ShapeDtypeStruct((t, D_ATT), bf16),
                   jax.ShapeDtypeStruct((t, D_ATT), kv_dtype),
                   jax.ShapeDtypeStruct((t, D_ATT), kv_dtype),
                   jax.ShapeDtypeStruct((t, 3 * D_HYENA), f32)),
        grid=(t // tm,),
        in_specs=[pl.BlockSpec((tm, D_MODEL), lambda i: (i, 0)),
                  pl.BlockSpec((1, MOD_ROWS, D_MODEL), lambda i: (i // blocks_per_mod, 0, 0)),
                  pl.BlockSpec((1, D_MODEL), lambda i: (0, 0)),
                  pl.BlockSpec((D_MODEL, D_IN), lambda i: (0, 0), pipeline_mode=pl.Buffered(1))],
        out_specs=(pl.BlockSpec((tm, D_ATT), lambda i: (i, 0)),
                   pl.BlockSpec((tm, D_ATT), lambda i: (i, 0)),
                   pl.BlockSpec((tm, D_ATT), lambda i: (i, 0)),
                   pl.BlockSpec((tm, 3 * D_HYENA), lambda i: (i, 0))),
        scratch_shapes=[pltpu.VMEM((D_MODEL, D_IN), bf16)],
        compiler_params=_cparams(("arbitrary",)),
        name="inproj",
    )(x, mod, norm_g.reshape(1, D_MODEL), w_in)


def _split_heads(q2):
    lane = lax.broadcasted_iota(jnp.int32, q2.shape, 1)
    qa = jnp.where(lane < HEAD_DIM, q2, 0.0)
    qb = jnp.where(lane >= HEAD_DIM, q2, 0.0)
    return jnp.concatenate([qa, qb], axis=0)


def _merge_heads(o_ab):
    m = o_ab.shape[0] // 2
    lane = lax.broadcasted_iota(jnp.int32, (m, LANES), 1)
    return jnp.where(lane < HEAD_DIM, o_ab[:m], o_ab[m:])


CTX_SEQS_PER_STEP = 2


def _ctx_attn_kernel(q_ref, k_ref, v_ref, g_ref, o_ref, *, seq):
    for b in range(q_ref.shape[0] // seq):
        rows = slice(b * seq, (b + 1) * seq)
        outs = []
        for p in range(D_ATT // LANES):
            cs = slice(p * LANES, (p + 1) * LANES)
            qq = _split_heads(q_ref[rows, cs] * ATT_SCALE).astype(bf16)
            s = _dot_nt(qq, k_ref[rows, cs].astype(bf16))
            m = jnp.max(s, axis=-1, keepdims=True)
            e = jnp.exp(s - m)
            l = jnp.sum(e, axis=-1, keepdims=True)
            o_ab = _dot(e.astype(bf16), v_ref[rows, cs].astype(bf16)) / l
            outs.append(_merge_heads(o_ab))
        o_ref[rows, :] = _rms(jnp.concatenate(outs, axis=-1), g_ref[...]).astype(o_ref.dtype)


def _ctx_attention(q, k, v, gnorm, seq):
    t = q.shape[0]
    tm = CTX_SEQS_PER_STEP * seq
    assert t % tm == 0
    spec = pl.BlockSpec((tm, D_ATT), lambda b: (b, 0))
    return pl.pallas_call(
        functools.partial(_ctx_attn_kernel, seq=seq),
        out_shape=jax.ShapeDtypeStruct((t, D_ATT), bf16),
        grid=(t // tm,),
        in_specs=[spec, spec, spec, pl.BlockSpec((1, D_ATT), lambda b: (0, 0))],
        out_specs=spec,
        compiler_params=_cparams(("arbitrary",)),
        name="ctx_attn",
    )(q, k, v, gnorm.reshape(1, D_ATT))


def _na_tables():
    col = np.arange(GRID_W)
    cs = np.clip(col - NA_COLS // 2, 0, GRID_W - NA_COLS)
    col_mask = (col[None, :] >= cs[:, None]) & (col[None, :] < cs[:, None] + NA_COLS)
    mask = np.tile(col_mask.astype(np.float32), (1, NA_ROWS))
    return mask


N_DR = 2 * NA_ROWS - 1
N_DC = 2 * NA_COLS - 1
BIAS_PAIRS = N_DR - 1


def _na_bias_rows(rpb):
    out = jnp.zeros((H_ATT, BIAS_PAIRS, LANES), f32)
    out = out.at[:, :, 0:N_DC].set(rpb[:, 0:BIAS_PAIRS])
    return out.at[:, :, GRID_W:GRID_W + N_DC].set(rpb[:, 1:N_DR])


NA_ROWS_PER_STEP = 8


def _na_row_start(r, rows):
    return jnp.clip(r - NA_ROWS // 2, 0, rows - NA_ROWS)


def _na_attn_kernel(q_ref, k_ref, v_ref, kc_ref, vc_ref, rp_ref, mask_ref, g_ref, o_ref,
                    t2_ref, *, rows):
    b = pl.program_id(0)
    r = pl.program_id(1)

    @pl.when((b == 0) & (r == 0))
    def _():
        valid = mask_ref[:, 0:LANES] != 0.0
        for h in range(H_ATT):
            for i in range(BIAS_PAIRS):
                v = jnp.broadcast_to(rp_ref[h, i:i + 1, :], (GRID_W, LANES))
                v = pltpu.roll(v, LANES - (NA_COLS - 1), 1, stride=1, stride_axis=0)
                t2_ref[h * BIAS_PAIRS + i] = jnp.where(valid, v, NEG_INF)

    nwin = NA_ROWS * GRID_W
    for rr in range(NA_ROWS_PER_STEP):
        row = r * NA_ROWS_PER_STEP + rr
        qs = slice(rr * GRID_W, (rr + 1) * GRID_W)
        rs = _na_row_start(row, rows)
        start = pl.multiple_of(rs * GRID_W, GRID_W)
        i0 = rs - row + NA_ROWS - 1
        outs = []
        for p in range(D_ATT // LANES):
            cs = slice(p * LANES, (p + 1) * LANES)
            qq = _split_heads(q_ref[qs, cs] * ATT_SCALE).astype(bf16)
            kw = k_ref[pl.ds(start, nwin), cs]
            vw = v_ref[pl.ds(start, nwin), cs]
            s_lat = _dot_nt(qq, kw)
            s_ctx = _dot_nt(qq, kc_ref[:, cs])
            bias2 = jnp.concatenate(
                [jnp.concatenate([t2_ref[(2 * p + hh) * BIAS_PAIRS + i0 + 2 * jp] for hh in range(2)], axis=0)
                 for jp in range(NA_ROWS // 2)], axis=-1)
            s_lat = s_lat + bias2
            m = jnp.maximum(jnp.max(s_lat, axis=-1, keepdims=True), jnp.max(s_ctx, axis=-1, keepdims=True))
            e_lat = jnp.exp(s_lat - m)
            e_ctx = jnp.exp(s_ctx - m)
            l = jnp.sum(e_lat, axis=-1, keepdims=True) + jnp.sum(e_ctx, axis=-1, keepdims=True)
            o_ab = (_dot(e_lat.astype(bf16), vw) + _dot(e_ctx.astype(bf16), vc_ref[:, cs])) / l
            outs.append(_merge_heads(o_ab))
        o_ref[qs, :] = _rms(jnp.concatenate(outs, axis=-1), g_ref[...]).astype(o_ref.dtype)


def _na_attention(q, k, v, kc, vc, rpb, gnorm, nb, seq):
    rows = seq // GRID_W
    assert rows % NA_ROWS_PER_STEP == 0
    steps = rows // NA_ROWS_PER_STEP
    qrows = NA_ROWS_PER_STEP * GRID_W
    past = kc.shape[0] // nb
    mask = _na_tables()
    return pl.pallas_call(
        functools.partial(_na_attn_kernel, rows=rows),
        out_shape=jax.ShapeDtypeStruct((nb * seq, D_ATT), bf16),
        grid=(nb, steps),
        in_specs=[pl.BlockSpec((qrows, D_ATT), lambda b, r: (b * steps + r, 0)),
                  pl.BlockSpec((seq, D_ATT), lambda b, r: (b, 0)),
                  pl.BlockSpec((seq, D_ATT), lambda b, r: (b, 0)),
                  pl.BlockSpec((past, D_ATT), lambda b, r: (b, 0)),
                  pl.BlockSpec((past, D_ATT), lambda b, r: (b, 0)),
                  pl.BlockSpec((H_ATT, BIAS_PAIRS, LANES), lambda b, r: (0, 0, 0)),
                  pl.BlockSpec((GRID_W, NA_ROWS * GRID_W), lambda b, r: (0, 0)),
                  pl.BlockSpec((1, D_ATT), lambda b, r: (0, 0))],
        out_specs=pl.BlockSpec((qrows, D_ATT), lambda b, r: (b * steps + r, 0)),
        scratch_shapes=[pltpu.VMEM((H_ATT * BIAS_PAIRS, GRID_W, LANES), f32)],
        compiler_params=_cparams(("arbitrary", "arbitrary")),
        name="na_attn",
    )(q, k, v, kc, vc, _na_bias_rows(rpb), jnp.asarray(mask), gnorm.reshape(1, D_ATT))


def _hy_front_kernel(x0_ref, x1_ref, v_ref, w0_ref, w1_ref, wv_ref, b0_ref, b1_ref, bv_ref,
                     zbf_ref, z_ref, x0c_ref):
    seq = x0_ref.shape[0]
    row = lax.broadcasted_iota(jnp.int32, x0_ref.shape, 0)
    first = row == 0
    last = row == seq - 1

    def conv(u_ref, w_ref, b_ref):
        u = u_ref[...]
        up = jnp.where(first, 0.0, pltpu.roll(u, 1, 0))
        un = jnp.where(last, 0.0, pltpu.roll(u, seq - 1, 0))
        y = b_ref[...] + up * w_ref[0:1, :]
        y = y + u * w_ref[1:2, :]
        return y + un * w_ref[2:3, :]

    z = conv(v_ref, wv_ref, bv_ref) * conv(x1_ref, w1_ref, b1_ref)
    z_ref[...] = z
    zbf_ref[...] = z.astype(bf16)
    x0c_ref[...] = conv(x0_ref, w0_ref, b0_ref)


def _hy_front(hy, conv_w, conv_b, nb, seq):
    tc = D_HYENA if seq <= 512 else D_HYENA // 2
    nc = D_HYENA // tc
    n = nb * D_HYENA
    cb = conv_b.reshape(1, 3 * D_HYENA)

    def part(k):
        return (pl.BlockSpec((seq, tc), lambda b, j: (b, k * nc + j)),
                pl.BlockSpec((SHORT_CONV, tc), lambda b, j: (0, k * nc + j)),
                pl.BlockSpec((1, tc), lambda b, j: (0, k * nc + j)))

    (x0s, w0s, b0s), (x1s, w1s, b1s), (vs, wvs, bvs) = part(0), part(1), part(2)
    ospec = pl.BlockSpec((seq, tc), lambda b, j: (0, b * nc + j))
    return pl.pallas_call(
        _hy_front_kernel,
        out_shape=(jax.ShapeDtypeStruct((seq, n), bf16),
                   jax.ShapeDtypeStruct((seq, n), f32),
                   jax.ShapeDtypeStruct((seq, n), f32)),
        grid=(nb, nc),
        in_specs=[x0s, x1s, vs, w0s, w1s, wvs, b0s, b1s, bvs],
        out_specs=(ospec, ospec, ospec),
        compiler_params=_cparams(("arbitrary", "arbitrary")),
        name="hyena_front",
    )(hy, hy, hy, conv_w, conv_w, conv_w, cb, cb, cb)


def _filter_features(seq):
    t = np.linspace(0.0, 1.0, seq, dtype=np.float64)[:, None]
    w = 2.0 * math.pi * np.arange(seq, dtype=np.float64)[:, None] / seq
    fb = np.linspace(1e-4, FILTER_BANDS - 1, FILTER_BANDS, dtype=np.float64)[None, :]
    ang = fb * w
    z = np.concatenate([t, np.cos(ang), -np.sin(ang)], axis=-1).astype(np.float32)
    return np.pad(z, ((0, 0), (0, LANES - EMB_DIM)))


def _filt_kernel(zt_ref, w1t_ref, b1_ref, fr_ref, w2t_ref, b2_ref, w3_ref, dl_ref, h_ref, kl_ref, *, seq):
    i = pl.program_id(0)
    tr = zt_ref.shape[1]
    fr = fr_ref[...]
    h = jnp.sin(fr * (_dot_hi(w1t_ref[...], zt_ref[...]) + b1_ref[...]))
    h = jnp.sin(fr * (_dot_hi(w2t_ref[...], h) + b2_ref[...]))
    h = _dot_hi(h.T, w3_ref[...])
    row = lax.broadcasted_iota(jnp.int32, (tr, D_HYENA), 0) + i * tr
    t = row[:, 0:1].astype(f32) * (1.0 / (seq - 1))
    decay = jnp.exp(-t * dl_ref[...])
    hf = h[:, :D_HYENA] * decay
    hb = jnp.where(row == 0, 0.0, h[:, D_HYENA:] * decay)
    h_ref[:, :D_HYENA] = (hf + hb).astype(bf16)
    h_ref[:, D_HYENA:] = (hb - hf).astype(bf16)
    alt = (1 - 2 * (row & 1)).astype(f32)
    part = jnp.sum(alt * (hf + hb), axis=0, keepdims=True)

    @pl.when(i == 0)
    def _():
        kl_ref[...] = jnp.zeros_like(kl_ref)

    kl_ref[...] += jnp.broadcast_to(part, kl_ref.shape)


def _hy_filters(seq, w1, b1, w2, b2, w3, freq):
    tr = 256
    zt = jnp.asarray(np.ascontiguousarray(_filter_features(seq).T))
    deltas = np.abs(np.linspace(MIN_DECAY, MAX_DECAY, D_HYENA, dtype=np.float64))[None, :].astype(np.float32)
    w1t = jnp.pad(w1, ((0, LANES - EMB_DIM), (0, 0))).T
    const = lambda shape: pl.BlockSpec(shape, lambda i: (0, 0))
    col = lambda v: v.reshape(FILTER_FF, 1)
    return pl.pallas_call(
        functools.partial(_filt_kernel, seq=seq),
        out_shape=(jax.ShapeDtypeStruct((seq, 2 * D_HYENA), bf16),
                   jax.ShapeDtypeStruct((SUBLANES, D_HYENA), f32)),
        grid=(seq // tr,),
        in_specs=[pl.BlockSpec((LANES, tr), lambda i: (0, i)),
                  const((FILTER_FF, LANES)), const((FILTER_FF, 1)), const((FILTER_FF, 1)),
                  const((FILTER_FF, FILTER_FF)), const((FILTER_FF, 1)),
                  const((FILTER_FF, 2 * D_HYENA)), const((1, D_HYENA))],
        out_specs=(pl.BlockSpec((tr, 2 * D_HYENA), lambda i: (i, 0)),
                   pl.BlockSpec((SUBLANES, D_HYENA), lambda i: (0, 0))),
        compiler_params=_cparams(("arbitrary",)),
        name="hyena_filters",
    )(zt, w1t, col(b1), col(freq), w2.T, col(b2), w3, jnp.asarray(deltas))


def _dft_mats(seq):
    n = 2 * seq
    ph = (np.arange(seq, dtype=np.int64)[:, None] * np.arange(seq, dtype=np.int64)[None, :]) % n
    ang = ph.astype(np.float64) * (2.0 * math.pi / n)
    return np.cos(ang).astype(np.float32), np.sin(ang).astype(np.float32)


def _alt_col(rows, offset):
    row = lax.broadcasted_iota(jnp.int32, (rows, 1), 0) + offset
    return (1 - 2 * (row & 1)).astype(f32)


def _hy_fwd_kernel(fr_ref, fi_ref, z_ref, h_ref, kl_ref, yr_ref, yi_ref, yl_ref, kr_s, ki_s, *, n):
    i = pl.program_id(0)
    j = pl.program_id(1)
    tf = fr_ref.shape[0]
    tn = z_ref.shape[1]
    frb = fr_ref[...].astype(bf16)
    fib = fi_ref[...].astype(bf16)

    @pl.when(j == 0)
    def _():
        f = lax.broadcasted_iota(jnp.int32, (tf, 1), 0) + i * tf
        cf = jnp.where(f == 0, 1.0 / n, 2.0 / n)
        kr_s[...] = _dot(frb, h_ref[:, :D_HYENA]) * cf
        ki_s[...] = _dot(fib, h_ref[:, D_HYENA:]) * cf

    a = _dot(frb, z_ref[...])
    b = _dot(fib, z_ref[...])
    kr = kr_s[...]
    ki = ki_s[...]
    for c in range(tn // D_HYENA):
        cs = slice(c * D_HYENA, (c + 1) * D_HYENA)
        yr_ref[:, cs] = (a[:, cs] * kr + b[:, cs] * ki).astype(bf16)
        yi_ref[:, cs] = (b[:, cs] * kr - a[:, cs] * ki).astype(bf16)

    @pl.when(i == 0)
    def _():
        alt = _alt_col(z_ref.shape[0], 0)
        nz = jnp.sum(z_ref[...].astype(f32) * alt, axis=0, keepdims=True)
        kl = jnp.concatenate([kl_ref[0:1, :]] * (tn // D_HYENA), axis=-1)
        yl_ref[...] = jnp.broadcast_to(nz * kl * (1.0 / n), yl_ref.shape)


def _hy_fwd(fr, fi, zbf, hcat, kl, seq):
    n_cols = zbf.shape[1]
    tf = min(seq, 512)
    tn = min(n_cols, 2048)
    ni, nj = seq // tf, n_cols // tn
    assert ni == 1 or nj == 1
    return pl.pallas_call(
        functools.partial(_hy_fwd_kernel, n=2 * seq),
        out_shape=(jax.ShapeDtypeStruct((seq, n_cols), bf16),
                   jax.ShapeDtypeStruct((seq, n_cols), bf16),
                   jax.ShapeDtypeStruct((SUBLANES, n_cols), f32)),
        grid=(ni, nj),
        in_specs=[pl.BlockSpec((tf, seq), lambda i, j: (i, 0)),
                  pl.BlockSpec((tf, seq), lambda i, j: (i, 0)),
                  pl.BlockSpec((seq, tn), lambda i, j: (0, j)),
                  pl.BlockSpec((seq, 2 * D_HYENA), lambda i, j: (0, 0)),
                  pl.BlockSpec((SUBLANES, D_HYENA), lambda i, j: (0, 0))],
        out_specs=(pl.BlockSpec((tf, tn), lambda i, j: (i, j)),
                   pl.BlockSpec((tf, tn), lambda i, j: (i, j)),
                   pl.BlockSpec((SUBLANES, tn), lambda i, j: (0, j))),
        scratch_shapes=[pltpu.VMEM((tf, D_HYENA), f32), pltpu.VMEM((tf, D_HYENA), f32)],
        compiler_params=_cparams(("arbitrary", "arbitrary")),
        name="hyena_dft_fwd",
    )(fr, fi, zbf, hcat, kl)


def _hy_inv_kernel(fr_ref, fi_ref, yr_ref, yi_ref, yl_ref, z_ref, x0_ref, skip_ref, g_ref, o_ref):
    tt = fr_ref.shape[0]
    tn = yr_ref.shape[1]
    y = _dot(fr_ref[...].astype(bf16), yr_ref[...]) + _dot(fi_ref[...].astype(bf16), yi_ref[...])
    alt = _alt_col(tt, pl.program_id(0) * tt)
    for c in range(tn // D_HYENA):
        cs = slice(c * D_HYENA, (c + 1) * D_HYENA)
        yc = y[:, cs] + alt * yl_ref[0:1, cs] + z_ref[:, cs] * skip_ref[...]
        o_ref[:, cs] = _rms(yc * x0_ref[:, cs], g_ref[...]).astype(o_ref.dtype)


def _hy_inv(fr, fi, yr, yi, yl, z, x0c, skip, gnorm, seq):
    n_cols = z.shape[1]
    tt = min(seq, 512)
    tn = min(n_cols, 2048)
    blk = pl.BlockSpec((tt, tn), lambda i, j: (i, j))
    return pl.pallas_call(
        _hy_inv_kernel,
        out_shape=jax.ShapeDtypeStruct((seq, n_cols), bf16),
        grid=(seq // tt, n_cols // tn),
        in_specs=[pl.BlockSpec((tt, seq), lambda i, j: (i, 0)),
                  pl.BlockSpec((tt, seq), lambda i, j: (i, 0)),
                  pl.BlockSpec((seq, tn), lambda i, j: (0, j)),
                  pl.BlockSpec((seq, tn), lambda i, j: (0, j)),
                  pl.BlockSpec((SUBLANES, tn), lambda i, j: (0, j)),
                  blk, blk,
                  pl.BlockSpec((1, D_HYENA), lambda i, j: (0, 0)),
                  pl.BlockSpec((1, D_HYENA), lambda i, j: (0, 0))],
        out_specs=blk,
        compiler_params=_cparams(("arbitrary", "arbitrary")),
        name="hyena_dft_inv",
    )(fr, fi, yr, yi, yl, z, x0c, skip.reshape(1, D_HYENA), gnorm.reshape(1, D_HYENA))


def _hyena(hy, lp, nb, seq):
    return _hyena_long(*_hy_front(hy, lp['conv_w'], lp['conv_b'], nb, seq), lp, seq)


def _hyena_long(zbf, z, x0c, lp, seq):
    fr_np, fi_np = _dft_mats(seq)
    fr, fi = jnp.asarray(fr_np), jnp.asarray(fi_np)
    hcat, kl = _hy_filters(seq, lp['filt_w1'], lp['filt_b1'], lp['filt_w2'], lp['filt_b2'],
                           lp['filt_w3'], lp['filt_freq'])
    yr, yi, yl = _hy_fwd(fr, fi, zbf, hcat, kl, seq)
    return _hy_inv(fr, fi, yr, yi, yl, z, x0c, lp['hyena_skip'], lp['gnorm_hyena'], seq)


def _store_token_tiles(ref, x, pitch=SUBLANES):
    m = x.shape[0]
    for c in range(D_MODEL // LANES):
        ref[pl.ds(c, m, stride=pitch), :] = x[:, c * LANES:(c + 1) * LANES]


def _load_token_tiles(ref, m, lead=(), pitch=SUBLANES):
    return jnp.concatenate([ref[lead + (pl.ds(c, m, stride=pitch), slice(None))]
                            for c in range(D_MODEL // LANES)], axis=-1)


def _route(logits):
    lane_i = lax.broadcasted_iota(jnp.int32, logits.shape, 1)
    lane = lane_i.astype(f32)
    big = float(ROUTE_LANES)
    is_g = lane_i < N_GROUPS
    mg = jnp.max(jnp.where(is_g, logits, -jnp.inf), axis=-1, keepdims=True)
    sg = jnp.sum(jnp.where(is_g, jnp.exp(logits - mg), 0.0), axis=-1, keepdims=True)
    g_w = 1.0 / sg
    g_idx = jnp.min(jnp.where(is_g & (logits == mg), lane, big), axis=-1, keepdims=True)
    e_id = lane_i - ROUTE_EXP_LANE0
    sel = (e_id >= 0) & (e_id < N_EXPERTS) & ((e_id >> 2).astype(f32) == g_idx)
    me = jnp.max(jnp.where(sel, logits, -jnp.inf), axis=-1, keepdims=True)
    ee = jnp.where(sel, jnp.exp(logits - me), 0.0)
    prob = ee / jnp.sum(ee, axis=-1, keepdims=True)
    p1 = jnp.max(jnp.where(sel, prob, -1.0), axis=-1, keepdims=True)
    i1 = jnp.min(jnp.where(sel & (prob == p1), lane, big), axis=-1, keepdims=True)
    sel2 = sel & (lane != i1)
    p2 = jnp.max(jnp.where(sel2, prob, -1.0), axis=-1, keepdims=True)
    i2 = jnp.min(jnp.where(sel2 & (prob == p2), lane, big), axis=-1, keepdims=True)
    tot = p1 + p2
    rec = jnp.where(lane_i == ROUTE_E1, i1 - ROUTE_EXP_LANE0, 0.0)
    rec = jnp.where(lane_i == ROUTE_E2, i2 - ROUTE_EXP_LANE0, rec)
    rec = jnp.where(lane_i == ROUTE_W1, g_w * (p1 / tot), rec)
    return jnp.where(lane_i == ROUTE_W2, g_w * (p2 / tot), rec)


def _outproj_kernel(xp_ref, xs_ref, attp_ref, atts_ref, hyp_ref, hys_ref, mod_ref, wo_ref, g2_ref,
                    wr_ref, br_ref, x1_ref, tok_ref, ee_ref, wobf_ref, wrh_ref, wrl_ref, *, prompt_tiles):
    i = pl.program_id(0)

    @pl.when(i == 0)
    def _():
        _cast_rows(wo_ref, wobf_ref, 128)
        wr = wr_ref[...]
        hi = wr.astype(bf16)
        wrh_ref[...] = hi
        wrl_ref[...] = (wr - hi.astype(f32)).astype(bf16)

    is_p = i < prompt_tiles
    hyp = jnp.concatenate([hyp_ref[:, b * D_HYENA:(b + 1) * D_HYENA]
                           for b in range(hyp_ref.shape[1] // D_HYENA)], axis=0)
    x = jnp.where(is_p, xp_ref[...], xs_ref[...])
    att = jnp.where(is_p, attp_ref[...], atts_ref[...])
    hyo = jnp.where(is_p, hyp, hys_ref[...])
    proj = _dot(att, wobf_ref[0:D_ATT, :]) + _dot(hyo, wobf_ref[D_ATT:, :])
    x1 = x + mod_ref[0, GT1:GT1 + 1, :] * proj
    x1_ref[...] = x1
    h2 = _rms(x1, g2_ref[...]) * (1.0 + mod_ref[0, SC2:SC2 + 1, :]) + mod_ref[0, SH2:SH2 + 1, :]
    h2h = h2.astype(bf16)
    h2l = (h2 - h2h.astype(f32)).astype(bf16)
    logits = _dot(h2h, wrh_ref[...]) + _dot(h2l, wrh_ref[...]) + _dot(h2h, wrl_ref[...]) + br_ref[...]
    rt = _route(logits)
    tm = h2.shape[0]
    per_row = tm // LANES
    trow = lax.broadcasted_iota(jnp.int32, (tm, LANES), 0)
    diag = (trow & (LANES - 1)) == lax.broadcasted_iota(jnp.int32, (tm, LANES), 1)
    qrow = lax.broadcasted_iota(jnp.int32, (2 * per_row, tm), 0)
    tblk = lax.shift_right_logical(lax.broadcasted_iota(jnp.int32, (2 * per_row, tm), 1), LANES.bit_length() - 1)
    ee = jnp.zeros((2 * per_row, LANES), f32)
    for j, lane0 in enumerate((ROUTE_E1, ROUTE_E2)):
        spread = jnp.where(diag, rt[:, lane0:lane0 + 1], 0.0).astype(bf16)
        pick_rows = (qrow == tblk + j * per_row).astype(bf16)
        ee = ee + _dot(pick_rows, spread)
    ee_ref[0] = ee
    tok_ref[...] = jnp.zeros_like(tok_ref)
    _store_token_tiles(tok_ref, h2, TOK_PITCH)
    tok_ref[pl.ds(TOK_RT_ROW, h2.shape[0], stride=TOK_PITCH), :] = rt


def _outproj(xp, xs, attp, atts, hyp, hys, mod, w_out, norm2_g, wr, br, seq_p, seq_s):
    tm = 512
    tp, ts = xp.shape[0], xs.shape[0]
    npt, nst = tp // tm, ts // tm
    assert tm % seq_p == 0 and seq_s % tm == 0
    spb = seq_s // tm
    bpt = tm // seq_p
    p_idx = lambda i: jnp.minimum(i, npt - 1)
    s_idx = lambda i: jnp.maximum(i - npt, 0)
    const = lambda shape: pl.BlockSpec(shape, lambda i: (0,) * len(shape))
    return pl.pallas_call(
        functools.partial(_outproj_kernel, prompt_tiles=npt),
        out_shape=(jax.ShapeDtypeStruct((tp + ts, D_MODEL), f32),
                   jax.ShapeDtypeStruct(((tp + ts) * TOK_PITCH, LANES), f32),
                   jax.ShapeDtypeStruct((npt + nst, 2 * tm // LANES, LANES), f32)),
        grid=(npt + nst,),
        in_specs=[pl.BlockSpec((tm, D_MODEL), lambda i: (p_idx(i), 0)),
                  pl.BlockSpec((tm, D_MODEL), lambda i: (s_idx(i), 0)),
                  pl.BlockSpec((tm, D_ATT), lambda i: (p_idx(i), 0)),
                  pl.BlockSpec((tm, D_ATT), lambda i: (s_idx(i), 0)),
                  pl.BlockSpec((seq_p, bpt * D_HYENA), lambda i: (0, p_idx(i))),
                  pl.BlockSpec((tm, D_HYENA), lambda i: (s_idx(i) % spb, s_idx(i) // spb)),
                  pl.BlockSpec((1, MOD_ROWS, D_MODEL),
                               lambda i: (jnp.where(i < npt, 0, 1 + s_idx(i) // spb), 0, 0)),
                  const((D_MODEL, D_MODEL)), const((1, D_MODEL)),
                  const((D_MODEL, ROUTE_LANES)), const((1, ROUTE_LANES))],
        out_specs=(pl.BlockSpec((tm, D_MODEL), lambda i: (i, 0)),
                   pl.BlockSpec((tm * TOK_PITCH, LANES), lambda i: (i, 0)),
                   pl.BlockSpec((1, 2 * tm // LANES, LANES), lambda i: (i, 0, 0))),
        scratch_shapes=[pltpu.VMEM((D_MODEL, D_MODEL), bf16),
                        pltpu.VMEM((D_MODEL, ROUTE_LANES), bf16), pltpu.VMEM((D_MODEL, ROUTE_LANES), bf16)],
        compiler_params=_cparams(("arbitrary",)),
        name="outproj_router",
    )(xp, xs, attp, atts, hyp, hys, mod, w_out, norm2_g.reshape(1, D_MODEL), wr, br)


PAIRS_PER_GROUP = 6
N_CLASSES = N_GROUPS * PAIRS_PER_GROUP
PAIR_SLOT_A = (0, 0, 0, 1, 1, 3)
PAIR_SLOT_B = (1, 2, 3, 3, 2, 2)
FLAG_NEW_A, FLAG_NEW_B, FLAG_TILE_START, FLAG_TILE_END = 1, 2, 4, 8


CLASS_ROWS = 32
TAB_TILE, TAB_EA, TAB_EB, TAB_LO, TAB_HI, TAB_FLAGS, TAB_N = range(7)


def _select_by(idx, values):
    out = jnp.full(idx.shape, float(values[-1]), f32)
    for i in range(len(values) - 2, -1, -1):
        out = jnp.where(idx == i, float(values[i]), out)
    return out


def _plan_kernel(e1_ref, e2_ref, pos_ref, tab_ref, *, tm):
    e1 = e1_ref[...]
    e2 = e2_ref[...]
    rows = e1.shape[0]
    grp = jnp.floor(e1 * (1.0 / EXPERTS_PER_GROUP))
    l1 = e1 - EXPERTS_PER_GROUP * grp
    l2 = e2 - EXPERTS_PER_GROUP * jnp.floor(e2 * (1.0 / EXPERTS_PER_GROUP))
    lo, hi = jnp.minimum(l1, l2), jnp.maximum(l1, l2)
    pair = jnp.where(lo == 0, hi - 1, jnp.where(lo == 1, jnp.where(hi == 3, 3.0, 4.0), 5.0))
    cls = grp * PAIRS_PER_GROUP + pair

    ri = lax.broadcasted_iota(jnp.int32, (LANES, LANES), 0)
    ci = lax.broadcasted_iota(jnp.int32, (LANES, LANES), 1)
    upper = (ri <= ci).astype(bf16)
    rr = lax.broadcasted_iota(jnp.int32, (rows, rows), 0)
    rc = lax.broadcasted_iota(jnp.int32, (rows, rows), 1)
    strict_lower = (rc < rr).astype(bf16)
    cid = lax.broadcasted_iota(jnp.int32, (CLASS_ROWS, 1), 0)

    pos = jnp.zeros(e1.shape, f32)
    base = jnp.zeros((1, 1), f32)
    cnt_col = jnp.zeros((CLASS_ROWS, 1), f32)
    off_col = jnp.zeros((CLASS_ROWS, 1), f32)
    for c in range(N_CLASSES):
        m = cls == c
        within = _dot(m.astype(bf16), upper)
        tot = jnp.broadcast_to(within[:, LANES - 1:LANES], within.shape)
        before = _dot(strict_lower, tot.astype(bf16))[:, 0:1]
        count = jnp.sum(within[:, LANES - 1:LANES], axis=0, keepdims=True)
        pos = jnp.where(m, base + before + within - 1.0, pos)
        cnt_col = jnp.where(cid == c, count, cnt_col)
        off_col = jnp.where(cid == c, base, off_col)
        base = base + count
    pos_ref[...] = pos.astype(jnp.int32)

    end_col = off_col + cnt_col
    inv_tm = 1.0 / tm
    first_col = jnp.floor(off_col * inv_tm)
    ntl_col = jnp.where(cnt_col > 0, jnp.floor((end_col - 1.0) * inv_tm) - first_col + 1.0, 0.0)
    cr = lax.broadcasted_iota(jnp.int32, (CLASS_ROWS, CLASS_ROWS), 0)
    cc = lax.broadcasted_iota(jnp.int32, (CLASS_ROWS, CLASS_ROWS), 1)
    lower = (cc <= cr).astype(bf16)
    iend_col = _dot(lower, jnp.broadcast_to(ntl_col, (CLASS_ROWS, LANES)).astype(bf16))[:, 0:1]
    istart_col = iend_col - ntl_col
    n_items = jnp.sum(ntl_col, axis=0, keepdims=True)
    slots = lax.broadcasted_iota(jnp.int32, (1, LANES), 1).astype(f32)
    items = jnp.minimum(slots, n_items - 1.0)
    past = jnp.where((items >= iend_col) & (cid < N_CLASSES), 1.0, 0.0)
    it_cls = jnp.minimum(jnp.sum(past, axis=0, keepdims=True), N_CLASSES - 1.0)
    sel = it_cls == cid.astype(f32)
    pick = lambda col: jnp.sum(jnp.where(sel, col, 0.0), axis=0, keepdims=True)
    grp_col = jnp.floor(cid.astype(f32) * (1.0 / PAIRS_PER_GROUP))
    pair_col = cid.astype(f32) - PAIRS_PER_GROUP * grp_col
    ea_col = EXPERTS_PER_GROUP * grp_col + _select_by(pair_col, PAIR_SLOT_A)
    eb_col = EXPERTS_PER_GROUP * grp_col + _select_by(pair_col, PAIR_SLOT_B)
    it_tile = pick(first_col) + items - pick(istart_col)
    it_lo = jnp.maximum(pick(off_col) - it_tile * tm, 0.0)
    it_hi = jnp.minimum(pick(end_col) - it_tile * tm, float(tm))
    it_ea, it_eb = pick(ea_col), pick(eb_col)
    prev = lambda v: pltpu.roll(jnp.broadcast_to(v, (SUBLANES, LANES)), 1, 1)[0:1]
    changed = lambda v: jnp.where((slots == 0) | (v != prev(v)), 1.0, 0.0)
    flags = (FLAG_NEW_A * changed(it_ea) + FLAG_NEW_B * changed(it_eb)
             + FLAG_TILE_START * jnp.where(it_lo == 0, 1.0, 0.0) + FLAG_TILE_END * jnp.where(it_hi == tm, 1.0, 0.0))
    table = {TAB_TILE: it_tile, TAB_EA: it_ea, TAB_EB: it_eb, TAB_LO: it_lo, TAB_HI: it_hi, TAB_FLAGS: flags,
             TAB_N: jnp.broadcast_to(n_items, (1, LANES))}
    trow = lax.broadcasted_iota(jnp.int32, tab_ref.shape, 0)
    tab = jnp.zeros(tab_ref.shape, f32)
    for r, v in table.items():
        tab = jnp.where(trow == r, v, tab)
    tab_ref[...] = tab.astype(jnp.int32)


def _route_plan(ee, tm, max_items):
    per = ee.shape[1] // 2
    t = ee.shape[0] * per * LANES
    assert max_items <= LANES and tm & (tm - 1) == 0
    pos, tab = pl.pallas_call(
        functools.partial(_plan_kernel, tm=tm),
        out_shape=(jax.ShapeDtypeStruct((t // LANES, LANES), jnp.int32),
                   jax.ShapeDtypeStruct((SUBLANES, LANES), jnp.int32)),
        name="moe_plan",
    )(ee[:, :per].reshape(t // LANES, LANES), ee[:, per:].reshape(t // LANES, LANES))
    row = lambda r: tab[r, :max_items]
    return dict(pos=pos.reshape(t), n_items=tab[TAB_N, :1], it_tile=row(TAB_TILE), it_ea=row(TAB_EA),
                it_eb=row(TAB_EB), it_lo=row(TAB_LO), it_hi=row(TAB_HI), it_flags=row(TAB_FLAGS))


ROW_DMA_UNROLL = 8
CHUNK_ALIGN_LOG2 = 6


def _row_dma_loop(g0, g1, make_copy, priorities):
    def body(i, c):
        for u in range(ROW_DMA_UNROLL):
            make_copy(i * ROW_DMA_UNROLL + u).start(priority=priorities[u % len(priorities)])
        return c

    lax.fori_loop(g0, g1, body, 0)


def _moe_ffn_kernel(pos_ref, it_tile, it_ea, it_eb, it_lo, it_hi, it_flags, n_items, tok_hbm,
                    wga_ref, wua_ref, wda_ref, wgb_ref, wub_ref, wdb_ref, out_hbm,
                    src_s, xbuf, ybuf, x_s, rt_s, acc_s, wa_s, wb_s, sem_in, sem_out, *, chunk, n_tiles):
    i = pl.program_id(0)
    tm = x_s.shape[0]
    groups = tm // ROW_DMA_UNROLL

    def gather_tile(k):
        base = k * tm
        slot = k % 2
        _row_dma_loop(0, groups, lambda r: pltpu.make_async_copy(
            tok_hbm.at[src_s[base + r]],
            xbuf.at[slot, pl.ds(pl.multiple_of(r * TOK_PITCH, TOK_PITCH), TOK_PITCH), :],
            sem_in.at[slot]), GATHER_PRIORITIES)

    def scatter_tile(k):
        base = k * tm
        slot = k % 2
        _row_dma_loop(0, groups, lambda r: pltpu.make_async_copy(
            ybuf.at[slot, pl.ds(pl.multiple_of(r * SUBLANES, SUBLANES), SUBLANES), :],
            out_hbm.at[src_s[base + r]],
            sem_out.at[slot]), SCATTER_PRIORITIES)

    def expert_ffn(x, w, wg, wu, wd):
        g = _dot(x, wg[...])
        u = _dot(x, wu[...])
        hid = (g * (1.0 / (1.0 + jnp.exp(-g)))) * u
        return _dot((hid * w).astype(bf16), wd[...])

    def wait_all(buf, sem, slot):
        pltpu.make_async_copy(buf.at[slot], buf.at[slot], sem.at[slot]).wait()

    @pl.when(i == 0)
    def _():
        def inv(a, c):
            for u in range(ROW_DMA_UNROLL):
                src_s[pos_ref[a * ROW_DMA_UNROLL + u]] = a * ROW_DMA_UNROLL + u
            return c

        lax.fori_loop(0, pos_ref.shape[0] // ROW_DMA_UNROLL, inv, 0)
        gather_tile(0)

    @pl.when(i < n_items[0])
    def _():
        flags = it_flags[i]
        k = it_tile[i]

        @pl.when((flags & FLAG_NEW_A) != 0)
        def _():
            for dst, src in zip(wa_s, (wga_ref, wua_ref, wda_ref)):
                dst[...] = src[0].astype(bf16)

        @pl.when((flags & FLAG_NEW_B) != 0)
        def _():
            for dst, src in zip(wb_s, (wgb_ref, wub_ref, wdb_ref)):
                dst[...] = src[0].astype(bf16)

        @pl.when((flags & FLAG_TILE_START) != 0)
        def _():
            @pl.when(k + 1 < n_tiles)
            def _():
                gather_tile(k + 1)

            wait_all(xbuf, sem_in, k % 2)
            x_s[...] = _load_token_tiles(xbuf, tm, (k % 2,), TOK_PITCH).astype(bf16)
            rt_s[...] = xbuf[k % 2, pl.ds(TOK_RT_ROW, tm, stride=TOK_PITCH), :]
            acc_s[...] = jnp.zeros_like(acc_s)

        lo = it_lo[i]
        hi = it_hi[i]
        ea = it_ea[i].astype(f32)

        base = lax.shift_left(lax.shift_right_logical(lo, CHUNK_ALIGN_LOG2), CHUNK_ALIGN_LOG2)
        n_chunks = lax.shift_right_logical(hi - base + (chunk - 1), chunk.bit_length() - 1)

        def body(j, c):
            r = pl.multiple_of(jnp.minimum(base + j * chunk, tm - chunk), 1 << CHUNK_ALIGN_LOG2)
            x = x_s[pl.ds(r, chunk), :]
            rt = rt_s[pl.ds(r, chunk), :]
            first_is_a = rt[:, ROUTE_E1:ROUTE_E1 + 1] == ea
            w1 = rt[:, ROUTE_W1:ROUTE_W1 + 1]
            w2 = rt[:, ROUTE_W2:ROUTE_W2 + 1]
            y = (expert_ffn(x, jnp.where(first_is_a, w1, w2), *wa_s)
                 + expert_ffn(x, jnp.where(first_is_a, w2, w1), *wb_s))
            row = lax.broadcasted_iota(jnp.int32, (chunk, 1), 0) + r
            mine = (row >= lo) & (row < hi)
            acc_s[pl.ds(r, chunk), :] = jnp.where(mine, y, acc_s[pl.ds(r, chunk), :])
            return c

        lax.fori_loop(0, n_chunks, body, 0)

        @pl.when((flags & FLAG_TILE_END) != 0)
        def _():
            @pl.when(k >= 2)
            def _():
                wait_all(ybuf, sem_out, k % 2)

            _store_token_tiles(ybuf.at[k % 2], acc_s[...])
            scatter_tile(k)

            @pl.when(k == n_tiles - 1)
            def _():
                if n_tiles > 1:
                    wait_all(ybuf, sem_out, (n_tiles - 2) % 2)
                wait_all(ybuf, sem_out, (n_tiles - 1) % 2)


def _moe_ffn(plan, tok, w_gate, w_up, w_down, tm, max_items):
    n_tok = tok.shape[0] // TOK_PITCH
    n_tiles = n_tok // tm
    tok3 = tok.reshape(n_tok, TOK_PITCH, LANES)
    spec_a = lambda shape: pl.BlockSpec((1,) + shape, lambda i, ps, tl, ea, eb, lo, hi, fl, n: (ea[i], 0, 0))
    spec_b = lambda shape: pl.BlockSpec((1,) + shape, lambda i, ps, tl, ea, eb, lo, hi, fl, n: (eb[i], 0, 0))
    shapes = ((D_MODEL, D_EXPERT), (D_MODEL, D_EXPERT), (D_EXPERT, D_MODEL))
    wscratch = lambda: tuple(pltpu.VMEM(s, bf16) for s in shapes)
    return pl.pallas_call(
        functools.partial(_moe_ffn_kernel, chunk=256, n_tiles=n_tiles),
        out_shape=jax.ShapeDtypeStruct((n_tok, SUBLANES, LANES), f32),
        grid_spec=pltpu.PrefetchScalarGridSpec(
            num_scalar_prefetch=8,
            grid=(max_items,),
            in_specs=[pl.BlockSpec(memory_space=pl.ANY)] + [spec_a(s) for s in shapes] + [spec_b(s) for s in shapes],
            out_specs=pl.BlockSpec(memory_space=pl.ANY),
            scratch_shapes=[pltpu.SMEM((n_tok,), jnp.int32),
                            pltpu.VMEM((2, tm * TOK_PITCH, LANES), f32),
                            pltpu.VMEM((2, tm * SUBLANES, LANES), f32),
                            pltpu.VMEM((tm, D_MODEL), bf16), pltpu.VMEM((tm, ROUTE_LANES), f32),
                            pltpu.VMEM((tm, D_MODEL), f32), wscratch(), wscratch(),
                            pltpu.SemaphoreType.DMA((2,)), pltpu.SemaphoreType.DMA((2,))]),
        compiler_params=_cparams(("arbitrary",), 60 * 1024 * 1024),
        name="moe_ffn",
    )(plan['pos'], plan['it_tile'], plan['it_ea'], plan['it_eb'], plan['it_lo'], plan['it_hi'],
      plan['it_flags'], plan['n_items'], tok3, w_gate, w_up, w_down, w_gate, w_up, w_down)


def _final_kernel(moe_ref, x1_ref, mod_ref, fg_ref, yp_ref, yl_ref, *, prompt_tiles):
    i = pl.program_id(0)
    tm = x1_ref.shape[0]
    y = _rms(x1_ref[...] + mod_ref[0, GT2:GT2 + 1, :] * _load_token_tiles(moe_ref, tm), fg_ref[...])

    @pl.when(i < prompt_tiles)
    def _():
        yp_ref[...] = y

    @pl.when(i >= prompt_tiles)
    def _():
        yl_ref[...] = y


def _final(moe, x1, mod, final_g, t_prompt, t_lat, seq_s):
    tm = 512
    npt, nst = t_prompt // tm, t_lat // tm
    spb = seq_s // tm
    moe2 = moe.reshape(moe.shape[0] * SUBLANES, LANES)
    return pl.pallas_call(
        functools.partial(_final_kernel, prompt_tiles=npt),
        out_shape=(jax.ShapeDtypeStruct((t_prompt, D_MODEL), f32),
                   jax.ShapeDtypeStruct((t_lat, D_MODEL), f32)),
        grid=(npt + nst,),
        in_specs=[pl.BlockSpec((tm * SUBLANES, LANES), lambda i: (i, 0)),
                  pl.BlockSpec((tm, D_MODEL), lambda i: (i, 0)),
                  pl.BlockSpec((1, MOD_ROWS, D_MODEL),
                               lambda i: (jnp.where(i < npt, 0, 1 + jnp.maximum(i - npt, 0) // spb), 0, 0)),
                  pl.BlockSpec((1, D_MODEL), lambda i: (0, 0))],
        out_specs=(pl.BlockSpec((tm, D_MODEL), lambda i: (jnp.minimum(i, npt - 1), 0)),
                   pl.BlockSpec((tm, D_MODEL), lambda i: (jnp.maximum(i - npt, 0), 0))),
        compiler_params=_cparams(("arbitrary",)),
        name="moe_combine_final",
    )(moe2, x1, mod, final_g.reshape(1, D_MODEL))


def _moe(tok, ee, x1, mod, w_gate, w_up, w_down, final_g, t_prompt, t_lat, seq_s):
    tm = MOE_TM
    n_rows = t_prompt + t_lat
    assert n_rows % tm == 0
    max_items = n_rows // tm + N_CLASSES
    plan = _route_plan(ee, tm, max_items)
    moe = _moe_ffn(plan, tok, w_gate, w_up, w_down, tm, max_items)
    return _final(moe, x1, mod, final_g, t_prompt, t_lat, seq_s)


def kernel(x_prompt, x_sample, cache_k, cache_v, c, c_ctx, w_ada, b_ada, norm1_g, w_in, rpb, conv_w, conv_b, filt_w1, filt_b1, filt_w2, filt_b2, filt_w3, filt_freq, hyena_skip, gnorm_att, gnorm_hyena, w_out, norm2_g, router_grp_w, router_grp_b, router_exp_w, router_exp_b, w_gate, w_up, w_down, final_g):
    depth = w_ada.shape[0]
    assert depth == 1
    batch, seq, _ = x_prompt.shape
    dec_batch, dec_seq, _ = x_sample.shape
    l = 0

    def pack_router(grp, exp):
        rows = grp.shape[0]
        return jnp.concatenate([grp, jnp.zeros((rows, ROUTE_EXP_LANE0 - N_GROUPS), f32), exp,
                                jnp.zeros((rows, ROUTE_LANES - ROUTE_EXP_LANE0 - N_EXPERTS), f32)], axis=1)

    wr = pack_router(router_grp_w[l], router_exp_w[l])
    br = pack_router(router_grp_b[l][None, :], router_exp_b[l][None, :])

    lp = {
        'norm1_g': norm1_g[l], 'w_in': w_in[l], 'conv_w': conv_w[l], 'conv_b': conv_b[l],
        'filt_w1': filt_w1[l], 'filt_b1': filt_b1[l], 'filt_w2': filt_w2[l], 'filt_b2': filt_b2[l],
        'filt_w3': filt_w3[l], 'filt_freq': filt_freq[l], 'hyena_skip': hyena_skip[l],
        'gnorm_hyena': gnorm_hyena[l], 'w_out': w_out[l], 'norm2_g': norm2_g[l],
        'wr': wr, 'br': br, 'w_gate': w_gate[l], 'w_up': w_up[l], 'w_down': w_down[l],
    }

    cond8 = jnp.concatenate([c_ctx[None, :], c, jnp.zeros((SUBLANES - 1 - dec_batch, D_MODEL), f32)], axis=0)
    mod = _ada_mod(cond8, w_ada[l], b_ada[l], 1 + dec_batch).reshape(SUBLANES, N_MOD, D_MODEL)
    mod = jnp.pad(mod, ((0, 0), (0, MOD_ROWS - N_MOD), (0, 0)))
    mod_ctx, mod_lat = mod[0:1], mod[1:1 + dec_batch]

    xp = x_prompt.reshape(batch * seq, D_MODEL)
    xs = x_sample.reshape(dec_batch * dec_seq, D_MODEL)

    qp, k_ctx, v_ctx, *front_p = _inproj_conv(xp, mod_ctx, lp['norm1_g'], lp['w_in'], lp['conv_w'], lp['conv_b'],
                                              seq, f32)
    attp = _ctx_attention(qp, k_ctx, v_ctx, gnorm_att[l], seq)
    hyop = _hyena_long(*front_p, lp, seq)

    ql, kl, vl, hyl = _inproj(xs, mod_lat, lp['norm1_g'], lp['w_in'], dec_seq, bf16)
    kc = cache_k[:, l].reshape(dec_batch * cache_k.shape[2], D_ATT).astype(bf16)
    vc = cache_v[:, l].reshape(dec_batch * cache_v.shape[2], D_ATT).astype(bf16)
    attl = _na_attention(ql, kl, vl, kc, vc, rpb[l], gnorm_att[l], dec_batch, dec_seq)
    hyol = _hyena(hyl, lp, dec_batch, dec_seq)

    x1, tok, ee = _outproj(xp, xs, attp, attl, hyop, hyol, mod[0:1 + dec_batch], lp['w_out'], lp['norm2_g'],
                           lp['wr'], lp['br'], seq, dec_seq)
    yp, ys = _moe(tok, ee, x1, mod[0:1 + dec_batch], lp['w_gate'], lp['w_up'], lp['w_down'], final_g,
                  batch * seq, dec_batch * dec_seq, dec_seq)

    y_prompt = yp.reshape(batch, seq, D_MODEL)
    y_sample = ys.reshape(dec_batch, dec_seq, D_MODEL)
    new_k = k_ctx.reshape(batch, 1, seq, H_ATT, HEAD_DIM)
    new_v = v_ctx.reshape(batch, 1, seq, H_ATT, HEAD_DIM)
    return (y_prompt, y_sample, new_k, new_v)
```

```python
import functools
import math

import jax
import jax.numpy as jnp
import numpy as np
from jax import lax
from jax.experimental import pallas as pl
from jax.experimental.pallas import tpu as pltpu

f32 = jnp.float32
bf16 = jnp.bfloat16
HIGHEST = lax.Precision.HIGHEST

D_MODEL = 1024
GRID_W = 64
H_ATT = 8
HEAD_DIM = 64
D_ATT = H_ATT * HEAD_DIM
D_HYENA = 512
D_IN = 3 * D_ATT + 3 * D_HYENA
NA_ROWS = 8
NA_COLS = 16
SHORT_CONV = 3
FILTER_BANDS = 16
EMB_DIM = 1 + 2 * FILTER_BANDS
FILTER_FF = 64
DECAY_TARGET = 1e-2
MIN_DECAY = math.log(DECAY_TARGET) / 1.5
MAX_DECAY = math.log(DECAY_TARGET) / 0.3
N_GROUPS = 4
EXPERTS_PER_GROUP = 4
N_EXPERTS = N_GROUPS * EXPERTS_PER_GROUP
D_EXPERT = 512
N_MOD = 6
EPS = 1e-6
NEG_INF = -1e30
ATT_SCALE = HEAD_DIM ** -0.5

LANES = 128
SUBLANES = 8
MOD_ROWS = 8
ROUTE_LANES = 128
ROUTE_EXP_LANE0 = 16
ROUTE_E1, ROUTE_E2, ROUTE_W1, ROUTE_W2 = 0, 1, 2, 3
TOK_PITCH = 16
TOK_RT_ROW = 8
MOE_TM = 512
GATHER_PRIORITIES = (0,)
SCATTER_PRIORITIES = (1,)
VMEM_LIMIT = 56 * 1024 * 1024

SH1, SC1, GT1, SH2, SC2, GT2 = range(6)


def _cparams(sem, vmem=VMEM_LIMIT):
    return pltpu.CompilerParams(dimension_semantics=sem, vmem_limit_bytes=vmem)


def _dot(a, b):
    return jnp.dot(a, b, preferred_element_type=f32)


def _dot_hi(a, b):
    return lax.dot_general(a, b, (((1,), (0,)), ((), ())), precision=HIGHEST,
                           preferred_element_type=f32)


def _dot_nt(a, b):
    return lax.dot_general(a, b, (((1,), (1,)), ((), ())), preferred_element_type=f32)


def _rms(x, g):
    ms = jnp.mean(x * x, axis=-1, keepdims=True)
    return x * lax.rsqrt(ms + EPS) * g


def _cast_rows(src_ref, dst_ref, chunk):
    n = src_ref.shape[0] // chunk

    def body(i, c):
        r = pl.multiple_of(i * chunk, chunk)
        dst_ref[pl.ds(r, chunk), :] = src_ref[pl.ds(r, chunk), :].astype(dst_ref.dtype)
        return c

    lax.fori_loop(0, n, body, 0)


def _ada_kernel(ct_ref, w_ref, b_ref, o_ref, *, n_cond):
    ct = ct_ref[...]
    st = ct * (1.0 / (1.0 + jnp.exp(-ct)))
    w = w_ref[...]
    rid = lax.broadcasted_iota(jnp.int32, o_ref.shape, 0)
    out = jnp.broadcast_to(b_ref[...], o_ref.shape)
    for m in range(n_cond):
        row = jnp.sum(w * st[:, m:m + 1], axis=0, keepdims=True)
        out = out + jnp.where(rid == m, row, 0.0)
    o_ref[...] = out


def _ada_mod(cond8, w_ada, b_ada, n_cond):
    tn = 1536
    n = N_MOD * D_MODEL
    return pl.pallas_call(
        functools.partial(_ada_kernel, n_cond=n_cond),
        out_shape=jax.ShapeDtypeStruct((SUBLANES, n), f32),
        grid=(n // tn,),
        in_specs=[pl.BlockSpec((D_MODEL, SUBLANES), lambda j: (0, 0)),
                  pl.BlockSpec((D_MODEL, tn), lambda j: (0, j)),
                  pl.BlockSpec((1, tn), lambda j: (0, j))],
        out_specs=pl.BlockSpec((SUBLANES, tn), lambda j: (0, j)),
        compiler_params=_cparams(("arbitrary",)),
        name="ada_mod",
    )(cond8.T, w_ada, b_ada.reshape(1, n))


def _short_conv_gate(u, w_ref, b_ref, prev_row=0.0, next_row=0.0):
    seq = u.shape[0]
    row = lax.broadcasted_iota(jnp.int32, u.shape, 0)
    up = jnp.where(row == 0, prev_row, pltpu.roll(u, 1, 0))
    un = jnp.where(row == seq - 1, next_row, pltpu.roll(u, seq - 1, 0))
    uc = b_ref[...] + up * w_ref[0:1, :]
    uc = uc + u * w_ref[1:2, :]
    uc = uc + un * w_ref[2:3, :]
    return uc[:, 2 * D_HYENA:] * uc[:, D_HYENA:2 * D_HYENA], uc[:, :D_HYENA]


def _inproj_conv_kernel(x_ref, mod_ref, g_ref, w_ref, cw_ref, cb_ref, q_ref, k_ref, v_ref, zbf_ref, z_ref, x0c_ref,
                        wbf_ref, *, seq):
    @pl.when(pl.program_id(0) == 0)
    def _():
        _cast_rows(w_ref, wbf_ref, 128)

    h = _rms(x_ref[...], g_ref[...])
    h = h * (1.0 + mod_ref[0, SC1:SC1 + 1, :]) + mod_ref[0, SH1:SH1 + 1, :]
    p = _dot(h.astype(bf16), wbf_ref[...])
    q_ref[...] = p[:, 0:D_ATT].astype(q_ref.dtype)
    k_ref[...] = p[:, D_ATT:2 * D_ATT].astype(k_ref.dtype)
    v_ref[...] = p[:, 2 * D_ATT:3 * D_ATT].astype(v_ref.dtype)
    for b in range(x_ref.shape[0] // seq):
        z, x0c = _short_conv_gate(p[b * seq:(b + 1) * seq, 3 * D_ATT:], cw_ref, cb_ref)
        cs = slice(b * D_HYENA, (b + 1) * D_HYENA)
        z_ref[:, cs] = z
        zbf_ref[:, cs] = z.astype(bf16)
        x0c_ref[:, cs] = x0c


def _inproj_conv(x, mod, norm_g, w_in, conv_w, conv_b, seq, kv_dtype):
    t = x.shape[0]
    tm = 512
    assert tm % seq == 0 and mod.shape[0] == 1
    bpt = tm // seq
    n = (t // seq) * D_HYENA
    tok = lambda w: pl.BlockSpec((tm, w), lambda i: (i, 0))
    seqm = pl.BlockSpec((seq, bpt * D_HYENA), lambda i: (0, i))
    const = lambda shape, **kw: pl.BlockSpec(shape, lambda i: (0,) * len(shape), **kw)
    return pl.pallas_call(
        functools.partial(_inproj_conv_kernel, seq=seq),
        out_shape=(jax.ShapeDtypeStruct((t, D_ATT), bf16),
                   jax.ShapeDtypeStruct((t, D_ATT), kv_dtype),
                   jax.ShapeDtypeStruct((t, D_ATT), kv_dtype),
                   jax.ShapeDtypeStruct((seq, n), bf16),
                   jax.ShapeDtypeStruct((seq, n), f32),
                   jax.ShapeDtypeStruct((seq, n), f32)),
        grid=(t // tm,),
        in_specs=[tok(D_MODEL), const((1, MOD_ROWS, D_MODEL)), const((1, D_MODEL)),
                  const((D_MODEL, D_IN), pipeline_mode=pl.Buffered(1)),
                  const((SHORT_CONV, 3 * D_HYENA)), const((1, 3 * D_HYENA))],
        out_specs=(tok(D_ATT), tok(D_ATT), tok(D_ATT), seqm, seqm, seqm),
        scratch_shapes=[pltpu.VMEM((D_MODEL, D_IN), bf16)],
        compiler_params=_cparams(("arbitrary",)),
        name="inproj_conv",
    )(x, mod, norm_g.reshape(1, D_MODEL), w_in, conv_w, conv_b.reshape(1, 3 * D_HYENA))


def _inproj_conv_halo_kernel(x_ref, xprev_ref, xnext_ref, mod_ref, g_ref, w_ref, cw_ref, cb_ref,
                             q_ref, k_ref, v_ref, zbf_ref, z_ref, x0c_ref, wbf_ref, *, tiles_per_seq):
    i = pl.program_id(0)

    @pl.when(i == 0)
    def _():
        _cast_rows(w_ref, wbf_ref, 128)

    def project(xv, cols):
        h = _rms(xv, g_ref[...])
        h = h * (1.0 + mod_ref[0, SC1:SC1 + 1, :]) + mod_ref[0, SH1:SH1 + 1, :]
        return _dot(h.astype(bf16), wbf_ref[:, cols])

    p = project(x_ref[...], slice(None))
    q_ref[...] = p[:, 0:D_ATT].astype(q_ref.dtype)
    k_ref[...] = p[:, D_ATT:2 * D_ATT].astype(k_ref.dtype)
    v_ref[...] = p[:, 2 * D_ATT:3 * D_ATT].astype(v_ref.dtype)
    halo = project(jnp.concatenate([xprev_ref[...], xnext_ref[...]], axis=0), slice(3 * D_ATT, D_IN))
    pos = lax.rem(i, tiles_per_seq)
    prev_row = jnp.where(pos == 0, 0.0, halo[SUBLANES - 1:SUBLANES, :])
    next_row = jnp.where(pos == tiles_per_seq - 1, 0.0, halo[SUBLANES:SUBLANES + 1, :])
    z, x0c = _short_conv_gate(p[:, 3 * D_ATT:], cw_ref, cb_ref, prev_row, next_row)
    z_ref[...] = z
    zbf_ref[...] = z.astype(bf16)
    x0c_ref[...] = x0c


def _inproj_conv_halo(x, mod, norm_g, w_in, conv_w, conv_b, seq, kv_dtype):
    t = x.shape[0]
    tm = 512
    assert seq % tm == 0 and mod.shape[0] == t // seq
    tps = seq // tm
    n = (t // seq) * D_HYENA
    last8 = t // SUBLANES - 1
    tok = lambda w: pl.BlockSpec((tm, w), lambda i: (i, 0))
    seqm = pl.BlockSpec((tm, D_HYENA), lambda i: (i % tps, i // tps))
    const = lambda shape, **kw: pl.BlockSpec(shape, lambda i: (0,) * len(shape), **kw)
    return pl.pallas_call(
        functools.partial(_inproj_conv_halo_kernel, tiles_per_seq=tps),
        out_shape=(jax.ShapeDtypeStruct((t, D_ATT), bf16),
                   jax.ShapeDtypeStruct((t, D_ATT), kv_dtype),
                   jax.ShapeDtypeStruct((t, D_ATT), kv_dtype),
                   jax.ShapeDtypeStruct((seq, n), bf16),
                   jax.ShapeDtypeStruct((seq, n), f32),
                   jax.ShapeDtypeStruct((seq, n), f32)),
        grid=(t // tm,),
        in_specs=[tok(D_MODEL),
                  pl.BlockSpec((SUBLANES, D_MODEL), lambda i: (jnp.maximum(i * (tm // SUBLANES) - 1, 0), 0)),
                  pl.BlockSpec((SUBLANES, D_MODEL), lambda i: (jnp.minimum((i + 1) * (tm // SUBLANES), last8), 0)),
                  pl.BlockSpec((1, MOD_ROWS, D_MODEL), lambda i: (i // tps, 0, 0)),
                  const((1, D_MODEL)), const((D_MODEL, D_IN), pipeline_mode=pl.Buffered(1)),
                  const((SHORT_CONV, 3 * D_HYENA)), const((1, 3 * D_HYENA))],
        out_specs=(tok(D_ATT), tok(D_ATT), tok(D_ATT), seqm, seqm, seqm),
        scratch_shapes=[pltpu.VMEM((D_MODEL, D_IN), bf16)],
        compiler_params=_cparams(("arbitrary",)),
        name="inproj_conv_halo",
    )(x, x, x, mod, norm_g.reshape(1, D_MODEL), w_in, conv_w, conv_b.reshape(1, 3 * D_HYENA))


def _split_heads(q2):
    lane = lax.broadcasted_iota(jnp.int32, q2.shape, 1)
    qa = jnp.where(lane < HEAD_DIM, q2, 0.0)
    qb = jnp.where(lane >= HEAD_DIM, q2, 0.0)
    return jnp.concatenate([qa, qb], axis=0)


def _merge_heads(o_ab):
    m = o_ab.shape[0] // 2
    lane = lax.broadcasted_iota(jnp.int32, (m, LANES), 1)
    return jnp.where(lane < HEAD_DIM, o_ab[:m], o_ab[m:])


CTX_SEQS_PER_STEP = 2


def _ctx_attn_kernel(q_ref, k_ref, v_ref, g_ref, o_ref, *, seq):
    for b in range(q_ref.shape[0] // seq):
        rows = slice(b * seq, (b + 1) * seq)
        outs = []
        for p in range(D_ATT // LANES):
            cs = slice(p * LANES, (p + 1) * LANES)
            qq = _split_heads(q_ref[rows, cs] * ATT_SCALE).astype(bf16)
            s = _dot_nt(qq, k_ref[rows, cs].astype(bf16))
            m = jnp.max(s, axis=-1, keepdims=True)
            e = jnp.exp(s - m)
            l = jnp.sum(e, axis=-1, keepdims=True)
            o_ab = _dot(e.astype(bf16), v_ref[rows, cs].astype(bf16)) / l
            outs.append(_merge_heads(o_ab))
        o_ref[rows, :] = _rms(jnp.concatenate(outs, axis=-1), g_ref[...]).astype(o_ref.dtype)


def _ctx_attention(q, k, v, gnorm, seq):
    t = q.shape[0]
    tm = CTX_SEQS_PER_STEP * seq
    assert t % tm == 0
    spec = pl.BlockSpec((tm, D_ATT), lambda b: (b, 0))
    return pl.pallas_call(
        functools.partial(_ctx_attn_kernel, seq=seq),
        out_shape=jax.ShapeDtypeStruct((t, D_ATT), bf16),
        grid=(t // tm,),
        in_specs=[spec, spec, spec, pl.BlockSpec((1, D_ATT), lambda b: (0, 0))],
        out_specs=spec,
        compiler_params=_cparams(("arbitrary",)),
        name="ctx_attn",
    )(q, k, v, gnorm.reshape(1, D_ATT))


def _na_tables():
    col = np.arange(GRID_W)
    cs = np.clip(col - NA_COLS // 2, 0, GRID_W - NA_COLS)
    col_mask = (col[None, :] >= cs[:, None]) & (col[None, :] < cs[:, None] + NA_COLS)
    mask = np.tile(col_mask.astype(np.float32), (1, NA_ROWS))
    return mask


N_DR = 2 * NA_ROWS - 1
N_DC = 2 * NA_COLS - 1
BIAS_PAIRS = N_DR - 1


def _na_bias_rows(rpb):
    out = jnp.zeros((H_ATT, BIAS_PAIRS, LANES), f32)
    out = out.at[:, :, 0:N_DC].set(rpb[:, 0:BIAS_PAIRS])
    return out.at[:, :, GRID_W:GRID_W + N_DC].set(rpb[:, 1:N_DR])


NA_ROWS_PER_STEP = 8


def _na_row_start(r, rows):
    return jnp.clip(r - NA_ROWS // 2, 0, rows - NA_ROWS)


def _na_attn_kernel(q_ref, k_ref, v_ref, kc_ref, vc_ref, rp_ref, mask_ref, g_ref, o_ref,
                    t2_ref, *, rows):
    b = pl.program_id(0)
    r = pl.program_id(1)

    @pl.when((b == 0) & (r == 0))
    def _():
        valid = mask_ref[:, 0:LANES] != 0.0
        for h in range(H_ATT):
            for i in range(BIAS_PAIRS):
                v = jnp.broadcast_to(rp_ref[h, i:i + 1, :], (GRID_W, LANES))
                v = pltpu.roll(v, LANES - (NA_COLS - 1), 1, stride=1, stride_axis=0)
                t2_ref[h * BIAS_PAIRS + i] = jnp.where(valid, v, NEG_INF)

    nwin = NA_ROWS * GRID_W
    for rr in range(NA_ROWS_PER_STEP):
        row = r * NA_ROWS_PER_STEP + rr
        qs = slice(rr * GRID_W, (rr + 1) * GRID_W)
        rs = _na_row_start(row, rows)
        start = pl.multiple_of(rs * GRID_W, GRID_W)
        i0 = rs - row + NA_ROWS - 1
        outs = []
        for p in range(D_ATT // LANES):
            cs = slice(p * LANES, (p + 1) * LANES)
            qq = _split_heads(q_ref[qs, cs] * ATT_SCALE).astype(bf16)
            kw = k_ref[pl.ds(start, nwin), cs]
            vw = v_ref[pl.ds(start, nwin), cs]
            s_lat = _dot_nt(qq, kw)
            s_ctx = _dot_nt(qq, kc_ref[:, cs])
            bias2 = jnp.concatenate(
                [jnp.concatenate([t2_ref[(2 * p + hh) * BIAS_PAIRS + i0 + 2 * jp] for hh in range(2)], axis=0)
                 for jp in range(NA_ROWS // 2)], axis=-1)
            s_lat = s_lat + bias2
            m = jnp.maximum(jnp.max(s_lat, axis=-1, keepdims=True), jnp.max(s_ctx, axis=-1, keepdims=True))
            e_lat = jnp.exp(s_lat - m)
            e_ctx = jnp.exp(s_ctx - m)
            l = jnp.sum(e_lat, axis=-1, keepdims=True) + jnp.sum(e_ctx, axis=-1, keepdims=True)
            o_ab = (_dot(e_lat.astype(bf16), vw) + _dot(e_ctx.astype(bf16), vc_ref[:, cs])) / l
            outs.append(_merge_heads(o_ab))
        o_ref[qs, :] = _rms(jnp.concatenate(outs, axis=-1), g_ref[...]).astype(o_ref.dtype)


def _na_attention(q, k, v, kc, vc, rpb, gnorm, nb, seq):
    rows = seq // GRID_W
    assert rows % NA_ROWS_PER_STEP == 0
    steps = rows // NA_ROWS_PER_STEP
    qrows = NA_ROWS_PER_STEP * GRID_W
    past = kc.shape[0] // nb
    mask = _na_tables()
    return pl.pallas_call(
        functools.partial(_na_attn_kernel, rows=rows),
        out_shape=jax.ShapeDtypeStruct((nb * seq, D_ATT), bf16),
        grid=(nb, steps),
        in_specs=[pl.BlockSpec((qrows, D_ATT), lambda b, r: (b * steps + r, 0)),
                  pl.BlockSpec((seq, D_ATT), lambda b, r: (b, 0)),
                  pl.BlockSpec((seq, D_ATT), lambda b, r: (b, 0)),
                  pl.BlockSpec((past, D_ATT), lambda b, r: (b, 0)),
                  pl.BlockSpec((past, D_ATT), lambda b, r: (b, 0)),
                  pl.BlockSpec((H_ATT, BIAS_PAIRS, LANES), lambda b, r: (0, 0, 0)),
                  pl.BlockSpec((GRID_W, NA_ROWS * GRID_W), lambda b, r: (0, 0)),
                  pl.BlockSpec((1, D_ATT), lambda b, r: (0, 0))],
        out_specs=pl.BlockSpec((qrows, D_ATT), lambda b, r: (b * steps + r, 0)),
        scratch_shapes=[pltpu.VMEM((H_ATT * BIAS_PAIRS, GRID_W, LANES), f32)],
        compiler_params=_cparams(("arbitrary", "arbitrary")),
        name="na_attn",
    )(q, k, v, kc, vc, _na_bias_rows(rpb), jnp.asarray(mask), gnorm.reshape(1, D_ATT))


def _filter_features(seq):
    t = np.linspace(0.0, 1.0, seq, dtype=np.float64)[:, None]
    w = 2.0 * math.pi * np.arange(seq, dtype=np.float64)[:, None] / seq
    fb = np.linspace(1e-4, FILTER_BANDS - 1, FILTER_BANDS, dtype=np.float64)[None, :]
    ang = fb * w
    z = np.concatenate([t, np.cos(ang), -np.sin(ang)], axis=-1).astype(np.float32)
    return np.pad(z, ((0, 0), (0, LANES - EMB_DIM)))


def _filt_kernel(zt_ref, w1t_ref, b1_ref, fr_ref, w2t_ref, b2_ref, w3_ref, dl_ref, h_ref, kl_ref, *, seq):
    i = pl.program_id(0)
    tr = zt_ref.shape[1]
    fr = fr_ref[...]
    h = jnp.sin(fr * (_dot_hi(w1t_ref[...], zt_ref[...]) + b1_ref[...]))
    h = jnp.sin(fr * (_dot_hi(w2t_ref[...], h) + b2_ref[...]))
    h = _dot_hi(h.T, w3_ref[...])
    row = lax.broadcasted_iota(jnp.int32, (tr, D_HYENA), 0) + i * tr
    t = row[:, 0:1].astype(f32) * (1.0 / (seq - 1))
    decay = jnp.exp(-t * dl_ref[...])
    hf = h[:, :D_HYENA] * decay
    hb = jnp.where(row == 0, 0.0, h[:, D_HYENA:] * decay)
    h_ref[:, :D_HYENA] = (hf + hb).astype(bf16)
    h_ref[:, D_HYENA:] = (hb - hf).astype(bf16)
    alt = (1 - 2 * (row & 1)).astype(f32)
    part = jnp.sum(alt * (hf + hb), axis=0, keepdims=True)

    @pl.when(i == 0)
    def _():
        kl_ref[...] = jnp.zeros_like(kl_ref)

    kl_ref[...] += jnp.broadcast_to(part, kl_ref.shape)


def _hy_filters(seq, w1, b1, w2, b2, w3, freq):
    tr = 256
    zt = jnp.asarray(np.ascontiguousarray(_filter_features(seq).T))
    deltas = np.abs(np.linspace(MIN_DECAY, MAX_DECAY, D_HYENA, dtype=np.float64))[None, :].astype(np.float32)
    w1t = jnp.pad(w1, ((0, LANES - EMB_DIM), (0, 0))).T
    const = lambda shape: pl.BlockSpec(shape, lambda i: (0, 0))
    col = lambda v: v.reshape(FILTER_FF, 1)
    return pl.pallas_call(
        functools.partial(_filt_kernel, seq=seq),
        out_shape=(jax.ShapeDtypeStruct((seq, 2 * D_HYENA), bf16),
                   jax.ShapeDtypeStruct((SUBLANES, D_HYENA), f32)),
        grid=(seq // tr,),
        in_specs=[pl.BlockSpec((LANES, tr), lambda i: (0, i)),
                  const((FILTER_FF, LANES)), const((FILTER_FF, 1)), const((FILTER_FF, 1)),
                  const((FILTER_FF, FILTER_FF)), const((FILTER_FF, 1)),
                  const((FILTER_FF, 2 * D_HYENA)), const((1, D_HYENA))],
        out_specs=(pl.BlockSpec((tr, 2 * D_HYENA), lambda i: (i, 0)),
                   pl.BlockSpec((SUBLANES, D_HYENA), lambda i: (0, 0))),
        compiler_params=_cparams(("arbitrary",)),
        name="hyena_filters",
    )(zt, w1t, col(b1), col(freq), w2.T, col(b2), w3, jnp.asarray(deltas))


def _dft_mats(seq):
    n = 2 * seq
    ph = (np.arange(seq, dtype=np.int64)[:, None] * np.arange(seq, dtype=np.int64)[None, :]) % n
    ang = ph.astype(np.float64) * (2.0 * math.pi / n)
    return np.cos(ang).astype(np.float32), np.sin(ang).astype(np.float32)


def _alt_col(rows, offset):
    row = lax.broadcasted_iota(jnp.int32, (rows, 1), 0) + offset
    return (1 - 2 * (row & 1)).astype(f32)


def _hy_fwd_kernel(fr_ref, fi_ref, z_ref, h_ref, kl_ref, yr_ref, yi_ref, yl_ref, kr_s, ki_s, *, n):
    i = pl.program_id(0)
    j = pl.program_id(1)
    tf = fr_ref.shape[0]
    tn = z_ref.shape[1]
    frb = fr_ref[...].astype(bf16)
    fib = fi_ref[...].astype(bf16)

    @pl.when(j == 0)
    def _():
        f = lax.broadcasted_iota(jnp.int32, (tf, 1), 0) + i * tf
        cf = jnp.where(f == 0, 1.0 / n, 2.0 / n)
        kr_s[...] = _dot(frb, h_ref[:, :D_HYENA]) * cf
        ki_s[...] = _dot(fib, h_ref[:, D_HYENA:]) * cf

    a = _dot(frb, z_ref[...])
    b = _dot(fib, z_ref[...])
    kr = kr_s[...]
    ki = ki_s[...]
    for c in range(tn // D_HYENA):
        cs = slice(c * D_HYENA, (c + 1) * D_HYENA)
        yr_ref[:, cs] = (a[:, cs] * kr + b[:, cs] * ki).astype(bf16)
        yi_ref[:, cs] = (b[:, cs] * kr - a[:, cs] * ki).astype(bf16)

    @pl.when(i == 0)
    def _():
        alt = _alt_col(z_ref.shape[0], 0)
        nz = jnp.sum(z_ref[...].astype(f32) * alt, axis=0, keepdims=True)
        kl = jnp.concatenate([kl_ref[0:1, :]] * (tn // D_HYENA), axis=-1)
        yl_ref[...] = jnp.broadcast_to(nz * kl * (1.0 / n), yl_ref.shape)


def _hy_fwd(fr, fi, zbf, hcat, kl, seq):
    n_cols = zbf.shape[1]
    tf = min(seq, 512)
    tn = min(n_cols, 2048)
    ni, nj = seq // tf, n_cols // tn
    assert ni == 1 or nj == 1
    return pl.pallas_call(
        functools.partial(_hy_fwd_kernel, n=2 * seq),
        out_shape=(jax.ShapeDtypeStruct((seq, n_cols), bf16),
                   jax.ShapeDtypeStruct((seq, n_cols), bf16),
                   jax.ShapeDtypeStruct((SUBLANES, n_cols), f32)),
        grid=(ni, nj),
        in_specs=[pl.BlockSpec((tf, seq), lambda i, j: (i, 0)),
                  pl.BlockSpec((tf, seq), lambda i, j: (i, 0)),
                  pl.BlockSpec((seq, tn), lambda i, j: (0, j)),
                  pl.BlockSpec((seq, 2 * D_HYENA), lambda i, j: (0, 0)),
                  pl.BlockSpec((SUBLANES, D_HYENA), lambda i, j: (0, 0))],
        out_specs=(pl.BlockSpec((tf, tn), lambda i, j: (i, j)),
                   pl.BlockSpec((tf, tn), lambda i, j: (i, j)),
                   pl.BlockSpec((SUBLANES, tn), lambda i, j: (0, j))),
        scratch_shapes=[pltpu.VMEM((tf, D_HYENA), f32), pltpu.VMEM((tf, D_HYENA), f32)],
        compiler_params=_cparams(("arbitrary", "arbitrary")),
        name="hyena_dft_fwd",
    )(fr, fi, zbf, hcat, kl)


def _hy_inv_kernel(fr_ref, fi_ref, yr_ref, yi_ref, yl_ref, z_ref, x0_ref, skip_ref, g_ref, o_ref):
    tt = fr_ref.shape[0]
    tn = yr_ref.shape[1]
    y = _dot(fr_ref[...].astype(bf16), yr_ref[...]) + _dot(fi_ref[...].astype(bf16), yi_ref[...])
    alt = _alt_col(tt, pl.program_id(0) * tt)
    for c in range(tn // D_HYENA):
        cs = slice(c * D_HYENA, (c + 1) * D_HYENA)
        yc = y[:, cs] + alt * yl_ref[0:1, cs] + z_ref[:, cs] * skip_ref[...]
        o_ref[:, cs] = _rms(yc * x0_ref[:, cs], g_ref[...]).astype(o_ref.dtype)


def _hy_inv(fr, fi, yr, yi, yl, z, x0c, skip, gnorm, seq):
    n_cols = z.shape[1]
    tt = min(seq, 512)
    tn = min(n_cols, 2048)
    blk = pl.BlockSpec((tt, tn), lambda i, j: (i, j))
    return pl.pallas_call(
        _hy_inv_kernel,
        out_shape=jax.ShapeDtypeStruct((seq, n_cols), bf16),
        grid=(seq // tt, n_cols // tn),
        in_specs=[pl.BlockSpec((tt, seq), lambda i, j: (i, 0)),
                  pl.BlockSpec((tt, seq), lambda i, j: (i, 0)),
                  pl.BlockSpec((seq, tn), lambda i, j: (0, j)),
                  pl.BlockSpec((seq, tn), lambda i, j: (0, j)),
                  pl.BlockSpec((SUBLANES, tn), lambda i, j: (0, j)),
                  blk, blk,
                  pl.BlockSpec((1, D_HYENA), lambda i, j: (0, 0)),
                  pl.BlockSpec((1, D_HYENA), lambda i, j: (0, 0))],
        out_specs=blk,
        compiler_params=_cparams(("arbitrary", "arbitrary")),
        name="hyena_dft_inv",
    )(fr, fi, yr, yi, yl, z, x0c, skip.reshape(1, D_HYENA), gnorm.reshape(1, D_HYENA))


def _hyena_long(zbf, z, x0c, lp, seq):
    fr_np, fi_np = _dft_mats(seq)
    fr, fi = jnp.asarray(fr_np), jnp.asarray(fi_np)
    hcat, kl = _hy_filters(seq, lp['filt_w1'], lp['filt_b1'], lp['filt_w2'], lp['filt_b2'],
                           lp['filt_w3'], lp['filt_freq'])
    yr, yi, yl = _hy_fwd(fr, fi, zbf, hcat, kl, seq)
    return _hy_inv(fr, fi, yr, yi, yl, z, x0c, lp['hyena_skip'], lp['gnorm_hyena'], seq)


def _store_token_tiles(ref, x, pitch=SUBLANES):
    m = x.shape[0]
    for c in range(D_MODEL // LANES):
        ref[pl.ds(c, m, stride=pitch), :] = x[:, c * LANES:(c + 1) * LANES]


def _load_token_tiles(ref, m, lead=(), pitch=SUBLANES):
    return jnp.concatenate([ref[lead + (pl.ds(c, m, stride=pitch), slice(None))]
                            for c in range(D_MODEL // LANES)], axis=-1)


def _route(logits):
    lane_i = lax.broadcasted_iota(jnp.int32, logits.shape, 1)
    lane = lane_i.astype(f32)
    big = float(ROUTE_LANES)
    is_g = lane_i < N_GROUPS
    mg = jnp.max(jnp.where(is_g, logits, -jnp.inf), axis=-1, keepdims=True)
    sg = jnp.sum(jnp.where(is_g, jnp.exp(logits - mg), 0.0), axis=-1, keepdims=True)
    g_w = 1.0 / sg
    g_idx = jnp.min(jnp.where(is_g & (logits == mg), lane, big), axis=-1, keepdims=True)
    e_id = lane_i - ROUTE_EXP_LANE0
    sel = (e_id >= 0) & (e_id < N_EXPERTS) & ((e_id >> 2).astype(f32) == g_idx)
    me = jnp.max(jnp.where(sel, logits, -jnp.inf), axis=-1, keepdims=True)
    ee = jnp.where(sel, jnp.exp(logits - me), 0.0)
    prob = ee / jnp.sum(ee, axis=-1, keepdims=True)
    p1 = jnp.max(jnp.where(sel, prob, -1.0), axis=-1, keepdims=True)
    i1 = jnp.min(jnp.where(sel & (prob == p1), lane, big), axis=-1, keepdims=True)
    sel2 = sel & (lane != i1)
    p2 = jnp.max(jnp.where(sel2, prob, -1.0), axis=-1, keepdims=True)
    i2 = jnp.min(jnp.where(sel2 & (prob == p2), lane, big), axis=-1, keepdims=True)
    tot = p1 + p2
    rec = jnp.where(lane_i == ROUTE_E1, i1 - ROUTE_EXP_LANE0, 0.0)
    rec = jnp.where(lane_i == ROUTE_E2, i2 - ROUTE_EXP_LANE0, rec)
    rec = jnp.where(lane_i == ROUTE_W1, g_w * (p1 / tot), rec)
    return jnp.where(lane_i == ROUTE_W2, g_w * (p2 / tot), rec)


def _outproj_kernel(xp_ref, xs_ref, attp_ref, atts_ref, hyp_ref, hys_ref, mod_ref, wo_ref, g2_ref,
                    wr_ref, br_ref, x1_ref, tok_ref, ee_ref, wobf_ref, wrh_ref, wrl_ref, *, prompt_tiles):
    i = pl.program_id(0)

    @pl.when(i == 0)
    def _():
        _cast_rows(wo_ref, wobf_ref, 128)
        wr = wr_ref[...]
        hi = wr.astype(bf16)
        wrh_ref[...] = hi
        wrl_ref[...] = (wr - hi.astype(f32)).astype(bf16)

    is_p = i < prompt_tiles
    hyp = jnp.concatenate([hyp_ref[:, b * D_HYENA:(b + 1) * D_HYENA]
                           for b in range(hyp_ref.shape[1] // D_HYENA)], axis=0)
    x = jnp.where(is_p, xp_ref[...], xs_ref[...])
    att = jnp.where(is_p, attp_ref[...], atts_ref[...])
    hyo = jnp.where(is_p, hyp, hys_ref[...])
    proj = _dot(att, wobf_ref[0:D_ATT, :]) + _dot(hyo, wobf_ref[D_ATT:, :])
    x1 = x + mod_ref[0, GT1:GT1 + 1, :] * proj
    x1_ref[...] = x1
    h2 = _rms(x1, g2_ref[...]) * (1.0 + mod_ref[0, SC2:SC2 + 1, :]) + mod_ref[0, SH2:SH2 + 1, :]
    h2h = h2.astype(bf16)
    h2l = (h2 - h2h.astype(f32)).astype(bf16)
    logits = _dot(h2h, wrh_ref[...]) + _dot(h2l, wrh_ref[...]) + _dot(h2h, wrl_ref[...]) + br_ref[...]
    rt = _route(logits)
    tm = h2.shape[0]
    per_row = tm // LANES
    trow = lax.broadcasted_iota(jnp.int32, (tm, LANES), 0)
    diag = (trow & (LANES - 1)) == lax.broadcasted_iota(jnp.int32, (tm, LANES), 1)
    qrow = lax.broadcasted_iota(jnp.int32, (2 * per_row, tm), 0)
    tblk = lax.shift_right_logical(lax.broadcasted_iota(jnp.int32, (2 * per_row, tm), 1), LANES.bit_length() - 1)
    ee = jnp.zeros((2 * per_row, LANES), f32)
    for j, lane0 in enumerate((ROUTE_E1, ROUTE_E2)):
        spread = jnp.where(diag, rt[:, lane0:lane0 + 1], 0.0).astype(bf16)
        pick_rows = (qrow == tblk + j * per_row).astype(bf16)
        ee = ee + _dot(pick_rows, spread)
    ee_ref[0] = ee
    tok_ref[...] = jnp.zeros_like(tok_ref)
    _store_token_tiles(tok_ref, h2, TOK_PITCH)
    tok_ref[pl.ds(TOK_RT_ROW, h2.shape[0], stride=TOK_PITCH), :] = rt


def _outproj(xp, xs, attp, atts, hyp, hys, mod, w_out, norm2_g, wr, br, seq_p, seq_s):
    tm = 512
    tp, ts = xp.shape[0], xs.shape[0]
    npt, nst = tp // tm, ts // tm
    assert tm % seq_p == 0 and seq_s % tm == 0
    spb = seq_s // tm
    bpt = tm // seq_p
    p_idx = lambda i: jnp.minimum(i, npt - 1)
    s_idx = lambda i: jnp.maximum(i - npt, 0)
    const = lambda shape: pl.BlockSpec(shape, lambda i: (0,) * len(shape))
    return pl.pallas_call(
        functools.partial(_outproj_kernel, prompt_tiles=npt),
        out_shape=(jax.ShapeDtypeStruct((tp + ts, D_MODEL), f32),
                   jax.ShapeDtypeStruct(((tp + ts) * TOK_PITCH, LANES), f32),
                   jax.ShapeDtypeStruct((npt + nst, 2 * tm // LANES, LANES), f32)),
        grid=(npt + nst,),
        in_specs=[pl.BlockSpec((tm, D_MODEL), lambda i: (p_idx(i), 0)),
                  pl.BlockSpec((tm, D_MODEL), lambda i: (s_idx(i), 0)),
                  pl.BlockSpec((tm, D_ATT), lambda i: (p_idx(i), 0)),
                  pl.BlockSpec((tm, D_ATT), lambda i: (s_idx(i), 0)),
                  pl.BlockSpec((seq_p, bpt * D_HYENA), lambda i: (0, p_idx(i))),
                  pl.BlockSpec((tm, D_HYENA), lambda i: (s_idx(i) % spb, s_idx(i) // spb)),
                  pl.BlockSpec((1, MOD_ROWS, D_MODEL),
                               lambda i: (jnp.where(i < npt, 0, 1 + s_idx(i) // spb), 0, 0)),
                  const((D_MODEL, D_MODEL)), const((1, D_MODEL)),
                  const((D_MODEL, ROUTE_LANES)), const((1, ROUTE_LANES))],
        out_specs=(pl.BlockSpec((tm, D_MODEL), lambda i: (i, 0)),
                   pl.BlockSpec((tm * TOK_PITCH, LANES), lambda i: (i, 0)),
                   pl.BlockSpec((1, 2 * tm // LANES, LANES), lambda i: (i, 0, 0))),
        scratch_shapes=[pltpu.VMEM((D_MODEL, D_MODEL), bf16),
                        pltpu.VMEM((D_MODEL, ROUTE_LANES), bf16), pltpu.VMEM((D_MODEL, ROUTE_LANES), bf16)],
        compiler_params=_cparams(("arbitrary",)),
        name="outproj_router",
    )(xp, xs, attp, atts, hyp, hys, mod, w_out, norm2_g.reshape(1, D_MODEL), wr, br)


PAIRS_PER_GROUP = 6
N_CLASSES = N_GROUPS * PAIRS_PER_GROUP
PAIR_SLOT_A = (0, 0, 0, 1, 1, 3)
PAIR_SLOT_B = (1, 2, 3, 3, 2, 2)
FLAG_NEW_A, FLAG_NEW_B, FLAG_TILE_START, FLAG_TILE_END = 1, 2, 4, 8


CLASS_ROWS = 32
TAB_TILE, TAB_EA, TAB_EB, TAB_LO, TAB_HI, TAB_FLAGS, TAB_N = range(7)


def _select_by(idx, values):
    out = jnp.full(idx.shape, float(values[-1]), f32)
    for i in range(len(values) - 2, -1, -1):
        out = jnp.where(idx == i, float(values[i]), out)
    return out


def _plan_kernel(e1_ref, e2_ref, pos_ref, tab_ref, *, tm):
    e1 = e1_ref[...]
    e2 = e2_ref[...]
    rows = e1.shape[0]
    grp = jnp.floor(e1 * (1.0 / EXPERTS_PER_GROUP))
    l1 = e1 - EXPERTS_PER_GROUP * grp
    l2 = e2 - EXPERTS_PER_GROUP * jnp.floor(e2 * (1.0 / EXPERTS_PER_GROUP))
    lo, hi = jnp.minimum(l1, l2), jnp.maximum(l1, l2)
    pair = jnp.where(lo == 0, hi - 1, jnp.where(lo == 1, jnp.where(hi == 3, 3.0, 4.0), 5.0))
    cls = grp * PAIRS_PER_GROUP + pair

    ri = lax.broadcasted_iota(jnp.int32, (LANES, LANES), 0)
    ci = lax.broadcasted_iota(jnp.int32, (LANES, LANES), 1)
    upper = (ri <= ci).astype(bf16)
    rr = lax.broadcasted_iota(jnp.int32, (rows, rows), 0)
    rc = lax.broadcasted_iota(jnp.int32, (rows, rows), 1)
    strict_lower = (rc < rr).astype(bf16)
    cid = lax.broadcasted_iota(jnp.int32, (CLASS_ROWS, 1), 0)

    pos = jnp.zeros(e1.shape, f32)
    base = jnp.zeros((1, 1), f32)
    cnt_col = jnp.zeros((CLASS_ROWS, 1), f32)
    off_col = jnp.zeros((CLASS_ROWS, 1), f32)
    for c in range(N_CLASSES):
        m = cls == c
        within = _dot(m.astype(bf16), upper)
        tot = jnp.broadcast_to(within[:, LANES - 1:LANES], within.shape)
        before = _dot(strict_lower, tot.astype(bf16))[:, 0:1]
        count = jnp.sum(within[:, LANES - 1:LANES], axis=0, keepdims=True)
        pos = jnp.where(m, base + before + within - 1.0, pos)
        cnt_col = jnp.where(cid == c, count, cnt_col)
        off_col = jnp.where(cid == c, base, off_col)
        base = base + count
    pos_ref[...] = pos.astype(jnp.int32)

    end_col = off_col + cnt_col
    inv_tm = 1.0 / tm
    first_col = jnp.floor(off_col * inv_tm)
    ntl_col = jnp.where(cnt_col > 0, jnp.floor((end_col - 1.0) * inv_tm) - first_col + 1.0, 0.0)
    cr = lax.broadcasted_iota(jnp.int32, (CLASS_ROWS, CLASS_ROWS), 0)
    cc = lax.broadcasted_iota(jnp.int32, (CLASS_ROWS, CLASS_ROWS), 1)
    lower = (cc <= cr).astype(bf16)
    iend_col = _dot(lower, jnp.broadcast_to(ntl_col, (CLASS_ROWS, LANES)).astype(bf16))[:, 0:1]
    istart_col = iend_col - ntl_col
    n_items = jnp.sum(ntl_col, axis=0, keepdims=True)
    slots = lax.broadcasted_iota(jnp.int32, (1, LANES), 1).astype(f32)
    items = jnp.minimum(slots, n_items - 1.0)
    past = jnp.where((items >= iend_col) & (cid < N_CLASSES), 1.0, 0.0)
    it_cls = jnp.minimum(jnp.sum(past, axis=0, keepdims=True), N_CLASSES - 1.0)
    sel = it_cls == cid.astype(f32)
    pick = lambda col: jnp.sum(jnp.where(sel, col, 0.0), axis=0, keepdims=True)
    grp_col = jnp.floor(cid.astype(f32) * (1.0 / PAIRS_PER_GROUP))
    pair_col = cid.astype(f32) - PAIRS_PER_GROUP * grp_col
    ea_col = EXPERTS_PER_GROUP * grp_col + _select_by(pair_col, PAIR_SLOT_A)
    eb_col = EXPERTS_PER_GROUP * grp_col + _select_by(pair_col, PAIR_SLOT_B)
    it_tile = pick(first_col) + items - pick(istart_col)
    it_lo = jnp.maximum(pick(off_col) - it_tile * tm, 0.0)
    it_hi = jnp.minimum(pick(end_col) - it_tile * tm, float(tm))
    it_ea, it_eb = pick(ea_col), pick(eb_col)
    prev = lambda v: pltpu.roll(jnp.broadcast_to(v, (SUBLANES, LANES)), 1, 1)[0:1]
    changed = lambda v: jnp.where((slots == 0) | (v != prev(v)), 1.0, 0.0)
    flags = (FLAG_NEW_A * changed(it_ea) + FLAG_NEW_B * changed(it_eb)
             + FLAG_TILE_START * jnp.where(it_lo == 0, 1.0, 0.0) + FLAG_TILE_END * jnp.where(it_hi == tm, 1.0, 0.0))
    table = {TAB_TILE: it_tile, TAB_EA: it_ea, TAB_EB: it_eb, TAB_LO: it_lo, TAB_HI: it_hi, TAB_FLAGS: flags,
             TAB_N: jnp.broadcast_to(n_items, (1, LANES))}
    trow = lax.broadcasted_iota(jnp.int32, tab_ref.shape, 0)
    tab = jnp.zeros(tab_ref.shape, f32)
    for r, v in table.items():
        tab = jnp.where(trow == r, v, tab)
    tab_ref[...] = tab.astype(jnp.int32)


def _route_plan(ee, tm, max_items):
    per = ee.shape[1] // 2
    t = ee.shape[0] * per * LANES
    assert max_items <= LANES and tm & (tm - 1) == 0
    pos, tab = pl.pallas_call(
        functools.partial(_plan_kernel, tm=tm),
        out_shape=(jax.ShapeDtypeStruct((t // LANES, LANES), jnp.int32),
                   jax.ShapeDtypeStruct((SUBLANES, LANES), jnp.int32)),
        name="moe_plan",
    )(ee[:, :per].reshape(t // LANES, LANES), ee[:, per:].reshape(t // LANES, LANES))
    row = lambda r: tab[r, :max_items]
    return dict(pos=pos.reshape(t), n_items=tab[TAB_N, :1], it_tile=row(TAB_TILE), it_ea=row(TAB_EA),
                it_eb=row(TAB_EB), it_lo=row(TAB_LO), it_hi=row(TAB_HI), it_flags=row(TAB_FLAGS))


ROW_DMA_UNROLL = 8
CHUNK_ALIGN_LOG2 = 6


def _row_dma_loop(g0, g1, make_copy, priorities):
    def body(i, c):
        for u in range(ROW_DMA_UNROLL):
            make_copy(i * ROW_DMA_UNROLL + u).start(priority=priorities[u % len(priorities)])
        return c

    lax.fori_loop(g0, g1, body, 0)


def _moe_ffn_kernel(pos_ref, it_tile, it_ea, it_eb, it_lo, it_hi, it_flags, n_items, tok_hbm,
                    wga_ref, wua_ref, wda_ref, wgb_ref, wub_ref, wdb_ref, out_hbm,
                    src_s, xbuf, ybuf, x_s, rt_s, acc_s, wa_s, wb_s, sem_in, sem_out, *, chunk, n_tiles):
    i = pl.program_id(0)
    tm = x_s.shape[0]
    groups = tm // ROW_DMA_UNROLL

    def gather_tile(k):
        base = k * tm
        slot = k % 2
        _row_dma_loop(0, groups, lambda r: pltpu.make_async_copy(
            tok_hbm.at[src_s[base + r]],
            xbuf.at[slot, pl.ds(pl.multiple_of(r * TOK_PITCH, TOK_PITCH), TOK_PITCH), :],
            sem_in.at[slot]), GATHER_PRIORITIES)

    def scatter_tile(k):
        base = k * tm
        slot = k % 2
        _row_dma_loop(0, groups, lambda r: pltpu.make_async_copy(
            ybuf.at[slot, pl.ds(pl.multiple_of(r * SUBLANES, SUBLANES), SUBLANES), :],
            out_hbm.at[src_s[base + r]],
            sem_out.at[slot]), SCATTER_PRIORITIES)

    def expert_ffn(x, w, wg, wu, wd):
        g = _dot(x, wg[...])
        u = _dot(x, wu[...])
        hid = (g * (1.0 / (1.0 + jnp.exp(-g)))) * u
        return _dot((hid * w).astype(bf16), wd[...])

    def wait_all(buf, sem, slot):
        pltpu.make_async_copy(buf.at[slot], buf.at[slot], sem.at[slot]).wait()

    @pl.when(i == 0)
    def _():
        def inv(a, c):
            for u in range(ROW_DMA_UNROLL):
                src_s[pos_ref[a * ROW_DMA_UNROLL + u]] = a * ROW_DMA_UNROLL + u
            return c

        lax.fori_loop(0, pos_ref.shape[0] // ROW_DMA_UNROLL, inv, 0)
        gather_tile(0)

    @pl.when(i < n_items[0])
    def _():
        flags = it_flags[i]
        k = it_tile[i]

        @pl.when((flags & FLAG_NEW_A) != 0)
        def _():
            for dst, src in zip(wa_s, (wga_ref, wua_ref, wda_ref)):
                dst[...] = src[0].astype(bf16)

        @pl.when((flags & FLAG_NEW_B) != 0)
        def _():
            for dst, src in zip(wb_s, (wgb_ref, wub_ref, wdb_ref)):
                dst[...] = src[0].astype(bf16)

        @pl.when((flags & FLAG_TILE_START) != 0)
        def _():
            @pl.when(k + 1 < n_tiles)
            def _():
                gather_tile(k + 1)

            wait_all(xbuf, sem_in, k % 2)
            x_s[...] = _load_token_tiles(xbuf, tm, (k % 2,), TOK_PITCH).astype(bf16)
            rt_s[...] = xbuf[k % 2, pl.ds(TOK_RT_ROW, tm, stride=TOK_PITCH), :]
            acc_s[...] = jnp.zeros_like(acc_s)

        lo = it_lo[i]
        hi = it_hi[i]
        ea = it_ea[i].astype(f32)

        base = lax.shift_left(lax.shift_right_logical(lo, CHUNK_ALIGN_LOG2), CHUNK_ALIGN_LOG2)
        n_chunks = lax.shift_right_logical(hi - base + (chunk - 1), chunk.bit_length() - 1)

        def body(j, c):
            r = pl.multiple_of(jnp.minimum(base + j * chunk, tm - chunk), 1 << CHUNK_ALIGN_LOG2)
            x = x_s[pl.ds(r, chunk), :]
            rt = rt_s[pl.ds(r, chunk), :]
            first_is_a = rt[:, ROUTE_E1:ROUTE_E1 + 1] == ea
            w1 = rt[:, ROUTE_W1:ROUTE_W1 + 1]
            w2 = rt[:, ROUTE_W2:ROUTE_W2 + 1]
            y = (expert_ffn(x, jnp.where(first_is_a, w1, w2), *wa_s)
                 + expert_ffn(x, jnp.where(first_is_a, w2, w1), *wb_s))
            row = lax.broadcasted_iota(jnp.int32, (chunk, 1), 0) + r
            mine = (row >= lo) & (row < hi)
            acc_s[pl.ds(r, chunk), :] = jnp.where(mine, y, acc_s[pl.ds(r, chunk), :])
            return c

        lax.fori_loop(0, n_chunks, body, 0)

        @pl.when((flags & FLAG_TILE_END) != 0)
        def _():
            @pl.when(k >= 2)
            def _():
                wait_all(ybuf, sem_out, k % 2)

            _store_token_tiles(ybuf.at[k % 2], acc_s[...])
            scatter_tile(k)

            @pl.when(k == n_tiles - 1)
            def _():
                if n_tiles > 1:
                    wait_all(ybuf, sem_out, (n_tiles - 2) % 2)
                wait_all(ybuf, sem_out, (n_tiles - 1) % 2)


def _moe_ffn(plan, tok, w_gate, w_up, w_down, tm, max_items):
    n_tok = tok.shape[0] // TOK_PITCH
    n_tiles = n_tok // tm
    tok3 = tok.reshape(n_tok, TOK_PITCH, LANES)
    spec_a = lambda shape: pl.BlockSpec((1,) + shape, lambda i, ps, tl, ea, eb, lo, hi, fl, n: (ea[i], 0, 0))
    spec_b = lambda shape: pl.BlockSpec((1,) + shape, lambda i, ps, tl, ea, eb, lo, hi, fl, n: (eb[i], 0, 0))
    shapes = ((D_MODEL, D_EXPERT), (D_MODEL, D_EXPERT), (D_EXPERT, D_MODEL))
    wscratch = lambda: tuple(pltpu.VMEM(s, bf16) for s in shapes)
    return pl.pallas_call(
        functools.partial(_moe_ffn_kernel, chunk=256, n_tiles=n_tiles),
        out_shape=jax.ShapeDtypeStruct((n_tok, SUBLANES, LANES), f32),
        grid_spec=pltpu.PrefetchScalarGridSpec(
            num_scalar_prefetch=8,
            grid=(max_items,),
            in_specs=[pl.BlockSpec(memory_space=pl.ANY)] + [spec_a(s) for s in shapes] + [spec_b(s) for s in shapes],
            out_specs=pl.BlockSpec(memory_space=pl.ANY),
            scratch_shapes=[pltpu.SMEM((n_tok,), jnp.int32),
                            pltpu.VMEM((2, tm * TOK_PITCH, LANES), f32),
                            pltpu.VMEM((2, tm * SUBLANES, LANES), f32),
                            pltpu.VMEM((tm, D_MODEL), bf16), pltpu.VMEM((tm, ROUTE_LANES), f32),
                            pltpu.VMEM((tm, D_MODEL), f32), wscratch(), wscratch(),
                            pltpu.SemaphoreType.DMA((2,)), pltpu.SemaphoreType.DMA((2,))]),
        compiler_params=_cparams(("arbitrary",), 60 * 1024 * 1024),
        name="moe_ffn",
    )(plan['pos'], plan['it_tile'], plan['it_ea'], plan['it_eb'], plan['it_lo'], plan['it_hi'],
      plan['it_flags'], plan['n_items'], tok3, w_gate, w_up, w_down, w_gate, w_up, w_down)


def _final_kernel(moe_ref, x1_ref, mod_ref, fg_ref, yp_ref, yl_ref, *, prompt_tiles):
    i = pl.program_id(0)
    tm = x1_ref.shape[0]
    y = _rms(x1_ref[...] + mod_ref[0, GT2:GT2 + 1, :] * _load_token_tiles(moe_ref, tm), fg_ref[...])

    @pl.when(i < prompt_tiles)
    def _():
        yp_ref[...] = y

    @pl.when(i >= prompt_tiles)
    def _():
        yl_ref[...] = y


def _final(moe, x1, mod, final_g, t_prompt, t_lat, seq_s):
    tm = 512
    npt, nst = t_prompt // tm, t_lat // tm
    spb = seq_s // tm
    moe2 = moe.reshape(moe.shape[0] * SUBLANES, LANES)
    return pl.pallas_call(
        functools.partial(_final_kernel, prompt_tiles=npt),
        out_shape=(jax.ShapeDtypeStruct((t_prompt, D_MODEL), f32),
                   jax.ShapeDtypeStruct((t_lat, D_MODEL), f32)),
        grid=(npt + nst,),
        in_specs=[pl.BlockSpec((tm * SUBLANES, LANES), lambda i: (i, 0)),
                  pl.BlockSpec((tm, D_MODEL), lambda i: (i, 0)),
                  pl.BlockSpec((1, MOD_ROWS, D_MODEL),
                               lambda i: (jnp.where(i < npt, 0, 1 + jnp.maximum(i - npt, 0) // spb), 0, 0)),
                  pl.BlockSpec((1, D_MODEL), lambda i: (0, 0))],
        out_specs=(pl.BlockSpec((tm, D_MODEL), lambda i: (jnp.minimum(i, npt - 1), 0)),
                   pl.BlockSpec((tm, D_MODEL), lambda i: (jnp.maximum(i - npt, 0), 0))),
        compiler_params=_cparams(("arbitrary",)),
        name="moe_combine_final",
    )(moe2, x1, mod, final_g.reshape(1, D_MODEL))


def _moe(tok, ee, x1, mod, w_gate, w_up, w_down, final_g, t_prompt, t_lat, seq_s):
    tm = MOE_TM
    n_rows = t_prompt + t_lat
    assert n_rows % tm == 0
    max_items = n_rows // tm + N_CLASSES
    plan = _route_plan(ee, tm, max_items)
    moe = _moe_ffn(plan, tok, w_gate, w_up, w_down, tm, max_items)
    return _final(moe, x1, mod, final_g, t_prompt, t_lat, seq_s)


def kernel(x_prompt, x_sample, cache_k, cache_v, c, c_ctx, w_ada, b_ada, norm1_g, w_in, rpb, conv_w, conv_b, filt_w1, filt_b1, filt_w2, filt_b2, filt_w3, filt_freq, hyena_skip, gnorm_att, gnorm_hyena, w_out, norm2_g, router_grp_w, router_grp_b, router_exp_w, router_exp_b, w_gate, w_up, w_down, final_g):
    depth = w_ada.shape[0]
    assert depth == 1
    batch, seq, _ = x_prompt.shape
    dec_batch, dec_seq, _ = x_sample.shape
    l = 0

    def pack_router(grp, exp):
        rows = grp.shape[0]
        return jnp.concatenate([grp, jnp.zeros((rows, ROUTE_EXP_LANE0 - N_GROUPS), f32), exp,
                                jnp.zeros((rows, ROUTE_LANES - ROUTE_EXP_LANE0 - N_EXPERTS), f32)], axis=1)

    wr = pack_router(router_grp_w[l], router_exp_w[l])
    br = pack_router(router_grp_b[l][None, :], router_exp_b[l][None, :])

    lp = {
        'norm1_g': norm1_g[l], 'w_in': w_in[l], 'conv_w': conv_w[l], 'conv_b': conv_b[l],
        'filt_w1': filt_w1[l], 'filt_b1': filt_b1[l], 'filt_w2': filt_w2[l], 'filt_b2': filt_b2[l],
        'filt_w3': filt_w3[l], 'filt_freq': filt_freq[l], 'hyena_skip': hyena_skip[l],
        'gnorm_hyena': gnorm_hyena[l], 'w_out': w_out[l], 'norm2_g': norm2_g[l],
        'wr': wr, 'br': br, 'w_gate': w_gate[l], 'w_up': w_up[l], 'w_down': w_down[l],
    }

    cond8 = jnp.concatenate([c_ctx[None, :], c, jnp.zeros((SUBLANES - 1 - dec_batch, D_MODEL), f32)], axis=0)
    mod = _ada_mod(cond8, w_ada[l], b_ada[l], 1 + dec_batch).reshape(SUBLANES, N_MOD, D_MODEL)
    mod = jnp.pad(mod, ((0, 0), (0, MOD_ROWS - N_MOD), (0, 0)))
    mod_ctx, mod_lat = mod[0:1], mod[1:1 + dec_batch]

    xp = x_prompt.reshape(batch * seq, D_MODEL)
    xs = x_sample.reshape(dec_batch * dec_seq, D_MODEL)

    qp, k_ctx, v_ctx, *front_p = _inproj_conv(xp, mod_ctx, lp['norm1_g'], lp['w_in'], lp['conv_w'], lp['conv_b'],
                                              seq, f32)
    attp = _ctx_attention(qp, k_ctx, v_ctx, gnorm_att[l], seq)
    hyop = _hyena_long(*front_p, lp, seq)

    ql, kl, vl, *front_l = _inproj_conv_halo(xs, mod_lat, lp['norm1_g'], lp['w_in'], lp['conv_w'], lp['conv_b'],
                                             dec_seq, bf16)
    kc = cache_k[:, l].reshape(dec_batch * cache_k.shape[2], D_ATT).astype(bf16)
    vc = cache_v[:, l].reshape(dec_batch * cache_v.shape[2], D_ATT).astype(bf16)
    attl = _na_attention(ql, kl, vl, kc, vc, rpb[l], gnorm_att[l], dec_batch, dec_seq)
    hyol = _hyena_long(*front_l, lp, dec_seq)

    x1, tok, ee = _outproj(xp, xs, attp, attl, hyop, hyol, mod[0:1 + dec_batch], lp['w_out'], lp['norm2_g'],
                           lp['wr'], lp['br'], seq, dec_seq)
    yp, ys = _moe(tok, ee, x1, mod[0:1 + dec_batch], lp['w_gate'], lp['w_up'], lp['w_down'], final_g,
                  batch * seq, dec_batch * dec_seq, dec_seq)

    y_prompt = yp.reshape(batch, seq, D_MODEL)
    y_sample = ys.reshape(dec_batch, dec_seq, D_MODEL)
    new_k = k_ctx.reshape(batch, 1, seq, H_ATT, HEAD_DIM)
    new_v = v_ctx.reshape(batch, 1, seq, H_ATT, HEAD_DIM)
    return (y_prompt, y_sample, new_k, new_v)
```

```python
import functools
import math

import jax
import jax.numpy as jnp
import numpy as np
from jax import lax
from jax.experimental import pallas as pl
from jax.experimental.pallas import tpu as pltpu

f32 = jnp.float32
bf16 = jnp.bfloat16
HIGHEST = lax.Precision.HIGHEST

D_MODEL = 1024
GRID_W = 64
H_ATT = 8
HEAD_DIM = 64
D_ATT = H_ATT * HEAD_DIM
D_HYENA = 512
D_IN = 3 * D_ATT + 3 * D_HYENA
NA_ROWS = 8
NA_COLS = 16
SHORT_CONV = 3
FILTER_BANDS = 16
EMB_DIM = 1 + 2 * FILTER_BANDS
FILTER_FF = 64
DECAY_TARGET = 1e-2
MIN_DECAY = math.log(DECAY_TARGET) / 1.5
MAX_DECAY = math.log(DECAY_TARGET) / 0.3
N_GROUPS = 4
EXPERTS_PER_GROUP = 4
N_EXPERTS = N_GROUPS * EXPERTS_PER_GROUP
D_EXPERT = 512
N_MOD = 6
EPS = 1e-6
NEG_INF = -1e30
ATT_SCALE = HEAD_DIM ** -0.5

LANES = 128
SUBLANES = 8
MOD_ROWS = 8
ROUTE_LANES = 128
ROUTE_EXP_LANE0 = 16
ROUTE_E1, ROUTE_E2, ROUTE_W1, ROUTE_W2 = 0, 1, 2, 3
TOK_PITCH = 16
TOK_RT_ROW = 8
MOE_TM = 512
GATHER_PRIORITIES = (0,)
SCATTER_PRIORITIES = (1,)
VMEM_LIMIT = 56 * 1024 * 1024

SH1, SC1, GT1, SH2, SC2, GT2 = range(6)


def _cparams(sem, vmem=VMEM_LIMIT):
    return pltpu.CompilerParams(dimension_semantics=sem, vmem_limit_bytes=vmem)


def _dot(a, b):
    return jnp.dot(a, b, preferred_element_type=f32)


def _dot_hi(a, b):
    return lax.dot_general(a, b, (((1,), (0,)), ((), ())), precision=HIGHEST,
                           preferred_element_type=f32)


def _dot_nt(a, b):
    return lax.dot_general(a, b, (((1,), (1,)), ((), ())), preferred_element_type=f32)


def _rms(x, g):
    ms = jnp.mean(x * x, axis=-1, keepdims=True)
    return x * lax.rsqrt(ms + EPS) * g


def _cast_rows(src_ref, dst_ref, chunk):
    n = src_ref.shape[0] // chunk

    def body(i, c):
        r = pl.multiple_of(i * chunk, chunk)
        dst_ref[pl.ds(r, chunk), :] = src_ref[pl.ds(r, chunk), :].astype(dst_ref.dtype)
        return c

    lax.fori_loop(0, n, body, 0)


def _ada_kernel(ct_ref, w_ref, b_ref, o_ref, *, n_cond):
    ct = ct_ref[...]
    st = ct * (1.0 / (1.0 + jnp.exp(-ct)))
    w = w_ref[...]
    rid = lax.broadcasted_iota(jnp.int32, o_ref.shape, 0)
    out = jnp.broadcast_to(b_ref[...], o_ref.shape)
    for m in range(n_cond):
        row = jnp.sum(w * st[:, m:m + 1], axis=0, keepdims=True)
        out = out + jnp.where(rid == m, row, 0.0)
    o_ref[...] = out


def _ada_mod(cond8, w_ada, b_ada, n_cond):
    tn = 1536
    n = N_MOD * D_MODEL
    return pl.pallas_call(
        functools.partial(_ada_kernel, n_cond=n_cond),
        out_shape=jax.ShapeDtypeStruct((SUBLANES, n), f32),
        grid=(n // tn,),
        in_specs=[pl.BlockSpec((D_MODEL, SUBLANES), lambda j: (0, 0)),
                  pl.BlockSpec((D_MODEL, tn), lambda j: (0, j)),
                  pl.BlockSpec((1, tn), lambda j: (0, j))],
        out_specs=pl.BlockSpec((SUBLANES, tn), lambda j: (0, j)),
        compiler_params=_cparams(("arbitrary",)),
        name="ada_mod",
    )(cond8.T, w_ada, b_ada.reshape(1, n))


def _short_conv_gate(u, w_ref, b_ref, prev_row=0.0, next_row=0.0):
    seq = u.shape[0]
    row = lax.broadcasted_iota(jnp.int32, u.shape, 0)
    up = jnp.where(row == 0, prev_row, pltpu.roll(u, 1, 0))
    un = jnp.where(row == seq - 1, next_row, pltpu.roll(u, seq - 1, 0))
    uc = b_ref[...] + up * w_ref[0:1, :]
    uc = uc + u * w_ref[1:2, :]
    uc = uc + un * w_ref[2:3, :]
    return uc[:, 2 * D_HYENA:] * uc[:, D_HYENA:2 * D_HYENA], uc[:, :D_HYENA]


def _inproj_conv_kernel(x_ref, mod_ref, g_ref, w_ref, cw_ref, cb_ref, q_ref, k_ref, v_ref, zbf_ref, z_ref, x0c_ref,
                        wbf_ref, *, seq):
    @pl.when(pl.program_id(0) == 0)
    def _():
        _cast_rows(w_ref, wbf_ref, 128)

    h = _rms(x_ref[...], g_ref[...])
    h = h * (1.0 + mod_ref[0, SC1:SC1 + 1, :]) + mod_ref[0, SH1:SH1 + 1, :]
    p = _dot(h.astype(bf16), wbf_ref[...])
    q_ref[...] = p[:, 0:D_ATT].astype(q_ref.dtype)
    k_ref[...] = p[:, D_ATT:2 * D_ATT].astype(k_ref.dtype)
    v_ref[...] = p[:, 2 * D_ATT:3 * D_ATT].astype(v_ref.dtype)
    for b in range(x_ref.shape[0] // seq):
        z, x0c = _short_conv_gate(p[b * seq:(b + 1) * seq, 3 * D_ATT:], cw_ref, cb_ref)
        cs = slice(b * D_HYENA, (b + 1) * D_HYENA)
        z_ref[:, cs] = z
        zbf_ref[:, cs] = z.astype(bf16)
        x0c_ref[:, cs] = x0c


def _inproj_conv(x, mod, norm_g, w_in, conv_w, conv_b, seq, kv_dtype):
    t = x.shape[0]
    tm = 512
    assert tm % seq == 0 and mod.shape[0] == 1
    bpt = tm // seq
    n = (t // seq) * D_HYENA
    tok = lambda w: pl.BlockSpec((tm, w), lambda i: (i, 0))
    seqm = pl.BlockSpec((seq, bpt * D_HYENA), lambda i: (0, i))
    const = lambda shape, **kw: pl.BlockSpec(shape, lambda i: (0,) * len(shape), **kw)
    return pl.pallas_call(
        functools.partial(_inproj_conv_kernel, seq=seq),
        out_shape=(jax.ShapeDtypeStruct((t, D_ATT), bf16),
                   jax.ShapeDtypeStruct((t, D_ATT), kv_dtype),
                   jax.ShapeDtypeStruct((t, D_ATT), kv_dtype),
                   jax.ShapeDtypeStruct((seq, n), bf16),
                   jax.ShapeDtypeStruct((seq, n), f32),
                   jax.ShapeDtypeStruct((seq, n), f32)),
        grid=(t // tm,),
        in_specs=[tok(D_MODEL), const((1, MOD_ROWS, D_MODEL)), const((1, D_MODEL)),
                  const((D_MODEL, D_IN), pipeline_mode=pl.Buffered(1)),
                  const((SHORT_CONV, 3 * D_HYENA)), const((1, 3 * D_HYENA))],
        out_specs=(tok(D_ATT), tok(D_ATT), tok(D_ATT), seqm, seqm, seqm),
        scratch_shapes=[pltpu.VMEM((D_MODEL, D_IN), bf16)],
        compiler_params=_cparams(("arbitrary",)),
        name="inproj_conv",
    )(x, mod, norm_g.reshape(1, D_MODEL), w_in, conv_w, conv_b.reshape(1, 3 * D_HYENA))


def _inproj_conv_halo_kernel(x_ref, xprev_ref, xnext_ref, mod_ref, g_ref, w_ref, cw_ref, cb_ref,
                             q_ref, k_ref, v_ref, zbf_ref, z_ref, x0c_ref, wbf_ref, *, tiles_per_seq):
    i = pl.program_id(0)

    @pl.when(i == 0)
    def _():
        _cast_rows(w_ref, wbf_ref, 128)

    def project(xv, cols):
        h = _rms(xv, g_ref[...])
        h = h * (1.0 + mod_ref[0, SC1:SC1 + 1, :]) + mod_ref[0, SH1:SH1 + 1, :]
        return _dot(h.astype(bf16), wbf_ref[:, cols])

    p = project(x_ref[...], slice(None))
    q_ref[...] = p[:, 0:D_ATT].astype(q_ref.dtype)
    k_ref[...] = p[:, D_ATT:2 * D_ATT].astype(k_ref.dtype)
    v_ref[...] = p[:, 2 * D_ATT:3 * D_ATT].astype(v_ref.dtype)
    halo = project(jnp.concatenate([xprev_ref[...], xnext_ref[...]], axis=0), slice(3 * D_ATT, D_IN))
    pos = lax.rem(i, tiles_per_seq)
    prev_row = jnp.where(pos == 0, 0.0, halo[SUBLANES - 1:SUBLANES, :])
    next_row = jnp.where(pos == tiles_per_seq - 1, 0.0, halo[SUBLANES:SUBLANES + 1, :])
    z, x0c = _short_conv_gate(p[:, 3 * D_ATT:], cw_ref, cb_ref, prev_row, next_row)
    z_ref[...] = z
    zbf_ref[...] = z.astype(bf16)
    x0c_ref[...] = x0c


def _inproj_conv_halo(x, mod, norm_g, w_in, conv_w, conv_b, seq, kv_dtype):
    t = x.shape[0]
    tm = 512
    assert seq % tm == 0 and mod.shape[0] == t // seq
    tps = seq // tm
    n = (t // seq) * D_HYENA
    last8 = t // SUBLANES - 1
    tok = lambda w: pl.BlockSpec((tm, w), lambda i: (i, 0))
    seqm = pl.BlockSpec((tm, D_HYENA), lambda i: (i % tps, i // tps))
    const = lambda shape, **kw: pl.BlockSpec(shape, lambda i: (0,) * len(shape), **kw)
    return pl.pallas_call(
        functools.partial(_inproj_conv_halo_kernel, tiles_per_seq=tps),
        out_shape=(jax.ShapeDtypeStruct((t, D_ATT), bf16),
                   jax.ShapeDtypeStruct((t, D_ATT), kv_dtype),
                   jax.ShapeDtypeStruct((t, D_ATT), kv_dtype),
                   jax.ShapeDtypeStruct((seq, n), bf16),
                   jax.ShapeDtypeStruct((seq, n), f32),
                   jax.ShapeDtypeStruct((seq, n), f32)),
        grid=(t // tm,),
        in_specs=[tok(D_MODEL),
                  pl.BlockSpec((SUBLANES, D_MODEL), lambda i: (jnp.maximum(i * (tm // SUBLANES) - 1, 0), 0)),
                  pl.BlockSpec((SUBLANES, D_MODEL), lambda i: (jnp.minimum((i + 1) * (tm // SUBLANES), last8), 0)),
                  pl.BlockSpec((1, MOD_ROWS, D_MODEL), lambda i: (i // tps, 0, 0)),
                  const((1, D_MODEL)), const((D_MODEL, D_IN), pipeline_mode=pl.Buffered(1)),
                  const((SHORT_CONV, 3 * D_HYENA)), const((1, 3 * D_HYENA))],
        out_specs=(tok(D_ATT), tok(D_ATT), tok(D_ATT), seqm, seqm, seqm),
        scratch_shapes=[pltpu.VMEM((D_MODEL, D_IN), bf16)],
        compiler_params=_cparams(("arbitrary",)),
        name="inproj_conv_halo",
    )(x, x, x, mod, norm_g.reshape(1, D_MODEL), w_in, conv_w, conv_b.reshape(1, 3 * D_HYENA))


def _split_heads(q2):
    lane = lax.broadcasted_iota(jnp.int32, q2.shape, 1)
    qa = jnp.where(lane < HEAD_DIM, q2, 0.0)
    qb = jnp.where(lane >= HEAD_DIM, q2, 0.0)
    return jnp.concatenate([qa, qb], axis=0)


def _merge_heads(o_ab):
    m = o_ab.shape[0] // 2
    lane = lax.broadcasted_iota(jnp.int32, (m, LANES), 1)
    return jnp.where(lane < HEAD_DIM, o_ab[:m], o_ab[m:])


CTX_SEQS_PER_STEP = 2


def _ctx_attn_kernel(q_ref, k_ref, v_ref, g_ref, o_ref, *, seq):
    for b in range(q_ref.shape[0] // seq):
        rows = slice(b * seq, (b + 1) * seq)
        outs = []
        for p in range(D_ATT // LANES):
            cs = slice(p * LANES, (p + 1) * LANES)
            qq = _split_heads(q_ref[rows, cs] * ATT_SCALE).astype(bf16)
            s = _dot_nt(qq, k_ref[rows, cs].astype(bf16))
            m = jnp.max(s, axis=-1, keepdims=True)
            e = jnp.exp(s - m)
            l = jnp.sum(e, axis=-1, keepdims=True)
            o_ab = _dot(e.astype(bf16), v_ref[rows, cs].astype(bf16)) / l
            outs.append(_merge_heads(o_ab))
        o_ref[rows, :] = _rms(jnp.concatenate(outs, axis=-1), g_ref[...]).astype(o_ref.dtype)


def _ctx_attention(q, k, v, gnorm, seq):
    t = q.shape[0]
    tm = CTX_SEQS_PER_STEP * seq
    assert t % tm == 0
    spec = pl.BlockSpec((tm, D_ATT), lambda b: (b, 0))
    return pl.pallas_call(
        functools.partial(_ctx_attn_kernel, seq=seq),
        out_shape=jax.ShapeDtypeStruct((t, D_ATT), bf16),
        grid=(t // tm,),
        in_specs=[spec, spec, spec, pl.BlockSpec((1, D_ATT), lambda b: (0, 0))],
        out_specs=spec,
        compiler_params=_cparams(("arbitrary",)),
        name="ctx_attn",
    )(q, k, v, gnorm.reshape(1, D_ATT))


def _na_tables():
    col = np.arange(GRID_W)
    cs = np.clip(col - NA_COLS // 2, 0, GRID_W - NA_COLS)
    col_mask = (col[None, :] >= cs[:, None]) & (col[None, :] < cs[:, None] + NA_COLS)
    mask = np.tile(col_mask.astype(np.float32), (1, NA_ROWS))
    return mask


N_DR = 2 * NA_ROWS - 1
N_DC = 2 * NA_COLS - 1
BIAS_PAIRS = N_DR - 1


def _na_bias_rows(rpb):
    out = jnp.zeros((H_ATT, BIAS_PAIRS, LANES), f32)
    out = out.at[:, :, 0:N_DC].set(rpb[:, 0:BIAS_PAIRS])
    return out.at[:, :, GRID_W:GRID_W + N_DC].set(rpb[:, 1:N_DR])


NA_ROWS_PER_STEP = 8


def _na_row_start(r, rows):
    return jnp.clip(r - NA_ROWS // 2, 0, rows - NA_ROWS)


def _na_attn_kernel(q_ref, k_ref, v_ref, kc_ref, vc_ref, rp_ref, mask_ref, g_ref, o_ref,
                    t2_ref, *, rows):
    b = pl.program_id(0)
    r = pl.program_id(1)

    @pl.when((b == 0) & (r == 0))
    def _():
        valid = mask_ref[:, 0:LANES] != 0.0
        for h in range(H_ATT):
            for i in range(BIAS_PAIRS):
                v = jnp.broadcast_to(rp_ref[h, i:i + 1, :], (GRID_W, LANES))
                v = pltpu.roll(v, LANES - (NA_COLS - 1), 1, stride=1, stride_axis=0)
                t2_ref[h * BIAS_PAIRS + i] = jnp.where(valid, v, NEG_INF)

    nwin = NA_ROWS * GRID_W
    for rr in range(NA_ROWS_PER_STEP):
        row = r * NA_ROWS_PER_STEP + rr
        qs = slice(rr * GRID_W, (rr + 1) * GRID_W)
        rs = _na_row_start(row, rows)
        start = pl.multiple_of(rs * GRID_W, GRID_W)
        i0 = rs - row + NA_ROWS - 1
        outs = []
        for p in range(D_ATT // LANES):
            cs = slice(p * LANES, (p + 1) * LANES)
            qq = _split_heads(q_ref[qs, cs] * ATT_SCALE).astype(bf16)
            kw = k_ref[pl.ds(start, nwin), cs]
            vw = v_ref[pl.ds(start, nwin), cs]
            s_lat = _dot_nt(qq, kw)
            s_ctx = _dot_nt(qq, kc_ref[:, cs])
            bias2 = jnp.concatenate(
                [jnp.concatenate([t2_ref[(2 * p + hh) * BIAS_PAIRS + i0 + 2 * jp] for hh in range(2)], axis=0)
                 for jp in range(NA_ROWS // 2)], axis=-1)
            s_lat = s_lat + bias2
            m = jnp.maximum(jnp.max(s_lat, axis=-1, keepdims=True), jnp.max(s_ctx, axis=-1, keepdims=True))
            e_lat = jnp.exp(s_lat - m)
            e_ctx = jnp.exp(s_ctx - m)
            l = jnp.sum(e_lat, axis=-1, keepdims=True) + jnp.sum(e_ctx, axis=-1, keepdims=True)
            o_ab = (_dot(e_lat.astype(bf16), vw) + _dot(e_ctx.astype(bf16), vc_ref[:, cs])) / l
            outs.append(_merge_heads(o_ab))
        o_ref[qs, :] = _rms(jnp.concatenate(outs, axis=-1), g_ref[...]).astype(o_ref.dtype)


def _na_attention(q, k, v, kc, vc, rpb, gnorm, nb, seq):
    rows = seq // GRID_W
    assert rows % NA_ROWS_PER_STEP == 0
    steps = rows // NA_ROWS_PER_STEP
    qrows = NA_ROWS_PER_STEP * GRID_W
    past = kc.shape[0] // nb
    mask = _na_tables()
    return pl.pallas_call(
        functools.partial(_na_attn_kernel, rows=rows),
        out_shape=jax.ShapeDtypeStruct((nb * seq, D_ATT), bf16),
        grid=(nb, steps),
        in_specs=[pl.BlockSpec((qrows, D_ATT), lambda b, r: (b * steps + r, 0)),
                  pl.BlockSpec((seq, D_ATT), lambda b, r: (b, 0)),
                  pl.BlockSpec((seq, D_ATT), lambda b, r: (b, 0)),
                  pl.BlockSpec((past, D_ATT), lambda b, r: (b, 0)),
                  pl.BlockSpec((past, D_ATT), lambda b, r: (b, 0)),
                  pl.BlockSpec((H_ATT, BIAS_PAIRS, LANES), lambda b, r: (0, 0, 0)),
                  pl.BlockSpec((GRID_W, NA_ROWS * GRID_W), lambda b, r: (0, 0)),
                  pl.BlockSpec((1, D_ATT), lambda b, r: (0, 0))],
        out_specs=pl.BlockSpec((qrows, D_ATT), lambda b, r: (b * steps + r, 0)),
        scratch_shapes=[pltpu.VMEM((H_ATT * BIAS_PAIRS, GRID_W, LANES), f32)],
        compiler_params=_cparams(("arbitrary", "arbitrary")),
        name="na_attn",
    )(q, k, v, kc, vc, _na_bias_rows(rpb), jnp.asarray(mask), gnorm.reshape(1, D_ATT))


def _filter_features(seq):
    t = np.linspace(0.0, 1.0, seq, dtype=np.float64)[:, None]
    w = 2.0 * math.pi * np.arange(seq, dtype=np.float64)[:, None] / seq
    fb = np.linspace(1e-4, FILTER_BANDS - 1, FILTER_BANDS, dtype=np.float64)[None, :]
    ang = fb * w
    z = np.concatenate([t, np.cos(ang), -np.sin(ang)], axis=-1).astype(np.float32)
    return np.pad(z, ((0, 0), (0, LANES - EMB_DIM)))


def _filt_kernel(zt_ref, w1t_ref, b1_ref, fr_ref, w2t_ref, b2_ref, w3_ref, dl_ref, h_ref, kl_ref, *, seq):
    i = pl.program_id(0)
    tr = zt_ref.shape[1]
    fr = fr_ref[...]
    h = jnp.sin(fr * (_dot_hi(w1t_ref[...], zt_ref[...]) + b1_ref[...]))
    h = jnp.sin(fr * (_dot_hi(w2t_ref[...], h) + b2_ref[...]))
    h = _dot_hi(h.T, w3_ref[...])
    row = lax.broadcasted_iota(jnp.int32, (tr, D_HYENA), 0) + i * tr
    t = row[:, 0:1].astype(f32) * (1.0 / (seq - 1))
    decay = jnp.exp(-t * dl_ref[...])
    hf = h[:, :D_HYENA] * decay
    hb = jnp.where(row == 0, 0.0, h[:, D_HYENA:] * decay)
    h_ref[:, :D_HYENA] = (hf + hb).astype(bf16)
    h_ref[:, D_HYENA:] = (hb - hf).astype(bf16)
    alt = (1 - 2 * (row & 1)).astype(f32)
    part = jnp.sum(alt * (hf + hb), axis=0, keepdims=True)

    @pl.when(i == 0)
    def _():
        kl_ref[...] = jnp.zeros_like(kl_ref)

    kl_ref[...] += jnp.broadcast_to(part, kl_ref.shape)


def _hy_filters(seq, w1, b1, w2, b2, w3, freq):
    tr = 256
    zt = jnp.asarray(np.ascontiguousarray(_filter_features(seq).T))
    deltas = np.abs(np.linspace(MIN_DECAY, MAX_DECAY, D_HYENA, dtype=np.float64))[None, :].astype(np.float32)
    w1t = jnp.pad(w1, ((0, LANES - EMB_DIM), (0, 0))).T
    const = lambda shape: pl.BlockSpec(shape, lambda i: (0, 0))
    col = lambda v: v.reshape(FILTER_FF, 1)
    return pl.pallas_call(
        functools.partial(_filt_kernel, seq=seq),
        out_shape=(jax.ShapeDtypeStruct((seq, 2 * D_HYENA), bf16),
                   jax.ShapeDtypeStruct((SUBLANES, D_HYENA), f32)),
        grid=(seq // tr,),
        in_specs=[pl.BlockSpec((LANES, tr), lambda i: (0, i)),
                  const((FILTER_FF, LANES)), const((FILTER_FF, 1)), const((FILTER_FF, 1)),
                  const((FILTER_FF, FILTER_FF)), const((FILTER_FF, 1)),
                  const((FILTER_FF, 2 * D_HYENA)), const((1, D_HYENA))],
        out_specs=(pl.BlockSpec((tr, 2 * D_HYENA), lambda i: (i, 0)),
                   pl.BlockSpec((SUBLANES, D_HYENA), lambda i: (0, 0))),
        compiler_params=_cparams(("arbitrary",)),
        name="hyena_filters",
    )(zt, w1t, col(b1), col(freq), w2.T, col(b2), w3, jnp.asarray(deltas))


def _dft_mats(seq):
    n = 2 * seq
    ph = (np.arange(seq, dtype=np.int64)[:, None] * np.arange(seq, dtype=np.int64)[None, :]) % n
    ang = ph.astype(np.float64) * (2.0 * math.pi / n)
    return np.cos(ang).astype(np.float32), np.sin(ang).astype(np.float32)


def _alt_col(rows, offset):
    row = lax.broadcasted_iota(jnp.int32, (rows, 1), 0) + offset
    return (1 - 2 * (row & 1)).astype(f32)


def _hy_fwd_kernel(fr_ref, fi_ref, z_ref, h_ref, kl_ref, yr_ref, yi_ref, yl_ref, kr_s, ki_s, *, n):
    i = pl.program_id(0)
    j = pl.program_id(1)
    tf = fr_ref.shape[0]
    tn = z_ref.shape[1]
    frb = fr_ref[...].astype(bf16)
    fib = fi_ref[...].astype(bf16)

    @pl.when(j == 0)
    def _():
        f = lax.broadcasted_iota(jnp.int32, (tf, 1), 0) + i * tf
        cf = jnp.where(f == 0, 1.0 / n, 2.0 / n)
        kr_s[...] = _dot(frb, h_ref[:, :D_HYENA]) * cf
        ki_s[...] = _dot(fib, h_ref[:, D_HYENA:]) * cf

    a = _dot(frb, z_ref[...])
    b = _dot(fib, z_ref[...])
    kr = kr_s[...]
    ki = ki_s[...]
    for c in range(tn // D_HYENA):
        cs = slice(c * D_HYENA, (c + 1) * D_HYENA)
        yr_ref[:, cs] = (a[:, cs] * kr + b[:, cs] * ki).astype(bf16)
        yi_ref[:, cs] = (b[:, cs] * kr - a[:, cs] * ki).astype(bf16)

    @pl.when(i == 0)
    def _():
        alt = _alt_col(z_ref.shape[0], 0)
        nz = jnp.sum(z_ref[...].astype(f32) * alt, axis=0, keepdims=True)
        kl = jnp.concatenate([kl_ref[0:1, :]] * (tn // D_HYENA), axis=-1)
        yl_ref[...] = jnp.broadcast_to(nz * kl * (1.0 / n), yl_ref.shape)


def _hy_fwd(fr, fi, zbf, hcat, kl, seq):
    n_cols = zbf.shape[1]
    tf = min(seq, 512)
    tn = min(n_cols, 2048)
    ni, nj = seq // tf, n_cols // tn
    assert ni == 1 or nj == 1
    return pl.pallas_call(
        functools.partial(_hy_fwd_kernel, n=2 * seq),
        out_shape=(jax.ShapeDtypeStruct((seq, n_cols), bf16),
                   jax.ShapeDtypeStruct((seq, n_cols), bf16),
                   jax.ShapeDtypeStruct((SUBLANES, n_cols), f32)),
        grid=(ni, nj),
        in_specs=[pl.BlockSpec((tf, seq), lambda i, j: (i, 0)),
                  pl.BlockSpec((tf, seq), lambda i, j: (i, 0)),
                  pl.BlockSpec((seq, tn), lambda i, j: (0, j)),
                  pl.BlockSpec((seq, 2 * D_HYENA), lambda i, j: (0, 0)),
                  pl.BlockSpec((SUBLANES, D_HYENA), lambda i, j: (0, 0))],
        out_specs=(pl.BlockSpec((tf, tn), lambda i, j: (i, j)),
                   pl.BlockSpec((tf, tn), lambda i, j: (i, j)),
                   pl.BlockSpec((SUBLANES, tn), lambda i, j: (0, j))),
        scratch_shapes=[pltpu.VMEM((tf, D_HYENA), f32), pltpu.VMEM((tf, D_HYENA), f32)],
        compiler_params=_cparams(("arbitrary", "arbitrary")),
        name="hyena_dft_fwd",
    )(fr, fi, zbf, hcat, kl)


def _hy_inv_kernel(fr_ref, fi_ref, yr_ref, yi_ref, yl_ref, z_ref, x0_ref, skip_ref, g_ref, o_ref):
    tt = fr_ref.shape[0]
    tn = yr_ref.shape[1]
    y = _dot(fr_ref[...].astype(bf16), yr_ref[...]) + _dot(fi_ref[...].astype(bf16), yi_ref[...])
    alt = _alt_col(tt, pl.program_id(0) * tt)
    for c in range(tn // D_HYENA):
        cs = slice(c * D_HYENA, (c + 1) * D_HYENA)
        yc = y[:, cs] + alt * yl_ref[0:1, cs] + z_ref[:, cs] * skip_ref[...]
        o_ref[:, cs] = _rms(yc * x0_ref[:, cs], g_ref[...]).astype(o_ref.dtype)


def _hy_inv(fr, fi, yr, yi, yl, z, x0c, skip, gnorm, seq):
    n_cols = z.shape[1]
    tt = min(seq, 512)
    tn = min(n_cols, 2048)
    blk = pl.BlockSpec((tt, tn), lambda i, j: (i, j))
    return pl.pallas_call(
        _hy_inv_kernel,
        out_shape=jax.ShapeDtypeStruct((seq, n_cols), bf16),
        grid=(seq // tt, n_cols // tn),
        in_specs=[pl.BlockSpec((tt, seq), lambda i, j: (i, 0)),
                  pl.BlockSpec((tt, seq), lambda i, j: (i, 0)),
                  pl.BlockSpec((seq, tn), lambda i, j: (0, j)),
                  pl.BlockSpec((seq, tn), lambda i, j: (0, j)),
                  pl.BlockSpec((SUBLANES, tn), lambda i, j: (0, j)),
                  blk, blk,
                  pl.BlockSpec((1, D_HYENA), lambda i, j: (0, 0)),
                  pl.BlockSpec((1, D_HYENA), lambda i, j: (0, 0))],
        out_specs=blk,
        compiler_params=_cparams(("arbitrary", "arbitrary")),
        name="hyena_dft_inv",
    )(fr, fi, yr, yi, yl, z, x0c, skip.reshape(1, D_HYENA), gnorm.reshape(1, D_HYENA))


def _hyena_long(zbf, z, x0c, lp, seq):
    fr_np, fi_np = _dft_mats(seq)
    fr, fi = jnp.asarray(fr_np), jnp.asarray(fi_np)
    hcat, kl = _hy_filters(seq, lp['filt_w1'], lp['filt_b1'], lp['filt_w2'], lp['filt_b2'],
                           lp['filt_w3'], lp['filt_freq'])
    yr, yi, yl = _hy_fwd(fr, fi, zbf, hcat, kl, seq)
    return _hy_inv(fr, fi, yr, yi, yl, z, x0c, lp['hyena_skip'], lp['gnorm_hyena'], seq)


def _store_token_tiles(ref, x, pitch=SUBLANES):
    m = x.shape[0]
    for c in range(D_MODEL // LANES):
        ref[pl.ds(c, m, stride=pitch), :] = x[:, c * LANES:(c + 1) * LANES]


def _load_token_tiles(ref, m, lead=(), pitch=SUBLANES):
    return jnp.concatenate([ref[lead + (pl.ds(c, m, stride=pitch), slice(None))]
                            for c in range(D_MODEL // LANES)], axis=-1)


ROUTE_ROWS = 32


def _route(logits_t):
    ch_i = lax.broadcasted_iota(jnp.int32, logits_t.shape, 0)
    ch = ch_i.astype(f32)
    big = float(ROUTE_LANES)
    red = lambda f, v: f(v, axis=0, keepdims=True)
    is_g = ch_i < N_GROUPS
    mg = red(jnp.max, jnp.where(is_g, logits_t, -jnp.inf))
    sg = red(jnp.sum, jnp.where(is_g, jnp.exp(logits_t - mg), 0.0))
    g_w = 1.0 / sg
    g_idx = red(jnp.min, jnp.where(is_g & (logits_t == mg), ch, big))
    e_id = ch_i - ROUTE_EXP_LANE0
    sel = (e_id >= 0) & (e_id < N_EXPERTS) & ((e_id >> 2).astype(f32) == g_idx)
    me = red(jnp.max, jnp.where(sel, logits_t, -jnp.inf))
    ee = jnp.where(sel, jnp.exp(logits_t - me), 0.0)
    prob = ee / red(jnp.sum, ee)
    p1 = red(jnp.max, jnp.where(sel, prob, -1.0))
    i1 = red(jnp.min, jnp.where(sel & (prob == p1), ch, big))
    sel2 = sel & (ch != i1)
    p2 = red(jnp.max, jnp.where(sel2, prob, -1.0))
    i2 = red(jnp.min, jnp.where(sel2 & (prob == p2), ch, big))
    tot = p1 + p2
    return i1 - ROUTE_EXP_LANE0, i2 - ROUTE_EXP_LANE0, g_w * (p1 / tot), g_w * (p2 / tot)


def _outproj_kernel(xp_ref, xs_ref, attp_ref, atts_ref, hyp_ref, hys_ref, mod_ref, wo_ref, g2_ref,
                    wr_ref, br_ref, x1_ref, tok_ref, ee_ref, wobf_ref, wrh_ref, wrl_ref, *, prompt_tiles):
    i = pl.program_id(0)

    @pl.when(i == 0)
    def _():
        _cast_rows(wo_ref, wobf_ref, 128)
        wr = wr_ref[...]
        hi = wr.astype(bf16)
        wrh_ref[...] = hi
        wrl_ref[...] = (wr - hi.astype(f32)).astype(bf16)

    is_p = i < prompt_tiles
    hyp = jnp.concatenate([hyp_ref[:, b * D_HYENA:(b + 1) * D_HYENA]
                           for b in range(hyp_ref.shape[1] // D_HYENA)], axis=0)
    x = jnp.where(is_p, xp_ref[...], xs_ref[...])
    att = jnp.where(is_p, attp_ref[...], atts_ref[...])
    hyo = jnp.where(is_p, hyp, hys_ref[...])
    proj = _dot(att, wobf_ref[0:D_ATT, :]) + _dot(hyo, wobf_ref[D_ATT:, :])
    x1 = x + mod_ref[0, GT1:GT1 + 1, :] * proj
    x1_ref[...] = x1
    h2 = _rms(x1, g2_ref[...]) * (1.0 + mod_ref[0, SC2:SC2 + 1, :]) + mod_ref[0, SH2:SH2 + 1, :]
    h2h = h2.astype(bf16)
    h2l = (h2 - h2h.astype(f32)).astype(bf16)
    logits = _dot(h2h, wrh_ref[...]) + _dot(h2l, wrh_ref[...]) + _dot(h2h, wrl_ref[...]) + br_ref[...]
    tm = h2.shape[0]
    per_row = tm // LANES
    e1, e2, w1, w2 = _route(logits.T[0:ROUTE_ROWS, :])
    for j, e in enumerate((e1, e2)):
        for q in range(per_row):
            ee_ref[0, j * per_row + q:j * per_row + q + 1, :] = e[:, q * LANES:(q + 1) * LANES]
    rrow = lax.broadcasted_iota(jnp.int32, (LANES, tm), 0)
    rec_t = jnp.zeros((LANES, tm), f32)
    for r, v in ((ROUTE_E1, e1), (ROUTE_E2, e2), (ROUTE_W1, w1), (ROUTE_W2, w2)):
        rec_t = jnp.where(rrow == r, v, rec_t)
    tok_ref[...] = jnp.zeros_like(tok_ref)
    _store_token_tiles(tok_ref, h2, TOK_PITCH)
    tok_ref[pl.ds(TOK_RT_ROW, tm, stride=TOK_PITCH), :] = rec_t.T


def _outproj(xp, xs, attp, atts, hyp, hys, mod, w_out, norm2_g, wr, br, seq_p, seq_s):
    tm = 512
    tp, ts = xp.shape[0], xs.shape[0]
    npt, nst = tp // tm, ts // tm
    assert tm % seq_p == 0 and seq_s % tm == 0
    spb = seq_s // tm
    bpt = tm // seq_p
    p_idx = lambda i: jnp.minimum(i, npt - 1)
    s_idx = lambda i: jnp.maximum(i - npt, 0)
    const = lambda shape: pl.BlockSpec(shape, lambda i: (0,) * len(shape))
    return pl.pallas_call(
        functools.partial(_outproj_kernel, prompt_tiles=npt),
        out_shape=(jax.ShapeDtypeStruct((tp + ts, D_MODEL), f32),
                   jax.ShapeDtypeStruct(((tp + ts) * TOK_PITCH, LANES), f32),
                   jax.ShapeDtypeStruct((npt + nst, 2 * tm // LANES, LANES), f32)),
        grid=(npt + nst,),
        in_specs=[pl.BlockSpec((tm, D_MODEL), lambda i: (p_idx(i), 0)),
                  pl.BlockSpec((tm, D_MODEL), lambda i: (s_idx(i), 0)),
                  pl.BlockSpec((tm, D_ATT), lambda i: (p_idx(i), 0)),
                  pl.BlockSpec((tm, D_ATT), lambda i: (s_idx(i), 0)),
                  pl.BlockSpec((seq_p, bpt * D_HYENA), lambda i: (0, p_idx(i))),
                  pl.BlockSpec((tm, D_HYENA), lambda i: (s_idx(i) % spb, s_idx(i) // spb)),
                  pl.BlockSpec((1, MOD_ROWS, D_MODEL),
                               lambda i: (jnp.where(i < npt, 0, 1 + s_idx(i) // spb), 0, 0)),
                  const((D_MODEL, D_MODEL)), const((1, D_MODEL)),
                  const((D_MODEL, ROUTE_LANES)), const((1, ROUTE_LANES))],
        out_specs=(pl.BlockSpec((tm, D_MODEL), lambda i: (i, 0)),
                   pl.BlockSpec((tm * TOK_PITCH, LANES), lambda i: (i, 0)),
                   pl.BlockSpec((1, 2 * tm // LANES, LANES), lambda i: (i, 0, 0))),
        scratch_shapes=[pltpu.VMEM((D_MODEL, D_MODEL), bf16),
                        pltpu.VMEM((D_MODEL, ROUTE_LANES), bf16), pltpu.VMEM((D_MODEL, ROUTE_LANES), bf16)],
        compiler_params=_cparams(("arbitrary",)),
        name="outproj_router",
    )(xp, xs, attp, atts, hyp, hys, mod, w_out, norm2_g.reshape(1, D_MODEL), wr, br)


PAIRS_PER_GROUP = 6
N_CLASSES = N_GROUPS * PAIRS_PER_GROUP
PAIR_SLOT_A = (0, 0, 0, 1, 1, 3)
PAIR_SLOT_B = (1, 2, 3, 3, 2, 2)
FLAG_NEW_A, FLAG_NEW_B, FLAG_TILE_START, FLAG_TILE_END = 1, 2, 4, 8


CLASS_ROWS = 32
TAB_TILE, TAB_EA, TAB_EB, TAB_LO, TAB_HI, TAB_FLAGS, TAB_N = range(7)


def _select_by(idx, values):
    out = jnp.full(idx.shape, float(values[-1]), f32)
    for i in range(len(values) - 2, -1, -1):
        out = jnp.where(idx == i, float(values[i]), out)
    return out


def _plan_kernel(e1_ref, e2_ref, pos_ref, tab_ref, *, tm):
    e1 = e1_ref[...]
    e2 = e2_ref[...]
    rows = e1.shape[0]
    grp = jnp.floor(e1 * (1.0 / EXPERTS_PER_GROUP))
    l1 = e1 - EXPERTS_PER_GROUP * grp
    l2 = e2 - EXPERTS_PER_GROUP * jnp.floor(e2 * (1.0 / EXPERTS_PER_GROUP))
    lo, hi = jnp.minimum(l1, l2), jnp.maximum(l1, l2)
    pair = jnp.where(lo == 0, hi - 1, jnp.where(lo == 1, jnp.where(hi == 3, 3.0, 4.0), 5.0))
    cls = grp * PAIRS_PER_GROUP + pair

    ri = lax.broadcasted_iota(jnp.int32, (LANES, LANES), 0)
    ci = lax.broadcasted_iota(jnp.int32, (LANES, LANES), 1)
    upper = (ri <= ci).astype(bf16)
    rr = lax.broadcasted_iota(jnp.int32, (rows, rows), 0)
    rc = lax.broadcasted_iota(jnp.int32, (rows, rows), 1)
    strict_lower = (rc < rr).astype(bf16)
    cid = lax.broadcasted_iota(jnp.int32, (CLASS_ROWS, 1), 0)

    pos = jnp.zeros(e1.shape, f32)
    base = jnp.zeros((1, 1), f32)
    cnt_col = jnp.zeros((CLASS_ROWS, 1), f32)
    off_col = jnp.zeros((CLASS_ROWS, 1), f32)
    for c in range(N_CLASSES):
        m = cls == c
        within = _dot(m.astype(bf16), upper)
        tot = jnp.broadcast_to(within[:, LANES - 1:LANES], within.shape)
        before = _dot(strict_lower, tot.astype(bf16))[:, 0:1]
        count = jnp.sum(within[:, LANES - 1:LANES], axis=0, keepdims=True)
        pos = jnp.where(m, base + before + within - 1.0, pos)
        cnt_col = jnp.where(cid == c, count, cnt_col)
        off_col = jnp.where(cid == c, base, off_col)
        base = base + count
    pos_ref[...] = pos.astype(jnp.int32)

    end_col = off_col + cnt_col
    inv_tm = 1.0 / tm
    first_col = jnp.floor(off_col * inv_tm)
    ntl_col = jnp.where(cnt_col > 0, jnp.floor((end_col - 1.0) * inv_tm) - first_col + 1.0, 0.0)
    cr = lax.broadcasted_iota(jnp.int32, (CLASS_ROWS, CLASS_ROWS), 0)
    cc = lax.broadcasted_iota(jnp.int32, (CLASS_ROWS, CLASS_ROWS), 1)
    lower = (cc <= cr).astype(bf16)
    iend_col = _dot(lower, jnp.broadcast_to(ntl_col, (CLASS_ROWS, LANES)).astype(bf16))[:, 0:1]
    istart_col = iend_col - ntl_col
    n_items = jnp.sum(ntl_col, axis=0, keepdims=True)
    slots = lax.broadcasted_iota(jnp.int32, (1, LANES), 1).astype(f32)
    items = jnp.minimum(slots, n_items - 1.0)
    past = jnp.where((items >= iend_col) & (cid < N_CLASSES), 1.0, 0.0)
    it_cls = jnp.minimum(jnp.sum(past, axis=0, keepdims=True), N_CLASSES - 1.0)
    sel = it_cls == cid.astype(f32)
    pick = lambda col: jnp.sum(jnp.where(sel, col, 0.0), axis=0, keepdims=True)
    grp_col = jnp.floor(cid.astype(f32) * (1.0 / PAIRS_PER_GROUP))
    pair_col = cid.astype(f32) - PAIRS_PER_GROUP * grp_col
    ea_col = EXPERTS_PER_GROUP * grp_col + _select_by(pair_col, PAIR_SLOT_A)
    eb_col = EXPERTS_PER_GROUP * grp_col + _select_by(pair_col, PAIR_SLOT_B)
    it_tile = pick(first_col) + items - pick(istart_col)
    it_lo = jnp.maximum(pick(off_col) - it_tile * tm, 0.0)
    it_hi = jnp.minimum(pick(end_col) - it_tile * tm, float(tm))
    it_ea, it_eb = pick(ea_col), pick(eb_col)
    prev = lambda v: pltpu.roll(jnp.broadcast_to(v, (SUBLANES, LANES)), 1, 1)[0:1]
    changed = lambda v: jnp.where((slots == 0) | (v != prev(v)), 1.0, 0.0)
    flags = (FLAG_NEW_A * changed(it_ea) + FLAG_NEW_B * changed(it_eb)
             + FLAG_TILE_START * jnp.where(it_lo == 0, 1.0, 0.0) + FLAG_TILE_END * jnp.where(it_hi == tm, 1.0, 0.0))
    table = {TAB_TILE: it_tile, TAB_EA: it_ea, TAB_EB: it_eb, TAB_LO: it_lo, TAB_HI: it_hi, TAB_FLAGS: flags,
             TAB_N: jnp.broadcast_to(n_items, (1, LANES))}
    trow = lax.broadcasted_iota(jnp.int32, tab_ref.shape, 0)
    tab = jnp.zeros(tab_ref.shape, f32)
    for r, v in table.items():
        tab = jnp.where(trow == r, v, tab)
    tab_ref[...] = tab.astype(jnp.int32)


def _route_plan(ee, tm, max_items):
    per = ee.shape[1] // 2
    t = ee.shape[0] * per * LANES
    assert max_items <= LANES and tm & (tm - 1) == 0
    pos, tab = pl.pallas_call(
        functools.partial(_plan_kernel, tm=tm),
        out_shape=(jax.ShapeDtypeStruct((t // LANES, LANES), jnp.int32),
                   jax.ShapeDtypeStruct((SUBLANES, LANES), jnp.int32)),
        name="moe_plan",
    )(ee[:, :per].reshape(t // LANES, LANES), ee[:, per:].reshape(t // LANES, LANES))
    row = lambda r: tab[r, :max_items]
    return dict(pos=pos.reshape(t), n_items=tab[TAB_N, :1], it_tile=row(TAB_TILE), it_ea=row(TAB_EA),
                it_eb=row(TAB_EB), it_lo=row(TAB_LO), it_hi=row(TAB_HI), it_flags=row(TAB_FLAGS))


ROW_DMA_UNROLL = 8
CHUNK_ALIGN_LOG2 = 6


def _row_dma_loop(g0, g1, make_copy, priorities):
    def body(i, c):
        for u in range(ROW_DMA_UNROLL):
            make_copy(i * ROW_DMA_UNROLL + u).start(priority=priorities[u % len(priorities)])
        return c

    lax.fori_loop(g0, g1, body, 0)


def _moe_ffn_kernel(pos_ref, it_tile, it_ea, it_eb, it_lo, it_hi, it_flags, n_items, tok_hbm,
                    wga_ref, wua_ref, wda_ref, wgb_ref, wub_ref, wdb_ref, out_hbm,
                    src_s, xbuf, ybuf, x_s, rt_s, acc_s, wa_s, wb_s, sem_in, sem_out, *, chunk, n_tiles):
    i = pl.program_id(0)
    tm = x_s.shape[0]
    groups = tm // ROW_DMA_UNROLL

    def gather_tile(k):
        base = k * tm
        slot = k % 2
        _row_dma_loop(0, groups, lambda r: pltpu.make_async_copy(
            tok_hbm.at[src_s[base + r]],
            xbuf.at[slot, pl.ds(pl.multiple_of(r * TOK_PITCH, TOK_PITCH), TOK_PITCH), :],
            sem_in.at[slot]), GATHER_PRIORITIES)

    def scatter_tile(k):
        base = k * tm
        slot = k % 2
        _row_dma_loop(0, groups, lambda r: pltpu.make_async_copy(
            ybuf.at[slot, pl.ds(pl.multiple_of(r * SUBLANES, SUBLANES), SUBLANES), :],
            out_hbm.at[src_s[base + r]],
            sem_out.at[slot]), SCATTER_PRIORITIES)

    def expert_ffn(x, w, wg, wu, wd):
        g = _dot(x, wg[...])
        u = _dot(x, wu[...])
        hid = (g * (1.0 / (1.0 + jnp.exp(-g)))) * u
        return _dot((hid * w).astype(bf16), wd[...])

    def wait_all(buf, sem, slot):
        pltpu.make_async_copy(buf.at[slot], buf.at[slot], sem.at[slot]).wait()

    @pl.when(i == 0)
    def _():
        def inv(a, c):
            for u in range(ROW_DMA_UNROLL):
                src_s[pos_ref[a * ROW_DMA_UNROLL + u]] = a * ROW_DMA_UNROLL + u
            return c

        lax.fori_loop(0, pos_ref.shape[0] // ROW_DMA_UNROLL, inv, 0)
        gather_tile(0)

    @pl.when(i < n_items[0])
    def _():
        flags = it_flags[i]
        k = it_tile[i]

        @pl.when((flags & FLAG_NEW_A) != 0)
        def _():
            for dst, src in zip(wa_s, (wga_ref, wua_ref, wda_ref)):
                dst[...] = src[0].astype(bf16)

        @pl.when((flags & FLAG_NEW_B) != 0)
        def _():
            for dst, src in zip(wb_s, (wgb_ref, wub_ref, wdb_ref)):
                dst[...] = src[0].astype(bf16)

        @pl.when((flags & FLAG_TILE_START) != 0)
        def _():
            @pl.when(k + 1 < n_tiles)
            def _():
                gather_tile(k + 1)

            wait_all(xbuf, sem_in, k % 2)
            x_s[...] = _load_token_tiles(xbuf, tm, (k % 2,), TOK_PITCH).astype(bf16)
            rt_s[...] = xbuf[k % 2, pl.ds(TOK_RT_ROW, tm, stride=TOK_PITCH), :]
            acc_s[...] = jnp.zeros_like(acc_s)

        lo = it_lo[i]
        hi = it_hi[i]
        ea = it_ea[i].astype(f32)

        base = lax.shift_left(lax.shift_right_logical(lo, CHUNK_ALIGN_LOG2), CHUNK_ALIGN_LOG2)
        n_chunks = lax.shift_right_logical(hi - base + (chunk - 1), chunk.bit_length() - 1)

        def body(j, c):
            r = pl.multiple_of(jnp.minimum(base + j * chunk, tm - chunk), 1 << CHUNK_ALIGN_LOG2)
            x = x_s[pl.ds(r, chunk), :]
            rt = rt_s[pl.ds(r, chunk), :]
            first_is_a = rt[:, ROUTE_E1:ROUTE_E1 + 1] == ea
            w1 = rt[:, ROUTE_W1:ROUTE_W1 + 1]
            w2 = rt[:, ROUTE_W2:ROUTE_W2 + 1]
            y = (expert_ffn(x, jnp.where(first_is_a, w1, w2), *wa_s)
                 + expert_ffn(x, jnp.where(first_is_a, w2, w1), *wb_s))
            row = lax.broadcasted_iota(jnp.int32, (chunk, 1), 0) + r
            mine = (row >= lo) & (row < hi)
            acc_s[pl.ds(r, chunk), :] = jnp.where(mine, y, acc_s[pl.ds(r, chunk), :])
            return c

        lax.fori_loop(0, n_chunks, body, 0)

        @pl.when((flags & FLAG_TILE_END) != 0)
        def _():
            @pl.when(k >= 2)
            def _():
                wait_all(ybuf, sem_out, k % 2)

            _store_token_tiles(ybuf.at[k % 2], acc_s[...])
            scatter_tile(k)

            @pl.when(k == n_tiles - 1)
            def _():
                if n_tiles > 1:
                    wait_all(ybuf, sem_out, (n_tiles - 2) % 2)
                wait_all(ybuf, sem_out, (n_tiles - 1) % 2)


def _moe_ffn(plan, tok, w_gate, w_up, w_down, tm, max_items):
    n_tok = tok.shape[0] // TOK_PITCH
    n_tiles = n_tok // tm
    tok3 = tok.reshape(n_tok, TOK_PITCH, LANES)
    spec_a = lambda shape: pl.BlockSpec((1,) + shape, lambda i, ps, tl, ea, eb, lo, hi, fl, n: (ea[i], 0, 0))
    spec_b = lambda shape: pl.BlockSpec((1,) + shape, lambda i, ps, tl, ea, eb, lo, hi, fl, n: (eb[i], 0, 0))
    shapes = ((D_MODEL, D_EXPERT), (D_MODEL, D_EXPERT), (D_EXPERT, D_MODEL))
    wscratch = lambda: tuple(pltpu.VMEM(s, bf16) for s in shapes)
    return pl.pallas_call(
        functools.partial(_moe_ffn_kernel, chunk=256, n_tiles=n_tiles),
        out_shape=jax.ShapeDtypeStruct((n_tok, SUBLANES, LANES), f32),
        grid_spec=pltpu.PrefetchScalarGridSpec(
            num_scalar_prefetch=8,
            grid=(max_items,),
            in_specs=[pl.BlockSpec(memory_space=pl.ANY)] + [spec_a(s) for s in shapes] + [spec_b(s) for s in shapes],
            out_specs=pl.BlockSpec(memory_space=pl.ANY),
            scratch_shapes=[pltpu.SMEM((n_tok,), jnp.int32),
                            pltpu.VMEM((2, tm * TOK_PITCH, LANES), f32),
                            pltpu.VMEM((2, tm * SUBLANES, LANES), f32),
                            pltpu.VMEM((tm, D_MODEL), bf16), pltpu.VMEM((tm, ROUTE_LANES), f32),
                            pltpu.VMEM((tm, D_MODEL), f32), wscratch(), wscratch(),
                            pltpu.SemaphoreType.DMA((2,)), pltpu.SemaphoreType.DMA((2,))]),
        compiler_params=_cparams(("arbitrary",), 60 * 1024 * 1024),
        name="moe_ffn",
    )(plan['pos'], plan['it_tile'], plan['it_ea'], plan['it_eb'], plan['it_lo'], plan['it_hi'],
      plan['it_flags'], plan['n_items'], tok3, w_gate, w_up, w_down, w_gate, w_up, w_down)


def _final_kernel(moe_ref, x1_ref, mod_ref, fg_ref, yp_ref, yl_ref, *, prompt_tiles):
    i = pl.program_id(0)
    tm = x1_ref.shape[0]
    y = _rms(x1_ref[...] + mod_ref[0, GT2:GT2 + 1, :] * _load_token_tiles(moe_ref, tm), fg_ref[...])

    @pl.when(i < prompt_tiles)
    def _():
        yp_ref[...] = y

    @pl.when(i >= prompt_tiles)
    def _():
        yl_ref[...] = y


def _final(moe, x1, mod, final_g, t_prompt, t_lat, seq_s):
    tm = 512
    npt, nst = t_prompt // tm, t_lat // tm
    spb = seq_s // tm
    moe2 = moe.reshape(moe.shape[0] * SUBLANES, LANES)
    return pl.pallas_call(
        functools.partial(_final_kernel, prompt_tiles=npt),
        out_shape=(jax.ShapeDtypeStruct((t_prompt, D_MODEL), f32),
                   jax.ShapeDtypeStruct((t_lat, D_MODEL), f32)),
        grid=(npt + nst,),
        in_specs=[pl.BlockSpec((tm * SUBLANES, LANES), lambda i: (i, 0)),
                  pl.BlockSpec((tm, D_MODEL), lambda i: (i, 0)),
                  pl.BlockSpec((1, MOD_ROWS, D_MODEL),
                               lambda i: (jnp.where(i < npt, 0, 1 + jnp.maximum(i - npt, 0) // spb), 0, 0)),
                  pl.BlockSpec((1, D_MODEL), lambda i: (0, 0))],
        out_specs=(pl.BlockSpec((tm, D_MODEL), lambda i: (jnp.minimum(i, npt - 1), 0)),
                   pl.BlockSpec((tm, D_MODEL), lambda i: (jnp.maximum(i - npt, 0), 0))),
        compiler_params=_cparams(("arbitrary",)),
        name="moe_combine_final",
    )(moe2, x1, mod, final_g.reshape(1, D_MODEL))


def _moe(tok, ee, x1, mod, w_gate, w_up, w_down, final_g, t_prompt, t_lat, seq_s):
    tm = MOE_TM
    n_rows = t_prompt + t_lat
    assert n_rows % tm == 0
    max_items = n_rows // tm + N_CLASSES
    plan = _route_plan(ee, tm, max_items)
    moe = _moe_ffn(plan, tok, w_gate, w_up, w_down, tm, max_items)
    return _final(moe, x1, mod, final_g, t_prompt, t_lat, seq_s)


def kernel(x_prompt, x_sample, cache_k, cache_v, c, c_ctx, w_ada, b_ada, norm1_g, w_in, rpb, conv_w, conv_b, filt_w1, filt_b1, filt_w2, filt_b2, filt_w3, filt_freq, hyena_skip, gnorm_att, gnorm_hyena, w_out, norm2_g, router_grp_w, router_grp_b, router_exp_w, router_exp_b, w_gate, w_up, w_down, final_g):
    depth = w_ada.shape[0]
    assert depth == 1
    batch, seq, _ = x_prompt.shape
    dec_batch, dec_seq, _ = x_sample.shape
    l = 0

    def pack_router(grp, exp):
        rows = grp.shape[0]
        return jnp.concatenate([grp, jnp.zeros((rows, ROUTE_EXP_LANE0 - N_GROUPS), f32), exp,
                                jnp.zeros((rows, ROUTE_LANES - ROUTE_EXP_LANE0 - N_EXPERTS), f32)], axis=1)

    wr = pack_router(router_grp_w[l], router_exp_w[l])
    br = pack_router(router_grp_b[l][None, :], router_exp_b[l][None, :])

    lp = {
        'norm1_g': norm1_g[l], 'w_in': w_in[l], 'conv_w': conv_w[l], 'conv_b': conv_b[l],
        'filt_w1': filt_w1[l], 'filt_b1': filt_b1[l], 'filt_w2': filt_w2[l], 'filt_b2': filt_b2[l],
        'filt_w3': filt_w3[l], 'filt_freq': filt_freq[l], 'hyena_skip': hyena_skip[l],
        'gnorm_hyena': gnorm_hyena[l], 'w_out': w_out[l], 'norm2_g': norm2_g[l],
        'wr': wr, 'br': br, 'w_gate': w_gate[l], 'w_up': w_up[l], 'w_down': w_down[l],
    }

    cond8 = jnp.concatenate([c_ctx[None, :], c, jnp.zeros((SUBLANES - 1 - dec_batch, D_MODEL), f32)], axis=0)
    mod = _ada_mod(cond8, w_ada[l], b_ada[l], 1 + dec_batch).reshape(SUBLANES, N_MOD, D_MODEL)
    mod = jnp.pad(mod, ((0, 0), (0, MOD_ROWS - N_MOD), (0, 0)))
    mod_ctx, mod_lat = mod[0:1], mod[1:1 + dec_batch]

    xp = x_prompt.reshape(batch * seq, D_MODEL)
    xs = x_sample.reshape(dec_batch * dec_seq, D_MODEL)

    qp, k_ctx, v_ctx, *front_p = _inproj_conv(xp, mod_ctx, lp['norm1_g'], lp['w_in'], lp['conv_w'], lp['conv_b'],
                                              seq, f32)
    attp = _ctx_attention(qp, k_ctx, v_ctx, gnorm_att[l], seq)
    hyop = _hyena_long(*front_p, lp, seq)

    ql, kl, vl, *front_l = _inproj_conv_halo(xs, mod_lat, lp['norm1_g'], lp['w_in'], lp['conv_w'], lp['conv_b'],
                                             dec_seq, bf16)
    kc = cache_k[:, l].reshape(dec_batch * cache_k.shape[2], D_ATT).astype(bf16)
    vc = cache_v[:, l].reshape(dec_batch * cache_v.shape[2], D_ATT).astype(bf16)
    attl = _na_attention(ql, kl, vl, kc, vc, rpb[l], gnorm_att[l], dec_batch, dec_seq)
    hyol = _hyena_long(*front_l, lp, dec_seq)

    x1, tok, ee = _outproj(xp, xs, attp, attl, hyop, hyol, mod[0:1 + dec_batch], lp['w_out'], lp['norm2_g'],
                           lp['wr'], lp['br'], seq, dec_seq)
    yp, ys = _moe(tok, ee, x1, mod[0:1 + dec_batch], lp['w_gate'], lp['w_up'], lp['w_down'], final_g,
                  batch * seq, dec_batch * dec_seq, dec_seq)

    y_prompt = yp.reshape(batch, seq, D_MODEL)
    y_sample = ys.reshape(dec_batch, dec_seq, D_MODEL)
    new_k = k_ctx.reshape(batch, 1, seq, H_ATT, HEAD_DIM)
    new_v = v_ctx.reshape(batch, 1, seq, H_ATT, HEAD_DIM)
    return (y_prompt, y_sample, new_k, new_v)
```

```python
import functools
import math

import jax
import jax.numpy as jnp
import numpy as np
from jax import lax
from jax.experimental import pallas as pl
from jax.experimental.pallas import tpu as pltpu

f32 = jnp.float32
bf16 = jnp.bfloat16
HIGHEST = lax.Precision.HIGHEST

D_MODEL = 1024
GRID_W = 64
H_ATT = 8
HEAD_DIM = 64
D_ATT = H_ATT * HEAD_DIM
D_HYENA = 512
D_IN = 3 * D_ATT + 3 * D_HYENA
NA_ROWS = 8
NA_COLS = 16
SHORT_CONV = 3
FILTER_BANDS = 16
EMB_DIM = 1 + 2 * FILTER_BANDS
FILTER_FF = 64
DECAY_TARGET = 1e-2
MIN_DECAY = math.log(DECAY_TARGET) / 1.5
MAX_DECAY = math.log(DECAY_TARGET) / 0.3
N_GROUPS = 4
EXPERTS_PER_GROUP = 4
N_EXPERTS = N_GROUPS * EXPERTS_PER_GROUP
D_EXPERT = 512
N_MOD = 6
EPS = 1e-6
NEG_INF = -1e30
ATT_SCALE = HEAD_DIM ** -0.5

LANES = 128
SUBLANES = 8
MOD_ROWS = 8
ROUTE_LANES = 128
ROUTE_EXP_LANE0 = 16
ROUTE_E1, ROUTE_E2, ROUTE_W1, ROUTE_W2 = 0, 1, 2, 3
TOK_PITCH = 16
TOK_RT_ROW = 8
MOE_TM = 512
GATHER_PRIORITIES = (0,)
SCATTER_PRIORITIES = (1,)
VMEM_LIMIT = 56 * 1024 * 1024

SH1, SC1, GT1, SH2, SC2, GT2 = range(6)


def _cparams(sem, vmem=VMEM_LIMIT):
    return pltpu.CompilerParams(dimension_semantics=sem, vmem_limit_bytes=vmem)


def _dot(a, b):
    return jnp.dot(a, b, preferred_element_type=f32)


def _dot_hi(a, b):
    return lax.dot_general(a, b, (((1,), (0,)), ((), ())), precision=HIGHEST,
                           preferred_element_type=f32)


def _dot_nt(a, b):
    return lax.dot_general(a, b, (((1,), (1,)), ((), ())), preferred_element_type=f32)


def _rms(x, g):
    ms = jnp.mean(x * x, axis=-1, keepdims=True)
    return x * lax.rsqrt(ms + EPS) * g


def _cast_rows(src_ref, dst_ref, chunk):
    n = src_ref.shape[0] // chunk

    def body(i, c):
        r = pl.multiple_of(i * chunk, chunk)
        dst_ref[pl.ds(r, chunk), :] = src_ref[pl.ds(r, chunk), :].astype(dst_ref.dtype)
        return c

    lax.fori_loop(0, n, body, 0)


def _ada_kernel(ct_ref, w_ref, b_ref, o_ref, *, n_cond):
    ct = ct_ref[...]
    st = ct * (1.0 / (1.0 + jnp.exp(-ct)))
    w = w_ref[...]
    rid = lax.broadcasted_iota(jnp.int32, o_ref.shape, 0)
    out = jnp.broadcast_to(b_ref[...], o_ref.shape)
    for m in range(n_cond):
        row = jnp.sum(w * st[:, m:m + 1], axis=0, keepdims=True)
        out = out + jnp.where(rid == m, row, 0.0)
    o_ref[...] = out


def _ada_mod(cond8, w_ada, b_ada, n_cond):
    tn = 1536
    n = N_MOD * D_MODEL
    return pl.pallas_call(
        functools.partial(_ada_kernel, n_cond=n_cond),
        out_shape=jax.ShapeDtypeStruct((SUBLANES, n), f32),
        grid=(n // tn,),
        in_specs=[pl.BlockSpec((D_MODEL, SUBLANES), lambda j: (0, 0)),
                  pl.BlockSpec((D_MODEL, tn), lambda j: (0, j)),
                  pl.BlockSpec((1, tn), lambda j: (0, j))],
        out_specs=pl.BlockSpec((SUBLANES, tn), lambda j: (0, j)),
        compiler_params=_cparams(("arbitrary",)),
        name="ada_mod",
    )(cond8.T, w_ada, b_ada.reshape(1, n))


def _short_conv_gate(u, w_ref, b_ref, prev_row=0.0, next_row=0.0):
    seq = u.shape[0]
    row = lax.broadcasted_iota(jnp.int32, u.shape, 0)
    up = jnp.where(row == 0, prev_row, pltpu.roll(u, 1, 0))
    un = jnp.where(row == seq - 1, next_row, pltpu.roll(u, seq - 1, 0))
    uc = b_ref[...] + up * w_ref[0:1, :]
    uc = uc + u * w_ref[1:2, :]
    uc = uc + un * w_ref[2:3, :]
    return uc[:, 2 * D_HYENA:] * uc[:, D_HYENA:2 * D_HYENA], uc[:, :D_HYENA]


def _inproj_conv_kernel(x_ref, mod_ref, g_ref, w_ref, cw_ref, cb_ref, q_ref, k_ref, v_ref, zbf_ref, z_ref, x0c_ref,
                        wbf_ref, *, seq):
    @pl.when(pl.program_id(0) == 0)
    def _():
        _cast_rows(w_ref, wbf_ref, 128)

    h = _rms(x_ref[...], g_ref[...])
    h = h * (1.0 + mod_ref[0, SC1:SC1 + 1, :]) + mod_ref[0, SH1:SH1 + 1, :]
    p = _dot(h.astype(bf16), wbf_ref[...])
    q_ref[...] = p[:, 0:D_ATT].astype(q_ref.dtype)
    k_ref[...] = p[:, D_ATT:2 * D_ATT].astype(k_ref.dtype)
    v_ref[...] = p[:, 2 * D_ATT:3 * D_ATT].astype(v_ref.dtype)
    for b in range(x_ref.shape[0] // seq):
        z, x0c = _short_conv_gate(p[b * seq:(b + 1) * seq, 3 * D_ATT:], cw_ref, cb_ref)
        cs = slice(b * D_HYENA, (b + 1) * D_HYENA)
        z_ref[:, cs] = z
        zbf_ref[:, cs] = z.astype(bf16)
        x0c_ref[:, cs] = x0c


def _inproj_conv(x, mod, norm_g, w_in, conv_w, conv_b, seq, kv_dtype):
    t = x.shape[0]
    tm = 512
    assert tm % seq == 0 and mod.shape[0] == 1
    bpt = tm // seq
    n = (t // seq) * D_HYENA
    tok = lambda w: pl.BlockSpec((tm, w), lambda i: (i, 0))
    seqm = pl.BlockSpec((seq, bpt * D_HYENA), lambda i: (0, i))
    const = lambda shape, **kw: pl.BlockSpec(shape, lambda i: (0,) * len(shape), **kw)
    return pl.pallas_call(
        functools.partial(_inproj_conv_kernel, seq=seq),
        out_shape=(jax.ShapeDtypeStruct((t, D_ATT), bf16),
                   jax.ShapeDtypeStruct((t, D_ATT), kv_dtype),
                   jax.ShapeDtypeStruct((t, D_ATT), kv_dtype),
                   jax.ShapeDtypeStruct((seq, n), bf16),
                   jax.ShapeDtypeStruct((seq, n), f32),
                   jax.ShapeDtypeStruct((seq, n), f32)),
        grid=(t // tm,),
        in_specs=[tok(D_MODEL), const((1, MOD_ROWS, D_MODEL)), const((1, D_MODEL)),
                  const((D_MODEL, D_IN), pipeline_mode=pl.Buffered(1)),
                  const((SHORT_CONV, 3 * D_HYENA)), const((1, 3 * D_HYENA))],
        out_specs=(tok(D_ATT), tok(D_ATT), tok(D_ATT), seqm, seqm, seqm),
        scratch_shapes=[pltpu.VMEM((D_MODEL, D_IN), bf16)],
        compiler_params=_cparams(("arbitrary",)),
        name="inproj_conv",
    )(x, mod, norm_g.reshape(1, D_MODEL), w_in, conv_w, conv_b.reshape(1, 3 * D_HYENA))


def _inproj_conv_halo_kernel(x_ref, xprev_ref, xnext_ref, mod_ref, g_ref, w_ref, cw_ref, cb_ref,
                             q_ref, k_ref, v_ref, zbf_ref, z_ref, x0c_ref, wbf_ref, *, tiles_per_seq):
    i = pl.program_id(0)

    @pl.when(i == 0)
    def _():
        _cast_rows(w_ref, wbf_ref, 128)

    tm = x_ref.shape[0]
    xv = jnp.concatenate([x_ref[...], xprev_ref[...], xnext_ref[...]], axis=0)
    h = _rms(xv, g_ref[...])
    h = h * (1.0 + mod_ref[0, SC1:SC1 + 1, :]) + mod_ref[0, SH1:SH1 + 1, :]
    p = _dot(h.astype(bf16), wbf_ref[...])
    q_ref[...] = p[0:tm, 0:D_ATT].astype(q_ref.dtype)
    k_ref[...] = p[0:tm, D_ATT:2 * D_ATT].astype(k_ref.dtype)
    v_ref[...] = p[0:tm, 2 * D_ATT:3 * D_ATT].astype(v_ref.dtype)
    pos = lax.rem(i, tiles_per_seq)
    prev_row = jnp.where(pos == 0, 0.0, p[tm + SUBLANES - 1:tm + SUBLANES, 3 * D_ATT:])
    next_row = jnp.where(pos == tiles_per_seq - 1, 0.0, p[tm + SUBLANES:tm + SUBLANES + 1, 3 * D_ATT:])
    z, x0c = _short_conv_gate(p[0:tm, 3 * D_ATT:], cw_ref, cb_ref, prev_row, next_row)
    z_ref[...] = z
    zbf_ref[...] = z.astype(bf16)
    x0c_ref[...] = x0c


def _inproj_conv_halo(x, mod, norm_g, w_in, conv_w, conv_b, seq, kv_dtype):
    t = x.shape[0]
    tm = 512
    assert seq % tm == 0 and mod.shape[0] == t // seq
    tps = seq // tm
    n = (t // seq) * D_HYENA
    last8 = t // SUBLANES - 1
    tok = lambda w: pl.BlockSpec((tm, w), lambda i: (i, 0))
    seqm = pl.BlockSpec((tm, D_HYENA), lambda i: (i % tps, i // tps))
    const = lambda shape, **kw: pl.BlockSpec(shape, lambda i: (0,) * len(shape), **kw)
    return pl.pallas_call(
        functools.partial(_inproj_conv_halo_kernel, tiles_per_seq=tps),
        out_shape=(jax.ShapeDtypeStruct((t, D_ATT), bf16),
                   jax.ShapeDtypeStruct((t, D_ATT), kv_dtype),
                   jax.ShapeDtypeStruct((t, D_ATT), kv_dtype),
                   jax.ShapeDtypeStruct((seq, n), bf16),
                   jax.ShapeDtypeStruct((seq, n), f32),
                   jax.ShapeDtypeStruct((seq, n), f32)),
        grid=(t // tm,),
        in_specs=[tok(D_MODEL),
                  pl.BlockSpec((SUBLANES, D_MODEL), lambda i: (jnp.maximum(i * (tm // SUBLANES) - 1, 0), 0)),
                  pl.BlockSpec((SUBLANES, D_MODEL), lambda i: (jnp.minimum((i + 1) * (tm // SUBLANES), last8), 0)),
                  pl.BlockSpec((1, MOD_ROWS, D_MODEL), lambda i: (i // tps, 0, 0)),
                  const((1, D_MODEL)), const((D_MODEL, D_IN), pipeline_mode=pl.Buffered(1)),
                  const((SHORT_CONV, 3 * D_HYENA)), const((1, 3 * D_HYENA))],
        out_specs=(tok(D_ATT), tok(D_ATT), tok(D_ATT), seqm, seqm, seqm),
        scratch_shapes=[pltpu.VMEM((D_MODEL, D_IN), bf16)],
        compiler_params=_cparams(("arbitrary",)),
        name="inproj_conv_halo",
    )(x, x, x, mod, norm_g.reshape(1, D_MODEL), w_in, conv_w, conv_b.reshape(1, 3 * D_HYENA))


def _split_heads(q2):
    lane = lax.broadcasted_iota(jnp.int32, q2.shape, 1)
    qa = jnp.where(lane < HEAD_DIM, q2, 0.0)
    qb = jnp.where(lane >= HEAD_DIM, q2, 0.0)
    return jnp.concatenate([qa, qb], axis=0)


def _merge_heads(o_ab):
    m = o_ab.shape[0] // 2
    lane = lax.broadcasted_iota(jnp.int32, (m, LANES), 1)
    return jnp.where(lane < HEAD_DIM, o_ab[:m], o_ab[m:])


CTX_SEQS_PER_STEP = 4


def _ctx_attn_kernel(q_ref, k_ref, v_ref, g_ref, o_ref, *, seq):
    for b in range(q_ref.shape[0] // seq):
        rows = slice(b * seq, (b + 1) * seq)
        outs = []
        for p in range(D_ATT // LANES):
            cs = slice(p * LANES, (p + 1) * LANES)
            qq = _split_heads(q_ref[rows, cs] * ATT_SCALE).astype(bf16)
            s = _dot_nt(qq, k_ref[rows, cs].astype(bf16))
            m = jnp.max(s, axis=-1, keepdims=True)
            e = jnp.exp(s - m)
            l = jnp.sum(e, axis=-1, keepdims=True)
            o_ab = _dot(e.astype(bf16), v_ref[rows, cs].astype(bf16)) / l
            outs.append(_merge_heads(o_ab))
        o_ref[rows, :] = _rms(jnp.concatenate(outs, axis=-1), g_ref[...]).astype(o_ref.dtype)


def _ctx_attention(q, k, v, gnorm, seq):
    t = q.shape[0]
    tm = CTX_SEQS_PER_STEP * seq
    assert t % tm == 0
    spec = pl.BlockSpec((tm, D_ATT), lambda b: (b, 0))
    return pl.pallas_call(
        functools.partial(_ctx_attn_kernel, seq=seq),
        out_shape=jax.ShapeDtypeStruct((t, D_ATT), bf16),
        grid=(t // tm,),
        in_specs=[spec, spec, spec, pl.BlockSpec((1, D_ATT), lambda b: (0, 0))],
        out_specs=spec,
        compiler_params=_cparams(("arbitrary",)),
        name="ctx_attn",
    )(q, k, v, gnorm.reshape(1, D_ATT))


def _na_tables():
    col = np.arange(GRID_W)
    cs = np.clip(col - NA_COLS // 2, 0, GRID_W - NA_COLS)
    col_mask = (col[None, :] >= cs[:, None]) & (col[None, :] < cs[:, None] + NA_COLS)
    mask = np.tile(col_mask.astype(np.float32), (1, NA_ROWS))
    return mask


N_DR = 2 * NA_ROWS - 1
N_DC = 2 * NA_COLS - 1
BIAS_PAIRS = N_DR - 1


def _na_bias_rows(rpb):
    out = jnp.zeros((H_ATT, BIAS_PAIRS, LANES), f32)
    out = out.at[:, :, 0:N_DC].set(rpb[:, 0:BIAS_PAIRS])
    return out.at[:, :, GRID_W:GRID_W + N_DC].set(rpb[:, 1:N_DR])


NA_ROWS_PER_STEP = 8


def _na_row_start(r, rows):
    return jnp.clip(r - NA_ROWS // 2, 0, rows - NA_ROWS)


def _na_attn_kernel(q_ref, k_ref, v_ref, kc_ref, vc_ref, rp_ref, mask_ref, g_ref, o_ref,
                    t2_ref, *, rows):
    b = pl.program_id(0)
    r = pl.program_id(1)

    @pl.when((b == 0) & (r == 0))
    def _():
        valid = mask_ref[:, 0:LANES] != 0.0
        for h in range(H_ATT):
            for i in range(BIAS_PAIRS):
                v = jnp.broadcast_to(rp_ref[h, i:i + 1, :], (GRID_W, LANES))
                v = pltpu.roll(v, LANES - (NA_COLS - 1), 1, stride=1, stride_axis=0)
                t2_ref[h * BIAS_PAIRS + i] = jnp.where(valid, v, NEG_INF)

    nwin = NA_ROWS * GRID_W
    for rr in range(NA_ROWS_PER_STEP):
        row = r * NA_ROWS_PER_STEP + rr
        qs = slice(rr * GRID_W, (rr + 1) * GRID_W)
        rs = _na_row_start(row, rows)
        start = pl.multiple_of(rs * GRID_W, GRID_W)
        i0 = rs - row + NA_ROWS - 1
        outs = []
        for p in range(D_ATT // LANES):
            cs = slice(p * LANES, (p + 1) * LANES)
            qq = _split_heads(q_ref[qs, cs] * ATT_SCALE).astype(bf16)
            kw = k_ref[pl.ds(start, nwin), cs]
            vw = v_ref[pl.ds(start, nwin), cs]
            s_lat = _dot_nt(qq, kw)
            s_ctx = _dot_nt(qq, kc_ref[:, cs])
            bias2 = jnp.concatenate(
                [jnp.concatenate([t2_ref[(2 * p + hh) * BIAS_PAIRS + i0 + 2 * jp] for hh in range(2)], axis=0)
                 for jp in range(NA_ROWS // 2)], axis=-1)
            s_lat = s_lat + bias2
            m = jnp.maximum(jnp.max(s_lat, axis=-1, keepdims=True), jnp.max(s_ctx, axis=-1, keepdims=True))
            e_lat = jnp.exp(s_lat - m)
            e_ctx = jnp.exp(s_ctx - m)
            l = jnp.sum(e_lat, axis=-1, keepdims=True) + jnp.sum(e_ctx, axis=-1, keepdims=True)
            o_ab = (_dot(e_lat.astype(bf16), vw) + _dot(e_ctx.astype(bf16), vc_ref[:, cs])) / l
            outs.append(_merge_heads(o_ab))
        o_ref[qs, :] = _rms(jnp.concatenate(outs, axis=-1), g_ref[...]).astype(o_ref.dtype)


def _na_attention(q, k, v, kc, vc, rpb, gnorm, nb, seq):
    rows = seq // GRID_W
    assert rows % NA_ROWS_PER_STEP == 0
    steps = rows // NA_ROWS_PER_STEP
    qrows = NA_ROWS_PER_STEP * GRID_W
    past = kc.shape[0] // nb
    mask = _na_tables()
    return pl.pallas_call(
        functools.partial(_na_attn_kernel, rows=rows),
        out_shape=jax.ShapeDtypeStruct((nb * seq, D_ATT), bf16),
        grid=(nb, steps),
        in_specs=[pl.BlockSpec((qrows, D_ATT), lambda b, r: (b * steps + r, 0)),
                  pl.BlockSpec((seq, D_ATT), lambda b, r: (b, 0)),
                  pl.BlockSpec((seq, D_ATT), lambda b, r: (b, 0)),
                  pl.BlockSpec((past, D_ATT), lambda b, r: (b, 0)),
                  pl.BlockSpec((past, D_ATT), lambda b, r: (b, 0)),
                  pl.BlockSpec((H_ATT, BIAS_PAIRS, LANES), lambda b, r: (0, 0, 0)),
                  pl.BlockSpec((GRID_W, NA_ROWS * GRID_W), lambda b, r: (0, 0)),
                  pl.BlockSpec((1, D_ATT), lambda b, r: (0, 0))],
        out_specs=pl.BlockSpec((qrows, D_ATT), lambda b, r: (b * steps + r, 0)),
        scratch_shapes=[pltpu.VMEM((H_ATT * BIAS_PAIRS, GRID_W, LANES), f32)],
        compiler_params=_cparams(("arbitrary", "arbitrary")),
        name="na_attn",
    )(q, k, v, kc, vc, _na_bias_rows(rpb), jnp.asarray(mask), gnorm.reshape(1, D_ATT))


def _filter_features(seq):
    t = np.linspace(0.0, 1.0, seq, dtype=np.float64)[:, None]
    w = 2.0 * math.pi * np.arange(seq, dtype=np.float64)[:, None] / seq
    fb = np.linspace(1e-4, FILTER_BANDS - 1, FILTER_BANDS, dtype=np.float64)[None, :]
    ang = fb * w
    z = np.concatenate([t, np.cos(ang), -np.sin(ang)], axis=-1).astype(np.float32)
    return np.pad(z, ((0, 0), (0, LANES - EMB_DIM)))


def _filt_kernel(zt_ref, w1t_ref, b1_ref, fr_ref, w2t_ref, b2_ref, w3_ref, dl_ref, h_ref, kl_ref, *, seq):
    i = pl.program_id(0)
    tr = zt_ref.shape[1]
    fr = fr_ref[...]
    h = jnp.sin(fr * (_dot_hi(w1t_ref[...], zt_ref[...]) + b1_ref[...]))
    h = jnp.sin(fr * (_dot_hi(w2t_ref[...], h) + b2_ref[...]))
    h = _dot_hi(h.T, w3_ref[...])
    row = lax.broadcasted_iota(jnp.int32, (tr, D_HYENA), 0) + i * tr
    t = row[:, 0:1].astype(f32) * (1.0 / (seq - 1))
    decay = jnp.exp(-t * dl_ref[...])
    hf = h[:, :D_HYENA] * decay
    hb = jnp.where(row == 0, 0.0, h[:, D_HYENA:] * decay)
    h_ref[:, :D_HYENA] = (hf + hb).astype(bf16)
    h_ref[:, D_HYENA:] = (hb - hf).astype(bf16)
    alt = (1 - 2 * (row & 1)).astype(f32)
    part = jnp.sum(alt * (hf + hb), axis=0, keepdims=True)

    @pl.when(i == 0)
    def _():
        kl_ref[...] = jnp.zeros_like(kl_ref)

    kl_ref[...] += jnp.broadcast_to(part, kl_ref.shape)


def _hy_filters(seq, w1, b1, w2, b2, w3, freq):
    tr = 256
    zt = jnp.asarray(np.ascontiguousarray(_filter_features(seq).T))
    deltas = np.abs(np.linspace(MIN_DECAY, MAX_DECAY, D_HYENA, dtype=np.float64))[None, :].astype(np.float32)
    w1t = jnp.pad(w1, ((0, LANES - EMB_DIM), (0, 0))).T
    const = lambda shape: pl.BlockSpec(shape, lambda i: (0, 0))
    col = lambda v: v.reshape(FILTER_FF, 1)
    return pl.pallas_call(
        functools.partial(_filt_kernel, seq=seq),
        out_shape=(jax.ShapeDtypeStruct((seq, 2 * D_HYENA), bf16),
                   jax.ShapeDtypeStruct((SUBLANES, D_HYENA), f32)),
        grid=(seq // tr,),
        in_specs=[pl.BlockSpec((LANES, tr), lambda i: (0, i)),
                  const((FILTER_FF, LANES)), const((FILTER_FF, 1)), const((FILTER_FF, 1)),
                  const((FILTER_FF, FILTER_FF)), const((FILTER_FF, 1)),
                  const((FILTER_FF, 2 * D_HYENA)), const((1, D_HYENA))],
        out_specs=(pl.BlockSpec((tr, 2 * D_HYENA), lambda i: (i, 0)),
                   pl.BlockSpec((SUBLANES, D_HYENA), lambda i: (0, 0))),
        compiler_params=_cparams(("arbitrary",)),
        name="hyena_filters",
    )(zt, w1t, col(b1), col(freq), w2.T, col(b2), w3, jnp.asarray(deltas))


def _dft_mats(seq):
    n = 2 * seq
    ph = (np.arange(seq, dtype=np.int64)[:, None] * np.arange(seq, dtype=np.int64)[None, :]) % n
    ang = ph.astype(np.float64) * (2.0 * math.pi / n)
    return np.cos(ang).astype(np.float32), np.sin(ang).astype(np.float32)


def _alt_col(rows, offset):
    row = lax.broadcasted_iota(jnp.int32, (rows, 1), 0) + offset
    return (1 - 2 * (row & 1)).astype(f32)


def _hy_fwd_kernel(fr_ref, fi_ref, z_ref, h_ref, kl_ref, yr_ref, yi_ref, yl_ref, kr_s, ki_s, *, n):
    i = pl.program_id(0)
    j = pl.program_id(1)
    tf = fr_ref.shape[0]
    tn = z_ref.shape[1]
    frb = fr_ref[...].astype(bf16)
    fib = fi_ref[...].astype(bf16)

    @pl.when(j == 0)
    def _():
        f = lax.broadcasted_iota(jnp.int32, (tf, 1), 0) + i * tf
        cf = jnp.where(f == 0, 1.0 / n, 2.0 / n)
        kr_s[...] = _dot(frb, h_ref[:, :D_HYENA]) * cf
        ki_s[...] = _dot(fib, h_ref[:, D_HYENA:]) * cf

    a = _dot(frb, z_ref[...])
    b = _dot(fib, z_ref[...])
    kr = kr_s[...]
    ki = ki_s[...]
    for c in range(tn // D_HYENA):
        cs = slice(c * D_HYENA, (c + 1) * D_HYENA)
        yr_ref[:, cs] = (a[:, cs] * kr + b[:, cs] * ki).astype(bf16)
        yi_ref[:, cs] = (b[:, cs] * kr - a[:, cs] * ki).astype(bf16)

    @pl.when(i == 0)
    def _():
        alt = _alt_col(z_ref.shape[0], 0)
        nz = jnp.sum(z_ref[...].astype(f32) * alt, axis=0, keepdims=True)
        kl = jnp.concatenate([kl_ref[0:1, :]] * (tn // D_HYENA), axis=-1)
        yl_ref[...] = jnp.broadcast_to(nz * kl * (1.0 / n), yl_ref.shape)


def _hy_fwd(fr, fi, zbf, hcat, kl, seq):
    n_cols = zbf.shape[1]
    tf = min(seq, 512)
    tn = min(n_cols, 2048)
    ni, nj = seq // tf, n_cols // tn
    assert ni == 1 or nj == 1
    return pl.pallas_call(
        functools.partial(_hy_fwd_kernel, n=2 * seq),
        out_shape=(jax.ShapeDtypeStruct((seq, n_cols), bf16),
                   jax.ShapeDtypeStruct((seq, n_cols), bf16),
                   jax.ShapeDtypeStruct((SUBLANES, n_cols), f32)),
        grid=(ni, nj),
        in_specs=[pl.BlockSpec((tf, seq), lambda i, j: (i, 0)),
                  pl.BlockSpec((tf, seq), lambda i, j: (i, 0)),
                  pl.BlockSpec((seq, tn), lambda i, j: (0, j)),
                  pl.BlockSpec((seq, 2 * D_HYENA), lambda i, j: (0, 0)),
                  pl.BlockSpec((SUBLANES, D_HYENA), lambda i, j: (0, 0))],
        out_specs=(pl.BlockSpec((tf, tn), lambda i, j: (i, j)),
                   pl.BlockSpec((tf, tn), lambda i, j: (i, j)),
                   pl.BlockSpec((SUBLANES, tn), lambda i, j: (0, j))),
        scratch_shapes=[pltpu.VMEM((tf, D_HYENA), f32), pltpu.VMEM((tf, D_HYENA), f32)],
        compiler_params=_cparams(("arbitrary", "arbitrary")),
        name="hyena_dft_fwd",
    )(fr, fi, zbf, hcat, kl)


def _hy_inv_kernel(fr_ref, fi_ref, yr_ref, yi_ref, yl_ref, z_ref, x0_ref, skip_ref, g_ref, o_ref):
    tt = fr_ref.shape[0]
    tn = yr_ref.shape[1]
    y = _dot(fr_ref[...].astype(bf16), yr_ref[...]) + _dot(fi_ref[...].astype(bf16), yi_ref[...])
    alt = _alt_col(tt, pl.program_id(0) * tt)
    for c in range(tn // D_HYENA):
        cs = slice(c * D_HYENA, (c + 1) * D_HYENA)
        yc = y[:, cs] + alt * yl_ref[0:1, cs] + z_ref[:, cs] * skip_ref[...]
        o_ref[:, cs] = _rms(yc * x0_ref[:, cs], g_ref[...]).astype(o_ref.dtype)


def _hy_inv(fr, fi, yr, yi, yl, z, x0c, skip, gnorm, seq):
    n_cols = z.shape[1]
    tt = min(seq, 512)
    tn = min(n_cols, 2048)
    blk = pl.BlockSpec((tt, tn), lambda i, j: (i, j))
    return pl.pallas_call(
        _hy_inv_kernel,
        out_shape=jax.ShapeDtypeStruct((seq, n_cols), bf16),
        grid=(seq // tt, n_cols // tn),
        in_specs=[pl.BlockSpec((tt, seq), lambda i, j: (i, 0)),
                  pl.BlockSpec((tt, seq), lambda i, j: (i, 0)),
                  pl.BlockSpec((seq, tn), lambda i, j: (0, j)),
                  pl.BlockSpec((seq, tn), lambda i, j: (0, j)),
                  pl.BlockSpec((SUBLANES, tn), lambda i, j: (0, j)),
                  blk, blk,
                  pl.BlockSpec((1, D_HYENA), lambda i, j: (0, 0)),
                  pl.BlockSpec((1, D_HYENA), lambda i, j: (0, 0))],
        out_specs=blk,
        compiler_params=_cparams(("arbitrary", "arbitrary")),
        name="hyena_dft_inv",
    )(fr, fi, yr, yi, yl, z, x0c, skip.reshape(1, D_HYENA), gnorm.reshape(1, D_HYENA))


def _hyena_long(zbf, z, x0c, lp, seq):
    fr_np, fi_np = _dft_mats(seq)
    fr, fi = jnp.asarray(fr_np), jnp.asarray(fi_np)
    hcat, kl = _hy_filters(seq, lp['filt_w1'], lp['filt_b1'], lp['filt_w2'], lp['filt_b2'],
                           lp['filt_w3'], lp['filt_freq'])
    yr, yi, yl = _hy_fwd(fr, fi, zbf, hcat, kl, seq)
    return _hy_inv(fr, fi, yr, yi, yl, z, x0c, lp['hyena_skip'], lp['gnorm_hyena'], seq)


def _store_token_tiles(ref, x, pitch=SUBLANES):
    m = x.shape[0]
    for c in range(D_MODEL // LANES):
        ref[pl.ds(c, m, stride=pitch), :] = x[:, c * LANES:(c + 1) * LANES]


def _load_token_tiles(ref, m, lead=(), pitch=SUBLANES):
    return jnp.concatenate([ref[lead + (pl.ds(c, m, stride=pitch), slice(None))]
                            for c in range(D_MODEL // LANES)], axis=-1)


ROUTE_ROWS = 32


def _route(logits_t):
    ch_i = lax.broadcasted_iota(jnp.int32, logits_t.shape, 0)
    ch = ch_i.astype(f32)
    big = float(ROUTE_LANES)
    red = lambda f, v: f(v, axis=0, keepdims=True)
    is_g = ch_i < N_GROUPS
    mg = red(jnp.max, jnp.where(is_g, logits_t, -jnp.inf))
    sg = red(jnp.sum, jnp.where(is_g, jnp.exp(logits_t - mg), 0.0))
    g_w = 1.0 / sg
    g_idx = red(jnp.min, jnp.where(is_g & (logits_t == mg), ch, big))
    e_id = ch_i - ROUTE_EXP_LANE0
    sel = (e_id >= 0) & (e_id < N_EXPERTS) & ((e_id >> 2).astype(f32) == g_idx)
    me = red(jnp.max, jnp.where(sel, logits_t, -jnp.inf))
    ee = jnp.where(sel, jnp.exp(logits_t - me), 0.0)
    prob = ee / red(jnp.sum, ee)
    p1 = red(jnp.max, jnp.where(sel, prob, -1.0))
    i1 = red(jnp.min, jnp.where(sel & (prob == p1), ch, big))
    sel2 = sel & (ch != i1)
    p2 = red(jnp.max, jnp.where(sel2, prob, -1.0))
    i2 = red(jnp.min, jnp.where(sel2 & (prob == p2), ch, big))
    tot = p1 + p2
    return i1 - ROUTE_EXP_LANE0, i2 - ROUTE_EXP_LANE0, g_w * (p1 / tot), g_w * (p2 / tot)


def _outproj_kernel(xp_ref, xs_ref, attp_ref, atts_ref, hyp_ref, hys_ref, mod_ref, wo_ref, g2_ref,
                    wr_ref, br_ref, x1_ref, tok_ref, ee_ref, wobf_ref, wrh_ref, wrl_ref, *, prompt_tiles):
    i = pl.program_id(0)

    @pl.when(i == 0)
    def _():
        _cast_rows(wo_ref, wobf_ref, 128)
        wr = wr_ref[...]
        hi = wr.astype(bf16)
        wrh_ref[...] = hi
        wrl_ref[...] = (wr - hi.astype(f32)).astype(bf16)

    is_p = i < prompt_tiles
    hyp = jnp.concatenate([hyp_ref[:, b * D_HYENA:(b + 1) * D_HYENA]
                           for b in range(hyp_ref.shape[1] // D_HYENA)], axis=0)
    x = jnp.where(is_p, xp_ref[...], xs_ref[...])
    att = jnp.where(is_p, attp_ref[...], atts_ref[...])
    hyo = jnp.where(is_p, hyp, hys_ref[...])
    proj = _dot(att, wobf_ref[0:D_ATT, :]) + _dot(hyo, wobf_ref[D_ATT:, :])
    x1 = x + mod_ref[0, GT1:GT1 + 1, :] * proj
    x1_ref[...] = x1
    h2 = _rms(x1, g2_ref[...]) * (1.0 + mod_ref[0, SC2:SC2 + 1, :]) + mod_ref[0, SH2:SH2 + 1, :]
    h2h = h2.astype(bf16)
    h2l = (h2 - h2h.astype(f32)).astype(bf16)
    logits = _dot(h2h, wrh_ref[...]) + _dot(h2l, wrh_ref[...]) + _dot(h2h, wrl_ref[...]) + br_ref[...]
    tm = h2.shape[0]
    per_row = tm // LANES
    e1, e2, w1, w2 = _route(logits.T[0:ROUTE_ROWS, :])
    for j, e in enumerate((e1, e2)):
        for q in range(per_row):
            ee_ref[0, j * per_row + q:j * per_row + q + 1, :] = e[:, q * LANES:(q + 1) * LANES]
    rrow = lax.broadcasted_iota(jnp.int32, (LANES, tm), 0)
    rec_t = jnp.zeros((LANES, tm), f32)
    for r, v in ((ROUTE_E1, e1), (ROUTE_E2, e2), (ROUTE_W1, w1), (ROUTE_W2, w2)):
        rec_t = jnp.where(rrow == r, v, rec_t)
    tok_ref[...] = jnp.zeros_like(tok_ref)
    _store_token_tiles(tok_ref, h2, TOK_PITCH)
    tok_ref[pl.ds(TOK_RT_ROW, tm, stride=TOK_PITCH), :] = rec_t.T


def _outproj(xp, xs, attp, atts, hyp, hys, mod, w_out, norm2_g, wr, br, seq_p, seq_s):
    tm = 512
    tp, ts = xp.shape[0], xs.shape[0]
    npt, nst = tp // tm, ts // tm
    assert tm % seq_p == 0 and seq_s % tm == 0
    spb = seq_s // tm
    bpt = tm // seq_p
    p_idx = lambda i: jnp.minimum(i, npt - 1)
    s_idx = lambda i: jnp.maximum(i - npt, 0)
    const = lambda shape: pl.BlockSpec(shape, lambda i: (0,) * len(shape))
    return pl.pallas_call(
        functools.partial(_outproj_kernel, prompt_tiles=npt),
        out_shape=(jax.ShapeDtypeStruct((tp + ts, D_MODEL), f32),
                   jax.ShapeDtypeStruct(((tp + ts) * TOK_PITCH, LANES), f32),
                   jax.ShapeDtypeStruct((npt + nst, 2 * tm // LANES, LANES), f32)),
        grid=(npt + nst,),
        in_specs=[pl.BlockSpec((tm, D_MODEL), lambda i: (p_idx(i), 0)),
                  pl.BlockSpec((tm, D_MODEL), lambda i: (s_idx(i), 0)),
                  pl.BlockSpec((tm, D_ATT), lambda i: (p_idx(i), 0)),
                  pl.BlockSpec((tm, D_ATT), lambda i: (s_idx(i), 0)),
                  pl.BlockSpec((seq_p, bpt * D_HYENA), lambda i: (0, p_idx(i))),
                  pl.BlockSpec((tm, D_HYENA), lambda i: (s_idx(i) % spb, s_idx(i) // spb)),
                  pl.BlockSpec((1, MOD_ROWS, D_MODEL),
                               lambda i: (jnp.where(i < npt, 0, 1 + s_idx(i) // spb), 0, 0)),
                  const((D_MODEL, D_MODEL)), const((1, D_MODEL)),
                  const((D_MODEL, ROUTE_LANES)), const((1, ROUTE_LANES))],
        out_specs=(pl.BlockSpec((tm, D_MODEL), lambda i: (i, 0)),
                   pl.BlockSpec((tm * TOK_PITCH, LANES), lambda i: (i, 0)),
                   pl.BlockSpec((1, 2 * tm // LANES, LANES), lambda i: (i, 0, 0))),
        scratch_shapes=[pltpu.VMEM((D_MODEL, D_MODEL), bf16),
                        pltpu.VMEM((D_MODEL, ROUTE_LANES), bf16), pltpu.VMEM((D_MODEL, ROUTE_LANES), bf16)],
        compiler_params=_cparams(("arbitrary",)),
        name="outproj_router",
    )(xp, xs, attp, atts, hyp, hys, mod, w_out, norm2_g.reshape(1, D_MODEL), wr, br)


PAIRS_PER_GROUP = 6
N_CLASSES = N_GROUPS * PAIRS_PER_GROUP
PAIR_SLOT_A = (0, 0, 0, 1, 1, 3)
PAIR_SLOT_B = (1, 2, 3, 3, 2, 2)
FLAG_NEW_A, FLAG_NEW_B, FLAG_TILE_START, FLAG_TILE_END = 1, 2, 4, 8


CLASS_ROWS = 32
TAB_TILE, TAB_EA, TAB_EB, TAB_LO, TAB_HI, TAB_FLAGS, TAB_N = range(7)


def _select_by(idx, values):
    out = jnp.full(idx.shape, float(values[-1]), f32)
    for i in range(len(values) - 2, -1, -1):
        out = jnp.where(idx == i, float(values[i]), out)
    return out


def _plan_kernel(e1_ref, e2_ref, pos_ref, tab_ref, *, tm):
    e1 = e1_ref[...]
    e2 = e2_ref[...]
    rows = e1.shape[0]
    grp = jnp.floor(e1 * (1.0 / EXPERTS_PER_GROUP))
    l1 = e1 - EXPERTS_PER_GROUP * grp
    l2 = e2 - EXPERTS_PER_GROUP * jnp.floor(e2 * (1.0 / EXPERTS_PER_GROUP))
    lo, hi = jnp.minimum(l1, l2), jnp.maximum(l1, l2)
    pair = jnp.where(lo == 0, hi - 1, jnp.where(lo == 1, jnp.where(hi == 3, 3.0, 4.0), 5.0))
    cls = grp * PAIRS_PER_GROUP + pair

    ri = lax.broadcasted_iota(jnp.int32, (LANES, LANES), 0)
    ci = lax.broadcasted_iota(jnp.int32, (LANES, LANES), 1)
    upper = (ri <= ci).astype(bf16)
    rr = lax.broadcasted_iota(jnp.int32, (rows, rows), 0)
    rc = lax.broadcasted_iota(jnp.int32, (rows, rows), 1)
    strict_lower = (rc < rr).astype(bf16)
    cid = lax.broadcasted_iota(jnp.int32, (CLASS_ROWS, 1), 0)

    pos = jnp.zeros(e1.shape, f32)
    base = jnp.zeros((1, 1), f32)
    cnt_col = jnp.zeros((CLASS_ROWS, 1), f32)
    off_col = jnp.zeros((CLASS_ROWS, 1), f32)
    for c in range(N_CLASSES):
        m = cls == c
        within = _dot(m.astype(bf16), upper)
        tot = jnp.broadcast_to(within[:, LANES - 1:LANES], within.shape)
        before = _dot(strict_lower, tot.astype(bf16))[:, 0:1]
        count = jnp.sum(within[:, LANES - 1:LANES], axis=0, keepdims=True)
        pos = jnp.where(m, base + before + within - 1.0, pos)
        cnt_col = jnp.where(cid == c, count, cnt_col)
        off_col = jnp.where(cid == c, base, off_col)
        base = base + count
    pos_ref[...] = pos.astype(jnp.int32)

    end_col = off_col + cnt_col
    inv_tm = 1.0 / tm
    first_col = jnp.floor(off_col * inv_tm)
    ntl_col = jnp.where(cnt_col > 0, jnp.floor((end_col - 1.0) * inv_tm) - first_col + 1.0, 0.0)
    cr = lax.broadcasted_iota(jnp.int32, (CLASS_ROWS, CLASS_ROWS), 0)
    cc = lax.broadcasted_iota(jnp.int32, (CLASS_ROWS, CLASS_ROWS), 1)
    lower = (cc <= cr).astype(bf16)
    iend_col = _dot(lower, jnp.broadcast_to(ntl_col, (CLASS_ROWS, LANES)).astype(bf16))[:, 0:1]
    istart_col = iend_col - ntl_col
    n_items = jnp.sum(ntl_col, axis=0, keepdims=True)
    slots = lax.broadcasted_iota(jnp.int32, (1, LANES), 1).astype(f32)
    items = jnp.minimum(slots, n_items - 1.0)
    past = jnp.where((items >= iend_col) & (cid < N_CLASSES), 1.0, 0.0)
    it_cls = jnp.minimum(jnp.sum(past, axis=0, keepdims=True), N_CLASSES - 1.0)
    sel = it_cls == cid.astype(f32)
    pick = lambda col: jnp.sum(jnp.where(sel, col, 0.0), axis=0, keepdims=True)
    grp_col = jnp.floor(cid.astype(f32) * (1.0 / PAIRS_PER_GROUP))
    pair_col = cid.astype(f32) - PAIRS_PER_GROUP * grp_col
    ea_col = EXPERTS_PER_GROUP * grp_col + _select_by(pair_col, PAIR_SLOT_A)
    eb_col = EXPERTS_PER_GROUP * grp_col + _select_by(pair_col, PAIR_SLOT_B)
    it_tile = pick(first_col) + items - pick(istart_col)
    it_lo = jnp.maximum(pick(off_col) - it_tile * tm, 0.0)
    it_hi = jnp.minimum(pick(end_col) - it_tile * tm, float(tm))
    it_ea, it_eb = pick(ea_col), pick(eb_col)
    prev = lambda v: pltpu.roll(jnp.broadcast_to(v, (SUBLANES, LANES)), 1, 1)[0:1]
    changed = lambda v: jnp.where((slots == 0) | (v != prev(v)), 1.0, 0.0)
    flags = (FLAG_NEW_A * changed(it_ea) + FLAG_NEW_B * changed(it_eb)
             + FLAG_TILE_START * jnp.where(it_lo == 0, 1.0, 0.0) + FLAG_TILE_END * jnp.where(it_hi == tm, 1.0, 0.0))
    table = {TAB_TILE: it_tile, TAB_EA: it_ea, TAB_EB: it_eb, TAB_LO: it_lo, TAB_HI: it_hi, TAB_FLAGS: flags,
             TAB_N: jnp.broadcast_to(n_items, (1, LANES))}
    trow = lax.broadcasted_iota(jnp.int32, tab_ref.shape, 0)
    tab = jnp.zeros(tab_ref.shape, f32)
    for r, v in table.items():
        tab = jnp.where(trow == r, v, tab)
    tab_ref[...] = tab.astype(jnp.int32)


def _route_plan(ee, tm, max_items):
    per = ee.shape[1] // 2
    t = ee.shape[0] * per * LANES
    assert max_items <= LANES and tm & (tm - 1) == 0
    pos, tab = pl.pallas_call(
        functools.partial(_plan_kernel, tm=tm),
        out_shape=(jax.ShapeDtypeStruct((t // LANES, LANES), jnp.int32),
                   jax.ShapeDtypeStruct((SUBLANES, LANES), jnp.int32)),
        name="moe_plan",
    )(ee[:, :per].reshape(t // LANES, LANES), ee[:, per:].reshape(t // LANES, LANES))
    row = lambda r: tab[r, :max_items]
    return dict(pos=pos.reshape(t), n_items=tab[TAB_N, :1], it_tile=row(TAB_TILE), it_ea=row(TAB_EA),
                it_eb=row(TAB_EB), it_lo=row(TAB_LO), it_hi=row(TAB_HI), it_flags=row(TAB_FLAGS))


ROW_DMA_UNROLL = 8
CHUNK_ALIGN_LOG2 = 6


def _row_dma_loop(g0, g1, make_copy, priorities):
    def body(i, c):
        for u in range(ROW_DMA_UNROLL):
            make_copy(i * ROW_DMA_UNROLL + u).start(priority=priorities[u % len(priorities)])
        return c

    lax.fori_loop(g0, g1, body, 0)


def _moe_ffn_kernel(pos_ref, it_tile, it_ea, it_eb, it_lo, it_hi, it_flags, n_items, tok_hbm,
                    wga_ref, wua_ref, wda_ref, wgb_ref, wub_ref, wdb_ref, out_hbm,
                    src_s, xbuf, ybuf, x_s, rt_s, acc_s, wa_s, wb_s, sem_in, sem_out, *, chunk, n_tiles):
    i = pl.program_id(0)
    tm = x_s.shape[0]
    groups = tm // ROW_DMA_UNROLL

    def gather_tile(k):
        base = k * tm
        slot = k % 2
        _row_dma_loop(0, groups, lambda r: pltpu.make_async_copy(
            tok_hbm.at[src_s[base + r]],
            xbuf.at[slot, pl.ds(pl.multiple_of(r * TOK_PITCH, TOK_PITCH), TOK_PITCH), :],
            sem_in.at[slot]), GATHER_PRIORITIES)

    def scatter_tile(k):
        base = k * tm
        slot = k % 2
        _row_dma_loop(0, groups, lambda r: pltpu.make_async_copy(
            ybuf.at[slot, pl.ds(pl.multiple_of(r * SUBLANES, SUBLANES), SUBLANES), :],
            out_hbm.at[src_s[base + r]],
            sem_out.at[slot]), SCATTER_PRIORITIES)

    def expert_ffn(x, w, wg, wu, wd):
        g = _dot(x, wg[...])
        u = _dot(x, wu[...])
        hid = (g * (1.0 / (1.0 + jnp.exp(-g)))) * u
        return _dot((hid * w).astype(bf16), wd[...])

    def wait_all(buf, sem, slot):
        pltpu.make_async_copy(buf.at[slot], buf.at[slot], sem.at[slot]).wait()

    @pl.when(i == 0)
    def _():
        def inv(a, c):
            for u in range(ROW_DMA_UNROLL):
                src_s[pos_ref[a * ROW_DMA_UNROLL + u]] = a * ROW_DMA_UNROLL + u
            return c

        lax.fori_loop(0, pos_ref.shape[0] // ROW_DMA_UNROLL, inv, 0)
        gather_tile(0)

    @pl.when(i < n_items[0])
    def _():
        flags = it_flags[i]
        k = it_tile[i]

        @pl.when((flags & FLAG_NEW_A) != 0)
        def _():
            for dst, src in zip(wa_s, (wga_ref, wua_ref, wda_ref)):
                dst[...] = src[0].astype(bf16)

        @pl.when((flags & FLAG_NEW_B) != 0)
        def _():
            for dst, src in zip(wb_s, (wgb_ref, wub_ref, wdb_ref)):
                dst[...] = src[0].astype(bf16)

        @pl.when((flags & FLAG_TILE_START) != 0)
        def _():
            @pl.when(k + 1 < n_tiles)
            def _():
                gather_tile(k + 1)

            wait_all(xbuf, sem_in, k % 2)
            x_s[...] = _load_token_tiles(xbuf, tm, (k % 2,), TOK_PITCH).astype(bf16)
            rt_s[...] = xbuf[k % 2, pl.ds(TOK_RT_ROW, tm, stride=TOK_PITCH), :]
            acc_s[...] = jnp.zeros_like(acc_s)

        lo = it_lo[i]
        hi = it_hi[i]
        ea = it_ea[i].astype(f32)

        base = lax.shift_left(lax.shift_right_logical(lo, CHUNK_ALIGN_LOG2), CHUNK_ALIGN_LOG2)
        n_chunks = lax.shift_right_logical(hi - base + (chunk - 1), chunk.bit_length() - 1)

        def body(j, c):
            r = pl.multiple_of(jnp.minimum(base + j * chunk, tm - chunk), 1 << CHUNK_ALIGN_LOG2)
            x = x_s[pl.ds(r, chunk), :]
            rt = rt_s[pl.ds(r, chunk), :]
            first_is_a = rt[:, ROUTE_E1:ROUTE_E1 + 1] == ea
            w1 = rt[:, ROUTE_W1:ROUTE_W1 + 1]
            w2 = rt[:, ROUTE_W2:ROUTE_W2 + 1]
            y = (expert_ffn(x, jnp.where(first_is_a, w1, w2), *wa_s)
                 + expert_ffn(x, jnp.where(first_is_a, w2, w1), *wb_s))
            row = lax.broadcasted_iota(jnp.int32, (chunk, 1), 0) + r
            mine = (row >= lo) & (row < hi)
            acc_s[pl.ds(r, chunk), :] = jnp.where(mine, y, acc_s[pl.ds(r, chunk), :])
            return c

        lax.fori_loop(0, n_chunks, body, 0)

        @pl.when((flags & FLAG_TILE_END) != 0)
        def _():
            @pl.when(k >= 2)
            def _():
                wait_all(ybuf, sem_out, k % 2)

            _store_token_tiles(ybuf.at[k % 2], acc_s[...])
            scatter_tile(k)

            @pl.when(k == n_tiles - 1)
            def _():
                if n_tiles > 1:
                    wait_all(ybuf, sem_out, (n_tiles - 2) % 2)
                wait_all(ybuf, sem_out, (n_tiles - 1) % 2)


def _moe_ffn(plan, tok, w_gate, w_up, w_down, tm, max_items):
    n_tok = tok.shape[0] // TOK_PITCH
    n_tiles = n_tok // tm
    tok3 = tok.reshape(n_tok, TOK_PITCH, LANES)
    spec_a = lambda shape: pl.BlockSpec((1,) + shape, lambda i, ps, tl, ea, eb, lo, hi, fl, n: (ea[i], 0, 0))
    spec_b = lambda shape: pl.BlockSpec((1,) + shape, lambda i, ps, tl, ea, eb, lo, hi, fl, n: (eb[i], 0, 0))
    shapes = ((D_MODEL, D_EXPERT), (D_MODEL, D_EXPERT), (D_EXPERT, D_MODEL))
    wscratch = lambda: tuple(pltpu.VMEM(s, bf16) for s in shapes)
    return pl.pallas_call(
        functools.partial(_moe_ffn_kernel, chunk=256, n_tiles=n_tiles),
        out_shape=jax.ShapeDtypeStruct((n_tok, SUBLANES, LANES), f32),
        grid_spec=pltpu.PrefetchScalarGridSpec(
            num_scalar_prefetch=8,
            grid=(max_items,),
            in_specs=[pl.BlockSpec(memory_space=pl.ANY)] + [spec_a(s) for s in shapes] + [spec_b(s) for s in shapes],
            out_specs=pl.BlockSpec(memory_space=pl.ANY),
            scratch_shapes=[pltpu.SMEM((n_tok,), jnp.int32),
                            pltpu.VMEM((2, tm * TOK_PITCH, LANES), f32),
                            pltpu.VMEM((2, tm * SUBLANES, LANES), f32),
                            pltpu.VMEM((tm, D_MODEL), bf16), pltpu.VMEM((tm, ROUTE_LANES), f32),
                            pltpu.VMEM((tm, D_MODEL), f32), wscratch(), wscratch(),
                            pltpu.SemaphoreType.DMA((2,)), pltpu.SemaphoreType.DMA((2,))]),
        compiler_params=_cparams(("arbitrary",), 60 * 1024 * 1024),
        name="moe_ffn",
    )(plan['pos'], plan['it_tile'], plan['it_ea'], plan['it_eb'], plan['it_lo'], plan['it_hi'],
      plan['it_flags'], plan['n_items'], tok3, w_gate, w_up, w_down, w_gate, w_up, w_down)


def _final_kernel(moe_ref, x1_ref, mod_ref, fg_ref, yp_ref, yl_ref, *, prompt_tiles):
    i = pl.program_id(0)
    tm = x1_ref.shape[0]
    y = _rms(x1_ref[...] + mod_ref[0, GT2:GT2 + 1, :] * _load_token_tiles(moe_ref, tm), fg_ref[...])

    @pl.when(i < prompt_tiles)
    def _():
        yp_ref[...] = y

    @pl.when(i >= prompt_tiles)
    def _():
        yl_ref[...] = y


def _final(moe, x1, mod, final_g, t_prompt, t_lat, seq_s):
    tm = 512
    npt, nst = t_prompt // tm, t_lat // tm
    spb = seq_s // tm
    moe2 = moe.reshape(moe.shape[0] * SUBLANES, LANES)
    return pl.pallas_call(
        functools.partial(_final_kernel, prompt_tiles=npt),
        out_shape=(jax.ShapeDtypeStruct((t_prompt, D_MODEL), f32),
                   jax.ShapeDtypeStruct((t_lat, D_MODEL), f32)),
        grid=(npt + nst,),
        in_specs=[pl.BlockSpec((tm * SUBLANES, LANES), lambda i: (i, 0)),
                  pl.BlockSpec((tm, D_MODEL), lambda i: (i, 0)),
                  pl.BlockSpec((1, MOD_ROWS, D_MODEL),
                               lambda i: (jnp.where(i < npt, 0, 1 + jnp.maximum(i - npt, 0) // spb), 0, 0)),
                  pl.BlockSpec((1, D_MODEL), lambda i: (0, 0))],
        out_specs=(pl.BlockSpec((tm, D_MODEL), lambda i: (jnp.minimum(i, npt - 1), 0)),
                   pl.BlockSpec((tm, D_MODEL), lambda i: (jnp.maximum(i - npt, 0), 0))),
        compiler_params=_cparams(("arbitrary",)),
        name="moe_combine_final",
    )(moe2, x1, mod, final_g.reshape(1, D_MODEL))


def _moe(tok, ee, x1, mod, w_gate, w_up, w_down, final_g, t_prompt, t_lat, seq_s):
    tm = MOE_TM
    n_rows = t_prompt + t_lat
    assert n_rows % tm == 0
    max_items = n_rows // tm + N_CLASSES
    plan = _route_plan(ee, tm, max_items)
    moe = _moe_ffn(plan, tok, w_gate, w_up, w_down, tm, max_items)
    return _final(moe, x1, mod, final_g, t_prompt, t_lat, seq_s)


def kernel(x_prompt, x_sample, cache_k, cache_v, c, c_ctx, w_ada, b_ada, norm1_g, w_in, rpb, conv_w, conv_b, filt_w1, filt_b1, filt_w2, filt_b2, filt_w3, filt_freq, hyena_skip, gnorm_att, gnorm_hyena, w_out, norm2_g, router_grp_w, router_grp_b, router_exp_w, router_exp_b, w_gate, w_up, w_down, final_g):
    depth = w_ada.shape[0]
    assert depth == 1
    batch, seq, _ = x_prompt.shape
    dec_batch, dec_seq, _ = x_sample.shape
    l = 0

    def pack_router(grp, exp):
        rows = grp.shape[0]
        return jnp.concatenate([grp, jnp.zeros((rows, ROUTE_EXP_LANE0 - N_GROUPS), f32), exp,
                                jnp.zeros((rows, ROUTE_LANES - ROUTE_EXP_LANE0 - N_EXPERTS), f32)], axis=1)

    wr = pack_router(router_grp_w[l], router_exp_w[l])
    br = pack_router(router_grp_b[l][None, :], router_exp_b[l][None, :])

    lp = {
        'norm1_g': norm1_g[l], 'w_in': w_in[l], 'conv_w': conv_w[l], 'conv_b': conv_b[l],
        'filt_w1': filt_w1[l], 'filt_b1': filt_b1[l], 'filt_w2': filt_w2[l], 'filt_b2': filt_b2[l],
        'filt_w3': filt_w3[l], 'filt_freq': filt_freq[l], 'hyena_skip': hyena_skip[l],
        'gnorm_hyena': gnorm_hyena[l], 'w_out': w_out[l], 'norm2_g': norm2_g[l],
        'wr': wr, 'br': br, 'w_gate': w_gate[l], 'w_up': w_up[l], 'w_down': w_down[l],
    }

    cond8 = jnp.concatenate([c_ctx[None, :], c, jnp.zeros((SUBLANES - 1 - dec_batch, D_MODEL), f32)], axis=0)
    mod = _ada_mod(cond8, w_ada[l], b_ada[l], 1 + dec_batch).reshape(SUBLANES, N_MOD, D_MODEL)
    mod = jnp.pad(mod, ((0, 0), (0, MOD_ROWS - N_MOD), (0, 0)))
    mod_ctx, mod_lat = mod[0:1], mod[1:1 + dec_batch]

    xp = x_prompt.reshape(batch * seq, D_MODEL)
    xs = x_sample.reshape(dec_batch * dec_seq, D_MODEL)

    qp, k_ctx, v_ctx, *front_p = _inproj_conv(xp, mod_ctx, lp['norm1_g'], lp['w_in'], lp['conv_w'], lp['conv_b'],
                                              seq, f32)
    attp = _ctx_attention(qp, k_ctx, v_ctx, gnorm_att[l], seq)
    hyop = _hyena_long(*front_p, lp, seq)

    ql, kl, vl, *front_l = _inproj_conv_halo(xs, mod_lat, lp['norm1_g'], lp['w_in'], lp['conv_w'], lp['conv_b'],
                                             dec_seq, bf16)
    kc = cache_k[:, l].reshape(dec_batch * cache_k.shape[2], D_ATT).astype(bf16)
    vc = cache_v[:, l].reshape(dec_batch * cache_v.shape[2], D_ATT).astype(bf16)
    attl = _na_attention(ql, kl, vl, kc, vc, rpb[l], gnorm_att[l], dec_batch, dec_seq)
    hyol = _hyena_long(*front_l, lp, dec_seq)

    x1, tok, ee = _outproj(xp, xs, attp, attl, hyop, hyol, mod[0:1 + dec_batch], lp['w_out'], lp['norm2_g'],
                           lp['wr'], lp['br'], seq, dec_seq)
    yp, ys = _moe(tok, ee, x1, mod[0:1 + dec_batch], lp['w_gate'], lp['w_up'], lp['w_down'], final_g,
                  batch * seq, dec_batch * dec_seq, dec_seq)

    y_prompt = yp.reshape(batch, seq, D_MODEL)
    y_sample = ys.reshape(dec_batch, dec_seq, D_MODEL)
    new_k = k_ctx.reshape(batch, 1, seq, H_ATT, HEAD_DIM)
    new_v = v_ctx.reshape(batch, 1, seq, H_ATT, HEAD_DIM)
    return (y_prompt, y_sample, new_k, new_v)
```
